```python
import jax, jax.numpy as jnp
from jax import lax
import numpy as np

D_MODEL = 1024
BATCH = 1
SEQ = 16384
DEPTH = 1
DEC_BATCH = 8
DEC_SEQ = 16
PAST_LEN = 1024

CHUNK = 64
D_CONV = 512
CONV_WIDTH = 31
D_SG = 512
SG_HEADS = 4
SG_HEAD_DIM = D_SG // SG_HEADS
SG_CHUNK = 128
D_MIX = D_CONV + D_SG
N_MEM = 256
X_HEADS = 4
X_HEAD_DIM = D_MODEL // X_HEADS
N_GROUPS = 4
EXPERTS_PER_GROUP = 8
N_EXPERTS = N_GROUPS * EXPERTS_PER_GROUP
TOP_K = 2
D_EXPERT = 512
MOE_BLOCK = 128
EPS = 1e-6

kernel_name = "hybrid_conv_sgmlp_memxattn_hmoe_stream_step"


def rms_norm(x, g):
    xf = x.astype(jnp.float32)
    y = xf * lax.rsqrt(jnp.mean(xf * xf, axis=-1, keepdims=True) + EPS)
    return (y * g.astype(jnp.float32)).astype(x.dtype)


def layer_norm(x, g, b):
    xf = x.astype(jnp.float32)
    mu = jnp.mean(xf, axis=-1, keepdims=True)
    var = jnp.mean(jnp.square(xf - mu), axis=-1, keepdims=True)
    y = (xf - mu) * lax.rsqrt(var + EPS) * g.astype(jnp.float32) + b.astype(jnp.float32)
    return y.astype(x.dtype)


def conformer_conv(a_in, a_gate, hist, conv_w, conv_b, ln_g, ln_b):
    a = a_in * jax.nn.sigmoid(a_gate)
    ext = jnp.concatenate([hist.astype(a.dtype), a], axis=1)
    y = lax.conv_general_dilated(
        ext, conv_w[:, None, :].astype(ext.dtype), window_strides=(1,), padding='VALID',
        dimension_numbers=('NWC', 'WIO', 'NWC'), feature_group_count=D_CONV)
    y = jax.nn.silu(layer_norm(y + conv_b.astype(y.dtype), ln_g, ln_b))
    return y, ext[:, -(CONV_WIDTH - 1):]


def spatial_gate(u, v, w_s, b_s):
    B, T, _ = v.shape
    L = min(T, SG_CHUNK)
    mask = jnp.tril(jnp.ones((L, L), dtype=bool))
    w = jnp.where(mask, w_s[:, :L, :L], 0).astype(v.dtype)
    vc = v.reshape(B, T // L, L, SG_HEADS, SG_HEAD_DIM)
    s = jnp.einsum('hij,bcjhd->bcihd', w, vc) + b_s[:, :L].T[:, :, None].astype(v.dtype)
    return u * s.reshape(B, T, D_SG)


def memory_kv(mem, g_mem, w_mk, w_mv):
    B = mem.shape[0]
    m = rms_norm(mem, g_mem)
    k = (m @ w_mk).reshape(B, N_MEM, X_HEADS, X_HEAD_DIM)
    v = (m @ w_mv).reshape(B, N_MEM, X_HEADS, X_HEAD_DIM)
    return k, v


def memory_attend(h, mem_k, mem_v, w_xq, w_xo):
    B, T, _ = h.shape
    q = (h @ w_xq).reshape(B, T, X_HEADS, X_HEAD_DIM)
    s = jnp.einsum('bqhd,bkhd->bhqk', q, mem_k.astype(q.dtype)).astype(jnp.float32) * (X_HEAD_DIM ** -0.5)
    p = jax.nn.softmax(s, axis=-1).astype(q.dtype)
    o = jnp.einsum('bhqk,bkhd->bqhd', p, mem_v.astype(q.dtype)).reshape(B, T, D_MODEL)
    return o @ w_xo


def routed_experts(t, expert, w_eg, w_eu, w_ed):
    n, D = t.shape
    a = n * TOP_K
    flat = expert.reshape(-1)
    order = jnp.argsort(flat)
    sorted_e = flat[order]
    counts = jnp.bincount(flat, length=N_EXPERTS)
    padded = (counts + MOE_BLOCK - 1) // MOE_BLOCK * MOE_BLOCK
    pad_end = jnp.cumsum(padded)
    pad_start = pad_end - padded
    start = jnp.cumsum(counts) - counts
    slot = pad_start[sorted_e] + jnp.arange(a, dtype=jnp.int32) - start[sorted_e]
    n_blocks = -(-a // MOE_BLOCK) + N_EXPERTS
    token_of_slot = jnp.full((n_blocks * MOE_BLOCK,), n, jnp.int32).at[slot].set((order // TOP_K).astype(jnp.int32))
    block_expert = jnp.minimum(
        jnp.searchsorted(pad_end, jnp.arange(n_blocks, dtype=jnp.int32) * MOE_BLOCK, side='right'), N_EXPERTS - 1)
    xs = jnp.concatenate([t, jnp.zeros((1, D), t.dtype)], axis=0)[token_of_slot].reshape(n_blocks, MOE_BLOCK, D)

    def expert_block(args):
        xb, e = args
        return (jax.nn.silu(xb @ w_eg[e]) * (xb @ w_eu[e])) @ w_ed[e]

    ys = lax.map(expert_block, (xs, block_expert)).reshape(-1, D)
    slot_of_assign = jnp.zeros((a,), jnp.int32).at[order].set(slot)
    return ys[slot_of_assign].reshape(n, TOP_K, D)


def hier_moe(h, w_rg, b_rg, w_re, b_re, w_eg, w_eu, w_ed):
    B, T, D = h.shape
    t = h.reshape(-1, D)
    lg = (t @ w_rg).astype(jnp.float32) + b_rg.astype(jnp.float32)
    p_g = jax.nn.softmax(lg, axis=-1)
    g_idx = jnp.argmax(lg, axis=-1)
    g_w = jnp.take_along_axis(p_g, g_idx[:, None], axis=1)
    le = jnp.einsum('nd,gde->nge', t, w_re).astype(jnp.float32) + b_re.astype(jnp.float32)
    le = jnp.take_along_axis(le, g_idx[:, None, None], axis=1)[:, 0]
    top_v, top_i = lax.top_k(le, TOP_K)
    gate = g_w * jax.nn.softmax(top_v, axis=-1)
    expert = (g_idx[:, None] * EXPERTS_PER_GROUP + top_i).astype(jnp.int32)
    y = routed_experts(t, expert, w_eg, w_eu, w_ed)
    return jnp.einsum('nk,nkd->nd', gate.astype(t.dtype), y).reshape(B, T, D)


def trunk_layer(x, conv_hist, mem_k, mem_v, g_mix, w_in, conv_w, conv_b, ln_conv_g, ln_conv_b, ln_v_g, ln_v_b,
                w_sg, b_sg, w_out, g_xattn, w_xq, w_xo, g_ffn, w_rg, b_rg, w_re, b_re, w_eg, w_eu, w_ed):
    h = rms_norm(x, g_mix)
    z = h @ w_in
    a_in, a_gate, u, v = jnp.split(z, [D_CONV, 2 * D_CONV, 2 * D_CONV + D_SG], axis=-1)
    a_out, new_hist = conformer_conv(a_in, a_gate, conv_hist, conv_w, conv_b, ln_conv_g, ln_conv_b)
    v = layer_norm(v, ln_v_g, ln_v_b)
    b_out = spatial_gate(u, v, w_sg, b_sg)
    x = x + jnp.concatenate([a_out, b_out], axis=-1) @ w_out
    x = x + memory_attend(rms_norm(x, g_xattn), mem_k, mem_v, w_xq, w_xo)
    x = x + hier_moe(rms_norm(x, g_ffn), w_rg, b_rg, w_re, b_re, w_eg, w_eu, w_ed)
    return x, new_hist, v


def setup_inputs(seed: int = 0) -> dict:
    key = jax.random.key(seed)
    ks = iter(jax.random.split(key, 40))

    def nrm(shape, scale):
        return jax.random.normal(next(ks), shape, jnp.float32) * scale

    def gain(shape):
        return 1.0 + nrm(shape, 0.02)

    return {
        "x_prompt": nrm((BATCH, SEQ, D_MODEL), 1.0),
        "x_sample": nrm((DEC_BATCH, DEC_SEQ, D_MODEL), 1.0),
        "mem_prompt": nrm((BATCH, N_MEM, D_MODEL), 1.0),
        "state_conv": nrm((DEPTH, DEC_BATCH, CONV_WIDTH - 1, D_CONV), 0.5),
        "cache_mem_k": nrm((DEPTH, DEC_BATCH, N_MEM, X_HEADS, X_HEAD_DIM), 1.0),
        "cache_mem_v": nrm((DEPTH, DEC_BATCH, N_MEM, X_HEADS, X_HEAD_DIM), 1.0),
        "g_mix": gain((DEPTH, D_MODEL)),
        "w_in": nrm((DEPTH, D_MODEL, 2 * D_CONV + 2 * D_SG), D_MODEL ** -0.5),
        "conv_w": nrm((DEPTH, CONV_WIDTH, D_CONV), CONV_WIDTH ** -0.5),
        "conv_b": nrm((DEPTH, D_CONV), 0.02),
        "ln_conv_g": gain((DEPTH, D_CONV)),
        "ln_conv_b": nrm((DEPTH, D_CONV), 0.02),
        "ln_v_g": gain((DEPTH, D_SG)),
        "ln_v_b": nrm((DEPTH, D_SG), 0.02),
        "w_sg": nrm((DEPTH, SG_HEADS, SG_CHUNK, SG_CHUNK), SG_CHUNK ** -0.5),
        "b_sg": 1.0 + nrm((DEPTH, SG_HEADS, SG_CHUNK), 0.1),
        "w_out": nrm((DEPTH, D_MIX, D_MODEL), D_MIX ** -0.5),
        "g_mem": gain((DEPTH, D_MODEL)),
        "w_mk": nrm((DEPTH, D_MODEL, D_MODEL), D_MODEL ** -0.5),
        "w_mv": nrm((DEPTH, D_MODEL, D_MODEL), D_MODEL ** -0.5),
        "g_xattn": gain((DEPTH, D_MODEL)),
        "w_xq": nrm((DEPTH, D_MODEL, D_MODEL), D_MODEL ** -0.5),
        "w_xo": nrm((DEPTH, D_MODEL, D_MODEL), D_MODEL ** -0.5),
        "g_ffn": gain((DEPTH, D_MODEL)),
        "w_router_group": nrm((DEPTH, D_MODEL, N_GROUPS), D_MODEL ** -0.5),
        "b_router_group": nrm((DEPTH, N_GROUPS), 0.01),
        "w_router_expert": nrm((DEPTH, N_GROUPS, D_MODEL, EXPERTS_PER_GROUP), D_MODEL ** -0.5),
        "b_router_expert": nrm((DEPTH, N_GROUPS, EXPERTS_PER_GROUP), 0.01),
        "w_expert_gate": nrm((DEPTH, N_EXPERTS, D_MODEL, D_EXPERT), D_MODEL ** -0.5),
        "w_expert_up": nrm((DEPTH, N_EXPERTS, D_MODEL, D_EXPERT), D_MODEL ** -0.5),
        "w_expert_down": nrm((DEPTH, N_EXPERTS, D_EXPERT, D_MODEL), D_EXPERT ** -0.5),
        "g_final": gain((D_MODEL,)),
    }


def reference(x_prompt, x_sample, mem_prompt, state_conv, cache_mem_k, cache_mem_v, g_mix, w_in, conv_w, conv_b,
              ln_conv_g, ln_conv_b, ln_v_g, ln_v_b, w_sg, b_sg, w_out, g_mem, w_mk, w_mv, g_xattn, w_xq, w_xo,
              g_ffn, w_router_group, b_router_group, w_router_expert, b_router_expert, w_expert_gate,
              w_expert_up, w_expert_down, g_final):
    xp = x_prompt
    xs = x_sample
    conv_p, conv_s, mk_p, mv_p, sgv_s = [], [], [], [], []
    for i in range(DEPTH):
        lw = (g_mix[i], w_in[i], conv_w[i], conv_b[i], ln_conv_g[i], ln_conv_b[i], ln_v_g[i], ln_v_b[i],
              w_sg[i], b_sg[i], w_out[i], g_xattn[i], w_xq[i], w_xo[i], g_ffn[i], w_router_group[i],
              b_router_group[i], w_router_expert[i], b_router_expert[i], w_expert_gate[i], w_expert_up[i],
              w_expert_down[i])
        k_p, v_p = memory_kv(mem_prompt, g_mem[i], w_mk[i], w_mv[i])
        hist0 = jnp.zeros((xp.shape[0], CONV_WIDTH - 1, D_CONV), xp.dtype)
        xp, hist_p, _ = trunk_layer(xp, hist0, k_p, v_p, *lw)
        xs, hist_s, v_s = trunk_layer(xs, state_conv[i], cache_mem_k[i], cache_mem_v[i], *lw)
        conv_p.append(hist_p)
        conv_s.append(hist_s)
        mk_p.append(k_p)
        mv_p.append(v_p)
        sgv_s.append(v_s)
    y_prompt = rms_norm(xp, g_final)
    y_sample = rms_norm(xs, g_final)
    new_state_conv_prompt = jnp.stack(conv_p)
    new_state_conv_sample = jnp.stack(conv_s)
    new_cache_mem_k_prompt = jnp.stack(mk_p)
    new_cache_mem_v_prompt = jnp.stack(mv_p)
    new_state_sg_v_sample = jnp.stack(sgv_s)
    return (y_prompt, y_sample, new_state_conv_prompt, new_state_conv_sample, new_cache_mem_k_prompt,
            new_cache_mem_v_prompt, new_state_sg_v_sample)
```

```python
import functools

import jax
import jax.numpy as jnp
from jax import lax
from jax.experimental import pallas as pl
from jax.experimental.pallas import tpu as pltpu
from jax.experimental.pallas import tpu_sc as plsc

D_MODEL = 1024
D_CONV = 512
D_SG = 512
CONV_WIDTH = 31
HIST = CONV_WIDTH - 1
SG_HEADS = 4
SG_HEAD_DIM = 128
SG_CHUNK = 128
N_MEM = 256
X_HEADS = 4
X_HEAD_DIM = 256
N_GROUPS = 4
EXPERTS_PER_GROUP = 8
N_EXPERTS = 32
D_EXPERT = 512
EPS = 1e-6

LANES = 128
SUBLANES = 8
SC_CORES = 2
SC_SUBCORES = 16
SC_WORKERS = SC_CORES * SC_SUBCORES
VMEM_LIMIT = 56 * 1024 * 1024

TM = 512
HALO = 32
CONV_ROWS = 128
BM = 256
LOGIT_LANES = 128

F32 = jnp.float32
BF16 = jnp.bfloat16


def _dot(a, b):
    return jnp.dot(a, b, preferred_element_type=F32)


def _rms(x, g):
    return x * lax.rsqrt(jnp.mean(x * x, axis=-1, keepdims=True) + EPS) * g


def _ln(x, g, b):
    mu = jnp.mean(x, axis=-1, keepdims=True)
    xc = x - mu
    var = jnp.mean(xc * xc, axis=-1, keepdims=True)
    return xc * lax.rsqrt(var + EPS) * g + b


def _sigmoid(x):
    return 1.0 / (1.0 + jnp.exp(-x))


def _pack_bf16_pairs(h):
    bits = lax.bitcast_convert_type(h, jnp.uint32)
    half = h.shape[1] // 2
    lo = lax.shift_right_logical(bits[:, :half], jnp.uint32(16))
    hi = bits[:, half:] & jnp.uint32(0xFFFF0000)
    return hi | lo


def _unpack_bf16_pairs(p):
    lo = lax.bitcast_convert_type(lax.shift_left(p, jnp.uint32(16)), F32)
    hi = lax.bitcast_convert_type(p & jnp.uint32(0xFFFF0000), F32)
    return lo.astype(BF16), hi.astype(BF16)


def _memkv_kernel(mem_ref, g_ref, wk_ref, wv_ref, k_ref, v_ref):
    m = _rms(mem_ref[...], g_ref[...]).astype(BF16)
    k_ref[...] = _dot(m, wk_ref[...])
    v_ref[...] = _dot(m, wv_ref[...])


def _memkv(mem, g_mem, w_mk, w_mv):
    return pl.pallas_call(
        _memkv_kernel,
        out_shape=(jax.ShapeDtypeStruct((N_MEM, D_MODEL), F32), jax.ShapeDtypeStruct((N_MEM, D_MODEL), F32)),
        compiler_params=pltpu.CompilerParams(vmem_limit_bytes=VMEM_LIMIT),
        name="memkv",
    )(mem, g_mem, w_mk, w_mv)


def _attn_heads(q, kt, v):
    outs = []
    for h in range(X_HEADS):
        sl = slice(h * X_HEAD_DIM, (h + 1) * X_HEAD_DIM)
        s = _dot(q[:, sl], kt[sl, :]) * (X_HEAD_DIM ** -0.5)
        s = s - jnp.max(s, axis=-1, keepdims=True)
        p = jnp.exp(s)
        p = p / jnp.sum(p, axis=-1, keepdims=True)
        outs.append(_dot(p.astype(BF16), v[:, sl]).astype(BF16))
    return jnp.concatenate(outs, axis=1)


def _route(h3, wr_ref, br_ref, run):
    m = h3.shape[0]
    logits = _dot(h3, wr_ref[...]) + br_ref[...]
    lane = lax.broadcasted_iota(jnp.int32, (m, LOGIT_LANES), 1).astype(F32)
    neg = jnp.float32(-jnp.inf)
    big = jnp.float32(LOGIT_LANES)

    def first_argmax(vals):
        mx = jnp.max(vals, axis=-1, keepdims=True)
        idx = jnp.min(jnp.where(vals == mx, lane, big), axis=-1, keepdims=True)
        return mx, idx

    lg = jnp.where(lane < N_GROUPS, logits, neg)
    g_max, g_idx = first_argmax(lg)
    g_w = 1.0 / jnp.sum(jnp.exp(lg - g_max), axis=-1, keepdims=True)

    lo = N_GROUPS + g_idx * EXPERTS_PER_GROUP
    le = jnp.where((lane >= lo) & (lane < lo + EXPERTS_PER_GROUP), logits, neg)
    v1, i1 = first_argmax(le)
    v2, i2 = first_argmax(jnp.where(lane == i1, neg, le))
    t = jnp.exp(v2 - v1)
    gate1 = g_w / (1.0 + t)
    gate2 = g_w * t / (1.0 + t)
    e1 = i1 - N_GROUPS
    e2 = i2 - N_GROUPS

    oh1 = (lane == e1).astype(F32)
    oh2 = (lane == e2).astype(F32)
    oh = oh1 + oh2
    row = lax.broadcasted_iota(jnp.int32, (m, m), 0)
    col = lax.broadcasted_iota(jnp.int32, (m, m), 1)
    strict_lower = (col < row).astype(BF16)
    before = _dot(strict_lower, oh.astype(BF16)) + run
    rank1 = jnp.sum(before * oh1, axis=-1, keepdims=True)
    rank2 = jnp.sum(before * oh2, axis=-1, keepdims=True)
    new_run = run + jnp.sum(oh, axis=0, keepdims=True)

    rinfo = jnp.where(lane == 0, e1,
            jnp.where(lane == 1, e2,
            jnp.where(lane == 2, gate1,
            jnp.where(lane == 3, gate2,
            jnp.where(lane == 4, rank1,
            jnp.where(lane == 5, rank2, 0.0))))))
    return rinfo, new_run


def _conv_taps(win, w_ref, l0, rows):
    n = win.shape[0]
    acc = jnp.zeros((rows, LANES), F32)
    for r in range(SUBLANES):
        d = HALO - HIST + r
        if d % SUBLANES == 0:
            e, base = win, d
        else:
            e, base = pltpu.roll(win, n - d, axis=0), 0
        for qq in range(-(-CONV_WIDTH // SUBLANES)):
            k = SUBLANES * qq + r
            if k < CONV_WIDTH:
                s0 = base + SUBLANES * qq
                acc = acc + e[s0:s0 + rows, :] * w_ref[pl.ds(k, 1), pl.ds(l0, LANES)]
    return acc


def _trunk_prompt_kernel(x_ref, gmix_ref, win_ref, convw_ref, convb_ref, lncg_ref, lncb_ref, lnvg_ref, lnvb_ref,
                         wsg_ref, bsg_ref, wout_ref, gx_ref, wxq_ref, kt_ref, v_ref, wxo_ref, gffn_ref, wr_ref,
                         br_ref,
                         x2_ref, h3_ref, rinfo_ref, hist_ref, cnt_ref,
                         ext_ref, conv_ref, run_ref):
    i = pl.program_id(0)

    @pl.when(i == 0)
    def _():
        ext_ref[0:HALO, :] = jnp.zeros((HALO, D_CONV), F32)
        run_ref[...] = jnp.zeros((1, LOGIT_LANES), F32)

    x = x_ref[...]
    h = _rms(x, gmix_ref[...]).astype(BF16)

    a_in = _dot(h, win_ref[:, 0:D_CONV])
    a_gate = _dot(h, win_ref[:, D_CONV:2 * D_CONV])
    ext_ref[HALO:HALO + TM, :] = a_in * _sigmoid(a_gate)

    def conv_body(c, carry):
        r0 = pl.multiple_of((c // 4) * CONV_ROWS, CONV_ROWS)
        l0 = pl.multiple_of((c % 4) * LANES, LANES)
        win = ext_ref[pl.ds(r0, CONV_ROWS + HALO), pl.ds(l0, LANES)]
        conv_ref[pl.ds(r0, CONV_ROWS), pl.ds(l0, LANES)] = _conv_taps(win, convw_ref, l0, CONV_ROWS)
        return carry

    lax.fori_loop(0, (TM // CONV_ROWS) * (D_CONV // LANES), conv_body, 0)
    hist_ref[...] = ext_ref[TM:TM + HALO, :]
    ext_ref[0:HALO, :] = ext_ref[TM:TM + HALO, :]

    y = _ln(conv_ref[...] + convb_ref[...], lncg_ref[...], lncb_ref[...])
    a_out = (y * _sigmoid(y)).astype(BF16)

    u = _dot(h, win_ref[:, 2 * D_CONV:2 * D_CONV + D_SG])
    v = _ln(_dot(h, win_ref[:, 2 * D_CONV + D_SG:]), lnvg_ref[...], lnvb_ref[...]).astype(BF16)
    ri = lax.broadcasted_iota(jnp.int32, (SG_CHUNK, SG_CHUNK), 0)
    ci = lax.broadcasted_iota(jnp.int32, (SG_CHUNK, SG_CHUNK), 1)
    w_tril = [jnp.where(ci <= ri, wsg_ref[hh], 0.0).astype(BF16) for hh in range(SG_HEADS)]
    gate_rows = []
    for c in range(TM // SG_CHUNK):
        rs = slice(c * SG_CHUNK, (c + 1) * SG_CHUNK)
        heads = [_dot(w_tril[hh], v[rs, hh * SG_HEAD_DIM:(hh + 1) * SG_HEAD_DIM]) for hh in range(SG_HEADS)]
        gate_rows.append(jnp.concatenate(heads, axis=1) + bsg_ref[...])
    b_out = (u * jnp.concatenate(gate_rows, axis=0)).astype(BF16)

    x1 = x + _dot(a_out, wout_ref[0:D_CONV, :]) + _dot(b_out, wout_ref[D_CONV:, :])

    hx = _rms(x1, gx_ref[...]).astype(BF16)
    q = _dot(hx, wxq_ref[...]).astype(BF16)
    x2 = x1 + _dot(_attn_heads(q, kt_ref[...], v_ref[...]), wxo_ref[...])
    x2_ref[...] = x2

    h3 = _rms(x2, gffn_ref[...]).astype(BF16)
    h3_ref[...] = _pack_bf16_pairs(h3.astype(F32))
    rinfo, new_run = _route(h3, wr_ref, br_ref, run_ref[...])
    rinfo_ref[...] = rinfo
    run_ref[...] = new_run
    cnt_ref[...] = new_run


def _const_spec(shape):
    nd = len(shape)
    return pl.BlockSpec(shape, lambda i: (0,) * nd, pipeline_mode=pl.Buffered(1))


def _trunk_prompt(x, p):
    n = x.shape[0]
    assert n % TM == 0
    row = lambda w: pl.BlockSpec((TM, w), lambda i: (i, 0))
    consts = [p["g_mix"], p["w_in"], p["conv_w"], p["conv_b"], p["ln_conv_g"], p["ln_conv_b"], p["ln_v_g"],
              p["ln_v_b"], p["w_sg"], p["b_sg_rows"], p["w_out"], p["g_xattn"], p["w_xq"], p["kt"], p["v"],
              p["w_xo"], p["g_ffn"], p["w_router"], p["b_router"]]
    return pl.pallas_call(
        _trunk_prompt_kernel,
        grid=(n // TM,),
        in_specs=[row(D_MODEL)] + [_const_spec(c.shape) for c in consts],
        out_specs=(row(D_MODEL), row(D_MODEL // 2), row(LOGIT_LANES),
                   pl.BlockSpec((HALO, D_CONV), lambda i: (0, 0)),
                   pl.BlockSpec((1, LOGIT_LANES), lambda i: (0, 0))),
        out_shape=(jax.ShapeDtypeStruct((n, D_MODEL), F32),
                   jax.ShapeDtypeStruct((n, D_MODEL // 2), jnp.uint32),
                   jax.ShapeDtypeStruct((n, LOGIT_LANES), F32),
                   jax.ShapeDtypeStruct((HALO, D_CONV), F32),
                   jax.ShapeDtypeStruct((1, LOGIT_LANES), F32)),
        scratch_shapes=[pltpu.VMEM((TM + HALO, D_CONV), F32),
                        pltpu.VMEM((TM, D_CONV), F32),
                        pltpu.VMEM((1, LOGIT_LANES), F32)],
        compiler_params=pltpu.CompilerParams(dimension_semantics=("arbitrary",), vmem_limit_bytes=VMEM_LIMIT),
        name="trunk_prompt",
    )(x, *consts)


def _trunk_sample_kernel(n_batch, t_len,
                         x_ref, hist_in_ref, run_in_ref, gmix_ref, win_ref, convw_ref, convb_ref, lncg_ref, lncb_ref,
                         lnvg_ref, lnvb_ref, wsgbd_ref, bsg_ref, wout_ref, gx_ref, wxq_ref, kt_ref, v_ref, wxo_ref,
                         gffn_ref, wr_ref, br_ref,
                         x2_ref, h3_ref, rinfo_ref, hist_ref, sgv_ref, cnt_ref,
                         ext_ref, conv_ref, att_ref):
    x = x_ref[...]
    h = _rms(x, gmix_ref[...]).astype(BF16)
    z = _dot(h, win_ref[...])
    a = z[:, 0:D_CONV] * _sigmoid(z[:, D_CONV:2 * D_CONV])
    ext_len = HIST + t_len
    for b in range(n_batch):
        ext_ref[b, 0:HIST, :] = hist_in_ref[b]
        ext_ref[b, HIST:ext_len, :] = a[b * t_len:(b + 1) * t_len, :]
    for b in range(n_batch):
        acc = jnp.zeros((t_len, D_CONV), F32)
        for k in range(CONV_WIDTH):
            acc = acc + ext_ref[b, k:k + t_len, :] * convw_ref[k:k + 1, :]
        conv_ref[b * t_len:(b + 1) * t_len, :] = acc
        hist_ref[b] = ext_ref[b, ext_len - HIST:ext_len, :]

    y = _ln(conv_ref[...] + convb_ref[...], lncg_ref[...], lncb_ref[...])
    a_out = (y * _sigmoid(y)).astype(BF16)

    u = z[:, 2 * D_CONV:2 * D_CONV + D_SG]
    v = _ln(z[:, 2 * D_CONV + D_SG:], lnvg_ref[...], lnvb_ref[...])
    sgv_ref[...] = v
    vb = v.astype(BF16)
    heads = [_dot(wsgbd_ref[hh], vb[:, hh * SG_HEAD_DIM:(hh + 1) * SG_HEAD_DIM]) for hh in range(SG_HEADS)]
    b_out = (u * (jnp.concatenate(heads, axis=1) + bsg_ref[...])).astype(BF16)

    x1 = x + _dot(a_out, wout_ref[0:D_CONV, :]) + _dot(b_out, wout_ref[D_CONV:, :])

    hx = _rms(x1, gx_ref[...]).astype(BF16)
    q = _dot(hx, wxq_ref[...]).astype(BF16)
    for b in range(n_batch):
        rs = slice(b * t_len, (b + 1) * t_len)
        att_ref[rs, :] = _attn_heads(q[rs, :], kt_ref[b], v_ref[b])
    x2 = x1 + _dot(att_ref[...], wxo_ref[...])
    x2_ref[...] = x2

    h3 = _rms(x2, gffn_ref[...]).astype(BF16)
    h3_ref[...] = _pack_bf16_pairs(h3.astype(F32))
    rinfo, new_run = _route(h3, wr_ref, br_ref, run_in_ref[...])
    rinfo_ref[...] = rinfo
    cnt_ref[...] = new_run


def _trunk_sample(x, hist, run, p, n_batch, t_len):
    m = n_batch * t_len
    args = [x, hist, run, p["g_mix"], p["w_in"], p["conv_w"], p["conv_b"], p["ln_conv_g"], p["ln_conv_b"],
            p["ln_v_g"], p["ln_v_b"], p["w_sg_bd"], p["b_sg_rows_s"], p["w_out"], p["g_xattn"], p["w_xq"],
            p["kt_s"], p["v_s"], p["w_xo"], p["g_ffn"], p["w_router"], p["b_router"]]
    return pl.pallas_call(
        functools.partial(_trunk_sample_kernel, n_batch, t_len),
        out_shape=(jax.ShapeDtypeStruct((m, D_MODEL), F32),
                   jax.ShapeDtypeStruct((m, D_MODEL // 2), jnp.uint32),
                   jax.ShapeDtypeStruct((m, LOGIT_LANES), F32),
                   jax.ShapeDtypeStruct((n_batch, HIST, D_CONV), F32),
                   jax.ShapeDtypeStruct((m, D_SG), F32),
                   jax.ShapeDtypeStruct((1, LOGIT_LANES), F32)),
        scratch_shapes=[pltpu.VMEM((n_batch, HIST + t_len, D_CONV), F32),
                        pltpu.VMEM((m, D_CONV), F32),
                        pltpu.VMEM((m, D_MODEL), BF16)],
        compiler_params=pltpu.CompilerParams(vmem_limit_bytes=VMEM_LIMIT),
        name="trunk_sample",
    )(*args)


def _sc_gather_rows(table, idx, max_chunk):
    n_rows, d = idx.shape[0], table.shape[1]
    per_w = n_rows // SC_WORKERS
    assert per_w * SC_WORKERS == n_rows and per_w % SUBLANES == 0 and max_chunk <= LANES
    chunk = max(c for c in range(SUBLANES, max_chunk + 1, SUBLANES) if per_w % c == 0)
    n_chunks = per_w // chunk
    mesh = plsc.VectorSubcoreMesh(core_axis_name="c", subcore_axis_name="s")

    @functools.partial(
        pl.kernel, mesh=mesh,
        out_type=jax.ShapeDtypeStruct((n_rows, d), table.dtype),
        scratch_types=[pltpu.VMEM((chunk,), jnp.int32), pltpu.VMEM((chunk, d), table.dtype),
                       pltpu.SemaphoreType.DMA],
    )
    def gather(table_hbm, idx_hbm, out_hbm, idx_v, rows_v, sem):
        wid = lax.axis_index("s") * SC_CORES + lax.axis_index("c")
        base = wid * per_w

        @pl.loop(0, n_chunks)
        def _(j):
            off = base + j * chunk
            pltpu.sync_copy(idx_hbm.at[pl.ds(off, chunk)], idx_v)
            pltpu.async_copy(table_hbm.at[idx_v], rows_v, sem).wait()
            pltpu.sync_copy(rows_v, out_hbm.at[pl.ds(off, chunk)])

    return gather(table, idx)


def _experts_kernel(be_ref, nv_ref, xs_ref, wg_ref, wu_ref, wd_ref, ys_ref, wg_bf, wu_bf, wd_bf):
    b = pl.program_id(0)

    @pl.when(b < nv_ref[0])
    def _():
        prev = be_ref[jnp.maximum(b - 1, 0)]

        @pl.when((b == 0) | (be_ref[b] != prev))
        def _():
            wg_bf[...] = wg_ref[0].astype(BF16)
            wu_bf[...] = wu_ref[0].astype(BF16)
            wd_bf[...] = wd_ref[0].astype(BF16)

        half = D_MODEL // 2
        lo, hi = _unpack_bf16_pairs(xs_ref[...])
        g = _dot(lo, wg_bf[0:half, :]) + _dot(hi, wg_bf[half:, :])
        u = _dot(lo, wu_bf[0:half, :]) + _dot(hi, wu_bf[half:, :])
        hm = (g * _sigmoid(g) * u).astype(BF16)
        ys_ref[...] = _dot(hm, wd_bf[...])


def _experts(xs, block_expert, n_valid, w_eg, w_eu, w_ed):
    n_blocks = xs.shape[0] // BM
    row_map = lambda b, be, nv: (jnp.minimum(b, nv[0] - 1), 0)
    w_map = lambda b, be, nv: (be[b], 0, 0)
    return pl.pallas_call(
        _experts_kernel,
        grid_spec=pltpu.PrefetchScalarGridSpec(
            num_scalar_prefetch=2,
            grid=(n_blocks,),
            in_specs=[pl.BlockSpec((BM, D_MODEL // 2), row_map),
                      pl.BlockSpec((1, D_MODEL, D_EXPERT), w_map),
                      pl.BlockSpec((1, D_MODEL, D_EXPERT), w_map),
                      pl.BlockSpec((1, D_EXPERT, D_MODEL), w_map)],
            out_specs=pl.BlockSpec((BM, D_MODEL), row_map),
            scratch_shapes=[pltpu.VMEM((D_MODEL, D_EXPERT), BF16), pltpu.VMEM((D_MODEL, D_EXPERT), BF16),
                            pltpu.VMEM((D_EXPERT, D_MODEL), BF16)]),
        out_shape=jax.ShapeDtypeStruct((xs.shape[0], D_MODEL), F32),
        compiler_params=pltpu.CompilerParams(dimension_semantics=("arbitrary",), vmem_limit_bytes=VMEM_LIMIT),
        name="experts",
    )(block_expert, n_valid, xs, w_eg, w_eu, w_ed)


def _combine_kernel(x2_ref, y1_ref, y2_ref, rinfo_ref, g_ref, o_ref):
    r = rinfo_ref[...]
    x = x2_ref[...] + r[:, 2:3] * y1_ref[...] + r[:, 3:4] * y2_ref[...]
    o_ref[...] = _rms(x, g_ref[...])


def _combine(x2, yg, rinfo, g_final, tm, blk1, blk2):
    n = x2.shape[0]
    return pl.pallas_call(
        _combine_kernel,
        grid=(n // tm,),
        in_specs=[pl.BlockSpec((tm, D_MODEL), lambda i: (i, 0)),
                  pl.BlockSpec((tm, D_MODEL), lambda i: (blk1 + i, 0)),
                  pl.BlockSpec((tm, D_MODEL), lambda i: (blk2 + i, 0)),
                  pl.BlockSpec((tm, LOGIT_LANES), lambda i: (i, 0)),
                  pl.BlockSpec((1, D_MODEL), lambda i: (0, 0))],
        out_specs=pl.BlockSpec((tm, D_MODEL), lambda i: (i, 0)),
        out_shape=jax.ShapeDtypeStruct((n, D_MODEL), F32),
        compiler_params=pltpu.CompilerParams(dimension_semantics=("arbitrary",), vmem_limit_bytes=VMEM_LIMIT),
        name="combine",
    )(x2, yg, yg, rinfo, g_final)


def _gather_rows(table, idx, chunk):
    return _sc_gather_rows(table, idx, chunk)


def kernel(x_prompt, x_sample, mem_prompt, state_conv, cache_mem_k, cache_mem_v, g_mix, w_in, conv_w, conv_b, ln_conv_g, ln_conv_b, ln_v_g, ln_v_b, w_sg, b_sg, w_out, g_mem, w_mk, w_mv, g_xattn, w_xq, w_xo, g_ffn, w_router_group, b_router_group, w_router_expert, b_router_expert, w_expert_gate, w_expert_up, w_expert_down, g_final):
    assert x_prompt.shape[0] == 1 and g_mix.shape[0] == 1
    n_p = x_prompt.shape[1]
    n_batch, t_len = x_sample.shape[0], x_sample.shape[1]
    n_s = n_batch * t_len
    row = lambda a: a.reshape(1, -1)

    w_router = jnp.concatenate(
        [w_router_group[0], jnp.transpose(w_router_expert[0], (1, 0, 2)).reshape(D_MODEL, N_EXPERTS)], axis=1)
    w_router = jnp.pad(w_router, ((0, 0), (0, LOGIT_LANES - w_router.shape[1]))).astype(BF16)
    b_router = jnp.pad(jnp.concatenate([b_router_group[0], b_router_expert[0].reshape(-1)]),
                       (0, LOGIT_LANES - N_GROUPS - N_EXPERTS)).reshape(1, LOGIT_LANES)
    tril_t = jnp.tril(jnp.ones((t_len, t_len), bool))
    w_sg_t = jnp.where(tril_t, w_sg[0][:, :t_len, :t_len], 0.0)
    eye_b = jnp.eye(n_batch, dtype=F32)
    w_sg_bd = jnp.einsum("ab,hij->haibj", eye_b, w_sg_t).reshape(SG_HEADS, n_s, n_s).astype(BF16)
    p = {
        "g_mix": row(g_mix[0]), "w_in": w_in[0].astype(BF16),
        "conv_w": jnp.pad(conv_w[0], ((0, 1), (0, 0))), "conv_b": row(conv_b[0]),
        "ln_conv_g": row(ln_conv_g[0]), "ln_conv_b": row(ln_conv_b[0]),
        "ln_v_g": row(ln_v_g[0]), "ln_v_b": row(ln_v_b[0]),
        "w_sg": w_sg[0],
        "b_sg_rows": jnp.repeat(b_sg[0].T, SG_HEAD_DIM, axis=1),
        "w_sg_bd": w_sg_bd,
        "b_sg_rows_s": jnp.tile(jnp.repeat(b_sg[0][:, :t_len].T, SG_HEAD_DIM, axis=1), (n_batch, 1)),
        "w_out": w_out[0].astype(BF16), "g_xattn": row(g_xattn[0]),
        "w_xq": w_xq[0].astype(BF16), "w_xo": w_xo[0].astype(BF16), "g_ffn": row(g_ffn[0]),
        "w_router": w_router, "b_router": b_router,
    }

    k_p, v_p = _memkv(mem_prompt[0], row(g_mem[0]), w_mk[0].astype(BF16), w_mv[0].astype(BF16))
    p["kt"] = k_p.T.astype(BF16)
    p["v"] = v_p.astype(BF16)
    p["kt_s"] = jnp.transpose(cache_mem_k[0].reshape(n_batch, N_MEM, D_MODEL), (0, 2, 1)).astype(BF16)
    p["v_s"] = cache_mem_v[0].reshape(n_batch, N_MEM, D_MODEL).astype(BF16)

    x2_p, h3_p, rinfo_p, hist_p, cnt_p = _trunk_prompt(x_prompt[0], p)
    x2_s, h3_s, rinfo_s, hist_s, sgv_s, cnt = _trunk_sample(
        x_sample.reshape(n_s, D_MODEL), state_conv[0], cnt_p, p, n_batch, t_len)

    n_tot = n_p + n_s
    n_slots = -(-(n_tot * 2) // BM) * BM + N_EXPERTS * BM
    n_blocks = n_slots // BM
    counts = cnt[0, :N_EXPERTS].astype(jnp.int32)
    padded = (counts + BM - 1) // BM * BM
    pad_end = jnp.cumsum(padded)
    pad_start = pad_end - padded
    rinfo = jnp.concatenate([rinfo_p[:, :8], rinfo_s[:, :8]], axis=0)
    e12 = rinfo[:, 0:2].astype(jnp.int32)
    rank12 = rinfo[:, 4:6].astype(jnp.int32)
    slot12 = pad_start[e12] + rank12
    tok = jnp.broadcast_to(jnp.arange(n_tot, dtype=jnp.int32)[:, None], (n_tot, 2))
    token_of_slot = jnp.zeros((n_slots,), jnp.int32).at[slot12.reshape(-1)].set(tok.reshape(-1))
    n_valid = (pad_end[-1] // BM).astype(jnp.int32)
    blk = jnp.minimum(jnp.arange(n_blocks, dtype=jnp.int32), n_valid - 1)
    block_expert = jnp.minimum(jnp.searchsorted(pad_end, blk * BM, side="right"), N_EXPERTS - 1).astype(jnp.int32)

    h3_all = jnp.concatenate([h3_p, h3_s], axis=0)
    xs = _gather_rows(h3_all, token_of_slot, 64)
    ys = _experts(xs, block_expert, n_valid.reshape(1), w_expert_gate[0], w_expert_up[0], w_expert_down[0])
    back_idx = jnp.concatenate([slot12[:n_p, 0], slot12[:n_p, 1], slot12[n_p:, 0], slot12[n_p:, 1]])
    yg = _gather_rows(ys, back_idx, 32)

    y_p = _combine(x2_p, yg, rinfo_p, row(g_final), TM, 0, n_p // TM)
    y_s = _combine(x2_s, yg, rinfo_s, row(g_final), n_s, 2 * n_p // n_s, 2 * n_p // n_s + 1)

    return (y_p.reshape(1, n_p, D_MODEL),
            y_s.reshape(n_batch, t_len, D_MODEL),
            hist_p[HALO - HIST:].reshape(1, 1, HIST, D_CONV),
            hist_s.reshape(1, n_batch, HIST, D_CONV),
            k_p.reshape(1, 1, N_MEM, X_HEADS, X_HEAD_DIM),
            v_p.reshape(1, 1, N_MEM, X_HEADS, X_HEAD_DIM),
            sgv_s.reshape(1, n_batch, t_len, D_SG))
```

```python
import functools

import jax
import jax.numpy as jnp
from jax import lax
from jax.experimental import pallas as pl
from jax.experimental.pallas import tpu as pltpu
from jax.experimental.pallas import tpu_sc as plsc

D_MODEL = 1024
D_CONV = 512
D_SG = 512
CONV_WIDTH = 31
HIST = CONV_WIDTH - 1
SG_HEADS = 4
SG_HEAD_DIM = 128
SG_CHUNK = 128
N_MEM = 256
X_HEADS = 4
X_HEAD_DIM = 256
N_GROUPS = 4
EXPERTS_PER_GROUP = 8
N_EXPERTS = 32
D_EXPERT = 512
EPS = 1e-6

LANES = 128
SUBLANES = 8
SC_CORES = 2
SC_SUBCORES = 16
SC_WORKERS = SC_CORES * SC_SUBCORES
VMEM_LIMIT = 56 * 1024 * 1024

TM = 512
HALO = 32
CONV_ROWS = 128
BM = 256
LOGIT_LANES = 128

F32 = jnp.float32
BF16 = jnp.bfloat16


def _dot(a, b):
    return jnp.dot(a, b, preferred_element_type=F32)


def _rms(x, g):
    return x * lax.rsqrt(jnp.mean(x * x, axis=-1, keepdims=True) + EPS) * g


def _ln(x, g, b):
    mu = jnp.mean(x, axis=-1, keepdims=True)
    xc = x - mu
    var = jnp.mean(xc * xc, axis=-1, keepdims=True)
    return xc * lax.rsqrt(var + EPS) * g + b


def _sigmoid(x):
    return 1.0 / (1.0 + jnp.exp(-x))


def _pack_bf16_pairs(h):
    bits = lax.bitcast_convert_type(h, jnp.uint32)
    half = h.shape[1] // 2
    lo = lax.shift_right_logical(bits[:, :half], jnp.uint32(16))
    hi = bits[:, half:] & jnp.uint32(0xFFFF0000)
    return hi | lo


def _unpack_bf16_pairs(p):
    lo = lax.bitcast_convert_type(lax.shift_left(p, jnp.uint32(16)), F32)
    hi = lax.bitcast_convert_type(p & jnp.uint32(0xFFFF0000), F32)
    return lo.astype(BF16), hi.astype(BF16)


def _memkv_kernel(mem_ref, g_ref, wk_ref, wv_ref, k_ref, v_ref):
    m = _rms(mem_ref[...], g_ref[...]).astype(BF16)
    k_ref[...] = _dot(m, wk_ref[...])
    v_ref[...] = _dot(m, wv_ref[...])


def _memkv(mem, g_mem, w_mk, w_mv):
    return pl.pallas_call(
        _memkv_kernel,
        out_shape=(jax.ShapeDtypeStruct((N_MEM, D_MODEL), F32), jax.ShapeDtypeStruct((N_MEM, D_MODEL), F32)),
        compiler_params=pltpu.CompilerParams(vmem_limit_bytes=VMEM_LIMIT),
        name="memkv",
    )(mem, g_mem, w_mk, w_mv)


def _attn_heads(q, kt, v):
    outs = []
    for h in range(X_HEADS):
        sl = slice(h * X_HEAD_DIM, (h + 1) * X_HEAD_DIM)
        s = _dot(q[:, sl], kt[sl, :]) * (X_HEAD_DIM ** -0.5)
        s = s - jnp.max(s, axis=-1, keepdims=True)
        p = jnp.exp(s)
        p = p / jnp.sum(p, axis=-1, keepdims=True)
        outs.append(_dot(p.astype(BF16), v[:, sl]).astype(BF16))
    return jnp.concatenate(outs, axis=1)


def _route(h3, wr_ref, br_ref, run):
    m = h3.shape[0]
    logits = _dot(h3, wr_ref[...]) + br_ref[...]
    lane = lax.broadcasted_iota(jnp.int32, (m, LOGIT_LANES), 1).astype(F32)
    neg = jnp.float32(-jnp.inf)
    big = jnp.float32(LOGIT_LANES)

    def first_argmax(vals):
        mx = jnp.max(vals, axis=-1, keepdims=True)
        idx = jnp.min(jnp.where(vals == mx, lane, big), axis=-1, keepdims=True)
        return mx, idx

    lg = jnp.where(lane < N_GROUPS, logits, neg)
    g_max, g_idx = first_argmax(lg)
    g_w = 1.0 / jnp.sum(jnp.exp(lg - g_max), axis=-1, keepdims=True)

    lo = N_GROUPS + g_idx * EXPERTS_PER_GROUP
    le = jnp.where((lane >= lo) & (lane < lo + EXPERTS_PER_GROUP), logits, neg)
    v1, i1 = first_argmax(le)
    v2, i2 = first_argmax(jnp.where(lane == i1, neg, le))
    t = jnp.exp(v2 - v1)
    gate1 = g_w / (1.0 + t)
    gate2 = g_w * t / (1.0 + t)
    e1 = i1 - N_GROUPS
    e2 = i2 - N_GROUPS

    oh1 = (lane == e1).astype(F32)
    oh2 = (lane == e2).astype(F32)
    oh = oh1 + oh2
    row = lax.broadcasted_iota(jnp.int32, (m, m), 0)
    col = lax.broadcasted_iota(jnp.int32, (m, m), 1)
    strict_lower = (col < row).astype(BF16)
    before = _dot(strict_lower, oh.astype(BF16)) + run
    rank1 = jnp.sum(before * oh1, axis=-1, keepdims=True)
    rank2 = jnp.sum(before * oh2, axis=-1, keepdims=True)
    new_run = run + jnp.sum(oh, axis=0, keepdims=True)

    rinfo = jnp.where(lane == 0, e1,
            jnp.where(lane == 1, e2,
            jnp.where(lane == 2, gate1,
            jnp.where(lane == 3, gate2,
            jnp.where(lane == 4, rank1,
            jnp.where(lane == 5, rank2, 0.0))))))
    return rinfo, new_run


def _conv_taps(win, w_ref, l0, rows):
    n = win.shape[0]
    acc = jnp.zeros((rows, LANES), F32)
    for r in range(SUBLANES):
        d = HALO - HIST + r
        if d % SUBLANES == 0:
            e, base = win, d
        else:
            e, base = pltpu.roll(win, n - d, axis=0), 0
        for qq in range(-(-CONV_WIDTH // SUBLANES)):
            k = SUBLANES * qq + r
            if k < CONV_WIDTH:
                s0 = base + SUBLANES * qq
                acc = acc + e[s0:s0 + rows, :] * w_ref[pl.ds(k, 1), pl.ds(l0, LANES)]
    return acc


def _trunk_prompt_kernel(x_ref, gmix_ref, win_ref, convw_ref, convb_ref, lncg_ref, lncb_ref, lnvg_ref, lnvb_ref,
                         wsg_ref, bsg_ref, wout_ref, gx_ref, wxq_ref, kt_ref, v_ref, wxo_ref, gffn_ref, wr_ref,
                         br_ref,
                         x2_ref, h3_ref, rinfo_ref, hist_ref, cnt_ref,
                         ext_ref, conv_ref, run_ref):
    i = pl.program_id(0)

    @pl.when(i == 0)
    def _():
        ext_ref[0:HALO, :] = jnp.zeros((HALO, D_CONV), F32)
        run_ref[...] = jnp.zeros((1, LOGIT_LANES), F32)

    x = x_ref[...]
    h = _rms(x, gmix_ref[...]).astype(BF16)

    a_in = _dot(h, win_ref[:, 0:D_CONV])
    a_gate = _dot(h, win_ref[:, D_CONV:2 * D_CONV])
    ext_ref[HALO:HALO + TM, :] = a_in * _sigmoid(a_gate)

    def conv_body(c, carry):
        r0 = pl.multiple_of((c // 4) * CONV_ROWS, CONV_ROWS)
        l0 = pl.multiple_of((c % 4) * LANES, LANES)
        win = ext_ref[pl.ds(r0, CONV_ROWS + HALO), pl.ds(l0, LANES)]
        conv_ref[pl.ds(r0, CONV_ROWS), pl.ds(l0, LANES)] = _conv_taps(win, convw_ref, l0, CONV_ROWS)
        return carry

    lax.fori_loop(0, (TM // CONV_ROWS) * (D_CONV // LANES), conv_body, 0)
    hist_ref[...] = ext_ref[TM:TM + HALO, :]
    ext_ref[0:HALO, :] = ext_ref[TM:TM + HALO, :]

    y = _ln(conv_ref[...] + convb_ref[...], lncg_ref[...], lncb_ref[...])
    a_out = (y * _sigmoid(y)).astype(BF16)

    u = _dot(h, win_ref[:, 2 * D_CONV:2 * D_CONV + D_SG])
    v = _ln(_dot(h, win_ref[:, 2 * D_CONV + D_SG:]), lnvg_ref[...], lnvb_ref[...]).astype(BF16)
    ri = lax.broadcasted_iota(jnp.int32, (SG_CHUNK, SG_CHUNK), 0)
    ci = lax.broadcasted_iota(jnp.int32, (SG_CHUNK, SG_CHUNK), 1)
    w_tril = [jnp.where(ci <= ri, wsg_ref[hh], 0.0).astype(BF16) for hh in range(SG_HEADS)]
    gate_rows = []
    for c in range(TM // SG_CHUNK):
        rs = slice(c * SG_CHUNK, (c + 1) * SG_CHUNK)
        heads = [_dot(w_tril[hh], v[rs, hh * SG_HEAD_DIM:(hh + 1) * SG_HEAD_DIM]) for hh in range(SG_HEADS)]
        gate_rows.append(jnp.concatenate(heads, axis=1) + bsg_ref[...])
    b_out = (u * jnp.concatenate(gate_rows, axis=0)).astype(BF16)

    x1 = x + _dot(a_out, wout_ref[0:D_CONV, :]) + _dot(b_out, wout_ref[D_CONV:, :])

    hx = _rms(x1, gx_ref[...]).astype(BF16)
    q = _dot(hx, wxq_ref[...]).astype(BF16)
    x2 = x1 + _dot(_attn_heads(q, kt_ref[...], v_ref[...]), wxo_ref[...])
    x2_ref[...] = x2

    h3 = _rms(x2, gffn_ref[...]).astype(BF16)
    h3_ref[...] = _pack_bf16_pairs(h3.astype(F32))
    rinfo, new_run = _route(h3, wr_ref, br_ref, run_ref[...])
    rinfo_ref[...] = rinfo
    run_ref[...] = new_run
    cnt_ref[...] = new_run


def _const_spec(shape):
    nd = len(shape)
    return pl.BlockSpec(shape, lambda i: (0,) * nd, pipeline_mode=pl.Buffered(1))


def _trunk_prompt(x, p):
    n = x.shape[0]
    assert n % TM == 0
    row = lambda w: pl.BlockSpec((TM, w), lambda i: (i, 0))
    consts = [p["g_mix"], p["w_in"], p["conv_w"], p["conv_b"], p["ln_conv_g"], p["ln_conv_b"], p["ln_v_g"],
              p["ln_v_b"], p["w_sg"], p["b_sg_rows"], p["w_out"], p["g_xattn"], p["w_xq"], p["kt"], p["v"],
              p["w_xo"], p["g_ffn"], p["w_router"], p["b_router"]]
    return pl.pallas_call(
        _trunk_prompt_kernel,
        grid=(n // TM,),
        in_specs=[row(D_MODEL)] + [_const_spec(c.shape) for c in consts],
        out_specs=(row(D_MODEL), row(D_MODEL // 2), row(LOGIT_LANES),
                   pl.BlockSpec((HALO, D_CONV), lambda i: (0, 0)),
                   pl.BlockSpec((1, LOGIT_LANES), lambda i: (0, 0))),
        out_shape=(jax.ShapeDtypeStruct((n, D_MODEL), F32),
                   jax.ShapeDtypeStruct((n, D_MODEL // 2), jnp.uint32),
                   jax.ShapeDtypeStruct((n, LOGIT_LANES), F32),
                   jax.ShapeDtypeStruct((HALO, D_CONV), F32),
                   jax.ShapeDtypeStruct((1, LOGIT_LANES), F32)),
        scratch_shapes=[pltpu.VMEM((TM + HALO, D_CONV), F32),
                        pltpu.VMEM((TM, D_CONV), F32),
                        pltpu.VMEM((1, LOGIT_LANES), F32)],
        compiler_params=pltpu.CompilerParams(dimension_semantics=("arbitrary",), vmem_limit_bytes=VMEM_LIMIT),
        name="trunk_prompt",
    )(x, *consts)


def _trunk_sample_kernel(n_batch, t_len,
                         x_ref, hist_in_ref, run_in_ref, gmix_ref, win_ref, convw_ref, convb_ref, lncg_ref, lncb_ref,
                         lnvg_ref, lnvb_ref, wsgbd_ref, bsg_ref, wout_ref, gx_ref, wxq_ref, kt_ref, v_ref, wxo_ref,
                         gffn_ref, wr_ref, br_ref,
                         x2_ref, h3_ref, rinfo_ref, hist_ref, sgv_ref, cnt_ref,
                         ext_ref, conv_ref, att_ref):
    x = x_ref[...]
    h = _rms(x, gmix_ref[...]).astype(BF16)
    z = _dot(h, win_ref[...])
    a = z[:, 0:D_CONV] * _sigmoid(z[:, D_CONV:2 * D_CONV])
    ext_len = HIST + t_len
    for b in range(n_batch):
        ext_ref[b, 0:HIST, :] = hist_in_ref[b]
        ext_ref[b, HIST:ext_len, :] = a[b * t_len:(b + 1) * t_len, :]
    for b in range(n_batch):
        acc = jnp.zeros((t_len, D_CONV), F32)
        for k in range(CONV_WIDTH):
            acc = acc + ext_ref[b, k:k + t_len, :] * convw_ref[k:k + 1, :]
        conv_ref[b * t_len:(b + 1) * t_len, :] = acc
        hist_ref[b] = ext_ref[b, ext_len - HIST:ext_len, :]

    y = _ln(conv_ref[...] + convb_ref[...], lncg_ref[...], lncb_ref[...])
    a_out = (y * _sigmoid(y)).astype(BF16)

    u = z[:, 2 * D_CONV:2 * D_CONV + D_SG]
    v = _ln(z[:, 2 * D_CONV + D_SG:], lnvg_ref[...], lnvb_ref[...])
    sgv_ref[...] = v
    vb = v.astype(BF16)
    heads = [_dot(wsgbd_ref[hh], vb[:, hh * SG_HEAD_DIM:(hh + 1) * SG_HEAD_DIM]) for hh in range(SG_HEADS)]
    b_out = (u * (jnp.concatenate(heads, axis=1) + bsg_ref[...])).astype(BF16)

    x1 = x + _dot(a_out, wout_ref[0:D_CONV, :]) + _dot(b_out, wout_ref[D_CONV:, :])

    hx = _rms(x1, gx_ref[...]).astype(BF16)
    q = _dot(hx, wxq_ref[...]).astype(BF16)
    for b in range(n_batch):
        rs = slice(b * t_len, (b + 1) * t_len)
        att_ref[rs, :] = _attn_heads(q[rs, :], kt_ref[b], v_ref[b])
    x2 = x1 + _dot(att_ref[...], wxo_ref[...])
    x2_ref[...] = x2

    h3 = _rms(x2, gffn_ref[...]).astype(BF16)
    m = n_batch * t_len
    h3_ref[0:m, :] = _pack_bf16_pairs(h3.astype(F32))
    if h3_ref.shape[0] > m:
        h3_ref[m:, :] = jnp.zeros((h3_ref.shape[0] - m, D_MODEL // 2), jnp.uint32)
    rinfo, new_run = _route(h3, wr_ref, br_ref, run_in_ref[...])
    rinfo_ref[...] = rinfo
    cnt_ref[...] = new_run


def _trunk_sample(x, hist, run, p, n_batch, t_len):
    m = n_batch * t_len
    args = [x, hist, run, p["g_mix"], p["w_in"], p["conv_w"], p["conv_b"], p["ln_conv_g"], p["ln_conv_b"],
            p["ln_v_g"], p["ln_v_b"], p["w_sg_bd"], p["b_sg_rows_s"], p["w_out"], p["g_xattn"], p["w_xq"],
            p["kt_s"], p["v_s"], p["w_xo"], p["g_ffn"], p["w_router"], p["b_router"]]
    return pl.pallas_call(
        functools.partial(_trunk_sample_kernel, n_batch, t_len),
        out_shape=(jax.ShapeDtypeStruct((m, D_MODEL), F32),
                   jax.ShapeDtypeStruct((-(-m // (SC_WORKERS * SUBLANES)) * SC_WORKERS * SUBLANES, D_MODEL // 2),
                                        jnp.uint32),
                   jax.ShapeDtypeStruct((m, LOGIT_LANES), F32),
                   jax.ShapeDtypeStruct((n_batch, HIST, D_CONV), F32),
                   jax.ShapeDtypeStruct((m, D_SG), F32),
                   jax.ShapeDtypeStruct((1, LOGIT_LANES), F32)),
        scratch_shapes=[pltpu.VMEM((n_batch, HIST + t_len, D_CONV), F32),
                        pltpu.VMEM((m, D_CONV), F32),
                        pltpu.VMEM((m, D_MODEL), BF16)],
        compiler_params=pltpu.CompilerParams(vmem_limit_bytes=VMEM_LIMIT),
        name="trunk_sample",
    )(*args)


def _sc_worker_id():
    return lax.axis_index("s") * SC_CORES + lax.axis_index("c")


def _sc_chunk(per_w, max_chunk):
    assert per_w % SUBLANES == 0 and max_chunk <= LANES
    return max(c for c in range(SUBLANES, max_chunk + 1, SUBLANES) if per_w % c == 0)


def _sc_gather_rows(table, idx, max_chunk):
    n_rows, d = idx.shape[0], table.shape[1]
    per_w = n_rows // SC_WORKERS
    assert per_w * SC_WORKERS == n_rows
    chunk = _sc_chunk(per_w, max_chunk)
    n_chunks = per_w // chunk
    mesh = plsc.VectorSubcoreMesh(core_axis_name="c", subcore_axis_name="s")

    @functools.partial(
        pl.kernel, mesh=mesh,
        out_type=jax.ShapeDtypeStruct((n_rows, d), table.dtype),
        scratch_types=[pltpu.VMEM((chunk,), jnp.int32), pltpu.VMEM((chunk,), jnp.int32),
                       pltpu.VMEM((chunk, d), table.dtype), pltpu.VMEM((chunk, d), table.dtype),
                       pltpu.SemaphoreType.DMA, pltpu.SemaphoreType.DMA,
                       pltpu.SemaphoreType.DMA, pltpu.SemaphoreType.DMA],
    )
    def gather(table_hbm, idx_hbm, out_hbm, idx0, idx1, rows0, rows1, gsem0, gsem1, wsem0, wsem1):
        base = _sc_worker_id() * per_w
        bufs = ((idx0, rows0, gsem0, wsem0), (idx1, rows1, gsem1, wsem1))

        def start_read(j, b):
            idx_v, rows_v, gsem, _ = bufs[b]
            pltpu.sync_copy(idx_hbm.at[pl.ds(base + j * chunk, chunk)], idx_v)
            pltpu.async_copy(table_hbm.at[idx_v], rows_v, gsem)

        def finish(j, b):
            idx_v, rows_v, gsem, wsem = bufs[b]
            pltpu.make_async_copy(table_hbm.at[idx_v], rows_v, gsem).wait()
            pltpu.async_copy(rows_v, out_hbm.at[pl.ds(base + j * chunk, chunk)], wsem).wait()

        start_read(0, 0)

        @pl.loop(0, n_chunks // 2)
        def _(jj):
            j0 = 2 * jj
            start_read(j0 + 1, 1)
            finish(j0, 0)

            @pl.when(j0 + 2 < n_chunks)
            def _():
                start_read(j0 + 2, 0)

            finish(j0 + 1, 1)

        if n_chunks % 2 == 1:
            finish(n_chunks - 1, 0)

    return gather(table, idx)


def _sc_scatter_rows2(tables, slots_a, slots_b, n_rows_out, max_chunk):
    d, dtype = tables[0].shape[1], tables[0].dtype
    plans = []
    for t in tables:
        per_w = t.shape[0] // SC_WORKERS
        assert per_w * SC_WORKERS == t.shape[0]
        chunk = _sc_chunk(per_w, max_chunk)
        plans.append((per_w, chunk, per_w // chunk))
    cmax = max(c for _, c, _ in plans)
    n_t = len(tables)
    mesh = plsc.VectorSubcoreMesh(core_axis_name="c", subcore_axis_name="s")

    scratch = []
    for _, chunk, _ in plans:
        for _ in range(2):
            scratch += [pltpu.VMEM((chunk,), jnp.int32), pltpu.VMEM((chunk,), jnp.int32)]
    scratch += [pltpu.VMEM((cmax, d), dtype), pltpu.VMEM((cmax, d), dtype)]
    scratch += [pltpu.SemaphoreType.DMA] * 4

    @functools.partial(pl.kernel, mesh=mesh, out_type=jax.ShapeDtypeStruct((n_rows_out, d), dtype),
                       scratch_types=scratch)
    def scatter(*refs):
        tab_hbm = refs[0:n_t]
        sa_hbm = refs[n_t:2 * n_t]
        sb_hbm = refs[2 * n_t:3 * n_t]
        out_hbm = refs[3 * n_t]
        sc = refs[3 * n_t + 1:]
        idx_refs = sc[:4 * n_t]
        rows = sc[4 * n_t:4 * n_t + 2]
        lsem = sc[4 * n_t + 2:4 * n_t + 4]
        ssem = sc[4 * n_t + 4:4 * n_t + 6]
        wid = _sc_worker_id()

        work = []
        for t, (per_w, chunk, n_chunks) in enumerate(plans):
            for j in range(n_chunks):
                work.append((t, wid * per_w + j * chunk, chunk))

        def parts(k):
            t, off, chunk = work[k]
            b = k % 2
            ia, ib = idx_refs[4 * t + 2 * b], idx_refs[4 * t + 2 * b + 1]
            rv = rows[b] if chunk == cmax else rows[b].at[pl.ds(0, chunk)]
            return t, off, chunk, b, ia, ib, rv

        def start_load(k):
            t, off, chunk, b, ia, ib, rv = parts(k)
            return (pltpu.async_copy(tab_hbm[t].at[pl.ds(off, chunk)], rv, lsem[b]),
                    pltpu.async_copy(sa_hbm[t].at[pl.ds(off, chunk)], ia, lsem[b]),
                    pltpu.async_copy(sb_hbm[t].at[pl.ds(off, chunk)], ib, lsem[b]))

        def start_scatter(k):
            t, off, chunk, b, ia, ib, rv = parts(k)
            return (pltpu.async_copy(rv, out_hbm.at[ia], ssem[b]), pltpu.async_copy(rv, out_hbm.at[ib], ssem[b]))

        loads = {0: start_load(0)}
        scatters = {}
        for k in range(len(work)):
            for c in loads.pop(k):
                c.wait()
            scatters[k] = start_scatter(k)
            if k >= 1:
                for c in scatters.pop(k - 1):
                    c.wait()
            if k + 1 < len(work):
                loads[k + 1] = start_load(k + 1)
        for c in scatters.pop(len(work) - 1):
            c.wait()

    return scatter(*tables, *slots_a, *slots_b)


def _experts_kernel(be_ref, rows_ref, nv_ref, xs_ref, wg_ref, wu_ref, wd_ref, ys_ref, wg_bf, wu_bf, wd_bf):
    b = pl.program_id(0)

    @pl.when(b < nv_ref[0])
    def _():
        prev = be_ref[jnp.maximum(b - 1, 0)]

        @pl.when((b == 0) | (be_ref[b] != prev))
        def _():
            wg_bf[...] = wg_ref[0].astype(BF16)
            wu_bf[...] = wu_ref[0].astype(BF16)
            wd_bf[...] = wd_ref[0].astype(BF16)

        half = D_MODEL // 2
        live = lax.broadcasted_iota(jnp.int32, (BM, half), 0) < rows_ref[b]
        lo, hi = _unpack_bf16_pairs(jnp.where(live, xs_ref[...], jnp.uint32(0)))
        g = _dot(lo, wg_bf[0:half, :]) + _dot(hi, wg_bf[half:, :])
        u = _dot(lo, wu_bf[0:half, :]) + _dot(hi, wu_bf[half:, :])
        hm = (g * _sigmoid(g) * u).astype(BF16)
        ys_ref[...] = _dot(hm, wd_bf[...])


def _experts(xs, n_blocks, block_expert, block_rows, n_valid, w_eg, w_eu, w_ed):
    row_map = lambda b, be, br, nv: (jnp.minimum(b, nv[0] - 1), 0)
    w_map = lambda b, be, br, nv: (be[b], 0, 0)
    return pl.pallas_call(
        _experts_kernel,
        grid_spec=pltpu.PrefetchScalarGridSpec(
            num_scalar_prefetch=3,
            grid=(n_blocks,),
            in_specs=[pl.BlockSpec((BM, D_MODEL // 2), row_map),
                      pl.BlockSpec((1, D_MODEL, D_EXPERT), w_map),
                      pl.BlockSpec((1, D_MODEL, D_EXPERT), w_map),
                      pl.BlockSpec((1, D_EXPERT, D_MODEL), w_map)],
            out_specs=pl.BlockSpec((BM, D_MODEL), row_map),
            scratch_shapes=[pltpu.VMEM((D_MODEL, D_EXPERT), BF16), pltpu.VMEM((D_MODEL, D_EXPERT), BF16),
                            pltpu.VMEM((D_EXPERT, D_MODEL), BF16)]),
        out_shape=jax.ShapeDtypeStruct((n_blocks * BM, D_MODEL), F32),
        compiler_params=pltpu.CompilerParams(dimension_semantics=("arbitrary",), vmem_limit_bytes=VMEM_LIMIT),
        name="experts",
    )(block_expert, block_rows, n_valid, xs, w_eg, w_eu, w_ed)


def _combine_kernel(x2_ref, y1_ref, y2_ref, rinfo_ref, g_ref, o_ref):
    r = rinfo_ref[...]
    x = x2_ref[...] + r[:, 2:3] * y1_ref[...] + r[:, 3:4] * y2_ref[...]
    o_ref[...] = _rms(x, g_ref[...])


def _combine(x2, yg, rinfo, g_final, tm, blk1, blk2):
    n = x2.shape[0]
    return pl.pallas_call(
        _combine_kernel,
        grid=(n // tm,),
        in_specs=[pl.BlockSpec((tm, D_MODEL), lambda i: (i, 0)),
                  pl.BlockSpec((tm, D_MODEL), lambda i: (blk1 + i, 0)),
                  pl.BlockSpec((tm, D_MODEL), lambda i: (blk2 + i, 0)),
                  pl.BlockSpec((tm, LOGIT_LANES), lambda i: (i, 0)),
                  pl.BlockSpec((1, D_MODEL), lambda i: (0, 0))],
        out_specs=pl.BlockSpec((tm, D_MODEL), lambda i: (i, 0)),
        out_shape=jax.ShapeDtypeStruct((n, D_MODEL), F32),
        compiler_params=pltpu.CompilerParams(dimension_semantics=("arbitrary",), vmem_limit_bytes=VMEM_LIMIT),
        name="combine",
    )(x2, yg, yg, rinfo, g_final)


def _gather_rows(table, idx):
    return _sc_gather_rows(table, idx, 32)


def _scatter_rows2(tables, slots_a, slots_b, n_rows_out):
    return _sc_scatter_rows2(tables, slots_a, slots_b, n_rows_out, 64)


def kernel(x_prompt, x_sample, mem_prompt, state_conv, cache_mem_k, cache_mem_v, g_mix, w_in, conv_w, conv_b, ln_conv_g, ln_conv_b, ln_v_g, ln_v_b, w_sg, b_sg, w_out, g_mem, w_mk, w_mv, g_xattn, w_xq, w_xo, g_ffn, w_router_group, b_router_group, w_router_expert, b_router_expert, w_expert_gate, w_expert_up, w_expert_down, g_final):
    assert x_prompt.shape[0] == 1 and g_mix.shape[0] == 1
    n_p = x_prompt.shape[1]
    n_batch, t_len = x_sample.shape[0], x_sample.shape[1]
    n_s = n_batch * t_len
    row = lambda a: a.reshape(1, -1)

    w_router = jnp.concatenate(
        [w_router_group[0], jnp.transpose(w_router_expert[0], (1, 0, 2)).reshape(D_MODEL, N_EXPERTS)], axis=1)
    w_router = jnp.pad(w_router, ((0, 0), (0, LOGIT_LANES - w_router.shape[1]))).astype(BF16)
    b_router = jnp.pad(jnp.concatenate([b_router_group[0], b_router_expert[0].reshape(-1)]),
                       (0, LOGIT_LANES - N_GROUPS - N_EXPERTS)).reshape(1, LOGIT_LANES)
    tril_t = jnp.tril(jnp.ones((t_len, t_len), bool))
    w_sg_t = jnp.where(tril_t, w_sg[0][:, :t_len, :t_len], 0.0)
    eye_b = jnp.eye(n_batch, dtype=F32)
    w_sg_bd = jnp.einsum("ab,hij->haibj", eye_b, w_sg_t).reshape(SG_HEADS, n_s, n_s).astype(BF16)
    p = {
        "g_mix": row(g_mix[0]), "w_in": w_in[0].astype(BF16),
        "conv_w": jnp.pad(conv_w[0], ((0, 1), (0, 0))), "conv_b": row(conv_b[0]),
        "ln_conv_g": row(ln_conv_g[0]), "ln_conv_b": row(ln_conv_b[0]),
        "ln_v_g": row(ln_v_g[0]), "ln_v_b": row(ln_v_b[0]),
        "w_sg": w_sg[0],
        "b_sg_rows": jnp.repeat(b_sg[0].T, SG_HEAD_DIM, axis=1),
        "w_sg_bd": w_sg_bd,
        "b_sg_rows_s": jnp.tile(jnp.repeat(b_sg[0][:, :t_len].T, SG_HEAD_DIM, axis=1), (n_batch, 1)),
        "w_out": w_out[0].astype(BF16), "g_xattn": row(g_xattn[0]),
        "w_xq": w_xq[0].astype(BF16), "w_xo": w_xo[0].astype(BF16), "g_ffn": row(g_ffn[0]),
        "w_router": w_router, "b_router": b_router,
    }

    k_p, v_p = _memkv(mem_prompt[0], row(g_mem[0]), w_mk[0].astype(BF16), w_mv[0].astype(BF16))
    p["kt"] = k_p.T.astype(BF16)
    p["v"] = v_p.astype(BF16)
    p["kt_s"] = jnp.transpose(cache_mem_k[0].reshape(n_batch, N_MEM, D_MODEL), (0, 2, 1)).astype(BF16)
    p["v_s"] = cache_mem_v[0].reshape(n_batch, N_MEM, D_MODEL).astype(BF16)

    x2_p, h3_p, rinfo_p, hist_p, cnt_p = _trunk_prompt(x_prompt[0], p)
    x2_s, h3_s, rinfo_s, hist_s, sgv_s, cnt = _trunk_sample(
        x_sample.reshape(n_s, D_MODEL), state_conv[0], cnt_p, p, n_batch, t_len)

    n_tot = n_p + n_s
    n_slots = -(-(n_tot * 2) // BM) * BM + N_EXPERTS * BM
    n_blocks = n_slots // BM
    counts = cnt[0, :N_EXPERTS].astype(jnp.int32)
    padded = (counts + BM - 1) // BM * BM
    pad_end = jnp.cumsum(padded)
    pad_start = pad_end - padded
    experts = jnp.arange(N_EXPERTS, dtype=jnp.int32)

    def slots_of(rinfo8):
        e12 = rinfo8[:, 0:2].astype(jnp.int32)
        rank12 = rinfo8[:, 4:6].astype(jnp.int32)
        start12 = jnp.sum(jnp.where(e12[:, :, None] == experts, pad_start, 0), axis=-1)
        return start12 + rank12

    slot_p = slots_of(rinfo_p[:, :8])
    slot_s = slots_of(rinfo_s[:n_s, :8])
    n_valid = (pad_end[-1] // BM).astype(jnp.int32)
    blk = jnp.minimum(jnp.arange(n_blocks, dtype=jnp.int32), n_valid - 1)
    block_expert = jnp.minimum(jnp.sum((pad_end[None, :] <= (blk * BM)[:, None]).astype(jnp.int32), axis=1),
                               N_EXPERTS - 1)
    is_e = block_expert[:, None] == experts
    block_rows = jnp.clip(jnp.sum(jnp.where(is_e, pad_start + counts, 0), axis=1) - blk * BM, 0, BM)

    n_spare = h3_s.shape[0] - n_s
    spare = n_slots + jnp.arange(n_spare, dtype=jnp.int32)
    xs = _scatter_rows2(
        (h3_p, h3_s),
        (slot_p[:, 0], jnp.concatenate([slot_s[:, 0], spare])),
        (slot_p[:, 1], jnp.concatenate([slot_s[:, 1], spare + n_spare])),
        n_slots + 2 * n_spare)
    ys = _experts(xs, n_blocks, block_expert, block_rows, n_valid.reshape(1),
                  w_expert_gate[0], w_expert_up[0], w_expert_down[0])
    back_idx = jnp.concatenate([slot_p[:, 0], slot_p[:, 1], slot_s[:, 0], slot_s[:, 1]])
    n_back = -(-back_idx.shape[0] // (SC_WORKERS * 32)) * (SC_WORKERS * 32)
    yg = _gather_rows(ys, jnp.pad(back_idx, (0, n_back - back_idx.shape[0])))

    y_p = _combine(x2_p, yg, rinfo_p, row(g_final), TM, 0, n_p // TM)
    y_s = _combine(x2_s, yg, rinfo_s, row(g_final), n_s, 2 * n_p // n_s, 2 * n_p // n_s + 1)

    return (y_p.reshape(1, n_p, D_MODEL),
            y_s.reshape(n_batch, t_len, D_MODEL),
            hist_p[HALO - HIST:].reshape(1, 1, HIST, D_CONV),
            hist_s.reshape(1, n_batch, HIST, D_CONV),
            k_p.reshape(1, 1, N_MEM, X_HEADS, X_HEAD_DIM),
            v_p.reshape(1, 1, N_MEM, X_HEADS, X_HEAD_DIM),
            sgv_s.reshape(1, n_batch, t_len, D_SG))
```

```python
import functools

import jax
import jax.numpy as jnp
from jax import lax
from jax.experimental import pallas as pl
from jax.experimental.pallas import tpu as pltpu
from jax.experimental.pallas import tpu_sc as plsc

D_MODEL = 1024
D_CONV = 512
D_SG = 512
CONV_WIDTH = 31
HIST = CONV_WIDTH - 1
SG_HEADS = 4
SG_HEAD_DIM = 128
SG_CHUNK = 128
N_MEM = 256
X_HEADS = 4
X_HEAD_DIM = 256
N_GROUPS = 4
EXPERTS_PER_GROUP = 8
N_EXPERTS = 32
D_EXPERT = 512
EPS = 1e-6

LANES = 128
SUBLANES = 8
SC_CORES = 2
SC_SUBCORES = 16
SC_WORKERS = SC_CORES * SC_SUBCORES
VMEM_LIMIT = 56 * 1024 * 1024

TM = 512
HALO = 32
CONV_ROWS = 128
BM = 256
GATHER_CHUNK = 64
LOGIT_LANES = 128

F32 = jnp.float32
BF16 = jnp.bfloat16


def _dot(a, b):
    return jnp.dot(a, b, preferred_element_type=F32)


def _rms(x, g):
    return x * lax.rsqrt(jnp.mean(x * x, axis=-1, keepdims=True) + EPS) * g


def _ln(x, g, b):
    mu = jnp.mean(x, axis=-1, keepdims=True)
    xc = x - mu
    var = jnp.mean(xc * xc, axis=-1, keepdims=True)
    return xc * lax.rsqrt(var + EPS) * g + b


def _sigmoid(x):
    return 1.0 / (1.0 + jnp.exp(-x))


def _pack_bf16_pairs(h):
    bits = lax.bitcast_convert_type(h, jnp.uint32)
    half = h.shape[1] // 2
    lo = lax.shift_right_logical(bits[:, :half], jnp.uint32(16))
    hi = bits[:, half:] & jnp.uint32(0xFFFF0000)
    return hi | lo


def _unpack_bf16_pairs_f32(p):
    lo = lax.bitcast_convert_type(lax.shift_left(p, jnp.uint32(16)), F32)
    hi = lax.bitcast_convert_type(p & jnp.uint32(0xFFFF0000), F32)
    return lo, hi


def _unpack_bf16_pairs(p):
    lo, hi = _unpack_bf16_pairs_f32(p)
    return lo.astype(BF16), hi.astype(BF16)


def _memkv_kernel(mem_ref, g_ref, wk_ref, wv_ref, k_ref, v_ref):
    m = _rms(mem_ref[...], g_ref[...]).astype(BF16)
    k_ref[...] = _dot(m, wk_ref[...])
    v_ref[...] = _dot(m, wv_ref[...])


def _memkv(mem, g_mem, w_mk, w_mv):
    return pl.pallas_call(
        _memkv_kernel,
        out_shape=(jax.ShapeDtypeStruct((N_MEM, D_MODEL), F32), jax.ShapeDtypeStruct((N_MEM, D_MODEL), F32)),
        compiler_params=pltpu.CompilerParams(vmem_limit_bytes=VMEM_LIMIT),
        name="memkv",
    )(mem, g_mem, w_mk, w_mv)


def _attn_heads(q, kt, v):
    outs = []
    for h in range(X_HEADS):
        sl = slice(h * X_HEAD_DIM, (h + 1) * X_HEAD_DIM)
        s = _dot(q[:, sl], kt[sl, :]) * (X_HEAD_DIM ** -0.5)
        s = s - jnp.max(s, axis=-1, keepdims=True)
        p = jnp.exp(s)
        p = p / jnp.sum(p, axis=-1, keepdims=True)
        outs.append(_dot(p.astype(BF16), v[:, sl]).astype(BF16))
    return jnp.concatenate(outs, axis=1)


def _route(h3, wr_ref, br_ref, run):
    m = h3.shape[0]
    logits = _dot(h3, wr_ref[...]) + br_ref[...]
    lane = lax.broadcasted_iota(jnp.int32, (m, LOGIT_LANES), 1).astype(F32)
    neg = jnp.float32(-jnp.inf)
    big = jnp.float32(LOGIT_LANES)

    def first_argmax(vals):
        mx = jnp.max(vals, axis=-1, keepdims=True)
        idx = jnp.min(jnp.where(vals == mx, lane, big), axis=-1, keepdims=True)
        return mx, idx

    lg = jnp.where(lane < N_GROUPS, logits, neg)
    g_max, g_idx = first_argmax(lg)
    g_w = 1.0 / jnp.sum(jnp.exp(lg - g_max), axis=-1, keepdims=True)

    lo = N_GROUPS + g_idx * EXPERTS_PER_GROUP
    le = jnp.where((lane >= lo) & (lane < lo + EXPERTS_PER_GROUP), logits, neg)
    v1, i1 = first_argmax(le)
    v2, i2 = first_argmax(jnp.where(lane == i1, neg, le))
    t = jnp.exp(v2 - v1)
    gate1 = g_w / (1.0 + t)
    gate2 = g_w * t / (1.0 + t)
    e1 = i1 - N_GROUPS
    e2 = i2 - N_GROUPS

    oh1 = (lane == e1).astype(F32)
    oh2 = (lane == e2).astype(F32)
    oh = oh1 + oh2
    row = lax.broadcasted_iota(jnp.int32, (m, m), 0)
    col = lax.broadcasted_iota(jnp.int32, (m, m), 1)
    strict_lower = (col < row).astype(BF16)
    before = _dot(strict_lower, oh.astype(BF16)) + run
    rank1 = jnp.sum(before * oh1, axis=-1, keepdims=True)
    rank2 = jnp.sum(before * oh2, axis=-1, keepdims=True)
    new_run = run + jnp.sum(oh, axis=0, keepdims=True)

    rinfo = jnp.where(lane == 0, e1,
            jnp.where(lane == 1, e2,
            jnp.where(lane == 2, gate1,
            jnp.where(lane == 3, gate2,
            jnp.where(lane == 4, rank1,
            jnp.where(lane == 5, rank2, 0.0))))))
    return rinfo, new_run


def _conv_taps(win, w_ref, l0, rows):
    n = win.shape[0]
    acc = jnp.zeros((rows, LANES), F32)
    for r in range(SUBLANES):
        d = HALO - HIST + r
        if d % SUBLANES == 0:
            e, base = win, d
        else:
            e, base = pltpu.roll(win, n - d, axis=0), 0
        for qq in range(-(-CONV_WIDTH // SUBLANES)):
            k = SUBLANES * qq + r
            if k < CONV_WIDTH:
                s0 = base + SUBLANES * qq
                acc = acc + e[s0:s0 + rows, :] * w_ref[pl.ds(k, 1), pl.ds(l0, LANES)]
    return acc


def _trunk_prompt_kernel(x_ref, gmix_ref, win_ref, convw_ref, convb_ref, lncg_ref, lncb_ref, lnvg_ref, lnvb_ref,
                         wsg_ref, bsg_ref, wout_ref, gx_ref, wxq_ref, kt_ref, v_ref, wxo_ref, gffn_ref, wr_ref,
                         br_ref,
                         x2_ref, h3_ref, rinfo_ref, hist_ref, cnt_ref,
                         ext_ref, conv_ref, run_ref):
    i = pl.program_id(0)

    @pl.when(i == 0)
    def _():
        ext_ref[0:HALO, :] = jnp.zeros((HALO, D_CONV), F32)
        run_ref[...] = jnp.zeros((1, LOGIT_LANES), F32)

    x = x_ref[...]
    h = _rms(x, gmix_ref[...]).astype(BF16)

    a_in = _dot(h, win_ref[:, 0:D_CONV])
    a_gate = _dot(h, win_ref[:, D_CONV:2 * D_CONV])
    ext_ref[HALO:HALO + TM, :] = a_in * _sigmoid(a_gate)

    def conv_body(c, carry):
        r0 = pl.multiple_of((c // 4) * CONV_ROWS, CONV_ROWS)
        l0 = pl.multiple_of((c % 4) * LANES, LANES)
        win = ext_ref[pl.ds(r0, CONV_ROWS + HALO), pl.ds(l0, LANES)]
        conv_ref[pl.ds(r0, CONV_ROWS), pl.ds(l0, LANES)] = _conv_taps(win, convw_ref, l0, CONV_ROWS)
        return carry

    lax.fori_loop(0, (TM // CONV_ROWS) * (D_CONV // LANES), conv_body, 0)
    hist_ref[...] = ext_ref[TM:TM + HALO, :]
    ext_ref[0:HALO, :] = ext_ref[TM:TM + HALO, :]

    y = _ln(conv_ref[...] + convb_ref[...], lncg_ref[...], lncb_ref[...])
    a_out = (y * _sigmoid(y)).astype(BF16)

    u = _dot(h, win_ref[:, 2 * D_CONV:2 * D_CONV + D_SG])
    v = _ln(_dot(h, win_ref[:, 2 * D_CONV + D_SG:]), lnvg_ref[...], lnvb_ref[...]).astype(BF16)
    ri = lax.broadcasted_iota(jnp.int32, (SG_CHUNK, SG_CHUNK), 0)
    ci = lax.broadcasted_iota(jnp.int32, (SG_CHUNK, SG_CHUNK), 1)
    w_tril = [jnp.where(ci <= ri, wsg_ref[hh], 0.0).astype(BF16) for hh in range(SG_HEADS)]
    gate_rows = []
    for c in range(TM // SG_CHUNK):
        rs = slice(c * SG_CHUNK, (c + 1) * SG_CHUNK)
        heads = [_dot(w_tril[hh], v[rs, hh * SG_HEAD_DIM:(hh + 1) * SG_HEAD_DIM]) for hh in range(SG_HEADS)]
        gate_rows.append(jnp.concatenate(heads, axis=1) + bsg_ref[...])
    b_out = (u * jnp.concatenate(gate_rows, axis=0)).astype(BF16)

    x1 = x + _dot(a_out, wout_ref[0:D_CONV, :]) + _dot(b_out, wout_ref[D_CONV:, :])

    hx = _rms(x1, gx_ref[...]).astype(BF16)
    q = _dot(hx, wxq_ref[...]).astype(BF16)
    x2 = x1 + _dot(_attn_heads(q, kt_ref[...], v_ref[...]), wxo_ref[...])
    x2_ref[...] = x2

    h3 = _rms(x2, gffn_ref[...]).astype(BF16)
    h3_ref[...] = _pack_bf16_pairs(h3.astype(F32))
    rinfo, new_run = _route(h3, wr_ref, br_ref, run_ref[...])
    rinfo_ref[...] = rinfo
    run_ref[...] = new_run
    cnt_ref[...] = new_run


def _const_spec(shape):
    nd = len(shape)
    return pl.BlockSpec(shape, lambda i: (0,) * nd, pipeline_mode=pl.Buffered(1))


def _trunk_prompt(x, p):
    n = x.shape[0]
    assert n % TM == 0
    row = lambda w: pl.BlockSpec((TM, w), lambda i: (i, 0))
    consts = [p["g_mix"], p["w_in"], p["conv_w"], p["conv_b"], p["ln_conv_g"], p["ln_conv_b"], p["ln_v_g"],
              p["ln_v_b"], p["w_sg"], p["b_sg_rows"], p["w_out"], p["g_xattn"], p["w_xq"], p["kt"], p["v"],
              p["w_xo"], p["g_ffn"], p["w_router"], p["b_router"]]
    return pl.pallas_call(
        _trunk_prompt_kernel,
        grid=(n // TM,),
        in_specs=[row(D_MODEL)] + [_const_spec(c.shape) for c in consts],
        out_specs=(row(D_MODEL), row(D_MODEL // 2), row(LOGIT_LANES),
                   pl.BlockSpec((HALO, D_CONV), lambda i: (0, 0)),
                   pl.BlockSpec((1, LOGIT_LANES), lambda i: (0, 0))),
        out_shape=(jax.ShapeDtypeStruct((n, D_MODEL), F32),
                   jax.ShapeDtypeStruct((n, D_MODEL // 2), jnp.uint32),
                   jax.ShapeDtypeStruct((n, LOGIT_LANES), F32),
                   jax.ShapeDtypeStruct((HALO, D_CONV), F32),
                   jax.ShapeDtypeStruct((1, LOGIT_LANES), F32)),
        scratch_shapes=[pltpu.VMEM((TM + HALO, D_CONV), F32),
                        pltpu.VMEM((TM, D_CONV), F32),
                        pltpu.VMEM((1, LOGIT_LANES), F32)],
        compiler_params=pltpu.CompilerParams(dimension_semantics=("arbitrary",), vmem_limit_bytes=VMEM_LIMIT),
        name="trunk_prompt",
    )(x, *consts)


def _trunk_sample_kernel(n_batch, t_len,
                         x_ref, hist_in_ref, run_in_ref, gmix_ref, win_ref, convw_ref, convb_ref, lncg_ref, lncb_ref,
                         lnvg_ref, lnvb_ref, wsgbd_ref, bsg_ref, wout_ref, gx_ref, wxq_ref, kt_ref, v_ref, wxo_ref,
                         gffn_ref, wr_ref, br_ref,
                         x2_ref, h3_ref, rinfo_ref, hist_ref, sgv_ref, cnt_ref,
                         ext_ref, conv_ref, att_ref):
    x = x_ref[...]
    h = _rms(x, gmix_ref[...]).astype(BF16)
    z = _dot(h, win_ref[...])
    a = z[:, 0:D_CONV] * _sigmoid(z[:, D_CONV:2 * D_CONV])
    ext_len = HIST + t_len
    for b in range(n_batch):
        ext_ref[b, 0:HIST, :] = hist_in_ref[b]
        ext_ref[b, HIST:ext_len, :] = a[b * t_len:(b + 1) * t_len, :]
    for b in range(n_batch):
        acc = jnp.zeros((t_len, D_CONV), F32)
        for k in range(CONV_WIDTH):
            acc = acc + ext_ref[b, k:k + t_len, :] * convw_ref[k:k + 1, :]
        conv_ref[b * t_len:(b + 1) * t_len, :] = acc
        hist_ref[b] = ext_ref[b, ext_len - HIST:ext_len, :]

    y = _ln(conv_ref[...] + convb_ref[...], lncg_ref[...], lncb_ref[...])
    a_out = (y * _sigmoid(y)).astype(BF16)

    u = z[:, 2 * D_CONV:2 * D_CONV + D_SG]
    v = _ln(z[:, 2 * D_CONV + D_SG:], lnvg_ref[...], lnvb_ref[...])
    sgv_ref[...] = v
    vb = v.astype(BF16)
    heads = [_dot(wsgbd_ref[hh], vb[:, hh * SG_HEAD_DIM:(hh + 1) * SG_HEAD_DIM]) for hh in range(SG_HEADS)]
    b_out = (u * (jnp.concatenate(heads, axis=1) + bsg_ref[...])).astype(BF16)

    x1 = x + _dot(a_out, wout_ref[0:D_CONV, :]) + _dot(b_out, wout_ref[D_CONV:, :])

    hx = _rms(x1, gx_ref[...]).astype(BF16)
    q = _dot(hx, wxq_ref[...]).astype(BF16)
    for b in range(n_batch):
        rs = slice(b * t_len, (b + 1) * t_len)
        att_ref[rs, :] = _attn_heads(q[rs, :], kt_ref[b], v_ref[b])
    x2 = x1 + _dot(att_ref[...], wxo_ref[...])
    x2_ref[...] = x2

    h3 = _rms(x2, gffn_ref[...]).astype(BF16)
    m = n_batch * t_len
    h3_ref[0:m, :] = _pack_bf16_pairs(h3.astype(F32))
    if h3_ref.shape[0] > m:
        h3_ref[m:, :] = jnp.zeros((h3_ref.shape[0] - m, D_MODEL // 2), jnp.uint32)
    rinfo, new_run = _route(h3, wr_ref, br_ref, run_in_ref[...])
    rinfo_ref[...] = rinfo
    cnt_ref[...] = new_run


def _trunk_sample(x, hist, run, p, n_batch, t_len):
    m = n_batch * t_len
    args = [x, hist, run, p["g_mix"], p["w_in"], p["conv_w"], p["conv_b"], p["ln_conv_g"], p["ln_conv_b"],
            p["ln_v_g"], p["ln_v_b"], p["w_sg_bd"], p["b_sg_rows_s"], p["w_out"], p["g_xattn"], p["w_xq"],
            p["kt_s"], p["v_s"], p["w_xo"], p["g_ffn"], p["w_router"], p["b_router"]]
    return pl.pallas_call(
        functools.partial(_trunk_sample_kernel, n_batch, t_len),
        out_shape=(jax.ShapeDtypeStruct((m, D_MODEL), F32),
                   jax.ShapeDtypeStruct((-(-m // (SC_WORKERS * SUBLANES)) * SC_WORKERS * SUBLANES, D_MODEL // 2),
                                        jnp.uint32),
                   jax.ShapeDtypeStruct((m, LOGIT_LANES), F32),
                   jax.ShapeDtypeStruct((n_batch, HIST, D_CONV), F32),
                   jax.ShapeDtypeStruct((m, D_SG), F32),
                   jax.ShapeDtypeStruct((1, LOGIT_LANES), F32)),
        scratch_shapes=[pltpu.VMEM((n_batch, HIST + t_len, D_CONV), F32),
                        pltpu.VMEM((m, D_CONV), F32),
                        pltpu.VMEM((m, D_MODEL), BF16)],
        compiler_params=pltpu.CompilerParams(vmem_limit_bytes=VMEM_LIMIT),
        name="trunk_sample",
    )(*args)


def _sc_worker_id():
    return lax.axis_index("s") * SC_CORES + lax.axis_index("c")


def _sc_chunk(per_w, max_chunk):
    assert per_w % SUBLANES == 0 and max_chunk <= LANES
    return max(c for c in range(SUBLANES, max_chunk + 1, SUBLANES) if per_w % c == 0)


def _sc_gather_rows(table, idx, max_chunk):
    n_rows, d = idx.shape[0], table.shape[1]
    per_w = n_rows // SC_WORKERS
    assert per_w * SC_WORKERS == n_rows
    chunk = _sc_chunk(per_w, max_chunk)
    n_chunks = per_w // chunk
    mesh = plsc.VectorSubcoreMesh(core_axis_name="c", subcore_axis_name="s")

    @functools.partial(
        pl.kernel, mesh=mesh,
        out_type=jax.ShapeDtypeStruct((n_rows, d), table.dtype),
        scratch_types=[pltpu.VMEM((chunk,), jnp.int32), pltpu.VMEM((chunk,), jnp.int32),
                       pltpu.VMEM((chunk, d), table.dtype), pltpu.VMEM((chunk, d), table.dtype),
                       pltpu.SemaphoreType.DMA, pltpu.SemaphoreType.DMA,
                       pltpu.SemaphoreType.DMA, pltpu.SemaphoreType.DMA],
    )
    def gather(table_hbm, idx_hbm, out_hbm, idx0, idx1, rows0, rows1, gsem0, gsem1, wsem0, wsem1):
        base = _sc_worker_id() * per_w
        bufs = ((idx0, rows0, gsem0, wsem0), (idx1, rows1, gsem1, wsem1))

        def start_read(j, b):
            idx_v, rows_v, gsem, _ = bufs[b]
            pltpu.sync_copy(idx_hbm.at[pl.ds(base + j * chunk, chunk)], idx_v)
            pltpu.async_copy(table_hbm.at[idx_v], rows_v, gsem)

        def finish(j, b):
            idx_v, rows_v, gsem, wsem = bufs[b]
            pltpu.make_async_copy(table_hbm.at[idx_v], rows_v, gsem).wait()
            pltpu.async_copy(rows_v, out_hbm.at[pl.ds(base + j * chunk, chunk)], wsem).wait()

        start_read(0, 0)

        @pl.loop(0, n_chunks // 2)
        def _(jj):
            j0 = 2 * jj
            start_read(j0 + 1, 1)
            finish(j0, 0)

            @pl.when(j0 + 2 < n_chunks)
            def _():
                start_read(j0 + 2, 0)

            finish(j0 + 1, 1)

        if n_chunks % 2 == 1:
            finish(n_chunks - 1, 0)

    return gather(table, idx)


def _sc_scatter_rows2(tables, slots_a, slots_b, n_rows_out, max_chunk):
    d, dtype = tables[0].shape[1], tables[0].dtype
    plans = []
    for t in tables:
        per_w = t.shape[0] // SC_WORKERS
        assert per_w * SC_WORKERS == t.shape[0]
        chunk = _sc_chunk(per_w, max_chunk)
        plans.append((per_w, chunk, per_w // chunk))
    cmax = max(c for _, c, _ in plans)
    n_t = len(tables)
    mesh = plsc.VectorSubcoreMesh(core_axis_name="c", subcore_axis_name="s")

    scratch = []
    for _, chunk, _ in plans:
        for _ in range(2):
            scratch += [pltpu.VMEM((chunk,), jnp.int32), pltpu.VMEM((chunk,), jnp.int32)]
    scratch += [pltpu.VMEM((cmax, d), dtype), pltpu.VMEM((cmax, d), dtype)]
    scratch += [pltpu.SemaphoreType.DMA] * 4

    @functools.partial(pl.kernel, mesh=mesh, out_type=jax.ShapeDtypeStruct((n_rows_out, d), dtype),
                       scratch_types=scratch)
    def scatter(*refs):
        tab_hbm = refs[0:n_t]
        sa_hbm = refs[n_t:2 * n_t]
        sb_hbm = refs[2 * n_t:3 * n_t]
        out_hbm = refs[3 * n_t]
        sc = refs[3 * n_t + 1:]
        idx_refs = sc[:4 * n_t]
        rows = sc[4 * n_t:4 * n_t + 2]
        lsem = sc[4 * n_t + 2:4 * n_t + 4]
        ssem = sc[4 * n_t + 4:4 * n_t + 6]
        wid = _sc_worker_id()

        work = []
        for t, (per_w, chunk, n_chunks) in enumerate(plans):
            for j in range(n_chunks):
                work.append((t, wid * per_w + j * chunk, chunk))

        def parts(k):
            t, off, chunk = work[k]
            b = k % 2
            ia, ib = idx_refs[4 * t + 2 * b], idx_refs[4 * t + 2 * b + 1]
            rv = rows[b] if chunk == cmax else rows[b].at[pl.ds(0, chunk)]
            return t, off, chunk, b, ia, ib, rv

        def start_load(k):
            t, off, chunk, b, ia, ib, rv = parts(k)
            return (pltpu.async_copy(tab_hbm[t].at[pl.ds(off, chunk)], rv, lsem[b]),
                    pltpu.async_copy(sa_hbm[t].at[pl.ds(off, chunk)], ia, lsem[b]),
                    pltpu.async_copy(sb_hbm[t].at[pl.ds(off, chunk)], ib, lsem[b]))

        def start_scatter(k):
            t, off, chunk, b, ia, ib, rv = parts(k)
            return (pltpu.async_copy(rv, out_hbm.at[ia], ssem[b]), pltpu.async_copy(rv, out_hbm.at[ib], ssem[b]))

        loads = {0: start_load(0)}
        scatters = {}
        for k in range(len(work)):
            for c in loads.pop(k):
                c.wait()
            scatters[k] = start_scatter(k)
            if k >= 1:
                for c in scatters.pop(k - 1):
                    c.wait()
            if k + 1 < len(work):
                loads[k + 1] = start_load(k + 1)
        for c in scatters.pop(len(work) - 1):
            c.wait()

    return scatter(*tables, *slots_a, *slots_b)


def _experts_kernel(first_ref, nblk_ref, cnt_ref, xs_hbm, wg_ref, wu_ref, wd_ref, ys_hbm,
                    xbuf, ybuf, wg_bf, wu_bf, wd_bf, in_sem, out_sem):
    e = pl.program_id(0)
    nb = nblk_ref[e]
    first = first_ref[e]
    cnt = cnt_ref[e]
    half = D_MODEL // 2

    def in_copy(j, slot):
        return pltpu.make_async_copy(xs_hbm.at[pl.ds((first + j) * BM, BM)], xbuf.at[slot], in_sem.at[slot])

    def out_copy(j, slot):
        return pltpu.make_async_copy(ybuf.at[slot], ys_hbm.at[pl.ds((first + j) * BM, BM)], out_sem.at[slot])

    @pl.when(nb > 0)
    def _():
        in_copy(0, 0).start()
        wg_bf[...] = wg_ref[0].astype(BF16)
        wu_bf[...] = wu_ref[0].astype(BF16)
        wd_bf[...] = wd_ref[0].astype(BF16)

        def block(j, carry):
            slot = lax.rem(j, 2)

            @pl.when(j + 1 < nb)
            def _():
                in_copy(j + 1, 1 - slot).start()

            in_copy(j, slot).wait()

            @pl.when(j >= 2)
            def _():
                out_copy(j - 2, slot).wait()

            live = lax.broadcasted_iota(jnp.int32, (BM, half), 0) < cnt - j * BM
            lo, hi = _unpack_bf16_pairs(jnp.where(live, xbuf[slot], jnp.uint32(0)))
            g = _dot(lo, wg_bf[0:half, :]) + _dot(hi, wg_bf[half:, :])
            u = _dot(lo, wu_bf[0:half, :]) + _dot(hi, wu_bf[half:, :])
            hm = (g * _sigmoid(g) * u).astype(BF16)
            y = _dot(hm, wd_bf[...])
            ybuf[slot] = _pack_bf16_pairs(y.astype(BF16).astype(F32))
            out_copy(j, slot).start()
            return carry

        lax.fori_loop(0, nb, block, 0)

        @pl.when(nb >= 2)
        def _():
            out_copy(nb - 2, lax.rem(nb, 2)).wait()

        out_copy(nb - 1, lax.rem(nb - 1, 2)).wait()


def _experts(xs, n_rows_out, first_block, n_blocks_e, counts, w_eg, w_eu, w_ed):
    w_map = lambda e, fb, nb, ct: (e, 0, 0)
    half = D_MODEL // 2
    return pl.pallas_call(
        _experts_kernel,
        grid_spec=pltpu.PrefetchScalarGridSpec(
            num_scalar_prefetch=3,
            grid=(N_EXPERTS,),
            in_specs=[pl.BlockSpec(memory_space=pl.ANY),
                      pl.BlockSpec((1, D_MODEL, D_EXPERT), w_map),
                      pl.BlockSpec((1, D_MODEL, D_EXPERT), w_map),
                      pl.BlockSpec((1, D_EXPERT, D_MODEL), w_map)],
            out_specs=pl.BlockSpec(memory_space=pl.ANY),
            scratch_shapes=[pltpu.VMEM((2, BM, half), jnp.uint32), pltpu.VMEM((2, BM, half), jnp.uint32),
                            pltpu.VMEM((D_MODEL, D_EXPERT), BF16), pltpu.VMEM((D_MODEL, D_EXPERT), BF16),
                            pltpu.VMEM((D_EXPERT, D_MODEL), BF16),
                            pltpu.SemaphoreType.DMA((2,)), pltpu.SemaphoreType.DMA((2,))]),
        out_shape=jax.ShapeDtypeStruct((n_rows_out, half), jnp.uint32),
        compiler_params=pltpu.CompilerParams(dimension_semantics=("arbitrary",), vmem_limit_bytes=VMEM_LIMIT),
        name="experts",
    )(first_block, n_blocks_e, counts, xs, w_eg, w_eu, w_ed)


def _combine_kernel(x2_ref, y1_ref, y2_ref, rinfo_ref, g_ref, o_ref):
    r = rinfo_ref[...]
    g1, g2 = r[:, 2:3], r[:, 3:4]
    half = D_MODEL // 2
    y1_lo, y1_hi = _unpack_bf16_pairs_f32(y1_ref[...])
    y2_lo, y2_hi = _unpack_bf16_pairs_f32(y2_ref[...])
    x_lo = x2_ref[:, 0:half] + g1 * y1_lo + g2 * y2_lo
    x_hi = x2_ref[:, half:] + g1 * y1_hi + g2 * y2_hi
    ms = (jnp.sum(x_lo * x_lo, axis=-1, keepdims=True) + jnp.sum(x_hi * x_hi, axis=-1, keepdims=True)) / D_MODEL
    inv = lax.rsqrt(ms + EPS)
    o_ref[:, 0:half] = x_lo * inv * g_ref[:, 0:half]
    o_ref[:, half:] = x_hi * inv * g_ref[:, half:]


def _combine(x2, yg, rinfo, g_final, tm, blk1, blk2):
    n = x2.shape[0]
    return pl.pallas_call(
        _combine_kernel,
        grid=(n // tm,),
        in_specs=[pl.BlockSpec((tm, D_MODEL), lambda i: (i, 0)),
                  pl.BlockSpec((tm, D_MODEL // 2), lambda i: (blk1 + i, 0)),
                  pl.BlockSpec((tm, D_MODEL // 2), lambda i: (blk2 + i, 0)),
                  pl.BlockSpec((tm, LOGIT_LANES), lambda i: (i, 0)),
                  pl.BlockSpec((1, D_MODEL), lambda i: (0, 0))],
        out_specs=pl.BlockSpec((tm, D_MODEL), lambda i: (i, 0)),
        out_shape=jax.ShapeDtypeStruct((n, D_MODEL), F32),
        compiler_params=pltpu.CompilerParams(dimension_semantics=("arbitrary",), vmem_limit_bytes=VMEM_LIMIT),
        name="combine",
    )(x2, yg, yg, rinfo, g_final)


def _gather_rows(table, idx):
    return _sc_gather_rows(table, idx, GATHER_CHUNK)


def _scatter_rows2(tables, slots_a, slots_b, n_rows_out):
    return _sc_scatter_rows2(tables, slots_a, slots_b, n_rows_out, GATHER_CHUNK)


def kernel(x_prompt, x_sample, mem_prompt, state_conv, cache_mem_k, cache_mem_v, g_mix, w_in, conv_w, conv_b, ln_conv_g, ln_conv_b, ln_v_g, ln_v_b, w_sg, b_sg, w_out, g_mem, w_mk, w_mv, g_xattn, w_xq, w_xo, g_ffn, w_router_group, b_router_group, w_router_expert, b_router_expert, w_expert_gate, w_expert_up, w_expert_down, g_final):
    assert x_prompt.shape[0] == 1 and g_mix.shape[0] == 1
    n_p = x_prompt.shape[1]
    n_batch, t_len = x_sample.shape[0], x_sample.shape[1]
    n_s = n_batch * t_len
    row = lambda a: a.reshape(1, -1)

    w_router = jnp.concatenate(
        [w_router_group[0], jnp.transpose(w_router_expert[0], (1, 0, 2)).reshape(D_MODEL, N_EXPERTS)], axis=1)
    w_router = jnp.pad(w_router, ((0, 0), (0, LOGIT_LANES - w_router.shape[1]))).astype(BF16)
    b_router = jnp.pad(jnp.concatenate([b_router_group[0], b_router_expert[0].reshape(-1)]),
                       (0, LOGIT_LANES - N_GROUPS - N_EXPERTS)).reshape(1, LOGIT_LANES)
    tril_t = jnp.tril(jnp.ones((t_len, t_len), bool))
    w_sg_t = jnp.where(tril_t, w_sg[0][:, :t_len, :t_len], 0.0)
    eye_b = jnp.eye(n_batch, dtype=F32)
    w_sg_bd = jnp.einsum("ab,hij->haibj", eye_b, w_sg_t).reshape(SG_HEADS, n_s, n_s).astype(BF16)
    p = {
        "g_mix": row(g_mix[0]), "w_in": w_in[0].astype(BF16),
        "conv_w": jnp.pad(conv_w[0], ((0, 1), (0, 0))), "conv_b": row(conv_b[0]),
        "ln_conv_g": row(ln_conv_g[0]), "ln_conv_b": row(ln_conv_b[0]),
        "ln_v_g": row(ln_v_g[0]), "ln_v_b": row(ln_v_b[0]),
        "w_sg": w_sg[0],
        "b_sg_rows": jnp.repeat(b_sg[0].T, SG_HEAD_DIM, axis=1),
        "w_sg_bd": w_sg_bd,
        "b_sg_rows_s": jnp.tile(jnp.repeat(b_sg[0][:, :t_len].T, SG_HEAD_DIM, axis=1), (n_batch, 1)),
        "w_out": w_out[0].astype(BF16), "g_xattn": row(g_xattn[0]),
        "w_xq": w_xq[0].astype(BF16), "w_xo": w_xo[0].astype(BF16), "g_ffn": row(g_ffn[0]),
        "w_router": w_router, "b_router": b_router,
    }

    k_p, v_p = _memkv(mem_prompt[0], row(g_mem[0]), w_mk[0].astype(BF16), w_mv[0].astype(BF16))
    p["kt"] = k_p.T.astype(BF16)
    p["v"] = v_p.astype(BF16)
    p["kt_s"] = jnp.transpose(cache_mem_k[0].reshape(n_batch, N_MEM, D_MODEL), (0, 2, 1)).astype(BF16)
    p["v_s"] = cache_mem_v[0].reshape(n_batch, N_MEM, D_MODEL).astype(BF16)

    x2_p, h3_p, rinfo_p, hist_p, cnt_p = _trunk_prompt(x_prompt[0], p)
    x2_s, h3_s, rinfo_s, hist_s, sgv_s, cnt = _trunk_sample(
        x_sample.reshape(n_s, D_MODEL), state_conv[0], cnt_p, p, n_batch, t_len)

    n_tot = n_p + n_s
    n_slots = -(-(n_tot * 2) // BM) * BM + N_EXPERTS * BM
    n_blocks = n_slots // BM
    counts = cnt[0, :N_EXPERTS].astype(jnp.int32)
    padded = (counts + BM - 1) // BM * BM
    pad_end = jnp.cumsum(padded)
    pad_start = pad_end - padded
    experts = jnp.arange(N_EXPERTS, dtype=jnp.int32)

    def slots_of(rinfo8):
        e12 = rinfo8[:, 0:2].astype(jnp.int32)
        rank12 = rinfo8[:, 4:6].astype(jnp.int32)
        start12 = jnp.sum(jnp.where(e12[:, :, None] == experts, pad_start, 0), axis=-1)
        return start12 + rank12

    slot_p = slots_of(rinfo_p[:, :8])
    slot_s = slots_of(rinfo_s[:n_s, :8])

    n_spare = h3_s.shape[0] - n_s
    spare = n_slots + jnp.arange(n_spare, dtype=jnp.int32)
    xs = _scatter_rows2(
        (h3_p, h3_s),
        (slot_p[:, 0], jnp.concatenate([slot_s[:, 0], spare])),
        (slot_p[:, 1], jnp.concatenate([slot_s[:, 1], spare + n_spare])),
        n_slots + 2 * n_spare)
    ys = _experts(xs, n_slots, pad_start // BM, padded // BM, counts,
                  w_expert_gate[0], w_expert_up[0], w_expert_down[0])
    back_idx = jnp.concatenate([slot_p[:, 0], slot_p[:, 1], slot_s[:, 0], slot_s[:, 1]])
    n_back = -(-back_idx.shape[0] // (SC_WORKERS * GATHER_CHUNK)) * (SC_WORKERS * GATHER_CHUNK)
    yg = _gather_rows(ys, jnp.pad(back_idx, (0, n_back - back_idx.shape[0])))

    y_p = _combine(x2_p, yg, rinfo_p, row(g_final), TM, 0, n_p // TM)
    y_s = _combine(x2_s, yg, rinfo_s, row(g_final), n_s, 2 * n_p // n_s, 2 * n_p // n_s + 1)

    return (y_p.reshape(1, n_p, D_MODEL),
            y_s.reshape(n_batch, t_len, D_MODEL),
            hist_p[HALO - HIST:].reshape(1, 1, HIST, D_CONV),
            hist_s.reshape(1, n_batch, HIST, D_CONV),
            k_p.reshape(1, 1, N_MEM, X_HEADS, X_HEAD_DIM),
            v_p.reshape(1, 1, N_MEM, X_HEADS, X_HEAD_DIM),
            sgv_s.reshape(1, n_batch, t_len, D_SG))
```

```python
import functools

import jax
import jax.numpy as jnp
from jax import lax
from jax.experimental import pallas as pl
from jax.experimental.pallas import tpu as pltpu
from jax.experimental.pallas import tpu_sc as plsc

D_MODEL = 1024
D_CONV = 512
D_SG = 512
CONV_WIDTH = 31
HIST = CONV_WIDTH - 1
SG_HEADS = 4
SG_HEAD_DIM = 128
SG_CHUNK = 128
N_MEM = 256
X_HEADS = 4
X_HEAD_DIM = 256
N_GROUPS = 4
EXPERTS_PER_GROUP = 8
N_EXPERTS = 32
D_EXPERT = 512
EPS = 1e-6

LANES = 128
SUBLANES = 8
SC_CORES = 2
SC_SUBCORES = 16
SC_WORKERS = SC_CORES * SC_SUBCORES
VMEM_LIMIT = 56 * 1024 * 1024

TM = 512
HALO = 32
CONV_ROWS = 128
BM = 256
X_LOOKAHEAD = 3
X_BUFS = X_LOOKAHEAD + 1
Y_BUFS = 3
ROW_DMA_PRIORITY = 1
GATHER_ROWS = 32
GATHER_BUFS = 6
SCATTER_CHUNK = 64
LOGIT_LANES = 128

F32 = jnp.float32
BF16 = jnp.bfloat16


def _dot(a, b):
    return jnp.dot(a, b, preferred_element_type=F32)


def _rms(x, g):
    return x * lax.rsqrt(jnp.mean(x * x, axis=-1, keepdims=True) + EPS) * g


def _ln(x, g, b):
    mu = jnp.mean(x, axis=-1, keepdims=True)
    xc = x - mu
    var = jnp.mean(xc * xc, axis=-1, keepdims=True)
    return xc * lax.rsqrt(var + EPS) * g + b


def _sigmoid(x):
    return 1.0 / (1.0 + jnp.exp(-x))


def _pack_bf16_pairs(h):
    bits = lax.bitcast_convert_type(h, jnp.uint32)
    half = h.shape[1] // 2
    lo = lax.shift_right_logical(bits[:, :half], jnp.uint32(16))
    hi = bits[:, half:] & jnp.uint32(0xFFFF0000)
    return hi | lo


def _unpack_bf16_pairs_f32(p):
    lo = lax.bitcast_convert_type(lax.shift_left(p, jnp.uint32(16)), F32)
    hi = lax.bitcast_convert_type(p & jnp.uint32(0xFFFF0000), F32)
    return lo, hi


def _unpack_bf16_pairs(p):
    lo, hi = _unpack_bf16_pairs_f32(p)
    return lo.astype(BF16), hi.astype(BF16)


def _memkv_kernel(mem_ref, g_ref, wk_ref, wv_ref, k_ref, v_ref):
    m = _rms(mem_ref[...], g_ref[...]).astype(BF16)
    k_ref[...] = _dot(m, wk_ref[...])
    v_ref[...] = _dot(m, wv_ref[...])


def _memkv(mem, g_mem, w_mk, w_mv):
    return pl.pallas_call(
        _memkv_kernel,
        out_shape=(jax.ShapeDtypeStruct((N_MEM, D_MODEL), F32), jax.ShapeDtypeStruct((N_MEM, D_MODEL), F32)),
        compiler_params=pltpu.CompilerParams(vmem_limit_bytes=VMEM_LIMIT),
        name="memkv",
    )(mem, g_mem, w_mk, w_mv)


def _attn_heads(q, kt, v):
    outs = []
    for h in range(X_HEADS):
        sl = slice(h * X_HEAD_DIM, (h + 1) * X_HEAD_DIM)
        s = _dot(q[:, sl], kt[sl, :]) * (X_HEAD_DIM ** -0.5)
        s = s - jnp.max(s, axis=-1, keepdims=True)
        p = jnp.exp(s)
        p = p / jnp.sum(p, axis=-1, keepdims=True)
        outs.append(_dot(p.astype(BF16), v[:, sl]).astype(BF16))
    return jnp.concatenate(outs, axis=1)


def _route(h3, wr_ref, br_ref, run):
    m = h3.shape[0]
    logits = _dot(h3, wr_ref[...]) + br_ref[...]
    lane = lax.broadcasted_iota(jnp.int32, (m, LOGIT_LANES), 1).astype(F32)
    neg = jnp.float32(-jnp.inf)
    big = jnp.float32(LOGIT_LANES)

    def first_argmax(vals):
        mx = jnp.max(vals, axis=-1, keepdims=True)
        idx = jnp.min(jnp.where(vals == mx, lane, big), axis=-1, keepdims=True)
        return mx, idx

    lg = jnp.where(lane < N_GROUPS, logits, neg)
    g_max, g_idx = first_argmax(lg)
    g_w = 1.0 / jnp.sum(jnp.exp(lg - g_max), axis=-1, keepdims=True)

    lo = N_GROUPS + g_idx * EXPERTS_PER_GROUP
    le = jnp.where((lane >= lo) & (lane < lo + EXPERTS_PER_GROUP), logits, neg)
    v1, i1 = first_argmax(le)
    v2, i2 = first_argmax(jnp.where(lane == i1, neg, le))
    t = jnp.exp(v2 - v1)
    gate1 = g_w / (1.0 + t)
    gate2 = g_w * t / (1.0 + t)
    e1 = i1 - N_GROUPS
    e2 = i2 - N_GROUPS

    oh1 = (lane == e1).astype(F32)
    oh2 = (lane == e2).astype(F32)
    oh = oh1 + oh2
    row = lax.broadcasted_iota(jnp.int32, (m, m), 0)
    col = lax.broadcasted_iota(jnp.int32, (m, m), 1)
    strict_lower = (col < row).astype(BF16)
    before = _dot(strict_lower, oh.astype(BF16)) + run
    rank1 = jnp.sum(before * oh1, axis=-1, keepdims=True)
    rank2 = jnp.sum(before * oh2, axis=-1, keepdims=True)
    new_run = run + jnp.sum(oh, axis=0, keepdims=True)

    rinfo = jnp.where(lane == 0, e1,
            jnp.where(lane == 1, e2,
            jnp.where(lane == 2, gate1,
            jnp.where(lane == 3, gate2,
            jnp.where(lane == 4, rank1,
            jnp.where(lane == 5, rank2, 0.0))))))
    return rinfo, new_run


def _conv_taps(win, w_ref, l0, rows):
    n = win.shape[0]
    acc = jnp.zeros((rows, LANES), F32)
    for r in range(SUBLANES):
        d = HALO - HIST + r
        if d % SUBLANES == 0:
            e, base = win, d
        else:
            e, base = pltpu.roll(win, n - d, axis=0), 0
        for qq in range(-(-CONV_WIDTH // SUBLANES)):
            k = SUBLANES * qq + r
            if k < CONV_WIDTH:
                s0 = base + SUBLANES * qq
                acc = acc + e[s0:s0 + rows, :] * w_ref[pl.ds(k, 1), pl.ds(l0, LANES)]
    return acc


def _trunk_prompt_kernel(x_ref, gmix_ref, win_ref, convw_ref, convb_ref, lncg_ref, lncb_ref, lnvg_ref, lnvb_ref,
                         wsg_ref, bsg_ref, wout_ref, gx_ref, wxq_ref, kt_ref, v_ref, wxo_ref, gffn_ref, wr_ref,
                         br_ref,
                         x2_ref, h3_ref, rinfo_ref, hist_ref, cnt_ref,
                         ext_ref, conv_ref, run_ref):
    i = pl.program_id(0)

    @pl.when(i == 0)
    def _():
        ext_ref[0:HALO, :] = jnp.zeros((HALO, D_CONV), F32)
        run_ref[...] = jnp.zeros((1, LOGIT_LANES), F32)

    x = x_ref[...]
    h = _rms(x, gmix_ref[...]).astype(BF16)

    a_in = _dot(h, win_ref[:, 0:D_CONV])
    a_gate = _dot(h, win_ref[:, D_CONV:2 * D_CONV])
    ext_ref[HALO:HALO + TM, :] = a_in * _sigmoid(a_gate)

    for r0 in range(0, TM, CONV_ROWS):
        for l0 in range(0, D_CONV, LANES):
            win = ext_ref[r0:r0 + CONV_ROWS + HALO, l0:l0 + LANES]
            conv_ref[r0:r0 + CONV_ROWS, l0:l0 + LANES] = _conv_taps(win, convw_ref, l0, CONV_ROWS)
    hist_ref[...] = ext_ref[TM:TM + HALO, :]
    ext_ref[0:HALO, :] = ext_ref[TM:TM + HALO, :]

    y = _ln(conv_ref[...] + convb_ref[...], lncg_ref[...], lncb_ref[...])
    a_out = (y * _sigmoid(y)).astype(BF16)

    u = _dot(h, win_ref[:, 2 * D_CONV:2 * D_CONV + D_SG])
    v = _ln(_dot(h, win_ref[:, 2 * D_CONV + D_SG:]), lnvg_ref[...], lnvb_ref[...]).astype(BF16)
    ri = lax.broadcasted_iota(jnp.int32, (SG_CHUNK, SG_CHUNK), 0)
    ci = lax.broadcasted_iota(jnp.int32, (SG_CHUNK, SG_CHUNK), 1)
    w_tril = [jnp.where(ci <= ri, wsg_ref[hh], 0.0).astype(BF16) for hh in range(SG_HEADS)]
    gate_rows = []
    for c in range(TM // SG_CHUNK):
        rs = slice(c * SG_CHUNK, (c + 1) * SG_CHUNK)
        heads = [_dot(w_tril[hh], v[rs, hh * SG_HEAD_DIM:(hh + 1) * SG_HEAD_DIM]) for hh in range(SG_HEADS)]
        gate_rows.append(jnp.concatenate(heads, axis=1) + bsg_ref[...])
    b_out = (u * jnp.concatenate(gate_rows, axis=0)).astype(BF16)

    x1 = x + _dot(a_out, wout_ref[0:D_CONV, :]) + _dot(b_out, wout_ref[D_CONV:, :])

    hx = _rms(x1, gx_ref[...]).astype(BF16)
    q = _dot(hx, wxq_ref[...]).astype(BF16)
    x2 = x1 + _dot(_attn_heads(q, kt_ref[...], v_ref[...]), wxo_ref[...])
    x2_ref[...] = x2

    h3 = _rms(x2, gffn_ref[...]).astype(BF16)
    h3_ref[...] = _pack_bf16_pairs(h3.astype(F32))
    rinfo, new_run = _route(h3, wr_ref, br_ref, run_ref[...])
    rinfo_ref[...] = rinfo
    run_ref[...] = new_run
    cnt_ref[...] = new_run


def _const_spec(shape):
    nd = len(shape)
    return pl.BlockSpec(shape, lambda i: (0,) * nd, pipeline_mode=pl.Buffered(1))


def _trunk_prompt(x, p):
    n = x.shape[0]
    assert n % TM == 0
    row = lambda w: pl.BlockSpec((TM, w), lambda i: (i, 0))
    consts = [p["g_mix"], p["w_in"], p["conv_w"], p["conv_b"], p["ln_conv_g"], p["ln_conv_b"], p["ln_v_g"],
              p["ln_v_b"], p["w_sg"], p["b_sg_rows"], p["w_out"], p["g_xattn"], p["w_xq"], p["kt"], p["v"],
              p["w_xo"], p["g_ffn"], p["w_router"], p["b_router"]]
    return pl.pallas_call(
        _trunk_prompt_kernel,
        grid=(n // TM,),
        in_specs=[row(D_MODEL)] + [_const_spec(c.shape) for c in consts],
        out_specs=(row(D_MODEL), row(D_MODEL // 2), row(LOGIT_LANES),
                   pl.BlockSpec((HALO, D_CONV), lambda i: (0, 0)),
                   pl.BlockSpec((1, LOGIT_LANES), lambda i: (0, 0))),
        out_shape=(jax.ShapeDtypeStruct((n, D_MODEL), F32),
                   jax.ShapeDtypeStruct((n, D_MODEL // 2), jnp.uint32),
                   jax.ShapeDtypeStruct((n, LOGIT_LANES), F32),
                   jax.ShapeDtypeStruct((HALO, D_CONV), F32),
                   jax.ShapeDtypeStruct((1, LOGIT_LANES), F32)),
        scratch_shapes=[pltpu.VMEM((TM + HALO, D_CONV), F32),
                        pltpu.VMEM((TM, D_CONV), F32),
                        pltpu.VMEM((1, LOGIT_LANES), F32)],
        compiler_params=pltpu.CompilerParams(dimension_semantics=("arbitrary",), vmem_limit_bytes=VMEM_LIMIT),
        name="trunk_prompt",
    )(x, *consts)


def _trunk_sample_kernel(n_batch, t_len,
                         x_ref, hist_in_ref, run_in_ref, gmix_ref, win_ref, convw_ref, convb_ref, lncg_ref, lncb_ref,
                         lnvg_ref, lnvb_ref, wsgbd_ref, bsg_ref, wout_ref, gx_ref, wxq_ref, kt_ref, v_ref, wxo_ref,
                         gffn_ref, wr_ref, br_ref,
                         x2_ref, h3_ref, rinfo_ref, hist_ref, sgv_ref, cnt_ref,
                         ext_ref, conv_ref, att_ref):
    x = x_ref[...]
    h = _rms(x, gmix_ref[...]).astype(BF16)
    z = _dot(h, win_ref[...])
    a = z[:, 0:D_CONV] * _sigmoid(z[:, D_CONV:2 * D_CONV])
    ext_len = HIST + t_len
    for b in range(n_batch):
        ext_ref[b, 0:HIST, :] = hist_in_ref[b]
        ext_ref[b, HIST:ext_len, :] = a[b * t_len:(b + 1) * t_len, :]
    for b in range(n_batch):
        acc = jnp.zeros((t_len, D_CONV), F32)
        for k in range(CONV_WIDTH):
            acc = acc + ext_ref[b, k:k + t_len, :] * convw_ref[k:k + 1, :]
        conv_ref[b * t_len:(b + 1) * t_len, :] = acc
        hist_ref[b] = ext_ref[b, ext_len - HIST:ext_len, :]

    y = _ln(conv_ref[...] + convb_ref[...], lncg_ref[...], lncb_ref[...])
    a_out = (y * _sigmoid(y)).astype(BF16)

    u = z[:, 2 * D_CONV:2 * D_CONV + D_SG]
    v = _ln(z[:, 2 * D_CONV + D_SG:], lnvg_ref[...], lnvb_ref[...])
    sgv_ref[...] = v
    vb = v.astype(BF16)
    heads = [_dot(wsgbd_ref[hh], vb[:, hh * SG_HEAD_DIM:(hh + 1) * SG_HEAD_DIM]) for hh in range(SG_HEADS)]
    b_out = (u * (jnp.concatenate(heads, axis=1) + bsg_ref[...])).astype(BF16)

    x1 = x + _dot(a_out, wout_ref[0:D_CONV, :]) + _dot(b_out, wout_ref[D_CONV:, :])

    hx = _rms(x1, gx_ref[...]).astype(BF16)
    q = _dot(hx, wxq_ref[...]).astype(BF16)
    for b in range(n_batch):
        rs = slice(b * t_len, (b + 1) * t_len)
        att_ref[rs, :] = _attn_heads(q[rs, :], kt_ref[b], v_ref[b])
    x2 = x1 + _dot(att_ref[...], wxo_ref[...])
    x2_ref[...] = x2

    h3 = _rms(x2, gffn_ref[...]).astype(BF16)
    m = n_batch * t_len
    h3_ref[0:m, :] = _pack_bf16_pairs(h3.astype(F32))
    if h3_ref.shape[0] > m:
        h3_ref[m:, :] = jnp.zeros((h3_ref.shape[0] - m, D_MODEL // 2), jnp.uint32)
    rinfo, new_run = _route(h3, wr_ref, br_ref, run_in_ref[...])
    rinfo_ref[...] = rinfo
    cnt_ref[...] = new_run


def _trunk_sample(x, hist, run, p, n_batch, t_len):
    m = n_batch * t_len
    args = [x, hist, run, p["g_mix"], p["w_in"], p["conv_w"], p["conv_b"], p["ln_conv_g"], p["ln_conv_b"],
            p["ln_v_g"], p["ln_v_b"], p["w_sg_bd"], p["b_sg_rows_s"], p["w_out"], p["g_xattn"], p["w_xq"],
            p["kt_s"], p["v_s"], p["w_xo"], p["g_ffn"], p["w_router"], p["b_router"]]
    return pl.pallas_call(
        functools.partial(_trunk_sample_kernel, n_batch, t_len),
        out_shape=(jax.ShapeDtypeStruct((m, D_MODEL), F32),
                   jax.ShapeDtypeStruct((-(-m // (SC_WORKERS * SUBLANES)) * SC_WORKERS * SUBLANES, D_MODEL // 2),
                                        jnp.uint32),
                   jax.ShapeDtypeStruct((m, LOGIT_LANES), F32),
                   jax.ShapeDtypeStruct((n_batch, HIST, D_CONV), F32),
                   jax.ShapeDtypeStruct((m, D_SG), F32),
                   jax.ShapeDtypeStruct((1, LOGIT_LANES), F32)),
        scratch_shapes=[pltpu.VMEM((n_batch, HIST + t_len, D_CONV), F32),
                        pltpu.VMEM((m, D_CONV), F32),
                        pltpu.VMEM((m, D_MODEL), BF16)],
        compiler_params=pltpu.CompilerParams(vmem_limit_bytes=VMEM_LIMIT),
        name="trunk_sample",
    )(*args)


def _sc_worker_id():
    return lax.axis_index("s") * SC_CORES + lax.axis_index("c")


def _sc_chunk(per_w, max_chunk):
    assert per_w % SUBLANES == 0 and max_chunk <= LANES
    return max(c for c in range(SUBLANES, max_chunk + 1, SUBLANES) if per_w % c == 0)


def _sc_gather_rows(table, idx):
    n_rows, d = idx.shape[0], table.shape[1]
    per_w = n_rows // SC_WORKERS
    assert per_w * SC_WORKERS == n_rows and per_w % GATHER_ROWS == 0
    n_chunks = per_w // GATHER_ROWS
    lag = GATHER_BUFS // 2
    mesh = plsc.VectorSubcoreMesh(core_axis_name="c", subcore_axis_name="s")

    @functools.partial(
        pl.kernel, mesh=mesh,
        out_type=jax.ShapeDtypeStruct((n_rows, d), table.dtype),
        scratch_types=([pltpu.VMEM((per_w,), jnp.int32)]
                       + [pltpu.VMEM((GATHER_ROWS, d), table.dtype)] * GATHER_BUFS
                       + [pltpu.SemaphoreType.DMA] * (2 * GATHER_BUFS)),
    )
    def gather(table_hbm, idx_hbm, out_hbm, idx_all, *rest):
        rows = rest[:GATHER_BUFS]
        gsem = rest[GATHER_BUFS:2 * GATHER_BUFS]
        wsem = rest[2 * GATHER_BUFS:]
        base = _sc_worker_id() * per_w
        pltpu.sync_copy(idx_hbm.at[pl.ds(base, per_w)], idx_all)

        reads, writes = {}, {}
        for c in range(n_chunks + lag):
            if c < n_chunks:
                b = c % GATHER_BUFS
                if c >= GATHER_BUFS:
                    writes.pop(c - GATHER_BUFS).wait()
                reads[c] = pltpu.async_copy(
                    table_hbm.at[idx_all.at[pl.ds(c * GATHER_ROWS, GATHER_ROWS)]], rows[b], gsem[b])
            w = c - lag
            if w >= 0:
                b = w % GATHER_BUFS
                reads.pop(w).wait()
                writes[w] = pltpu.async_copy(
                    rows[b], out_hbm.at[pl.ds(base + w * GATHER_ROWS, GATHER_ROWS)], wsem[b])
        for w in sorted(writes):
            writes[w].wait()

    return gather(table, idx)


def _sc_scatter_rows2(tables, slots_a, slots_b, n_rows_out, max_chunk):
    d, dtype = tables[0].shape[1], tables[0].dtype
    plans = []
    for t in tables:
        per_w = t.shape[0] // SC_WORKERS
        assert per_w * SC_WORKERS == t.shape[0]
        chunk = _sc_chunk(per_w, max_chunk)
        plans.append((per_w, chunk, per_w // chunk))
    cmax = max(c for _, c, _ in plans)
    n_t = len(tables)
    mesh = plsc.VectorSubcoreMesh(core_axis_name="c", subcore_axis_name="s")

    scratch = []
    for _, chunk, _ in plans:
        for _ in range(2):
            scratch += [pltpu.VMEM((chunk,), jnp.int32), pltpu.VMEM((chunk,), jnp.int32)]
    scratch += [pltpu.VMEM((cmax, d), dtype), pltpu.VMEM((cmax, d), dtype)]
    scratch += [pltpu.SemaphoreType.DMA] * 4

    @functools.partial(pl.kernel, mesh=mesh, out_type=jax.ShapeDtypeStruct((n_rows_out, d), dtype),
                       scratch_types=scratch)
    def scatter(*refs):
        tab_hbm = refs[0:n_t]
        sa_hbm = refs[n_t:2 * n_t]
        sb_hbm = refs[2 * n_t:3 * n_t]
        out_hbm = refs[3 * n_t]
        sc = refs[3 * n_t + 1:]
        idx_refs = sc[:4 * n_t]
        rows = sc[4 * n_t:4 * n_t + 2]
        lsem = sc[4 * n_t + 2:4 * n_t + 4]
        ssem = sc[4 * n_t + 4:4 * n_t + 6]
        wid = _sc_worker_id()

        work = []
        for t, (per_w, chunk, n_chunks) in enumerate(plans):
            for j in range(n_chunks):
                work.append((t, wid * per_w + j * chunk, chunk))

        def parts(k):
            t, off, chunk = work[k]
            b = k % 2
            ia, ib = idx_refs[4 * t + 2 * b], idx_refs[4 * t + 2 * b + 1]
            rv = rows[b] if chunk == cmax else rows[b].at[pl.ds(0, chunk)]
            return t, off, chunk, b, ia, ib, rv

        def start_load(k):
            t, off, chunk, b, ia, ib, rv = parts(k)
            return (pltpu.async_copy(tab_hbm[t].at[pl.ds(off, chunk)], rv, lsem[b]),
                    pltpu.async_copy(sa_hbm[t].at[pl.ds(off, chunk)], ia, lsem[b]),
                    pltpu.async_copy(sb_hbm[t].at[pl.ds(off, chunk)], ib, lsem[b]))

        def start_scatter(k):
            t, off, chunk, b, ia, ib, rv = parts(k)
            return (pltpu.async_copy(rv, out_hbm.at[ia], ssem[b]), pltpu.async_copy(rv, out_hbm.at[ib], ssem[b]))

        loads = {0: start_load(0)}
        scatters = {}
        for k in range(len(work)):
            for c in loads.pop(k):
                c.wait()
            scatters[k] = start_scatter(k)
            if k >= 1:
                for c in scatters.pop(k - 1):
                    c.wait()
            if k + 1 < len(work):
                loads[k + 1] = start_load(k + 1)
        for c in scatters.pop(len(work) - 1):
            c.wait()

    return scatter(*tables, *slots_a, *slots_b)


def _experts_kernel(first_ref, nblk_ref, cnt_ref, tot_ref, xs_hbm, wg_ref, wu_ref, wd_ref, ys_hbm,
                    xbuf, ybuf, wg_bf, wu_bf, wd_bf, in_sem, out_sem):
    e = pl.program_id(0)
    nb = nblk_ref[e]
    first = first_ref[e]
    cnt = cnt_ref[e]
    total = tot_ref[0]
    half = D_MODEL // 2

    def in_copy(gb):
        slot = lax.rem(gb, X_BUFS)
        return pltpu.make_async_copy(xs_hbm.at[pl.ds(gb * BM, BM)], xbuf.at[slot], in_sem.at[slot])

    def out_copy(gb):
        slot = lax.rem(gb, Y_BUFS)
        return pltpu.make_async_copy(ybuf.at[slot], ys_hbm.at[pl.ds(gb * BM, BM)], out_sem.at[slot])

    @pl.when(nb > 0)
    def _():
        @pl.when(first == 0)
        def _():
            for k in range(X_LOOKAHEAD):
                @pl.when(k < total)
                def _():
                    in_copy(k).start(priority=ROW_DMA_PRIORITY)

        wg_bf[...] = wg_ref[0].astype(BF16)
        wu_bf[...] = wu_ref[0].astype(BF16)
        wd_bf[...] = wd_ref[0].astype(BF16)

        def block(j, carry):
            gb = first + j

            @pl.when(gb + X_LOOKAHEAD < total)
            def _():
                in_copy(gb + X_LOOKAHEAD).start(priority=ROW_DMA_PRIORITY)

            in_copy(gb).wait()

            @pl.when(gb >= Y_BUFS)
            def _():
                out_copy(gb - Y_BUFS).wait()

            live = lax.broadcasted_iota(jnp.int32, (BM, half), 0) < cnt - j * BM
            lo, hi = _unpack_bf16_pairs(jnp.where(live, xbuf[lax.rem(gb, X_BUFS)], jnp.uint32(0)))
            g = _dot(lo, wg_bf[0:half, :]) + _dot(hi, wg_bf[half:, :])
            u = _dot(lo, wu_bf[0:half, :]) + _dot(hi, wu_bf[half:, :])
            hm = (g * _sigmoid(g) * u).astype(BF16)
            y = _dot(hm, wd_bf[...])
            ybuf[lax.rem(gb, Y_BUFS)] = _pack_bf16_pairs(y.astype(BF16).astype(F32))
            out_copy(gb).start(priority=ROW_DMA_PRIORITY)
            return carry

        lax.fori_loop(0, nb, block, 0)

        @pl.when(first + nb == total)
        def _():
            for k in range(Y_BUFS):
                @pl.when(total - 1 - k >= 0)
                def _():
                    out_copy(total - 1 - k).wait()


def _experts(xs, n_rows_out, first_block, n_blocks_e, counts, w_eg, w_eu, w_ed):
    w_map = lambda e, fb, nb, ct, tot: (e, 0, 0)
    half = D_MODEL // 2
    total = jnp.sum(n_blocks_e).astype(jnp.int32).reshape(1)
    return pl.pallas_call(
        _experts_kernel,
        grid_spec=pltpu.PrefetchScalarGridSpec(
            num_scalar_prefetch=4,
            grid=(N_EXPERTS,),
            in_specs=[pl.BlockSpec(memory_space=pl.ANY),
                      pl.BlockSpec((1, D_MODEL, D_EXPERT), w_map),
                      pl.BlockSpec((1, D_MODEL, D_EXPERT), w_map),
                      pl.BlockSpec((1, D_EXPERT, D_MODEL), w_map)],
            out_specs=pl.BlockSpec(memory_space=pl.ANY),
            scratch_shapes=[pltpu.VMEM((X_BUFS, BM, half), jnp.uint32), pltpu.VMEM((Y_BUFS, BM, half), jnp.uint32),
                            pltpu.VMEM((D_MODEL, D_EXPERT), BF16), pltpu.VMEM((D_MODEL, D_EXPERT), BF16),
                            pltpu.VMEM((D_EXPERT, D_MODEL), BF16),
                            pltpu.SemaphoreType.DMA((X_BUFS,)), pltpu.SemaphoreType.DMA((Y_BUFS,))]),
        out_shape=jax.ShapeDtypeStruct((n_rows_out, half), jnp.uint32),
        compiler_params=pltpu.CompilerParams(dimension_semantics=("arbitrary",), vmem_limit_bytes=VMEM_LIMIT),
        name="experts",
    )(first_block, n_blocks_e, counts, total, xs, w_eg, w_eu, w_ed)


def _combine_kernel(x2_ref, y1_ref, y2_ref, rinfo_ref, g_ref, o_ref):
    r = rinfo_ref[...]
    g1, g2 = r[:, 2:3], r[:, 3:4]
    half = D_MODEL // 2
    y1_lo, y1_hi = _unpack_bf16_pairs_f32(y1_ref[...])
    y2_lo, y2_hi = _unpack_bf16_pairs_f32(y2_ref[...])
    x_lo = x2_ref[:, 0:half] + g1 * y1_lo + g2 * y2_lo
    x_hi = x2_ref[:, half:] + g1 * y1_hi + g2 * y2_hi
    ms = (jnp.sum(x_lo * x_lo, axis=-1, keepdims=True) + jnp.sum(x_hi * x_hi, axis=-1, keepdims=True)) / D_MODEL
    inv = lax.rsqrt(ms + EPS)
    o_ref[:, 0:half] = x_lo * inv * g_ref[:, 0:half]
    o_ref[:, half:] = x_hi * inv * g_ref[:, half:]


def _combine(x2, yg, rinfo, g_final, tm, blk1, blk2):
    n = x2.shape[0]
    return pl.pallas_call(
        _combine_kernel,
        grid=(n // tm,),
        in_specs=[pl.BlockSpec((tm, D_MODEL), lambda i: (i, 0)),
                  pl.BlockSpec((tm, D_MODEL // 2), lambda i: (blk1 + i, 0)),
                  pl.BlockSpec((tm, D_MODEL // 2), lambda i: (blk2 + i, 0)),
                  pl.BlockSpec((tm, LOGIT_LANES), lambda i: (i, 0)),
                  pl.BlockSpec((1, D_MODEL), lambda i: (0, 0))],
        out_specs=pl.BlockSpec((tm, D_MODEL), lambda i: (i, 0)),
        out_shape=jax.ShapeDtypeStruct((n, D_MODEL), F32),
        compiler_params=pltpu.CompilerParams(dimension_semantics=("arbitrary",), vmem_limit_bytes=VMEM_LIMIT),
        name="combine",
    )(x2, yg, yg, rinfo, g_final)


def _gather_rows(table, idx):
    return _sc_gather_rows(table, idx)


def _scatter_rows2(tables, slots_a, slots_b, n_rows_out):
    return _sc_scatter_rows2(tables, slots_a, slots_b, n_rows_out, SCATTER_CHUNK)


def kernel(x_prompt, x_sample, mem_prompt, state_conv, cache_mem_k, cache_mem_v, g_mix, w_in, conv_w, conv_b, ln_conv_g, ln_conv_b, ln_v_g, ln_v_b, w_sg, b_sg, w_out, g_mem, w_mk, w_mv, g_xattn, w_xq, w_xo, g_ffn, w_router_group, b_router_group, w_router_expert, b_router_expert, w_expert_gate, w_expert_up, w_expert_down, g_final):
    assert x_prompt.shape[0] == 1 and g_mix.shape[0] == 1
    n_p = x_prompt.shape[1]
    n_batch, t_len = x_sample.shape[0], x_sample.shape[1]
    n_s = n_batch * t_len
    row = lambda a: a.reshape(1, -1)

    w_router = jnp.concatenate(
        [w_router_group[0], jnp.transpose(w_router_expert[0], (1, 0, 2)).reshape(D_MODEL, N_EXPERTS)], axis=1)
    w_router = jnp.pad(w_router, ((0, 0), (0, LOGIT_LANES - w_router.shape[1]))).astype(BF16)
    b_router = jnp.pad(jnp.concatenate([b_router_group[0], b_router_expert[0].reshape(-1)]),
                       (0, LOGIT_LANES - N_GROUPS - N_EXPERTS)).reshape(1, LOGIT_LANES)
    tril_t = jnp.tril(jnp.ones((t_len, t_len), bool))
    w_sg_t = jnp.where(tril_t, w_sg[0][:, :t_len, :t_len], 0.0)
    eye_b = jnp.eye(n_batch, dtype=F32)
    w_sg_bd = jnp.einsum("ab,hij->haibj", eye_b, w_sg_t).reshape(SG_HEADS, n_s, n_s).astype(BF16)
    p = {
        "g_mix": row(g_mix[0]), "w_in": w_in[0].astype(BF16),
        "conv_w": jnp.pad(conv_w[0], ((0, 1), (0, 0))), "conv_b": row(conv_b[0]),
        "ln_conv_g": row(ln_conv_g[0]), "ln_conv_b": row(ln_conv_b[0]),
        "ln_v_g": row(ln_v_g[0]), "ln_v_b": row(ln_v_b[0]),
        "w_sg": w_sg[0],
        "b_sg_rows": jnp.repeat(b_sg[0].T, SG_HEAD_DIM, axis=1),
        "w_sg_bd": w_sg_bd,
        "b_sg_rows_s": jnp.tile(jnp.repeat(b_sg[0][:, :t_len].T, SG_HEAD_DIM, axis=1), (n_batch, 1)),
        "w_out": w_out[0].astype(BF16), "g_xattn": row(g_xattn[0]),
        "w_xq": w_xq[0].astype(BF16), "w_xo": w_xo[0].astype(BF16), "g_ffn": row(g_ffn[0]),
        "w_router": w_router, "b_router": b_router,
    }

    k_p, v_p = _memkv(mem_prompt[0], row(g_mem[0]), w_mk[0].astype(BF16), w_mv[0].astype(BF16))
    p["kt"] = k_p.T.astype(BF16)
    p["v"] = v_p.astype(BF16)
    p["kt_s"] = jnp.transpose(cache_mem_k[0].reshape(n_batch, N_MEM, D_MODEL), (0, 2, 1)).astype(BF16)
    p["v_s"] = cache_mem_v[0].reshape(n_batch, N_MEM, D_MODEL).astype(BF16)

    x2_p, h3_p, rinfo_p, hist_p, cnt_p = _trunk_prompt(x_prompt[0], p)
    x2_s, h3_s, rinfo_s, hist_s, sgv_s, cnt = _trunk_sample(
        x_sample.reshape(n_s, D_MODEL), state_conv[0], cnt_p, p, n_batch, t_len)

    n_tot = n_p + n_s
    n_slots = -(-(n_tot * 2) // BM) * BM + N_EXPERTS * BM
    counts = cnt[0, :N_EXPERTS].astype(jnp.int32)
    padded = (counts + BM - 1) // BM * BM
    pad_end = jnp.cumsum(padded)
    pad_start = pad_end - padded
    experts = jnp.arange(N_EXPERTS, dtype=jnp.int32)

    def slots_of(rinfo8):
        e12 = rinfo8[:, 0:2].astype(jnp.int32)
        rank12 = rinfo8[:, 4:6].astype(jnp.int32)
        start12 = jnp.sum(jnp.where(e12[:, :, None] == experts, pad_start, 0), axis=-1)
        return start12 + rank12

    slot_p = slots_of(rinfo_p[:, :8])
    slot_s = slots_of(rinfo_s[:n_s, :8])

    n_spare = h3_s.shape[0] - n_s
    spare = n_slots + jnp.arange(n_spare, dtype=jnp.int32)
    xs = _scatter_rows2(
        (h3_p, h3_s),
        (slot_p[:, 0], jnp.concatenate([slot_s[:, 0], spare])),
        (slot_p[:, 1], jnp.concatenate([slot_s[:, 1], spare + n_spare])),
        n_slots + 2 * n_spare)
    ys = _experts(xs, n_slots, pad_start // BM, padded // BM, counts,
                  w_expert_gate[0], w_expert_up[0], w_expert_down[0])
    back_idx = jnp.concatenate([slot_p[:, 0], slot_p[:, 1], slot_s[:, 0], slot_s[:, 1]])
    n_back = -(-back_idx.shape[0] // (SC_WORKERS * GATHER_ROWS)) * (SC_WORKERS * GATHER_ROWS)
    yg = _gather_rows(ys, jnp.pad(back_idx, (0, n_back - back_idx.shape[0])))

    y_p = _combine(x2_p, yg, rinfo_p, row(g_final), TM, 0, n_p // TM)
    y_s = _combine(x2_s, yg, rinfo_s, row(g_final), n_s, 2 * n_p // n_s, 2 * n_p // n_s + 1)

    return (y_p.reshape(1, n_p, D_MODEL),
            y_s.reshape(n_batch, t_len, D_MODEL),
            hist_p[HALO - HIST:].reshape(1, 1, HIST, D_CONV),
            hist_s.reshape(1, n_batch, HIST, D_CONV),
            k_p.reshape(1, 1, N_MEM, X_HEADS, X_HEAD_DIM),
            v_p.reshape(1, 1, N_MEM, X_HEADS, X_HEAD_DIM),
            sgv_s.reshape(1, n_batch, t_len, D_SG))
```

```python
import functools

import jax
import jax.numpy as jnp
from jax import lax
from jax.experimental import pallas as pl
from jax.experimental.pallas import tpu as pltpu
from jax.experimental.pallas import tpu_sc as plsc

D_MODEL = 1024
D_CONV = 512
D_SG = 512
CONV_WIDTH = 31
HIST = CONV_WIDTH - 1
SG_HEADS = 4
SG_HEAD_DIM = 128
SG_CHUNK = 128
N_MEM = 256
X_HEADS = 4
X_HEAD_DIM = 256
N_GROUPS = 4
EXPERTS_PER_GROUP = 8
N_EXPERTS = 32
D_EXPERT = 512
EPS = 1e-6

LANES = 128
SUBLANES = 8
SC_CORES = 2
SC_SUBCORES = 16
SC_WORKERS = SC_CORES * SC_SUBCORES
VMEM_LIMIT = 56 * 1024 * 1024

TM = 512
HALO = 32
CONV_ROWS = 128
BM = 256
X_LOOKAHEAD = 3
X_BUFS = X_LOOKAHEAD + 1
Y_BUFS = 3
ROW_DMA_PRIORITY = 1
GATHER_ROWS = 32
GATHER_BUFS = 6
SCATTER_CHUNK = 64
LOGIT_LANES = 128

F32 = jnp.float32
BF16 = jnp.bfloat16


def _dot(a, b):
    return jnp.dot(a, b, preferred_element_type=F32)


def _rms(x, g):
    return x * lax.rsqrt(jnp.mean(x * x, axis=-1, keepdims=True) + EPS) * g


def _ln(x, g, b):
    mu = jnp.mean(x, axis=-1, keepdims=True)
    xc = x - mu
    var = jnp.mean(xc * xc, axis=-1, keepdims=True)
    return xc * lax.rsqrt(var + EPS) * g + b


def _sigmoid(x):
    return 1.0 / (1.0 + jnp.exp(-x))


def _pack_bf16_pairs(h):
    bits = lax.bitcast_convert_type(h, jnp.uint32)
    half = h.shape[1] // 2
    lo = lax.shift_right_logical(bits[:, :half], jnp.uint32(16))
    hi = bits[:, half:] & jnp.uint32(0xFFFF0000)
    return hi | lo


def _unpack_bf16_pairs_f32(p):
    lo = lax.bitcast_convert_type(lax.shift_left(p, jnp.uint32(16)), F32)
    hi = lax.bitcast_convert_type(p & jnp.uint32(0xFFFF0000), F32)
    return lo, hi


def _unpack_bf16_pairs(p):
    lo, hi = _unpack_bf16_pairs_f32(p)
    return lo.astype(BF16), hi.astype(BF16)


def _memkv_kernel(mem_ref, g_ref, wk_ref, wv_ref, k_ref, v_ref):
    m = _rms(mem_ref[...], g_ref[...]).astype(BF16)
    k_ref[...] = _dot(m, wk_ref[...])
    v_ref[...] = _dot(m, wv_ref[...])


def _memkv(mem, g_mem, w_mk, w_mv):
    return pl.pallas_call(
        _memkv_kernel,
        out_shape=(jax.ShapeDtypeStruct((N_MEM, D_MODEL), F32), jax.ShapeDtypeStruct((N_MEM, D_MODEL), F32)),
        compiler_params=pltpu.CompilerParams(vmem_limit_bytes=VMEM_LIMIT),
        name="memkv",
    )(mem, g_mem, w_mk, w_mv)


def _attn_heads(q, kt, v):
    outs = []
    for h in range(X_HEADS):
        sl = slice(h * X_HEAD_DIM, (h + 1) * X_HEAD_DIM)
        s = _dot(q[:, sl], kt[sl, :]) * (X_HEAD_DIM ** -0.5)
        s = s - jnp.max(s, axis=-1, keepdims=True)
        p = jnp.exp(s)
        p = p / jnp.sum(p, axis=-1, keepdims=True)
        outs.append(_dot(p.astype(BF16), v[:, sl]).astype(BF16))
    return jnp.concatenate(outs, axis=1)


def _route(logits, run):
    m = logits.shape[0]
    lane = lax.broadcasted_iota(jnp.int32, (m, LOGIT_LANES), 1).astype(F32)
    neg = jnp.float32(-jnp.inf)
    big = jnp.float32(LOGIT_LANES)

    def first_argmax(vals):
        mx = jnp.max(vals, axis=-1, keepdims=True)
        idx = jnp.min(jnp.where(vals == mx, lane, big), axis=-1, keepdims=True)
        return mx, idx

    lg = jnp.where(lane < N_GROUPS, logits, neg)
    g_max, g_idx = first_argmax(lg)
    g_w = 1.0 / jnp.sum(jnp.exp(lg - g_max), axis=-1, keepdims=True)

    lo = N_GROUPS + g_idx * EXPERTS_PER_GROUP
    le = jnp.where((lane >= lo) & (lane < lo + EXPERTS_PER_GROUP), logits, neg)
    v1, i1 = first_argmax(le)
    v2, i2 = first_argmax(jnp.where(lane == i1, neg, le))
    t = jnp.exp(v2 - v1)
    gate1 = g_w / (1.0 + t)
    gate2 = g_w * t / (1.0 + t)
    e1 = i1 - N_GROUPS
    e2 = i2 - N_GROUPS

    oh1 = (lane == e1).astype(F32)
    oh2 = (lane == e2).astype(F32)
    oh = oh1 + oh2
    row = lax.broadcasted_iota(jnp.int32, (m, m), 0)
    col = lax.broadcasted_iota(jnp.int32, (m, m), 1)
    strict_lower = (col < row).astype(BF16)
    before = _dot(strict_lower, oh.astype(BF16)) + run
    rank1 = jnp.sum(before * oh1, axis=-1, keepdims=True)
    rank2 = jnp.sum(before * oh2, axis=-1, keepdims=True)
    new_run = run + jnp.sum(oh, axis=0, keepdims=True)

    rinfo = jnp.where(lane == 0, e1,
            jnp.where(lane == 1, e2,
            jnp.where(lane == 2, gate1,
            jnp.where(lane == 3, gate2,
            jnp.where(lane == 4, rank1,
            jnp.where(lane == 5, rank2, 0.0))))))
    return rinfo, new_run


def _conv_taps(win, w_ref, l0, rows):
    n = win.shape[0]
    acc = jnp.zeros((rows, LANES), F32)
    for r in range(SUBLANES):
        d = HALO - HIST + r
        if d % SUBLANES == 0:
            e, base = win, d
        else:
            e, base = pltpu.roll(win, n - d, axis=0), 0
        for qq in range(-(-CONV_WIDTH // SUBLANES)):
            k = SUBLANES * qq + r
            if k < CONV_WIDTH:
                s0 = base + SUBLANES * qq
                acc = acc + e[s0:s0 + rows, :] * w_ref[pl.ds(k, 1), pl.ds(l0, LANES)]
    return acc


def _trunk_prompt_kernel(x_ref, gmix_ref, win_ref, convw_ref, convb_ref, lncg_ref, lncb_ref, lnvg_ref, lnvb_ref,
                         wsg_ref, bsg_ref, wout_ref, gx_ref, wxq_ref, kt_ref, v_ref, wxo_ref, gffn_ref, wr_ref,
                         br_ref,
                         x2_ref, h3_ref, rinfo_ref, rt_ref, hist_ref, cnt_ref,
                         ext_ref, conv_ref, run_ref):
    i = pl.program_id(0)

    @pl.when(i == 0)
    def _():
        ext_ref[0:HALO, :] = jnp.zeros((HALO, D_CONV), F32)
        run_ref[...] = jnp.zeros((1, LOGIT_LANES), F32)

    x = x_ref[...]
    h = _rms(x, gmix_ref[...]).astype(BF16)

    a_in = _dot(h, win_ref[:, 0:D_CONV])
    a_gate = _dot(h, win_ref[:, D_CONV:2 * D_CONV])
    ext_ref[HALO:HALO + TM, :] = a_in * _sigmoid(a_gate)

    for r0 in range(0, TM, CONV_ROWS):
        for l0 in range(0, D_CONV, LANES):
            win = ext_ref[r0:r0 + CONV_ROWS + HALO, l0:l0 + LANES]
            conv_ref[r0:r0 + CONV_ROWS, l0:l0 + LANES] = _conv_taps(win, convw_ref, l0, CONV_ROWS)
    hist_ref[...] = ext_ref[TM:TM + HALO, :]
    ext_ref[0:HALO, :] = ext_ref[TM:TM + HALO, :]

    y = _ln(conv_ref[...] + convb_ref[...], lncg_ref[...], lncb_ref[...])
    a_out = (y * _sigmoid(y)).astype(BF16)

    u = _dot(h, win_ref[:, 2 * D_CONV:2 * D_CONV + D_SG])
    v = _ln(_dot(h, win_ref[:, 2 * D_CONV + D_SG:]), lnvg_ref[...], lnvb_ref[...]).astype(BF16)
    ri = lax.broadcasted_iota(jnp.int32, (SG_CHUNK, SG_CHUNK), 0)
    ci = lax.broadcasted_iota(jnp.int32, (SG_CHUNK, SG_CHUNK), 1)
    w_tril = [jnp.where(ci <= ri, wsg_ref[hh], 0.0).astype(BF16) for hh in range(SG_HEADS)]
    gate_rows = []
    for c in range(TM // SG_CHUNK):
        rs = slice(c * SG_CHUNK, (c + 1) * SG_CHUNK)
        heads = [_dot(w_tril[hh], v[rs, hh * SG_HEAD_DIM:(hh + 1) * SG_HEAD_DIM]) for hh in range(SG_HEADS)]
        gate_rows.append(jnp.concatenate(heads, axis=1) + bsg_ref[...])
    b_out = (u * jnp.concatenate(gate_rows, axis=0)).astype(BF16)

    x1 = x + _dot(a_out, wout_ref[0:D_CONV, :]) + _dot(b_out, wout_ref[D_CONV:, :])

    hx = _rms(x1, gx_ref[...]).astype(BF16)
    q = _dot(hx, wxq_ref[...]).astype(BF16)
    x2 = x1 + _dot(_attn_heads(q, kt_ref[...], v_ref[...]), wxo_ref[...])
    x2_ref[...] = x2

    h3 = _rms(x2, gffn_ref[...]).astype(BF16)
    h3_ref[...] = _pack_bf16_pairs(h3.astype(F32))
    rinfo, new_run = _route(_dot(h3, wr_ref[...]) + br_ref[...], run_ref[...])
    rinfo_ref[...] = rinfo
    rt_ref[...] = jnp.transpose(rinfo)[0:SUBLANES, :]
    run_ref[...] = new_run
    cnt_ref[...] = new_run


def _const_spec(shape):
    nd = len(shape)
    return pl.BlockSpec(shape, lambda i: (0,) * nd, pipeline_mode=pl.Buffered(1))


def _trunk_prompt(x, p):
    n = x.shape[0]
    assert n % TM == 0
    row = lambda w: pl.BlockSpec((TM, w), lambda i: (i, 0))
    consts = [p["g_mix"], p["w_in"], p["conv_w"], p["conv_b"], p["ln_conv_g"], p["ln_conv_b"], p["ln_v_g"],
              p["ln_v_b"], p["w_sg"], p["b_sg_rows"], p["w_out"], p["g_xattn"], p["w_xq"], p["kt"], p["v"],
              p["w_xo"], p["g_ffn"], p["w_router"], p["b_router"]]
    return pl.pallas_call(
        _trunk_prompt_kernel,
        grid=(n // TM,),
        in_specs=[row(D_MODEL)] + [_const_spec(c.shape) for c in consts],
        out_specs=(row(D_MODEL), row(D_MODEL // 2), row(LOGIT_LANES),
                   pl.BlockSpec((SUBLANES, TM), lambda i: (0, i)),
                   pl.BlockSpec((HALO, D_CONV), lambda i: (0, 0)),
                   pl.BlockSpec((1, LOGIT_LANES), lambda i: (0, 0))),
        out_shape=(jax.ShapeDtypeStruct((n, D_MODEL), F32),
                   jax.ShapeDtypeStruct((n, D_MODEL // 2), jnp.uint32),
                   jax.ShapeDtypeStruct((n, LOGIT_LANES), F32),
                   jax.ShapeDtypeStruct((SUBLANES, n), F32),
                   jax.ShapeDtypeStruct((HALO, D_CONV), F32),
                   jax.ShapeDtypeStruct((1, LOGIT_LANES), F32)),
        scratch_shapes=[pltpu.VMEM((TM + HALO, D_CONV), F32),
                        pltpu.VMEM((TM, D_CONV), F32),
                        pltpu.VMEM((1, LOGIT_LANES), F32)],
        compiler_params=pltpu.CompilerParams(dimension_semantics=("arbitrary",), vmem_limit_bytes=VMEM_LIMIT),
        name="trunk_prompt",
    )(x, *consts)


def _trunk_sample_kernel(n_batch, t_len,
                         x_ref, hist_in_ref, run_in_ref, gmix_ref, win_ref, convw_ref, convb_ref, lncg_ref, lncb_ref,
                         lnvg_ref, lnvb_ref, wsgbd_ref, bsg_ref, wout_ref, gx_ref, wxq_ref, kt_ref, v_ref, wxo_ref,
                         gffn_ref, wr_ref, br_ref,
                         x2_ref, h3_ref, rinfo_ref, rt_ref, hist_ref, sgv_ref, cnt_ref,
                         ext_ref, conv_ref, att_ref):
    x = x_ref[...]
    h = _rms(x, gmix_ref[...]).astype(BF16)
    z = _dot(h, win_ref[...])
    a = z[:, 0:D_CONV] * _sigmoid(z[:, D_CONV:2 * D_CONV])
    ext_len = HIST + t_len
    for b in range(n_batch):
        ext_ref[b, 0:HIST, :] = hist_in_ref[b]
        ext_ref[b, HIST:ext_len, :] = a[b * t_len:(b + 1) * t_len, :]
    for b in range(n_batch):
        acc = jnp.zeros((t_len, D_CONV), F32)
        for k in range(CONV_WIDTH):
            acc = acc + ext_ref[b, k:k + t_len, :] * convw_ref[k:k + 1, :]
        conv_ref[b * t_len:(b + 1) * t_len, :] = acc
        hist_ref[b] = ext_ref[b, ext_len - HIST:ext_len, :]

    y = _ln(conv_ref[...] + convb_ref[...], lncg_ref[...], lncb_ref[...])
    a_out = (y * _sigmoid(y)).astype(BF16)

    u = z[:, 2 * D_CONV:2 * D_CONV + D_SG]
    v = _ln(z[:, 2 * D_CONV + D_SG:], lnvg_ref[...], lnvb_ref[...])
    sgv_ref[...] = v
    vb = v.astype(BF16)
    heads = [_dot(wsgbd_ref[hh], vb[:, hh * SG_HEAD_DIM:(hh + 1) * SG_HEAD_DIM]) for hh in range(SG_HEADS)]
    b_out = (u * (jnp.concatenate(heads, axis=1) + bsg_ref[...])).astype(BF16)

    x1 = x + _dot(a_out, wout_ref[0:D_CONV, :]) + _dot(b_out, wout_ref[D_CONV:, :])

    hx = _rms(x1, gx_ref[...]).astype(BF16)
    q = _dot(hx, wxq_ref[...]).astype(BF16)
    for b in range(n_batch):
        rs = slice(b * t_len, (b + 1) * t_len)
        att_ref[rs, :] = _attn_heads(q[rs, :], kt_ref[b], v_ref[b])
    x2 = x1 + _dot(att_ref[...], wxo_ref[...])
    x2_ref[...] = x2

    h3 = _rms(x2, gffn_ref[...]).astype(BF16)
    m = n_batch * t_len
    h3_ref[0:m, :] = _pack_bf16_pairs(h3.astype(F32))
    if h3_ref.shape[0] > m:
        h3_ref[m:, :] = jnp.zeros((h3_ref.shape[0] - m, D_MODEL // 2), jnp.uint32)
    rinfo, new_run = _route(_dot(h3, wr_ref[...]) + br_ref[...], run_in_ref[...])
    rinfo_ref[...] = rinfo
    rt_ref[...] = jnp.transpose(rinfo)[0:SUBLANES, :]
    cnt_ref[...] = new_run


def _trunk_sample(x, hist, run, p, n_batch, t_len):
    m = n_batch * t_len
    args = [x, hist, run, p["g_mix"], p["w_in"], p["conv_w"], p["conv_b"], p["ln_conv_g"], p["ln_conv_b"],
            p["ln_v_g"], p["ln_v_b"], p["w_sg_bd"], p["b_sg_rows_s"], p["w_out"], p["g_xattn"], p["w_xq"],
            p["kt_s"], p["v_s"], p["w_xo"], p["g_ffn"], p["w_router"], p["b_router"]]
    return pl.pallas_call(
        functools.partial(_trunk_sample_kernel, n_batch, t_len),
        out_shape=(jax.ShapeDtypeStruct((m, D_MODEL), F32),
                   jax.ShapeDtypeStruct((-(-m // (SC_WORKERS * SUBLANES)) * SC_WORKERS * SUBLANES, D_MODEL // 2),
                                        jnp.uint32),
                   jax.ShapeDtypeStruct((m, LOGIT_LANES), F32),
                   jax.ShapeDtypeStruct((SUBLANES, m), F32),
                   jax.ShapeDtypeStruct((n_batch, HIST, D_CONV), F32),
                   jax.ShapeDtypeStruct((m, D_SG), F32),
                   jax.ShapeDtypeStruct((1, LOGIT_LANES), F32)),
        scratch_shapes=[pltpu.VMEM((n_batch, HIST + t_len, D_CONV), F32),
                        pltpu.VMEM((m, D_CONV), F32),
                        pltpu.VMEM((m, D_MODEL), BF16)],
        compiler_params=pltpu.CompilerParams(vmem_limit_bytes=VMEM_LIMIT),
        name="trunk_sample",
    )(*args)


def _sc_worker_id():
    return lax.axis_index("s") * SC_CORES + lax.axis_index("c")


def _sc_chunk(per_w, max_chunk):
    assert per_w % SUBLANES == 0 and max_chunk <= LANES
    return max(c for c in range(SUBLANES, max_chunk + 1, SUBLANES) if per_w % c == 0)


def _sc_gather_rows(table, idx):
    n_rows, d = idx.shape[0], table.shape[1]
    per_w = n_rows // SC_WORKERS
    assert per_w * SC_WORKERS == n_rows and per_w % GATHER_ROWS == 0
    n_chunks = per_w // GATHER_ROWS
    lag = GATHER_BUFS // 2
    mesh = plsc.VectorSubcoreMesh(core_axis_name="c", subcore_axis_name="s")

    @functools.partial(
        pl.kernel, mesh=mesh,
        out_type=jax.ShapeDtypeStruct((n_rows, d), table.dtype),
        scratch_types=([pltpu.VMEM((per_w,), jnp.int32)]
                       + [pltpu.VMEM((GATHER_ROWS, d), table.dtype)] * GATHER_BUFS
                       + [pltpu.SemaphoreType.DMA] * (2 * GATHER_BUFS)),
    )
    def gather(table_hbm, idx_hbm, out_hbm, idx_all, *rest):
        rows = rest[:GATHER_BUFS]
        gsem = rest[GATHER_BUFS:2 * GATHER_BUFS]
        wsem = rest[2 * GATHER_BUFS:]
        base = _sc_worker_id() * per_w
        pltpu.sync_copy(idx_hbm.at[pl.ds(base, per_w)], idx_all)

        reads, writes = {}, {}
        for c in range(n_chunks + lag):
            if c < n_chunks:
                b = c % GATHER_BUFS
                if c >= GATHER_BUFS:
                    writes.pop(c - GATHER_BUFS).wait()
                reads[c] = pltpu.async_copy(
                    table_hbm.at[idx_all.at[pl.ds(c * GATHER_ROWS, GATHER_ROWS)]], rows[b], gsem[b])
            w = c - lag
            if w >= 0:
                b = w % GATHER_BUFS
                reads.pop(w).wait()
                writes[w] = pltpu.async_copy(
                    rows[b], out_hbm.at[pl.ds(base + w * GATHER_ROWS, GATHER_ROWS)], wsem[b])
        for w in sorted(writes):
            writes[w].wait()

    return gather(table, idx)


def _sc_scatter_rows2(tables, slots_a, slots_b, n_rows_out, max_chunk):
    d, dtype = tables[0].shape[1], tables[0].dtype
    plans = []
    for t in tables:
        per_w = t.shape[0] // SC_WORKERS
        assert per_w * SC_WORKERS == t.shape[0]
        chunk = _sc_chunk(per_w, max_chunk)
        plans.append((per_w, chunk, per_w // chunk))
    cmax = max(c for _, c, _ in plans)
    n_t = len(tables)
    mesh = plsc.VectorSubcoreMesh(core_axis_name="c", subcore_axis_name="s")

    scratch = []
    for _, chunk, _ in plans:
        for _ in range(2):
            scratch += [pltpu.VMEM((chunk,), jnp.int32), pltpu.VMEM((chunk,), jnp.int32)]
    scratch += [pltpu.VMEM((cmax, d), dtype), pltpu.VMEM((cmax, d), dtype)]
    scratch += [pltpu.SemaphoreType.DMA] * 4

    @functools.partial(pl.kernel, mesh=mesh, out_type=jax.ShapeDtypeStruct((n_rows_out, d), dtype),
                       scratch_types=scratch)
    def scatter(*refs):
        tab_hbm = refs[0:n_t]
        sa_hbm = refs[n_t:2 * n_t]
        sb_hbm = refs[2 * n_t:3 * n_t]
        out_hbm = refs[3 * n_t]
        sc = refs[3 * n_t + 1:]
        idx_refs = sc[:4 * n_t]
        rows = sc[4 * n_t:4 * n_t + 2]
        lsem = sc[4 * n_t + 2:4 * n_t + 4]
        ssem = sc[4 * n_t + 4:4 * n_t + 6]
        wid = _sc_worker_id()

        work = []
        for t, (per_w, chunk, n_chunks) in enumerate(plans):
            for j in range(n_chunks):
                work.append((t, wid * per_w + j * chunk, chunk))

        def parts(k):
            t, off, chunk = work[k]
            b = k % 2
            ia, ib = idx_refs[4 * t + 2 * b], idx_refs[4 * t + 2 * b + 1]
            rv = rows[b] if chunk == cmax else rows[b].at[pl.ds(0, chunk)]
            return t, off, chunk, b, ia, ib, rv

        def start_load(k):
            t, off, chunk, b, ia, ib, rv = parts(k)
            return (pltpu.async_copy(tab_hbm[t].at[pl.ds(off, chunk)], rv, lsem[b]),
                    pltpu.async_copy(sa_hbm[t].at[pl.ds(off, chunk)], ia, lsem[b]),
                    pltpu.async_copy(sb_hbm[t].at[pl.ds(off, chunk)], ib, lsem[b]))

        def start_scatter(k):
            t, off, chunk, b, ia, ib, rv = parts(k)
            return (pltpu.async_copy(rv, out_hbm.at[ia], ssem[b]), pltpu.async_copy(rv, out_hbm.at[ib], ssem[b]))

        loads = {0: start_load(0)}
        scatters = {}
        for k in range(len(work)):
            for c in loads.pop(k):
                c.wait()
            scatters[k] = start_scatter(k)
            if k >= 1:
                for c in scatters.pop(k - 1):
                    c.wait()
            if k + 1 < len(work):
                loads[k + 1] = start_load(k + 1)
        for c in scatters.pop(len(work) - 1):
            c.wait()

    return scatter(*tables, *slots_a, *slots_b)


def _experts_kernel(first_ref, nblk_ref, cnt_ref, tot_ref, xs_hbm, wg_ref, wu_ref, wd_ref, ys_hbm,
                    xbuf, ybuf, wg_bf, wu_bf, wd_bf, in_sem, out_sem):
    e = pl.program_id(0)
    nb = nblk_ref[e]
    first = first_ref[e]
    cnt = cnt_ref[e]
    total = tot_ref[0]
    half = D_MODEL // 2

    def in_copy(gb):
        slot = lax.rem(gb, X_BUFS)
        return pltpu.make_async_copy(xs_hbm.at[pl.ds(gb * BM, BM)], xbuf.at[slot], in_sem.at[slot])

    def out_copy(gb):
        slot = lax.rem(gb, Y_BUFS)
        return pltpu.make_async_copy(ybuf.at[slot], ys_hbm.at[pl.ds(gb * BM, BM)], out_sem.at[slot])

    @pl.when(nb > 0)
    def _():
        @pl.when(first == 0)
        def _():
            for k in range(X_LOOKAHEAD):
                @pl.when(k < total)
                def _():
                    in_copy(k).start(priority=ROW_DMA_PRIORITY)

        wg_bf[...] = wg_ref[0].astype(BF16)
        wu_bf[...] = wu_ref[0].astype(BF16)
        wd_bf[...] = wd_ref[0].astype(BF16)

        def block(j, carry):
            gb = first + j

            @pl.when(gb + X_LOOKAHEAD < total)
            def _():
                in_copy(gb + X_LOOKAHEAD).start(priority=ROW_DMA_PRIORITY)

            in_copy(gb).wait()

            @pl.when(gb >= Y_BUFS)
            def _():
                out_copy(gb - Y_BUFS).wait()

            live = lax.broadcasted_iota(jnp.int32, (BM, half), 0) < cnt - j * BM
            lo, hi = _unpack_bf16_pairs(jnp.where(live, xbuf[lax.rem(gb, X_BUFS)], jnp.uint32(0)))
            g = _dot(lo, wg_bf[0:half, :]) + _dot(hi, wg_bf[half:, :])
            u = _dot(lo, wu_bf[0:half, :]) + _dot(hi, wu_bf[half:, :])
            hm = (g * _sigmoid(g) * u).astype(BF16)
            y = _dot(hm, wd_bf[...])
            ybuf[lax.rem(gb, Y_BUFS)] = _pack_bf16_pairs(y.astype(BF16).astype(F32))
            out_copy(gb).start(priority=ROW_DMA_PRIORITY)
            return carry

        lax.fori_loop(0, nb, block, 0)

        @pl.when(first + nb == total)
        def _():
            for k in range(Y_BUFS):
                @pl.when(total - 1 - k >= 0)
                def _():
                    out_copy(total - 1 - k).wait()


def _experts(xs, n_rows_out, first_block, n_blocks_e, counts, w_eg, w_eu, w_ed):
    w_map = lambda e, fb, nb, ct, tot: (e, 0, 0)
    half = D_MODEL // 2
    total = jnp.sum(n_blocks_e).astype(jnp.int32).reshape(1)
    return pl.pallas_call(
        _experts_kernel,
        grid_spec=pltpu.PrefetchScalarGridSpec(
            num_scalar_prefetch=4,
            grid=(N_EXPERTS,),
            in_specs=[pl.BlockSpec(memory_space=pl.ANY),
                      pl.BlockSpec((1, D_MODEL, D_EXPERT), w_map),
                      pl.BlockSpec((1, D_MODEL, D_EXPERT), w_map),
                      pl.BlockSpec((1, D_EXPERT, D_MODEL), w_map)],
            out_specs=pl.BlockSpec(memory_space=pl.ANY),
            scratch_shapes=[pltpu.VMEM((X_BUFS, BM, half), jnp.uint32), pltpu.VMEM((Y_BUFS, BM, half), jnp.uint32),
                            pltpu.VMEM((D_MODEL, D_EXPERT), BF16), pltpu.VMEM((D_MODEL, D_EXPERT), BF16),
                            pltpu.VMEM((D_EXPERT, D_MODEL), BF16),
                            pltpu.SemaphoreType.DMA((X_BUFS,)), pltpu.SemaphoreType.DMA((Y_BUFS,))]),
        out_shape=jax.ShapeDtypeStruct((n_rows_out, half), jnp.uint32),
        compiler_params=pltpu.CompilerParams(dimension_semantics=("arbitrary",), vmem_limit_bytes=VMEM_LIMIT),
        name="experts",
    )(first_block, n_blocks_e, counts, total, xs, w_eg, w_eu, w_ed)


def _combine_kernel(x2_ref, y1_ref, y2_ref, rinfo_ref, g_ref, o_ref):
    r = rinfo_ref[...]
    g1, g2 = r[:, 2:3], r[:, 3:4]
    half = D_MODEL // 2
    y1_lo, y1_hi = _unpack_bf16_pairs_f32(y1_ref[...])
    y2_lo, y2_hi = _unpack_bf16_pairs_f32(y2_ref[...])
    x_lo = x2_ref[:, 0:half] + g1 * y1_lo + g2 * y2_lo
    x_hi = x2_ref[:, half:] + g1 * y1_hi + g2 * y2_hi
    ms = (jnp.sum(x_lo * x_lo, axis=-1, keepdims=True) + jnp.sum(x_hi * x_hi, axis=-1, keepdims=True)) / D_MODEL
    inv = lax.rsqrt(ms + EPS)
    o_ref[:, 0:half] = x_lo * inv * g_ref[:, 0:half]
    o_ref[:, half:] = x_hi * inv * g_ref[:, half:]


def _combine(x2, yg, rinfo, g_final, tm, blk1, blk2):
    n = x2.shape[0]
    return pl.pallas_call(
        _combine_kernel,
        grid=(n // tm,),
        in_specs=[pl.BlockSpec((tm, D_MODEL), lambda i: (i, 0)),
                  pl.BlockSpec((tm, D_MODEL // 2), lambda i: (blk1 + i, 0)),
                  pl.BlockSpec((tm, D_MODEL // 2), lambda i: (blk2 + i, 0)),
                  pl.BlockSpec((tm, LOGIT_LANES), lambda i: (i, 0)),
                  pl.BlockSpec((1, D_MODEL), lambda i: (0, 0))],
        out_specs=pl.BlockSpec((tm, D_MODEL), lambda i: (i, 0)),
        out_shape=jax.ShapeDtypeStruct((n, D_MODEL), F32),
        compiler_params=pltpu.CompilerParams(dimension_semantics=("arbitrary",), vmem_limit_bytes=VMEM_LIMIT),
        name="combine",
    )(x2, yg, yg, rinfo, g_final)


def _gather_rows(table, idx):
    return _sc_gather_rows(table, idx)


def _scatter_rows2(tables, slots_a, slots_b, n_rows_out):
    return _sc_scatter_rows2(tables, slots_a, slots_b, n_rows_out, SCATTER_CHUNK)


def kernel(x_prompt, x_sample, mem_prompt, state_conv, cache_mem_k, cache_mem_v, g_mix, w_in, conv_w, conv_b, ln_conv_g, ln_conv_b, ln_v_g, ln_v_b, w_sg, b_sg, w_out, g_mem, w_mk, w_mv, g_xattn, w_xq, w_xo, g_ffn, w_router_group, b_router_group, w_router_expert, b_router_expert, w_expert_gate, w_expert_up, w_expert_down, g_final):
    assert x_prompt.shape[0] == 1 and g_mix.shape[0] == 1
    n_p = x_prompt.shape[1]
    n_batch, t_len = x_sample.shape[0], x_sample.shape[1]
    n_s = n_batch * t_len
    row = lambda a: a.reshape(1, -1)

    w_router = jnp.concatenate(
        [w_router_group[0], jnp.transpose(w_router_expert[0], (1, 0, 2)).reshape(D_MODEL, N_EXPERTS)], axis=1)
    w_router = jnp.pad(w_router, ((0, 0), (0, LOGIT_LANES - w_router.shape[1]))).astype(BF16)
    b_router = jnp.pad(jnp.concatenate([b_router_group[0], b_router_expert[0].reshape(-1)]),
                       (0, LOGIT_LANES - N_GROUPS - N_EXPERTS)).reshape(1, LOGIT_LANES)
    tril_t = jnp.tril(jnp.ones((t_len, t_len), bool))
    w_sg_t = jnp.where(tril_t, w_sg[0][:, :t_len, :t_len], 0.0)
    eye_b = jnp.eye(n_batch, dtype=F32)
    w_sg_bd = jnp.einsum("ab,hij->haibj", eye_b, w_sg_t).reshape(SG_HEADS, n_s, n_s).astype(BF16)
    p = {
        "g_mix": row(g_mix[0]), "w_in": w_in[0].astype(BF16),
        "conv_w": jnp.pad(conv_w[0], ((0, 1), (0, 0))), "conv_b": row(conv_b[0]),
        "ln_conv_g": row(ln_conv_g[0]), "ln_conv_b": row(ln_conv_b[0]),
        "ln_v_g": row(ln_v_g[0]), "ln_v_b": row(ln_v_b[0]),
        "w_sg": w_sg[0],
        "b_sg_rows": jnp.repeat(b_sg[0].T, SG_HEAD_DIM, axis=1),
        "w_sg_bd": w_sg_bd,
        "b_sg_rows_s": jnp.tile(jnp.repeat(b_sg[0][:, :t_len].T, SG_HEAD_DIM, axis=1), (n_batch, 1)),
        "w_out": w_out[0].astype(BF16), "g_xattn": row(g_xattn[0]),
        "w_xq": w_xq[0].astype(BF16), "w_xo": w_xo[0].astype(BF16), "g_ffn": row(g_ffn[0]),
        "w_router": w_router, "b_router": b_router,
    }

    k_p, v_p = _memkv(mem_prompt[0], row(g_mem[0]), w_mk[0].astype(BF16), w_mv[0].astype(BF16))
    p["kt"] = k_p.T.astype(BF16)
    p["v"] = v_p.astype(BF16)
    p["kt_s"] = jnp.transpose(cache_mem_k[0].reshape(n_batch, N_MEM, D_MODEL), (0, 2, 1)).astype(BF16)
    p["v_s"] = cache_mem_v[0].reshape(n_batch, N_MEM, D_MODEL).astype(BF16)

    x2_p, h3_p, rinfo_p, rt_p, hist_p, cnt_p = _trunk_prompt(x_prompt[0], p)
    x2_s, h3_s, rinfo_s, rt_s, hist_s, sgv_s, cnt = _trunk_sample(
        x_sample.reshape(n_s, D_MODEL), state_conv[0], cnt_p, p, n_batch, t_len)

    n_tot = n_p + n_s
    n_slots = -(-(n_tot * 2) // BM) * BM + N_EXPERTS * BM
    counts = cnt[0, :N_EXPERTS].astype(jnp.int32)
    padded = (counts + BM - 1) // BM * BM
    pad_end = jnp.cumsum(padded)
    pad_start = pad_end - padded
    experts = jnp.arange(N_EXPERTS, dtype=jnp.int32)

    def slots_of(rt):
        def one(e_row, rank_row):
            e = e_row.astype(jnp.int32)
            start = jnp.sum(jnp.where(e[None, :] == experts[:, None], pad_start[:, None], 0), axis=0)
            return start + rank_row.astype(jnp.int32)
        return one(rt[0], rt[4]), one(rt[1], rt[5])

    slot_p = slots_of(rt_p)
    slot_s = slots_of(rt_s)

    n_spare = h3_s.shape[0] - n_s
    spare = n_slots + jnp.arange(n_spare, dtype=jnp.int32)
    xs = _scatter_rows2(
        (h3_p, h3_s),
        (slot_p[0], jnp.concatenate([slot_s[0], spare])),
        (slot_p[1], jnp.concatenate([slot_s[1], spare + n_spare])),
        n_slots + 2 * n_spare)
    ys = _experts(xs, n_slots, pad_start // BM, padded // BM, counts,
                  w_expert_gate[0], w_expert_up[0], w_expert_down[0])
    back_idx = jnp.concatenate([slot_p[0], slot_p[1], slot_s[0], slot_s[1]])
    n_back = -(-back_idx.shape[0] // (SC_WORKERS * GATHER_ROWS)) * (SC_WORKERS * GATHER_ROWS)
    yg = _gather_rows(ys, jnp.pad(back_idx, (0, n_back - back_idx.shape[0])))

    y_p = _combine(x2_p, yg, rinfo_p, row(g_final), TM, 0, n_p // TM)
    y_s = _combine(x2_s, yg, rinfo_s, row(g_final), n_s, 2 * n_p // n_s, 2 * n_p // n_s + 1)

    return (y_p.reshape(1, n_p, D_MODEL),
            y_s.reshape(n_batch, t_len, D_MODEL),
            hist_p[HALO - HIST:].reshape(1, 1, HIST, D_CONV),
            hist_s.reshape(1, n_batch, HIST, D_CONV),
            k_p.reshape(1, 1, N_MEM, X_HEADS, X_HEAD_DIM),
            v_p.reshape(1, 1, N_MEM, X_HEADS, X_HEAD_DIM),
            sgv_s.reshape(1, n_batch, t_len, D_SG))
```

```python
import functools

import jax
import jax.numpy as jnp
from jax import lax
from jax.experimental import pallas as pl
from jax.experimental.pallas import tpu as pltpu
from jax.experimental.pallas import tpu_sc as plsc

D_MODEL = 1024
D_CONV = 512
D_SG = 512
CONV_WIDTH = 31
HIST = CONV_WIDTH - 1
SG_HEADS = 4
SG_HEAD_DIM = 128
SG_CHUNK = 128
N_MEM = 256
X_HEADS = 4
X_HEAD_DIM = 256
N_GROUPS = 4
EXPERTS_PER_GROUP = 8
N_EXPERTS = 32
D_EXPERT = 512
EPS = 1e-6

LANES = 128
SUBLANES = 8
SC_CORES = 2
SC_SUBCORES = 16
SC_WORKERS = SC_CORES * SC_SUBCORES
VMEM_LIMIT = 56 * 1024 * 1024

TM = 512
HALO = 32
CONV_ROWS = 128
BM = 256
X_LOOKAHEAD = 3
X_BUFS = X_LOOKAHEAD + 1
Y_BUFS = 3
ROW_DMA_PRIORITY = 1
GATHER_ROWS = 32
GATHER_BUFS = 6
SCATTER_CHUNK = 64
LOGIT_LANES = 128

F32 = jnp.float32
BF16 = jnp.bfloat16


def _dot(a, b):
    return jnp.dot(a, b, preferred_element_type=F32)


def _rms(x, g):
    return x * lax.rsqrt(jnp.mean(x * x, axis=-1, keepdims=True) + EPS) * g


def _ln(x, g, b):
    mu = jnp.mean(x, axis=-1, keepdims=True)
    xc = x - mu
    var = jnp.mean(xc * xc, axis=-1, keepdims=True)
    return xc * lax.rsqrt(var + EPS) * g + b


def _sigmoid(x):
    return 1.0 / (1.0 + jnp.exp(-x))


def _pack_bf16_pairs(h):
    bits = lax.bitcast_convert_type(h, jnp.uint32)
    half = h.shape[1] // 2
    lo = lax.shift_right_logical(bits[:, :half], jnp.uint32(16))
    hi = bits[:, half:] & jnp.uint32(0xFFFF0000)
    return hi | lo


def _unpack_bf16_pairs_f32(p):
    lo = lax.bitcast_convert_type(lax.shift_left(p, jnp.uint32(16)), F32)
    hi = lax.bitcast_convert_type(p & jnp.uint32(0xFFFF0000), F32)
    return lo, hi


def _unpack_bf16_pairs(p):
    lo, hi = _unpack_bf16_pairs_f32(p)
    return lo.astype(BF16), hi.astype(BF16)


def _memkv_kernel(mem_ref, g_ref, wk_ref, wv_ref, k_ref, v_ref):
    m = _rms(mem_ref[...], g_ref[...]).astype(BF16)
    k_ref[...] = _dot(m, wk_ref[...])
    v_ref[...] = _dot(m, wv_ref[...])


def _memkv(mem, g_mem, w_mk, w_mv):
    return pl.pallas_call(
        _memkv_kernel,
        out_shape=(jax.ShapeDtypeStruct((N_MEM, D_MODEL), F32), jax.ShapeDtypeStruct((N_MEM, D_MODEL), F32)),
        compiler_params=pltpu.CompilerParams(vmem_limit_bytes=VMEM_LIMIT),
        name="memkv",
    )(mem, g_mem, w_mk, w_mv)


def _attn_heads(q, kt, v):
    outs = []
    for h in range(X_HEADS):
        sl = slice(h * X_HEAD_DIM, (h + 1) * X_HEAD_DIM)
        s = _dot(q[:, sl], kt[sl, :]) * (X_HEAD_DIM ** -0.5)
        s = s - jnp.max(s, axis=-1, keepdims=True)
        p = jnp.exp(s)
        p = p / jnp.sum(p, axis=-1, keepdims=True)
        outs.append(_dot(p.astype(BF16), v[:, sl]).astype(BF16))
    return jnp.concatenate(outs, axis=1)


def _route(logits, run):
    m = logits.shape[0]
    lane = lax.broadcasted_iota(jnp.int32, (m, LOGIT_LANES), 1).astype(F32)
    neg = jnp.float32(-jnp.inf)
    big = jnp.float32(LOGIT_LANES)

    def first_argmax(vals):
        mx = jnp.max(vals, axis=-1, keepdims=True)
        idx = jnp.min(jnp.where(vals == mx, lane, big), axis=-1, keepdims=True)
        return mx, idx

    lg = jnp.where(lane < N_GROUPS, logits, neg)
    g_max, g_idx = first_argmax(lg)
    g_w = 1.0 / jnp.sum(jnp.exp(lg - g_max), axis=-1, keepdims=True)

    lo = N_GROUPS + g_idx * EXPERTS_PER_GROUP
    le = jnp.where((lane >= lo) & (lane < lo + EXPERTS_PER_GROUP), logits, neg)
    v1, i1 = first_argmax(le)
    v2, i2 = first_argmax(jnp.where(lane == i1, neg, le))
    t = jnp.exp(v2 - v1)
    gate1 = g_w / (1.0 + t)
    gate2 = g_w * t / (1.0 + t)
    e1 = i1 - N_GROUPS
    e2 = i2 - N_GROUPS

    oh1 = (lane == e1).astype(F32)
    oh2 = (lane == e2).astype(F32)
    oh = oh1 + oh2
    row = lax.broadcasted_iota(jnp.int32, (m, m), 0)
    col = lax.broadcasted_iota(jnp.int32, (m, m), 1)
    strict_lower = (col < row).astype(BF16)
    before = _dot(strict_lower, oh.astype(BF16)) + run
    rank1 = jnp.sum(before * oh1, axis=-1, keepdims=True)
    rank2 = jnp.sum(before * oh2, axis=-1, keepdims=True)
    new_run = run + jnp.sum(oh, axis=0, keepdims=True)

    rinfo = jnp.where(lane == 0, e1,
            jnp.where(lane == 1, e2,
            jnp.where(lane == 2, gate1,
            jnp.where(lane == 3, gate2,
            jnp.where(lane == 4, rank1,
            jnp.where(lane == 5, rank2, 0.0))))))
    return rinfo, new_run


def _conv_taps(win, w_ref, l0, rows):
    n = win.shape[0]
    acc = jnp.zeros((rows, LANES), F32)
    for r in range(SUBLANES):
        d = HALO - HIST + r
        if d % SUBLANES == 0:
            e, base = win, d
        else:
            e, base = pltpu.roll(win, n - d, axis=0), 0
        for qq in range(-(-CONV_WIDTH // SUBLANES)):
            k = SUBLANES * qq + r
            if k < CONV_WIDTH:
                s0 = base + SUBLANES * qq
                acc = acc + e[s0:s0 + rows, :] * w_ref[pl.ds(k, 1), pl.ds(l0, LANES)]
    return acc


def _trunk_prompt_kernel(x_ref, hist_in_ref, gmix_ref, win_ref, convw_ref, convb_ref, lncg_ref, lncb_ref, lnvg_ref, lnvb_ref,
                         wsg_ref, bsg_ref, wout_ref, gx_ref, wxq_ref, kt_ref, v_ref, wxo_ref, gffn_ref, wr_ref,
                         br_ref,
                         x2_ref, h3_ref, rinfo_ref, rt_ref, hist_ref, cnt_ref,
                         ext_ref, conv_ref, run_ref):
    i = pl.program_id(0)

    @pl.when(i == 0)
    def _():
        ext_ref[0:HALO, :] = hist_in_ref[...]
        run_ref[...] = jnp.zeros((1, LOGIT_LANES), F32)

    x = x_ref[...]
    h = _rms(x, gmix_ref[...]).astype(BF16)

    a_in = _dot(h, win_ref[:, 0:D_CONV])
    a_gate = _dot(h, win_ref[:, D_CONV:2 * D_CONV])
    ext_ref[HALO:HALO + TM, :] = a_in * _sigmoid(a_gate)

    for r0 in range(0, TM, CONV_ROWS):
        for l0 in range(0, D_CONV, LANES):
            win = ext_ref[r0:r0 + CONV_ROWS + HALO, l0:l0 + LANES]
            conv_ref[r0:r0 + CONV_ROWS, l0:l0 + LANES] = _conv_taps(win, convw_ref, l0, CONV_ROWS)
    hist_ref[...] = ext_ref[TM:TM + HALO, :]
    ext_ref[0:HALO, :] = ext_ref[TM:TM + HALO, :]

    y = _ln(conv_ref[...] + convb_ref[...], lncg_ref[...], lncb_ref[...])
    a_out = (y * _sigmoid(y)).astype(BF16)

    u = _dot(h, win_ref[:, 2 * D_CONV:2 * D_CONV + D_SG])
    v = _ln(_dot(h, win_ref[:, 2 * D_CONV + D_SG:]), lnvg_ref[...], lnvb_ref[...]).astype(BF16)
    ri = lax.broadcasted_iota(jnp.int32, (SG_CHUNK, SG_CHUNK), 0)
    ci = lax.broadcasted_iota(jnp.int32, (SG_CHUNK, SG_CHUNK), 1)
    w_tril = [jnp.where(ci <= ri, wsg_ref[hh], 0.0).astype(BF16) for hh in range(SG_HEADS)]
    gate_rows = []
    for c in range(TM // SG_CHUNK):
        rs = slice(c * SG_CHUNK, (c + 1) * SG_CHUNK)
        heads = [_dot(w_tril[hh], v[rs, hh * SG_HEAD_DIM:(hh + 1) * SG_HEAD_DIM]) for hh in range(SG_HEADS)]
        gate_rows.append(jnp.concatenate(heads, axis=1) + bsg_ref[...])
    b_out = (u * jnp.concatenate(gate_rows, axis=0)).astype(BF16)

    x1 = x + _dot(a_out, wout_ref[0:D_CONV, :]) + _dot(b_out, wout_ref[D_CONV:, :])

    hx = _rms(x1, gx_ref[...]).astype(BF16)
    q = _dot(hx, wxq_ref[...]).astype(BF16)
    x2 = x1 + _dot(_attn_heads(q, kt_ref[...], v_ref[...]), wxo_ref[...])
    x2_ref[...] = x2

    h3 = _rms(x2, gffn_ref[...]).astype(BF16)
    h3_ref[...] = _pack_bf16_pairs(h3.astype(F32))
    rinfo, new_run = _route(_dot(h3, wr_ref[...]) + br_ref[...], run_ref[...])
    rinfo_ref[...] = rinfo
    rt_ref[...] = jnp.transpose(rinfo)[0:SUBLANES, :]
    run_ref[...] = new_run
    cnt_ref[...] = new_run


def _const_spec(shape):
    nd = len(shape)
    return pl.BlockSpec(shape, lambda i: (0,) * nd, pipeline_mode=pl.Buffered(1))


def _trunk_prompt(x, row0, n, hist_in, p):
    assert n % TM == 0 and row0 % TM == 0
    blk0 = row0 // TM
    row = lambda w: pl.BlockSpec((TM, w), lambda i: (i, 0))
    consts = [p["g_mix"], p["w_in"], p["conv_w"], p["conv_b"], p["ln_conv_g"], p["ln_conv_b"], p["ln_v_g"],
              p["ln_v_b"], p["w_sg"], p["b_sg_rows"], p["w_out"], p["g_xattn"], p["w_xq"], p["kt"], p["v"],
              p["w_xo"], p["g_ffn"], p["w_router"], p["b_router"]]
    return pl.pallas_call(
        _trunk_prompt_kernel,
        grid=(n // TM,),
        in_specs=([pl.BlockSpec((TM, D_MODEL), lambda i: (blk0 + i, 0)), _const_spec(hist_in.shape)]
                  + [_const_spec(c.shape) for c in consts]),
        out_specs=(row(D_MODEL), row(D_MODEL // 2), row(LOGIT_LANES),
                   pl.BlockSpec((SUBLANES, TM), lambda i: (0, i)),
                   pl.BlockSpec((HALO, D_CONV), lambda i: (0, 0)),
                   pl.BlockSpec((1, LOGIT_LANES), lambda i: (0, 0))),
        out_shape=(jax.ShapeDtypeStruct((n, D_MODEL), F32),
                   jax.ShapeDtypeStruct((n, D_MODEL // 2), jnp.uint32),
                   jax.ShapeDtypeStruct((n, LOGIT_LANES), F32),
                   jax.ShapeDtypeStruct((SUBLANES, n), F32),
                   jax.ShapeDtypeStruct((HALO, D_CONV), F32),
                   jax.ShapeDtypeStruct((1, LOGIT_LANES), F32)),
        scratch_shapes=[pltpu.VMEM((TM + HALO, D_CONV), F32),
                        pltpu.VMEM((TM, D_CONV), F32),
                        pltpu.VMEM((1, LOGIT_LANES), F32)],
        compiler_params=pltpu.CompilerParams(dimension_semantics=("arbitrary",), vmem_limit_bytes=VMEM_LIMIT),
        name="trunk_prompt",
    )(x, hist_in, *consts)


def _trunk_sample_kernel(n_batch, t_len,
                         x_ref, hist_in_ref, run_in_ref, gmix_ref, win_ref, convw_ref, convb_ref, lncg_ref, lncb_ref,
                         lnvg_ref, lnvb_ref, wsgbd_ref, bsg_ref, wout_ref, gx_ref, wxq_ref, kt_ref, v_ref, wxo_ref,
                         gffn_ref, wr_ref, br_ref,
                         x2_ref, h3_ref, rinfo_ref, rt_ref, hist_ref, sgv_ref, cnt_ref,
                         ext_ref, conv_ref, att_ref):
    x = x_ref[...]
    h = _rms(x, gmix_ref[...]).astype(BF16)
    z = _dot(h, win_ref[...])
    a = z[:, 0:D_CONV] * _sigmoid(z[:, D_CONV:2 * D_CONV])
    ext_len = HIST + t_len
    for b in range(n_batch):
        ext_ref[b, 0:HIST, :] = hist_in_ref[b]
        ext_ref[b, HIST:ext_len, :] = a[b * t_len:(b + 1) * t_len, :]
    for b in range(n_batch):
        acc = jnp.zeros((t_len, D_CONV), F32)
        for k in range(CONV_WIDTH):
            acc = acc + ext_ref[b, k:k + t_len, :] * convw_ref[k:k + 1, :]
        conv_ref[b * t_len:(b + 1) * t_len, :] = acc
        hist_ref[b] = ext_ref[b, ext_len - HIST:ext_len, :]

    y = _ln(conv_ref[...] + convb_ref[...], lncg_ref[...], lncb_ref[...])
    a_out = (y * _sigmoid(y)).astype(BF16)

    u = z[:, 2 * D_CONV:2 * D_CONV + D_SG]
    v = _ln(z[:, 2 * D_CONV + D_SG:], lnvg_ref[...], lnvb_ref[...])
    sgv_ref[...] = v
    vb = v.astype(BF16)
    heads = [_dot(wsgbd_ref[hh], vb[:, hh * SG_HEAD_DIM:(hh + 1) * SG_HEAD_DIM]) for hh in range(SG_HEADS)]
    b_out = (u * (jnp.concatenate(heads, axis=1) + bsg_ref[...])).astype(BF16)

    x1 = x + _dot(a_out, wout_ref[0:D_CONV, :]) + _dot(b_out, wout_ref[D_CONV:, :])

    hx = _rms(x1, gx_ref[...]).astype(BF16)
    q = _dot(hx, wxq_ref[...]).astype(BF16)
    for b in range(n_batch):
        rs = slice(b * t_len, (b + 1) * t_len)
        att_ref[rs, :] = _attn_heads(q[rs, :], kt_ref[b], v_ref[b])
    x2 = x1 + _dot(att_ref[...], wxo_ref[...])
    x2_ref[...] = x2

    h3 = _rms(x2, gffn_ref[...]).astype(BF16)
    m = n_batch * t_len
    h3_ref[0:m, :] = _pack_bf16_pairs(h3.astype(F32))
    if h3_ref.shape[0] > m:
        h3_ref[m:, :] = jnp.zeros((h3_ref.shape[0] - m, D_MODEL // 2), jnp.uint32)
    rinfo, new_run = _route(_dot(h3, wr_ref[...]) + br_ref[...], run_in_ref[...])
    rinfo_ref[...] = rinfo
    rt_ref[...] = jnp.transpose(rinfo)[0:SUBLANES, :]
    cnt_ref[...] = new_run


def _trunk_sample(x, hist, run, p, n_batch, t_len):
    m = n_batch * t_len
    args = [x, hist, run, p["g_mix"], p["w_in"], p["conv_w"], p["conv_b"], p["ln_conv_g"], p["ln_conv_b"],
            p["ln_v_g"], p["ln_v_b"], p["w_sg_bd"], p["b_sg_rows_s"], p["w_out"], p["g_xattn"], p["w_xq"],
            p["kt_s"], p["v_s"], p["w_xo"], p["g_ffn"], p["w_router"], p["b_router"]]
    return pl.pallas_call(
        functools.partial(_trunk_sample_kernel, n_batch, t_len),
        out_shape=(jax.ShapeDtypeStruct((m, D_MODEL), F32),
                   jax.ShapeDtypeStruct((-(-m // (SC_WORKERS * SUBLANES)) * SC_WORKERS * SUBLANES, D_MODEL // 2),
                                        jnp.uint32),
                   jax.ShapeDtypeStruct((m, LOGIT_LANES), F32),
                   jax.ShapeDtypeStruct((SUBLANES, m), F32),
                   jax.ShapeDtypeStruct((n_batch, HIST, D_CONV), F32),
                   jax.ShapeDtypeStruct((m, D_SG), F32),
                   jax.ShapeDtypeStruct((1, LOGIT_LANES), F32)),
        scratch_shapes=[pltpu.VMEM((n_batch, HIST + t_len, D_CONV), F32),
                        pltpu.VMEM((m, D_CONV), F32),
                        pltpu.VMEM((m, D_MODEL), BF16)],
        compiler_params=pltpu.CompilerParams(vmem_limit_bytes=VMEM_LIMIT),
        name="trunk_sample",
    )(*args)


def _sc_worker_id():
    return lax.axis_index("s") * SC_CORES + lax.axis_index("c")


def _sc_chunk(per_w, max_chunk):
    assert per_w % SUBLANES == 0 and max_chunk <= LANES
    return max(c for c in range(SUBLANES, max_chunk + 1, SUBLANES) if per_w % c == 0)


def _sc_gather_rows(table, idx):
    n_rows, d = idx.shape[0], table.shape[1]
    per_w = n_rows // SC_WORKERS
    assert per_w * SC_WORKERS == n_rows and per_w % GATHER_ROWS == 0
    n_chunks = per_w // GATHER_ROWS
    lag = GATHER_BUFS // 2
    mesh = plsc.VectorSubcoreMesh(core_axis_name="c", subcore_axis_name="s")

    @functools.partial(
        pl.kernel, mesh=mesh,
        out_type=jax.ShapeDtypeStruct((n_rows, d), table.dtype),
        scratch_types=([pltpu.VMEM((per_w,), jnp.int32)]
                       + [pltpu.VMEM((GATHER_ROWS, d), table.dtype)] * GATHER_BUFS
                       + [pltpu.SemaphoreType.DMA] * (2 * GATHER_BUFS)),
    )
    def gather(table_hbm, idx_hbm, out_hbm, idx_all, *rest):
        rows = rest[:GATHER_BUFS]
        gsem = rest[GATHER_BUFS:2 * GATHER_BUFS]
        wsem = rest[2 * GATHER_BUFS:]
        base = _sc_worker_id() * per_w
        pltpu.sync_copy(idx_hbm.at[pl.ds(base, per_w)], idx_all)

        reads, writes = {}, {}
        for c in range(n_chunks + lag):
            if c < n_chunks:
                b = c % GATHER_BUFS
                if c >= GATHER_BUFS:
                    writes.pop(c - GATHER_BUFS).wait()
                reads[c] = pltpu.async_copy(
                    table_hbm.at[idx_all.at[pl.ds(c * GATHER_ROWS, GATHER_ROWS)]], rows[b], gsem[b])
            w = c - lag
            if w >= 0:
                b = w % GATHER_BUFS
                reads.pop(w).wait()
                writes[w] = pltpu.async_copy(
                    rows[b], out_hbm.at[pl.ds(base + w * GATHER_ROWS, GATHER_ROWS)], wsem[b])
        for w in sorted(writes):
            writes[w].wait()

    return gather(table, idx)


def _sc_scatter_rows2(tables, slots_a, slots_b, n_rows_out, max_chunk):
    d, dtype = tables[0].shape[1], tables[0].dtype
    plans = []
    for t in tables:
        per_w = t.shape[0] // SC_WORKERS
        assert per_w * SC_WORKERS == t.shape[0]
        chunk = _sc_chunk(per_w, max_chunk)
        plans.append((per_w, chunk, per_w // chunk))
    cmax = max(c for _, c, _ in plans)
    n_t = len(tables)
    mesh = plsc.VectorSubcoreMesh(core_axis_name="c", subcore_axis_name="s")

    scratch = []
    for _, chunk, _ in plans:
        for _ in range(2):
            scratch += [pltpu.VMEM((chunk,), jnp.int32), pltpu.VMEM((chunk,), jnp.int32)]
    scratch += [pltpu.VMEM((cmax, d), dtype), pltpu.VMEM((cmax, d), dtype)]
    scratch += [pltpu.SemaphoreType.DMA] * 4

    @functools.partial(pl.kernel, mesh=mesh, out_type=jax.ShapeDtypeStruct((n_rows_out, d), dtype),
                       scratch_types=scratch)
    def scatter(*refs):
        tab_hbm = refs[0:n_t]
        sa_hbm = refs[n_t:2 * n_t]
        sb_hbm = refs[2 * n_t:3 * n_t]
        out_hbm = refs[3 * n_t]
        sc = refs[3 * n_t + 1:]
        idx_refs = sc[:4 * n_t]
        rows = sc[4 * n_t:4 * n_t + 2]
        lsem = sc[4 * n_t + 2:4 * n_t + 4]
        ssem = sc[4 * n_t + 4:4 * n_t + 6]
        wid = _sc_worker_id()

        work = []
        for t, (per_w, chunk, n_chunks) in enumerate(plans):
            for j in range(n_chunks):
                work.append((t, wid * per_w + j * chunk, chunk))

        def parts(k):
            t, off, chunk = work[k]
            b = k % 2
            ia, ib = idx_refs[4 * t + 2 * b], idx_refs[4 * t + 2 * b + 1]
            rv = rows[b] if chunk == cmax else rows[b].at[pl.ds(0, chunk)]
            return t, off, chunk, b, ia, ib, rv

        def start_load(k):
            t, off, chunk, b, ia, ib, rv = parts(k)
            return (pltpu.async_copy(tab_hbm[t].at[pl.ds(off, chunk)], rv, lsem[b]),
                    pltpu.async_copy(sa_hbm[t].at[pl.ds(off, chunk)], ia, lsem[b]),
                    pltpu.async_copy(sb_hbm[t].at[pl.ds(off, chunk)], ib, lsem[b]))

        def start_scatter(k):
            t, off, chunk, b, ia, ib, rv = parts(k)
            return (pltpu.async_copy(rv, out_hbm.at[ia], ssem[b]), pltpu.async_copy(rv, out_hbm.at[ib], ssem[b]))

        loads = {0: start_load(0)}
        scatters = {}
        for k in range(len(work)):
            for c in loads.pop(k):
                c.wait()
            scatters[k] = start_scatter(k)
            if k >= 1:
                for c in scatters.pop(k - 1):
                    c.wait()
            if k + 1 < len(work):
                loads[k + 1] = start_load(k + 1)
        for c in scatters.pop(len(work) - 1):
            c.wait()

    return scatter(*tables, *slots_a, *slots_b)


def _experts_kernel(first_ref, nblk_ref, cnt_ref, tot_ref, xs_hbm, wg_ref, wu_ref, wd_ref, ys_hbm,
                    xbuf, ybuf, wg_bf, wu_bf, wd_bf, in_sem, out_sem):
    e = pl.program_id(0)
    nb = nblk_ref[e]
    first = first_ref[e]
    cnt = cnt_ref[e]
    total = tot_ref[0]
    half = D_MODEL // 2

    def in_copy(gb):
        slot = lax.rem(gb, X_BUFS)
        return pltpu.make_async_copy(xs_hbm.at[pl.ds(gb * BM, BM)], xbuf.at[slot], in_sem.at[slot])

    def out_copy(gb):
        slot = lax.rem(gb, Y_BUFS)
        return pltpu.make_async_copy(ybuf.at[slot], ys_hbm.at[pl.ds(gb * BM, BM)], out_sem.at[slot])

    @pl.when(nb > 0)
    def _():
        @pl.when(first == 0)
        def _():
            for k in range(X_LOOKAHEAD):
                @pl.when(k < total)
                def _():
                    in_copy(k).start(priority=ROW_DMA_PRIORITY)

        wg_bf[...] = wg_ref[0].astype(BF16)
        wu_bf[...] = wu_ref[0].astype(BF16)
        wd_bf[...] = wd_ref[0].astype(BF16)

        def block(j, carry):
            gb = first + j

            @pl.when(gb + X_LOOKAHEAD < total)
            def _():
                in_copy(gb + X_LOOKAHEAD).start(priority=ROW_DMA_PRIORITY)

            in_copy(gb).wait()

            @pl.when(gb >= Y_BUFS)
            def _():
                out_copy(gb - Y_BUFS).wait()

            live = lax.broadcasted_iota(jnp.int32, (BM, half), 0) < cnt - j * BM
            lo, hi = _unpack_bf16_pairs(jnp.where(live, xbuf[lax.rem(gb, X_BUFS)], jnp.uint32(0)))
            g = _dot(lo, wg_bf[0:half, :]) + _dot(hi, wg_bf[half:, :])
            u = _dot(lo, wu_bf[0:half, :]) + _dot(hi, wu_bf[half:, :])
            hm = (g * _sigmoid(g) * u).astype(BF16)
            y = _dot(hm, wd_bf[...])
            ybuf[lax.rem(gb, Y_BUFS)] = _pack_bf16_pairs(y.astype(BF16).astype(F32))
            out_copy(gb).start(priority=ROW_DMA_PRIORITY)
            return carry

        lax.fori_loop(0, nb, block, 0)

        @pl.when(first + nb == total)
        def _():
            for k in range(Y_BUFS):
                @pl.when(total - 1 - k >= 0)
                def _():
                    out_copy(total - 1 - k).wait()


def _experts(xs, n_rows_out, first_block, n_blocks_e, counts, w_eg, w_eu, w_ed):
    w_map = lambda e, fb, nb, ct, tot: (e, 0, 0)
    half = D_MODEL // 2
    total = jnp.sum(n_blocks_e).astype(jnp.int32).reshape(1)
    return pl.pallas_call(
        _experts_kernel,
        grid_spec=pltpu.PrefetchScalarGridSpec(
            num_scalar_prefetch=4,
            grid=(N_EXPERTS,),
            in_specs=[pl.BlockSpec(memory_space=pl.ANY),
                      pl.BlockSpec((1, D_MODEL, D_EXPERT), w_map),
                      pl.BlockSpec((1, D_MODEL, D_EXPERT), w_map),
                      pl.BlockSpec((1, D_EXPERT, D_MODEL), w_map)],
            out_specs=pl.BlockSpec(memory_space=pl.ANY),
            scratch_shapes=[pltpu.VMEM((X_BUFS, BM, half), jnp.uint32), pltpu.VMEM((Y_BUFS, BM, half), jnp.uint32),
                            pltpu.VMEM((D_MODEL, D_EXPERT), BF16), pltpu.VMEM((D_MODEL, D_EXPERT), BF16),
                            pltpu.VMEM((D_EXPERT, D_MODEL), BF16),
                            pltpu.SemaphoreType.DMA((X_BUFS,)), pltpu.SemaphoreType.DMA((Y_BUFS,))]),
        out_shape=jax.ShapeDtypeStruct((n_rows_out, half), jnp.uint32),
        compiler_params=pltpu.CompilerParams(dimension_semantics=("arbitrary",), vmem_limit_bytes=VMEM_LIMIT),
        name="experts",
    )(first_block, n_blocks_e, counts, total, xs, w_eg, w_eu, w_ed)


def _combine_kernel(x2_ref, y1_ref, y2_ref, rinfo_ref, g_ref, *rest):
    o_ref = rest[-1]
    r = rinfo_ref[...]
    g1, g2 = r[:, 2:3], r[:, 3:4]
    half = D_MODEL // 2
    y1_lo, y1_hi = _unpack_bf16_pairs_f32(y1_ref[...])
    y2_lo, y2_hi = _unpack_bf16_pairs_f32(y2_ref[...])
    x_lo = x2_ref[:, 0:half] + g1 * y1_lo + g2 * y2_lo
    x_hi = x2_ref[:, half:] + g1 * y1_hi + g2 * y2_hi
    ms = (jnp.sum(x_lo * x_lo, axis=-1, keepdims=True) + jnp.sum(x_hi * x_hi, axis=-1, keepdims=True)) / D_MODEL
    inv = lax.rsqrt(ms + EPS)
    o_ref[:, 0:half] = x_lo * inv * g_ref[:, 0:half]
    o_ref[:, half:] = x_hi * inv * g_ref[:, half:]


def _combine(x2, yg, rinfo, g_final, tm, blk1, blk2, n_out, out_blk, prev_out=None):
    n = x2.shape[0]
    in_specs = [pl.BlockSpec((tm, D_MODEL), lambda i: (i, 0)),
                pl.BlockSpec((tm, D_MODEL // 2), lambda i: (blk1 + i, 0)),
                pl.BlockSpec((tm, D_MODEL // 2), lambda i: (blk2 + i, 0)),
                pl.BlockSpec((tm, LOGIT_LANES), lambda i: (i, 0)),
                pl.BlockSpec((1, D_MODEL), lambda i: (0, 0))]
    args = [x2, yg, yg, rinfo, g_final]
    aliases = {}
    if prev_out is not None:
        in_specs.append(pl.BlockSpec(memory_space=pl.ANY))
        args.append(prev_out)
        aliases = {len(args) - 1: 0}
    return pl.pallas_call(
        _combine_kernel,
        grid=(n // tm,),
        in_specs=in_specs,
        out_specs=pl.BlockSpec((tm, D_MODEL), lambda i: (out_blk + i, 0)),
        out_shape=jax.ShapeDtypeStruct((n_out, D_MODEL), F32),
        input_output_aliases=aliases,
        compiler_params=pltpu.CompilerParams(dimension_semantics=("arbitrary",), vmem_limit_bytes=VMEM_LIMIT),
        name="combine",
    )(*args)


def _gather_rows(table, idx):
    return _sc_gather_rows(table, idx)


def _scatter_rows2(tables, slots_a, slots_b, n_rows_out):
    return _sc_scatter_rows2(tables, slots_a, slots_b, n_rows_out, SCATTER_CHUNK)


def kernel(x_prompt, x_sample, mem_prompt, state_conv, cache_mem_k, cache_mem_v, g_mix, w_in, conv_w, conv_b, ln_conv_g, ln_conv_b, ln_v_g, ln_v_b, w_sg, b_sg, w_out, g_mem, w_mk, w_mv, g_xattn, w_xq, w_xo, g_ffn, w_router_group, b_router_group, w_router_expert, b_router_expert, w_expert_gate, w_expert_up, w_expert_down, g_final):
    assert x_prompt.shape[0] == 1 and g_mix.shape[0] == 1
    n_p = x_prompt.shape[1]
    n_batch, t_len = x_sample.shape[0], x_sample.shape[1]
    n_s = n_batch * t_len
    row = lambda a: a.reshape(1, -1)

    w_router = jnp.concatenate(
        [w_router_group[0], jnp.transpose(w_router_expert[0], (1, 0, 2)).reshape(D_MODEL, N_EXPERTS)], axis=1)
    w_router = jnp.pad(w_router, ((0, 0), (0, LOGIT_LANES - w_router.shape[1]))).astype(BF16)
    b_router = jnp.pad(jnp.concatenate([b_router_group[0], b_router_expert[0].reshape(-1)]),
                       (0, LOGIT_LANES - N_GROUPS - N_EXPERTS)).reshape(1, LOGIT_LANES)
    tril_t = jnp.tril(jnp.ones((t_len, t_len), bool))
    w_sg_t = jnp.where(tril_t, w_sg[0][:, :t_len, :t_len], 0.0)
    eye_b = jnp.eye(n_batch, dtype=F32)
    w_sg_bd = jnp.einsum("ab,hij->haibj", eye_b, w_sg_t).reshape(SG_HEADS, n_s, n_s).astype(BF16)
    p = {
        "g_mix": row(g_mix[0]), "w_in": w_in[0].astype(BF16),
        "conv_w": jnp.pad(conv_w[0], ((0, 1), (0, 0))), "conv_b": row(conv_b[0]),
        "ln_conv_g": row(ln_conv_g[0]), "ln_conv_b": row(ln_conv_b[0]),
        "ln_v_g": row(ln_v_g[0]), "ln_v_b": row(ln_v_b[0]),
        "w_sg": w_sg[0],
        "b_sg_rows": jnp.repeat(b_sg[0].T, SG_HEAD_DIM, axis=1),
        "w_sg_bd": w_sg_bd,
        "b_sg_rows_s": jnp.tile(jnp.repeat(b_sg[0][:, :t_len].T, SG_HEAD_DIM, axis=1), (n_batch, 1)),
        "w_out": w_out[0].astype(BF16), "g_xattn": row(g_xattn[0]),
        "w_xq": w_xq[0].astype(BF16), "w_xo": w_xo[0].astype(BF16), "g_ffn": row(g_ffn[0]),
        "w_router": w_router, "b_router": b_router,
    }

    k_p, v_p = _memkv(mem_prompt[0], row(g_mem[0]), w_mk[0].astype(BF16), w_mv[0].astype(BF16))
    p["kt"] = k_p.T.astype(BF16)
    p["v"] = v_p.astype(BF16)
    p["kt_s"] = jnp.transpose(cache_mem_k[0].reshape(n_batch, N_MEM, D_MODEL), (0, 2, 1)).astype(BF16)
    p["v_s"] = cache_mem_v[0].reshape(n_batch, N_MEM, D_MODEL).astype(BF16)

    n_seg = n_p // 2
    assert n_seg % TM == 0 and n_seg % n_s == 0
    xp = x_prompt[0]
    hist0 = jnp.zeros((HALO, D_CONV), F32)
    x2_a, h3_a, rinfo_a, rt_a, hist_a, cnt_a = _trunk_prompt(xp, 0, n_seg, hist0, p)
    x2_b, h3_b, rinfo_b, rt_b, hist_p, cnt_b = _trunk_prompt(xp, n_seg, n_seg, hist_a, p)
    x2_s, h3_s, rinfo_s, rt_s, hist_s, sgv_s, cnt_bs = _trunk_sample(
        x_sample.reshape(n_s, D_MODEL), state_conv[0], cnt_b, p, n_batch, t_len)

    experts = jnp.arange(N_EXPERTS, dtype=jnp.int32)
    w_e = (w_expert_gate[0], w_expert_up[0], w_expert_down[0])

    def moe_pass(cnt, h3_tables, rts, n_real):
        n_tot = sum(n_real)
        n_slots = -(-(n_tot * 2) // BM) * BM + N_EXPERTS * BM
        counts = cnt[0, :N_EXPERTS].astype(jnp.int32)
        padded = (counts + BM - 1) // BM * BM
        pad_start = jnp.cumsum(padded) - padded

        def one(e_row, rank_row):
            e = e_row.astype(jnp.int32)
            start = jnp.sum(jnp.where(e[None, :] == experts[:, None], pad_start[:, None], 0), axis=0)
            return start + rank_row.astype(jnp.int32)

        slots = [(one(rt[0], rt[4]), one(rt[1], rt[5])) for rt in rts]
        sa, sb, spare0 = [], [], n_slots
        for tab, (a, b), n in zip(h3_tables, slots, n_real):
            n_spare = tab.shape[0] - n
            spare = spare0 + jnp.arange(n_spare, dtype=jnp.int32)
            sa.append(jnp.concatenate([a, spare]))
            sb.append(jnp.concatenate([b, spare + n_spare]))
            spare0 += 2 * n_spare
        xs = _scatter_rows2(tuple(h3_tables), tuple(sa), tuple(sb), spare0)
        ys = _experts(xs, n_slots, pad_start // BM, padded // BM, counts, *w_e)
        back_idx = jnp.concatenate([s for ab in slots for s in ab])
        n_back = -(-back_idx.shape[0] // (SC_WORKERS * GATHER_ROWS)) * (SC_WORKERS * GATHER_ROWS)
        return _gather_rows(ys, jnp.pad(back_idx, (0, n_back - back_idx.shape[0])))

    yg_a = moe_pass(cnt_a, [h3_a], [rt_a], [n_seg])
    yg_b = moe_pass(cnt_bs, [h3_b, h3_s], [rt_b, rt_s], [n_seg, n_s])

    gf = row(g_final)
    y_p = _combine(x2_a, yg_a, rinfo_a, gf, TM, 0, n_seg // TM, n_p, 0)
    y_p = _combine(x2_b, yg_b, rinfo_b, gf, TM, 0, n_seg // TM, n_p, n_seg // TM, prev_out=y_p)
    y_s = _combine(x2_s, yg_b, rinfo_s, gf, n_s, 2 * n_seg // n_s, 2 * n_seg // n_s + 1, n_s, 0)

    return (y_p.reshape(1, n_p, D_MODEL),
            y_s.reshape(n_batch, t_len, D_MODEL),
            hist_p[HALO - HIST:].reshape(1, 1, HIST, D_CONV),
            hist_s.reshape(1, n_batch, HIST, D_CONV),
            k_p.reshape(1, 1, N_MEM, X_HEADS, X_HEAD_DIM),
            v_p.reshape(1, 1, N_MEM, X_HEADS, X_HEAD_DIM),
            sgv_s.reshape(1, n_batch, t_len, D_SG))
```

```python
import functools

import jax
import jax.numpy as jnp
from jax import lax
from jax.experimental import pallas as pl
from jax.experimental.pallas import tpu as pltpu
from jax.experimental.pallas import tpu_sc as plsc

D_MODEL = 1024
D_CONV = 512
D_SG = 512
CONV_WIDTH = 31
HIST = CONV_WIDTH - 1
SG_HEADS = 4
SG_HEAD_DIM = 128
SG_CHUNK = 128
N_MEM = 256
X_HEADS = 4
X_HEAD_DIM = 256
N_GROUPS = 4
EXPERTS_PER_GROUP = 8
N_EXPERTS = 32
D_EXPERT = 512
EPS = 1e-6

LANES = 128
SUBLANES = 8
SC_CORES = 2
SC_SUBCORES = 16
SC_WORKERS = SC_CORES * SC_SUBCORES
VMEM_LIMIT = 56 * 1024 * 1024

TM = 512
HALO = 32
CONV_ROWS = 128
CAST_ROWS = 64
BM = 256
X_LOOKAHEAD = 4
X_BUFS = X_LOOKAHEAD + 2
Y_BUFS = 4
ROW_DMA_PRIORITY = 1
GATHER_ROWS = 32
GATHER_BUFS = 6
SCATTER_CHUNK = 64
LOGIT_LANES = 128

F32 = jnp.float32
BF16 = jnp.bfloat16


def _dot(a, b):
    return jnp.dot(a, b, preferred_element_type=F32)


def _rms(x, g):
    return x * lax.rsqrt(jnp.mean(x * x, axis=-1, keepdims=True) + EPS) * g


def _ln(x, g, b):
    mu = jnp.mean(x, axis=-1, keepdims=True)
    xc = x - mu
    var = jnp.mean(xc * xc, axis=-1, keepdims=True)
    return xc * lax.rsqrt(var + EPS) * g + b


def _sigmoid(x):
    return 1.0 / (1.0 + jnp.exp(-x))


def _pack_bf16_pairs(h):
    bits = lax.bitcast_convert_type(h, jnp.uint32)
    half = h.shape[1] // 2
    lo = lax.shift_right_logical(bits[:, :half], jnp.uint32(16))
    hi = bits[:, half:] & jnp.uint32(0xFFFF0000)
    return hi | lo


def _unpack_bf16_pairs_f32(p):
    lo = lax.bitcast_convert_type(lax.shift_left(p, jnp.uint32(16)), F32)
    hi = lax.bitcast_convert_type(p & jnp.uint32(0xFFFF0000), F32)
    return lo, hi


def _unpack_bf16_pairs(p):
    lo, hi = _unpack_bf16_pairs_f32(p)
    return lo.astype(BF16), hi.astype(BF16)


def _memkv_kernel(mem_ref, g_ref, wk_ref, wv_ref, k_ref, v_ref):
    m = _rms(mem_ref[...], g_ref[...]).astype(BF16)
    k_ref[...] = _dot(m, wk_ref[...].astype(BF16))
    v_ref[...] = _dot(m, wv_ref[...].astype(BF16))


def _memkv(mem, g_mem, w_mk, w_mv):
    return pl.pallas_call(
        _memkv_kernel,
        out_shape=(jax.ShapeDtypeStruct((N_MEM, D_MODEL), F32), jax.ShapeDtypeStruct((N_MEM, D_MODEL), F32)),
        compiler_params=pltpu.CompilerParams(vmem_limit_bytes=VMEM_LIMIT),
        name="memkv",
    )(mem, g_mem, w_mk, w_mv)


def _attn_heads(q, kt, v):
    outs = []
    for h in range(X_HEADS):
        sl = slice(h * X_HEAD_DIM, (h + 1) * X_HEAD_DIM)
        s = _dot(q[:, sl], kt[sl, :]) * (X_HEAD_DIM ** -0.5)
        s = s - jnp.max(s, axis=-1, keepdims=True)
        p = jnp.exp(s)
        p = p / jnp.sum(p, axis=-1, keepdims=True)
        outs.append(_dot(p.astype(BF16), v[:, sl]).astype(BF16))
    return jnp.concatenate(outs, axis=1)


def _route(logits, run):
    m = logits.shape[0]
    lane = lax.broadcasted_iota(jnp.int32, (m, LOGIT_LANES), 1).astype(F32)
    neg = jnp.float32(-jnp.inf)
    big = jnp.float32(LOGIT_LANES)

    def first_argmax(vals):
        mx = jnp.max(vals, axis=-1, keepdims=True)
        idx = jnp.min(jnp.where(vals == mx, lane, big), axis=-1, keepdims=True)
        return mx, idx

    lg = jnp.where(lane < N_GROUPS, logits, neg)
    g_max, g_idx = first_argmax(lg)
    g_w = 1.0 / jnp.sum(jnp.exp(lg - g_max), axis=-1, keepdims=True)

    lo = N_GROUPS + g_idx * EXPERTS_PER_GROUP
    le = jnp.where((lane >= lo) & (lane < lo + EXPERTS_PER_GROUP), logits, neg)
    v1, i1 = first_argmax(le)
    v2, i2 = first_argmax(jnp.where(lane == i1, neg, le))
    t = jnp.exp(v2 - v1)
    gate1 = g_w / (1.0 + t)
    gate2 = g_w * t / (1.0 + t)
    e1 = i1 - N_GROUPS
    e2 = i2 - N_GROUPS

    oh1 = (lane == e1).astype(F32)
    oh2 = (lane == e2).astype(F32)
    oh = oh1 + oh2
    row = lax.broadcasted_iota(jnp.int32, (m, m), 0)
    col = lax.broadcasted_iota(jnp.int32, (m, m), 1)
    strict_lower = (col < row).astype(BF16)
    before = _dot(strict_lower, oh.astype(BF16)) + run
    rank1 = jnp.sum(before * oh1, axis=-1, keepdims=True)
    rank2 = jnp.sum(before * oh2, axis=-1, keepdims=True)
    new_run = run + jnp.sum(oh, axis=0, keepdims=True)

    rinfo = jnp.where(lane == 0, e1,
            jnp.where(lane == 1, e2,
            jnp.where(lane == 2, gate1,
            jnp.where(lane == 3, gate2,
            jnp.where(lane == 4, rank1,
            jnp.where(lane == 5, rank2, 0.0))))))
    return rinfo, new_run


def _conv_taps(win, w_ref, l0, rows):
    n = win.shape[0]
    acc = jnp.zeros((rows, LANES), F32)
    for r in range(SUBLANES):
        d = HALO - HIST + r
        if d % SUBLANES == 0:
            e, base = win, d
        else:
            e, base = pltpu.roll(win, n - d, axis=0), 0
        for qq in range(-(-CONV_WIDTH // SUBLANES)):
            k = SUBLANES * qq + r
            if k < CONV_WIDTH:
                s0 = base + SUBLANES * qq
                acc = acc + e[s0:s0 + rows, :] * w_ref[pl.ds(k, 1), pl.ds(l0, LANES)]
    return acc


def _cast_rows(src_ref, dst_ref):
    rows = src_ref.shape[0]

    def body(c, carry):
        r0 = pl.multiple_of(c * CAST_ROWS, CAST_ROWS)
        dst_ref[pl.ds(r0, CAST_ROWS), :] = src_ref[pl.ds(r0, CAST_ROWS), :].astype(BF16)
        return carry

    lax.fori_loop(0, rows // CAST_ROWS, body, 0)


def _trunk_prompt_kernel(x_ref, gmix_ref, win32_ref, convw_ref, convb_ref, lncg_ref, lncb_ref, lnvg_ref, lnvb_ref,
                         wsg_ref, bsg_ref, wout32_ref, gx_ref, wxq32_ref, kt_ref, v_ref, wxo32_ref, gffn_ref, wr_ref,
                         br_ref,
                         x2_ref, h3_ref, rinfo_ref, rt_ref, hist_ref, cnt_ref,
                         ext_ref, conv_ref, run_ref, win_ref, wout_ref, wxq_ref, wxo_ref):
    i = pl.program_id(0)

    @pl.when(i == 0)
    def _():
        ext_ref[0:HALO, :] = jnp.zeros((HALO, D_CONV), F32)
        run_ref[...] = jnp.zeros((1, LOGIT_LANES), F32)
        _cast_rows(win32_ref, win_ref)
        _cast_rows(wout32_ref, wout_ref)
        _cast_rows(wxq32_ref, wxq_ref)
        _cast_rows(wxo32_ref, wxo_ref)

    x = x_ref[...]
    h = _rms(x, gmix_ref[...]).astype(BF16)

    a_in = _dot(h, win_ref[:, 0:D_CONV])
    a_gate = _dot(h, win_ref[:, D_CONV:2 * D_CONV])
    ext_ref[HALO:HALO + TM, :] = a_in * _sigmoid(a_gate)

    for r0 in range(0, TM, CONV_ROWS):
        for l0 in range(0, D_CONV, LANES):
            win = ext_ref[r0:r0 + CONV_ROWS + HALO, l0:l0 + LANES]
            conv_ref[r0:r0 + CONV_ROWS, l0:l0 + LANES] = _conv_taps(win, convw_ref, l0, CONV_ROWS)
    hist_ref[...] = ext_ref[TM:TM + HALO, :]
    ext_ref[0:HALO, :] = ext_ref[TM:TM + HALO, :]

    y = _ln(conv_ref[...] + convb_ref[...], lncg_ref[...], lncb_ref[...])
    a_out = (y * _sigmoid(y)).astype(BF16)

    u = _dot(h, win_ref[:, 2 * D_CONV:2 * D_CONV + D_SG])
    v = _ln(_dot(h, win_ref[:, 2 * D_CONV + D_SG:]), lnvg_ref[...], lnvb_ref[...]).astype(BF16)
    ri = lax.broadcasted_iota(jnp.int32, (SG_CHUNK, SG_CHUNK), 0)
    ci = lax.broadcasted_iota(jnp.int32, (SG_CHUNK, SG_CHUNK), 1)
    w_tril = [jnp.where(ci <= ri, wsg_ref[hh], 0.0).astype(BF16) for hh in range(SG_HEADS)]
    gate_rows = []
    for c in range(TM // SG_CHUNK):
        rs = slice(c * SG_CHUNK, (c + 1) * SG_CHUNK)
        heads = [_dot(w_tril[hh], v[rs, hh * SG_HEAD_DIM:(hh + 1) * SG_HEAD_DIM]) for hh in range(SG_HEADS)]
        gate_rows.append(jnp.concatenate(heads, axis=1) + bsg_ref[...])
    b_out = (u * jnp.concatenate(gate_rows, axis=0)).astype(BF16)

    x1 = x + _dot(a_out, wout_ref[0:D_CONV, :]) + _dot(b_out, wout_ref[D_CONV:, :])

    hx = _rms(x1, gx_ref[...]).astype(BF16)
    q = _dot(hx, wxq_ref[...]).astype(BF16)
    x2 = x1 + _dot(_attn_heads(q, kt_ref[...], v_ref[...]), wxo_ref[...])
    x2_ref[...] = x2

    h3 = _rms(x2, gffn_ref[...]).astype(BF16)
    h3_ref[...] = _pack_bf16_pairs(h3.astype(F32))
    rinfo, new_run = _route(_dot(h3, wr_ref[...]) + br_ref[...], run_ref[...])
    rinfo_ref[...] = rinfo
    rt_ref[...] = jnp.transpose(rinfo)[0:SUBLANES, :]
    run_ref[...] = new_run
    cnt_ref[...] = new_run


def _const_spec(shape):
    nd = len(shape)
    return pl.BlockSpec(shape, lambda i: (0,) * nd, pipeline_mode=pl.Buffered(1))


def _trunk_prompt(x, p):
    n = x.shape[0]
    assert n % TM == 0
    row = lambda w: pl.BlockSpec((TM, w), lambda i: (i, 0))
    consts = [p["g_mix"], p["w_in"], p["conv_w"], p["conv_b"], p["ln_conv_g"], p["ln_conv_b"], p["ln_v_g"],
              p["ln_v_b"], p["w_sg"], p["b_sg_rows"], p["w_out"], p["g_xattn"], p["w_xq"], p["kt"], p["v"],
              p["w_xo"], p["g_ffn"], p["w_router"], p["b_router"]]
    return pl.pallas_call(
        _trunk_prompt_kernel,
        grid=(n // TM,),
        in_specs=[row(D_MODEL)] + [_const_spec(c.shape) for c in consts],
        out_specs=(row(D_MODEL), row(D_MODEL // 2), row(LOGIT_LANES),
                   pl.BlockSpec((SUBLANES, TM), lambda i: (0, i)),
                   pl.BlockSpec((HALO, D_CONV), lambda i: (0, 0)),
                   pl.BlockSpec((1, LOGIT_LANES), lambda i: (0, 0))),
        out_shape=(jax.ShapeDtypeStruct((n, D_MODEL), F32),
                   jax.ShapeDtypeStruct((n, D_MODEL // 2), jnp.uint32),
                   jax.ShapeDtypeStruct((n, LOGIT_LANES), F32),
                   jax.ShapeDtypeStruct((SUBLANES, n), F32),
                   jax.ShapeDtypeStruct((HALO, D_CONV), F32),
                   jax.ShapeDtypeStruct((1, LOGIT_LANES), F32)),
        scratch_shapes=[pltpu.VMEM((TM + HALO, D_CONV), F32),
                        pltpu.VMEM((TM, D_CONV), F32),
                        pltpu.VMEM((1, LOGIT_LANES), F32),
                        pltpu.VMEM(p["w_in"].shape, BF16), pltpu.VMEM(p["w_out"].shape, BF16),
                        pltpu.VMEM(p["w_xq"].shape, BF16), pltpu.VMEM(p["w_xo"].shape, BF16)],
        compiler_params=pltpu.CompilerParams(dimension_semantics=("arbitrary",), vmem_limit_bytes=VMEM_LIMIT),
        name="trunk_prompt",
    )(x, *consts)


def _trunk_sample_kernel(n_batch, t_len,
                         x_ref, hist_in_ref, run_in_ref, gmix_ref, win_ref, convw_ref, convb_ref, lncg_ref, lncb_ref,
                         lnvg_ref, lnvb_ref, wsgbd_ref, bsg_ref, wout_ref, gx_ref, wxq_ref, kt_ref, v_ref, wxo_ref,
                         gffn_ref, wr_ref, br_ref,
                         x2_ref, h3_ref, rinfo_ref, rt_ref, hist_ref, sgv_ref, cnt_ref,
                         ext_ref, conv_ref, att_ref):
    x = x_ref[...]
    h = _rms(x, gmix_ref[...]).astype(BF16)
    z = _dot(h, win_ref[...].astype(BF16))
    a = z[:, 0:D_CONV] * _sigmoid(z[:, D_CONV:2 * D_CONV])
    ext_len = HIST + t_len
    for b in range(n_batch):
        ext_ref[b, 0:HIST, :] = hist_in_ref[b]
        ext_ref[b, HIST:ext_len, :] = a[b * t_len:(b + 1) * t_len, :]
    for b in range(n_batch):
        acc = jnp.zeros((t_len, D_CONV), F32)
        for k in range(CONV_WIDTH):
            acc = acc + ext_ref[b, k:k + t_len, :] * convw_ref[k:k + 1, :]
        conv_ref[b * t_len:(b + 1) * t_len, :] = acc
        hist_ref[b] = ext_ref[b, ext_len - HIST:ext_len, :]

    y = _ln(conv_ref[...] + convb_ref[...], lncg_ref[...], lncb_ref[...])
    a_out = (y * _sigmoid(y)).astype(BF16)

    u = z[:, 2 * D_CONV:2 * D_CONV + D_SG]
    v = _ln(z[:, 2 * D_CONV + D_SG:], lnvg_ref[...], lnvb_ref[...])
    sgv_ref[...] = v
    vb = v.astype(BF16)
    heads = [_dot(wsgbd_ref[hh], vb[:, hh * SG_HEAD_DIM:(hh + 1) * SG_HEAD_DIM]) for hh in range(SG_HEADS)]
    b_out = (u * (jnp.concatenate(heads, axis=1) + bsg_ref[...])).astype(BF16)

    x1 = (x + _dot(a_out, wout_ref[0:D_CONV, :].astype(BF16))
          + _dot(b_out, wout_ref[D_CONV:, :].astype(BF16)))

    hx = _rms(x1, gx_ref[...]).astype(BF16)
    q = _dot(hx, wxq_ref[...].astype(BF16)).astype(BF16)
    for b in range(n_batch):
        rs = slice(b * t_len, (b + 1) * t_len)
        att_ref[rs, :] = _attn_heads(q[rs, :], kt_ref[b], v_ref[b])
    x2 = x1 + _dot(att_ref[...], wxo_ref[...].astype(BF16))
    x2_ref[...] = x2

    h3 = _rms(x2, gffn_ref[...]).astype(BF16)
    m = n_batch * t_len
    h3_ref[0:m, :] = _pack_bf16_pairs(h3.astype(F32))
    if h3_ref.shape[0] > m:
        h3_ref[m:, :] = jnp.zeros((h3_ref.shape[0] - m, D_MODEL // 2), jnp.uint32)
    rinfo, new_run = _route(_dot(h3, wr_ref[...]) + br_ref[...], run_in_ref[...])
    rinfo_ref[...] = rinfo
    rt_ref[...] = jnp.transpose(rinfo)[0:SUBLANES, :]
    cnt_ref[...] = new_run


def _trunk_sample(x, hist, run, p, n_batch, t_len):
    m = n_batch * t_len
    args = [x, hist, run, p["g_mix"], p["w_in"], p["conv_w"], p["conv_b"], p["ln_conv_g"], p["ln_conv_b"],
            p["ln_v_g"], p["ln_v_b"], p["w_sg_bd"], p["b_sg_rows_s"], p["w_out"], p["g_xattn"], p["w_xq"],
            p["kt_s"], p["v_s"], p["w_xo"], p["g_ffn"], p["w_router"], p["b_router"]]
    return pl.pallas_call(
        functools.partial(_trunk_sample_kernel, n_batch, t_len),
        out_shape=(jax.ShapeDtypeStruct((m, D_MODEL), F32),
                   jax.ShapeDtypeStruct((-(-m // (SC_WORKERS * SUBLANES)) * SC_WORKERS * SUBLANES, D_MODEL // 2),
                                        jnp.uint32),
                   jax.ShapeDtypeStruct((m, LOGIT_LANES), F32),
                   jax.ShapeDtypeStruct((SUBLANES, m), F32),
                   jax.ShapeDtypeStruct((n_batch, HIST, D_CONV), F32),
                   jax.ShapeDtypeStruct((m, D_SG), F32),
                   jax.ShapeDtypeStruct((1, LOGIT_LANES), F32)),
        scratch_shapes=[pltpu.VMEM((n_batch, HIST + t_len, D_CONV), F32),
                        pltpu.VMEM((m, D_CONV), F32),
                        pltpu.VMEM((m, D_MODEL), BF16)],
        compiler_params=pltpu.CompilerParams(vmem_limit_bytes=VMEM_LIMIT),
        name="trunk_sample",
    )(*args)


def _sc_worker_id():
    return lax.axis_index("s") * SC_CORES + lax.axis_index("c")


def _sc_chunk(per_w, max_chunk):
    assert per_w % SUBLANES == 0 and max_chunk <= LANES
    return max(c for c in range(SUBLANES, max_chunk + 1, SUBLANES) if per_w % c == 0)


def _sc_gather_rows(table, idx):
    n_rows, d = idx.shape[0], table.shape[1]
    per_w = n_rows // SC_WORKERS
    assert per_w * SC_WORKERS == n_rows and per_w % GATHER_ROWS == 0
    n_chunks = per_w // GATHER_ROWS
    lag = GATHER_BUFS // 2
    mesh = plsc.VectorSubcoreMesh(core_axis_name="c", subcore_axis_name="s")

    @functools.partial(
        pl.kernel, mesh=mesh,
        out_type=jax.ShapeDtypeStruct((n_rows, d), table.dtype),
        scratch_types=([pltpu.VMEM((per_w,), jnp.int32)]
                       + [pltpu.VMEM((GATHER_ROWS, d), table.dtype)] * GATHER_BUFS
                       + [pltpu.SemaphoreType.DMA] * (2 * GATHER_BUFS)),
    )
    def gather(table_hbm, idx_hbm, out_hbm, idx_all, *rest):
        rows = rest[:GATHER_BUFS]
        gsem = rest[GATHER_BUFS:2 * GATHER_BUFS]
        wsem = rest[2 * GATHER_BUFS:]
        base = _sc_worker_id() * per_w
        pltpu.sync_copy(idx_hbm.at[pl.ds(base, per_w)], idx_all)

        reads, writes = {}, {}
        for c in range(n_chunks + lag):
            if c < n_chunks:
                b = c % GATHER_BUFS
                if c >= GATHER_BUFS:
                    writes.pop(c - GATHER_BUFS).wait()
                reads[c] = pltpu.async_copy(
                    table_hbm.at[idx_all.at[pl.ds(c * GATHER_ROWS, GATHER_ROWS)]], rows[b], gsem[b])
            w = c - lag
            if w >= 0:
                b = w % GATHER_BUFS
                reads.pop(w).wait()
                writes[w] = pltpu.async_copy(
                    rows[b], out_hbm.at[pl.ds(base + w * GATHER_ROWS, GATHER_ROWS)], wsem[b])
        for w in sorted(writes):
            writes[w].wait()

    return gather(table, idx)


def _sc_scatter_rows2(tables, slots_a, slots_b, n_rows_out, max_chunk):
    d, dtype = tables[0].shape[1], tables[0].dtype
    plans = []
    for t in tables:
        per_w = t.shape[0] // SC_WORKERS
        assert per_w * SC_WORKERS == t.shape[0]
        chunk = _sc_chunk(per_w, max_chunk)
        plans.append((per_w, chunk, per_w // chunk))
    cmax = max(c for _, c, _ in plans)
    n_t = len(tables)
    mesh = plsc.VectorSubcoreMesh(core_axis_name="c", subcore_axis_name="s")

    scratch = []
    for _, chunk, _ in plans:
        for _ in range(2):
            scratch += [pltpu.VMEM((chunk,), jnp.int32), pltpu.VMEM((chunk,), jnp.int32)]
    scratch += [pltpu.VMEM((cmax, d), dtype), pltpu.VMEM((cmax, d), dtype)]
    scratch += [pltpu.SemaphoreType.DMA] * 4

    @functools.partial(pl.kernel, mesh=mesh, out_type=jax.ShapeDtypeStruct((n_rows_out, d), dtype),
                       scratch_types=scratch)
    def scatter(*refs):
        tab_hbm = refs[0:n_t]
        sa_hbm = refs[n_t:2 * n_t]
        sb_hbm = refs[2 * n_t:3 * n_t]
        out_hbm = refs[3 * n_t]
        sc = refs[3 * n_t + 1:]
        idx_refs = sc[:4 * n_t]
        rows = sc[4 * n_t:4 * n_t + 2]
        lsem = sc[4 * n_t + 2:4 * n_t + 4]
        ssem = sc[4 * n_t + 4:4 * n_t + 6]
        wid = _sc_worker_id()

        work = []
        for t, (per_w, chunk, n_chunks) in enumerate(plans):
            for j in range(n_chunks):
                work.append((t, wid * per_w + j * chunk, chunk))

        def parts(k):
            t, off, chunk = work[k]
            b = k % 2
            ia, ib = idx_refs[4 * t + 2 * b], idx_refs[4 * t + 2 * b + 1]
            rv = rows[b] if chunk == cmax else rows[b].at[pl.ds(0, chunk)]
            return t, off, chunk, b, ia, ib, rv

        def start_load(k):
            t, off, chunk, b, ia, ib, rv = parts(k)
            return (pltpu.async_copy(tab_hbm[t].at[pl.ds(off, chunk)], rv, lsem[b]),
                    pltpu.async_copy(sa_hbm[t].at[pl.ds(off, chunk)], ia, lsem[b]),
                    pltpu.async_copy(sb_hbm[t].at[pl.ds(off, chunk)], ib, lsem[b]))

        def start_scatter(k):
            t, off, chunk, b, ia, ib, rv = parts(k)
            return (pltpu.async_copy(rv, out_hbm.at[ia], ssem[b]), pltpu.async_copy(rv, out_hbm.at[ib], ssem[b]))

        loads = {0: start_load(0)}
        scatters = {}
        for k in range(len(work)):
            for c in loads.pop(k):
                c.wait()
            scatters[k] = start_scatter(k)
            if k >= 1:
                for c in scatters.pop(k - 1):
                    c.wait()
            if k + 1 < len(work):
                loads[k + 1] = start_load(k + 1)
        for c in scatters.pop(len(work) - 1):
            c.wait()

    return scatter(*tables, *slots_a, *slots_b)


def _experts_kernel(first_ref, nblk_ref, cnt_ref, tot_ref, xs_hbm, wg_ref, wu_ref, wd_ref, ys_hbm,
                    xbuf, ybuf, wg_bf, wu_bf, wd_bf, in_sem, out_sem):
    e = pl.program_id(0)
    nb = nblk_ref[e]
    first = first_ref[e]
    cnt = cnt_ref[e]
    total = tot_ref[0]
    half = D_MODEL // 2

    def in_copy(gb):
        slot = lax.rem(gb, X_BUFS)
        return pltpu.make_async_copy(xs_hbm.at[pl.ds(gb * BM, BM)], xbuf.at[slot], in_sem.at[slot])

    def out_copy(gb):
        slot = lax.rem(gb, Y_BUFS)
        return pltpu.make_async_copy(ybuf.at[slot], ys_hbm.at[pl.ds(gb * BM, BM)], out_sem.at[slot])

    @pl.when(nb > 0)
    def _():
        @pl.when(first == 0)
        def _():
            for k in range(X_LOOKAHEAD):
                @pl.when(k < total)
                def _():
                    in_copy(k).start(priority=ROW_DMA_PRIORITY)

        wg_bf[...] = wg_ref[0].astype(BF16)
        wu_bf[...] = wu_ref[0].astype(BF16)
        wd_bf[...] = wd_ref[0].astype(BF16)

        def acquire(gb):
            @pl.when(gb + X_LOOKAHEAD < total)
            def _():
                in_copy(gb + X_LOOKAHEAD).start(priority=ROW_DMA_PRIORITY)

            in_copy(gb).wait()

            @pl.when(gb >= Y_BUFS)
            def _():
                out_copy(gb - Y_BUFS).wait()

        def ffn(gb, j):
            live = lax.broadcasted_iota(jnp.int32, (BM, half), 0) < cnt - j * BM
            lo, hi = _unpack_bf16_pairs(jnp.where(live, xbuf[lax.rem(gb, X_BUFS)], jnp.uint32(0)))
            g = _dot(lo, wg_bf[0:half, :]) + _dot(hi, wg_bf[half:, :])
            u = _dot(lo, wu_bf[0:half, :]) + _dot(hi, wu_bf[half:, :])
            hm = (g * _sigmoid(g) * u).astype(BF16)
            y = _dot(hm, wd_bf[...])
            ybuf[lax.rem(gb, Y_BUFS)] = _pack_bf16_pairs(y.astype(BF16).astype(F32))

        def block_pair(jp, carry):
            j0 = 2 * jp
            g0 = first + j0
            acquire(g0)
            acquire(g0 + 1)
            ffn(g0, j0)
            ffn(g0 + 1, j0 + 1)
            out_copy(g0).start(priority=ROW_DMA_PRIORITY)
            out_copy(g0 + 1).start(priority=ROW_DMA_PRIORITY)
            return carry

        lax.fori_loop(0, nb // 2, block_pair, 0)

        @pl.when(lax.rem(nb, 2) == 1)
        def _():
            gl = first + nb - 1
            acquire(gl)
            ffn(gl, nb - 1)
            out_copy(gl).start(priority=ROW_DMA_PRIORITY)

        @pl.when(first + nb == total)
        def _():
            for k in range(Y_BUFS):
                @pl.when(total - 1 - k >= 0)
                def _():
                    out_copy(total - 1 - k).wait()


def _experts(xs, n_rows_out, first_block, n_blocks_e, counts, w_eg, w_eu, w_ed):
    w_map = lambda e, fb, nb, ct, tot: (e, 0, 0)
    half = D_MODEL // 2
    total = jnp.sum(n_blocks_e).astype(jnp.int32).reshape(1)
    return pl.pallas_call(
        _experts_kernel,
        grid_spec=pltpu.PrefetchScalarGridSpec(
            num_scalar_prefetch=4,
            grid=(N_EXPERTS,),
            in_specs=[pl.BlockSpec(memory_space=pl.ANY),
                      pl.BlockSpec((1, D_MODEL, D_EXPERT), w_map),
                      pl.BlockSpec((1, D_MODEL, D_EXPERT), w_map),
                      pl.BlockSpec((1, D_EXPERT, D_MODEL), w_map)],
            out_specs=pl.BlockSpec(memory_space=pl.ANY),
            scratch_shapes=[pltpu.VMEM((X_BUFS, BM, half), jnp.uint32), pltpu.VMEM((Y_BUFS, BM, half), jnp.uint32),
                            pltpu.VMEM((D_MODEL, D_EXPERT), BF16), pltpu.VMEM((D_MODEL, D_EXPERT), BF16),
                            pltpu.VMEM((D_EXPERT, D_MODEL), BF16),
                            pltpu.SemaphoreType.DMA((X_BUFS,)), pltpu.SemaphoreType.DMA((Y_BUFS,))]),
        out_shape=jax.ShapeDtypeStruct((n_rows_out, half), jnp.uint32),
        compiler_params=pltpu.CompilerParams(dimension_semantics=("arbitrary",), vmem_limit_bytes=VMEM_LIMIT),
        name="experts",
    )(first_block, n_blocks_e, counts, total, xs, w_eg, w_eu, w_ed)


def _combine_kernel(x2_ref, y1_ref, y2_ref, rinfo_ref, g_ref, o_ref):
    r = rinfo_ref[...]
    g1, g2 = r[:, 2:3], r[:, 3:4]
    half = D_MODEL // 2
    y1_lo, y1_hi = _unpack_bf16_pairs_f32(y1_ref[...])
    y2_lo, y2_hi = _unpack_bf16_pairs_f32(y2_ref[...])
    x_lo = x2_ref[:, 0:half] + g1 * y1_lo + g2 * y2_lo
    x_hi = x2_ref[:, half:] + g1 * y1_hi + g2 * y2_hi
    ms = (jnp.sum(x_lo * x_lo, axis=-1, keepdims=True) + jnp.sum(x_hi * x_hi, axis=-1, keepdims=True)) / D_MODEL
    inv = lax.rsqrt(ms + EPS)
    o_ref[:, 0:half] = x_lo * inv * g_ref[:, 0:half]
    o_ref[:, half:] = x_hi * inv * g_ref[:, half:]


def _combine(x2, yg, rinfo, g_final, tm, blk1, blk2):
    n = x2.shape[0]
    return pl.pallas_call(
        _combine_kernel,
        grid=(n // tm,),
        in_specs=[pl.BlockSpec((tm, D_MODEL), lambda i: (i, 0)),
                  pl.BlockSpec((tm, D_MODEL // 2), lambda i: (blk1 + i, 0)),
                  pl.BlockSpec((tm, D_MODEL // 2), lambda i: (blk2 + i, 0)),
                  pl.BlockSpec((tm, LOGIT_LANES), lambda i: (i, 0)),
                  pl.BlockSpec((1, D_MODEL), lambda i: (0, 0))],
        out_specs=pl.BlockSpec((tm, D_MODEL), lambda i: (i, 0)),
        out_shape=jax.ShapeDtypeStruct((n, D_MODEL), F32),
        compiler_params=pltpu.CompilerParams(dimension_semantics=("arbitrary",), vmem_limit_bytes=VMEM_LIMIT),
        name="combine",
    )(x2, yg, yg, rinfo, g_final)


def _gather_rows(table, idx):
    return _sc_gather_rows(table, idx)


def _scatter_rows2(tables, slots_a, slots_b, n_rows_out):
    return _sc_scatter_rows2(tables, slots_a, slots_b, n_rows_out, SCATTER_CHUNK)


def kernel(x_prompt, x_sample, mem_prompt, state_conv, cache_mem_k, cache_mem_v, g_mix, w_in, conv_w, conv_b, ln_conv_g, ln_conv_b, ln_v_g, ln_v_b, w_sg, b_sg, w_out, g_mem, w_mk, w_mv, g_xattn, w_xq, w_xo, g_ffn, w_router_group, b_router_group, w_router_expert, b_router_expert, w_expert_gate, w_expert_up, w_expert_down, g_final):
    assert x_prompt.shape[0] == 1 and g_mix.shape[0] == 1
    n_p = x_prompt.shape[1]
    n_batch, t_len = x_sample.shape[0], x_sample.shape[1]
    n_s = n_batch * t_len
    row = lambda a: a.reshape(1, -1)

    w_router = jnp.concatenate(
        [w_router_group[0], jnp.transpose(w_router_expert[0], (1, 0, 2)).reshape(D_MODEL, N_EXPERTS)], axis=1)
    w_router = jnp.pad(w_router, ((0, 0), (0, LOGIT_LANES - w_router.shape[1]))).astype(BF16)
    b_router = jnp.pad(jnp.concatenate([b_router_group[0], b_router_expert[0].reshape(-1)]),
                       (0, LOGIT_LANES - N_GROUPS - N_EXPERTS)).reshape(1, LOGIT_LANES)
    tril_t = jnp.tril(jnp.ones((t_len, t_len), bool))
    w_sg_t = jnp.where(tril_t, w_sg[0][:, :t_len, :t_len], 0.0)
    eye_b = jnp.eye(n_batch, dtype=F32)
    w_sg_bd = jnp.einsum("ab,hij->haibj", eye_b, w_sg_t).reshape(SG_HEADS, n_s, n_s).astype(BF16)
    p = {
        "g_mix": row(g_mix[0]), "w_in": w_in[0],
        "conv_w": jnp.pad(conv_w[0], ((0, 1), (0, 0))), "conv_b": row(conv_b[0]),
        "ln_conv_g": row(ln_conv_g[0]), "ln_conv_b": row(ln_conv_b[0]),
        "ln_v_g": row(ln_v_g[0]), "ln_v_b": row(ln_v_b[0]),
        "w_sg": w_sg[0],
        "b_sg_rows": jnp.repeat(b_sg[0].T, SG_HEAD_DIM, axis=1),
        "w_sg_bd": w_sg_bd,
        "b_sg_rows_s": jnp.tile(jnp.repeat(b_sg[0][:, :t_len].T, SG_HEAD_DIM, axis=1), (n_batch, 1)),
        "w_out": w_out[0], "g_xattn": row(g_xattn[0]),
        "w_xq": w_xq[0], "w_xo": w_xo[0], "g_ffn": row(g_ffn[0]),
        "w_router": w_router, "b_router": b_router,
    }

    k_p, v_p = _memkv(mem_prompt[0], row(g_mem[0]), w_mk[0], w_mv[0])
    p["kt"] = k_p.T.astype(BF16)
    p["v"] = v_p.astype(BF16)
    p["kt_s"] = jnp.transpose(cache_mem_k[0].reshape(n_batch, N_MEM, D_MODEL), (0, 2, 1)).astype(BF16)
    p["v_s"] = cache_mem_v[0].reshape(n_batch, N_MEM, D_MODEL).astype(BF16)

    assert n_p % n_s == 0
    x2_p, h3_p, rinfo_p, rt_p, hist_p, cnt_p = _trunk_prompt(x_prompt[0], p)
    x2_s, h3_s, rinfo_s, rt_s, hist_s, sgv_s, cnt = _trunk_sample(
        x_sample.reshape(n_s, D_MODEL), state_conv[0], cnt_p, p, n_batch, t_len)

    experts = jnp.arange(N_EXPERTS, dtype=jnp.int32)
    w_e = (w_expert_gate[0], w_expert_up[0], w_expert_down[0])

    def moe_pass(cnt, h3_tables, rts, n_real):
        n_tot = sum(n_real)
        n_slots = -(-(n_tot * 2) // BM) * BM + N_EXPERTS * BM
        counts = cnt[0, :N_EXPERTS].astype(jnp.int32)
        padded = (counts + BM - 1) // BM * BM
        pad_start = jnp.cumsum(padded) - padded

        def one(e_row, rank_row):
            e = e_row.astype(jnp.int32)
            start = jnp.sum(jnp.where(e[None, :] == experts[:, None], pad_start[:, None], 0), axis=0)
            return start + rank_row.astype(jnp.int32)

        slots = [(one(rt[0], rt[4]), one(rt[1], rt[5])) for rt in rts]
        sa, sb, spare0 = [], [], n_slots
        for tab, (a, b), n in zip(h3_tables, slots, n_real):
            n_spare = tab.shape[0] - n
            spare = spare0 + jnp.arange(n_spare, dtype=jnp.int32)
            sa.append(jnp.concatenate([a, spare]))
            sb.append(jnp.concatenate([b, spare + n_spare]))
            spare0 += 2 * n_spare
        xs = _scatter_rows2(tuple(h3_tables), tuple(sa), tuple(sb), spare0)
        ys = _experts(xs, n_slots, pad_start // BM, padded // BM, counts, *w_e)
        back_idx = jnp.concatenate([s for ab in slots for s in ab])
        n_back = -(-back_idx.shape[0] // (SC_WORKERS * GATHER_ROWS)) * (SC_WORKERS * GATHER_ROWS)
        return _gather_rows(ys, jnp.pad(back_idx, (0, n_back - back_idx.shape[0])))

    yg = moe_pass(cnt, [h3_p, h3_s], [rt_p, rt_s], [n_p, n_s])

    gf = row(g_final)
    y_p = _combine(x2_p, yg, rinfo_p, gf, TM, 0, n_p // TM)
    y_s = _combine(x2_s, yg, rinfo_s, gf, n_s, 2 * n_p // n_s, 2 * n_p // n_s + 1)

    return (y_p.reshape(1, n_p, D_MODEL),
            y_s.reshape(n_batch, t_len, D_MODEL),
            hist_p[HALO - HIST:].reshape(1, 1, HIST, D_CONV),
            hist_s.reshape(1, n_batch, HIST, D_CONV),
            k_p.reshape(1, 1, N_MEM, X_HEADS, X_HEAD_DIM),
            v_p.reshape(1, 1, N_MEM, X_HEADS, X_HEAD_DIM),
            sgv_s.reshape(1, n_batch, t_len, D_SG))
```

```python
import functools

import jax
import jax.numpy as jnp
from jax import lax
from jax.experimental import pallas as pl
from jax.experimental.pallas import tpu as pltpu
from jax.experimental.pallas import tpu_sc as plsc

D_MODEL = 1024
D_CONV = 512
D_SG = 512
CONV_WIDTH = 31
HIST = CONV_WIDTH - 1
SG_HEADS = 4
SG_HEAD_DIM = 128
SG_CHUNK = 128
N_MEM = 256
X_HEADS = 4
X_HEAD_DIM = 256
N_GROUPS = 4
EXPERTS_PER_GROUP = 8
N_EXPERTS = 32
D_EXPERT = 512
EPS = 1e-6

LANES = 128
SUBLANES = 8
SC_CORES = 2
SC_SUBCORES = 16
SC_WORKERS = SC_CORES * SC_SUBCORES
VMEM_LIMIT = 56 * 1024 * 1024

TM = 512
HALO = 32
CONV_ROWS = 128
CAST_ROWS = 64
BM = 256
X_LOOKAHEAD = 4
X_BUFS = X_LOOKAHEAD + 2
Y_BUFS = 4
ROW_DMA_PRIORITY = 1
GATHER_ROWS = 32
GATHER_BUFS = 6
SCATTER_CHUNK = 32
SCATTER_BUFS = 6
LOGIT_LANES = 128

F32 = jnp.float32
BF16 = jnp.bfloat16


def _dot(a, b):
    return jnp.dot(a, b, preferred_element_type=F32)


def _rms(x, g):
    return x * lax.rsqrt(jnp.mean(x * x, axis=-1, keepdims=True) + EPS) * g


def _ln(x, g, b):
    mu = jnp.mean(x, axis=-1, keepdims=True)
    xc = x - mu
    var = jnp.mean(xc * xc, axis=-1, keepdims=True)
    return xc * lax.rsqrt(var + EPS) * g + b


def _sigmoid(x):
    return 1.0 / (1.0 + jnp.exp(-x))


def _pack_bf16_pairs(h):
    bits = lax.bitcast_convert_type(h, jnp.uint32)
    half = h.shape[1] // 2
    lo = lax.shift_right_logical(bits[:, :half], jnp.uint32(16))
    hi = bits[:, half:] & jnp.uint32(0xFFFF0000)
    return hi | lo


def _unpack_bf16_pairs_f32(p):
    lo = lax.bitcast_convert_type(lax.shift_left(p, jnp.uint32(16)), F32)
    hi = lax.bitcast_convert_type(p & jnp.uint32(0xFFFF0000), F32)
    return lo, hi


def _unpack_bf16_pairs(p):
    lo, hi = _unpack_bf16_pairs_f32(p)
    return lo.astype(BF16), hi.astype(BF16)


def _memkv_kernel(mem_ref, g_ref, wk_ref, wv_ref, k_ref, v_ref):
    m = _rms(mem_ref[...], g_ref[...]).astype(BF16)
    k_ref[...] = _dot(m, wk_ref[...].astype(BF16))
    v_ref[...] = _dot(m, wv_ref[...].astype(BF16))


def _memkv(mem, g_mem, w_mk, w_mv):
    return pl.pallas_call(
        _memkv_kernel,
        out_shape=(jax.ShapeDtypeStruct((N_MEM, D_MODEL), F32), jax.ShapeDtypeStruct((N_MEM, D_MODEL), F32)),
        compiler_params=pltpu.CompilerParams(vmem_limit_bytes=VMEM_LIMIT),
        name="memkv",
    )(mem, g_mem, w_mk, w_mv)


def _attn_heads(q, k, v):
    outs = []
    for h in range(X_HEADS):
        sl = slice(h * X_HEAD_DIM, (h + 1) * X_HEAD_DIM)
        s = lax.dot_general(q[:, sl], k[:, sl], (((1,), (1,)), ((), ())),
                            preferred_element_type=F32) * (X_HEAD_DIM ** -0.5)
        s = s - jnp.max(s, axis=-1, keepdims=True)
        p = jnp.exp(s)
        p = p / jnp.sum(p, axis=-1, keepdims=True)
        outs.append(_dot(p.astype(BF16), v[:, sl]).astype(BF16))
    return jnp.concatenate(outs, axis=1)


def _route(logits, run):
    m = logits.shape[0]
    lane = lax.broadcasted_iota(jnp.int32, (m, LOGIT_LANES), 1).astype(F32)
    neg = jnp.float32(-jnp.inf)
    big = jnp.float32(LOGIT_LANES)

    def first_argmax(vals):
        mx = jnp.max(vals, axis=-1, keepdims=True)
        idx = jnp.min(jnp.where(vals == mx, lane, big), axis=-1, keepdims=True)
        return mx, idx

    lg = jnp.where(lane < N_GROUPS, logits, neg)
    g_max, g_idx = first_argmax(lg)
    g_w = 1.0 / jnp.sum(jnp.exp(lg - g_max), axis=-1, keepdims=True)

    lo = N_GROUPS + g_idx * EXPERTS_PER_GROUP
    le = jnp.where((lane >= lo) & (lane < lo + EXPERTS_PER_GROUP), logits, neg)
    v1, i1 = first_argmax(le)
    v2, i2 = first_argmax(jnp.where(lane == i1, neg, le))
    t = jnp.exp(v2 - v1)
    gate1 = g_w / (1.0 + t)
    gate2 = g_w * t / (1.0 + t)
    e1 = i1 - N_GROUPS
    e2 = i2 - N_GROUPS

    oh1 = (lane == e1).astype(F32)
    oh2 = (lane == e2).astype(F32)
    oh = oh1 + oh2
    row = lax.broadcasted_iota(jnp.int32, (m, m), 0)
    col = lax.broadcasted_iota(jnp.int32, (m, m), 1)
    strict_lower = (col < row).astype(BF16)
    before = _dot(strict_lower, oh.astype(BF16)) + run
    rank1 = jnp.sum(before * oh1, axis=-1, keepdims=True)
    rank2 = jnp.sum(before * oh2, axis=-1, keepdims=True)
    new_run = run + jnp.sum(oh, axis=0, keepdims=True)

    rinfo = jnp.where(lane == 0, e1,
            jnp.where(lane == 1, e2,
            jnp.where(lane == 2, gate1,
            jnp.where(lane == 3, gate2,
            jnp.where(lane == 4, rank1,
            jnp.where(lane == 5, rank2, 0.0))))))
    return rinfo, new_run


def _conv_taps(win, w_ref, l0, rows):
    n = win.shape[0]
    acc = jnp.zeros((rows, LANES), F32)
    for r in range(SUBLANES):
        d = HALO - HIST + r
        if d % SUBLANES == 0:
            e, base = win, d
        else:
            e, base = pltpu.roll(win, n - d, axis=0), 0
        for qq in range(-(-CONV_WIDTH // SUBLANES)):
            k = SUBLANES * qq + r
            if k < CONV_WIDTH:
                s0 = base + SUBLANES * qq
                acc = acc + e[s0:s0 + rows, :] * w_ref[pl.ds(k, 1), pl.ds(l0, LANES)]
    return acc


def _cast_rows(src_ref, dst_ref):
    rows = src_ref.shape[0]

    def body(c, carry):
        r0 = pl.multiple_of(c * CAST_ROWS, CAST_ROWS)
        dst_ref[pl.ds(r0, CAST_ROWS), :] = src_ref[pl.ds(r0, CAST_ROWS), :].astype(BF16)
        return carry

    lax.fori_loop(0, rows // CAST_ROWS, body, 0)


def _trunk_prompt_kernel(x_ref, gmix_ref, win32_ref, convw_ref, convb_ref, lncg_ref, lncb_ref, lnvg_ref, lnvb_ref,
                         wsg_ref, bsg_ref, wout32_ref, gx_ref, wxq32_ref, kmem_ref, v_ref, wxo32_ref, gffn_ref, wr_ref,
                         br_ref,
                         x2_ref, h3_ref, rinfo_ref, rt_ref, hist_ref, cnt_ref,
                         ext_ref, conv_ref, run_ref, win_ref, wout_ref, wxq_ref, wxo_ref):
    i = pl.program_id(0)

    @pl.when(i == 0)
    def _():
        ext_ref[0:HALO, :] = jnp.zeros((HALO, D_CONV), F32)
        run_ref[...] = jnp.zeros((1, LOGIT_LANES), F32)
        _cast_rows(win32_ref, win_ref)
        _cast_rows(wout32_ref, wout_ref)
        _cast_rows(wxq32_ref, wxq_ref)
        _cast_rows(wxo32_ref, wxo_ref)

    x = x_ref[...]
    h = _rms(x, gmix_ref[...]).astype(BF16)

    a_in = _dot(h, win_ref[:, 0:D_CONV])
    a_gate = _dot(h, win_ref[:, D_CONV:2 * D_CONV])
    ext_ref[HALO:HALO + TM, :] = a_in * _sigmoid(a_gate)

    for r0 in range(0, TM, CONV_ROWS):
        for l0 in range(0, D_CONV, LANES):
            win = ext_ref[r0:r0 + CONV_ROWS + HALO, l0:l0 + LANES]
            conv_ref[r0:r0 + CONV_ROWS, l0:l0 + LANES] = _conv_taps(win, convw_ref, l0, CONV_ROWS)
    hist_ref[...] = ext_ref[TM:TM + HALO, :]
    ext_ref[0:HALO, :] = ext_ref[TM:TM + HALO, :]

    y = _ln(conv_ref[...] + convb_ref[...], lncg_ref[...], lncb_ref[...])
    a_out = (y * _sigmoid(y)).astype(BF16)

    u = _dot(h, win_ref[:, 2 * D_CONV:2 * D_CONV + D_SG])
    v = _ln(_dot(h, win_ref[:, 2 * D_CONV + D_SG:]), lnvg_ref[...], lnvb_ref[...]).astype(BF16)
    ri = lax.broadcasted_iota(jnp.int32, (SG_CHUNK, SG_CHUNK), 0)
    ci = lax.broadcasted_iota(jnp.int32, (SG_CHUNK, SG_CHUNK), 1)
    w_tril = [jnp.where(ci <= ri, wsg_ref[hh], 0.0).astype(BF16) for hh in range(SG_HEADS)]
    gate_rows = []
    for c in range(TM // SG_CHUNK):
        rs = slice(c * SG_CHUNK, (c + 1) * SG_CHUNK)
        heads = [_dot(w_tril[hh], v[rs, hh * SG_HEAD_DIM:(hh + 1) * SG_HEAD_DIM]) for hh in range(SG_HEADS)]
        gate_rows.append(jnp.concatenate(heads, axis=1) + bsg_ref[...])
    b_out = (u * jnp.concatenate(gate_rows, axis=0)).astype(BF16)

    x1 = x + _dot(a_out, wout_ref[0:D_CONV, :]) + _dot(b_out, wout_ref[D_CONV:, :])

    hx = _rms(x1, gx_ref[...]).astype(BF16)
    q = _dot(hx, wxq_ref[...]).astype(BF16)
    x2 = x1 + _dot(_attn_heads(q, kmem_ref[...], v_ref[...]), wxo_ref[...])
    x2_ref[...] = x2

    h3 = _rms(x2, gffn_ref[...]).astype(BF16)
    h3_ref[...] = _pack_bf16_pairs(h3.astype(F32))
    rinfo, new_run = _route(_dot(h3, wr_ref[...]) + br_ref[...], run_ref[...])
    rinfo_ref[...] = rinfo
    rt_ref[...] = jnp.transpose(rinfo)[0:SUBLANES, :]
    run_ref[...] = new_run
    cnt_ref[...] = new_run


def _const_spec(shape):
    nd = len(shape)
    return pl.BlockSpec(shape, lambda i: (0,) * nd, pipeline_mode=pl.Buffered(1))


def _trunk_prompt(x, p):
    n = x.shape[0]
    assert n % TM == 0
    row = lambda w: pl.BlockSpec((TM, w), lambda i: (i, 0))
    consts = [p["g_mix"], p["w_in"], p["conv_w"], p["conv_b"], p["ln_conv_g"], p["ln_conv_b"], p["ln_v_g"],
              p["ln_v_b"], p["w_sg"], p["b_sg_rows"], p["w_out"], p["g_xattn"], p["w_xq"], p["k"], p["v"],
              p["w_xo"], p["g_ffn"], p["w_router"], p["b_router"]]
    return pl.pallas_call(
        _trunk_prompt_kernel,
        grid=(n // TM,),
        in_specs=[row(D_MODEL)] + [_const_spec(c.shape) for c in consts],
        out_specs=(row(D_MODEL), row(D_MODEL // 2), row(LOGIT_LANES),
                   pl.BlockSpec((SUBLANES, TM), lambda i: (0, i)),
                   pl.BlockSpec((HALO, D_CONV), lambda i: (0, 0)),
                   pl.BlockSpec((1, LOGIT_LANES), lambda i: (0, 0))),
        out_shape=(jax.ShapeDtypeStruct((n, D_MODEL), F32),
                   jax.ShapeDtypeStruct((n, D_MODEL // 2), jnp.uint32),
                   jax.ShapeDtypeStruct((n, LOGIT_LANES), F32),
                   jax.ShapeDtypeStruct((SUBLANES, n), F32),
                   jax.ShapeDtypeStruct((HALO, D_CONV), F32),
                   jax.ShapeDtypeStruct((1, LOGIT_LANES), F32)),
        scratch_shapes=[pltpu.VMEM((TM + HALO, D_CONV), F32),
                        pltpu.VMEM((TM, D_CONV), F32),
                        pltpu.VMEM((1, LOGIT_LANES), F32),
                        pltpu.VMEM(p["w_in"].shape, BF16), pltpu.VMEM(p["w_out"].shape, BF16),
                        pltpu.VMEM(p["w_xq"].shape, BF16), pltpu.VMEM(p["w_xo"].shape, BF16)],
        compiler_params=pltpu.CompilerParams(dimension_semantics=("arbitrary",), vmem_limit_bytes=VMEM_LIMIT),
        name="trunk_prompt",
    )(x, *consts)


def _trunk_sample_kernel(n_batch, t_len,
                         x_ref, hist_in_ref, run_in_ref, gmix_ref, win_ref, convw_ref, convb_ref, lncg_ref, lncb_ref,
                         lnvg_ref, lnvb_ref, wsgbd_ref, bsg_ref, wout_ref, gx_ref, wxq_ref, kmem_ref, v_ref, wxo_ref,
                         gffn_ref, wr_ref, br_ref,
                         x2_ref, h3_ref, rinfo_ref, rt_ref, hist_ref, sgv_ref, cnt_ref,
                         ext_ref, conv_ref, att_ref):
    x = x_ref[...]
    h = _rms(x, gmix_ref[...]).astype(BF16)
    z = _dot(h, win_ref[...].astype(BF16))
    a = z[:, 0:D_CONV] * _sigmoid(z[:, D_CONV:2 * D_CONV])
    ext_len = HIST + t_len
    for b in range(n_batch):
        ext_ref[b, 0:HIST, :] = hist_in_ref[b]
        ext_ref[b, HIST:ext_len, :] = a[b * t_len:(b + 1) * t_len, :]
    for b in range(n_batch):
        acc = jnp.zeros((t_len, D_CONV), F32)
        for k in range(CONV_WIDTH):
            acc = acc + ext_ref[b, k:k + t_len, :] * convw_ref[k:k + 1, :]
        conv_ref[b * t_len:(b + 1) * t_len, :] = acc
        hist_ref[b] = ext_ref[b, ext_len - HIST:ext_len, :]

    y = _ln(conv_ref[...] + convb_ref[...], lncg_ref[...], lncb_ref[...])
    a_out = (y * _sigmoid(y)).astype(BF16)

    u = z[:, 2 * D_CONV:2 * D_CONV + D_SG]
    v = _ln(z[:, 2 * D_CONV + D_SG:], lnvg_ref[...], lnvb_ref[...])
    sgv_ref[...] = v
    vb = v.astype(BF16)
    heads = [_dot(wsgbd_ref[hh], vb[:, hh * SG_HEAD_DIM:(hh + 1) * SG_HEAD_DIM]) for hh in range(SG_HEADS)]
    b_out = (u * (jnp.concatenate(heads, axis=1) + bsg_ref[...])).astype(BF16)

    x1 = (x + _dot(a_out, wout_ref[0:D_CONV, :].astype(BF16))
          + _dot(b_out, wout_ref[D_CONV:, :].astype(BF16)))

    hx = _rms(x1, gx_ref[...]).astype(BF16)
    q = _dot(hx, wxq_ref[...].astype(BF16)).astype(BF16)
    for b in range(n_batch):
        rs = slice(b * t_len, (b + 1) * t_len)
        att_ref[rs, :] = _attn_heads(q[rs, :], kmem_ref[b], v_ref[b])
    x2 = x1 + _dot(att_ref[...], wxo_ref[...].astype(BF16))
    x2_ref[...] = x2

    h3 = _rms(x2, gffn_ref[...]).astype(BF16)
    m = n_batch * t_len
    h3_ref[0:m, :] = _pack_bf16_pairs(h3.astype(F32))
    if h3_ref.shape[0] > m:
        h3_ref[m:, :] = jnp.zeros((h3_ref.shape[0] - m, D_MODEL // 2), jnp.uint32)
    rinfo, new_run = _route(_dot(h3, wr_ref[...]) + br_ref[...], run_in_ref[...])
    rinfo_ref[...] = rinfo
    rt_ref[...] = jnp.transpose(rinfo)[0:SUBLANES, :]
    cnt_ref[...] = new_run


def _trunk_sample(x, hist, run, p, n_batch, t_len):
    m = n_batch * t_len
    args = [x, hist, run, p["g_mix"], p["w_in"], p["conv_w"], p["conv_b"], p["ln_conv_g"], p["ln_conv_b"],
            p["ln_v_g"], p["ln_v_b"], p["w_sg_bd"], p["b_sg_rows_s"], p["w_out"], p["g_xattn"], p["w_xq"],
            p["k_s"], p["v_s"], p["w_xo"], p["g_ffn"], p["w_router"], p["b_router"]]
    return pl.pallas_call(
        functools.partial(_trunk_sample_kernel, n_batch, t_len),
        out_shape=(jax.ShapeDtypeStruct((m, D_MODEL), F32),
                   jax.ShapeDtypeStruct((-(-m // (SC_WORKERS * SUBLANES)) * SC_WORKERS * SUBLANES, D_MODEL // 2),
                                        jnp.uint32),
                   jax.ShapeDtypeStruct((m, LOGIT_LANES), F32),
                   jax.ShapeDtypeStruct((SUBLANES, m), F32),
                   jax.ShapeDtypeStruct((n_batch, HIST, D_CONV), F32),
                   jax.ShapeDtypeStruct((m, D_SG), F32),
                   jax.ShapeDtypeStruct((1, LOGIT_LANES), F32)),
        scratch_shapes=[pltpu.VMEM((n_batch, HIST + t_len, D_CONV), F32),
                        pltpu.VMEM((m, D_CONV), F32),
                        pltpu.VMEM((m, D_MODEL), BF16)],
        compiler_params=pltpu.CompilerParams(vmem_limit_bytes=VMEM_LIMIT),
        name="trunk_sample",
    )(*args)


def _sc_worker_id():
    return lax.axis_index("s") * SC_CORES + lax.axis_index("c")


def _sc_chunk(per_w, max_chunk):
    assert per_w % SUBLANES == 0 and max_chunk <= LANES
    return max(c for c in range(SUBLANES, max_chunk + 1, SUBLANES) if per_w % c == 0)


def _sc_gather_rows(table, idx):
    n_rows, d = idx.shape[0], table.shape[1]
    per_w = n_rows // SC_WORKERS
    assert per_w * SC_WORKERS == n_rows and per_w % GATHER_ROWS == 0
    n_chunks = per_w // GATHER_ROWS
    lag = GATHER_BUFS // 2
    mesh = plsc.VectorSubcoreMesh(core_axis_name="c", subcore_axis_name="s")

    @functools.partial(
        pl.kernel, mesh=mesh,
        out_type=jax.ShapeDtypeStruct((n_rows, d), table.dtype),
        scratch_types=([pltpu.VMEM((per_w,), jnp.int32)]
                       + [pltpu.VMEM((GATHER_ROWS, d), table.dtype)] * GATHER_BUFS
                       + [pltpu.SemaphoreType.DMA] * (2 * GATHER_BUFS)),
    )
    def gather(table_hbm, idx_hbm, out_hbm, idx_all, *rest):
        rows = rest[:GATHER_BUFS]
        gsem = rest[GATHER_BUFS:2 * GATHER_BUFS]
        wsem = rest[2 * GATHER_BUFS:]
        base = _sc_worker_id() * per_w
        pltpu.sync_copy(idx_hbm.at[pl.ds(base, per_w)], idx_all)

        reads, writes = {}, {}
        for c in range(n_chunks + lag):
            if c < n_chunks:
                b = c % GATHER_BUFS
                if c >= GATHER_BUFS:
                    writes.pop(c - GATHER_BUFS).wait()
                reads[c] = pltpu.async_copy(
                    table_hbm.at[idx_all.at[pl.ds(c * GATHER_ROWS, GATHER_ROWS)]], rows[b], gsem[b])
            w = c - lag
            if w >= 0:
                b = w % GATHER_BUFS
                reads.pop(w).wait()
                writes[w] = pltpu.async_copy(
                    rows[b], out_hbm.at[pl.ds(base + w * GATHER_ROWS, GATHER_ROWS)], wsem[b])
        for w in sorted(writes):
            writes[w].wait()

    return gather(table, idx)


def _sc_scatter_rows2(tables, slots_a, slots_b, n_rows_out, max_chunk):
    d, dtype = tables[0].shape[1], tables[0].dtype
    plans = []
    for t in tables:
        per_w = t.shape[0] // SC_WORKERS
        assert per_w * SC_WORKERS == t.shape[0]
        chunk = _sc_chunk(per_w, max_chunk)
        plans.append((per_w, chunk, per_w // chunk))
    cmax = max(c for _, c, _ in plans)
    n_t = len(tables)
    mesh = plsc.VectorSubcoreMesh(core_axis_name="c", subcore_axis_name="s")

    nb = SCATTER_BUFS
    lag = nb // 3
    scratch = []
    for _, chunk, _ in plans:
        for _ in range(nb):
            scratch += [pltpu.VMEM((chunk,), jnp.int32), pltpu.VMEM((chunk,), jnp.int32)]
    scratch += [pltpu.VMEM((cmax, d), dtype)] * nb
    scratch += [pltpu.SemaphoreType.DMA] * (2 * nb)

    @functools.partial(pl.kernel, mesh=mesh, out_type=jax.ShapeDtypeStruct((n_rows_out, d), dtype),
                       scratch_types=scratch)
    def scatter(*refs):
        tab_hbm = refs[0:n_t]
        sa_hbm = refs[n_t:2 * n_t]
        sb_hbm = refs[2 * n_t:3 * n_t]
        out_hbm = refs[3 * n_t]
        sc = refs[3 * n_t + 1:]
        idx_refs = sc[:2 * nb * n_t]
        rows = sc[2 * nb * n_t:2 * nb * n_t + nb]
        lsem = sc[2 * nb * n_t + nb:2 * nb * n_t + 2 * nb]
        ssem = sc[2 * nb * n_t + 2 * nb:]
        wid = _sc_worker_id()

        work = []
        for t, (per_w, chunk, n_chunks) in enumerate(plans):
            for j in range(n_chunks):
                work.append((t, wid * per_w + j * chunk, chunk))

        def parts(k):
            t, off, chunk = work[k]
            b = k % nb
            ia, ib = idx_refs[2 * nb * t + 2 * b], idx_refs[2 * nb * t + 2 * b + 1]
            rv = rows[b] if chunk == cmax else rows[b].at[pl.ds(0, chunk)]
            return t, off, chunk, b, ia, ib, rv

        def start_load(k):
            t, off, chunk, b, ia, ib, rv = parts(k)
            return (pltpu.async_copy(tab_hbm[t].at[pl.ds(off, chunk)], rv, lsem[b]),
                    pltpu.async_copy(sa_hbm[t].at[pl.ds(off, chunk)], ia, lsem[b]),
                    pltpu.async_copy(sb_hbm[t].at[pl.ds(off, chunk)], ib, lsem[b]))

        def start_scatter(k):
            t, off, chunk, b, ia, ib, rv = parts(k)
            return (pltpu.async_copy(rv, out_hbm.at[ia], ssem[b]), pltpu.async_copy(rv, out_hbm.at[ib], ssem[b]))

        loads, scatters = {}, {}
        for k in range(len(work) + lag):
            if k < len(work):
                if k >= nb:
                    for c in scatters.pop(k - nb):
                        c.wait()
                loads[k] = start_load(k)
            w = k - lag
            if w >= 0:
                for c in loads.pop(w):
                    c.wait()
                scatters[w] = start_scatter(w)
        for w in sorted(scatters):
            for c in scatters[w]:
                c.wait()

    return scatter(*tables, *slots_a, *slots_b)


def _experts_kernel(first_ref, nblk_ref, cnt_ref, tot_ref, xs_hbm, wg_ref, wu_ref, wd_ref, ys_hbm,
                    xbuf, ybuf, wg_bf, wu_bf, wd_bf, in_sem, out_sem):
    e = pl.program_id(0)
    nb = nblk_ref[e]
    first = first_ref[e]
    cnt = cnt_ref[e]
    total = tot_ref[0]
    half = D_MODEL // 2

    def in_copy(gb):
        slot = lax.rem(gb, X_BUFS)
        return pltpu.make_async_copy(xs_hbm.at[pl.ds(gb * BM, BM)], xbuf.at[slot], in_sem.at[slot])

    def out_copy(gb):
        slot = lax.rem(gb, Y_BUFS)
        return pltpu.make_async_copy(ybuf.at[slot], ys_hbm.at[pl.ds(gb * BM, BM)], out_sem.at[slot])

    @pl.when(nb > 0)
    def _():
        @pl.when(first == 0)
        def _():
            for k in range(X_LOOKAHEAD):
                @pl.when(k < total)
                def _():
                    in_copy(k).start(priority=ROW_DMA_PRIORITY)

        wg_bf[...] = wg_ref[0].astype(BF16)
        wu_bf[...] = wu_ref[0].astype(BF16)
        wd_bf[...] = wd_ref[0].astype(BF16)

        def acquire(gb):
            @pl.when(gb + X_LOOKAHEAD < total)
            def _():
                in_copy(gb + X_LOOKAHEAD).start(priority=ROW_DMA_PRIORITY)

            in_copy(gb).wait()

            @pl.when(gb >= Y_BUFS)
            def _():
                out_copy(gb - Y_BUFS).wait()

        def ffn(gb, j):
            live = lax.broadcasted_iota(jnp.int32, (BM, half), 0) < cnt - j * BM
            lo, hi = _unpack_bf16_pairs(jnp.where(live, xbuf[lax.rem(gb, X_BUFS)], jnp.uint32(0)))
            g = _dot(lo, wg_bf[0:half, :]) + _dot(hi, wg_bf[half:, :])
            u = _dot(lo, wu_bf[0:half, :]) + _dot(hi, wu_bf[half:, :])
            hm = (g * _sigmoid(g) * u).astype(BF16)
            y = _dot(hm, wd_bf[...])
            ybuf[lax.rem(gb, Y_BUFS)] = _pack_bf16_pairs(y.astype(BF16).astype(F32))

        def block_pair(jp, carry):
            j0 = 2 * jp
            g0 = first + j0
            acquire(g0)
            acquire(g0 + 1)
            ffn(g0, j0)
            ffn(g0 + 1, j0 + 1)
            out_copy(g0).start(priority=ROW_DMA_PRIORITY)
            out_copy(g0 + 1).start(priority=ROW_DMA_PRIORITY)
            return carry

        lax.fori_loop(0, nb // 2, block_pair, 0)

        @pl.when(lax.rem(nb, 2) == 1)
        def _():
            gl = first + nb - 1
            acquire(gl)
            ffn(gl, nb - 1)
            out_copy(gl).start(priority=ROW_DMA_PRIORITY)

        @pl.when(first + nb == total)
        def _():
            for k in range(Y_BUFS):
                @pl.when(total - 1 - k >= 0)
                def _():
                    out_copy(total - 1 - k).wait()


def _experts(xs, n_rows_out, first_block, n_blocks_e, counts, w_eg, w_eu, w_ed):
    w_map = lambda e, fb, nb, ct, tot: (e, 0, 0)
    half = D_MODEL // 2
    total = jnp.sum(n_blocks_e).astype(jnp.int32).reshape(1)
    return pl.pallas_call(
        _experts_kernel,
        grid_spec=pltpu.PrefetchScalarGridSpec(
            num_scalar_prefetch=4,
            grid=(N_EXPERTS,),
            in_specs=[pl.BlockSpec(memory_space=pl.ANY),
                      pl.BlockSpec((1, D_MODEL, D_EXPERT), w_map),
                      pl.BlockSpec((1, D_MODEL, D_EXPERT), w_map),
                      pl.BlockSpec((1, D_EXPERT, D_MODEL), w_map)],
            out_specs=pl.BlockSpec(memory_space=pl.ANY),
            scratch_shapes=[pltpu.VMEM((X_BUFS, BM, half), jnp.uint32), pltpu.VMEM((Y_BUFS, BM, half), jnp.uint32),
                            pltpu.VMEM((D_MODEL, D_EXPERT), BF16), pltpu.VMEM((D_MODEL, D_EXPERT), BF16),
                            pltpu.VMEM((D_EXPERT, D_MODEL), BF16),
                            pltpu.SemaphoreType.DMA((X_BUFS,)), pltpu.SemaphoreType.DMA((Y_BUFS,))]),
        out_shape=jax.ShapeDtypeStruct((n_rows_out, half), jnp.uint32),
        compiler_params=pltpu.CompilerParams(dimension_semantics=("arbitrary",), vmem_limit_bytes=VMEM_LIMIT),
        name="experts",
    )(first_block, n_blocks_e, counts, total, xs, w_eg, w_eu, w_ed)


def _combine_kernel(x2_ref, y1_ref, y2_ref, rinfo_ref, g_ref, o_ref):
    r = rinfo_ref[...]
    g1, g2 = r[:, 2:3], r[:, 3:4]
    half = D_MODEL // 2
    y1_lo, y1_hi = _unpack_bf16_pairs_f32(y1_ref[...])
    y2_lo, y2_hi = _unpack_bf16_pairs_f32(y2_ref[...])
    x_lo = x2_ref[:, 0:half] + g1 * y1_lo + g2 * y2_lo
    x_hi = x2_ref[:, half:] + g1 * y1_hi + g2 * y2_hi
    ms = (jnp.sum(x_lo * x_lo, axis=-1, keepdims=True) + jnp.sum(x_hi * x_hi, axis=-1, keepdims=True)) / D_MODEL
    inv = lax.rsqrt(ms + EPS)
    o_ref[:, 0:half] = x_lo * inv * g_ref[:, 0:half]
    o_ref[:, half:] = x_hi * inv * g_ref[:, half:]


def _combine(x2, yg, rinfo, g_final, tm, blk1, blk2):
    n = x2.shape[0]
    return pl.pallas_call(
        _combine_kernel,
        grid=(n // tm,),
        in_specs=[pl.BlockSpec((tm, D_MODEL), lambda i: (i, 0)),
                  pl.BlockSpec((tm, D_MODEL // 2), lambda i: (blk1 + i, 0)),
                  pl.BlockSpec((tm, D_MODEL // 2), lambda i: (blk2 + i, 0)),
                  pl.BlockSpec((tm, LOGIT_LANES), lambda i: (i, 0)),
                  pl.BlockSpec((1, D_MODEL), lambda i: (0, 0))],
        out_specs=pl.BlockSpec((tm, D_MODEL), lambda i: (i, 0)),
        out_shape=jax.ShapeDtypeStruct((n, D_MODEL), F32),
        compiler_params=pltpu.CompilerParams(dimension_semantics=("arbitrary",), vmem_limit_bytes=VMEM_LIMIT),
        name="combine",
    )(x2, yg, yg, rinfo, g_final)


def _gather_rows(table, idx):
    return _sc_gather_rows(table, idx)


def _scatter_rows2(tables, slots_a, slots_b, n_rows_out):
    return _sc_scatter_rows2(tables, slots_a, slots_b, n_rows_out, SCATTER_CHUNK)


def kernel(x_prompt, x_sample, mem_prompt, state_conv, cache_mem_k, cache_mem_v, g_mix, w_in, conv_w, conv_b, ln_conv_g, ln_conv_b, ln_v_g, ln_v_b, w_sg, b_sg, w_out, g_mem, w_mk, w_mv, g_xattn, w_xq, w_xo, g_ffn, w_router_group, b_router_group, w_router_expert, b_router_expert, w_expert_gate, w_expert_up, w_expert_down, g_final):
    assert x_prompt.shape[0] == 1 and g_mix.shape[0] == 1
    n_p = x_prompt.shape[1]
    n_batch, t_len = x_sample.shape[0], x_sample.shape[1]
    n_s = n_batch * t_len
    row = lambda a: a.reshape(1, -1)

    w_router = jnp.concatenate(
        [w_router_group[0], jnp.transpose(w_router_expert[0], (1, 0, 2)).reshape(D_MODEL, N_EXPERTS)], axis=1)
    w_router = jnp.pad(w_router, ((0, 0), (0, LOGIT_LANES - w_router.shape[1]))).astype(BF16)
    b_router = jnp.pad(jnp.concatenate([b_router_group[0], b_router_expert[0].reshape(-1)]),
                       (0, LOGIT_LANES - N_GROUPS - N_EXPERTS)).reshape(1, LOGIT_LANES)
    tril_t = jnp.tril(jnp.ones((t_len, t_len), bool))
    w_sg_t = jnp.where(tril_t, w_sg[0][:, :t_len, :t_len], 0.0)
    eye_b = jnp.eye(n_batch, dtype=F32)
    w_sg_bd = jnp.einsum("ab,hij->haibj", eye_b, w_sg_t).reshape(SG_HEADS, n_s, n_s).astype(BF16)
    p = {
        "g_mix": row(g_mix[0]), "w_in": w_in[0],
        "conv_w": jnp.pad(conv_w[0], ((0, 1), (0, 0))), "conv_b": row(conv_b[0]),
        "ln_conv_g": row(ln_conv_g[0]), "ln_conv_b": row(ln_conv_b[0]),
        "ln_v_g": row(ln_v_g[0]), "ln_v_b": row(ln_v_b[0]),
        "w_sg": w_sg[0],
        "b_sg_rows": jnp.repeat(b_sg[0].T, SG_HEAD_DIM, axis=1),
        "w_sg_bd": w_sg_bd,
        "b_sg_rows_s": jnp.tile(jnp.repeat(b_sg[0][:, :t_len].T, SG_HEAD_DIM, axis=1), (n_batch, 1)),
        "w_out": w_out[0], "g_xattn": row(g_xattn[0]),
        "w_xq": w_xq[0], "w_xo": w_xo[0], "g_ffn": row(g_ffn[0]),
        "w_router": w_router, "b_router": b_router,
    }

    k_p, v_p = _memkv(mem_prompt[0], row(g_mem[0]), w_mk[0], w_mv[0])
    p["k"] = k_p.astype(BF16)
    p["v"] = v_p.astype(BF16)
    p["k_s"] = cache_mem_k[0].reshape(n_batch, N_MEM, D_MODEL).astype(BF16)
    p["v_s"] = cache_mem_v[0].reshape(n_batch, N_MEM, D_MODEL).astype(BF16)

    assert n_p % n_s == 0
    x2_p, h3_p, rinfo_p, rt_p, hist_p, cnt_p = _trunk_prompt(x_prompt[0], p)
    x2_s, h3_s, rinfo_s, rt_s, hist_s, sgv_s, cnt = _trunk_sample(
        x_sample.reshape(n_s, D_MODEL), state_conv[0], cnt_p, p, n_batch, t_len)

    experts = jnp.arange(N_EXPERTS, dtype=jnp.int32)
    w_e = (w_expert_gate[0], w_expert_up[0], w_expert_down[0])

    def moe_pass(cnt, h3_tables, rts, n_real):
        n_tot = sum(n_real)
        n_slots = -(-(n_tot * 2) // BM) * BM + N_EXPERTS * BM
        counts = cnt[0, :N_EXPERTS].astype(jnp.int32)
        padded = (counts + BM - 1) // BM * BM
        pad_start = jnp.cumsum(padded) - padded

        def one(e_row, rank_row):
            e = e_row.astype(jnp.int32)
            start = jnp.sum(jnp.where(e[None, :] == experts[:, None], pad_start[:, None], 0), axis=0)
            return start + rank_row.astype(jnp.int32)

        slots = [(one(rt[0], rt[4]), one(rt[1], rt[5])) for rt in rts]
        sa, sb, spare0 = [], [], n_slots
        for tab, (a, b), n in zip(h3_tables, slots, n_real):
            n_spare = tab.shape[0] - n
            spare = spare0 + jnp.arange(n_spare, dtype=jnp.int32)
            sa.append(jnp.concatenate([a, spare]))
            sb.append(jnp.concatenate([b, spare + n_spare]))
            spare0 += 2 * n_spare
        xs = _scatter_rows2(tuple(h3_tables), tuple(sa), tuple(sb), spare0)
        ys = _experts(xs, n_slots, pad_start // BM, padded // BM, counts, *w_e)
        back_idx = jnp.concatenate([s for ab in slots for s in ab])
        n_back = -(-back_idx.shape[0] // (SC_WORKERS * GATHER_ROWS)) * (SC_WORKERS * GATHER_ROWS)
        return _gather_rows(ys, jnp.pad(back_idx, (0, n_back - back_idx.shape[0])))

    yg = moe_pass(cnt, [h3_p, h3_s], [rt_p, rt_s], [n_p, n_s])

    gf = row(g_final)
    y_p = _combine(x2_p, yg, rinfo_p, gf, TM, 0, n_p // TM)
    y_s = _combine(x2_s, yg, rinfo_s, gf, n_s, 2 * n_p // n_s, 2 * n_p // n_s + 1)

    return (y_p.reshape(1, n_p, D_MODEL),
            y_s.reshape(n_batch, t_len, D_MODEL),
            hist_p[HALO - HIST:].reshape(1, 1, HIST, D_CONV),
            hist_s.reshape(1, n_batch, HIST, D_CONV),
            k_p.reshape(1, 1, N_MEM, X_HEADS, X_HEAD_DIM),
            v_p.reshape(1, 1, N_MEM, X_HEADS, X_HEAD_DIM),
            sgv_s.reshape(1, n_batch, t_len, D_SG))
```

```python
import functools

import jax
import jax.numpy as jnp
from jax import lax
from jax.experimental import pallas as pl
from jax.experimental.pallas import tpu as pltpu
from jax.experimental.pallas import tpu_sc as plsc

D_MODEL = 1024
D_CONV = 512
D_SG = 512
CONV_WIDTH = 31
HIST = CONV_WIDTH - 1
SG_HEADS = 4
SG_HEAD_DIM = 128
SG_CHUNK = 128
N_MEM = 256
X_HEADS = 4
X_HEAD_DIM = 256
N_GROUPS = 4
EXPERTS_PER_GROUP = 8
N_EXPERTS = 32
D_EXPERT = 512
EPS = 1e-6

LANES = 128
SUBLANES = 8
SC_CORES = 2
SC_SUBCORES = 16
SC_WORKERS = SC_CORES * SC_SUBCORES
VMEM_LIMIT = 56 * 1024 * 1024

TM = 512
TM_COMBINE = 1024
HALO = 32
CONV_ROWS = 128
CAST_ROWS = 64
BM = 256
X_LOOKAHEAD = 4
X_BUFS = X_LOOKAHEAD + 2
Y_BUFS = 4
ROW_DMA_PRIORITY = 1
GATHER_ROWS = 32
GATHER_BUFS = 6
SCATTER_CHUNK = 32
SCATTER_BUFS = 6
LOGIT_LANES = 128

F32 = jnp.float32
BF16 = jnp.bfloat16


def _dot(a, b):
    return jnp.dot(a, b, preferred_element_type=F32)


def _rms(x, g):
    return x * lax.rsqrt(jnp.mean(x * x, axis=-1, keepdims=True) + EPS) * g


def _ln(x, g, b):
    mu = jnp.mean(x, axis=-1, keepdims=True)
    xc = x - mu
    var = jnp.mean(xc * xc, axis=-1, keepdims=True)
    return xc * lax.rsqrt(var + EPS) * g + b


def _sigmoid(x):
    return 1.0 / (1.0 + jnp.exp(-x))


def _pack_bf16_pairs(h):
    bits = lax.bitcast_convert_type(h, jnp.uint32)
    half = h.shape[1] // 2
    lo = lax.shift_right_logical(bits[:, :half], jnp.uint32(16))
    hi = bits[:, half:] & jnp.uint32(0xFFFF0000)
    return hi | lo


def _unpack_bf16_pairs_f32(p):
    lo = lax.bitcast_convert_type(lax.shift_left(p, jnp.uint32(16)), F32)
    hi = lax.bitcast_convert_type(p & jnp.uint32(0xFFFF0000), F32)
    return lo, hi


def _unpack_bf16_pairs(p):
    lo, hi = _unpack_bf16_pairs_f32(p)
    return lo.astype(BF16), hi.astype(BF16)


def _memkv_kernel(mem_ref, g_ref, wk_ref, wv_ref, k_ref, v_ref):
    m = _rms(mem_ref[...], g_ref[...]).astype(BF16)
    k_ref[...] = _dot(m, wk_ref[...].astype(BF16))
    v_ref[...] = _dot(m, wv_ref[...].astype(BF16))


def _memkv(mem, g_mem, w_mk, w_mv):
    return pl.pallas_call(
        _memkv_kernel,
        out_shape=(jax.ShapeDtypeStruct((N_MEM, D_MODEL), F32), jax.ShapeDtypeStruct((N_MEM, D_MODEL), F32)),
        compiler_params=pltpu.CompilerParams(vmem_limit_bytes=VMEM_LIMIT),
        name="memkv",
    )(mem, g_mem, w_mk, w_mv)


def _attn_heads(q, k, v, k_transposed):
    outs = []
    for h in range(X_HEADS):
        sl = slice(h * X_HEAD_DIM, (h + 1) * X_HEAD_DIM)
        if k_transposed:
            s = _dot(q[:, sl], k[sl, :])
        else:
            s = lax.dot_general(q[:, sl], k[:, sl], (((1,), (1,)), ((), ())), preferred_element_type=F32)
        s = s * (X_HEAD_DIM ** -0.5)
        s = s - jnp.max(s, axis=-1, keepdims=True)
        p = jnp.exp(s)
        p = p / jnp.sum(p, axis=-1, keepdims=True)
        outs.append(_dot(p.astype(BF16), v[:, sl]).astype(BF16))
    return jnp.concatenate(outs, axis=1)


def _route(logits, run, strict_lower):
    m = logits.shape[0]
    lane = lax.broadcasted_iota(jnp.int32, (m, LOGIT_LANES), 1).astype(F32)
    neg = jnp.float32(-jnp.inf)
    big = jnp.float32(LOGIT_LANES)

    def first_argmax(vals):
        mx = jnp.max(vals, axis=-1, keepdims=True)
        idx = jnp.min(jnp.where(vals == mx, lane, big), axis=-1, keepdims=True)
        return mx, idx

    lg = jnp.where(lane < N_GROUPS, logits, neg)
    g_max, g_idx = first_argmax(lg)
    g_w = 1.0 / jnp.sum(jnp.exp(lg - g_max), axis=-1, keepdims=True)

    lo = N_GROUPS + g_idx * EXPERTS_PER_GROUP
    le = jnp.where((lane >= lo) & (lane < lo + EXPERTS_PER_GROUP), logits, neg)
    v1, i1 = first_argmax(le)
    v2, i2 = first_argmax(jnp.where(lane == i1, neg, le))
    t = jnp.exp(v2 - v1)
    gate1 = g_w / (1.0 + t)
    gate2 = g_w * t / (1.0 + t)
    e1 = i1 - N_GROUPS
    e2 = i2 - N_GROUPS

    oh1 = (lane == e1).astype(F32)
    oh2 = (lane == e2).astype(F32)
    oh = oh1 + oh2
    before = _dot(strict_lower, oh.astype(BF16)) + run
    rank1 = jnp.sum(before * oh1, axis=-1, keepdims=True)
    rank2 = jnp.sum(before * oh2, axis=-1, keepdims=True)
    new_run = run + jnp.sum(oh, axis=0, keepdims=True)

    rinfo = jnp.where(lane == 0, e1,
            jnp.where(lane == 1, e2,
            jnp.where(lane == 2, gate1,
            jnp.where(lane == 3, gate2,
            jnp.where(lane == 4, rank1,
            jnp.where(lane == 5, rank2, 0.0))))))
    return jnp.transpose(rinfo)[0:SUBLANES, :], new_run


def _conv_taps(win, w_ref, l0, rows):
    n = win.shape[0]
    acc = jnp.zeros((rows, LANES), F32)
    for r in range(SUBLANES):
        d = HALO - HIST + r
        if d % SUBLANES == 0:
            e, base = win, d
        else:
            e, base = pltpu.roll(win, n - d, axis=0), 0
        for qq in range(-(-CONV_WIDTH // SUBLANES)):
            k = SUBLANES * qq + r
            if k < CONV_WIDTH:
                s0 = base + SUBLANES * qq
                acc = acc + e[s0:s0 + rows, :] * w_ref[pl.ds(k, 1), pl.ds(l0, LANES)]
    return acc


def _cast_rows(src_ref, dst_ref):
    rows = src_ref.shape[0]

    def body(c, carry):
        r0 = pl.multiple_of(c * CAST_ROWS, CAST_ROWS)
        dst_ref[pl.ds(r0, CAST_ROWS), :] = src_ref[pl.ds(r0, CAST_ROWS), :].astype(BF16)
        return carry

    lax.fori_loop(0, rows // CAST_ROWS, body, 0)


def _trunk_prompt_kernel(x_ref, gmix_ref, win32_ref, convw_ref, convb_ref, lncg_ref, lncb_ref, lnvg_ref, lnvb_ref,
                         wsg_ref, bsg_ref, wout32_ref, gx_ref, wxq32_ref, kmem_ref, v_ref, wxo32_ref, gffn_ref, wr_ref,
                         br_ref, lower_ref,
                         x2_ref, h3_ref, rt_ref, hist_ref, cnt_ref,
                         ext_ref, conv_ref, run_ref, win_ref, wout_ref, wxq_ref, wxo_ref):
    i = pl.program_id(0)

    @pl.when(i == 0)
    def _():
        ext_ref[0:HALO, :] = jnp.zeros((HALO, D_CONV), F32)
        run_ref[...] = jnp.zeros((1, LOGIT_LANES), F32)
        _cast_rows(win32_ref, win_ref)
        _cast_rows(wout32_ref, wout_ref)
        _cast_rows(wxq32_ref, wxq_ref)
        _cast_rows(wxo32_ref, wxo_ref)

    x = x_ref[...]
    h = _rms(x, gmix_ref[...]).astype(BF16)

    a_in = _dot(h, win_ref[:, 0:D_CONV])
    a_gate = _dot(h, win_ref[:, D_CONV:2 * D_CONV])
    ext_ref[HALO:HALO + TM, :] = a_in * _sigmoid(a_gate)

    for r0 in range(0, TM, CONV_ROWS):
        for l0 in range(0, D_CONV, LANES):
            win = ext_ref[r0:r0 + CONV_ROWS + HALO, l0:l0 + LANES]
            conv_ref[r0:r0 + CONV_ROWS, l0:l0 + LANES] = _conv_taps(win, convw_ref, l0, CONV_ROWS)
    hist_ref[...] = ext_ref[TM:TM + HALO, :]
    ext_ref[0:HALO, :] = ext_ref[TM:TM + HALO, :]

    y = _ln(conv_ref[...] + convb_ref[...], lncg_ref[...], lncb_ref[...])
    a_out = (y * _sigmoid(y)).astype(BF16)

    u = _dot(h, win_ref[:, 2 * D_CONV:2 * D_CONV + D_SG])
    v = _ln(_dot(h, win_ref[:, 2 * D_CONV + D_SG:]), lnvg_ref[...], lnvb_ref[...]).astype(BF16)
    ri = lax.broadcasted_iota(jnp.int32, (SG_CHUNK, SG_CHUNK), 0)
    ci = lax.broadcasted_iota(jnp.int32, (SG_CHUNK, SG_CHUNK), 1)
    w_tril = [jnp.where(ci <= ri, wsg_ref[hh], 0.0).astype(BF16) for hh in range(SG_HEADS)]
    gate_rows = []
    for c in range(TM // SG_CHUNK):
        rs = slice(c * SG_CHUNK, (c + 1) * SG_CHUNK)
        heads = [_dot(w_tril[hh], v[rs, hh * SG_HEAD_DIM:(hh + 1) * SG_HEAD_DIM]) for hh in range(SG_HEADS)]
        gate_rows.append(jnp.concatenate(heads, axis=1) + bsg_ref[...])
    b_out = (u * jnp.concatenate(gate_rows, axis=0)).astype(BF16)

    x1 = x + _dot(a_out, wout_ref[0:D_CONV, :]) + _dot(b_out, wout_ref[D_CONV:, :])

    hx = _rms(x1, gx_ref[...]).astype(BF16)
    q = _dot(hx, wxq_ref[...]).astype(BF16)
    x2 = x1 + _dot(_attn_heads(q, kmem_ref[...], v_ref[...], False), wxo_ref[...])
    x2_ref[...] = x2

    h3 = _rms(x2, gffn_ref[...]).astype(BF16)
    h3_ref[...] = _pack_bf16_pairs(h3.astype(F32))
    rt, new_run = _route(_dot(h3, wr_ref[...]) + br_ref[...], run_ref[...], lower_ref[...])
    rt_ref[...] = rt
    run_ref[...] = new_run
    cnt_ref[...] = new_run


def _const_spec(shape):
    nd = len(shape)
    return pl.BlockSpec(shape, lambda i: (0,) * nd, pipeline_mode=pl.Buffered(1))


def _trunk_prompt(x, p):
    n = x.shape[0]
    assert n % TM == 0
    row = lambda w: pl.BlockSpec((TM, w), lambda i: (i, 0))
    consts = [p["g_mix"], p["w_in"], p["conv_w"], p["conv_b"], p["ln_conv_g"], p["ln_conv_b"], p["ln_v_g"],
              p["ln_v_b"], p["w_sg"], p["b_sg_rows"], p["w_out"], p["g_xattn"], p["w_xq"], p["k"], p["v"],
              p["w_xo"], p["g_ffn"], p["w_router"], p["b_router"], p["lower"]]
    return pl.pallas_call(
        _trunk_prompt_kernel,
        grid=(n // TM,),
        in_specs=[row(D_MODEL)] + [_const_spec(c.shape) for c in consts],
        out_specs=(row(D_MODEL), row(D_MODEL // 2),
                   pl.BlockSpec((SUBLANES, TM), lambda i: (0, i)),
                   pl.BlockSpec((HALO, D_CONV), lambda i: (0, 0)),
                   pl.BlockSpec((1, LOGIT_LANES), lambda i: (0, 0))),
        out_shape=(jax.ShapeDtypeStruct((n, D_MODEL), F32),
                   jax.ShapeDtypeStruct((n, D_MODEL // 2), jnp.uint32),
                   jax.ShapeDtypeStruct((SUBLANES, n), F32),
                   jax.ShapeDtypeStruct((HALO, D_CONV), F32),
                   jax.ShapeDtypeStruct((1, LOGIT_LANES), F32)),
        scratch_shapes=[pltpu.VMEM((TM + HALO, D_CONV), F32),
                        pltpu.VMEM((TM, D_CONV), F32),
                        pltpu.VMEM((1, LOGIT_LANES), F32),
                        pltpu.VMEM(p["w_in"].shape, BF16), pltpu.VMEM(p["w_out"].shape, BF16),
                        pltpu.VMEM(p["w_xq"].shape, BF16), pltpu.VMEM(p["w_xo"].shape, BF16)],
        compiler_params=pltpu.CompilerParams(dimension_semantics=("arbitrary",), vmem_limit_bytes=VMEM_LIMIT),
        name="trunk_prompt",
    )(x, *consts)


def _trunk_sample_kernel(n_batch, t_len,
                         x_ref, hist_in_ref, run_in_ref, gmix_ref, win_ref, convw_ref, convb_ref, lncg_ref, lncb_ref,
                         lnvg_ref, lnvb_ref, wsgbd_ref, bsg_ref, wout_ref, gx_ref, wxq_ref, kmem_ref, v_ref, wxo_ref,
                         gffn_ref, wr_ref, br_ref, lower_ref,
                         x2_ref, h3_ref, rt_ref, hist_ref, sgv_ref, cnt_ref,
                         ext_ref, conv_ref, att_ref):
    x = x_ref[...]
    h = _rms(x, gmix_ref[...]).astype(BF16)
    z = _dot(h, win_ref[...].astype(BF16))
    a = z[:, 0:D_CONV] * _sigmoid(z[:, D_CONV:2 * D_CONV])
    ext_len = HIST + t_len
    for b in range(n_batch):
        ext_ref[b, 0:HIST, :] = hist_in_ref[b]
        ext_ref[b, HIST:ext_len, :] = a[b * t_len:(b + 1) * t_len, :]
    for b in range(n_batch):
        acc = jnp.zeros((t_len, D_CONV), F32)
        for k in range(CONV_WIDTH):
            acc = acc + ext_ref[b, k:k + t_len, :] * convw_ref[k:k + 1, :]
        conv_ref[b * t_len:(b + 1) * t_len, :] = acc
        hist_ref[b] = ext_ref[b, ext_len - HIST:ext_len, :]

    y = _ln(conv_ref[...] + convb_ref[...], lncg_ref[...], lncb_ref[...])
    a_out = (y * _sigmoid(y)).astype(BF16)

    u = z[:, 2 * D_CONV:2 * D_CONV + D_SG]
    v = _ln(z[:, 2 * D_CONV + D_SG:], lnvg_ref[...], lnvb_ref[...])
    sgv_ref[...] = v
    vb = v.astype(BF16)
    heads = [_dot(wsgbd_ref[hh], vb[:, hh * SG_HEAD_DIM:(hh + 1) * SG_HEAD_DIM]) for hh in range(SG_HEADS)]
    b_out = (u * (jnp.concatenate(heads, axis=1) + bsg_ref[...])).astype(BF16)

    x1 = (x + _dot(a_out, wout_ref[0:D_CONV, :].astype(BF16))
          + _dot(b_out, wout_ref[D_CONV:, :].astype(BF16)))

    hx = _rms(x1, gx_ref[...]).astype(BF16)
    q = _dot(hx, wxq_ref[...].astype(BF16)).astype(BF16)
    for b in range(n_batch):
        rs = slice(b * t_len, (b + 1) * t_len)
        att_ref[rs, :] = _attn_heads(q[rs, :], kmem_ref[b], v_ref[b], True)
    x2 = x1 + _dot(att_ref[...], wxo_ref[...].astype(BF16))
    x2_ref[...] = x2

    h3 = _rms(x2, gffn_ref[...]).astype(BF16)
    m = n_batch * t_len
    h3_ref[0:m, :] = _pack_bf16_pairs(h3.astype(F32))
    if h3_ref.shape[0] > m:
        h3_ref[m:, :] = jnp.zeros((h3_ref.shape[0] - m, D_MODEL // 2), jnp.uint32)
    rt, new_run = _route(_dot(h3, wr_ref[...]) + br_ref[...], run_in_ref[...], lower_ref[...])
    rt_ref[...] = rt
    cnt_ref[...] = new_run


def _trunk_sample(x, hist, run, p, n_batch, t_len):
    m = n_batch * t_len
    args = [x, hist, run, p["g_mix"], p["w_in"], p["conv_w"], p["conv_b"], p["ln_conv_g"], p["ln_conv_b"],
            p["ln_v_g"], p["ln_v_b"], p["w_sg_bd"], p["b_sg_rows_s"], p["w_out"], p["g_xattn"], p["w_xq"],
            p["k_s"], p["v_s"], p["w_xo"], p["g_ffn"], p["w_router"], p["b_router"], p["lower"][:m, :m]]
    return pl.pallas_call(
        functools.partial(_trunk_sample_kernel, n_batch, t_len),
        out_shape=(jax.ShapeDtypeStruct((m, D_MODEL), F32),
                   jax.ShapeDtypeStruct((-(-m // (SC_WORKERS * SUBLANES)) * SC_WORKERS * SUBLANES, D_MODEL // 2),
                                        jnp.uint32),
                   jax.ShapeDtypeStruct((SUBLANES, m), F32),
                   jax.ShapeDtypeStruct((n_batch, HIST, D_CONV), F32),
                   jax.ShapeDtypeStruct((m, D_SG), F32),
                   jax.ShapeDtypeStruct((1, LOGIT_LANES), F32)),
        scratch_shapes=[pltpu.VMEM((n_batch, HIST + t_len, D_CONV), F32),
                        pltpu.VMEM((m, D_CONV), F32),
                        pltpu.VMEM((m, D_MODEL), BF16)],
        compiler_params=pltpu.CompilerParams(vmem_limit_bytes=VMEM_LIMIT),
        name="trunk_sample",
    )(*args)


def _sc_worker_id():
    return lax.axis_index("s") * SC_CORES + lax.axis_index("c")


def _sc_chunk(per_w, max_chunk):
    assert per_w % SUBLANES == 0 and max_chunk <= LANES
    return max(c for c in range(SUBLANES, max_chunk + 1, SUBLANES) if per_w % c == 0)


def _sc_gather_rows(table, idx):
    n_rows, d = idx.shape[0], table.shape[1]
    per_w = n_rows // SC_WORKERS
    assert per_w * SC_WORKERS == n_rows and per_w % GATHER_ROWS == 0
    n_chunks = per_w // GATHER_ROWS
    lag = GATHER_BUFS // 2
    mesh = plsc.VectorSubcoreMesh(core_axis_name="c", subcore_axis_name="s")

    @functools.partial(
        pl.kernel, mesh=mesh,
        out_type=jax.ShapeDtypeStruct((n_rows, d), table.dtype),
        scratch_types=([pltpu.VMEM((per_w,), jnp.int32)]
                       + [pltpu.VMEM((GATHER_ROWS, d), table.dtype)] * GATHER_BUFS
                       + [pltpu.SemaphoreType.DMA] * (2 * GATHER_BUFS)),
    )
    def gather(table_hbm, idx_hbm, out_hbm, idx_all, *rest):
        rows = rest[:GATHER_BUFS]
        gsem = rest[GATHER_BUFS:2 * GATHER_BUFS]
        wsem = rest[2 * GATHER_BUFS:]
        base = _sc_worker_id() * per_w
        pltpu.sync_copy(idx_hbm.at[pl.ds(base, per_w)], idx_all)

        reads, writes = {}, {}
        for c in range(n_chunks + lag):
            if c < n_chunks:
                b = c % GATHER_BUFS
                if c >= GATHER_BUFS:
                    writes.pop(c - GATHER_BUFS).wait()
                reads[c] = pltpu.async_copy(
                    table_hbm.at[idx_all.at[pl.ds(c * GATHER_ROWS, GATHER_ROWS)]], rows[b], gsem[b])
            w = c - lag
            if w >= 0:
                b = w % GATHER_BUFS
                reads.pop(w).wait()
                writes[w] = pltpu.async_copy(
                    rows[b], out_hbm.at[pl.ds(base + w * GATHER_ROWS, GATHER_ROWS)], wsem[b])
        for w in sorted(writes):
            writes[w].wait()

    return gather(table, idx)


def _sc_scatter_rows2(tables, slots_a, slots_b, n_rows_out, max_chunk):
    d, dtype = tables[0].shape[1], tables[0].dtype
    plans = []
    for t in tables:
        per_w = t.shape[0] // SC_WORKERS
        assert per_w * SC_WORKERS == t.shape[0]
        chunk = _sc_chunk(per_w, max_chunk)
        plans.append((per_w, chunk, per_w // chunk))
    cmax = max(c for _, c, _ in plans)
    n_t = len(tables)
    mesh = plsc.VectorSubcoreMesh(core_axis_name="c", subcore_axis_name="s")

    nb = SCATTER_BUFS
    lag = nb // 3
    scratch = []
    for _, chunk, _ in plans:
        for _ in range(nb):
            scratch += [pltpu.VMEM((chunk,), jnp.int32), pltpu.VMEM((chunk,), jnp.int32)]
    scratch += [pltpu.VMEM((cmax, d), dtype)] * nb
    scratch += [pltpu.SemaphoreType.DMA] * (2 * nb)

    @functools.partial(pl.kernel, mesh=mesh, out_type=jax.ShapeDtypeStruct((n_rows_out, d), dtype),
                       scratch_types=scratch)
    def scatter(*refs):
        tab_hbm = refs[0:n_t]
        sa_hbm = refs[n_t:2 * n_t]
        sb_hbm = refs[2 * n_t:3 * n_t]
        out_hbm = refs[3 * n_t]
        sc = refs[3 * n_t + 1:]
        idx_refs = sc[:2 * nb * n_t]
        rows = sc[2 * nb * n_t:2 * nb * n_t + nb]
        lsem = sc[2 * nb * n_t + nb:2 * nb * n_t + 2 * nb]
        ssem = sc[2 * nb * n_t + 2 * nb:]
        wid = _sc_worker_id()

        work = []
        for t, (per_w, chunk, n_chunks) in enumerate(plans):
            for j in range(n_chunks):
                work.append((t, wid * per_w + j * chunk, chunk))

        def parts(k):
            t, off, chunk = work[k]
            b = k % nb
            ia, ib = idx_refs[2 * nb * t + 2 * b], idx_refs[2 * nb * t + 2 * b + 1]
            rv = rows[b] if chunk == cmax else rows[b].at[pl.ds(0, chunk)]
            return t, off, chunk, b, ia, ib, rv

        def start_load(k):
            t, off, chunk, b, ia, ib, rv = parts(k)
            return (pltpu.async_copy(tab_hbm[t].at[pl.ds(off, chunk)], rv, lsem[b]),
                    pltpu.async_copy(sa_hbm[t].at[pl.ds(off, chunk)], ia, lsem[b]),
                    pltpu.async_copy(sb_hbm[t].at[pl.ds(off, chunk)], ib, lsem[b]))

        def start_scatter(k):
            t, off, chunk, b, ia, ib, rv = parts(k)
            return (pltpu.async_copy(rv, out_hbm.at[ia], ssem[b]), pltpu.async_copy(rv, out_hbm.at[ib], ssem[b]))

        loads, scatters = {}, {}
        for k in range(len(work) + lag):
            if k < len(work):
                if k >= nb:
                    for c in scatters.pop(k - nb):
                        c.wait()
                loads[k] = start_load(k)
            w = k - lag
            if w >= 0:
                for c in loads.pop(w):
                    c.wait()
                scatters[w] = start_scatter(w)
        for w in sorted(scatters):
            for c in scatters[w]:
                c.wait()

    return scatter(*tables, *slots_a, *slots_b)


def _experts_kernel(first_ref, nblk_ref, cnt_ref, tot_ref, xs_hbm, wg_ref, wu_ref, wd_ref, ys_hbm,
                    xbuf, ybuf, wg_bf, wu_bf, wd_bf, in_sem, out_sem):
    e = pl.program_id(0)
    nb = nblk_ref[e]
    first = first_ref[e]
    cnt = cnt_ref[e]
    total = tot_ref[0]
    half = D_MODEL // 2

    def in_copy(gb):
        slot = lax.rem(gb, X_BUFS)
        return pltpu.make_async_copy(xs_hbm.at[pl.ds(gb * BM, BM)], xbuf.at[slot], in_sem.at[slot])

    def out_copy(gb):
        slot = lax.rem(gb, Y_BUFS)
        return pltpu.make_async_copy(ybuf.at[slot], ys_hbm.at[pl.ds(gb * BM, BM)], out_sem.at[slot])

    @pl.when(nb > 0)
    def _():
        @pl.when(first == 0)
        def _():
            for k in range(X_LOOKAHEAD):
                @pl.when(k < total)
                def _():
                    in_copy(k).start(priority=ROW_DMA_PRIORITY)

        wg_bf[...] = wg_ref[0].astype(BF16)
        wu_bf[...] = wu_ref[0].astype(BF16)
        wd_bf[...] = wd_ref[0].astype(BF16)

        def acquire(gb):
            @pl.when(gb + X_LOOKAHEAD < total)
            def _():
                in_copy(gb + X_LOOKAHEAD).start(priority=ROW_DMA_PRIORITY)

            in_copy(gb).wait()

            @pl.when(gb >= Y_BUFS)
            def _():
                out_copy(gb - Y_BUFS).wait()

        def ffn(gb, j):
            live = lax.broadcasted_iota(jnp.int32, (BM, half), 0) < cnt - j * BM
            lo, hi = _unpack_bf16_pairs(jnp.where(live, xbuf[lax.rem(gb, X_BUFS)], jnp.uint32(0)))
            g = _dot(lo, wg_bf[0:half, :]) + _dot(hi, wg_bf[half:, :])
            u = _dot(lo, wu_bf[0:half, :]) + _dot(hi, wu_bf[half:, :])
            hm = (g * _sigmoid(g) * u).astype(BF16)
            y = _dot(hm, wd_bf[...])
            ybuf[lax.rem(gb, Y_BUFS)] = _pack_bf16_pairs(y.astype(BF16).astype(F32))

        def block_pair(jp, carry):
            j0 = 2 * jp
            g0 = first + j0
            acquire(g0)
            acquire(g0 + 1)
            ffn(g0, j0)
            ffn(g0 + 1, j0 + 1)
            out_copy(g0).start(priority=ROW_DMA_PRIORITY)
            out_copy(g0 + 1).start(priority=ROW_DMA_PRIORITY)
            return carry

        lax.fori_loop(0, nb // 2, block_pair, 0)

        @pl.when(lax.rem(nb, 2) == 1)
        def _():
            gl = first + nb - 1
            acquire(gl)
            ffn(gl, nb - 1)
            out_copy(gl).start(priority=ROW_DMA_PRIORITY)

        @pl.when(first + nb == total)
        def _():
            for k in range(Y_BUFS):
                @pl.when(total - 1 - k >= 0)
                def _():
                    out_copy(total - 1 - k).wait()


def _experts(xs, n_rows_out, first_block, n_blocks_e, counts, w_eg, w_eu, w_ed):
    w_map = lambda e, fb, nb, ct, tot: (e, 0, 0)
    half = D_MODEL // 2
    total = jnp.sum(n_blocks_e).astype(jnp.int32).reshape(1)
    return pl.pallas_call(
        _experts_kernel,
        grid_spec=pltpu.PrefetchScalarGridSpec(
            num_scalar_prefetch=4,
            grid=(N_EXPERTS,),
            in_specs=[pl.BlockSpec(memory_space=pl.ANY),
                      pl.BlockSpec((1, D_MODEL, D_EXPERT), w_map),
                      pl.BlockSpec((1, D_MODEL, D_EXPERT), w_map),
                      pl.BlockSpec((1, D_EXPERT, D_MODEL), w_map)],
            out_specs=pl.BlockSpec(memory_space=pl.ANY),
            scratch_shapes=[pltpu.VMEM((X_BUFS, BM, half), jnp.uint32), pltpu.VMEM((Y_BUFS, BM, half), jnp.uint32),
                            pltpu.VMEM((D_MODEL, D_EXPERT), BF16), pltpu.VMEM((D_MODEL, D_EXPERT), BF16),
                            pltpu.VMEM((D_EXPERT, D_MODEL), BF16),
                            pltpu.SemaphoreType.DMA((X_BUFS,)), pltpu.SemaphoreType.DMA((Y_BUFS,))]),
        out_shape=jax.ShapeDtypeStruct((n_rows_out, half), jnp.uint32),
        compiler_params=pltpu.CompilerParams(dimension_semantics=("arbitrary",), vmem_limit_bytes=VMEM_LIMIT),
        name="experts",
    )(first_block, n_blocks_e, counts, total, xs, w_eg, w_eu, w_ed)


def _combine_kernel(x2_ref, y1_ref, y2_ref, rt_ref, g_ref, o_ref):
    rt = rt_ref[...]
    r = jnp.transpose(jnp.concatenate([rt, jnp.zeros((LANES - rt.shape[0], rt.shape[1]), F32)], axis=0))
    g1, g2 = r[:, 2:3], r[:, 3:4]
    half = D_MODEL // 2
    y1_lo, y1_hi = _unpack_bf16_pairs_f32(y1_ref[...])
    y2_lo, y2_hi = _unpack_bf16_pairs_f32(y2_ref[...])
    x_lo = x2_ref[:, 0:half] + g1 * y1_lo + g2 * y2_lo
    x_hi = x2_ref[:, half:] + g1 * y1_hi + g2 * y2_hi
    ms = (jnp.sum(x_lo * x_lo, axis=-1, keepdims=True) + jnp.sum(x_hi * x_hi, axis=-1, keepdims=True)) / D_MODEL
    inv = lax.rsqrt(ms + EPS)
    o_ref[:, 0:half] = x_lo * inv * g_ref[:, 0:half]
    o_ref[:, half:] = x_hi * inv * g_ref[:, half:]


def _combine(x2, yg, rt, g_final, tm, blk1, blk2):
    n = x2.shape[0]
    return pl.pallas_call(
        _combine_kernel,
        grid=(n // tm,),
        in_specs=[pl.BlockSpec((tm, D_MODEL), lambda i: (i, 0)),
                  pl.BlockSpec((tm, D_MODEL // 2), lambda i: (blk1 + i, 0)),
                  pl.BlockSpec((tm, D_MODEL // 2), lambda i: (blk2 + i, 0)),
                  pl.BlockSpec((SUBLANES, tm), lambda i: (0, i)),
                  pl.BlockSpec((1, D_MODEL), lambda i: (0, 0))],
        out_specs=pl.BlockSpec((tm, D_MODEL), lambda i: (i, 0)),
        out_shape=jax.ShapeDtypeStruct((n, D_MODEL), F32),
        compiler_params=pltpu.CompilerParams(dimension_semantics=("arbitrary",), vmem_limit_bytes=VMEM_LIMIT),
        name="combine",
    )(x2, yg, yg, rt, g_final)


def _gather_rows(table, idx):
    return _sc_gather_rows(table, idx)


def _scatter_rows2(tables, slots_a, slots_b, n_rows_out):
    return _sc_scatter_rows2(tables, slots_a, slots_b, n_rows_out, SCATTER_CHUNK)


def kernel(x_prompt, x_sample, mem_prompt, state_conv, cache_mem_k, cache_mem_v, g_mix, w_in, conv_w, conv_b, ln_conv_g, ln_conv_b, ln_v_g, ln_v_b, w_sg, b_sg, w_out, g_mem, w_mk, w_mv, g_xattn, w_xq, w_xo, g_ffn, w_router_group, b_router_group, w_router_expert, b_router_expert, w_expert_gate, w_expert_up, w_expert_down, g_final):
    assert x_prompt.shape[0] == 1 and g_mix.shape[0] == 1
    n_p = x_prompt.shape[1]
    n_batch, t_len = x_sample.shape[0], x_sample.shape[1]
    n_s = n_batch * t_len
    row = lambda a: a.reshape(1, -1)

    w_router = jnp.concatenate(
        [w_router_group[0], jnp.transpose(w_router_expert[0], (1, 0, 2)).reshape(D_MODEL, N_EXPERTS)], axis=1)
    w_router = jnp.pad(w_router, ((0, 0), (0, LOGIT_LANES - w_router.shape[1]))).astype(BF16)
    b_router = jnp.pad(jnp.concatenate([b_router_group[0], b_router_expert[0].reshape(-1)]),
                       (0, LOGIT_LANES - N_GROUPS - N_EXPERTS)).reshape(1, LOGIT_LANES)
    tril_t = jnp.tril(jnp.ones((t_len, t_len), bool))
    w_sg_t = jnp.where(tril_t, w_sg[0][:, :t_len, :t_len], 0.0)
    eye_b = jnp.eye(n_batch, dtype=F32)
    w_sg_bd = jnp.einsum("ab,hij->haibj", eye_b, w_sg_t).reshape(SG_HEADS, n_s, n_s).astype(BF16)
    p = {
        "g_mix": row(g_mix[0]), "w_in": w_in[0],
        "conv_w": jnp.pad(conv_w[0], ((0, 1), (0, 0))), "conv_b": row(conv_b[0]),
        "ln_conv_g": row(ln_conv_g[0]), "ln_conv_b": row(ln_conv_b[0]),
        "ln_v_g": row(ln_v_g[0]), "ln_v_b": row(ln_v_b[0]),
        "w_sg": w_sg[0],
        "b_sg_rows": jnp.repeat(b_sg[0].T, SG_HEAD_DIM, axis=1),
        "w_sg_bd": w_sg_bd,
        "b_sg_rows_s": jnp.tile(jnp.repeat(b_sg[0][:, :t_len].T, SG_HEAD_DIM, axis=1), (n_batch, 1)),
        "w_out": w_out[0], "g_xattn": row(g_xattn[0]),
        "w_xq": w_xq[0], "w_xo": w_xo[0], "g_ffn": row(g_ffn[0]),
        "w_router": w_router, "b_router": b_router,
        "lower": jnp.tril(jnp.ones((TM, TM), BF16), -1),
    }

    k_p, v_p = _memkv(mem_prompt[0], row(g_mem[0]), w_mk[0], w_mv[0])
    p["k"] = k_p.astype(BF16)
    p["v"] = v_p.astype(BF16)
    p["k_s"] = jnp.transpose(cache_mem_k[0].reshape(n_batch, N_MEM, D_MODEL), (0, 2, 1)).astype(BF16)
    p["v_s"] = cache_mem_v[0].reshape(n_batch, N_MEM, D_MODEL).astype(BF16)

    assert n_p % n_s == 0
    x2_p, h3_p, rt_p, hist_p, cnt_p = _trunk_prompt(x_prompt[0], p)
    x2_s, h3_s, rt_s, hist_s, sgv_s, cnt = _trunk_sample(
        x_sample.reshape(n_s, D_MODEL), state_conv[0], cnt_p, p, n_batch, t_len)

    experts = jnp.arange(N_EXPERTS, dtype=jnp.int32)
    w_e = (w_expert_gate[0], w_expert_up[0], w_expert_down[0])

    def moe_pass(cnt, h3_tables, rts, n_real):
        n_tot = sum(n_real)
        n_slots = -(-(n_tot * 2) // BM) * BM + N_EXPERTS * BM
        counts = cnt[0, :N_EXPERTS].astype(jnp.int32)
        padded = (counts + BM - 1) // BM * BM
        pad_start = jnp.cumsum(padded) - padded

        def one(e_row, rank_row):
            e = e_row.astype(jnp.int32)
            start = jnp.sum(jnp.where(e[None, :] == experts[:, None], pad_start[:, None], 0), axis=0)
            return start + rank_row.astype(jnp.int32)

        slots = [(one(rt[0], rt[4]), one(rt[1], rt[5])) for rt in rts]
        sa, sb, spare0 = [], [], n_slots
        for tab, (a, b), n in zip(h3_tables, slots, n_real):
            n_spare = tab.shape[0] - n
            spare = spare0 + jnp.arange(n_spare, dtype=jnp.int32)
            sa.append(jnp.concatenate([a, spare]))
            sb.append(jnp.concatenate([b, spare + n_spare]))
            spare0 += 2 * n_spare
        xs = _scatter_rows2(tuple(h3_tables), tuple(sa), tuple(sb), spare0)
        ys = _experts(xs, n_slots, pad_start // BM, padded // BM, counts, *w_e)
        back_idx = jnp.concatenate([s for ab in slots for s in ab])
        n_back = -(-back_idx.shape[0] // (SC_WORKERS * GATHER_ROWS)) * (SC_WORKERS * GATHER_ROWS)
        return _gather_rows(ys, jnp.pad(back_idx, (0, n_back - back_idx.shape[0])))

    yg = moe_pass(cnt, [h3_p, h3_s], [rt_p, rt_s], [n_p, n_s])

    gf = row(g_final)
    y_p = _combine(x2_p, yg, rt_p, gf, TM_COMBINE, 0, n_p // TM_COMBINE)
    y_s = _combine(x2_s, yg, rt_s, gf, n_s, 2 * n_p // n_s, 2 * n_p // n_s + 1)

    return (y_p.reshape(1, n_p, D_MODEL),
            y_s.reshape(n_batch, t_len, D_MODEL),
            hist_p[HALO - HIST:].reshape(1, 1, HIST, D_CONV),
            hist_s.reshape(1, n_batch, HIST, D_CONV),
            k_p.reshape(1, 1, N_MEM, X_HEADS, X_HEAD_DIM),
            v_p.reshape(1, 1, N_MEM, X_HEADS, X_HEAD_DIM),
            sgv_s.reshape(1, n_batch, t_len, D_SG))
```

```python
import functools

import jax
import jax.numpy as jnp
from jax import lax
from jax.experimental import pallas as pl
from jax.experimental.pallas import tpu as pltpu
from jax.experimental.pallas import tpu_sc as plsc

D_MODEL = 1024
D_CONV = 512
D_SG = 512
CONV_WIDTH = 31
HIST = CONV_WIDTH - 1
SG_HEADS = 4
SG_HEAD_DIM = 128
SG_CHUNK = 128
N_MEM = 256
X_HEADS = 4
X_HEAD_DIM = 256
N_GROUPS = 4
EXPERTS_PER_GROUP = 8
N_EXPERTS = 32
D_EXPERT = 512
EPS = 1e-6

LANES = 128
SUBLANES = 8
SC_CORES = 2
SC_SUBCORES = 16
SC_WORKERS = SC_CORES * SC_SUBCORES
VMEM_LIMIT = 56 * 1024 * 1024

TM = 512
TM_COMBINE = 1024
HALO = 32
SEG = TM // SUBLANES
SEG_HALO = 32
CONV_BLOCK = 16
CAST_ROWS = 64
BM = 256
X_LOOKAHEAD = 4
X_BUFS = X_LOOKAHEAD + 2
Y_BUFS = 4
ROW_DMA_PRIORITY = 1
GATHER_ROWS = 32
GATHER_BUFS = 6
SCATTER_CHUNK = 32
SCATTER_BUFS = 6
LOGIT_LANES = 128

F32 = jnp.float32
BF16 = jnp.bfloat16


def _dot(a, b):
    return jnp.dot(a, b, preferred_element_type=F32)


def _rms(x, g):
    return x * lax.rsqrt(jnp.mean(x * x, axis=-1, keepdims=True) + EPS) * g


def _ln(x, g, b):
    mu = jnp.mean(x, axis=-1, keepdims=True)
    xc = x - mu
    var = jnp.mean(xc * xc, axis=-1, keepdims=True)
    return xc * lax.rsqrt(var + EPS) * g + b


def _sigmoid(x):
    return 1.0 / (1.0 + jnp.exp(-x))


def _pack_bf16_pairs(h):
    bits = lax.bitcast_convert_type(h, jnp.uint32)
    half = h.shape[1] // 2
    lo = lax.shift_right_logical(bits[:, :half], jnp.uint32(16))
    hi = bits[:, half:] & jnp.uint32(0xFFFF0000)
    return hi | lo


def _unpack_bf16_pairs_f32(p):
    lo = lax.bitcast_convert_type(lax.shift_left(p, jnp.uint32(16)), F32)
    hi = lax.bitcast_convert_type(p & jnp.uint32(0xFFFF0000), F32)
    return lo, hi


def _unpack_bf16_pairs(p):
    lo, hi = _unpack_bf16_pairs_f32(p)
    return lo.astype(BF16), hi.astype(BF16)


def _memkv_kernel(mem_ref, g_ref, wk_ref, wv_ref, k_ref, v_ref):
    m = _rms(mem_ref[...], g_ref[...]).astype(BF16)
    k_ref[...] = _dot(m, wk_ref[...].astype(BF16))
    v_ref[...] = _dot(m, wv_ref[...].astype(BF16))


def _memkv(mem, g_mem, w_mk, w_mv):
    return pl.pallas_call(
        _memkv_kernel,
        out_shape=(jax.ShapeDtypeStruct((N_MEM, D_MODEL), F32), jax.ShapeDtypeStruct((N_MEM, D_MODEL), F32)),
        compiler_params=pltpu.CompilerParams(vmem_limit_bytes=VMEM_LIMIT),
        name="memkv",
    )(mem, g_mem, w_mk, w_mv)


def _attn_heads(q, k, v, k_transposed):
    outs = []
    for h in range(X_HEADS):
        sl = slice(h * X_HEAD_DIM, (h + 1) * X_HEAD_DIM)
        if k_transposed:
            s = _dot(q[:, sl], k[sl, :])
        else:
            s = lax.dot_general(q[:, sl], k[:, sl], (((1,), (1,)), ((), ())), preferred_element_type=F32)
        s = s * (X_HEAD_DIM ** -0.5)
        s = s - jnp.max(s, axis=-1, keepdims=True)
        p = jnp.exp(s)
        p = p / jnp.sum(p, axis=-1, keepdims=True)
        outs.append(_dot(p.astype(BF16), v[:, sl]).astype(BF16))
    return jnp.concatenate(outs, axis=1)


def _route(logits, run, strict_lower):
    m = logits.shape[0]
    lane = lax.broadcasted_iota(jnp.int32, (m, LOGIT_LANES), 1).astype(F32)
    neg = jnp.float32(-jnp.inf)
    big = jnp.float32(LOGIT_LANES)

    def first_argmax(vals):
        mx = jnp.max(vals, axis=-1, keepdims=True)
        idx = jnp.min(jnp.where(vals == mx, lane, big), axis=-1, keepdims=True)
        return mx, idx

    lg = jnp.where(lane < N_GROUPS, logits, neg)
    g_max, g_idx = first_argmax(lg)
    g_w = 1.0 / jnp.sum(jnp.exp(lg - g_max), axis=-1, keepdims=True)

    lo = N_GROUPS + g_idx * EXPERTS_PER_GROUP
    le = jnp.where((lane >= lo) & (lane < lo + EXPERTS_PER_GROUP), logits, neg)
    v1, i1 = first_argmax(le)
    v2, i2 = first_argmax(jnp.where(lane == i1, neg, le))
    t = jnp.exp(v2 - v1)
    gate1 = g_w / (1.0 + t)
    gate2 = g_w * t / (1.0 + t)
    e1 = i1 - N_GROUPS
    e2 = i2 - N_GROUPS

    oh1 = (lane == e1).astype(F32)
    oh2 = (lane == e2).astype(F32)
    oh = oh1 + oh2
    before = _dot(strict_lower, oh.astype(BF16)) + run
    rank1 = jnp.sum(before * oh1, axis=-1, keepdims=True)
    rank2 = jnp.sum(before * oh2, axis=-1, keepdims=True)
    new_run = run + jnp.sum(oh, axis=0, keepdims=True)

    rinfo = jnp.where(lane == 0, e1,
            jnp.where(lane == 1, e2,
            jnp.where(lane == 2, gate1,
            jnp.where(lane == 3, gate2,
            jnp.where(lane == 4, rank1,
            jnp.where(lane == 5, rank2, 0.0))))))
    return jnp.transpose(rinfo)[0:SUBLANES, :], new_run


def _conv_segments(a, w_ref, seg_ref, tail_ref, yseg_ref, conv_ref):
    sub = lax.broadcasted_iota(jnp.int32, (SUBLANES, LANES), 0)
    for lt in range(D_CONV // LANES):
        ls = slice(lt * LANES, (lt + 1) * LANES)
        for t0 in range(0, TM, SUBLANES):
            s, m = divmod(t0, SEG)
            seg_ref[lt, pl.ds((SEG_HALO + m) * SUBLANES + s, SUBLANES, stride=SUBLANES), :] = a[t0:t0 + SUBLANES, ls]
        for j in range(SEG_HALO):
            cur = seg_ref[lt, (SEG + j) * SUBLANES:(SEG + j + 1) * SUBLANES, :]
            prev = tail_ref[lt, j * SUBLANES:(j + 1) * SUBLANES, :]
            seg_ref[lt, j * SUBLANES:(j + 1) * SUBLANES, :] = jnp.where(
                sub == 0, pltpu.roll(prev, 1, axis=0), pltpu.roll(cur, 1, axis=0))
            tail_ref[lt, j * SUBLANES:(j + 1) * SUBLANES, :] = cur
        for m0 in range(0, SEG, CONV_BLOCK):
            acc = [jnp.zeros((SUBLANES, LANES), F32) for _ in range(CONV_BLOCK)]
            for idx in range(m0 - HIST, m0 + CONV_BLOCK):
                b = seg_ref[lt, (SEG_HALO + idx) * SUBLANES:(SEG_HALO + idx + 1) * SUBLANES, :]
                for m in range(max(m0, idx), min(m0 + CONV_BLOCK, idx + CONV_WIDTH)):
                    k = idx - m + HIST
                    acc[m - m0] = acc[m - m0] + b * w_ref[k:k + 1, ls]
            for m in range(m0, m0 + CONV_BLOCK):
                yseg_ref[lt, m * SUBLANES:(m + 1) * SUBLANES, :] = acc[m - m0]
        for t0 in range(0, TM, SUBLANES):
            s, m = divmod(t0, SEG)
            conv_ref[t0:t0 + SUBLANES, ls] = yseg_ref[lt, pl.ds(m * SUBLANES + s, SUBLANES, stride=SUBLANES), :]


def _cast_rows(src_ref, dst_ref):
    rows = src_ref.shape[0]

    def body(c, carry):
        r0 = pl.multiple_of(c * CAST_ROWS, CAST_ROWS)
        dst_ref[pl.ds(r0, CAST_ROWS), :] = src_ref[pl.ds(r0, CAST_ROWS), :].astype(BF16)
        return carry

    lax.fori_loop(0, rows // CAST_ROWS, body, 0)


def _trunk_prompt_kernel(x_ref, gmix_ref, win32_ref, convw_ref, convb_ref, lncg_ref, lncb_ref, lnvg_ref, lnvb_ref,
                         wsg_ref, bsg_ref, wout32_ref, gx_ref, wxq32_ref, kmem_ref, v_ref, wxo32_ref, gffn_ref, wr_ref,
                         br_ref, lower_ref,
                         x2_ref, h3_ref, rt_ref, hist_ref, cnt_ref,
                         seg_ref, tail_ref, yseg_ref, conv_ref, run_ref, win_ref, wout_ref, wxq_ref, wxo_ref):
    i = pl.program_id(0)

    @pl.when(i == 0)
    def _():
        tail_ref[...] = jnp.zeros(tail_ref.shape, F32)
        run_ref[...] = jnp.zeros((1, LOGIT_LANES), F32)
        _cast_rows(win32_ref, win_ref)
        _cast_rows(wout32_ref, wout_ref)
        _cast_rows(wxq32_ref, wxq_ref)
        _cast_rows(wxo32_ref, wxo_ref)

    x = x_ref[...]
    h = _rms(x, gmix_ref[...]).astype(BF16)

    a_in = _dot(h, win_ref[:, 0:D_CONV])
    a_gate = _dot(h, win_ref[:, D_CONV:2 * D_CONV])
    a = a_in * _sigmoid(a_gate)
    hist_ref[...] = a[TM - HALO:, :]
    _conv_segments(a, convw_ref, seg_ref, tail_ref, yseg_ref, conv_ref)

    y = _ln(conv_ref[...] + convb_ref[...], lncg_ref[...], lncb_ref[...])
    a_out = (y * _sigmoid(y)).astype(BF16)

    u = _dot(h, win_ref[:, 2 * D_CONV:2 * D_CONV + D_SG])
    v = _ln(_dot(h, win_ref[:, 2 * D_CONV + D_SG:]), lnvg_ref[...], lnvb_ref[...]).astype(BF16)
    ri = lax.broadcasted_iota(jnp.int32, (SG_CHUNK, SG_CHUNK), 0)
    ci = lax.broadcasted_iota(jnp.int32, (SG_CHUNK, SG_CHUNK), 1)
    w_tril = [jnp.where(ci <= ri, wsg_ref[hh], 0.0).astype(BF16) for hh in range(SG_HEADS)]
    gate_rows = []
    for c in range(TM // SG_CHUNK):
        rs = slice(c * SG_CHUNK, (c + 1) * SG_CHUNK)
        heads = [_dot(w_tril[hh], v[rs, hh * SG_HEAD_DIM:(hh + 1) * SG_HEAD_DIM]) for hh in range(SG_HEADS)]
        gate_rows.append(jnp.concatenate(heads, axis=1) + bsg_ref[...])
    b_out = (u * jnp.concatenate(gate_rows, axis=0)).astype(BF16)

    x1 = x + _dot(a_out, wout_ref[0:D_CONV, :]) + _dot(b_out, wout_ref[D_CONV:, :])

    hx = _rms(x1, gx_ref[...]).astype(BF16)
    q = _dot(hx, wxq_ref[...]).astype(BF16)
    x2 = x1 + _dot(_attn_heads(q, kmem_ref[...], v_ref[...], False), wxo_ref[...])
    x2_ref[...] = x2

    h3 = _rms(x2, gffn_ref[...]).astype(BF16)
    h3_ref[...] = _pack_bf16_pairs(h3.astype(F32))
    rt, new_run = _route(_dot(h3, wr_ref[...]) + br_ref[...], run_ref[...], lower_ref[...])
    rt_ref[...] = rt
    run_ref[...] = new_run
    cnt_ref[...] = new_run


def _const_spec(shape):
    nd = len(shape)
    return pl.BlockSpec(shape, lambda i: (0,) * nd, pipeline_mode=pl.Buffered(1))


def _trunk_prompt(x, p):
    n = x.shape[0]
    assert n % TM == 0
    row = lambda w: pl.BlockSpec((TM, w), lambda i: (i, 0))
    consts = [p["g_mix"], p["w_in"], p["conv_w"], p["conv_b"], p["ln_conv_g"], p["ln_conv_b"], p["ln_v_g"],
              p["ln_v_b"], p["w_sg"], p["b_sg_rows"], p["w_out"], p["g_xattn"], p["w_xq"], p["k"], p["v"],
              p["w_xo"], p["g_ffn"], p["w_router"], p["b_router"], p["lower"]]
    return pl.pallas_call(
        _trunk_prompt_kernel,
        grid=(n // TM,),
        in_specs=[row(D_MODEL)] + [_const_spec(c.shape) for c in consts],
        out_specs=(row(D_MODEL), row(D_MODEL // 2),
                   pl.BlockSpec((SUBLANES, TM), lambda i: (0, i)),
                   pl.BlockSpec((HALO, D_CONV), lambda i: (0, 0)),
                   pl.BlockSpec((1, LOGIT_LANES), lambda i: (0, 0))),
        out_shape=(jax.ShapeDtypeStruct((n, D_MODEL), F32),
                   jax.ShapeDtypeStruct((n, D_MODEL // 2), jnp.uint32),
                   jax.ShapeDtypeStruct((SUBLANES, n), F32),
                   jax.ShapeDtypeStruct((HALO, D_CONV), F32),
                   jax.ShapeDtypeStruct((1, LOGIT_LANES), F32)),
        scratch_shapes=[pltpu.VMEM((D_CONV // LANES, (SEG_HALO + SEG) * SUBLANES, LANES), F32),
                        pltpu.VMEM((D_CONV // LANES, SEG_HALO * SUBLANES, LANES), F32),
                        pltpu.VMEM((D_CONV // LANES, TM, LANES), F32),
                        pltpu.VMEM((TM, D_CONV), F32),
                        pltpu.VMEM((1, LOGIT_LANES), F32),
                        pltpu.VMEM(p["w_in"].shape, BF16), pltpu.VMEM(p["w_out"].shape, BF16),
                        pltpu.VMEM(p["w_xq"].shape, BF16), pltpu.VMEM(p["w_xo"].shape, BF16)],
        compiler_params=pltpu.CompilerParams(dimension_semantics=("arbitrary",), vmem_limit_bytes=VMEM_LIMIT),
        name="trunk_prompt",
    )(x, *consts)


def _trunk_sample_kernel(n_batch, t_len,
                         x_ref, hist_in_ref, run_in_ref, gmix_ref, win_ref, convw_ref, convb_ref, lncg_ref, lncb_ref,
                         lnvg_ref, lnvb_ref, wsgbd_ref, bsg_ref, wout_ref, gx_ref, wxq_ref, kmem_ref, v_ref, wxo_ref,
                         gffn_ref, wr_ref, br_ref, lower_ref,
                         x2_ref, h3_ref, rt_ref, hist_ref, sgv_ref, cnt_ref,
                         ext_ref, conv_ref, att_ref):
    x = x_ref[...]
    h = _rms(x, gmix_ref[...]).astype(BF16)
    z = _dot(h, win_ref[...].astype(BF16))
    a = z[:, 0:D_CONV] * _sigmoid(z[:, D_CONV:2 * D_CONV])
    ext_len = HIST + t_len
    for b in range(n_batch):
        ext_ref[b, 0:HIST, :] = hist_in_ref[b]
        ext_ref[b, HIST:ext_len, :] = a[b * t_len:(b + 1) * t_len, :]
    for b in range(n_batch):
        acc = jnp.zeros((t_len, D_CONV), F32)
        for k in range(CONV_WIDTH):
            acc = acc + ext_ref[b, k:k + t_len, :] * convw_ref[k:k + 1, :]
        conv_ref[b * t_len:(b + 1) * t_len, :] = acc
        hist_ref[b] = ext_ref[b, ext_len - HIST:ext_len, :]

    y = _ln(conv_ref[...] + convb_ref[...], lncg_ref[...], lncb_ref[...])
    a_out = (y * _sigmoid(y)).astype(BF16)

    u = z[:, 2 * D_CONV:2 * D_CONV + D_SG]
    v = _ln(z[:, 2 * D_CONV + D_SG:], lnvg_ref[...], lnvb_ref[...])
    sgv_ref[...] = v
    vb = v.astype(BF16)
    heads = [_dot(wsgbd_ref[hh], vb[:, hh * SG_HEAD_DIM:(hh + 1) * SG_HEAD_DIM]) for hh in range(SG_HEADS)]
    b_out = (u * (jnp.concatenate(heads, axis=1) + bsg_ref[...])).astype(BF16)

    x1 = (x + _dot(a_out, wout_ref[0:D_CONV, :].astype(BF16))
          + _dot(b_out, wout_ref[D_CONV:, :].astype(BF16)))

    hx = _rms(x1, gx_ref[...]).astype(BF16)
    q = _dot(hx, wxq_ref[...].astype(BF16)).astype(BF16)
    for b in range(n_batch):
        rs = slice(b * t_len, (b + 1) * t_len)
        att_ref[rs, :] = _attn_heads(q[rs, :], kmem_ref[b], v_ref[b], True)
    x2 = x1 + _dot(att_ref[...], wxo_ref[...].astype(BF16))
    x2_ref[...] = x2

    h3 = _rms(x2, gffn_ref[...]).astype(BF16)
    m = n_batch * t_len
    h3_ref[0:m, :] = _pack_bf16_pairs(h3.astype(F32))
    if h3_ref.shape[0] > m:
        h3_ref[m:, :] = jnp.zeros((h3_ref.shape[0] - m, D_MODEL // 2), jnp.uint32)
    rt, new_run = _route(_dot(h3, wr_ref[...]) + br_ref[...], run_in_ref[...], lower_ref[...])
    rt_ref[...] = rt
    cnt_ref[...] = new_run


def _trunk_sample(x, hist, run, p, n_batch, t_len):
    m = n_batch * t_len
    args = [x, hist, run, p["g_mix"], p["w_in"], p["conv_w"], p["conv_b"], p["ln_conv_g"], p["ln_conv_b"],
            p["ln_v_g"], p["ln_v_b"], p["w_sg_bd"], p["b_sg_rows_s"], p["w_out"], p["g_xattn"], p["w_xq"],
            p["k_s"], p["v_s"], p["w_xo"], p["g_ffn"], p["w_router"], p["b_router"], p["lower"][:m, :m]]
    return pl.pallas_call(
        functools.partial(_trunk_sample_kernel, n_batch, t_len),
        out_shape=(jax.ShapeDtypeStruct((m, D_MODEL), F32),
                   jax.ShapeDtypeStruct((-(-m // (SC_WORKERS * SUBLANES)) * SC_WORKERS * SUBLANES, D_MODEL // 2),
                                        jnp.uint32),
                   jax.ShapeDtypeStruct((SUBLANES, m), F32),
                   jax.ShapeDtypeStruct((n_batch, HIST, D_CONV), F32),
                   jax.ShapeDtypeStruct((m, D_SG), F32),
                   jax.ShapeDtypeStruct((1, LOGIT_LANES), F32)),
        scratch_shapes=[pltpu.VMEM((n_batch, HIST + t_len, D_CONV), F32),
                        pltpu.VMEM((m, D_CONV), F32),
                        pltpu.VMEM((m, D_MODEL), BF16)],
        compiler_params=pltpu.CompilerParams(vmem_limit_bytes=VMEM_LIMIT),
        name="trunk_sample",
    )(*args)


def _sc_worker_id():
    return lax.axis_index("s") * SC_CORES + lax.axis_index("c")


def _sc_chunk(per_w, max_chunk):
    assert per_w % SUBLANES == 0 and max_chunk <= LANES
    return max(c for c in range(SUBLANES, max_chunk + 1, SUBLANES) if per_w % c == 0)


def _sc_gather_rows(table, idx):
    n_rows, d = idx.shape[0], table.shape[1]
    per_w = n_rows // SC_WORKERS
    assert per_w * SC_WORKERS == n_rows and per_w % GATHER_ROWS == 0
    n_chunks = per_w // GATHER_ROWS
    lag = GATHER_BUFS // 2
    mesh = plsc.VectorSubcoreMesh(core_axis_name="c", subcore_axis_name="s")

    @functools.partial(
        pl.kernel, mesh=mesh,
        out_type=jax.ShapeDtypeStruct((n_rows, d), table.dtype),
        scratch_types=([pltpu.VMEM((per_w,), jnp.int32)]
                       + [pltpu.VMEM((GATHER_ROWS, d), table.dtype)] * GATHER_BUFS
                       + [pltpu.SemaphoreType.DMA] * (2 * GATHER_BUFS)),
    )
    def gather(table_hbm, idx_hbm, out_hbm, idx_all, *rest):
        rows = rest[:GATHER_BUFS]
        gsem = rest[GATHER_BUFS:2 * GATHER_BUFS]
        wsem = rest[2 * GATHER_BUFS:]
        base = _sc_worker_id() * per_w
        pltpu.sync_copy(idx_hbm.at[pl.ds(base, per_w)], idx_all)

        reads, writes = {}, {}
        for c in range(n_chunks + lag):
            if c < n_chunks:
                b = c % GATHER_BUFS
                if c >= GATHER_BUFS:
                    writes.pop(c - GATHER_BUFS).wait()
                reads[c] = pltpu.async_copy(
                    table_hbm.at[idx_all.at[pl.ds(c * GATHER_ROWS, GATHER_ROWS)]], rows[b], gsem[b])
            w = c - lag
            if w >= 0:
                b = w % GATHER_BUFS
                reads.pop(w).wait()
                writes[w] = pltpu.async_copy(
                    rows[b], out_hbm.at[pl.ds(base + w * GATHER_ROWS, GATHER_ROWS)], wsem[b])
        for w in sorted(writes):
            writes[w].wait()

    return gather(table, idx)


def _sc_scatter_rows2(tables, slots_a, slots_b, n_rows_out, max_chunk):
    d, dtype = tables[0].shape[1], tables[0].dtype
    plans = []
    for t in tables:
        per_w = t.shape[0] // SC_WORKERS
        assert per_w * SC_WORKERS == t.shape[0]
        chunk = _sc_chunk(per_w, max_chunk)
        plans.append((per_w, chunk, per_w // chunk))
    cmax = max(c for _, c, _ in plans)
    n_t = len(tables)
    mesh = plsc.VectorSubcoreMesh(core_axis_name="c", subcore_axis_name="s")

    nb = SCATTER_BUFS
    lag = nb // 3
    scratch = []
    for _, chunk, _ in plans:
        for _ in range(nb):
            scratch += [pltpu.VMEM((chunk,), jnp.int32), pltpu.VMEM((chunk,), jnp.int32)]
    scratch += [pltpu.VMEM((cmax, d), dtype)] * nb
    scratch += [pltpu.SemaphoreType.DMA] * (2 * nb)

    @functools.partial(pl.kernel, mesh=mesh, out_type=jax.ShapeDtypeStruct((n_rows_out, d), dtype),
                       scratch_types=scratch)
    def scatter(*refs):
        tab_hbm = refs[0:n_t]
        sa_hbm = refs[n_t:2 * n_t]
        sb_hbm = refs[2 * n_t:3 * n_t]
        out_hbm = refs[3 * n_t]
        sc = refs[3 * n_t + 1:]
        idx_refs = sc[:2 * nb * n_t]
        rows = sc[2 * nb * n_t:2 * nb * n_t + nb]
        lsem = sc[2 * nb * n_t + nb:2 * nb * n_t + 2 * nb]
        ssem = sc[2 * nb * n_t + 2 * nb:]
        wid = _sc_worker_id()

        work = []
        for t, (per_w, chunk, n_chunks) in enumerate(plans):
            for j in range(n_chunks):
                work.append((t, wid * per_w + j * chunk, chunk))

        def parts(k):
            t, off, chunk = work[k]
            b = k % nb
            ia, ib = idx_refs[2 * nb * t + 2 * b], idx_refs[2 * nb * t + 2 * b + 1]
            rv = rows[b] if chunk == cmax else rows[b].at[pl.ds(0, chunk)]
            return t, off, chunk, b, ia, ib, rv

        def start_load(k):
            t, off, chunk, b, ia, ib, rv = parts(k)
            return (pltpu.async_copy(tab_hbm[t].at[pl.ds(off, chunk)], rv, lsem[b]),
                    pltpu.async_copy(sa_hbm[t].at[pl.ds(off, chunk)], ia, lsem[b]),
                    pltpu.async_copy(sb_hbm[t].at[pl.ds(off, chunk)], ib, lsem[b]))

        def start_scatter(k):
            t, off, chunk, b, ia, ib, rv = parts(k)
            return (pltpu.async_copy(rv, out_hbm.at[ia], ssem[b]), pltpu.async_copy(rv, out_hbm.at[ib], ssem[b]))

        loads, scatters = {}, {}
        for k in range(len(work) + lag):
            if k < len(work):
                if k >= nb:
                    for c in scatters.pop(k - nb):
                        c.wait()
                loads[k] = start_load(k)
            w = k - lag
            if w >= 0:
                for c in loads.pop(w):
                    c.wait()
                scatters[w] = start_scatter(w)
        for w in sorted(scatters):
            for c in scatters[w]:
                c.wait()

    return scatter(*tables, *slots_a, *slots_b)


def _experts_kernel(first_ref, nblk_ref, cnt_ref, tot_ref, xs_hbm, wg_ref, wu_ref, wd_ref, ys_hbm,
                    xbuf, ybuf, wg_bf, wu_bf, wd_bf, in_sem, out_sem):
    e = pl.program_id(0)
    nb = nblk_ref[e]
    first = first_ref[e]
    cnt = cnt_ref[e]
    total = tot_ref[0]
    half = D_MODEL // 2

    def in_copy(gb):
        slot = lax.rem(gb, X_BUFS)
        return pltpu.make_async_copy(xs_hbm.at[pl.ds(gb * BM, BM)], xbuf.at[slot], in_sem.at[slot])

    def out_copy(gb):
        slot = lax.rem(gb, Y_BUFS)
        return pltpu.make_async_copy(ybuf.at[slot], ys_hbm.at[pl.ds(gb * BM, BM)], out_sem.at[slot])

    @pl.when(nb > 0)
    def _():
        @pl.when(first == 0)
        def _():
            for k in range(X_LOOKAHEAD):
                @pl.when(k < total)
                def _():
                    in_copy(k).start(priority=ROW_DMA_PRIORITY)

        wg_bf[...] = wg_ref[0].astype(BF16)
        wu_bf[...] = wu_ref[0].astype(BF16)
        wd_bf[...] = wd_ref[0].astype(BF16)

        def acquire(gb):
            @pl.when(gb + X_LOOKAHEAD < total)
            def _():
                in_copy(gb + X_LOOKAHEAD).start(priority=ROW_DMA_PRIORITY)

            in_copy(gb).wait()

            @pl.when(gb >= Y_BUFS)
            def _():
                out_copy(gb - Y_BUFS).wait()

        def ffn(gb, j):
            live = lax.broadcasted_iota(jnp.int32, (BM, half), 0) < cnt - j * BM
            lo, hi = _unpack_bf16_pairs(jnp.where(live, xbuf[lax.rem(gb, X_BUFS)], jnp.uint32(0)))
            g = _dot(lo, wg_bf[0:half, :]) + _dot(hi, wg_bf[half:, :])
            u = _dot(lo, wu_bf[0:half, :]) + _dot(hi, wu_bf[half:, :])
            hm = (g * _sigmoid(g) * u).astype(BF16)
            y = _dot(hm, wd_bf[...])
            ybuf[lax.rem(gb, Y_BUFS)] = _pack_bf16_pairs(y.astype(BF16).astype(F32))

        def block_pair(jp, carry):
            j0 = 2 * jp
            g0 = first + j0
            acquire(g0)
            acquire(g0 + 1)
            ffn(g0, j0)
            ffn(g0 + 1, j0 + 1)
            out_copy(g0).start(priority=ROW_DMA_PRIORITY)
            out_copy(g0 + 1).start(priority=ROW_DMA_PRIORITY)
            return carry

        lax.fori_loop(0, nb // 2, block_pair, 0)

        @pl.when(lax.rem(nb, 2) == 1)
        def _():
            gl = first + nb - 1
            acquire(gl)
            ffn(gl, nb - 1)
            out_copy(gl).start(priority=ROW_DMA_PRIORITY)

        @pl.when(first + nb == total)
        def _():
            for k in range(Y_BUFS):
                @pl.when(total - 1 - k >= 0)
                def _():
                    out_copy(total - 1 - k).wait()


def _experts(xs, n_rows_out, first_block, n_blocks_e, counts, w_eg, w_eu, w_ed):
    w_map = lambda e, fb, nb, ct, tot: (e, 0, 0)
    half = D_MODEL // 2
    total = jnp.sum(n_blocks_e).astype(jnp.int32).reshape(1)
    return pl.pallas_call(
        _experts_kernel,
        grid_spec=pltpu.PrefetchScalarGridSpec(
            num_scalar_prefetch=4,
            grid=(N_EXPERTS,),
            in_specs=[pl.BlockSpec(memory_space=pl.ANY),
                      pl.BlockSpec((1, D_MODEL, D_EXPERT), w_map),
                      pl.BlockSpec((1, D_MODEL, D_EXPERT), w_map),
                      pl.BlockSpec((1, D_EXPERT, D_MODEL), w_map)],
            out_specs=pl.BlockSpec(memory_space=pl.ANY),
            scratch_shapes=[pltpu.VMEM((X_BUFS, BM, half), jnp.uint32), pltpu.VMEM((Y_BUFS, BM, half), jnp.uint32),
                            pltpu.VMEM((D_MODEL, D_EXPERT), BF16), pltpu.VMEM((D_MODEL, D_EXPERT), BF16),
                            pltpu.VMEM((D_EXPERT, D_MODEL), BF16),
                            pltpu.SemaphoreType.DMA((X_BUFS,)), pltpu.SemaphoreType.DMA((Y_BUFS,))]),
        out_shape=jax.ShapeDtypeStruct((n_rows_out, half), jnp.uint32),
        compiler_params=pltpu.CompilerParams(dimension_semantics=("arbitrary",), vmem_limit_bytes=VMEM_LIMIT),
        name="experts",
    )(first_block, n_blocks_e, counts, total, xs, w_eg, w_eu, w_ed)


def _combine_kernel(x2_ref, y1_ref, y2_ref, rt_ref, g_ref, o_ref):
    rt = rt_ref[...]
    r = jnp.transpose(jnp.concatenate([rt, jnp.zeros((LANES - rt.shape[0], rt.shape[1]), F32)], axis=0))
    g1, g2 = r[:, 2:3], r[:, 3:4]
    half = D_MODEL // 2
    y1_lo, y1_hi = _unpack_bf16_pairs_f32(y1_ref[...])
    y2_lo, y2_hi = _unpack_bf16_pairs_f32(y2_ref[...])
    x_lo = x2_ref[:, 0:half] + g1 * y1_lo + g2 * y2_lo
    x_hi = x2_ref[:, half:] + g1 * y1_hi + g2 * y2_hi
    ms = (jnp.sum(x_lo * x_lo, axis=-1, keepdims=True) + jnp.sum(x_hi * x_hi, axis=-1, keepdims=True)) / D_MODEL
    inv = lax.rsqrt(ms + EPS)
    o_ref[:, 0:half] = x_lo * inv * g_ref[:, 0:half]
    o_ref[:, half:] = x_hi * inv * g_ref[:, half:]


def _combine(x2, yg, rt, g_final, tm, blk1, blk2):
    n = x2.shape[0]
    return pl.pallas_call(
        _combine_kernel,
        grid=(n // tm,),
        in_specs=[pl.BlockSpec((tm, D_MODEL), lambda i: (i, 0)),
                  pl.BlockSpec((tm, D_MODEL // 2), lambda i: (blk1 + i, 0)),
                  pl.BlockSpec((tm, D_MODEL // 2), lambda i: (blk2 + i, 0)),
                  pl.BlockSpec((SUBLANES, tm), lambda i: (0, i)),
                  pl.BlockSpec((1, D_MODEL), lambda i: (0, 0))],
        out_specs=pl.BlockSpec((tm, D_MODEL), lambda i: (i, 0)),
        out_shape=jax.ShapeDtypeStruct((n, D_MODEL), F32),
        compiler_params=pltpu.CompilerParams(dimension_semantics=("arbitrary",), vmem_limit_bytes=VMEM_LIMIT),
        name="combine",
    )(x2, yg, yg, rt, g_final)


def _gather_rows(table, idx):
    return _sc_gather_rows(table, idx)


def _scatter_rows2(tables, slots_a, slots_b, n_rows_out):
    return _sc_scatter_rows2(tables, slots_a, slots_b, n_rows_out, SCATTER_CHUNK)


def kernel(x_prompt, x_sample, mem_prompt, state_conv, cache_mem_k, cache_mem_v, g_mix, w_in, conv_w, conv_b, ln_conv_g, ln_conv_b, ln_v_g, ln_v_b, w_sg, b_sg, w_out, g_mem, w_mk, w_mv, g_xattn, w_xq, w_xo, g_ffn, w_router_group, b_router_group, w_router_expert, b_router_expert, w_expert_gate, w_expert_up, w_expert_down, g_final):
    assert x_prompt.shape[0] == 1 and g_mix.shape[0] == 1
    n_p = x_prompt.shape[1]
    n_batch, t_len = x_sample.shape[0], x_sample.shape[1]
    n_s = n_batch * t_len
    row = lambda a: a.reshape(1, -1)

    w_router = jnp.concatenate(
        [w_router_group[0], jnp.transpose(w_router_expert[0], (1, 0, 2)).reshape(D_MODEL, N_EXPERTS)], axis=1)
    w_router = jnp.pad(w_router, ((0, 0), (0, LOGIT_LANES - w_router.shape[1]))).astype(BF16)
    b_router = jnp.pad(jnp.concatenate([b_router_group[0], b_router_expert[0].reshape(-1)]),
                       (0, LOGIT_LANES - N_GROUPS - N_EXPERTS)).reshape(1, LOGIT_LANES)
    tril_t = jnp.tril(jnp.ones((t_len, t_len), bool))
    w_sg_t = jnp.where(tril_t, w_sg[0][:, :t_len, :t_len], 0.0)
    eye_b = jnp.eye(n_batch, dtype=F32)
    w_sg_bd = jnp.einsum("ab,hij->haibj", eye_b, w_sg_t).reshape(SG_HEADS, n_s, n_s).astype(BF16)
    p = {
        "g_mix": row(g_mix[0]), "w_in": w_in[0],
        "conv_w": jnp.pad(conv_w[0], ((0, 1), (0, 0))), "conv_b": row(conv_b[0]),
        "ln_conv_g": row(ln_conv_g[0]), "ln_conv_b": row(ln_conv_b[0]),
        "ln_v_g": row(ln_v_g[0]), "ln_v_b": row(ln_v_b[0]),
        "w_sg": w_sg[0],
        "b_sg_rows": jnp.repeat(b_sg[0].T, SG_HEAD_DIM, axis=1),
        "w_sg_bd": w_sg_bd,
        "b_sg_rows_s": jnp.tile(jnp.repeat(b_sg[0][:, :t_len].T, SG_HEAD_DIM, axis=1), (n_batch, 1)),
        "w_out": w_out[0], "g_xattn": row(g_xattn[0]),
        "w_xq": w_xq[0], "w_xo": w_xo[0], "g_ffn": row(g_ffn[0]),
        "w_router": w_router, "b_router": b_router,
        "lower": jnp.tril(jnp.ones((TM, TM), BF16), -1),
    }

    k_p, v_p = _memkv(mem_prompt[0], row(g_mem[0]), w_mk[0], w_mv[0])
    p["k"] = k_p.astype(BF16)
    p["v"] = v_p.astype(BF16)
    p["k_s"] = jnp.transpose(cache_mem_k[0].reshape(n_batch, N_MEM, D_MODEL), (0, 2, 1)).astype(BF16)
    p["v_s"] = cache_mem_v[0].reshape(n_batch, N_MEM, D_MODEL).astype(BF16)

    assert n_p % n_s == 0
    x2_p, h3_p, rt_p, hist_p, cnt_p = _trunk_prompt(x_prompt[0], p)
    x2_s, h3_s, rt_s, hist_s, sgv_s, cnt = _trunk_sample(
        x_sample.reshape(n_s, D_MODEL), state_conv[0], cnt_p, p, n_batch, t_len)

    experts = jnp.arange(N_EXPERTS, dtype=jnp.int32)
    w_e = (w_expert_gate[0], w_expert_up[0], w_expert_down[0])

    def moe_pass(cnt, h3_tables, rts, n_real):
        n_tot = sum(n_real)
        n_slots = -(-(n_tot * 2) // BM) * BM + N_EXPERTS * BM
        counts = cnt[0, :N_EXPERTS].astype(jnp.int32)
        padded = (counts + BM - 1) // BM * BM
        pad_start = jnp.cumsum(padded) - padded

        def one(e_row, rank_row):
            e = e_row.astype(jnp.int32)
            start = jnp.sum(jnp.where(e[None, :] == experts[:, None], pad_start[:, None], 0), axis=0)
            return start + rank_row.astype(jnp.int32)

        slots = [(one(rt[0], rt[4]), one(rt[1], rt[5])) for rt in rts]
        sa, sb, spare0 = [], [], n_slots
        for tab, (a, b), n in zip(h3_tables, slots, n_real):
            n_spare = tab.shape[0] - n
            spare = spare0 + jnp.arange(n_spare, dtype=jnp.int32)
            sa.append(jnp.concatenate([a, spare]))
            sb.append(jnp.concatenate([b, spare + n_spare]))
            spare0 += 2 * n_spare
        xs = _scatter_rows2(tuple(h3_tables), tuple(sa), tuple(sb), spare0)
        ys = _experts(xs, n_slots, pad_start // BM, padded // BM, counts, *w_e)
        back_idx = jnp.concatenate([s for ab in slots for s in ab])
        n_back = -(-back_idx.shape[0] // (SC_WORKERS * GATHER_ROWS)) * (SC_WORKERS * GATHER_ROWS)
        return _gather_rows(ys, jnp.pad(back_idx, (0, n_back - back_idx.shape[0])))

    yg = moe_pass(cnt, [h3_p, h3_s], [rt_p, rt_s], [n_p, n_s])

    gf = row(g_final)
    y_p = _combine(x2_p, yg, rt_p, gf, TM_COMBINE, 0, n_p // TM_COMBINE)
    y_s = _combine(x2_s, yg, rt_s, gf, n_s, 2 * n_p // n_s, 2 * n_p // n_s + 1)

    return (y_p.reshape(1, n_p, D_MODEL),
            y_s.reshape(n_batch, t_len, D_MODEL),
            hist_p[HALO - HIST:].reshape(1, 1, HIST, D_CONV),
            hist_s.reshape(1, n_batch, HIST, D_CONV),
            k_p.reshape(1, 1, N_MEM, X_HEADS, X_HEAD_DIM),
            v_p.reshape(1, 1, N_MEM, X_HEADS, X_HEAD_DIM),
            sgv_s.reshape(1, n_batch, t_len, D_SG))
```

```python
import functools

import jax
import jax.numpy as jnp
from jax import lax
from jax.experimental import pallas as pl
from jax.experimental.pallas import tpu as pltpu
from jax.experimental.pallas import tpu_sc as plsc

D_MODEL = 1024
D_CONV = 512
D_SG = 512
CONV_WIDTH = 31
HIST = CONV_WIDTH - 1
SG_HEADS = 4
SG_HEAD_DIM = 128
SG_CHUNK = 128
N_MEM = 256
X_HEADS = 4
X_HEAD_DIM = 256
N_GROUPS = 4
EXPERTS_PER_GROUP = 8
N_EXPERTS = 32
D_EXPERT = 512
EPS = 1e-6

LANES = 128
SUBLANES = 8
SC_CORES = 2
SC_SUBCORES = 16
SC_WORKERS = SC_CORES * SC_SUBCORES
VMEM_LIMIT = 56 * 1024 * 1024

TM = 512
TM_COMBINE = 1024
ROUTE_ROWS = 2048
HALO = 32
SEG = TM // SUBLANES
SEG_HALO = 32
CONV_BLOCK = 16
CAST_ROWS = 64
BM = 256
X_LOOKAHEAD = 4
X_BUFS = X_LOOKAHEAD + 2
Y_BUFS = 4
ROW_DMA_PRIORITY = 1
GATHER_ROWS = 32
GATHER_BUFS = 6
SCATTER_CHUNK = 32
SCATTER_BUFS = 6
LOGIT_LANES = 128

F32 = jnp.float32
BF16 = jnp.bfloat16


def _dot(a, b):
    return jnp.dot(a, b, preferred_element_type=F32)


def _rms(x, g):
    return x * lax.rsqrt(jnp.mean(x * x, axis=-1, keepdims=True) + EPS) * g


def _ln(x, g, b):
    mu = jnp.mean(x, axis=-1, keepdims=True)
    xc = x - mu
    var = jnp.mean(xc * xc, axis=-1, keepdims=True)
    return xc * lax.rsqrt(var + EPS) * g + b


def _sigmoid(x):
    return 1.0 / (1.0 + jnp.exp(-x))


def _pack_bf16_pairs(h):
    bits = lax.bitcast_convert_type(h, jnp.uint32)
    half = h.shape[1] // 2
    lo = lax.shift_right_logical(bits[:, :half], jnp.uint32(16))
    hi = bits[:, half:] & jnp.uint32(0xFFFF0000)
    return hi | lo


def _unpack_bf16_pairs_f32(p):
    lo = lax.bitcast_convert_type(lax.shift_left(p, jnp.uint32(16)), F32)
    hi = lax.bitcast_convert_type(p & jnp.uint32(0xFFFF0000), F32)
    return lo, hi


def _unpack_bf16_pairs(p):
    lo, hi = _unpack_bf16_pairs_f32(p)
    return lo.astype(BF16), hi.astype(BF16)


def _memkv_kernel(mem_ref, g_ref, wk_ref, wv_ref, k_ref, v_ref):
    m = _rms(mem_ref[...], g_ref[...]).astype(BF16)
    k_ref[...] = _dot(m, wk_ref[...].astype(BF16))
    v_ref[...] = _dot(m, wv_ref[...].astype(BF16))


def _memkv(mem, g_mem, w_mk, w_mv):
    return pl.pallas_call(
        _memkv_kernel,
        out_shape=(jax.ShapeDtypeStruct((N_MEM, D_MODEL), F32), jax.ShapeDtypeStruct((N_MEM, D_MODEL), F32)),
        compiler_params=pltpu.CompilerParams(vmem_limit_bytes=VMEM_LIMIT),
        name="memkv",
    )(mem, g_mem, w_mk, w_mv)


def _attn_heads(q, k, v, k_transposed):
    outs = []
    for h in range(X_HEADS):
        sl = slice(h * X_HEAD_DIM, (h + 1) * X_HEAD_DIM)
        if k_transposed:
            s = _dot(q[:, sl], k[sl, :])
        else:
            s = lax.dot_general(q[:, sl], k[:, sl], (((1,), (1,)), ((), ())), preferred_element_type=F32)
        s = s * (X_HEAD_DIM ** -0.5)
        s = s - jnp.max(s, axis=-1, keepdims=True)
        p = jnp.exp(s)
        p = p / jnp.sum(p, axis=-1, keepdims=True)
        outs.append(_dot(p.astype(BF16), v[:, sl]).astype(BF16))
    return jnp.concatenate(outs, axis=1)


def _route(logits, run, strict_lower):
    m = logits.shape[0]
    r = strict_lower.shape[0]
    lane = lax.broadcasted_iota(jnp.int32, (m, LOGIT_LANES), 1).astype(F32)
    neg = jnp.float32(-jnp.inf)
    big = jnp.float32(LOGIT_LANES)

    def first_argmax(vals):
        mx = jnp.max(vals, axis=-1, keepdims=True)
        idx = jnp.min(jnp.where(vals == mx, lane, big), axis=-1, keepdims=True)
        return mx, idx

    lg = jnp.where(lane < N_GROUPS, logits, neg)
    g_max, g_idx = first_argmax(lg)
    g_w = 1.0 / jnp.sum(jnp.exp(lg - g_max), axis=-1, keepdims=True)

    lo = N_GROUPS + g_idx * EXPERTS_PER_GROUP
    le = jnp.where((lane >= lo) & (lane < lo + EXPERTS_PER_GROUP), logits, neg)
    v1, i1 = first_argmax(le)
    v2, i2 = first_argmax(jnp.where(lane == i1, neg, le))
    t = jnp.exp(v2 - v1)
    gate1 = g_w / (1.0 + t)
    gate2 = g_w * t / (1.0 + t)
    e1 = i1 - N_GROUPS
    e2 = i2 - N_GROUPS

    oh1 = (lane == e1).astype(F32)
    oh2 = (lane == e2).astype(F32)
    oh = oh1 + oh2
    befores = []
    for r0 in range(0, m, r):
        oh_r = oh[r0:r0 + r, :]
        befores.append(_dot(strict_lower, oh_r.astype(BF16)) + run)
        run = run + jnp.sum(oh_r, axis=0, keepdims=True)
    before = befores[0] if len(befores) == 1 else jnp.concatenate(befores, axis=0)
    rank1 = jnp.sum(before * oh1, axis=-1, keepdims=True)
    rank2 = jnp.sum(before * oh2, axis=-1, keepdims=True)
    new_run = run

    rinfo = jnp.where(lane == 0, e1,
            jnp.where(lane == 1, e2,
            jnp.where(lane == 2, gate1,
            jnp.where(lane == 3, gate2,
            jnp.where(lane == 4, rank1,
            jnp.where(lane == 5, rank2, 0.0))))))
    return jnp.transpose(rinfo)[0:SUBLANES, :], new_run


def _conv_segments(a, w_ref, seg_ref, tail_ref, yseg_ref, conv_ref):
    sub = lax.broadcasted_iota(jnp.int32, (SUBLANES, LANES), 0)
    for lt in range(D_CONV // LANES):
        ls = slice(lt * LANES, (lt + 1) * LANES)
        for t0 in range(0, TM, SUBLANES):
            s, m = divmod(t0, SEG)
            seg_ref[lt, pl.ds((SEG_HALO + m) * SUBLANES + s, SUBLANES, stride=SUBLANES), :] = a[t0:t0 + SUBLANES, ls]
        for j in range(SEG_HALO):
            cur = seg_ref[lt, (SEG + j) * SUBLANES:(SEG + j + 1) * SUBLANES, :]
            prev = tail_ref[lt, j * SUBLANES:(j + 1) * SUBLANES, :]
            seg_ref[lt, j * SUBLANES:(j + 1) * SUBLANES, :] = jnp.where(
                sub == 0, pltpu.roll(prev, 1, axis=0), pltpu.roll(cur, 1, axis=0))
            tail_ref[lt, j * SUBLANES:(j + 1) * SUBLANES, :] = cur
        for m0 in range(0, SEG, CONV_BLOCK):
            acc = [jnp.zeros((SUBLANES, LANES), F32) for _ in range(CONV_BLOCK)]
            for idx in range(m0 - HIST, m0 + CONV_BLOCK):
                b = seg_ref[lt, (SEG_HALO + idx) * SUBLANES:(SEG_HALO + idx + 1) * SUBLANES, :]
                for m in range(max(m0, idx), min(m0 + CONV_BLOCK, idx + CONV_WIDTH)):
                    k = idx - m + HIST
                    acc[m - m0] = acc[m - m0] + b * w_ref[k:k + 1, ls]
            for m in range(m0, m0 + CONV_BLOCK):
                yseg_ref[lt, m * SUBLANES:(m + 1) * SUBLANES, :] = acc[m - m0]
        for t0 in range(0, TM, SUBLANES):
            s, m = divmod(t0, SEG)
            conv_ref[t0:t0 + SUBLANES, ls] = yseg_ref[lt, pl.ds(m * SUBLANES + s, SUBLANES, stride=SUBLANES), :]


def _cast_rows(src_ref, dst_ref):
    rows = src_ref.shape[0]

    def body(c, carry):
        r0 = pl.multiple_of(c * CAST_ROWS, CAST_ROWS)
        dst_ref[pl.ds(r0, CAST_ROWS), :] = src_ref[pl.ds(r0, CAST_ROWS), :].astype(BF16)
        return carry

    lax.fori_loop(0, rows // CAST_ROWS, body, 0)


def _trunk_prompt_kernel(x_ref, gmix_ref, win32_ref, convw_ref, convb_ref, lncg_ref, lncb_ref, lnvg_ref, lnvb_ref,
                         wsg_ref, bsg_ref, wout32_ref, gx_ref, wxq32_ref, kmem_ref, v_ref, wxo32_ref, gffn_ref, wr_ref,
                         br_ref,
                         x2_ref, h3_ref, logit_ref, hist_ref,
                         seg_ref, tail_ref, yseg_ref, conv_ref, win_ref, wout_ref, wxq_ref, wxo_ref):
    i = pl.program_id(0)

    @pl.when(i == 0)
    def _():
        tail_ref[...] = jnp.zeros(tail_ref.shape, F32)
        _cast_rows(win32_ref, win_ref)
        _cast_rows(wout32_ref, wout_ref)
        _cast_rows(wxq32_ref, wxq_ref)
        _cast_rows(wxo32_ref, wxo_ref)

    x = x_ref[...]
    h = _rms(x, gmix_ref[...]).astype(BF16)

    a_in = _dot(h, win_ref[:, 0:D_CONV])
    a_gate = _dot(h, win_ref[:, D_CONV:2 * D_CONV])
    a = a_in * _sigmoid(a_gate)
    hist_ref[...] = a[TM - HALO:, :]
    _conv_segments(a, convw_ref, seg_ref, tail_ref, yseg_ref, conv_ref)

    y = _ln(conv_ref[...] + convb_ref[...], lncg_ref[...], lncb_ref[...])
    a_out = (y * _sigmoid(y)).astype(BF16)

    u = _dot(h, win_ref[:, 2 * D_CONV:2 * D_CONV + D_SG])
    v = _ln(_dot(h, win_ref[:, 2 * D_CONV + D_SG:]), lnvg_ref[...], lnvb_ref[...]).astype(BF16)
    ri = lax.broadcasted_iota(jnp.int32, (SG_CHUNK, SG_CHUNK), 0)
    ci = lax.broadcasted_iota(jnp.int32, (SG_CHUNK, SG_CHUNK), 1)
    w_tril = [jnp.where(ci <= ri, wsg_ref[hh], 0.0).astype(BF16) for hh in range(SG_HEADS)]
    gate_rows = []
    for c in range(TM // SG_CHUNK):
        rs = slice(c * SG_CHUNK, (c + 1) * SG_CHUNK)
        heads = [_dot(w_tril[hh], v[rs, hh * SG_HEAD_DIM:(hh + 1) * SG_HEAD_DIM]) for hh in range(SG_HEADS)]
        gate_rows.append(jnp.concatenate(heads, axis=1) + bsg_ref[...])
    b_out = (u * jnp.concatenate(gate_rows, axis=0)).astype(BF16)

    x1 = x + _dot(a_out, wout_ref[0:D_CONV, :]) + _dot(b_out, wout_ref[D_CONV:, :])

    hx = _rms(x1, gx_ref[...]).astype(BF16)
    q = _dot(hx, wxq_ref[...]).astype(BF16)
    x2 = x1 + _dot(_attn_heads(q, kmem_ref[...], v_ref[...], False), wxo_ref[...])
    x2_ref[...] = x2

    h3 = _rms(x2, gffn_ref[...]).astype(BF16)
    h3_ref[...] = _pack_bf16_pairs(h3.astype(F32))
    logit_ref[...] = _dot(h3, wr_ref[...]) + br_ref[...]


def _router_kernel(logit_ref, lower_ref, rt_ref, cnt_ref, run_ref):
    @pl.when(pl.program_id(0) == 0)
    def _():
        run_ref[...] = jnp.zeros((1, LOGIT_LANES), F32)

    rt, new_run = _route(logit_ref[...], run_ref[...], lower_ref[...])
    rt_ref[...] = rt
    run_ref[...] = new_run
    cnt_ref[...] = new_run


def _router(logits, lower):
    n = logits.shape[0]
    assert n % ROUTE_ROWS == 0
    return pl.pallas_call(
        _router_kernel,
        grid=(n // ROUTE_ROWS,),
        in_specs=[pl.BlockSpec((ROUTE_ROWS, LOGIT_LANES), lambda i: (i, 0)),
                  pl.BlockSpec(lower.shape, lambda i: (0, 0))],
        out_specs=(pl.BlockSpec((SUBLANES, ROUTE_ROWS), lambda i: (0, i)),
                   pl.BlockSpec((1, LOGIT_LANES), lambda i: (0, 0))),
        out_shape=(jax.ShapeDtypeStruct((SUBLANES, n), F32), jax.ShapeDtypeStruct((1, LOGIT_LANES), F32)),
        scratch_shapes=[pltpu.VMEM((1, LOGIT_LANES), F32)],
        compiler_params=pltpu.CompilerParams(dimension_semantics=("arbitrary",), vmem_limit_bytes=VMEM_LIMIT),
        name="router",
    )(logits, lower)


def _const_spec(shape):
    nd = len(shape)
    return pl.BlockSpec(shape, lambda i: (0,) * nd, pipeline_mode=pl.Buffered(1))


def _trunk_prompt(x, p):
    n = x.shape[0]
    assert n % TM == 0
    row = lambda w: pl.BlockSpec((TM, w), lambda i: (i, 0))
    consts = [p["g_mix"], p["w_in"], p["conv_w"], p["conv_b"], p["ln_conv_g"], p["ln_conv_b"], p["ln_v_g"],
              p["ln_v_b"], p["w_sg"], p["b_sg_rows"], p["w_out"], p["g_xattn"], p["w_xq"], p["k"], p["v"],
              p["w_xo"], p["g_ffn"], p["w_router"], p["b_router"]]
    return pl.pallas_call(
        _trunk_prompt_kernel,
        grid=(n // TM,),
        in_specs=[row(D_MODEL)] + [_const_spec(c.shape) for c in consts],
        out_specs=(row(D_MODEL), row(D_MODEL // 2), row(LOGIT_LANES),
                   pl.BlockSpec((HALO, D_CONV), lambda i: (0, 0))),
        out_shape=(jax.ShapeDtypeStruct((n, D_MODEL), F32),
                   jax.ShapeDtypeStruct((n, D_MODEL // 2), jnp.uint32),
                   jax.ShapeDtypeStruct((n, LOGIT_LANES), F32),
                   jax.ShapeDtypeStruct((HALO, D_CONV), F32)),
        scratch_shapes=[pltpu.VMEM((D_CONV // LANES, (SEG_HALO + SEG) * SUBLANES, LANES), F32),
                        pltpu.VMEM((D_CONV // LANES, SEG_HALO * SUBLANES, LANES), F32),
                        pltpu.VMEM((D_CONV // LANES, TM, LANES), F32),
                        pltpu.VMEM((TM, D_CONV), F32),
                        pltpu.VMEM(p["w_in"].shape, BF16), pltpu.VMEM(p["w_out"].shape, BF16),
                        pltpu.VMEM(p["w_xq"].shape, BF16), pltpu.VMEM(p["w_xo"].shape, BF16)],
        compiler_params=pltpu.CompilerParams(dimension_semantics=("arbitrary",), vmem_limit_bytes=VMEM_LIMIT),
        name="trunk_prompt",
    )(x, *consts)


def _trunk_sample_kernel(n_batch, t_len,
                         x_ref, hist_in_ref, run_in_ref, gmix_ref, win_ref, convw_ref, convb_ref, lncg_ref, lncb_ref,
                         lnvg_ref, lnvb_ref, wsgbd_ref, bsg_ref, wout_ref, gx_ref, wxq_ref, kmem_ref, v_ref, wxo_ref,
                         gffn_ref, wr_ref, br_ref, lower_ref,
                         x2_ref, h3_ref, rt_ref, hist_ref, sgv_ref, cnt_ref,
                         ext_ref, conv_ref, att_ref):
    x = x_ref[...]
    h = _rms(x, gmix_ref[...]).astype(BF16)
    z = _dot(h, win_ref[...].astype(BF16))
    a = z[:, 0:D_CONV] * _sigmoid(z[:, D_CONV:2 * D_CONV])
    ext_len = HIST + t_len
    for b in range(n_batch):
        ext_ref[b, 0:HIST, :] = hist_in_ref[b]
        ext_ref[b, HIST:ext_len, :] = a[b * t_len:(b + 1) * t_len, :]
    for b in range(n_batch):
        acc = jnp.zeros((t_len, D_CONV), F32)
        for k in range(CONV_WIDTH):
            acc = acc + ext_ref[b, k:k + t_len, :] * convw_ref[k:k + 1, :]
        conv_ref[b * t_len:(b + 1) * t_len, :] = acc
        hist_ref[b] = ext_ref[b, ext_len - HIST:ext_len, :]

    y = _ln(conv_ref[...] + convb_ref[...], lncg_ref[...], lncb_ref[...])
    a_out = (y * _sigmoid(y)).astype(BF16)

    u = z[:, 2 * D_CONV:2 * D_CONV + D_SG]
    v = _ln(z[:, 2 * D_CONV + D_SG:], lnvg_ref[...], lnvb_ref[...])
    sgv_ref[...] = v
    vb = v.astype(BF16)
    heads = [_dot(wsgbd_ref[hh], vb[:, hh * SG_HEAD_DIM:(hh + 1) * SG_HEAD_DIM]) for hh in range(SG_HEADS)]
    b_out = (u * (jnp.concatenate(heads, axis=1) + bsg_ref[...])).astype(BF16)

    x1 = (x + _dot(a_out, wout_ref[0:D_CONV, :].astype(BF16))
          + _dot(b_out, wout_ref[D_CONV:, :].astype(BF16)))

    hx = _rms(x1, gx_ref[...]).astype(BF16)
    q = _dot(hx, wxq_ref[...].astype(BF16)).astype(BF16)
    for b in range(n_batch):
        rs = slice(b * t_len, (b + 1) * t_len)
        att_ref[rs, :] = _attn_heads(q[rs, :], kmem_ref[b], v_ref[b], True)
    x2 = x1 + _dot(att_ref[...], wxo_ref[...].astype(BF16))
    x2_ref[...] = x2

    h3 = _rms(x2, gffn_ref[...]).astype(BF16)
    m = n_batch * t_len
    h3_ref[0:m, :] = _pack_bf16_pairs(h3.astype(F32))
    if h3_ref.shape[0] > m:
        h3_ref[m:, :] = jnp.zeros((h3_ref.shape[0] - m, D_MODEL // 2), jnp.uint32)
    rt, new_run = _route(_dot(h3, wr_ref[...]) + br_ref[...], run_in_ref[...], lower_ref[...])
    rt_ref[...] = rt
    cnt_ref[...] = new_run


def _trunk_sample(x, hist, run, p, n_batch, t_len):
    m = n_batch * t_len
    args = [x, hist, run, p["g_mix"], p["w_in"], p["conv_w"], p["conv_b"], p["ln_conv_g"], p["ln_conv_b"],
            p["ln_v_g"], p["ln_v_b"], p["w_sg_bd"], p["b_sg_rows_s"], p["w_out"], p["g_xattn"], p["w_xq"],
            p["k_s"], p["v_s"], p["w_xo"], p["g_ffn"], p["w_router"], p["b_router"], p["lower"][:m, :m]]
    return pl.pallas_call(
        functools.partial(_trunk_sample_kernel, n_batch, t_len),
        out_shape=(jax.ShapeDtypeStruct((m, D_MODEL), F32),
                   jax.ShapeDtypeStruct((-(-m // (SC_WORKERS * SUBLANES)) * SC_WORKERS * SUBLANES, D_MODEL // 2),
                                        jnp.uint32),
                   jax.ShapeDtypeStruct((SUBLANES, m), F32),
                   jax.ShapeDtypeStruct((n_batch, HIST, D_CONV), F32),
                   jax.ShapeDtypeStruct((m, D_SG), F32),
                   jax.ShapeDtypeStruct((1, LOGIT_LANES), F32)),
        scratch_shapes=[pltpu.VMEM((n_batch, HIST + t_len, D_CONV), F32),
                        pltpu.VMEM((m, D_CONV), F32),
                        pltpu.VMEM((m, D_MODEL), BF16)],
        compiler_params=pltpu.CompilerParams(vmem_limit_bytes=VMEM_LIMIT),
        name="trunk_sample",
    )(*args)


def _sc_worker_id():
    return lax.axis_index("s") * SC_CORES + lax.axis_index("c")


def _sc_chunk(per_w, max_chunk):
    assert per_w % SUBLANES == 0 and max_chunk <= LANES
    return max(c for c in range(SUBLANES, max_chunk + 1, SUBLANES) if per_w % c == 0)


def _sc_gather_rows(table, idx):
    n_rows, d = idx.shape[0], table.shape[1]
    per_w = n_rows // SC_WORKERS
    assert per_w * SC_WORKERS == n_rows and per_w % GATHER_ROWS == 0
    n_chunks = per_w // GATHER_ROWS
    lag = GATHER_BUFS // 2
    mesh = plsc.VectorSubcoreMesh(core_axis_name="c", subcore_axis_name="s")

    @functools.partial(
        pl.kernel, mesh=mesh,
        out_type=jax.ShapeDtypeStruct((n_rows, d), table.dtype),
        scratch_types=([pltpu.VMEM((per_w,), jnp.int32)]
                       + [pltpu.VMEM((GATHER_ROWS, d), table.dtype)] * GATHER_BUFS
                       + [pltpu.SemaphoreType.DMA] * (2 * GATHER_BUFS)),
    )
    def gather(table_hbm, idx_hbm, out_hbm, idx_all, *rest):
        rows = rest[:GATHER_BUFS]
        gsem = rest[GATHER_BUFS:2 * GATHER_BUFS]
        wsem = rest[2 * GATHER_BUFS:]
        base = _sc_worker_id() * per_w
        pltpu.sync_copy(idx_hbm.at[pl.ds(base, per_w)], idx_all)

        reads, writes = {}, {}
        for c in range(n_chunks + lag):
            if c < n_chunks:
                b = c % GATHER_BUFS
                if c >= GATHER_BUFS:
                    writes.pop(c - GATHER_BUFS).wait()
                reads[c] = pltpu.async_copy(
                    table_hbm.at[idx_all.at[pl.ds(c * GATHER_ROWS, GATHER_ROWS)]], rows[b], gsem[b])
            w = c - lag
            if w >= 0:
                b = w % GATHER_BUFS
                reads.pop(w).wait()
                writes[w] = pltpu.async_copy(
                    rows[b], out_hbm.at[pl.ds(base + w * GATHER_ROWS, GATHER_ROWS)], wsem[b])
        for w in sorted(writes):
            writes[w].wait()

    return gather(table, idx)


def _sc_scatter_rows2(tables, slots_a, slots_b, n_rows_out, max_chunk):
    d, dtype = tables[0].shape[1], tables[0].dtype
    plans = []
    for t in tables:
        per_w = t.shape[0] // SC_WORKERS
        assert per_w * SC_WORKERS == t.shape[0]
        chunk = _sc_chunk(per_w, max_chunk)
        plans.append((per_w, chunk, per_w // chunk))
    cmax = max(c for _, c, _ in plans)
    n_t = len(tables)
    mesh = plsc.VectorSubcoreMesh(core_axis_name="c", subcore_axis_name="s")

    nb = SCATTER_BUFS
    lag = nb // 3
    scratch = []
    for _, chunk, _ in plans:
        for _ in range(nb):
            scratch += [pltpu.VMEM((chunk,), jnp.int32), pltpu.VMEM((chunk,), jnp.int32)]
    scratch += [pltpu.VMEM((cmax, d), dtype)] * nb
    scratch += [pltpu.SemaphoreType.DMA] * (2 * nb)

    @functools.partial(pl.kernel, mesh=mesh, out_type=jax.ShapeDtypeStruct((n_rows_out, d), dtype),
                       scratch_types=scratch)
    def scatter(*refs):
        tab_hbm = refs[0:n_t]
        sa_hbm = refs[n_t:2 * n_t]
        sb_hbm = refs[2 * n_t:3 * n_t]
        out_hbm = refs[3 * n_t]
        sc = refs[3 * n_t + 1:]
        idx_refs = sc[:2 * nb * n_t]
        rows = sc[2 * nb * n_t:2 * nb * n_t + nb]
        lsem = sc[2 * nb * n_t + nb:2 * nb * n_t + 2 * nb]
        ssem = sc[2 * nb * n_t + 2 * nb:]
        wid = _sc_worker_id()

        work = []
        for t, (per_w, chunk, n_chunks) in enumerate(plans):
            for j in range(n_chunks):
                work.append((t, wid * per_w + j * chunk, chunk))

        def parts(k):
            t, off, chunk = work[k]
            b = k % nb
            ia, ib = idx_refs[2 * nb * t + 2 * b], idx_refs[2 * nb * t + 2 * b + 1]
            rv = rows[b] if chunk == cmax else rows[b].at[pl.ds(0, chunk)]
            return t, off, chunk, b, ia, ib, rv

        def start_load(k):
            t, off, chunk, b, ia, ib, rv = parts(k)
            return (pltpu.async_copy(tab_hbm[t].at[pl.ds(off, chunk)], rv, lsem[b]),
                    pltpu.async_copy(sa_hbm[t].at[pl.ds(off, chunk)], ia, lsem[b]),
                    pltpu.async_copy(sb_hbm[t].at[pl.ds(off, chunk)], ib, lsem[b]))

        def start_scatter(k):
            t, off, chunk, b, ia, ib, rv = parts(k)
            return (pltpu.async_copy(rv, out_hbm.at[ia], ssem[b]), pltpu.async_copy(rv, out_hbm.at[ib], ssem[b]))

        loads, scatters = {}, {}
        for k in range(len(work) + lag):
            if k < len(work):
                if k >= nb:
                    for c in scatters.pop(k - nb):
                        c.wait()
                loads[k] = start_load(k)
            w = k - lag
            if w >= 0:
                for c in loads.pop(w):
                    c.wait()
                scatters[w] = start_scatter(w)
        for w in sorted(scatters):
            for c in scatters[w]:
                c.wait()

    return scatter(*tables, *slots_a, *slots_b)


def _experts_kernel(first_ref, nblk_ref, cnt_ref, tot_ref, xs_hbm, wg_ref, wu_ref, wd_ref, ys_hbm,
                    xbuf, ybuf, wg_bf, wu_bf, wd_bf, in_sem, out_sem):
    e = pl.program_id(0)
    nb = nblk_ref[e]
    first = first_ref[e]
    cnt = cnt_ref[e]
    total = tot_ref[0]
    half = D_MODEL // 2

    def in_copy(gb):
        slot = lax.rem(gb, X_BUFS)
        return pltpu.make_async_copy(xs_hbm.at[pl.ds(gb * BM, BM)], xbuf.at[slot], in_sem.at[slot])

    def out_copy(gb):
        slot = lax.rem(gb, Y_BUFS)
        return pltpu.make_async_copy(ybuf.at[slot], ys_hbm.at[pl.ds(gb * BM, BM)], out_sem.at[slot])

    @pl.when(nb > 0)
    def _():
        @pl.when(first == 0)
        def _():
            for k in range(X_LOOKAHEAD):
                @pl.when(k < total)
                def _():
                    in_copy(k).start(priority=ROW_DMA_PRIORITY)

        wg_bf[...] = wg_ref[0].astype(BF16)
        wu_bf[...] = wu_ref[0].astype(BF16)
        wd_bf[...] = wd_ref[0].astype(BF16)

        def acquire(gb):
            @pl.when(gb + X_LOOKAHEAD < total)
            def _():
                in_copy(gb + X_LOOKAHEAD).start(priority=ROW_DMA_PRIORITY)

            in_copy(gb).wait()

            @pl.when(gb >= Y_BUFS)
            def _():
                out_copy(gb - Y_BUFS).wait()

        def ffn(gb, j):
            live = lax.broadcasted_iota(jnp.int32, (BM, half), 0) < cnt - j * BM
            lo, hi = _unpack_bf16_pairs(jnp.where(live, xbuf[lax.rem(gb, X_BUFS)], jnp.uint32(0)))
            g = _dot(lo, wg_bf[0:half, :]) + _dot(hi, wg_bf[half:, :])
            u = _dot(lo, wu_bf[0:half, :]) + _dot(hi, wu_bf[half:, :])
            hm = (g * _sigmoid(g) * u).astype(BF16)
            y = _dot(hm, wd_bf[...])
            ybuf[lax.rem(gb, Y_BUFS)] = _pack_bf16_pairs(y.astype(BF16).astype(F32))

        def block_pair(jp, carry):
            j0 = 2 * jp
            g0 = first + j0
            acquire(g0)
            acquire(g0 + 1)
            ffn(g0, j0)
            ffn(g0 + 1, j0 + 1)
            out_copy(g0).start(priority=ROW_DMA_PRIORITY)
            out_copy(g0 + 1).start(priority=ROW_DMA_PRIORITY)
            return carry

        lax.fori_loop(0, nb // 2, block_pair, 0)

        @pl.when(lax.rem(nb, 2) == 1)
        def _():
            gl = first + nb - 1
            acquire(gl)
            ffn(gl, nb - 1)
            out_copy(gl).start(priority=ROW_DMA_PRIORITY)

        @pl.when(first + nb == total)
        def _():
            for k in range(Y_BUFS):
                @pl.when(total - 1 - k >= 0)
                def _():
                    out_copy(total - 1 - k).wait()


def _experts(xs, n_rows_out, first_block, n_blocks_e, counts, w_eg, w_eu, w_ed):
    w_map = lambda e, fb, nb, ct, tot: (e, 0, 0)
    half = D_MODEL // 2
    total = jnp.sum(n_blocks_e).astype(jnp.int32).reshape(1)
    return pl.pallas_call(
        _experts_kernel,
        grid_spec=pltpu.PrefetchScalarGridSpec(
            num_scalar_prefetch=4,
            grid=(N_EXPERTS,),
            in_specs=[pl.BlockSpec(memory_space=pl.ANY),
                      pl.BlockSpec((1, D_MODEL, D_EXPERT), w_map),
                      pl.BlockSpec((1, D_MODEL, D_EXPERT), w_map),
                      pl.BlockSpec((1, D_EXPERT, D_MODEL), w_map)],
            out_specs=pl.BlockSpec(memory_space=pl.ANY),
            scratch_shapes=[pltpu.VMEM((X_BUFS, BM, half), jnp.uint32), pltpu.VMEM((Y_BUFS, BM, half), jnp.uint32),
                            pltpu.VMEM((D_MODEL, D_EXPERT), BF16), pltpu.VMEM((D_MODEL, D_EXPERT), BF16),
                            pltpu.VMEM((D_EXPERT, D_MODEL), BF16),
                            pltpu.SemaphoreType.DMA((X_BUFS,)), pltpu.SemaphoreType.DMA((Y_BUFS,))]),
        out_shape=jax.ShapeDtypeStruct((n_rows_out, half), jnp.uint32),
        compiler_params=pltpu.CompilerParams(dimension_semantics=("arbitrary",), vmem_limit_bytes=VMEM_LIMIT),
        name="experts",
    )(first_block, n_blocks_e, counts, total, xs, w_eg, w_eu, w_ed)


def _combine_kernel(x2_ref, y1_ref, y2_ref, rt_ref, g_ref, o_ref):
    rt = rt_ref[...]
    r = jnp.transpose(jnp.concatenate([rt, jnp.zeros((LANES - rt.shape[0], rt.shape[1]), F32)], axis=0))
    g1, g2 = r[:, 2:3], r[:, 3:4]
    half = D_MODEL // 2
    y1_lo, y1_hi = _unpack_bf16_pairs_f32(y1_ref[...])
    y2_lo, y2_hi = _unpack_bf16_pairs_f32(y2_ref[...])
    x_lo = x2_ref[:, 0:half] + g1 * y1_lo + g2 * y2_lo
    x_hi = x2_ref[:, half:] + g1 * y1_hi + g2 * y2_hi
    ms = (jnp.sum(x_lo * x_lo, axis=-1, keepdims=True) + jnp.sum(x_hi * x_hi, axis=-1, keepdims=True)) / D_MODEL
    inv = lax.rsqrt(ms + EPS)
    o_ref[:, 0:half] = x_lo * inv * g_ref[:, 0:half]
    o_ref[:, half:] = x_hi * inv * g_ref[:, half:]


def _combine(x2, yg, rt, g_final, tm, blk1, blk2):
    n = x2.shape[0]
    return pl.pallas_call(
        _combine_kernel,
        grid=(n // tm,),
        in_specs=[pl.BlockSpec((tm, D_MODEL), lambda i: (i, 0)),
                  pl.BlockSpec((tm, D_MODEL // 2), lambda i: (blk1 + i, 0)),
                  pl.BlockSpec((tm, D_MODEL // 2), lambda i: (blk2 + i, 0)),
                  pl.BlockSpec((SUBLANES, tm), lambda i: (0, i)),
                  pl.BlockSpec((1, D_MODEL), lambda i: (0, 0))],
        out_specs=pl.BlockSpec((tm, D_MODEL), lambda i: (i, 0)),
        out_shape=jax.ShapeDtypeStruct((n, D_MODEL), F32),
        compiler_params=pltpu.CompilerParams(dimension_semantics=("arbitrary",), vmem_limit_bytes=VMEM_LIMIT),
        name="combine",
    )(x2, yg, yg, rt, g_final)


def _gather_rows(table, idx):
    return _sc_gather_rows(table, idx)


def _scatter_rows2(tables, slots_a, slots_b, n_rows_out):
    return _sc_scatter_rows2(tables, slots_a, slots_b, n_rows_out, SCATTER_CHUNK)


def kernel(x_prompt, x_sample, mem_prompt, state_conv, cache_mem_k, cache_mem_v, g_mix, w_in, conv_w, conv_b, ln_conv_g, ln_conv_b, ln_v_g, ln_v_b, w_sg, b_sg, w_out, g_mem, w_mk, w_mv, g_xattn, w_xq, w_xo, g_ffn, w_router_group, b_router_group, w_router_expert, b_router_expert, w_expert_gate, w_expert_up, w_expert_down, g_final):
    assert x_prompt.shape[0] == 1 and g_mix.shape[0] == 1
    n_p = x_prompt.shape[1]
    n_batch, t_len = x_sample.shape[0], x_sample.shape[1]
    n_s = n_batch * t_len
    row = lambda a: a.reshape(1, -1)

    w_router = jnp.concatenate(
        [w_router_group[0], jnp.transpose(w_router_expert[0], (1, 0, 2)).reshape(D_MODEL, N_EXPERTS)], axis=1)
    w_router = jnp.pad(w_router, ((0, 0), (0, LOGIT_LANES - w_router.shape[1]))).astype(BF16)
    b_router = jnp.pad(jnp.concatenate([b_router_group[0], b_router_expert[0].reshape(-1)]),
                       (0, LOGIT_LANES - N_GROUPS - N_EXPERTS)).reshape(1, LOGIT_LANES)
    tril_t = jnp.tril(jnp.ones((t_len, t_len), bool))
    w_sg_t = jnp.where(tril_t, w_sg[0][:, :t_len, :t_len], 0.0)
    eye_b = jnp.eye(n_batch, dtype=F32)
    w_sg_bd = jnp.einsum("ab,hij->haibj", eye_b, w_sg_t).reshape(SG_HEADS, n_s, n_s).astype(BF16)
    p = {
        "g_mix": row(g_mix[0]), "w_in": w_in[0],
        "conv_w": jnp.pad(conv_w[0], ((0, 1), (0, 0))), "conv_b": row(conv_b[0]),
        "ln_conv_g": row(ln_conv_g[0]), "ln_conv_b": row(ln_conv_b[0]),
        "ln_v_g": row(ln_v_g[0]), "ln_v_b": row(ln_v_b[0]),
        "w_sg": w_sg[0],
        "b_sg_rows": jnp.repeat(b_sg[0].T, SG_HEAD_DIM, axis=1),
        "w_sg_bd": w_sg_bd,
        "b_sg_rows_s": jnp.tile(jnp.repeat(b_sg[0][:, :t_len].T, SG_HEAD_DIM, axis=1), (n_batch, 1)),
        "w_out": w_out[0], "g_xattn": row(g_xattn[0]),
        "w_xq": w_xq[0], "w_xo": w_xo[0], "g_ffn": row(g_ffn[0]),
        "w_router": w_router, "b_router": b_router,
        "lower": jnp.tril(jnp.ones((TM, TM), BF16), -1),
    }

    k_p, v_p = _memkv(mem_prompt[0], row(g_mem[0]), w_mk[0], w_mv[0])
    p["k"] = k_p.astype(BF16)
    p["v"] = v_p.astype(BF16)
    p["k_s"] = jnp.transpose(cache_mem_k[0].reshape(n_batch, N_MEM, D_MODEL), (0, 2, 1)).astype(BF16)
    p["v_s"] = cache_mem_v[0].reshape(n_batch, N_MEM, D_MODEL).astype(BF16)

    assert n_p % n_s == 0
    x2_p, h3_p, logits_p, hist_p = _trunk_prompt(x_prompt[0], p)
    rt_p, cnt_p = _router(logits_p, p["lower"])
    x2_s, h3_s, rt_s, hist_s, sgv_s, cnt = _trunk_sample(
        x_sample.reshape(n_s, D_MODEL), state_conv[0], cnt_p, p, n_batch, t_len)

    experts = jnp.arange(N_EXPERTS, dtype=jnp.int32)
    w_e = (w_expert_gate[0], w_expert_up[0], w_expert_down[0])

    def moe_pass(cnt, h3_tables, rts, n_real):
        n_tot = sum(n_real)
        n_slots = -(-(n_tot * 2) // BM) * BM + N_EXPERTS * BM
        counts = cnt[0, :N_EXPERTS].astype(jnp.int32)
        padded = (counts + BM - 1) // BM * BM
        pad_start = jnp.cumsum(padded) - padded

        def one(e_row, rank_row):
            e = e_row.astype(jnp.int32)
            start = jnp.sum(jnp.where(e[None, :] == experts[:, None], pad_start[:, None], 0), axis=0)
            return start + rank_row.astype(jnp.int32)

        slots = [(one(rt[0], rt[4]), one(rt[1], rt[5])) for rt in rts]
        sa, sb, spare0 = [], [], n_slots
        for tab, (a, b), n in zip(h3_tables, slots, n_real):
            n_spare = tab.shape[0] - n
            spare = spare0 + jnp.arange(n_spare, dtype=jnp.int32)
            sa.append(jnp.concatenate([a, spare]))
            sb.append(jnp.concatenate([b, spare + n_spare]))
            spare0 += 2 * n_spare
        xs = _scatter_rows2(tuple(h3_tables), tuple(sa), tuple(sb), spare0)
        ys = _experts(xs, n_slots, pad_start // BM, padded // BM, counts, *w_e)
        back_idx = jnp.concatenate([s for ab in slots for s in ab])
        n_back = -(-back_idx.shape[0] // (SC_WORKERS * GATHER_ROWS)) * (SC_WORKERS * GATHER_ROWS)
        return _gather_rows(ys, jnp.pad(back_idx, (0, n_back - back_idx.shape[0])))

    yg = moe_pass(cnt, [h3_p, h3_s], [rt_p, rt_s], [n_p, n_s])

    gf = row(g_final)
    y_p = _combine(x2_p, yg, rt_p, gf, TM_COMBINE, 0, n_p // TM_COMBINE)
    y_s = _combine(x2_s, yg, rt_s, gf, n_s, 2 * n_p // n_s, 2 * n_p // n_s + 1)

    return (y_p.reshape(1, n_p, D_MODEL),
            y_s.reshape(n_batch, t_len, D_MODEL),
            hist_p[HALO - HIST:].reshape(1, 1, HIST, D_CONV),
            hist_s.reshape(1, n_batch, HIST, D_CONV),
            k_p.reshape(1, 1, N_MEM, X_HEADS, X_HEAD_DIM),
            v_p.reshape(1, 1, N_MEM, X_HEADS, X_HEAD_DIM),
            sgv_s.reshape(1, n_batch, t_len, D_SG))
```

```python
import functools

import jax
import jax.numpy as jnp
from jax import lax
from jax.experimental import pallas as pl
from jax.experimental.pallas import tpu as pltpu
from jax.experimental.pallas import tpu_sc as plsc

D_MODEL = 1024
D_CONV = 512
D_SG = 512
CONV_WIDTH = 31
HIST = CONV_WIDTH - 1
SG_HEADS = 4
SG_HEAD_DIM = 128
SG_CHUNK = 128
N_MEM = 256
X_HEADS = 4
X_HEAD_DIM = 256
N_GROUPS = 4
EXPERTS_PER_GROUP = 8
N_EXPERTS = 32
D_EXPERT = 512
EPS = 1e-6

LANES = 128
SUBLANES = 8
SC_CORES = 2
SC_SUBCORES = 16
SC_WORKERS = SC_CORES * SC_SUBCORES
VMEM_LIMIT = 56 * 1024 * 1024

TM = 512
TM_COMBINE = 1024
ROUTE_ROWS = 2048
HALO = 32
SEG = TM // SUBLANES
SEG_HALO = 32
CONV_BLOCK = 16
CAST_ROWS = 64
BM = 256
X_LOOKAHEAD = 4
X_BUFS = X_LOOKAHEAD + 2
Y_BUFS = 4
ROW_DMA_PRIORITY = 1
GATHER_ROWS = 32
GATHER_BUFS = 6
SCATTER_CHUNK = 32
SCATTER_BUFS = 6
LOGIT_LANES = 128

F32 = jnp.float32
BF16 = jnp.bfloat16


def _dot(a, b):
    return jnp.dot(a, b, preferred_element_type=F32)


def _rms(x, g):
    return x * lax.rsqrt(jnp.mean(x * x, axis=-1, keepdims=True) + EPS) * g


def _ln(x, g, b):
    mu = jnp.mean(x, axis=-1, keepdims=True)
    xc = x - mu
    var = jnp.mean(xc * xc, axis=-1, keepdims=True)
    return xc * lax.rsqrt(var + EPS) * g + b


def _sigmoid(x):
    return 1.0 / (1.0 + jnp.exp(-x))


def _pack_bf16_pairs(h):
    bits = lax.bitcast_convert_type(h, jnp.uint32)
    half = h.shape[1] // 2
    lo = lax.shift_right_logical(bits[:, :half], jnp.uint32(16))
    hi = bits[:, half:] & jnp.uint32(0xFFFF0000)
    return hi | lo


def _unpack_bf16_pairs_f32(p):
    lo = lax.bitcast_convert_type(lax.shift_left(p, jnp.uint32(16)), F32)
    hi = lax.bitcast_convert_type(p & jnp.uint32(0xFFFF0000), F32)
    return lo, hi


def _unpack_bf16_pairs(p):
    lo, hi = _unpack_bf16_pairs_f32(p)
    return lo.astype(BF16), hi.astype(BF16)


def _memkv_kernel(mem_ref, g_ref, wk_ref, wv_ref, k_ref, v_ref):
    m = _rms(mem_ref[...], g_ref[...]).astype(BF16)
    k_ref[...] = _dot(m, wk_ref[...].astype(BF16))
    v_ref[...] = _dot(m, wv_ref[...].astype(BF16))


def _memkv(mem, g_mem, w_mk, w_mv):
    return pl.pallas_call(
        _memkv_kernel,
        out_shape=(jax.ShapeDtypeStruct((N_MEM, D_MODEL), F32), jax.ShapeDtypeStruct((N_MEM, D_MODEL), F32)),
        compiler_params=pltpu.CompilerParams(vmem_limit_bytes=VMEM_LIMIT),
        name="memkv",
    )(mem, g_mem, w_mk, w_mv)


def _attn_heads(q, k, v, k_transposed):
    outs = []
    for h in range(X_HEADS):
        sl = slice(h * X_HEAD_DIM, (h + 1) * X_HEAD_DIM)
        if k_transposed:
            s = _dot(q[:, sl], k[sl, :])
        else:
            s = lax.dot_general(q[:, sl], k[:, sl], (((1,), (1,)), ((), ())), preferred_element_type=F32)
        s = s * (X_HEAD_DIM ** -0.5)
        s = s - jnp.max(s, axis=-1, keepdims=True)
        p = jnp.exp(s)
        p = p / jnp.sum(p, axis=-1, keepdims=True)
        outs.append(_dot(p.astype(BF16), v[:, sl]).astype(BF16))
    return jnp.concatenate(outs, axis=1)


def _route(logits, run, strict_lower):
    m = logits.shape[0]
    r = strict_lower.shape[0]
    lane = lax.broadcasted_iota(jnp.int32, (m, LOGIT_LANES), 1).astype(F32)
    neg = jnp.float32(-jnp.inf)
    big = jnp.float32(LOGIT_LANES)

    def first_argmax(vals):
        mx = jnp.max(vals, axis=-1, keepdims=True)
        idx = jnp.min(jnp.where(vals == mx, lane, big), axis=-1, keepdims=True)
        return mx, idx

    lg = jnp.where(lane < N_GROUPS, logits, neg)
    g_max, g_idx = first_argmax(lg)
    g_w = 1.0 / jnp.sum(jnp.exp(lg - g_max), axis=-1, keepdims=True)

    lo = N_GROUPS + g_idx * EXPERTS_PER_GROUP
    le = jnp.where((lane >= lo) & (lane < lo + EXPERTS_PER_GROUP), logits, neg)
    v1, i1 = first_argmax(le)
    v2, i2 = first_argmax(jnp.where(lane == i1, neg, le))
    t = jnp.exp(v2 - v1)
    gate1 = g_w / (1.0 + t)
    gate2 = g_w * t / (1.0 + t)
    e1 = i1 - N_GROUPS
    e2 = i2 - N_GROUPS

    oh1 = (lane == e1).astype(F32)
    oh2 = (lane == e2).astype(F32)
    oh = oh1 + oh2
    befores = []
    for r0 in range(0, m, r):
        oh_r = oh[r0:r0 + r, :]
        befores.append(_dot(strict_lower, oh_r.astype(BF16)) + run)
        run = run + jnp.sum(oh_r, axis=0, keepdims=True)
    before = befores[0] if len(befores) == 1 else jnp.concatenate(befores, axis=0)
    rank1 = jnp.sum(before * oh1, axis=-1, keepdims=True)
    rank2 = jnp.sum(before * oh2, axis=-1, keepdims=True)
    new_run = run

    rinfo = jnp.where(lane == 0, e1,
            jnp.where(lane == 1, e2,
            jnp.where(lane == 2, gate1,
            jnp.where(lane == 3, gate2,
            jnp.where(lane == 4, rank1,
            jnp.where(lane == 5, rank2, 0.0))))))
    return jnp.transpose(rinfo)[0:SUBLANES, :], new_run


def _conv_segments(a, w_ref, seg_ref, tail_ref, yseg_ref, conv_ref):
    sub = lax.broadcasted_iota(jnp.int32, (SUBLANES, LANES), 0)
    for lt in range(D_CONV // LANES):
        ls = slice(lt * LANES, (lt + 1) * LANES)
        for t0 in range(0, TM, SUBLANES):
            s, m = divmod(t0, SEG)
            seg_ref[lt, pl.ds((SEG_HALO + m) * SUBLANES + s, SUBLANES, stride=SUBLANES), :] = a[t0:t0 + SUBLANES, ls]
        for j in range(SEG_HALO):
            cur = seg_ref[lt, (SEG + j) * SUBLANES:(SEG + j + 1) * SUBLANES, :]
            prev = tail_ref[lt, j * SUBLANES:(j + 1) * SUBLANES, :]
            seg_ref[lt, j * SUBLANES:(j + 1) * SUBLANES, :] = jnp.where(
                sub == 0, pltpu.roll(prev, 1, axis=0), pltpu.roll(cur, 1, axis=0))
            tail_ref[lt, j * SUBLANES:(j + 1) * SUBLANES, :] = cur
        for m0 in range(0, SEG, CONV_BLOCK):
            acc = [jnp.zeros((SUBLANES, LANES), F32) for _ in range(CONV_BLOCK)]
            for idx in range(m0 - HIST, m0 + CONV_BLOCK):
                b = seg_ref[lt, (SEG_HALO + idx) * SUBLANES:(SEG_HALO + idx + 1) * SUBLANES, :]
                for m in range(max(m0, idx), min(m0 + CONV_BLOCK, idx + CONV_WIDTH)):
                    k = idx - m + HIST
                    acc[m - m0] = acc[m - m0] + b * w_ref[k:k + 1, ls]
            for m in range(m0, m0 + CONV_BLOCK):
                yseg_ref[lt, m * SUBLANES:(m + 1) * SUBLANES, :] = acc[m - m0]
        for t0 in range(0, TM, SUBLANES):
            s, m = divmod(t0, SEG)
            conv_ref[t0:t0 + SUBLANES, ls] = yseg_ref[lt, pl.ds(m * SUBLANES + s, SUBLANES, stride=SUBLANES), :]


def _cast_rows(src_ref, dst_ref):
    rows = src_ref.shape[0]

    def body(c, carry):
        r0 = pl.multiple_of(c * CAST_ROWS, CAST_ROWS)
        dst_ref[pl.ds(r0, CAST_ROWS), :] = src_ref[pl.ds(r0, CAST_ROWS), :].astype(BF16)
        return carry

    lax.fori_loop(0, rows // CAST_ROWS, body, 0)


def _trunk_prompt_kernel(x_ref, gmix_ref, win32_ref, convw_ref, convb_ref, lncg_ref, lncb_ref, lnvg_ref, lnvb_ref,
                         wsg_ref, bsg_ref, wout32_ref, gx_ref, wxq32_ref, kmem_ref, v_ref, wxo32_ref, gffn_ref, wr_ref,
                         br_ref,
                         x2_ref, h3_ref, logit_ref, hist_ref,
                         seg_ref, tail_ref, yseg_ref, conv_ref, win_ref, wout_ref, wxq_ref, wxo_ref):
    i = pl.program_id(0)

    @pl.when(i == 0)
    def _():
        tail_ref[...] = jnp.zeros(tail_ref.shape, F32)
        _cast_rows(win32_ref, win_ref)
        _cast_rows(wout32_ref, wout_ref)
        _cast_rows(wxq32_ref, wxq_ref)
        _cast_rows(wxo32_ref, wxo_ref)

    x = x_ref[...]
    h = _rms(x, gmix_ref[...]).astype(BF16)

    a_in = _dot(h, win_ref[:, 0:D_CONV])
    a_gate = _dot(h, win_ref[:, D_CONV:2 * D_CONV])
    a = a_in * _sigmoid(a_gate)
    hist_ref[...] = a[TM - HALO:, :]
    _conv_segments(a, convw_ref, seg_ref, tail_ref, yseg_ref, conv_ref)

    y = _ln(conv_ref[...] + convb_ref[...], lncg_ref[...], lncb_ref[...])
    a_out = (y * _sigmoid(y)).astype(BF16)

    u = _dot(h, win_ref[:, 2 * D_CONV:2 * D_CONV + D_SG])
    v = _ln(_dot(h, win_ref[:, 2 * D_CONV + D_SG:]), lnvg_ref[...], lnvb_ref[...]).astype(BF16)
    ri = lax.broadcasted_iota(jnp.int32, (SG_CHUNK, SG_CHUNK), 0)
    ci = lax.broadcasted_iota(jnp.int32, (SG_CHUNK, SG_CHUNK), 1)
    w_tril = [jnp.where(ci <= ri, wsg_ref[hh], 0.0).astype(BF16) for hh in range(SG_HEADS)]
    gate_rows = []
    for c in range(TM // SG_CHUNK):
        rs = slice(c * SG_CHUNK, (c + 1) * SG_CHUNK)
        heads = [_dot(w_tril[hh], v[rs, hh * SG_HEAD_DIM:(hh + 1) * SG_HEAD_DIM]) for hh in range(SG_HEADS)]
        gate_rows.append(jnp.concatenate(heads, axis=1) + bsg_ref[...])
    b_out = (u * jnp.concatenate(gate_rows, axis=0)).astype(BF16)

    x1 = x + _dot(a_out, wout_ref[0:D_CONV, :]) + _dot(b_out, wout_ref[D_CONV:, :])

    hx = _rms(x1, gx_ref[...]).astype(BF16)
    q = _dot(hx, wxq_ref[...]).astype(BF16)
    x2 = x1 + _dot(_attn_heads(q, kmem_ref[...], v_ref[...], False), wxo_ref[...])
    x2_ref[...] = x2

    h3 = _rms(x2, gffn_ref[...]).astype(BF16)
    h3_ref[...] = _pack_bf16_pairs(h3.astype(F32))
    logit_ref[...] = _dot(h3, wr_ref[...]) + br_ref[...]


def _router_kernel(logit_ref, upper_ref, ones_ref, rt_ref, cnt_ref, run_ref):
    @pl.when(pl.program_id(0) == 0)
    def _():
        run_ref[...] = jnp.zeros(run_ref.shape, F32)

    n = ROUTE_ROWS
    lt = jnp.transpose(logit_ref[...])
    neg = jnp.float32(-jnp.inf)
    big = jnp.float32(LOGIT_LANES)

    def first_argmax(vals, rows):
        mx = jnp.max(vals, axis=0, keepdims=True)
        idx = jnp.min(jnp.where(vals == mx, rows, big), axis=0, keepdims=True)
        return mx, idx

    row8 = lax.broadcasted_iota(jnp.int32, (SUBLANES, n), 0).astype(F32)
    lg = jnp.where(row8 < N_GROUPS, lt[0:SUBLANES, :], neg)
    g_max, g_idx = first_argmax(lg, row8)
    g_w = 1.0 / jnp.sum(jnp.exp(lg - g_max), axis=0, keepdims=True)

    n_rows = N_GROUPS + N_EXPERTS + (-(N_GROUPS + N_EXPERTS)) % SUBLANES
    rows = lax.broadcasted_iota(jnp.int32, (n_rows, n), 0).astype(F32)
    lo = N_GROUPS + g_idx * EXPERTS_PER_GROUP
    le = jnp.where((rows >= lo) & (rows < lo + EXPERTS_PER_GROUP), lt[0:n_rows, :], neg)
    v1, i1 = first_argmax(le, rows)
    v2, i2 = first_argmax(jnp.where(rows == i1, neg, le), rows)
    t = jnp.exp(v2 - v1)
    gate1 = g_w / (1.0 + t)
    gate2 = g_w * t / (1.0 + t)
    e1 = i1 - N_GROUPS
    e2 = i2 - N_GROUPS

    erow = lax.broadcasted_iota(jnp.int32, (LOGIT_LANES, n), 0).astype(F32)
    oh1 = (erow == e1).astype(F32)
    oh2 = (erow == e2).astype(F32)
    oh = (oh1 + oh2).astype(BF16)
    r = upper_ref.shape[0]
    run = run_ref[...]
    rank1, rank2 = [], []
    for c0 in range(0, n, r):
        cs = slice(c0, c0 + r)
        before = _dot(oh[:, cs], upper_ref[...]) + run
        rank1.append(jnp.sum(before * oh1[:, cs], axis=0, keepdims=True))
        rank2.append(jnp.sum(before * oh2[:, cs], axis=0, keepdims=True))
        run = run + _dot(oh[:, cs], ones_ref[...])
    run_ref[...] = run
    cnt_ref[...] = run[:, 0:LANES]

    sub = lax.broadcasted_iota(jnp.int32, (SUBLANES, n), 0)
    vals = (e1, e2, gate1, gate2, jnp.concatenate(rank1, axis=1), jnp.concatenate(rank2, axis=1))
    rt = jnp.zeros((SUBLANES, n), F32)
    for k, v in enumerate(vals):
        rt = jnp.where(sub == k, v, rt)
    rt_ref[...] = rt


def _router(logits, rank_block):
    n = logits.shape[0]
    assert n % ROUTE_ROWS == 0 and ROUTE_ROWS % rank_block == 0
    upper = jnp.triu(jnp.ones((rank_block, rank_block), BF16), 1)
    ones = jnp.ones((rank_block, rank_block), BF16)
    return pl.pallas_call(
        _router_kernel,
        grid=(n // ROUTE_ROWS,),
        in_specs=[pl.BlockSpec((ROUTE_ROWS, LOGIT_LANES), lambda i: (i, 0)),
                  pl.BlockSpec(upper.shape, lambda i: (0, 0)),
                  pl.BlockSpec(ones.shape, lambda i: (0, 0))],
        out_specs=(pl.BlockSpec((SUBLANES, ROUTE_ROWS), lambda i: (0, i)),
                   pl.BlockSpec((LOGIT_LANES, LANES), lambda i: (0, 0))),
        out_shape=(jax.ShapeDtypeStruct((SUBLANES, n), F32), jax.ShapeDtypeStruct((LOGIT_LANES, LANES), F32)),
        scratch_shapes=[pltpu.VMEM((LOGIT_LANES, rank_block), F32)],
        compiler_params=pltpu.CompilerParams(dimension_semantics=("arbitrary",), vmem_limit_bytes=VMEM_LIMIT),
        name="router",
    )(logits, upper, ones)


def _const_spec(shape):
    nd = len(shape)
    return pl.BlockSpec(shape, lambda i: (0,) * nd, pipeline_mode=pl.Buffered(1))


def _trunk_prompt(x, p):
    n = x.shape[0]
    assert n % TM == 0
    row = lambda w: pl.BlockSpec((TM, w), lambda i: (i, 0))
    consts = [p["g_mix"], p["w_in"], p["conv_w"], p["conv_b"], p["ln_conv_g"], p["ln_conv_b"], p["ln_v_g"],
              p["ln_v_b"], p["w_sg"], p["b_sg_rows"], p["w_out"], p["g_xattn"], p["w_xq"], p["k"], p["v"],
              p["w_xo"], p["g_ffn"], p["w_router"], p["b_router"]]
    return pl.pallas_call(
        _trunk_prompt_kernel,
        grid=(n // TM,),
        in_specs=[row(D_MODEL)] + [_const_spec(c.shape) for c in consts],
        out_specs=(row(D_MODEL), row(D_MODEL // 2), row(LOGIT_LANES),
                   pl.BlockSpec((HALO, D_CONV), lambda i: (0, 0))),
        out_shape=(jax.ShapeDtypeStruct((n, D_MODEL), F32),
                   jax.ShapeDtypeStruct((n, D_MODEL // 2), jnp.uint32),
                   jax.ShapeDtypeStruct((n, LOGIT_LANES), F32),
                   jax.ShapeDtypeStruct((HALO, D_CONV), F32)),
        scratch_shapes=[pltpu.VMEM((D_CONV // LANES, (SEG_HALO + SEG) * SUBLANES, LANES), F32),
                        pltpu.VMEM((D_CONV // LANES, SEG_HALO * SUBLANES, LANES), F32),
                        pltpu.VMEM((D_CONV // LANES, TM, LANES), F32),
                        pltpu.VMEM((TM, D_CONV), F32),
                        pltpu.VMEM(p["w_in"].shape, BF16), pltpu.VMEM(p["w_out"].shape, BF16),
                        pltpu.VMEM(p["w_xq"].shape, BF16), pltpu.VMEM(p["w_xo"].shape, BF16)],
        compiler_params=pltpu.CompilerParams(dimension_semantics=("arbitrary",), vmem_limit_bytes=VMEM_LIMIT),
        name="trunk_prompt",
    )(x, *consts)


def _trunk_sample_kernel(n_batch, t_len,
                         x_ref, hist_in_ref, run_in_ref, gmix_ref, win_ref, convw_ref, convb_ref, lncg_ref, lncb_ref,
                         lnvg_ref, lnvb_ref, wsgbd_ref, bsg_ref, wout_ref, gx_ref, wxq_ref, kmem_ref, v_ref, wxo_ref,
                         gffn_ref, wr_ref, br_ref, lower_ref,
                         x2_ref, h3_ref, rt_ref, hist_ref, sgv_ref, cnt_ref,
                         ext_ref, conv_ref, att_ref):
    x = x_ref[...]
    h = _rms(x, gmix_ref[...]).astype(BF16)
    z = _dot(h, win_ref[...].astype(BF16))
    a = z[:, 0:D_CONV] * _sigmoid(z[:, D_CONV:2 * D_CONV])
    ext_len = HIST + t_len
    for b in range(n_batch):
        ext_ref[b, 0:HIST, :] = hist_in_ref[b]
        ext_ref[b, HIST:ext_len, :] = a[b * t_len:(b + 1) * t_len, :]
    for b in range(n_batch):
        acc = jnp.zeros((t_len, D_CONV), F32)
        for k in range(CONV_WIDTH):
            acc = acc + ext_ref[b, k:k + t_len, :] * convw_ref[k:k + 1, :]
        conv_ref[b * t_len:(b + 1) * t_len, :] = acc
        hist_ref[b] = ext_ref[b, ext_len - HIST:ext_len, :]

    y = _ln(conv_ref[...] + convb_ref[...], lncg_ref[...], lncb_ref[...])
    a_out = (y * _sigmoid(y)).astype(BF16)

    u = z[:, 2 * D_CONV:2 * D_CONV + D_SG]
    v = _ln(z[:, 2 * D_CONV + D_SG:], lnvg_ref[...], lnvb_ref[...])
    sgv_ref[...] = v
    vb = v.astype(BF16)
    heads = [_dot(wsgbd_ref[hh], vb[:, hh * SG_HEAD_DIM:(hh + 1) * SG_HEAD_DIM]) for hh in range(SG_HEADS)]
    b_out = (u * (jnp.concatenate(heads, axis=1) + bsg_ref[...])).astype(BF16)

    x1 = (x + _dot(a_out, wout_ref[0:D_CONV, :].astype(BF16))
          + _dot(b_out, wout_ref[D_CONV:, :].astype(BF16)))

    hx = _rms(x1, gx_ref[...]).astype(BF16)
    q = _dot(hx, wxq_ref[...].astype(BF16)).astype(BF16)
    for b in range(n_batch):
        rs = slice(b * t_len, (b + 1) * t_len)
        att_ref[rs, :] = _attn_heads(q[rs, :], kmem_ref[b], v_ref[b], True)
    x2 = x1 + _dot(att_ref[...], wxo_ref[...].astype(BF16))
    x2_ref[...] = x2

    h3 = _rms(x2, gffn_ref[...]).astype(BF16)
    m = n_batch * t_len
    h3_ref[0:m, :] = _pack_bf16_pairs(h3.astype(F32))
    if h3_ref.shape[0] > m:
        h3_ref[m:, :] = jnp.zeros((h3_ref.shape[0] - m, D_MODEL // 2), jnp.uint32)
    rt, new_run = _route(_dot(h3, wr_ref[...]) + br_ref[...], run_in_ref[...], lower_ref[...])
    rt_ref[...] = rt
    cnt_ref[...] = new_run


def _trunk_sample(x, hist, run, p, n_batch, t_len):
    m = n_batch * t_len
    args = [x, hist, run, p["g_mix"], p["w_in"], p["conv_w"], p["conv_b"], p["ln_conv_g"], p["ln_conv_b"],
            p["ln_v_g"], p["ln_v_b"], p["w_sg_bd"], p["b_sg_rows_s"], p["w_out"], p["g_xattn"], p["w_xq"],
            p["k_s"], p["v_s"], p["w_xo"], p["g_ffn"], p["w_router"], p["b_router"], p["lower"]]
    return pl.pallas_call(
        functools.partial(_trunk_sample_kernel, n_batch, t_len),
        out_shape=(jax.ShapeDtypeStruct((m, D_MODEL), F32),
                   jax.ShapeDtypeStruct((-(-m // (SC_WORKERS * SUBLANES)) * SC_WORKERS * SUBLANES, D_MODEL // 2),
                                        jnp.uint32),
                   jax.ShapeDtypeStruct((SUBLANES, m), F32),
                   jax.ShapeDtypeStruct((n_batch, HIST, D_CONV), F32),
                   jax.ShapeDtypeStruct((m, D_SG), F32),
                   jax.ShapeDtypeStruct((1, LOGIT_LANES), F32)),
        scratch_shapes=[pltpu.VMEM((n_batch, HIST + t_len, D_CONV), F32),
                        pltpu.VMEM((m, D_CONV), F32),
                        pltpu.VMEM((m, D_MODEL), BF16)],
        compiler_params=pltpu.CompilerParams(vmem_limit_bytes=VMEM_LIMIT),
        name="trunk_sample",
    )(*args)


def _sc_worker_id():
    return lax.axis_index("s") * SC_CORES + lax.axis_index("c")


def _sc_chunk(per_w, max_chunk):
    assert per_w % SUBLANES == 0 and max_chunk <= LANES
    return max(c for c in range(SUBLANES, max_chunk + 1, SUBLANES) if per_w % c == 0)


def _sc_gather_rows(table, idx):
    n_rows, d = idx.shape[0], table.shape[1]
    per_w = n_rows // SC_WORKERS
    assert per_w * SC_WORKERS == n_rows and per_w % GATHER_ROWS == 0
    n_chunks = per_w // GATHER_ROWS
    lag = GATHER_BUFS // 2
    mesh = plsc.VectorSubcoreMesh(core_axis_name="c", subcore_axis_name="s")

    @functools.partial(
        pl.kernel, mesh=mesh,
        out_type=jax.ShapeDtypeStruct((n_rows, d), table.dtype),
        scratch_types=([pltpu.VMEM((per_w,), jnp.int32)]
                       + [pltpu.VMEM((GATHER_ROWS, d), table.dtype)] * GATHER_BUFS
                       + [pltpu.SemaphoreType.DMA] * (2 * GATHER_BUFS)),
    )
    def gather(table_hbm, idx_hbm, out_hbm, idx_all, *rest):
        rows = rest[:GATHER_BUFS]
        gsem = rest[GATHER_BUFS:2 * GATHER_BUFS]
        wsem = rest[2 * GATHER_BUFS:]
        base = _sc_worker_id() * per_w
        pltpu.sync_copy(idx_hbm.at[pl.ds(base, per_w)], idx_all)

        reads, writes = {}, {}
        for c in range(n_chunks + lag):
            if c < n_chunks:
                b = c % GATHER_BUFS
                if c >= GATHER_BUFS:
                    writes.pop(c - GATHER_BUFS).wait()
                reads[c] = pltpu.async_copy(
                    table_hbm.at[idx_all.at[pl.ds(c * GATHER_ROWS, GATHER_ROWS)]], rows[b], gsem[b])
            w = c - lag
            if w >= 0:
                b = w % GATHER_BUFS
                reads.pop(w).wait()
                writes[w] = pltpu.async_copy(
                    rows[b], out_hbm.at[pl.ds(base + w * GATHER_ROWS, GATHER_ROWS)], wsem[b])
        for w in sorted(writes):
            writes[w].wait()

    return gather(table, idx)


def _sc_scatter_rows2(tables, slots_a, slots_b, n_rows_out, max_chunk):
    d, dtype = tables[0].shape[1], tables[0].dtype
    plans = []
    for t in tables:
        per_w = t.shape[0] // SC_WORKERS
        assert per_w * SC_WORKERS == t.shape[0]
        chunk = _sc_chunk(per_w, max_chunk)
        plans.append((per_w, chunk, per_w // chunk))
    cmax = max(c for _, c, _ in plans)
    n_t = len(tables)
    mesh = plsc.VectorSubcoreMesh(core_axis_name="c", subcore_axis_name="s")

    nb = SCATTER_BUFS
    lag = nb // 3
    scratch = []
    for _, chunk, _ in plans:
        for _ in range(nb):
            scratch += [pltpu.VMEM((chunk,), jnp.int32), pltpu.VMEM((chunk,), jnp.int32)]
    scratch += [pltpu.VMEM((cmax, d), dtype)] * nb
    scratch += [pltpu.SemaphoreType.DMA] * (2 * nb)

    @functools.partial(pl.kernel, mesh=mesh, out_type=jax.ShapeDtypeStruct((n_rows_out, d), dtype),
                       scratch_types=scratch)
    def scatter(*refs):
        tab_hbm = refs[0:n_t]
        sa_hbm = refs[n_t:2 * n_t]
        sb_hbm = refs[2 * n_t:3 * n_t]
        out_hbm = refs[3 * n_t]
        sc = refs[3 * n_t + 1:]
        idx_refs = sc[:2 * nb * n_t]
        rows = sc[2 * nb * n_t:2 * nb * n_t + nb]
        lsem = sc[2 * nb * n_t + nb:2 * nb * n_t + 2 * nb]
        ssem = sc[2 * nb * n_t + 2 * nb:]
        wid = _sc_worker_id()

        work = []
        for t, (per_w, chunk, n_chunks) in enumerate(plans):
            for j in range(n_chunks):
                work.append((t, wid * per_w + j * chunk, chunk))

        def parts(k):
            t, off, chunk = work[k]
            b = k % nb
            ia, ib = idx_refs[2 * nb * t + 2 * b], idx_refs[2 * nb * t + 2 * b + 1]
            rv = rows[b] if chunk == cmax else rows[b].at[pl.ds(0, chunk)]
            return t, off, chunk, b, ia, ib, rv

        def start_load(k):
            t, off, chunk, b, ia, ib, rv = parts(k)
            return (pltpu.async_copy(tab_hbm[t].at[pl.ds(off, chunk)], rv, lsem[b]),
                    pltpu.async_copy(sa_hbm[t].at[pl.ds(off, chunk)], ia, lsem[b]),
                    pltpu.async_copy(sb_hbm[t].at[pl.ds(off, chunk)], ib, lsem[b]))

        def start_scatter(k):
            t, off, chunk, b, ia, ib, rv = parts(k)
            return (pltpu.async_copy(rv, out_hbm.at[ia], ssem[b]), pltpu.async_copy(rv, out_hbm.at[ib], ssem[b]))

        loads, scatters = {}, {}
        for k in range(len(work) + lag):
            if k < len(work):
                if k >= nb:
                    for c in scatters.pop(k - nb):
                        c.wait()
                loads[k] = start_load(k)
            w = k - lag
            if w >= 0:
                for c in loads.pop(w):
                    c.wait()
                scatters[w] = start_scatter(w)
        for w in sorted(scatters):
            for c in scatters[w]:
                c.wait()

    return scatter(*tables, *slots_a, *slots_b)


def _experts_kernel(first_ref, nblk_ref, cnt_ref, tot_ref, xs_hbm, wg_ref, wu_ref, wd_ref, ys_hbm,
                    xbuf, ybuf, wg_bf, wu_bf, wd_bf, in_sem, out_sem):
    e = pl.program_id(0)
    nb = nblk_ref[e]
    first = first_ref[e]
    cnt = cnt_ref[e]
    total = tot_ref[0]
    half = D_MODEL // 2

    def in_copy(gb):
        slot = lax.rem(gb, X_BUFS)
        return pltpu.make_async_copy(xs_hbm.at[pl.ds(gb * BM, BM)], xbuf.at[slot], in_sem.at[slot])

    def out_copy(gb):
        slot = lax.rem(gb, Y_BUFS)
        return pltpu.make_async_copy(ybuf.at[slot], ys_hbm.at[pl.ds(gb * BM, BM)], out_sem.at[slot])

    @pl.when(nb > 0)
    def _():
        @pl.when(first == 0)
        def _():
            for k in range(X_LOOKAHEAD):
                @pl.when(k < total)
                def _():
                    in_copy(k).start(priority=ROW_DMA_PRIORITY)

        wg_bf[...] = wg_ref[0].astype(BF16)
        wu_bf[...] = wu_ref[0].astype(BF16)
        wd_bf[...] = wd_ref[0].astype(BF16)

        def acquire(gb):
            @pl.when(gb + X_LOOKAHEAD < total)
            def _():
                in_copy(gb + X_LOOKAHEAD).start(priority=ROW_DMA_PRIORITY)

            in_copy(gb).wait()

            @pl.when(gb >= Y_BUFS)
            def _():
                out_copy(gb - Y_BUFS).wait()

        def ffn(gb, j):
            live = lax.broadcasted_iota(jnp.int32, (BM, half), 0) < cnt - j * BM
            lo, hi = _unpack_bf16_pairs(jnp.where(live, xbuf[lax.rem(gb, X_BUFS)], jnp.uint32(0)))
            g = _dot(lo, wg_bf[0:half, :]) + _dot(hi, wg_bf[half:, :])
            u = _dot(lo, wu_bf[0:half, :]) + _dot(hi, wu_bf[half:, :])
            hm = (g * _sigmoid(g) * u).astype(BF16)
            y = _dot(hm, wd_bf[...])
            ybuf[lax.rem(gb, Y_BUFS)] = _pack_bf16_pairs(y.astype(BF16).astype(F32))

        def block_pair(jp, carry):
            j0 = 2 * jp
            g0 = first + j0
            acquire(g0)
            acquire(g0 + 1)
            ffn(g0, j0)
            ffn(g0 + 1, j0 + 1)
            out_copy(g0).start(priority=ROW_DMA_PRIORITY)
            out_copy(g0 + 1).start(priority=ROW_DMA_PRIORITY)
            return carry

        lax.fori_loop(0, nb // 2, block_pair, 0)

        @pl.when(lax.rem(nb, 2) == 1)
        def _():
            gl = first + nb - 1
            acquire(gl)
            ffn(gl, nb - 1)
            out_copy(gl).start(priority=ROW_DMA_PRIORITY)

        @pl.when(first + nb == total)
        def _():
            for k in range(Y_BUFS):
                @pl.when(total - 1 - k >= 0)
                def _():
                    out_copy(total - 1 - k).wait()


def _experts(xs, n_rows_out, first_block, n_blocks_e, counts, w_eg, w_eu, w_ed):
    w_map = lambda e, fb, nb, ct, tot: (e, 0, 0)
    half = D_MODEL // 2
    total = jnp.sum(n_blocks_e).astype(jnp.int32).reshape(1)
    return pl.pallas_call(
        _experts_kernel,
        grid_spec=pltpu.PrefetchScalarGridSpec(
            num_scalar_prefetch=4,
            grid=(N_EXPERTS,),
            in_specs=[pl.BlockSpec(memory_space=pl.ANY),
                      pl.BlockSpec((1, D_MODEL, D_EXPERT), w_map),
                      pl.BlockSpec((1, D_MODEL, D_EXPERT), w_map),
                      pl.BlockSpec((1, D_EXPERT, D_MODEL), w_map)],
            out_specs=pl.BlockSpec(memory_space=pl.ANY),
            scratch_shapes=[pltpu.VMEM((X_BUFS, BM, half), jnp.uint32), pltpu.VMEM((Y_BUFS, BM, half), jnp.uint32),
                            pltpu.VMEM((D_MODEL, D_EXPERT), BF16), pltpu.VMEM((D_MODEL, D_EXPERT), BF16),
                            pltpu.VMEM((D_EXPERT, D_MODEL), BF16),
                            pltpu.SemaphoreType.DMA((X_BUFS,)), pltpu.SemaphoreType.DMA((Y_BUFS,))]),
        out_shape=jax.ShapeDtypeStruct((n_rows_out, half), jnp.uint32),
        compiler_params=pltpu.CompilerParams(dimension_semantics=("arbitrary",), vmem_limit_bytes=VMEM_LIMIT),
        name="experts",
    )(first_block, n_blocks_e, counts, total, xs, w_eg, w_eu, w_ed)


def _combine_kernel(x2_ref, y1_ref, y2_ref, rt_ref, g_ref, o_ref):
    rt = rt_ref[...]
    r = jnp.transpose(jnp.concatenate([rt, jnp.zeros((LANES - rt.shape[0], rt.shape[1]), F32)], axis=0))
    g1, g2 = r[:, 2:3], r[:, 3:4]
    half = D_MODEL // 2
    y1_lo, y1_hi = _unpack_bf16_pairs_f32(y1_ref[...])
    y2_lo, y2_hi = _unpack_bf16_pairs_f32(y2_ref[...])
    x_lo = x2_ref[:, 0:half] + g1 * y1_lo + g2 * y2_lo
    x_hi = x2_ref[:, half:] + g1 * y1_hi + g2 * y2_hi
    ms = (jnp.sum(x_lo * x_lo, axis=-1, keepdims=True) + jnp.sum(x_hi * x_hi, axis=-1, keepdims=True)) / D_MODEL
    inv = lax.rsqrt(ms + EPS)
    o_ref[:, 0:half] = x_lo * inv * g_ref[:, 0:half]
    o_ref[:, half:] = x_hi * inv * g_ref[:, half:]


def _combine(x2, yg, rt, g_final, tm, blk1, blk2):
    n = x2.shape[0]
    return pl.pallas_call(
        _combine_kernel,
        grid=(n // tm,),
        in_specs=[pl.BlockSpec((tm, D_MODEL), lambda i: (i, 0)),
                  pl.BlockSpec((tm, D_MODEL // 2), lambda i: (blk1 + i, 0)),
                  pl.BlockSpec((tm, D_MODEL // 2), lambda i: (blk2 + i, 0)),
                  pl.BlockSpec((SUBLANES, tm), lambda i: (0, i)),
                  pl.BlockSpec((1, D_MODEL), lambda i: (0, 0))],
        out_specs=pl.BlockSpec((tm, D_MODEL), lambda i: (i, 0)),
        out_shape=jax.ShapeDtypeStruct((n, D_MODEL), F32),
        compiler_params=pltpu.CompilerParams(dimension_semantics=("arbitrary",), vmem_limit_bytes=VMEM_LIMIT),
        name="combine",
    )(x2, yg, yg, rt, g_final)


def _gather_rows(table, idx):
    return _sc_gather_rows(table, idx)


def _scatter_rows2(tables, slots_a, slots_b, n_rows_out):
    return _sc_scatter_rows2(tables, slots_a, slots_b, n_rows_out, SCATTER_CHUNK)


def kernel(x_prompt, x_sample, mem_prompt, state_conv, cache_mem_k, cache_mem_v, g_mix, w_in, conv_w, conv_b, ln_conv_g, ln_conv_b, ln_v_g, ln_v_b, w_sg, b_sg, w_out, g_mem, w_mk, w_mv, g_xattn, w_xq, w_xo, g_ffn, w_router_group, b_router_group, w_router_expert, b_router_expert, w_expert_gate, w_expert_up, w_expert_down, g_final):
    assert x_prompt.shape[0] == 1 and g_mix.shape[0] == 1
    n_p = x_prompt.shape[1]
    n_batch, t_len = x_sample.shape[0], x_sample.shape[1]
    n_s = n_batch * t_len
    row = lambda a: a.reshape(1, -1)

    w_router = jnp.concatenate(
        [w_router_group[0], jnp.transpose(w_router_expert[0], (1, 0, 2)).reshape(D_MODEL, N_EXPERTS)], axis=1)
    w_router = jnp.pad(w_router, ((0, 0), (0, LOGIT_LANES - w_router.shape[1]))).astype(BF16)
    b_router = jnp.pad(jnp.concatenate([b_router_group[0], b_router_expert[0].reshape(-1)]),
                       (0, LOGIT_LANES - N_GROUPS - N_EXPERTS)).reshape(1, LOGIT_LANES)
    tril_t = jnp.tril(jnp.ones((t_len, t_len), bool))
    w_sg_t = jnp.where(tril_t, w_sg[0][:, :t_len, :t_len], 0.0)
    eye_b = jnp.eye(n_batch, dtype=F32)
    w_sg_bd = jnp.einsum("ab,hij->haibj", eye_b, w_sg_t).reshape(SG_HEADS, n_s, n_s).astype(BF16)
    p = {
        "g_mix": row(g_mix[0]), "w_in": w_in[0],
        "conv_w": jnp.pad(conv_w[0], ((0, 1), (0, 0))), "conv_b": row(conv_b[0]),
        "ln_conv_g": row(ln_conv_g[0]), "ln_conv_b": row(ln_conv_b[0]),
        "ln_v_g": row(ln_v_g[0]), "ln_v_b": row(ln_v_b[0]),
        "w_sg": w_sg[0],
        "b_sg_rows": jnp.repeat(b_sg[0].T, SG_HEAD_DIM, axis=1),
        "w_sg_bd": w_sg_bd,
        "b_sg_rows_s": jnp.tile(jnp.repeat(b_sg[0][:, :t_len].T, SG_HEAD_DIM, axis=1), (n_batch, 1)),
        "w_out": w_out[0], "g_xattn": row(g_xattn[0]),
        "w_xq": w_xq[0], "w_xo": w_xo[0], "g_ffn": row(g_ffn[0]),
        "w_router": w_router, "b_router": b_router,
        "lower": jnp.tril(jnp.ones((n_s, n_s), BF16), -1),
    }

    k_p, v_p = _memkv(mem_prompt[0], row(g_mem[0]), w_mk[0], w_mv[0])
    p["k"] = k_p.astype(BF16)
    p["v"] = v_p.astype(BF16)
    p["k_s"] = jnp.transpose(cache_mem_k[0].reshape(n_batch, N_MEM, D_MODEL), (0, 2, 1)).astype(BF16)
    p["v_s"] = cache_mem_v[0].reshape(n_batch, N_MEM, D_MODEL).astype(BF16)

    assert n_p % n_s == 0
    x2_p, h3_p, logits_p, hist_p = _trunk_prompt(x_prompt[0], p)
    rt_p, cnt_t = _router(logits_p, TM)
    cnt_p = cnt_t[:, 0].reshape(1, LOGIT_LANES)
    x2_s, h3_s, rt_s, hist_s, sgv_s, cnt = _trunk_sample(
        x_sample.reshape(n_s, D_MODEL), state_conv[0], cnt_p, p, n_batch, t_len)

    experts = jnp.arange(N_EXPERTS, dtype=jnp.int32)
    w_e = (w_expert_gate[0], w_expert_up[0], w_expert_down[0])

    def moe_pass(cnt, h3_tables, rts, n_real):
        n_tot = sum(n_real)
        n_slots = -(-(n_tot * 2) // BM) * BM + N_EXPERTS * BM
        counts = cnt[0, :N_EXPERTS].astype(jnp.int32)
        padded = (counts + BM - 1) // BM * BM
        pad_start = jnp.cumsum(padded) - padded

        def one(e_row, rank_row):
            e = e_row.astype(jnp.int32)
            start = jnp.sum(jnp.where(e[None, :] == experts[:, None], pad_start[:, None], 0), axis=0)
            return start + rank_row.astype(jnp.int32)

        slots = [(one(rt[0], rt[4]), one(rt[1], rt[5])) for rt in rts]
        sa, sb, spare0 = [], [], n_slots
        for tab, (a, b), n in zip(h3_tables, slots, n_real):
            n_spare = tab.shape[0] - n
            spare = spare0 + jnp.arange(n_spare, dtype=jnp.int32)
            sa.append(jnp.concatenate([a, spare]))
            sb.append(jnp.concatenate([b, spare + n_spare]))
            spare0 += 2 * n_spare
        xs = _scatter_rows2(tuple(h3_tables), tuple(sa), tuple(sb), spare0)
        ys = _experts(xs, n_slots, pad_start // BM, padded // BM, counts, *w_e)
        back_idx = jnp.concatenate([s for ab in slots for s in ab])
        n_back = -(-back_idx.shape[0] // (SC_WORKERS * GATHER_ROWS)) * (SC_WORKERS * GATHER_ROWS)
        return _gather_rows(ys, jnp.pad(back_idx, (0, n_back - back_idx.shape[0])))

    yg = moe_pass(cnt, [h3_p, h3_s], [rt_p, rt_s], [n_p, n_s])

    gf = row(g_final)
    y_p = _combine(x2_p, yg, rt_p, gf, TM_COMBINE, 0, n_p // TM_COMBINE)
    y_s = _combine(x2_s, yg, rt_s, gf, n_s, 2 * n_p // n_s, 2 * n_p // n_s + 1)

    return (y_p.reshape(1, n_p, D_MODEL),
            y_s.reshape(n_batch, t_len, D_MODEL),
            hist_p[HALO - HIST:].reshape(1, 1, HIST, D_CONV),
            hist_s.reshape(1, n_batch, HIST, D_CONV),
            k_p.reshape(1, 1, N_MEM, X_HEADS, X_HEAD_DIM),
            v_p.reshape(1, 1, N_MEM, X_HEADS, X_HEAD_DIM),
            sgv_s.reshape(1, n_batch, t_len, D_SG))
```

```python
import functools

import jax
import jax.numpy as jnp
from jax import lax
from jax.experimental import pallas as pl
from jax.experimental.pallas import tpu as pltpu
from jax.experimental.pallas import tpu_sc as plsc

D_MODEL = 1024
D_CONV = 512
D_SG = 512
CONV_WIDTH = 31
HIST = CONV_WIDTH - 1
SG_HEADS = 4
SG_HEAD_DIM = 128
SG_CHUNK = 128
N_MEM = 256
X_HEADS = 4
X_HEAD_DIM = 256
N_GROUPS = 4
EXPERTS_PER_GROUP = 8
N_EXPERTS = 32
D_EXPERT = 512
EPS = 1e-6

LANES = 128
SUBLANES = 8
SC_CORES = 2
SC_SUBCORES = 16
SC_WORKERS = SC_CORES * SC_SUBCORES
VMEM_LIMIT = 56 * 1024 * 1024

TM = 512
TM_COMBINE = 1024
ROUTE_ROWS = 2048
HALO = 32
SEG = TM // SUBLANES
SEG_HALO = 32
CONV_BLOCK = 16
CAST_ROWS = 64
BM = 256
X_LOOKAHEAD = 4
X_BUFS = X_LOOKAHEAD + 2
Y_BUFS = 4
ROW_DMA_PRIORITY = 1
GATHER_ROWS = 32
GATHER_BUFS = 6
SCATTER_CHUNK = 32
SCATTER_BUFS = 4
TAG_WORDS = 128
BACK_CHUNK = 56
BACK_BUFS = 4
LOGIT_LANES = 128

F32 = jnp.float32
BF16 = jnp.bfloat16


def _dot(a, b):
    return jnp.dot(a, b, preferred_element_type=F32)


def _rms(x, g):
    return x * lax.rsqrt(jnp.mean(x * x, axis=-1, keepdims=True) + EPS) * g


def _ln(x, g, b):
    mu = jnp.mean(x, axis=-1, keepdims=True)
    xc = x - mu
    var = jnp.mean(xc * xc, axis=-1, keepdims=True)
    return xc * lax.rsqrt(var + EPS) * g + b


def _sigmoid(x):
    return 1.0 / (1.0 + jnp.exp(-x))


def _pack_bf16_pairs(h):
    bits = lax.bitcast_convert_type(h, jnp.uint32)
    half = h.shape[1] // 2
    lo = lax.shift_right_logical(bits[:, :half], jnp.uint32(16))
    hi = bits[:, half:] & jnp.uint32(0xFFFF0000)
    return hi | lo


def _unpack_bf16_pairs_f32(p):
    lo = lax.bitcast_convert_type(lax.shift_left(p, jnp.uint32(16)), F32)
    hi = lax.bitcast_convert_type(p & jnp.uint32(0xFFFF0000), F32)
    return lo, hi


def _unpack_bf16_pairs(p):
    lo, hi = _unpack_bf16_pairs_f32(p)
    return lo.astype(BF16), hi.astype(BF16)


def _memkv_kernel(mem_ref, g_ref, wk_ref, wv_ref, k_ref, v_ref):
    m = _rms(mem_ref[...], g_ref[...]).astype(BF16)
    k_ref[...] = _dot(m, wk_ref[...].astype(BF16))
    v_ref[...] = _dot(m, wv_ref[...].astype(BF16))


def _memkv(mem, g_mem, w_mk, w_mv):
    return pl.pallas_call(
        _memkv_kernel,
        out_shape=(jax.ShapeDtypeStruct((N_MEM, D_MODEL), F32), jax.ShapeDtypeStruct((N_MEM, D_MODEL), F32)),
        compiler_params=pltpu.CompilerParams(vmem_limit_bytes=VMEM_LIMIT),
        name="memkv",
    )(mem, g_mem, w_mk, w_mv)


def _attn_heads(q, k, v, k_transposed):
    outs = []
    for h in range(X_HEADS):
        sl = slice(h * X_HEAD_DIM, (h + 1) * X_HEAD_DIM)
        if k_transposed:
            s = _dot(q[:, sl], k[sl, :])
        else:
            s = lax.dot_general(q[:, sl], k[:, sl], (((1,), (1,)), ((), ())), preferred_element_type=F32)
        s = s * (X_HEAD_DIM ** -0.5)
        s = s - jnp.max(s, axis=-1, keepdims=True)
        p = jnp.exp(s)
        p = p / jnp.sum(p, axis=-1, keepdims=True)
        outs.append(_dot(p.astype(BF16), v[:, sl]).astype(BF16))
    return jnp.concatenate(outs, axis=1)


def _route(logits, run, strict_lower):
    m = logits.shape[0]
    r = strict_lower.shape[0]
    lane = lax.broadcasted_iota(jnp.int32, (m, LOGIT_LANES), 1).astype(F32)
    neg = jnp.float32(-jnp.inf)
    big = jnp.float32(LOGIT_LANES)

    def first_argmax(vals):
        mx = jnp.max(vals, axis=-1, keepdims=True)
        idx = jnp.min(jnp.where(vals == mx, lane, big), axis=-1, keepdims=True)
        return mx, idx

    lg = jnp.where(lane < N_GROUPS, logits, neg)
    g_max, g_idx = first_argmax(lg)
    g_w = 1.0 / jnp.sum(jnp.exp(lg - g_max), axis=-1, keepdims=True)

    lo = N_GROUPS + g_idx * EXPERTS_PER_GROUP
    le = jnp.where((lane >= lo) & (lane < lo + EXPERTS_PER_GROUP), logits, neg)
    v1, i1 = first_argmax(le)
    v2, i2 = first_argmax(jnp.where(lane == i1, neg, le))
    t = jnp.exp(v2 - v1)
    gate1 = g_w / (1.0 + t)
    gate2 = g_w * t / (1.0 + t)
    e1 = i1 - N_GROUPS
    e2 = i2 - N_GROUPS

    oh1 = (lane == e1).astype(F32)
    oh2 = (lane == e2).astype(F32)
    oh = oh1 + oh2
    befores = []
    for r0 in range(0, m, r):
        oh_r = oh[r0:r0 + r, :]
        befores.append(_dot(strict_lower, oh_r.astype(BF16)) + run)
        run = run + jnp.sum(oh_r, axis=0, keepdims=True)
    before = befores[0] if len(befores) == 1 else jnp.concatenate(befores, axis=0)
    rank1 = jnp.sum(before * oh1, axis=-1, keepdims=True)
    rank2 = jnp.sum(before * oh2, axis=-1, keepdims=True)
    new_run = run

    rinfo = jnp.where(lane == 0, e1,
            jnp.where(lane == 1, e2,
            jnp.where(lane == 2, gate1,
            jnp.where(lane == 3, gate2,
            jnp.where(lane == 4, rank1,
            jnp.where(lane == 5, rank2, 0.0))))))
    return jnp.transpose(rinfo)[0:SUBLANES, :], new_run


def _conv_segments(a, w_ref, seg_ref, tail_ref, yseg_ref, conv_ref):
    sub = lax.broadcasted_iota(jnp.int32, (SUBLANES, LANES), 0)
    for lt in range(D_CONV // LANES):
        ls = slice(lt * LANES, (lt + 1) * LANES)
        for t0 in range(0, TM, SUBLANES):
            s, m = divmod(t0, SEG)
            seg_ref[lt, pl.ds((SEG_HALO + m) * SUBLANES + s, SUBLANES, stride=SUBLANES), :] = a[t0:t0 + SUBLANES, ls]
        for j in range(SEG_HALO):
            cur = seg_ref[lt, (SEG + j) * SUBLANES:(SEG + j + 1) * SUBLANES, :]
            prev = tail_ref[lt, j * SUBLANES:(j + 1) * SUBLANES, :]
            seg_ref[lt, j * SUBLANES:(j + 1) * SUBLANES, :] = jnp.where(
                sub == 0, pltpu.roll(prev, 1, axis=0), pltpu.roll(cur, 1, axis=0))
            tail_ref[lt, j * SUBLANES:(j + 1) * SUBLANES, :] = cur
        for m0 in range(0, SEG, CONV_BLOCK):
            acc = [jnp.zeros((SUBLANES, LANES), F32) for _ in range(CONV_BLOCK)]
            for idx in range(m0 - HIST, m0 + CONV_BLOCK):
                b = seg_ref[lt, (SEG_HALO + idx) * SUBLANES:(SEG_HALO + idx + 1) * SUBLANES, :]
                for m in range(max(m0, idx), min(m0 + CONV_BLOCK, idx + CONV_WIDTH)):
                    k = idx - m + HIST
                    acc[m - m0] = acc[m - m0] + b * w_ref[k:k + 1, ls]
            for m in range(m0, m0 + CONV_BLOCK):
                yseg_ref[lt, m * SUBLANES:(m + 1) * SUBLANES, :] = acc[m - m0]
        for t0 in range(0, TM, SUBLANES):
            s, m = divmod(t0, SEG)
            conv_ref[t0:t0 + SUBLANES, ls] = yseg_ref[lt, pl.ds(m * SUBLANES + s, SUBLANES, stride=SUBLANES), :]


def _cast_rows(src_ref, dst_ref):
    rows = src_ref.shape[0]

    def body(c, carry):
        r0 = pl.multiple_of(c * CAST_ROWS, CAST_ROWS)
        dst_ref[pl.ds(r0, CAST_ROWS), :] = src_ref[pl.ds(r0, CAST_ROWS), :].astype(BF16)
        return carry

    lax.fori_loop(0, rows // CAST_ROWS, body, 0)


def _trunk_prompt_kernel(x_ref, gmix_ref, win32_ref, convw_ref, convb_ref, lncg_ref, lncb_ref, lnvg_ref, lnvb_ref,
                         wsg_ref, bsg_ref, wout32_ref, gx_ref, wxq32_ref, kmem_ref, v_ref, wxo32_ref, gffn_ref, wr_ref,
                         br_ref,
                         x2_ref, h3_ref, logit_ref, hist_ref,
                         seg_ref, tail_ref, yseg_ref, conv_ref, win_ref, wout_ref, wxq_ref, wxo_ref):
    i = pl.program_id(0)

    @pl.when(i == 0)
    def _():
        tail_ref[...] = jnp.zeros(tail_ref.shape, F32)
        _cast_rows(win32_ref, win_ref)
        _cast_rows(wout32_ref, wout_ref)
        _cast_rows(wxq32_ref, wxq_ref)
        _cast_rows(wxo32_ref, wxo_ref)

    x = x_ref[...]
    h = _rms(x, gmix_ref[...]).astype(BF16)

    a_in = _dot(h, win_ref[:, 0:D_CONV])
    a_gate = _dot(h, win_ref[:, D_CONV:2 * D_CONV])
    a = a_in * _sigmoid(a_gate)
    hist_ref[...] = a[TM - HALO:, :]
    _conv_segments(a, convw_ref, seg_ref, tail_ref, yseg_ref, conv_ref)

    y = _ln(conv_ref[...] + convb_ref[...], lncg_ref[...], lncb_ref[...])
    a_out = (y * _sigmoid(y)).astype(BF16)

    u = _dot(h, win_ref[:, 2 * D_CONV:2 * D_CONV + D_SG])
    v = _ln(_dot(h, win_ref[:, 2 * D_CONV + D_SG:]), lnvg_ref[...], lnvb_ref[...]).astype(BF16)
    ri = lax.broadcasted_iota(jnp.int32, (SG_CHUNK, SG_CHUNK), 0)
    ci = lax.broadcasted_iota(jnp.int32, (SG_CHUNK, SG_CHUNK), 1)
    w_tril = [jnp.where(ci <= ri, wsg_ref[hh], 0.0).astype(BF16) for hh in range(SG_HEADS)]
    gate_rows = []
    for c in range(TM // SG_CHUNK):
        rs = slice(c * SG_CHUNK, (c + 1) * SG_CHUNK)
        heads = [_dot(w_tril[hh], v[rs, hh * SG_HEAD_DIM:(hh + 1) * SG_HEAD_DIM]) for hh in range(SG_HEADS)]
        gate_rows.append(jnp.concatenate(heads, axis=1) + bsg_ref[...])
    b_out = (u * jnp.concatenate(gate_rows, axis=0)).astype(BF16)

    x1 = x + _dot(a_out, wout_ref[0:D_CONV, :]) + _dot(b_out, wout_ref[D_CONV:, :])

    hx = _rms(x1, gx_ref[...]).astype(BF16)
    q = _dot(hx, wxq_ref[...]).astype(BF16)
    x2 = x1 + _dot(_attn_heads(q, kmem_ref[...], v_ref[...], False), wxo_ref[...])
    x2_ref[...] = x2

    h3 = _rms(x2, gffn_ref[...]).astype(BF16)
    h3_ref[...] = _pack_bf16_pairs(h3.astype(F32))
    logit_ref[...] = _dot(h3, wr_ref[...]) + br_ref[...]


def _router_kernel(logit_ref, upper_ref, ones_ref, rt_ref, cnt_ref, run_ref):
    @pl.when(pl.program_id(0) == 0)
    def _():
        run_ref[...] = jnp.zeros(run_ref.shape, F32)

    n = ROUTE_ROWS
    lt = jnp.transpose(logit_ref[...])
    neg = jnp.float32(-jnp.inf)
    big = jnp.float32(LOGIT_LANES)

    def first_argmax(vals, rows):
        mx = jnp.max(vals, axis=0, keepdims=True)
        idx = jnp.min(jnp.where(vals == mx, rows, big), axis=0, keepdims=True)
        return mx, idx

    row8 = lax.broadcasted_iota(jnp.int32, (SUBLANES, n), 0).astype(F32)
    lg = jnp.where(row8 < N_GROUPS, lt[0:SUBLANES, :], neg)
    g_max, g_idx = first_argmax(lg, row8)
    g_w = 1.0 / jnp.sum(jnp.exp(lg - g_max), axis=0, keepdims=True)

    n_rows = N_GROUPS + N_EXPERTS + (-(N_GROUPS + N_EXPERTS)) % SUBLANES
    rows = lax.broadcasted_iota(jnp.int32, (n_rows, n), 0).astype(F32)
    lo = N_GROUPS + g_idx * EXPERTS_PER_GROUP
    le = jnp.where((rows >= lo) & (rows < lo + EXPERTS_PER_GROUP), lt[0:n_rows, :], neg)
    v1, i1 = first_argmax(le, rows)
    v2, i2 = first_argmax(jnp.where(rows == i1, neg, le), rows)
    t = jnp.exp(v2 - v1)
    gate1 = g_w / (1.0 + t)
    gate2 = g_w * t / (1.0 + t)
    e1 = i1 - N_GROUPS
    e2 = i2 - N_GROUPS

    erow = lax.broadcasted_iota(jnp.int32, (LOGIT_LANES, n), 0).astype(F32)
    oh1 = (erow == e1).astype(F32)
    oh2 = (erow == e2).astype(F32)
    oh = (oh1 + oh2).astype(BF16)
    r = upper_ref.shape[0]
    run = run_ref[...]
    rank1, rank2 = [], []
    for c0 in range(0, n, r):
        cs = slice(c0, c0 + r)
        before = _dot(oh[:, cs], upper_ref[...]) + run
        rank1.append(jnp.sum(before * oh1[:, cs], axis=0, keepdims=True))
        rank2.append(jnp.sum(before * oh2[:, cs], axis=0, keepdims=True))
        run = run + _dot(oh[:, cs], ones_ref[...])
    run_ref[...] = run
    cnt_ref[...] = run[:, 0:LANES]

    sub = lax.broadcasted_iota(jnp.int32, (SUBLANES, n), 0)
    vals = (e1, e2, gate1, gate2, jnp.concatenate(rank1, axis=1), jnp.concatenate(rank2, axis=1))
    rt = jnp.zeros((SUBLANES, n), F32)
    for k, v in enumerate(vals):
        rt = jnp.where(sub == k, v, rt)
    rt_ref[...] = rt


def _router(logits, rank_block):
    n = logits.shape[0]
    assert n % ROUTE_ROWS == 0 and ROUTE_ROWS % rank_block == 0
    upper = jnp.triu(jnp.ones((rank_block, rank_block), BF16), 1)
    ones = jnp.ones((rank_block, rank_block), BF16)
    return pl.pallas_call(
        _router_kernel,
        grid=(n // ROUTE_ROWS,),
        in_specs=[pl.BlockSpec((ROUTE_ROWS, LOGIT_LANES), lambda i: (i, 0)),
                  pl.BlockSpec(upper.shape, lambda i: (0, 0)),
                  pl.BlockSpec(ones.shape, lambda i: (0, 0))],
        out_specs=(pl.BlockSpec((SUBLANES, ROUTE_ROWS), lambda i: (0, i)),
                   pl.BlockSpec((LOGIT_LANES, LANES), lambda i: (0, 0))),
        out_shape=(jax.ShapeDtypeStruct((SUBLANES, n), F32), jax.ShapeDtypeStruct((LOGIT_LANES, LANES), F32)),
        scratch_shapes=[pltpu.VMEM((LOGIT_LANES, rank_block), F32)],
        compiler_params=pltpu.CompilerParams(dimension_semantics=("arbitrary",), vmem_limit_bytes=VMEM_LIMIT),
        name="router",
    )(logits, upper, ones)


def _const_spec(shape):
    nd = len(shape)
    return pl.BlockSpec(shape, lambda i: (0,) * nd, pipeline_mode=pl.Buffered(1))


def _trunk_prompt(x, p):
    n = x.shape[0]
    assert n % TM == 0
    row = lambda w: pl.BlockSpec((TM, w), lambda i: (i, 0))
    consts = [p["g_mix"], p["w_in"], p["conv_w"], p["conv_b"], p["ln_conv_g"], p["ln_conv_b"], p["ln_v_g"],
              p["ln_v_b"], p["w_sg"], p["b_sg_rows"], p["w_out"], p["g_xattn"], p["w_xq"], p["k"], p["v"],
              p["w_xo"], p["g_ffn"], p["w_router"], p["b_router"]]
    return pl.pallas_call(
        _trunk_prompt_kernel,
        grid=(n // TM,),
        in_specs=[row(D_MODEL)] + [_const_spec(c.shape) for c in consts],
        out_specs=(row(D_MODEL), row(D_MODEL // 2), row(LOGIT_LANES),
                   pl.BlockSpec((HALO, D_CONV), lambda i: (0, 0))),
        out_shape=(jax.ShapeDtypeStruct((n, D_MODEL), F32),
                   jax.ShapeDtypeStruct((n, D_MODEL // 2), jnp.uint32),
                   jax.ShapeDtypeStruct((n, LOGIT_LANES), F32),
                   jax.ShapeDtypeStruct((HALO, D_CONV), F32)),
        scratch_shapes=[pltpu.VMEM((D_CONV // LANES, (SEG_HALO + SEG) * SUBLANES, LANES), F32),
                        pltpu.VMEM((D_CONV // LANES, SEG_HALO * SUBLANES, LANES), F32),
                        pltpu.VMEM((D_CONV // LANES, TM, LANES), F32),
                        pltpu.VMEM((TM, D_CONV), F32),
                        pltpu.VMEM(p["w_in"].shape, BF16), pltpu.VMEM(p["w_out"].shape, BF16),
                        pltpu.VMEM(p["w_xq"].shape, BF16), pltpu.VMEM(p["w_xo"].shape, BF16)],
        compiler_params=pltpu.CompilerParams(dimension_semantics=("arbitrary",), vmem_limit_bytes=VMEM_LIMIT),
        name="trunk_prompt",
    )(x, *consts)


def _trunk_sample_kernel(n_batch, t_len,
                         x_ref, hist_in_ref, run_in_ref, gmix_ref, win_ref, convw_ref, convb_ref, lncg_ref, lncb_ref,
                         lnvg_ref, lnvb_ref, wsgbd_ref, bsg_ref, wout_ref, gx_ref, wxq_ref, kmem_ref, v_ref, wxo_ref,
                         gffn_ref, wr_ref, br_ref, lower_ref,
                         x2_ref, h3_ref, rt_ref, hist_ref, sgv_ref, cnt_ref,
                         ext_ref, conv_ref, att_ref):
    x = x_ref[...]
    h = _rms(x, gmix_ref[...]).astype(BF16)
    z = _dot(h, win_ref[...].astype(BF16))
    a = z[:, 0:D_CONV] * _sigmoid(z[:, D_CONV:2 * D_CONV])
    ext_len = HIST + t_len
    for b in range(n_batch):
        ext_ref[b, 0:HIST, :] = hist_in_ref[b]
        ext_ref[b, HIST:ext_len, :] = a[b * t_len:(b + 1) * t_len, :]
    for b in range(n_batch):
        acc = jnp.zeros((t_len, D_CONV), F32)
        for k in range(CONV_WIDTH):
            acc = acc + ext_ref[b, k:k + t_len, :] * convw_ref[k:k + 1, :]
        conv_ref[b * t_len:(b + 1) * t_len, :] = acc
        hist_ref[b] = ext_ref[b, ext_len - HIST:ext_len, :]

    y = _ln(conv_ref[...] + convb_ref[...], lncg_ref[...], lncb_ref[...])
    a_out = (y * _sigmoid(y)).astype(BF16)

    u = z[:, 2 * D_CONV:2 * D_CONV + D_SG]
    v = _ln(z[:, 2 * D_CONV + D_SG:], lnvg_ref[...], lnvb_ref[...])
    sgv_ref[...] = v
    vb = v.astype(BF16)
    heads = [_dot(wsgbd_ref[hh], vb[:, hh * SG_HEAD_DIM:(hh + 1) * SG_HEAD_DIM]) for hh in range(SG_HEADS)]
    b_out = (u * (jnp.concatenate(heads, axis=1) + bsg_ref[...])).astype(BF16)

    x1 = (x + _dot(a_out, wout_ref[0:D_CONV, :].astype(BF16))
          + _dot(b_out, wout_ref[D_CONV:, :].astype(BF16)))

    hx = _rms(x1, gx_ref[...]).astype(BF16)
    q = _dot(hx, wxq_ref[...].astype(BF16)).astype(BF16)
    for b in range(n_batch):
        rs = slice(b * t_len, (b + 1) * t_len)
        att_ref[rs, :] = _attn_heads(q[rs, :], kmem_ref[b], v_ref[b], True)
    x2 = x1 + _dot(att_ref[...], wxo_ref[...].astype(BF16))
    x2_ref[...] = x2

    h3 = _rms(x2, gffn_ref[...]).astype(BF16)
    m = n_batch * t_len
    h3_ref[0:m, :] = _pack_bf16_pairs(h3.astype(F32))
    if h3_ref.shape[0] > m:
        h3_ref[m:, :] = jnp.zeros((h3_ref.shape[0] - m, D_MODEL // 2), jnp.uint32)
    rt, new_run = _route(_dot(h3, wr_ref[...]) + br_ref[...], run_in_ref[...], lower_ref[...])
    rt_ref[...] = rt
    cnt_ref[...] = new_run


def _trunk_sample(x, hist, run, p, n_batch, t_len):
    m = n_batch * t_len
    args = [x, hist, run, p["g_mix"], p["w_in"], p["conv_w"], p["conv_b"], p["ln_conv_g"], p["ln_conv_b"],
            p["ln_v_g"], p["ln_v_b"], p["w_sg_bd"], p["b_sg_rows_s"], p["w_out"], p["g_xattn"], p["w_xq"],
            p["k_s"], p["v_s"], p["w_xo"], p["g_ffn"], p["w_router"], p["b_router"], p["lower"]]
    return pl.pallas_call(
        functools.partial(_trunk_sample_kernel, n_batch, t_len),
        out_shape=(jax.ShapeDtypeStruct((m, D_MODEL), F32),
                   jax.ShapeDtypeStruct((-(-m // (SC_WORKERS * SUBLANES)) * SC_WORKERS * SUBLANES, D_MODEL // 2),
                                        jnp.uint32),
                   jax.ShapeDtypeStruct((SUBLANES, m), F32),
                   jax.ShapeDtypeStruct((n_batch, HIST, D_CONV), F32),
                   jax.ShapeDtypeStruct((m, D_SG), F32),
                   jax.ShapeDtypeStruct((1, LOGIT_LANES), F32)),
        scratch_shapes=[pltpu.VMEM((n_batch, HIST + t_len, D_CONV), F32),
                        pltpu.VMEM((m, D_CONV), F32),
                        pltpu.VMEM((m, D_MODEL), BF16)],
        compiler_params=pltpu.CompilerParams(vmem_limit_bytes=VMEM_LIMIT),
        name="trunk_sample",
    )(*args)


def _sc_worker_id():
    return lax.axis_index("s") * SC_CORES + lax.axis_index("c")


def _sc_chunk(per_w, max_chunk):
    assert per_w % SUBLANES == 0 and max_chunk <= LANES
    return max(c for c in range(SUBLANES, max_chunk + 1, SUBLANES) if per_w % c == 0)


def _sc_gather_rows(table, idx):
    n_rows, d = idx.shape[0], table.shape[1]
    per_w = n_rows // SC_WORKERS
    assert per_w * SC_WORKERS == n_rows and per_w % GATHER_ROWS == 0
    n_chunks = per_w // GATHER_ROWS
    lag = GATHER_BUFS // 2
    mesh = plsc.VectorSubcoreMesh(core_axis_name="c", subcore_axis_name="s")

    @functools.partial(
        pl.kernel, mesh=mesh,
        out_type=jax.ShapeDtypeStruct((n_rows, d), table.dtype),
        scratch_types=([pltpu.VMEM((per_w,), jnp.int32)]
                       + [pltpu.VMEM((GATHER_ROWS, d), table.dtype)] * GATHER_BUFS
                       + [pltpu.SemaphoreType.DMA] * (2 * GATHER_BUFS)),
    )
    def gather(table_hbm, idx_hbm, out_hbm, idx_all, *rest):
        rows = rest[:GATHER_BUFS]
        gsem = rest[GATHER_BUFS:2 * GATHER_BUFS]
        wsem = rest[2 * GATHER_BUFS:]
        base = _sc_worker_id() * per_w
        pltpu.sync_copy(idx_hbm.at[pl.ds(base, per_w)], idx_all)

        reads, writes = {}, {}
        for c in range(n_chunks + lag):
            if c < n_chunks:
                b = c % GATHER_BUFS
                if c >= GATHER_BUFS:
                    writes.pop(c - GATHER_BUFS).wait()
                reads[c] = pltpu.async_copy(
                    table_hbm.at[idx_all.at[pl.ds(c * GATHER_ROWS, GATHER_ROWS)]], rows[b], gsem[b])
            w = c - lag
            if w >= 0:
                b = w % GATHER_BUFS
                reads.pop(w).wait()
                writes[w] = pltpu.async_copy(
                    rows[b], out_hbm.at[pl.ds(base + w * GATHER_ROWS, GATHER_ROWS)], wsem[b])
        for w in sorted(writes):
            writes[w].wait()

    return gather(table, idx)


def _sc_scatter_rows2(tables, slots_a, slots_b, tags_a, tags_b, n_rows_out, max_chunk):
    d, dtype = tables[0].shape[1], tables[0].dtype
    plans = []
    for t in tables:
        per_w = t.shape[0] // SC_WORKERS
        assert per_w * SC_WORKERS == t.shape[0]
        chunk = _sc_chunk(per_w, max_chunk)
        plans.append((per_w, chunk, per_w // chunk))
    cmax = max(c for _, c, _ in plans)
    n_t = len(tables)
    mesh = plsc.VectorSubcoreMesh(core_axis_name="c", subcore_axis_name="s")

    nb = SCATTER_BUFS
    lag = nb // 3
    scratch = []
    for _, chunk, _ in plans:
        for _ in range(nb):
            scratch += [pltpu.VMEM((chunk,), jnp.int32), pltpu.VMEM((chunk,), jnp.int32)]
    scratch += [pltpu.VMEM((cmax, d), dtype)] * nb
    scratch += [pltpu.VMEM((cmax, TAG_WORDS), jnp.int32)] * (2 * nb)
    scratch += [pltpu.SemaphoreType.DMA] * (2 * nb)

    @functools.partial(pl.kernel, mesh=mesh,
                       out_type=(jax.ShapeDtypeStruct((n_rows_out, d), dtype),
                                 jax.ShapeDtypeStruct((n_rows_out, TAG_WORDS), jnp.int32)),
                       scratch_types=scratch)
    def scatter(*refs):
        tab_hbm = refs[0:n_t]
        sa_hbm = refs[n_t:2 * n_t]
        sb_hbm = refs[2 * n_t:3 * n_t]
        ta_hbm = refs[3 * n_t:4 * n_t]
        tb_hbm = refs[4 * n_t:5 * n_t]
        out_hbm, tag_hbm = refs[5 * n_t], refs[5 * n_t + 1]
        sc = refs[5 * n_t + 2:]
        idx_refs = sc[:2 * nb * n_t]
        rows = sc[2 * nb * n_t:2 * nb * n_t + nb]
        tagbufs = sc[2 * nb * n_t + nb:2 * nb * n_t + 3 * nb]
        lsem = sc[2 * nb * n_t + 3 * nb:2 * nb * n_t + 4 * nb]
        ssem = sc[2 * nb * n_t + 4 * nb:]
        wid = _sc_worker_id()

        work = []
        for t, (per_w, chunk, n_chunks) in enumerate(plans):
            for j in range(n_chunks):
                work.append((t, wid * per_w + j * chunk, chunk))

        def parts(k):
            t, off, chunk = work[k]
            b = k % nb
            ia, ib = idx_refs[2 * nb * t + 2 * b], idx_refs[2 * nb * t + 2 * b + 1]
            full = chunk == cmax
            rv = rows[b] if full else rows[b].at[pl.ds(0, chunk)]
            ta = tagbufs[2 * b] if full else tagbufs[2 * b].at[pl.ds(0, chunk)]
            tb = tagbufs[2 * b + 1] if full else tagbufs[2 * b + 1].at[pl.ds(0, chunk)]
            return t, off, chunk, b, ia, ib, rv, ta, tb

        def start_load(k):
            t, off, chunk, b, ia, ib, rv, ta, tb = parts(k)
            return (pltpu.async_copy(tab_hbm[t].at[pl.ds(off, chunk)], rv, lsem[b]),
                    pltpu.async_copy(sa_hbm[t].at[pl.ds(off, chunk)], ia, lsem[b]),
                    pltpu.async_copy(sb_hbm[t].at[pl.ds(off, chunk)], ib, lsem[b]),
                    pltpu.async_copy(ta_hbm[t].at[pl.ds(off, chunk)], ta, lsem[b]),
                    pltpu.async_copy(tb_hbm[t].at[pl.ds(off, chunk)], tb, lsem[b]))

        def start_scatter(k):
            t, off, chunk, b, ia, ib, rv, ta, tb = parts(k)
            return (pltpu.async_copy(rv, out_hbm.at[ia], ssem[b]), pltpu.async_copy(rv, out_hbm.at[ib], ssem[b]),
                    pltpu.async_copy(ta, tag_hbm.at[ia], ssem[b]), pltpu.async_copy(tb, tag_hbm.at[ib], ssem[b]))

        loads, scatters = {}, {}
        for k in range(len(work) + lag):
            if k < len(work):
                if k >= nb:
                    for c in scatters.pop(k - nb):
                        c.wait()
                loads[k] = start_load(k)
            w = k - lag
            if w >= 0:
                for c in loads.pop(w):
                    c.wait()
                scatters[w] = start_scatter(w)
        for w in sorted(scatters):
            for c in scatters[w]:
                c.wait()

    return scatter(*tables, *slots_a, *slots_b, *tags_a, *tags_b)


def _sc_scatter_back(ys, dest, n_rows_out):
    n_rows, d = ys.shape
    per_w = n_rows // SC_WORKERS
    assert per_w * SC_WORKERS == n_rows
    chunk = _sc_chunk(per_w, BACK_CHUNK)
    n_chunks = per_w // chunk
    nb = BACK_BUFS
    lag = nb // 2
    mesh = plsc.VectorSubcoreMesh(core_axis_name="c", subcore_axis_name="s")

    @functools.partial(
        pl.kernel, mesh=mesh,
        out_type=jax.ShapeDtypeStruct((n_rows_out, d), ys.dtype),
        scratch_types=([pltpu.VMEM((chunk,), jnp.int32)] * nb + [pltpu.VMEM((chunk, d), ys.dtype)] * nb
                       + [pltpu.SemaphoreType.DMA] * (2 * nb)),
    )
    def scatter_back(ys_hbm, dest_hbm, out_hbm, *rest):
        idx = rest[:nb]
        rows = rest[nb:2 * nb]
        lsem = rest[2 * nb:3 * nb]
        ssem = rest[3 * nb:]
        base = _sc_worker_id() * per_w

        loads, scatters = {}, {}
        for k in range(n_chunks + lag):
            if k < n_chunks:
                b = k % nb
                if k >= nb:
                    scatters.pop(k - nb).wait()
                off = base + k * chunk
                loads[k] = (pltpu.async_copy(ys_hbm.at[pl.ds(off, chunk)], rows[b], lsem[b]),
                            pltpu.async_copy(dest_hbm.at[pl.ds(off, chunk)], idx[b], lsem[b]))
            w = k - lag
            if w >= 0:
                b = w % nb
                for c in loads.pop(w):
                    c.wait()
                scatters[w] = pltpu.async_copy(rows[b], out_hbm.at[idx[b]], ssem[b])
        for w in sorted(scatters):
            scatters[w].wait()

    return scatter_back(ys, dest)


def _experts_kernel(first_ref, nblk_ref, cnt_ref, tot_ref, xs_hbm, wg_ref, wu_ref, wd_ref, ys_hbm,
                    xbuf, ybuf, wg_bf, wu_bf, wd_bf, in_sem, out_sem):
    e = pl.program_id(0)
    nb = nblk_ref[e]
    first = first_ref[e]
    cnt = cnt_ref[e]
    total = tot_ref[0]
    half = D_MODEL // 2

    def in_copy(gb):
        slot = lax.rem(gb, X_BUFS)
        return pltpu.make_async_copy(xs_hbm.at[pl.ds(gb * BM, BM)], xbuf.at[slot], in_sem.at[slot])

    def out_copy(gb):
        slot = lax.rem(gb, Y_BUFS)
        return pltpu.make_async_copy(ybuf.at[slot], ys_hbm.at[pl.ds(gb * BM, BM)], out_sem.at[slot])

    @pl.when(nb > 0)
    def _():
        @pl.when(first == 0)
        def _():
            for k in range(X_LOOKAHEAD):
                @pl.when(k < total)
                def _():
                    in_copy(k).start(priority=ROW_DMA_PRIORITY)

        wg_bf[...] = wg_ref[0].astype(BF16)
        wu_bf[...] = wu_ref[0].astype(BF16)
        wd_bf[...] = wd_ref[0].astype(BF16)

        def acquire(gb):
            @pl.when(gb + X_LOOKAHEAD < total)
            def _():
                in_copy(gb + X_LOOKAHEAD).start(priority=ROW_DMA_PRIORITY)

            in_copy(gb).wait()

            @pl.when(gb >= Y_BUFS)
            def _():
                out_copy(gb - Y_BUFS).wait()

        def ffn(gb, j):
            live = lax.broadcasted_iota(jnp.int32, (BM, half), 0) < cnt - j * BM
            lo, hi = _unpack_bf16_pairs(jnp.where(live, xbuf[lax.rem(gb, X_BUFS)], jnp.uint32(0)))
            g = _dot(lo, wg_bf[0:half, :]) + _dot(hi, wg_bf[half:, :])
            u = _dot(lo, wu_bf[0:half, :]) + _dot(hi, wu_bf[half:, :])
            hm = (g * _sigmoid(g) * u).astype(BF16)
            y = _dot(hm, wd_bf[...])
            ybuf[lax.rem(gb, Y_BUFS)] = _pack_bf16_pairs(y.astype(BF16).astype(F32))

        def block_pair(jp, carry):
            j0 = 2 * jp
            g0 = first + j0
            acquire(g0)
            acquire(g0 + 1)
            ffn(g0, j0)
            ffn(g0 + 1, j0 + 1)
            out_copy(g0).start(priority=ROW_DMA_PRIORITY)
            out_copy(g0 + 1).start(priority=ROW_DMA_PRIORITY)
            return carry

        lax.fori_loop(0, nb // 2, block_pair, 0)

        @pl.when(lax.rem(nb, 2) == 1)
        def _():
            gl = first + nb - 1
            acquire(gl)
            ffn(gl, nb - 1)
            out_copy(gl).start(priority=ROW_DMA_PRIORITY)

        @pl.when(first + nb == total)
        def _():
            for k in range(Y_BUFS):
                @pl.when(total - 1 - k >= 0)
                def _():
                    out_copy(total - 1 - k).wait()


def _experts(xs, n_rows_out, first_block, n_blocks_e, counts, w_eg, w_eu, w_ed):
    w_map = lambda e, fb, nb, ct, tot: (e, 0, 0)
    half = D_MODEL // 2
    total = jnp.sum(n_blocks_e).astype(jnp.int32).reshape(1)
    return pl.pallas_call(
        _experts_kernel,
        grid_spec=pltpu.PrefetchScalarGridSpec(
            num_scalar_prefetch=4,
            grid=(N_EXPERTS,),
            in_specs=[pl.BlockSpec(memory_space=pl.ANY),
                      pl.BlockSpec((1, D_MODEL, D_EXPERT), w_map),
                      pl.BlockSpec((1, D_MODEL, D_EXPERT), w_map),
                      pl.BlockSpec((1, D_EXPERT, D_MODEL), w_map)],
            out_specs=pl.BlockSpec(memory_space=pl.ANY),
            scratch_shapes=[pltpu.VMEM((X_BUFS, BM, half), jnp.uint32), pltpu.VMEM((Y_BUFS, BM, half), jnp.uint32),
                            pltpu.VMEM((D_MODEL, D_EXPERT), BF16), pltpu.VMEM((D_MODEL, D_EXPERT), BF16),
                            pltpu.VMEM((D_EXPERT, D_MODEL), BF16),
                            pltpu.SemaphoreType.DMA((X_BUFS,)), pltpu.SemaphoreType.DMA((Y_BUFS,))]),
        out_shape=jax.ShapeDtypeStruct((n_rows_out, half), jnp.uint32),
        compiler_params=pltpu.CompilerParams(dimension_semantics=("arbitrary",), vmem_limit_bytes=VMEM_LIMIT),
        name="experts",
    )(first_block, n_blocks_e, counts, total, xs, w_eg, w_eu, w_ed)


def _combine_kernel(x2_ref, y1_ref, y2_ref, rt_ref, g_ref, o_ref):
    rt = rt_ref[...]
    r = jnp.transpose(jnp.concatenate([rt, jnp.zeros((LANES - rt.shape[0], rt.shape[1]), F32)], axis=0))
    g1, g2 = r[:, 2:3], r[:, 3:4]
    half = D_MODEL // 2
    y1_lo, y1_hi = _unpack_bf16_pairs_f32(y1_ref[...])
    y2_lo, y2_hi = _unpack_bf16_pairs_f32(y2_ref[...])
    x_lo = x2_ref[:, 0:half] + g1 * y1_lo + g2 * y2_lo
    x_hi = x2_ref[:, half:] + g1 * y1_hi + g2 * y2_hi
    ms = (jnp.sum(x_lo * x_lo, axis=-1, keepdims=True) + jnp.sum(x_hi * x_hi, axis=-1, keepdims=True)) / D_MODEL
    inv = lax.rsqrt(ms + EPS)
    o_ref[:, 0:half] = x_lo * inv * g_ref[:, 0:half]
    o_ref[:, half:] = x_hi * inv * g_ref[:, half:]


def _combine(x2, yg, rt, g_final, tm, blk1, blk2):
    n = x2.shape[0]
    return pl.pallas_call(
        _combine_kernel,
        grid=(n // tm,),
        in_specs=[pl.BlockSpec((tm, D_MODEL), lambda i: (i, 0)),
                  pl.BlockSpec((tm, D_MODEL // 2), lambda i: (blk1 + i, 0)),
                  pl.BlockSpec((tm, D_MODEL // 2), lambda i: (blk2 + i, 0)),
                  pl.BlockSpec((SUBLANES, tm), lambda i: (0, i)),
                  pl.BlockSpec((1, D_MODEL), lambda i: (0, 0))],
        out_specs=pl.BlockSpec((tm, D_MODEL), lambda i: (i, 0)),
        out_shape=jax.ShapeDtypeStruct((n, D_MODEL), F32),
        compiler_params=pltpu.CompilerParams(dimension_semantics=("arbitrary",), vmem_limit_bytes=VMEM_LIMIT),
        name="combine",
    )(x2, yg, yg, rt, g_final)


def _scatter_back(ys, dest, n_rows_out):
    return _sc_scatter_back(ys, dest, n_rows_out)


def _scatter_rows2(tables, slots_a, slots_b, tags_a, tags_b, n_rows_out):
    return _sc_scatter_rows2(tables, slots_a, slots_b, tags_a, tags_b, n_rows_out, SCATTER_CHUNK)


def kernel(x_prompt, x_sample, mem_prompt, state_conv, cache_mem_k, cache_mem_v, g_mix, w_in, conv_w, conv_b, ln_conv_g, ln_conv_b, ln_v_g, ln_v_b, w_sg, b_sg, w_out, g_mem, w_mk, w_mv, g_xattn, w_xq, w_xo, g_ffn, w_router_group, b_router_group, w_router_expert, b_router_expert, w_expert_gate, w_expert_up, w_expert_down, g_final):
    assert x_prompt.shape[0] == 1 and g_mix.shape[0] == 1
    n_p = x_prompt.shape[1]
    n_batch, t_len = x_sample.shape[0], x_sample.shape[1]
    n_s = n_batch * t_len
    row = lambda a: a.reshape(1, -1)

    w_router = jnp.concatenate(
        [w_router_group[0], jnp.transpose(w_router_expert[0], (1, 0, 2)).reshape(D_MODEL, N_EXPERTS)], axis=1)
    w_router = jnp.pad(w_router, ((0, 0), (0, LOGIT_LANES - w_router.shape[1]))).astype(BF16)
    b_router = jnp.pad(jnp.concatenate([b_router_group[0], b_router_expert[0].reshape(-1)]),
                       (0, LOGIT_LANES - N_GROUPS - N_EXPERTS)).reshape(1, LOGIT_LANES)
    tril_t = jnp.tril(jnp.ones((t_len, t_len), bool))
    w_sg_t = jnp.where(tril_t, w_sg[0][:, :t_len, :t_len], 0.0)
    eye_b = jnp.eye(n_batch, dtype=F32)
    w_sg_bd = jnp.einsum("ab,hij->haibj", eye_b, w_sg_t).reshape(SG_HEADS, n_s, n_s).astype(BF16)
    p = {
        "g_mix": row(g_mix[0]), "w_in": w_in[0],
        "conv_w": jnp.pad(conv_w[0], ((0, 1), (0, 0))), "conv_b": row(conv_b[0]),
        "ln_conv_g": row(ln_conv_g[0]), "ln_conv_b": row(ln_conv_b[0]),
        "ln_v_g": row(ln_v_g[0]), "ln_v_b": row(ln_v_b[0]),
        "w_sg": w_sg[0],
        "b_sg_rows": jnp.repeat(b_sg[0].T, SG_HEAD_DIM, axis=1),
        "w_sg_bd": w_sg_bd,
        "b_sg_rows_s": jnp.tile(jnp.repeat(b_sg[0][:, :t_len].T, SG_HEAD_DIM, axis=1), (n_batch, 1)),
        "w_out": w_out[0], "g_xattn": row(g_xattn[0]),
        "w_xq": w_xq[0], "w_xo": w_xo[0], "g_ffn": row(g_ffn[0]),
        "w_router": w_router, "b_router": b_router,
        "lower": jnp.tril(jnp.ones((n_s, n_s), BF16), -1),
    }

    k_p, v_p = _memkv(mem_prompt[0], row(g_mem[0]), w_mk[0], w_mv[0])
    p["k"] = k_p.astype(BF16)
    p["v"] = v_p.astype(BF16)
    p["k_s"] = jnp.transpose(cache_mem_k[0].reshape(n_batch, N_MEM, D_MODEL), (0, 2, 1)).astype(BF16)
    p["v_s"] = cache_mem_v[0].reshape(n_batch, N_MEM, D_MODEL).astype(BF16)

    assert n_p % n_s == 0
    x2_p, h3_p, logits_p, hist_p = _trunk_prompt(x_prompt[0], p)
    rt_p, cnt_t = _router(logits_p, TM)
    cnt_p = cnt_t[:, 0].reshape(1, LOGIT_LANES)
    x2_s, h3_s, rt_s, hist_s, sgv_s, cnt = _trunk_sample(
        x_sample.reshape(n_s, D_MODEL), state_conv[0], cnt_p, p, n_batch, t_len)

    experts = jnp.arange(N_EXPERTS, dtype=jnp.int32)
    w_e = (w_expert_gate[0], w_expert_up[0], w_expert_down[0])

    def moe_pass(cnt, h3_tables, rts, n_real):
        n_tot = sum(n_real)
        n_slots = -(-(n_tot * 2) // BM) * BM + N_EXPERTS * BM
        counts = cnt[0, :N_EXPERTS].astype(jnp.int32)
        padded = (counts + BM - 1) // BM * BM
        pad_start = jnp.cumsum(padded) - padded

        def one(e_row, rank_row):
            e = e_row.astype(jnp.int32)
            start = jnp.sum(jnp.where(e[None, :] == experts[:, None], pad_start[:, None], 0), axis=0)
            return start + rank_row.astype(jnp.int32)

        slots = [(one(rt[0], rt[4]), one(rt[1], rt[5])) for rt in rts]
        sa, sb, ta, tb, spare0, dest0 = [], [], [], [], n_slots, 0
        for tab, (a, b), n in zip(h3_tables, slots, n_real):
            n_spare = tab.shape[0] - n
            spare = spare0 + jnp.arange(n_spare, dtype=jnp.int32)
            sa.append(jnp.concatenate([a, spare]))
            sb.append(jnp.concatenate([b, spare + n_spare]))
            spare0 += 2 * n_spare
            rows_t = jnp.arange(tab.shape[0], dtype=jnp.int32)[:, None]
            ta.append(jnp.broadcast_to(dest0 + rows_t, (tab.shape[0], TAG_WORDS)))
            tb.append(jnp.broadcast_to(dest0 + n + rows_t, (tab.shape[0], TAG_WORDS)))
            dest0 += 2 * n
        xs, tags = _scatter_rows2(tuple(h3_tables), tuple(sa), tuple(sb), tuple(ta), tuple(tb), spare0)
        ys = _experts(xs, n_slots, pad_start // BM, padded // BM, counts, *w_e)
        slot = jnp.arange(n_slots, dtype=jnp.int32)
        e_of = jnp.sum((jnp.cumsum(padded)[None, :] <= slot[:, None]).astype(jnp.int32), axis=1)
        e_of = jnp.minimum(e_of, N_EXPERTS - 1)
        is_e = e_of[:, None] == experts[None, :]
        live = slot - jnp.sum(jnp.where(is_e, pad_start, 0), axis=1) < jnp.sum(jnp.where(is_e, counts, 0), axis=1)
        dest = jnp.where(live, tags[:n_slots, 0], dest0 + slot)
        return _scatter_back(ys, dest, dest0 + n_slots)

    yg = moe_pass(cnt, [h3_p, h3_s], [rt_p, rt_s], [n_p, n_s])

    gf = row(g_final)
    y_p = _combine(x2_p, yg, rt_p, gf, TM_COMBINE, 0, n_p // TM_COMBINE)
    y_s = _combine(x2_s, yg, rt_s, gf, n_s, 2 * n_p // n_s, 2 * n_p // n_s + 1)

    return (y_p.reshape(1, n_p, D_MODEL),
            y_s.reshape(n_batch, t_len, D_MODEL),
            hist_p[HALO - HIST:].reshape(1, 1, HIST, D_CONV),
            hist_s.reshape(1, n_batch, HIST, D_CONV),
            k_p.reshape(1, 1, N_MEM, X_HEADS, X_HEAD_DIM),
            v_p.reshape(1, 1, N_MEM, X_HEADS, X_HEAD_DIM),
            sgv_s.reshape(1, n_batch, t_len, D_SG))
```

```python
import functools

import jax
import jax.numpy as jnp
from jax import lax
from jax.experimental import pallas as pl
from jax.experimental.pallas import tpu as pltpu
from jax.experimental.pallas import tpu_sc as plsc

D_MODEL = 1024
D_CONV = 512
D_SG = 512
CONV_WIDTH = 31
HIST = CONV_WIDTH - 1
SG_HEADS = 4
SG_HEAD_DIM = 128
SG_CHUNK = 128
N_MEM = 256
X_HEADS = 4
X_HEAD_DIM = 256
N_GROUPS = 4
EXPERTS_PER_GROUP = 8
N_EXPERTS = 32
D_EXPERT = 512
EPS = 1e-6

LANES = 128
SUBLANES = 8
SC_CORES = 2
SC_SUBCORES = 16
SC_WORKERS = SC_CORES * SC_SUBCORES
SC_LANES = 16
VMEM_LIMIT = 56 * 1024 * 1024

TM = 512
TM_COMBINE = 1024
ROUTE_ROWS = 2048
HALO = 32
SEG = TM // SUBLANES
SEG_HALO = 32
CONV_BLOCK = 16
CAST_ROWS = 64
BM = 256
X_LOOKAHEAD = 4
X_BUFS = X_LOOKAHEAD + 2
Y_BUFS = 4
ROW_DMA_PRIORITY = 1
GATHER_ROWS = 32
GATHER_BUFS = 6
SCATTER_CHUNK = 32
SCATTER_BUFS = 4
TAG_WORDS = 128
BACK_CHUNK = 56
BACK_BUFS = 4
LOGIT_LANES = 128

F32 = jnp.float32
BF16 = jnp.bfloat16


def _dot(a, b):
    return jnp.dot(a, b, preferred_element_type=F32)


def _rms(x, g):
    return x * lax.rsqrt(jnp.mean(x * x, axis=-1, keepdims=True) + EPS) * g


def _ln(x, g, b):
    mu = jnp.mean(x, axis=-1, keepdims=True)
    xc = x - mu
    var = jnp.mean(xc * xc, axis=-1, keepdims=True)
    return xc * lax.rsqrt(var + EPS) * g + b


def _sigmoid(x):
    return 1.0 / (1.0 + jnp.exp(-x))


def _pack_bf16_pairs(h):
    bits = lax.bitcast_convert_type(h, jnp.uint32)
    half = h.shape[1] // 2
    lo = lax.shift_right_logical(bits[:, :half], jnp.uint32(16))
    hi = bits[:, half:] & jnp.uint32(0xFFFF0000)
    return hi | lo


def _unpack_bf16_pairs_f32(p):
    lo = lax.bitcast_convert_type(lax.shift_left(p, jnp.uint32(16)), F32)
    hi = lax.bitcast_convert_type(p & jnp.uint32(0xFFFF0000), F32)
    return lo, hi


def _unpack_bf16_pairs(p):
    lo, hi = _unpack_bf16_pairs_f32(p)
    return lo.astype(BF16), hi.astype(BF16)


def _memkv_kernel(mem_ref, g_ref, wk_ref, wv_ref, k_ref, v_ref):
    m = _rms(mem_ref[...], g_ref[...]).astype(BF16)
    k_ref[...] = _dot(m, wk_ref[...].astype(BF16))
    v_ref[...] = _dot(m, wv_ref[...].astype(BF16))


def _memkv(mem, g_mem, w_mk, w_mv):
    return pl.pallas_call(
        _memkv_kernel,
        out_shape=(jax.ShapeDtypeStruct((N_MEM, D_MODEL), F32), jax.ShapeDtypeStruct((N_MEM, D_MODEL), F32)),
        compiler_params=pltpu.CompilerParams(vmem_limit_bytes=VMEM_LIMIT),
        name="memkv",
    )(mem, g_mem, w_mk, w_mv)


def _attn_heads(q, k, v, k_transposed):
    outs = []
    for h in range(X_HEADS):
        sl = slice(h * X_HEAD_DIM, (h + 1) * X_HEAD_DIM)
        if k_transposed:
            s = _dot(q[:, sl], k[sl, :])
        else:
            s = lax.dot_general(q[:, sl], k[:, sl], (((1,), (1,)), ((), ())), preferred_element_type=F32)
        s = s * (X_HEAD_DIM ** -0.5)
        s = s - jnp.max(s, axis=-1, keepdims=True)
        p = jnp.exp(s)
        p = p / jnp.sum(p, axis=-1, keepdims=True)
        outs.append(_dot(p.astype(BF16), v[:, sl]).astype(BF16))
    return jnp.concatenate(outs, axis=1)


def _route(logits, run, strict_lower):
    m = logits.shape[0]
    r = strict_lower.shape[0]
    lane = lax.broadcasted_iota(jnp.int32, (m, LOGIT_LANES), 1).astype(F32)
    neg = jnp.float32(-jnp.inf)
    big = jnp.float32(LOGIT_LANES)

    def first_argmax(vals):
        mx = jnp.max(vals, axis=-1, keepdims=True)
        idx = jnp.min(jnp.where(vals == mx, lane, big), axis=-1, keepdims=True)
        return mx, idx

    lg = jnp.where(lane < N_GROUPS, logits, neg)
    g_max, g_idx = first_argmax(lg)
    g_w = 1.0 / jnp.sum(jnp.exp(lg - g_max), axis=-1, keepdims=True)

    lo = N_GROUPS + g_idx * EXPERTS_PER_GROUP
    le = jnp.where((lane >= lo) & (lane < lo + EXPERTS_PER_GROUP), logits, neg)
    v1, i1 = first_argmax(le)
    v2, i2 = first_argmax(jnp.where(lane == i1, neg, le))
    t = jnp.exp(v2 - v1)
    gate1 = g_w / (1.0 + t)
    gate2 = g_w * t / (1.0 + t)
    e1 = i1 - N_GROUPS
    e2 = i2 - N_GROUPS

    oh1 = (lane == e1).astype(F32)
    oh2 = (lane == e2).astype(F32)
    oh = oh1 + oh2
    befores = []
    for r0 in range(0, m, r):
        oh_r = oh[r0:r0 + r, :]
        befores.append(_dot(strict_lower, oh_r.astype(BF16)) + run)
        run = run + jnp.sum(oh_r, axis=0, keepdims=True)
    before = befores[0] if len(befores) == 1 else jnp.concatenate(befores, axis=0)
    rank1 = jnp.sum(before * oh1, axis=-1, keepdims=True)
    rank2 = jnp.sum(before * oh2, axis=-1, keepdims=True)
    new_run = run

    rinfo = jnp.where(lane == 0, e1,
            jnp.where(lane == 1, e2,
            jnp.where(lane == 2, gate1,
            jnp.where(lane == 3, gate2,
            jnp.where(lane == 4, rank1,
            jnp.where(lane == 5, rank2, 0.0))))))
    return jnp.transpose(rinfo)[0:SUBLANES, :], new_run


def _conv_segments(a, w_ref, seg_ref, tail_ref, yseg_ref, conv_ref):
    sub = lax.broadcasted_iota(jnp.int32, (SUBLANES, LANES), 0)
    for lt in range(D_CONV // LANES):
        ls = slice(lt * LANES, (lt + 1) * LANES)
        for t0 in range(0, TM, SUBLANES):
            s, m = divmod(t0, SEG)
            seg_ref[lt, pl.ds((SEG_HALO + m) * SUBLANES + s, SUBLANES, stride=SUBLANES), :] = a[t0:t0 + SUBLANES, ls]
        for j in range(SEG_HALO):
            cur = seg_ref[lt, (SEG + j) * SUBLANES:(SEG + j + 1) * SUBLANES, :]
            prev = tail_ref[lt, j * SUBLANES:(j + 1) * SUBLANES, :]
            seg_ref[lt, j * SUBLANES:(j + 1) * SUBLANES, :] = jnp.where(
                sub == 0, pltpu.roll(prev, 1, axis=0), pltpu.roll(cur, 1, axis=0))
            tail_ref[lt, j * SUBLANES:(j + 1) * SUBLANES, :] = cur
        for m0 in range(0, SEG, CONV_BLOCK):
            acc = [jnp.zeros((SUBLANES, LANES), F32) for _ in range(CONV_BLOCK)]
            for idx in range(m0 - HIST, m0 + CONV_BLOCK):
                b = seg_ref[lt, (SEG_HALO + idx) * SUBLANES:(SEG_HALO + idx + 1) * SUBLANES, :]
                for m in range(max(m0, idx), min(m0 + CONV_BLOCK, idx + CONV_WIDTH)):
                    k = idx - m + HIST
                    acc[m - m0] = acc[m - m0] + b * w_ref[k:k + 1, ls]
            for m in range(m0, m0 + CONV_BLOCK):
                yseg_ref[lt, m * SUBLANES:(m + 1) * SUBLANES, :] = acc[m - m0]
        for t0 in range(0, TM, SUBLANES):
            s, m = divmod(t0, SEG)
            conv_ref[t0:t0 + SUBLANES, ls] = yseg_ref[lt, pl.ds(m * SUBLANES + s, SUBLANES, stride=SUBLANES), :]


def _cast_rows(src_ref, dst_ref):
    rows = src_ref.shape[0]

    def body(c, carry):
        r0 = pl.multiple_of(c * CAST_ROWS, CAST_ROWS)
        dst_ref[pl.ds(r0, CAST_ROWS), :] = src_ref[pl.ds(r0, CAST_ROWS), :].astype(BF16)
        return carry

    lax.fori_loop(0, rows // CAST_ROWS, body, 0)


def _trunk_prompt_kernel(x_ref, gmix_ref, win32_ref, convw_ref, convb_ref, lncg_ref, lncb_ref, lnvg_ref, lnvb_ref,
                         wsg_ref, bsg_ref, wout32_ref, gx_ref, wxq32_ref, kmem_ref, v_ref, wxo32_ref, gffn_ref, wr_ref,
                         br_ref,
                         x2_ref, h3_ref, logit_ref, hist_ref,
                         seg_ref, tail_ref, yseg_ref, conv_ref, win_ref, wout_ref, wxq_ref, wxo_ref):
    i = pl.program_id(0)

    @pl.when(i == 0)
    def _():
        tail_ref[...] = jnp.zeros(tail_ref.shape, F32)
        _cast_rows(win32_ref, win_ref)
        _cast_rows(wout32_ref, wout_ref)
        _cast_rows(wxq32_ref, wxq_ref)
        _cast_rows(wxo32_ref, wxo_ref)

    x = x_ref[...]
    h = _rms(x, gmix_ref[...]).astype(BF16)

    a_in = _dot(h, win_ref[:, 0:D_CONV])
    a_gate = _dot(h, win_ref[:, D_CONV:2 * D_CONV])
    a = a_in * _sigmoid(a_gate)
    hist_ref[...] = a[TM - HALO:, :]
    _conv_segments(a, convw_ref, seg_ref, tail_ref, yseg_ref, conv_ref)

    y = _ln(conv_ref[...] + convb_ref[...], lncg_ref[...], lncb_ref[...])
    a_out = (y * _sigmoid(y)).astype(BF16)

    u = _dot(h, win_ref[:, 2 * D_CONV:2 * D_CONV + D_SG])
    v = _ln(_dot(h, win_ref[:, 2 * D_CONV + D_SG:]), lnvg_ref[...], lnvb_ref[...]).astype(BF16)
    ri = lax.broadcasted_iota(jnp.int32, (SG_CHUNK, SG_CHUNK), 0)
    ci = lax.broadcasted_iota(jnp.int32, (SG_CHUNK, SG_CHUNK), 1)
    w_tril = [jnp.where(ci <= ri, wsg_ref[hh], 0.0).astype(BF16) for hh in range(SG_HEADS)]
    gate_rows = []
    for c in range(TM // SG_CHUNK):
        rs = slice(c * SG_CHUNK, (c + 1) * SG_CHUNK)
        heads = [_dot(w_tril[hh], v[rs, hh * SG_HEAD_DIM:(hh + 1) * SG_HEAD_DIM]) for hh in range(SG_HEADS)]
        gate_rows.append(jnp.concatenate(heads, axis=1) + bsg_ref[...])
    b_out = (u * jnp.concatenate(gate_rows, axis=0)).astype(BF16)

    x1 = x + _dot(a_out, wout_ref[0:D_CONV, :]) + _dot(b_out, wout_ref[D_CONV:, :])

    hx = _rms(x1, gx_ref[...]).astype(BF16)
    q = _dot(hx, wxq_ref[...]).astype(BF16)
    x2 = x1 + _dot(_attn_heads(q, kmem_ref[...], v_ref[...], False), wxo_ref[...])
    x2_ref[...] = x2

    h3 = _rms(x2, gffn_ref[...]).astype(BF16)
    h3_ref[...] = _pack_bf16_pairs(h3.astype(F32))
    logit_ref[...] = _dot(h3, wr_ref[...]) + br_ref[...]


def _router_kernel(logit_ref, upper_ref, ones_ref, rt_ref, cnt_ref, run_ref):
    @pl.when(pl.program_id(0) == 0)
    def _():
        run_ref[...] = jnp.zeros(run_ref.shape, F32)

    n = ROUTE_ROWS
    lt = jnp.transpose(logit_ref[...])
    neg = jnp.float32(-jnp.inf)
    big = jnp.float32(LOGIT_LANES)

    def first_argmax(vals, rows):
        mx = jnp.max(vals, axis=0, keepdims=True)
        idx = jnp.min(jnp.where(vals == mx, rows, big), axis=0, keepdims=True)
        return mx, idx

    row8 = lax.broadcasted_iota(jnp.int32, (SUBLANES, n), 0).astype(F32)
    lg = jnp.where(row8 < N_GROUPS, lt[0:SUBLANES, :], neg)
    g_max, g_idx = first_argmax(lg, row8)
    g_w = 1.0 / jnp.sum(jnp.exp(lg - g_max), axis=0, keepdims=True)

    n_rows = N_GROUPS + N_EXPERTS + (-(N_GROUPS + N_EXPERTS)) % SUBLANES
    rows = lax.broadcasted_iota(jnp.int32, (n_rows, n), 0).astype(F32)
    lo = N_GROUPS + g_idx * EXPERTS_PER_GROUP
    le = jnp.where((rows >= lo) & (rows < lo + EXPERTS_PER_GROUP), lt[0:n_rows, :], neg)
    v1, i1 = first_argmax(le, rows)
    v2, i2 = first_argmax(jnp.where(rows == i1, neg, le), rows)
    t = jnp.exp(v2 - v1)
    gate1 = g_w / (1.0 + t)
    gate2 = g_w * t / (1.0 + t)
    e1 = i1 - N_GROUPS
    e2 = i2 - N_GROUPS

    erow = lax.broadcasted_iota(jnp.int32, (LOGIT_LANES, n), 0).astype(F32)
    oh1 = (erow == e1).astype(F32)
    oh2 = (erow == e2).astype(F32)
    oh = (oh1 + oh2).astype(BF16)
    r = upper_ref.shape[0]
    run = run_ref[...]
    rank1, rank2 = [], []
    for c0 in range(0, n, r):
        cs = slice(c0, c0 + r)
        before = _dot(oh[:, cs], upper_ref[...]) + run
        rank1.append(jnp.sum(before * oh1[:, cs], axis=0, keepdims=True))
        rank2.append(jnp.sum(before * oh2[:, cs], axis=0, keepdims=True))
        run = run + _dot(oh[:, cs], ones_ref[...])
    run_ref[...] = run
    cnt_ref[...] = run[:, 0:LANES]

    sub = lax.broadcasted_iota(jnp.int32, (SUBLANES, n), 0)
    vals = (e1, e2, gate1, gate2, jnp.concatenate(rank1, axis=1), jnp.concatenate(rank2, axis=1))
    rt = jnp.zeros((SUBLANES, n), F32)
    for k, v in enumerate(vals):
        rt = jnp.where(sub == k, v, rt)
    rt_ref[...] = rt


def _router(logits, rank_block):
    n = logits.shape[0]
    assert n % ROUTE_ROWS == 0 and ROUTE_ROWS % rank_block == 0
    upper = jnp.triu(jnp.ones((rank_block, rank_block), BF16), 1)
    ones = jnp.ones((rank_block, rank_block), BF16)
    return pl.pallas_call(
        _router_kernel,
        grid=(n // ROUTE_ROWS,),
        in_specs=[pl.BlockSpec((ROUTE_ROWS, LOGIT_LANES), lambda i: (i, 0)),
                  pl.BlockSpec(upper.shape, lambda i: (0, 0)),
                  pl.BlockSpec(ones.shape, lambda i: (0, 0))],
        out_specs=(pl.BlockSpec((SUBLANES, ROUTE_ROWS), lambda i: (0, i)),
                   pl.BlockSpec((LOGIT_LANES, LANES), lambda i: (0, 0))),
        out_shape=(jax.ShapeDtypeStruct((SUBLANES, n), F32), jax.ShapeDtypeStruct((LOGIT_LANES, LANES), F32)),
        scratch_shapes=[pltpu.VMEM((LOGIT_LANES, rank_block), F32)],
        compiler_params=pltpu.CompilerParams(dimension_semantics=("arbitrary",), vmem_limit_bytes=VMEM_LIMIT),
        name="router",
    )(logits, upper, ones)


def _const_spec(shape):
    nd = len(shape)
    return pl.BlockSpec(shape, lambda i: (0,) * nd, pipeline_mode=pl.Buffered(1))


def _trunk_prompt(x, p):
    n = x.shape[0]
    assert n % TM == 0
    row = lambda w: pl.BlockSpec((TM, w), lambda i: (i, 0))
    consts = [p["g_mix"], p["w_in"], p["conv_w"], p["conv_b"], p["ln_conv_g"], p["ln_conv_b"], p["ln_v_g"],
              p["ln_v_b"], p["w_sg"], p["b_sg_rows"], p["w_out"], p["g_xattn"], p["w_xq"], p["k"], p["v"],
              p["w_xo"], p["g_ffn"], p["w_router"], p["b_router"]]
    return pl.pallas_call(
        _trunk_prompt_kernel,
        grid=(n // TM,),
        in_specs=[row(D_MODEL)] + [_const_spec(c.shape) for c in consts],
        out_specs=(row(D_MODEL), row(D_MODEL // 2), row(LOGIT_LANES),
                   pl.BlockSpec((HALO, D_CONV), lambda i: (0, 0))),
        out_shape=(jax.ShapeDtypeStruct((n, D_MODEL), F32),
                   jax.ShapeDtypeStruct((n, D_MODEL // 2), jnp.uint32),
                   jax.ShapeDtypeStruct((n, LOGIT_LANES), F32),
                   jax.ShapeDtypeStruct((HALO, D_CONV), F32)),
        scratch_shapes=[pltpu.VMEM((D_CONV // LANES, (SEG_HALO + SEG) * SUBLANES, LANES), F32),
                        pltpu.VMEM((D_CONV // LANES, SEG_HALO * SUBLANES, LANES), F32),
                        pltpu.VMEM((D_CONV // LANES, TM, LANES), F32),
                        pltpu.VMEM((TM, D_CONV), F32),
                        pltpu.VMEM(p["w_in"].shape, BF16), pltpu.VMEM(p["w_out"].shape, BF16),
                        pltpu.VMEM(p["w_xq"].shape, BF16), pltpu.VMEM(p["w_xo"].shape, BF16)],
        compiler_params=pltpu.CompilerParams(dimension_semantics=("arbitrary",), vmem_limit_bytes=VMEM_LIMIT),
        name="trunk_prompt",
    )(x, *consts)


def _trunk_sample_kernel(n_batch, t_len,
                         x_ref, hist_in_ref, run_in_ref, gmix_ref, win_ref, convw_ref, convb_ref, lncg_ref, lncb_ref,
                         lnvg_ref, lnvb_ref, wsgbd_ref, bsg_ref, wout_ref, gx_ref, wxq_ref, kmem_ref, v_ref, wxo_ref,
                         gffn_ref, wr_ref, br_ref, lower_ref,
                         x2_ref, h3_ref, rt_ref, hist_ref, sgv_ref, cnt_ref,
                         ext_ref, conv_ref, att_ref):
    x = x_ref[...]
    h = _rms(x, gmix_ref[...]).astype(BF16)
    z = _dot(h, win_ref[...].astype(BF16))
    a = z[:, 0:D_CONV] * _sigmoid(z[:, D_CONV:2 * D_CONV])
    ext_len = HIST + t_len
    for b in range(n_batch):
        ext_ref[b, 0:HIST, :] = hist_in_ref[b]
        ext_ref[b, HIST:ext_len, :] = a[b * t_len:(b + 1) * t_len, :]
    for b in range(n_batch):
        acc = jnp.zeros((t_len, D_CONV), F32)
        for k in range(CONV_WIDTH):
            acc = acc + ext_ref[b, k:k + t_len, :] * convw_ref[k:k + 1, :]
        conv_ref[b * t_len:(b + 1) * t_len, :] = acc
        hist_ref[b] = ext_ref[b, ext_len - HIST:ext_len, :]

    y = _ln(conv_ref[...] + convb_ref[...], lncg_ref[...], lncb_ref[...])
    a_out = (y * _sigmoid(y)).astype(BF16)

    u = z[:, 2 * D_CONV:2 * D_CONV + D_SG]
    v = _ln(z[:, 2 * D_CONV + D_SG:], lnvg_ref[...], lnvb_ref[...])
    sgv_ref[...] = v
    vb = v.astype(BF16)
    heads = [_dot(wsgbd_ref[hh], vb[:, hh * SG_HEAD_DIM:(hh + 1) * SG_HEAD_DIM]) for hh in range(SG_HEADS)]
    b_out = (u * (jnp.concatenate(heads, axis=1) + bsg_ref[...])).astype(BF16)

    x1 = (x + _dot(a_out, wout_ref[0:D_CONV, :].astype(BF16))
          + _dot(b_out, wout_ref[D_CONV:, :].astype(BF16)))

    hx = _rms(x1, gx_ref[...]).astype(BF16)
    q = _dot(hx, wxq_ref[...].astype(BF16)).astype(BF16)
    for b in range(n_batch):
        rs = slice(b * t_len, (b + 1) * t_len)
        att_ref[rs, :] = _attn_heads(q[rs, :], kmem_ref[b], v_ref[b], True)
    x2 = x1 + _dot(att_ref[...], wxo_ref[...].astype(BF16))
    x2_ref[...] = x2

    h3 = _rms(x2, gffn_ref[...]).astype(BF16)
    m = n_batch * t_len
    h3_ref[0:m, :] = _pack_bf16_pairs(h3.astype(F32))
    if h3_ref.shape[0] > m:
        h3_ref[m:, :] = jnp.zeros((h3_ref.shape[0] - m, D_MODEL // 2), jnp.uint32)
    rt, new_run = _route(_dot(h3, wr_ref[...]) + br_ref[...], run_in_ref[...], lower_ref[...])
    rt_ref[...] = rt
    cnt_ref[...] = new_run


def _trunk_sample(x, hist, run, p, n_batch, t_len):
    m = n_batch * t_len
    args = [x, hist, run, p["g_mix"], p["w_in"], p["conv_w"], p["conv_b"], p["ln_conv_g"], p["ln_conv_b"],
            p["ln_v_g"], p["ln_v_b"], p["w_sg_bd"], p["b_sg_rows_s"], p["w_out"], p["g_xattn"], p["w_xq"],
            p["k_s"], p["v_s"], p["w_xo"], p["g_ffn"], p["w_router"], p["b_router"], p["lower"]]
    return pl.pallas_call(
        functools.partial(_trunk_sample_kernel, n_batch, t_len),
        out_shape=(jax.ShapeDtypeStruct((m, D_MODEL), F32),
                   jax.ShapeDtypeStruct((-(-m // (SC_WORKERS * SUBLANES)) * SC_WORKERS * SUBLANES, D_MODEL // 2),
                                        jnp.uint32),
                   jax.ShapeDtypeStruct((SUBLANES, m), F32),
                   jax.ShapeDtypeStruct((n_batch, HIST, D_CONV), F32),
                   jax.ShapeDtypeStruct((m, D_SG), F32),
                   jax.ShapeDtypeStruct((1, LOGIT_LANES), F32)),
        scratch_shapes=[pltpu.VMEM((n_batch, HIST + t_len, D_CONV), F32),
                        pltpu.VMEM((m, D_CONV), F32),
                        pltpu.VMEM((m, D_MODEL), BF16)],
        compiler_params=pltpu.CompilerParams(vmem_limit_bytes=VMEM_LIMIT),
        name="trunk_sample",
    )(*args)


def _sc_worker_id():
    return lax.axis_index("s") * SC_CORES + lax.axis_index("c")


def _sc_chunk(per_w, max_chunk):
    assert per_w % SUBLANES == 0 and max_chunk <= LANES
    return max(c for c in range(SUBLANES, max_chunk + 1, SUBLANES) if per_w % c == 0)


def _sc_gather_rows(table, idx):
    n_rows, d = idx.shape[0], table.shape[1]
    per_w = n_rows // SC_WORKERS
    assert per_w * SC_WORKERS == n_rows and per_w % GATHER_ROWS == 0
    n_chunks = per_w // GATHER_ROWS
    lag = GATHER_BUFS // 2
    mesh = plsc.VectorSubcoreMesh(core_axis_name="c", subcore_axis_name="s")

    @functools.partial(
        pl.kernel, mesh=mesh,
        out_type=jax.ShapeDtypeStruct((n_rows, d), table.dtype),
        scratch_types=([pltpu.VMEM((per_w,), jnp.int32)]
                       + [pltpu.VMEM((GATHER_ROWS, d), table.dtype)] * GATHER_BUFS
                       + [pltpu.SemaphoreType.DMA] * (2 * GATHER_BUFS)),
    )
    def gather(table_hbm, idx_hbm, out_hbm, idx_all, *rest):
        rows = rest[:GATHER_BUFS]
        gsem = rest[GATHER_BUFS:2 * GATHER_BUFS]
        wsem = rest[2 * GATHER_BUFS:]
        base = _sc_worker_id() * per_w
        pltpu.sync_copy(idx_hbm.at[pl.ds(base, per_w)], idx_all)

        reads, writes = {}, {}
        for c in range(n_chunks + lag):
            if c < n_chunks:
                b = c % GATHER_BUFS
                if c >= GATHER_BUFS:
                    writes.pop(c - GATHER_BUFS).wait()
                reads[c] = pltpu.async_copy(
                    table_hbm.at[idx_all.at[pl.ds(c * GATHER_ROWS, GATHER_ROWS)]], rows[b], gsem[b])
            w = c - lag
            if w >= 0:
                b = w % GATHER_BUFS
                reads.pop(w).wait()
                writes[w] = pltpu.async_copy(
                    rows[b], out_hbm.at[pl.ds(base + w * GATHER_ROWS, GATHER_ROWS)], wsem[b])
        for w in sorted(writes):
            writes[w].wait()

    return gather(table, idx)


def _sc_scatter_rows2(tables, slots_a, slots_b, tag_bases, n_rows_out, max_chunk):
    d, dtype = tables[0].shape[1], tables[0].dtype
    plans = []
    for t in tables:
        per_w = t.shape[0] // SC_WORKERS
        assert per_w * SC_WORKERS == t.shape[0]
        chunk = _sc_chunk(per_w, max_chunk)
        plans.append((per_w, chunk, per_w // chunk))
    cmax = max(c for _, c, _ in plans)
    n_t = len(tables)
    mesh = plsc.VectorSubcoreMesh(core_axis_name="c", subcore_axis_name="s")

    nb = SCATTER_BUFS
    lag = nb // 3
    scratch = []
    for _, chunk, _ in plans:
        for _ in range(nb):
            scratch += [pltpu.VMEM((chunk,), jnp.int32), pltpu.VMEM((chunk,), jnp.int32)]
    scratch += [pltpu.VMEM((cmax, d), dtype)] * nb
    scratch += [pltpu.VMEM((cmax, TAG_WORDS), jnp.int32)] * (2 * nb)
    scratch += [pltpu.SemaphoreType.DMA] * (2 * nb)

    @functools.partial(pl.kernel, mesh=mesh,
                       out_type=(jax.ShapeDtypeStruct((n_rows_out, d), dtype),
                                 jax.ShapeDtypeStruct((n_rows_out, TAG_WORDS), jnp.int32)),
                       scratch_types=scratch)
    def scatter(*refs):
        tab_hbm = refs[0:n_t]
        sa_hbm = refs[n_t:2 * n_t]
        sb_hbm = refs[2 * n_t:3 * n_t]
        out_hbm, tag_hbm = refs[3 * n_t], refs[3 * n_t + 1]
        sc = refs[3 * n_t + 2:]
        idx_refs = sc[:2 * nb * n_t]
        rows = sc[2 * nb * n_t:2 * nb * n_t + nb]
        tagbufs = sc[2 * nb * n_t + nb:2 * nb * n_t + 3 * nb]
        lsem = sc[2 * nb * n_t + 3 * nb:2 * nb * n_t + 4 * nb]
        ssem = sc[2 * nb * n_t + 4 * nb:]
        wid = _sc_worker_id()

        work = []
        for t, (per_w, chunk, n_chunks) in enumerate(plans):
            for j in range(n_chunks):
                work.append((t, wid * per_w + j * chunk, chunk))

        def parts(k):
            t, off, chunk = work[k]
            b = k % nb
            ia, ib = idx_refs[2 * nb * t + 2 * b], idx_refs[2 * nb * t + 2 * b + 1]
            full = chunk == cmax
            rv = rows[b] if full else rows[b].at[pl.ds(0, chunk)]
            ta = tagbufs[2 * b] if full else tagbufs[2 * b].at[pl.ds(0, chunk)]
            tb = tagbufs[2 * b + 1] if full else tagbufs[2 * b + 1].at[pl.ds(0, chunk)]
            return t, off, chunk, b, ia, ib, rv, ta, tb

        def start_load(k):
            t, off, chunk, b, ia, ib, rv, ta, tb = parts(k)
            return (pltpu.async_copy(tab_hbm[t].at[pl.ds(off, chunk)], rv, lsem[b]),
                    pltpu.async_copy(sa_hbm[t].at[pl.ds(off, chunk)], ia, lsem[b]),
                    pltpu.async_copy(sb_hbm[t].at[pl.ds(off, chunk)], ib, lsem[b]))

        def start_scatter(k):
            t, off, chunk, b, ia, ib, rv, ta, tb = parts(k)
            base_a, base_b = tag_bases[t]
            for r in range(chunk):
                row_id = (off + r).astype(jnp.int32)
                tagbufs[2 * b][r, pl.ds(0, SC_LANES)] = jnp.zeros((SC_LANES,), jnp.int32) + (base_a + row_id)
                tagbufs[2 * b + 1][r, pl.ds(0, SC_LANES)] = jnp.zeros((SC_LANES,), jnp.int32) + (base_b + row_id)
            return (pltpu.async_copy(rv, out_hbm.at[ia], ssem[b]), pltpu.async_copy(rv, out_hbm.at[ib], ssem[b]),
                    pltpu.async_copy(ta, tag_hbm.at[ia], ssem[b]), pltpu.async_copy(tb, tag_hbm.at[ib], ssem[b]))

        loads, scatters = {}, {}
        for k in range(len(work) + lag):
            if k < len(work):
                if k >= nb:
                    for c in scatters.pop(k - nb):
                        c.wait()
                loads[k] = start_load(k)
            w = k - lag
            if w >= 0:
                for c in loads.pop(w):
                    c.wait()
                scatters[w] = start_scatter(w)
        for w in sorted(scatters):
            for c in scatters[w]:
                c.wait()

    return scatter(*tables, *slots_a, *slots_b)


def _sc_scatter_back(ys, dest, n_rows_out):
    n_rows, d = ys.shape
    per_w = n_rows // SC_WORKERS
    assert per_w * SC_WORKERS == n_rows
    chunk = _sc_chunk(per_w, BACK_CHUNK)
    n_chunks = per_w // chunk
    nb = BACK_BUFS
    lag = nb // 2
    mesh = plsc.VectorSubcoreMesh(core_axis_name="c", subcore_axis_name="s")

    @functools.partial(
        pl.kernel, mesh=mesh,
        out_type=jax.ShapeDtypeStruct((n_rows_out, d), ys.dtype),
        scratch_types=([pltpu.VMEM((chunk,), jnp.int32)] * nb + [pltpu.VMEM((chunk, d), ys.dtype)] * nb
                       + [pltpu.SemaphoreType.DMA] * (2 * nb)),
    )
    def scatter_back(ys_hbm, dest_hbm, out_hbm, *rest):
        idx = rest[:nb]
        rows = rest[nb:2 * nb]
        lsem = rest[2 * nb:3 * nb]
        ssem = rest[3 * nb:]
        base = _sc_worker_id() * per_w

        loads, scatters = {}, {}
        for k in range(n_chunks + lag):
            if k < n_chunks:
                b = k % nb
                if k >= nb:
                    scatters.pop(k - nb).wait()
                off = base + k * chunk
                loads[k] = (pltpu.async_copy(ys_hbm.at[pl.ds(off, chunk)], rows[b], lsem[b]),
                            pltpu.async_copy(dest_hbm.at[pl.ds(off, chunk)], idx[b], lsem[b]))
            w = k - lag
            if w >= 0:
                b = w % nb
                for c in loads.pop(w):
                    c.wait()
                scatters[w] = pltpu.async_copy(rows[b], out_hbm.at[idx[b]], ssem[b])
        for w in sorted(scatters):
            scatters[w].wait()

    return scatter_back(ys, dest)


def _experts_kernel(first_ref, nblk_ref, cnt_ref, tot_ref, xs_hbm, wg_ref, wu_ref, wd_ref, ys_hbm,
                    xbuf, ybuf, wg_bf, wu_bf, wd_bf, in_sem, out_sem):
    e = pl.program_id(0)
    nb = nblk_ref[e]
    first = first_ref[e]
    cnt = cnt_ref[e]
    total = tot_ref[0]
    half = D_MODEL // 2

    def in_copy(gb):
        slot = lax.rem(gb, X_BUFS)
        return pltpu.make_async_copy(xs_hbm.at[pl.ds(gb * BM, BM)], xbuf.at[slot], in_sem.at[slot])

    def out_copy(gb):
        slot = lax.rem(gb, Y_BUFS)
        return pltpu.make_async_copy(ybuf.at[slot], ys_hbm.at[pl.ds(gb * BM, BM)], out_sem.at[slot])

    @pl.when(nb > 0)
    def _():
        @pl.when(first == 0)
        def _():
            for k in range(X_LOOKAHEAD):
                @pl.when(k < total)
                def _():
                    in_copy(k).start(priority=ROW_DMA_PRIORITY)

        wg_bf[...] = wg_ref[0].astype(BF16)
        wu_bf[...] = wu_ref[0].astype(BF16)
        wd_bf[...] = wd_ref[0].astype(BF16)

        def acquire(gb):
            @pl.when(gb + X_LOOKAHEAD < total)
            def _():
                in_copy(gb + X_LOOKAHEAD).start(priority=ROW_DMA_PRIORITY)

            in_copy(gb).wait()

            @pl.when(gb >= Y_BUFS)
            def _():
                out_copy(gb - Y_BUFS).wait()

        def ffn(gb, j):
            live = lax.broadcasted_iota(jnp.int32, (BM, half), 0) < cnt - j * BM
            lo, hi = _unpack_bf16_pairs(jnp.where(live, xbuf[lax.rem(gb, X_BUFS)], jnp.uint32(0)))
            g = _dot(lo, wg_bf[0:half, :]) + _dot(hi, wg_bf[half:, :])
            u = _dot(lo, wu_bf[0:half, :]) + _dot(hi, wu_bf[half:, :])
            hm = (g * _sigmoid(g) * u).astype(BF16)
            y = _dot(hm, wd_bf[...])
            ybuf[lax.rem(gb, Y_BUFS)] = _pack_bf16_pairs(y.astype(BF16).astype(F32))

        def block_pair(jp, carry):
            j0 = 2 * jp
            g0 = first + j0
            acquire(g0)
            acquire(g0 + 1)
            ffn(g0, j0)
            ffn(g0 + 1, j0 + 1)
            out_copy(g0).start(priority=ROW_DMA_PRIORITY)
            out_copy(g0 + 1).start(priority=ROW_DMA_PRIORITY)
            return carry

        lax.fori_loop(0, nb // 2, block_pair, 0)

        @pl.when(lax.rem(nb, 2) == 1)
        def _():
            gl = first + nb - 1
            acquire(gl)
            ffn(gl, nb - 1)
            out_copy(gl).start(priority=ROW_DMA_PRIORITY)

        @pl.when(first + nb == total)
        def _():
            for k in range(Y_BUFS):
                @pl.when(total - 1 - k >= 0)
                def _():
                    out_copy(total - 1 - k).wait()


def _experts(xs, n_rows_out, first_block, n_blocks_e, counts, w_eg, w_eu, w_ed):
    w_map = lambda e, fb, nb, ct, tot: (e, 0, 0)
    half = D_MODEL // 2
    total = jnp.sum(n_blocks_e).astype(jnp.int32).reshape(1)
    return pl.pallas_call(
        _experts_kernel,
        grid_spec=pltpu.PrefetchScalarGridSpec(
            num_scalar_prefetch=4,
            grid=(N_EXPERTS,),
            in_specs=[pl.BlockSpec(memory_space=pl.ANY),
                      pl.BlockSpec((1, D_MODEL, D_EXPERT), w_map),
                      pl.BlockSpec((1, D_MODEL, D_EXPERT), w_map),
                      pl.BlockSpec((1, D_EXPERT, D_MODEL), w_map)],
            out_specs=pl.BlockSpec(memory_space=pl.ANY),
            scratch_shapes=[pltpu.VMEM((X_BUFS, BM, half), jnp.uint32), pltpu.VMEM((Y_BUFS, BM, half), jnp.uint32),
                            pltpu.VMEM((D_MODEL, D_EXPERT), BF16), pltpu.VMEM((D_MODEL, D_EXPERT), BF16),
                            pltpu.VMEM((D_EXPERT, D_MODEL), BF16),
                            pltpu.SemaphoreType.DMA((X_BUFS,)), pltpu.SemaphoreType.DMA((Y_BUFS,))]),
        out_shape=jax.ShapeDtypeStruct((n_rows_out, half), jnp.uint32),
        compiler_params=pltpu.CompilerParams(dimension_semantics=("arbitrary",), vmem_limit_bytes=VMEM_LIMIT),
        name="experts",
    )(first_block, n_blocks_e, counts, total, xs, w_eg, w_eu, w_ed)


def _combine_kernel(x2_ref, y1_ref, y2_ref, rt_ref, g_ref, o_ref):
    rt = rt_ref[...]
    r = jnp.transpose(jnp.concatenate([rt, jnp.zeros((LANES - rt.shape[0], rt.shape[1]), F32)], axis=0))
    g1, g2 = r[:, 2:3], r[:, 3:4]
    half = D_MODEL // 2
    y1_lo, y1_hi = _unpack_bf16_pairs_f32(y1_ref[...])
    y2_lo, y2_hi = _unpack_bf16_pairs_f32(y2_ref[...])
    x_lo = x2_ref[:, 0:half] + g1 * y1_lo + g2 * y2_lo
    x_hi = x2_ref[:, half:] + g1 * y1_hi + g2 * y2_hi
    ms = (jnp.sum(x_lo * x_lo, axis=-1, keepdims=True) + jnp.sum(x_hi * x_hi, axis=-1, keepdims=True)) / D_MODEL
    inv = lax.rsqrt(ms + EPS)
    o_ref[:, 0:half] = x_lo * inv * g_ref[:, 0:half]
    o_ref[:, half:] = x_hi * inv * g_ref[:, half:]


def _combine(x2, yg, rt, g_final, tm, blk1, blk2):
    n = x2.shape[0]
    return pl.pallas_call(
        _combine_kernel,
        grid=(n // tm,),
        in_specs=[pl.BlockSpec((tm, D_MODEL), lambda i: (i, 0)),
                  pl.BlockSpec((tm, D_MODEL // 2), lambda i: (blk1 + i, 0)),
                  pl.BlockSpec((tm, D_MODEL // 2), lambda i: (blk2 + i, 0)),
                  pl.BlockSpec((SUBLANES, tm), lambda i: (0, i)),
                  pl.BlockSpec((1, D_MODEL), lambda i: (0, 0))],
        out_specs=pl.BlockSpec((tm, D_MODEL), lambda i: (i, 0)),
        out_shape=jax.ShapeDtypeStruct((n, D_MODEL), F32),
        compiler_params=pltpu.CompilerParams(dimension_semantics=("arbitrary",), vmem_limit_bytes=VMEM_LIMIT),
        name="combine",
    )(x2, yg, yg, rt, g_final)


def _scatter_back(ys, dest, n_rows_out):
    return _sc_scatter_back(ys, dest, n_rows_out)


def _scatter_rows2(tables, slots_a, slots_b, tag_bases, n_rows_out):
    return _sc_scatter_rows2(tables, slots_a, slots_b, tag_bases, n_rows_out, SCATTER_CHUNK)


def kernel(x_prompt, x_sample, mem_prompt, state_conv, cache_mem_k, cache_mem_v, g_mix, w_in, conv_w, conv_b, ln_conv_g, ln_conv_b, ln_v_g, ln_v_b, w_sg, b_sg, w_out, g_mem, w_mk, w_mv, g_xattn, w_xq, w_xo, g_ffn, w_router_group, b_router_group, w_router_expert, b_router_expert, w_expert_gate, w_expert_up, w_expert_down, g_final):
    assert x_prompt.shape[0] == 1 and g_mix.shape[0] == 1
    n_p = x_prompt.shape[1]
    n_batch, t_len = x_sample.shape[0], x_sample.shape[1]
    n_s = n_batch * t_len
    row = lambda a: a.reshape(1, -1)

    w_router = jnp.concatenate(
        [w_router_group[0], jnp.transpose(w_router_expert[0], (1, 0, 2)).reshape(D_MODEL, N_EXPERTS)], axis=1)
    w_router = jnp.pad(w_router, ((0, 0), (0, LOGIT_LANES - w_router.shape[1]))).astype(BF16)
    b_router = jnp.pad(jnp.concatenate([b_router_group[0], b_router_expert[0].reshape(-1)]),
                       (0, LOGIT_LANES - N_GROUPS - N_EXPERTS)).reshape(1, LOGIT_LANES)
    tril_t = jnp.tril(jnp.ones((t_len, t_len), bool))
    w_sg_t = jnp.where(tril_t, w_sg[0][:, :t_len, :t_len], 0.0)
    eye_b = jnp.eye(n_batch, dtype=F32)
    w_sg_bd = jnp.einsum("ab,hij->haibj", eye_b, w_sg_t).reshape(SG_HEADS, n_s, n_s).astype(BF16)
    p = {
        "g_mix": row(g_mix[0]), "w_in": w_in[0],
        "conv_w": jnp.pad(conv_w[0], ((0, 1), (0, 0))), "conv_b": row(conv_b[0]),
        "ln_conv_g": row(ln_conv_g[0]), "ln_conv_b": row(ln_conv_b[0]),
        "ln_v_g": row(ln_v_g[0]), "ln_v_b": row(ln_v_b[0]),
        "w_sg": w_sg[0],
        "b_sg_rows": jnp.repeat(b_sg[0].T, SG_HEAD_DIM, axis=1),
        "w_sg_bd": w_sg_bd,
        "b_sg_rows_s": jnp.tile(jnp.repeat(b_sg[0][:, :t_len].T, SG_HEAD_DIM, axis=1), (n_batch, 1)),
        "w_out": w_out[0], "g_xattn": row(g_xattn[0]),
        "w_xq": w_xq[0], "w_xo": w_xo[0], "g_ffn": row(g_ffn[0]),
        "w_router": w_router, "b_router": b_router,
        "lower": jnp.tril(jnp.ones((n_s, n_s), BF16), -1),
    }

    k_p, v_p = _memkv(mem_prompt[0], row(g_mem[0]), w_mk[0], w_mv[0])
    p["k"] = k_p.astype(BF16)
    p["v"] = v_p.astype(BF16)
    p["k_s"] = jnp.transpose(cache_mem_k[0].reshape(n_batch, N_MEM, D_MODEL), (0, 2, 1)).astype(BF16)
    p["v_s"] = cache_mem_v[0].reshape(n_batch, N_MEM, D_MODEL).astype(BF16)

    assert n_p % n_s == 0
    x2_p, h3_p, logits_p, hist_p = _trunk_prompt(x_prompt[0], p)
    rt_p, cnt_t = _router(logits_p, TM)
    cnt_p = cnt_t[:, 0].reshape(1, LOGIT_LANES)
    x2_s, h3_s, rt_s, hist_s, sgv_s, cnt = _trunk_sample(
        x_sample.reshape(n_s, D_MODEL), state_conv[0], cnt_p, p, n_batch, t_len)

    experts = jnp.arange(N_EXPERTS, dtype=jnp.int32)
    w_e = (w_expert_gate[0], w_expert_up[0], w_expert_down[0])

    def moe_pass(cnt, h3_tables, rts, n_real):
        n_tot = sum(n_real)
        n_slots = -(-(n_tot * 2) // BM) * BM + N_EXPERTS * BM
        counts = cnt[0, :N_EXPERTS].astype(jnp.int32)
        padded = (counts + BM - 1) // BM * BM
        pad_start = jnp.cumsum(padded) - padded

        def one(e_row, rank_row):
            e = e_row.astype(jnp.int32)
            start = jnp.sum(jnp.where(e[None, :] == experts[:, None], pad_start[:, None], 0), axis=0)
            return start + rank_row.astype(jnp.int32)

        slots = [(one(rt[0], rt[4]), one(rt[1], rt[5])) for rt in rts]
        sa, sb, tag_bases, spare0, dest0 = [], [], [], n_slots, 0
        for tab, (a, b), n in zip(h3_tables, slots, n_real):
            n_spare = tab.shape[0] - n
            spare = spare0 + jnp.arange(n_spare, dtype=jnp.int32)
            sa.append(jnp.concatenate([a, spare]))
            sb.append(jnp.concatenate([b, spare + n_spare]))
            spare0 += 2 * n_spare
            tag_bases.append((dest0, dest0 + n))
            dest0 += 2 * n
        xs, tags = _scatter_rows2(tuple(h3_tables), tuple(sa), tuple(sb), tuple(tag_bases), spare0)
        ys = _experts(xs, n_slots, pad_start // BM, padded // BM, counts, *w_e)
        slot = jnp.arange(n_slots, dtype=jnp.int32)
        e_of = jnp.sum((jnp.cumsum(padded)[None, :] <= slot[:, None]).astype(jnp.int32), axis=1)
        e_of = jnp.minimum(e_of, N_EXPERTS - 1)
        is_e = e_of[:, None] == experts[None, :]
        live = slot - jnp.sum(jnp.where(is_e, pad_start, 0), axis=1) < jnp.sum(jnp.where(is_e, counts, 0), axis=1)
        dest = jnp.where(live, tags[:n_slots, 0], dest0 + slot)
        return _scatter_back(ys, dest, dest0 + n_slots)

    yg = moe_pass(cnt, [h3_p, h3_s], [rt_p, rt_s], [n_p, n_s])

    gf = row(g_final)
    y_p = _combine(x2_p, yg, rt_p, gf, TM_COMBINE, 0, n_p // TM_COMBINE)
    y_s = _combine(x2_s, yg, rt_s, gf, n_s, 2 * n_p // n_s, 2 * n_p // n_s + 1)

    return (y_p.reshape(1, n_p, D_MODEL),
            y_s.reshape(n_batch, t_len, D_MODEL),
            hist_p[HALO - HIST:].reshape(1, 1, HIST, D_CONV),
            hist_s.reshape(1, n_batch, HIST, D_CONV),
            k_p.reshape(1, 1, N_MEM, X_HEADS, X_HEAD_DIM),
            v_p.reshape(1, 1, N_MEM, X_HEADS, X_HEAD_DIM),
            sgv_s.reshape(1, n_batch, t_len, D_SG))
```

```python
import functools

import jax
import jax.numpy as jnp
from jax import lax
from jax.experimental import pallas as pl
from jax.experimental.pallas import tpu as pltpu
from jax.experimental.pallas import tpu_sc as plsc

D_MODEL = 1024
D_CONV = 512
D_SG = 512
CONV_WIDTH = 31
HIST = CONV_WIDTH - 1
SG_HEADS = 4
SG_HEAD_DIM = 128
SG_CHUNK = 128
N_MEM = 256
X_HEADS = 4
X_HEAD_DIM = 256
N_GROUPS = 4
EXPERTS_PER_GROUP = 8
N_EXPERTS = 32
D_EXPERT = 512
EPS = 1e-6

LANES = 128
SUBLANES = 8
SC_CORES = 2
SC_SUBCORES = 16
SC_WORKERS = SC_CORES * SC_SUBCORES
SC_LANES = 16
VMEM_LIMIT = 56 * 1024 * 1024

TM = 512
TM_COMBINE = 1024
ROUTE_ROWS = 2048
HALO = 32
SEG = TM // SUBLANES
SEG_HALO = 32
CONV_BLOCK = 16
CAST_ROWS = 64
BM = 256
X_LOOKAHEAD = 6
X_BUFS = X_LOOKAHEAD + 2
Y_BUFS = 6
ROW_DMA_PRIORITY = 1
SCATTER_CHUNK = 32
SCATTER_BUFS = 4
TAG_WORDS = 128
BACK_CHUNK = 56
BACK_BUFS = 4
LOGIT_LANES = 128

F32 = jnp.float32
BF16 = jnp.bfloat16


def _dot(a, b):
    return jnp.dot(a, b, preferred_element_type=F32)


def _rms(x, g):
    return x * lax.rsqrt(jnp.mean(x * x, axis=-1, keepdims=True) + EPS) * g


def _ln(x, g, b):
    mu = jnp.mean(x, axis=-1, keepdims=True)
    xc = x - mu
    var = jnp.mean(xc * xc, axis=-1, keepdims=True)
    return xc * lax.rsqrt(var + EPS) * g + b


def _sigmoid(x):
    return 1.0 / (1.0 + jnp.exp(-x))


def _pack_bf16_pairs(h):
    bits = lax.bitcast_convert_type(h, jnp.uint32)
    half = h.shape[1] // 2
    lo = lax.shift_right_logical(bits[:, :half], jnp.uint32(16))
    hi = bits[:, half:] & jnp.uint32(0xFFFF0000)
    return hi | lo


def _unpack_bf16_pairs_f32(p):
    lo = lax.bitcast_convert_type(lax.shift_left(p, jnp.uint32(16)), F32)
    hi = lax.bitcast_convert_type(p & jnp.uint32(0xFFFF0000), F32)
    return lo, hi


def _unpack_bf16_pairs(p):
    lo, hi = _unpack_bf16_pairs_f32(p)
    return lo.astype(BF16), hi.astype(BF16)


def _memkv_kernel(mem_ref, g_ref, wk_ref, wv_ref, k_ref, v_ref):
    m = _rms(mem_ref[...], g_ref[...]).astype(BF16)
    k_ref[...] = _dot(m, wk_ref[...].astype(BF16))
    v_ref[...] = _dot(m, wv_ref[...].astype(BF16))


def _memkv(mem, g_mem, w_mk, w_mv):
    return pl.pallas_call(
        _memkv_kernel,
        out_shape=(jax.ShapeDtypeStruct((N_MEM, D_MODEL), F32), jax.ShapeDtypeStruct((N_MEM, D_MODEL), F32)),
        compiler_params=pltpu.CompilerParams(vmem_limit_bytes=VMEM_LIMIT),
        name="memkv",
    )(mem, g_mem, w_mk, w_mv)


def _attn_heads(q, k, v, k_transposed):
    outs = []
    for h in range(X_HEADS):
        sl = slice(h * X_HEAD_DIM, (h + 1) * X_HEAD_DIM)
        if k_transposed:
            s = _dot(q[:, sl], k[sl, :])
        else:
            s = lax.dot_general(q[:, sl], k[:, sl], (((1,), (1,)), ((), ())), preferred_element_type=F32)
        s = s * (X_HEAD_DIM ** -0.5)
        s = s - jnp.max(s, axis=-1, keepdims=True)
        p = jnp.exp(s)
        p = p / jnp.sum(p, axis=-1, keepdims=True)
        outs.append(_dot(p.astype(BF16), v[:, sl]).astype(BF16))
    return jnp.concatenate(outs, axis=1)


def _route(logits, run, strict_lower):
    m = logits.shape[0]
    r = strict_lower.shape[0]
    lane = lax.broadcasted_iota(jnp.int32, (m, LOGIT_LANES), 1).astype(F32)
    neg = jnp.float32(-jnp.inf)
    big = jnp.float32(LOGIT_LANES)

    def first_argmax(vals):
        mx = jnp.max(vals, axis=-1, keepdims=True)
        idx = jnp.min(jnp.where(vals == mx, lane, big), axis=-1, keepdims=True)
        return mx, idx

    lg = jnp.where(lane < N_GROUPS, logits, neg)
    g_max, g_idx = first_argmax(lg)
    g_w = 1.0 / jnp.sum(jnp.exp(lg - g_max), axis=-1, keepdims=True)

    lo = N_GROUPS + g_idx * EXPERTS_PER_GROUP
    le = jnp.where((lane >= lo) & (lane < lo + EXPERTS_PER_GROUP), logits, neg)
    v1, i1 = first_argmax(le)
    v2, i2 = first_argmax(jnp.where(lane == i1, neg, le))
    t = jnp.exp(v2 - v1)
    gate1 = g_w / (1.0 + t)
    gate2 = g_w * t / (1.0 + t)
    e1 = i1 - N_GROUPS
    e2 = i2 - N_GROUPS

    oh1 = (lane == e1).astype(F32)
    oh2 = (lane == e2).astype(F32)
    oh = oh1 + oh2
    befores = []
    for r0 in range(0, m, r):
        oh_r = oh[r0:r0 + r, :]
        befores.append(_dot(strict_lower, oh_r.astype(BF16)) + run)
        run = run + jnp.sum(oh_r, axis=0, keepdims=True)
    before = befores[0] if len(befores) == 1 else jnp.concatenate(befores, axis=0)
    rank1 = jnp.sum(before * oh1, axis=-1, keepdims=True)
    rank2 = jnp.sum(before * oh2, axis=-1, keepdims=True)
    new_run = run

    rinfo = jnp.where(lane == 0, e1,
            jnp.where(lane == 1, e2,
            jnp.where(lane == 2, gate1,
            jnp.where(lane == 3, gate2,
            jnp.where(lane == 4, rank1,
            jnp.where(lane == 5, rank2, 0.0))))))
    return jnp.transpose(rinfo)[0:SUBLANES, :], new_run


def _conv_segments(a, w_ref, seg_ref, tail_ref, yseg_ref, conv_ref):
    sub = lax.broadcasted_iota(jnp.int32, (SUBLANES, LANES), 0)
    for lt in range(D_CONV // LANES):
        ls = slice(lt * LANES, (lt + 1) * LANES)
        for t0 in range(0, TM, SUBLANES):
            s, m = divmod(t0, SEG)
            seg_ref[lt, pl.ds((SEG_HALO + m) * SUBLANES + s, SUBLANES, stride=SUBLANES), :] = a[t0:t0 + SUBLANES, ls]
        for j in range(SEG_HALO):
            cur = seg_ref[lt, (SEG + j) * SUBLANES:(SEG + j + 1) * SUBLANES, :]
            prev = tail_ref[lt, j * SUBLANES:(j + 1) * SUBLANES, :]
            seg_ref[lt, j * SUBLANES:(j + 1) * SUBLANES, :] = jnp.where(
                sub == 0, pltpu.roll(prev, 1, axis=0), pltpu.roll(cur, 1, axis=0))
            tail_ref[lt, j * SUBLANES:(j + 1) * SUBLANES, :] = cur
        for m0 in range(0, SEG, CONV_BLOCK):
            acc = [jnp.zeros((SUBLANES, LANES), F32) for _ in range(CONV_BLOCK)]
            for idx in range(m0 - HIST, m0 + CONV_BLOCK):
                b = seg_ref[lt, (SEG_HALO + idx) * SUBLANES:(SEG_HALO + idx + 1) * SUBLANES, :]
                for m in range(max(m0, idx), min(m0 + CONV_BLOCK, idx + CONV_WIDTH)):
                    k = idx - m + HIST
                    acc[m - m0] = acc[m - m0] + b * w_ref[k:k + 1, ls]
            for m in range(m0, m0 + CONV_BLOCK):
                yseg_ref[lt, m * SUBLANES:(m + 1) * SUBLANES, :] = acc[m - m0]
        for t0 in range(0, TM, SUBLANES):
            s, m = divmod(t0, SEG)
            conv_ref[t0:t0 + SUBLANES, ls] = yseg_ref[lt, pl.ds(m * SUBLANES + s, SUBLANES, stride=SUBLANES), :]


def _cast_rows(src_ref, dst_ref):
    rows = src_ref.shape[0]

    def body(c, carry):
        r0 = pl.multiple_of(c * CAST_ROWS, CAST_ROWS)
        dst_ref[pl.ds(r0, CAST_ROWS), :] = src_ref[pl.ds(r0, CAST_ROWS), :].astype(BF16)
        return carry

    lax.fori_loop(0, rows // CAST_ROWS, body, 0)


def _trunk_prompt_kernel(x_ref, gmix_ref, win32_ref, convw_ref, convb_ref, lncg_ref, lncb_ref, lnvg_ref, lnvb_ref,
                         wsg_ref, bsg_ref, wout32_ref, gx_ref, wxq32_ref, kmem_ref, v_ref, wxo32_ref, gffn_ref, wr_ref,
                         br_ref,
                         x2_ref, h3_ref, logit_ref, hist_ref,
                         seg_ref, tail_ref, yseg_ref, conv_ref, win_ref, wout_ref, wxq_ref, wxo_ref):
    i = pl.program_id(0)

    @pl.when(i == 0)
    def _():
        tail_ref[...] = jnp.zeros(tail_ref.shape, F32)
        _cast_rows(win32_ref, win_ref)
        _cast_rows(wout32_ref, wout_ref)
        _cast_rows(wxq32_ref, wxq_ref)
        _cast_rows(wxo32_ref, wxo_ref)

    x = x_ref[...]
    h = _rms(x, gmix_ref[...]).astype(BF16)

    a_in = _dot(h, win_ref[:, 0:D_CONV])
    a_gate = _dot(h, win_ref[:, D_CONV:2 * D_CONV])
    a = a_in * _sigmoid(a_gate)
    hist_ref[...] = a[TM - HALO:, :]
    _conv_segments(a, convw_ref, seg_ref, tail_ref, yseg_ref, conv_ref)

    y = _ln(conv_ref[...] + convb_ref[...], lncg_ref[...], lncb_ref[...])
    a_out = (y * _sigmoid(y)).astype(BF16)

    u = _dot(h, win_ref[:, 2 * D_CONV:2 * D_CONV + D_SG])
    v = _ln(_dot(h, win_ref[:, 2 * D_CONV + D_SG:]), lnvg_ref[...], lnvb_ref[...]).astype(BF16)
    ri = lax.broadcasted_iota(jnp.int32, (SG_CHUNK, SG_CHUNK), 0)
    ci = lax.broadcasted_iota(jnp.int32, (SG_CHUNK, SG_CHUNK), 1)
    w_tril = [jnp.where(ci <= ri, wsg_ref[hh], 0.0).astype(BF16) for hh in range(SG_HEADS)]
    gate_rows = []
    for c in range(TM // SG_CHUNK):
        rs = slice(c * SG_CHUNK, (c + 1) * SG_CHUNK)
        heads = [_dot(w_tril[hh], v[rs, hh * SG_HEAD_DIM:(hh + 1) * SG_HEAD_DIM]) for hh in range(SG_HEADS)]
        gate_rows.append(jnp.concatenate(heads, axis=1) + bsg_ref[...])
    b_out = (u * jnp.concatenate(gate_rows, axis=0)).astype(BF16)

    x1 = x + _dot(a_out, wout_ref[0:D_CONV, :]) + _dot(b_out, wout_ref[D_CONV:, :])

    hx = _rms(x1, gx_ref[...]).astype(BF16)
    q = _dot(hx, wxq_ref[...]).astype(BF16)
    x2 = x1 + _dot(_attn_heads(q, kmem_ref[...], v_ref[...], False), wxo_ref[...])
    x2_ref[...] = x2

    h3 = _rms(x2, gffn_ref[...]).astype(BF16)
    h3_ref[...] = _pack_bf16_pairs(h3.astype(F32))
    logit_ref[...] = _dot(h3, wr_ref[...]) + br_ref[...]


def _router_kernel(logit_ref, upper_ref, ones_ref, rt_ref, cnt_ref, run_ref):
    @pl.when(pl.program_id(0) == 0)
    def _():
        run_ref[...] = jnp.zeros(run_ref.shape, F32)

    n = ROUTE_ROWS
    lt = jnp.transpose(logit_ref[...])
    neg = jnp.float32(-jnp.inf)
    big = jnp.float32(LOGIT_LANES)

    def first_argmax(vals, rows):
        mx = jnp.max(vals, axis=0, keepdims=True)
        idx = jnp.min(jnp.where(vals == mx, rows, big), axis=0, keepdims=True)
        return mx, idx

    row8 = lax.broadcasted_iota(jnp.int32, (SUBLANES, n), 0).astype(F32)
    lg = jnp.where(row8 < N_GROUPS, lt[0:SUBLANES, :], neg)
    g_max, g_idx = first_argmax(lg, row8)
    g_w = 1.0 / jnp.sum(jnp.exp(lg - g_max), axis=0, keepdims=True)

    n_rows = N_GROUPS + N_EXPERTS + (-(N_GROUPS + N_EXPERTS)) % SUBLANES
    rows = lax.broadcasted_iota(jnp.int32, (n_rows, n), 0).astype(F32)
    lo = N_GROUPS + g_idx * EXPERTS_PER_GROUP
    le = jnp.where((rows >= lo) & (rows < lo + EXPERTS_PER_GROUP), lt[0:n_rows, :], neg)
    v1, i1 = first_argmax(le, rows)
    v2, i2 = first_argmax(jnp.where(rows == i1, neg, le), rows)
    t = jnp.exp(v2 - v1)
    gate1 = g_w / (1.0 + t)
    gate2 = g_w * t / (1.0 + t)
    e1 = i1 - N_GROUPS
    e2 = i2 - N_GROUPS

    erow = lax.broadcasted_iota(jnp.int32, (LOGIT_LANES, n), 0).astype(F32)
    oh1 = (erow == e1).astype(F32)
    oh2 = (erow == e2).astype(F32)
    oh = (oh1 + oh2).astype(BF16)
    r = upper_ref.shape[0]
    run = run_ref[...]
    rank1, rank2 = [], []
    for c0 in range(0, n, r):
        cs = slice(c0, c0 + r)
        before = _dot(oh[:, cs], upper_ref[...]) + run
        rank1.append(jnp.sum(before * oh1[:, cs], axis=0, keepdims=True))
        rank2.append(jnp.sum(before * oh2[:, cs], axis=0, keepdims=True))
        run = run + _dot(oh[:, cs], ones_ref[...])
    run_ref[...] = run
    cnt_ref[...] = run[:, 0:LANES]

    sub = lax.broadcasted_iota(jnp.int32, (SUBLANES, n), 0)
    vals = (e1, e2, gate1, gate2, jnp.concatenate(rank1, axis=1), jnp.concatenate(rank2, axis=1))
    rt = jnp.zeros((SUBLANES, n), F32)
    for k, v in enumerate(vals):
        rt = jnp.where(sub == k, v, rt)
    rt_ref[...] = rt


def _router(logits, rank_block):
    n = logits.shape[0]
    assert n % ROUTE_ROWS == 0 and ROUTE_ROWS % rank_block == 0
    upper = jnp.triu(jnp.ones((rank_block, rank_block), BF16), 1)
    ones = jnp.ones((rank_block, rank_block), BF16)
    return pl.pallas_call(
        _router_kernel,
        grid=(n // ROUTE_ROWS,),
        in_specs=[pl.BlockSpec((ROUTE_ROWS, LOGIT_LANES), lambda i: (i, 0)),
                  pl.BlockSpec(upper.shape, lambda i: (0, 0)),
                  pl.BlockSpec(ones.shape, lambda i: (0, 0))],
        out_specs=(pl.BlockSpec((SUBLANES, ROUTE_ROWS), lambda i: (0, i)),
                   pl.BlockSpec((LOGIT_LANES, LANES), lambda i: (0, 0))),
        out_shape=(jax.ShapeDtypeStruct((SUBLANES, n), F32), jax.ShapeDtypeStruct((LOGIT_LANES, LANES), F32)),
        scratch_shapes=[pltpu.VMEM((LOGIT_LANES, rank_block), F32)],
        compiler_params=pltpu.CompilerParams(dimension_semantics=("arbitrary",), vmem_limit_bytes=VMEM_LIMIT),
        name="router",
    )(logits, upper, ones)


def _const_spec(shape):
    nd = len(shape)
    return pl.BlockSpec(shape, lambda i: (0,) * nd, pipeline_mode=pl.Buffered(1))


def _trunk_prompt(x, p):
    n = x.shape[0]
    assert n % TM == 0
    row = lambda w: pl.BlockSpec((TM, w), lambda i: (i, 0))
    consts = [p["g_mix"], p["w_in"], p["conv_w"], p["conv_b"], p["ln_conv_g"], p["ln_conv_b"], p["ln_v_g"],
              p["ln_v_b"], p["w_sg"], p["b_sg_rows"], p["w_out"], p["g_xattn"], p["w_xq"], p["k"], p["v"],
              p["w_xo"], p["g_ffn"], p["w_router"], p["b_router"]]
    return pl.pallas_call(
        _trunk_prompt_kernel,
        grid=(n // TM,),
        in_specs=[row(D_MODEL)] + [_const_spec(c.shape) for c in consts],
        out_specs=(row(D_MODEL), row(D_MODEL // 2), row(LOGIT_LANES),
                   pl.BlockSpec((HALO, D_CONV), lambda i: (0, 0))),
        out_shape=(jax.ShapeDtypeStruct((n, D_MODEL), F32),
                   jax.ShapeDtypeStruct((n, D_MODEL // 2), jnp.uint32),
                   jax.ShapeDtypeStruct((n, LOGIT_LANES), F32),
                   jax.ShapeDtypeStruct((HALO, D_CONV), F32)),
        scratch_shapes=[pltpu.VMEM((D_CONV // LANES, (SEG_HALO + SEG) * SUBLANES, LANES), F32),
                        pltpu.VMEM((D_CONV // LANES, SEG_HALO * SUBLANES, LANES), F32),
                        pltpu.VMEM((D_CONV // LANES, TM, LANES), F32),
                        pltpu.VMEM((TM, D_CONV), F32),
                        pltpu.VMEM(p["w_in"].shape, BF16), pltpu.VMEM(p["w_out"].shape, BF16),
                        pltpu.VMEM(p["w_xq"].shape, BF16), pltpu.VMEM(p["w_xo"].shape, BF16)],
        compiler_params=pltpu.CompilerParams(dimension_semantics=("arbitrary",), vmem_limit_bytes=VMEM_LIMIT),
        name="trunk_prompt",
    )(x, *consts)


def _trunk_sample_kernel(n_batch, t_len,
                         x_ref, hist_in_ref, run_in_ref, gmix_ref, win_ref, convw_ref, convb_ref, lncg_ref, lncb_ref,
                         lnvg_ref, lnvb_ref, wsgbd_ref, bsg_ref, wout_ref, gx_ref, wxq_ref, kmem_ref, v_ref, wxo_ref,
                         gffn_ref, wr_ref, br_ref, lower_ref,
                         x2_ref, h3_ref, rt_ref, hist_ref, sgv_ref, cnt_ref,
                         ext_ref, conv_ref, att_ref):
    x = x_ref[...]
    h = _rms(x, gmix_ref[...]).astype(BF16)
    z = _dot(h, win_ref[...].astype(BF16))
    a = z[:, 0:D_CONV] * _sigmoid(z[:, D_CONV:2 * D_CONV])
    ext_len = HIST + t_len
    for b in range(n_batch):
        ext_ref[b, 0:HIST, :] = hist_in_ref[b]
        ext_ref[b, HIST:ext_len, :] = a[b * t_len:(b + 1) * t_len, :]
    for b in range(n_batch):
        acc = jnp.zeros((t_len, D_CONV), F32)
        for k in range(CONV_WIDTH):
            acc = acc + ext_ref[b, k:k + t_len, :] * convw_ref[k:k + 1, :]
        conv_ref[b * t_len:(b + 1) * t_len, :] = acc
        hist_ref[b] = ext_ref[b, ext_len - HIST:ext_len, :]

    y = _ln(conv_ref[...] + convb_ref[...], lncg_ref[...], lncb_ref[...])
    a_out = (y * _sigmoid(y)).astype(BF16)

    u = z[:, 2 * D_CONV:2 * D_CONV + D_SG]
    v = _ln(z[:, 2 * D_CONV + D_SG:], lnvg_ref[...], lnvb_ref[...])
    sgv_ref[...] = v
    vb = v.astype(BF16)
    heads = [_dot(wsgbd_ref[hh], vb[:, hh * SG_HEAD_DIM:(hh + 1) * SG_HEAD_DIM]) for hh in range(SG_HEADS)]
    b_out = (u * (jnp.concatenate(heads, axis=1) + bsg_ref[...])).astype(BF16)

    x1 = (x + _dot(a_out, wout_ref[0:D_CONV, :].astype(BF16))
          + _dot(b_out, wout_ref[D_CONV:, :].astype(BF16)))

    hx = _rms(x1, gx_ref[...]).astype(BF16)
    q = _dot(hx, wxq_ref[...].astype(BF16)).astype(BF16)
    for b in range(n_batch):
        rs = slice(b * t_len, (b + 1) * t_len)
        att_ref[rs, :] = _attn_heads(q[rs, :], kmem_ref[b], v_ref[b], True)
    x2 = x1 + _dot(att_ref[...], wxo_ref[...].astype(BF16))
    x2_ref[...] = x2

    h3 = _rms(x2, gffn_ref[...]).astype(BF16)
    m = n_batch * t_len
    h3_ref[0:m, :] = _pack_bf16_pairs(h3.astype(F32))
    if h3_ref.shape[0] > m:
        h3_ref[m:, :] = jnp.zeros((h3_ref.shape[0] - m, D_MODEL // 2), jnp.uint32)
    rt, new_run = _route(_dot(h3, wr_ref[...]) + br_ref[...], run_in_ref[...], lower_ref[...])
    rt_ref[...] = rt
    cnt_ref[...] = new_run


def _trunk_sample(x, hist, run, p, n_batch, t_len):
    m = n_batch * t_len
    args = [x, hist, run, p["g_mix"], p["w_in"], p["conv_w"], p["conv_b"], p["ln_conv_g"], p["ln_conv_b"],
            p["ln_v_g"], p["ln_v_b"], p["w_sg_bd"], p["b_sg_rows_s"], p["w_out"], p["g_xattn"], p["w_xq"],
            p["k_s"], p["v_s"], p["w_xo"], p["g_ffn"], p["w_router"], p["b_router"], p["lower"]]
    return pl.pallas_call(
        functools.partial(_trunk_sample_kernel, n_batch, t_len),
        out_shape=(jax.ShapeDtypeStruct((m, D_MODEL), F32),
                   jax.ShapeDtypeStruct((-(-m // (SC_WORKERS * SUBLANES)) * SC_WORKERS * SUBLANES, D_MODEL // 2),
                                        jnp.uint32),
                   jax.ShapeDtypeStruct((SUBLANES, m), F32),
                   jax.ShapeDtypeStruct((n_batch, HIST, D_CONV), F32),
                   jax.ShapeDtypeStruct((m, D_SG), F32),
                   jax.ShapeDtypeStruct((1, LOGIT_LANES), F32)),
        scratch_shapes=[pltpu.VMEM((n_batch, HIST + t_len, D_CONV), F32),
                        pltpu.VMEM((m, D_CONV), F32),
                        pltpu.VMEM((m, D_MODEL), BF16)],
        compiler_params=pltpu.CompilerParams(vmem_limit_bytes=VMEM_LIMIT),
        name="trunk_sample",
    )(*args)


def _sc_worker_id():
    return lax.axis_index("s") * SC_CORES + lax.axis_index("c")


def _sc_chunk(per_w, max_chunk):
    assert per_w % SUBLANES == 0 and max_chunk <= LANES
    return max(c for c in range(SUBLANES, max_chunk + 1, SUBLANES) if per_w % c == 0)


def _sc_scatter_rows2(tables, slots_a, slots_b, tag_bases, n_rows_out, max_chunk):
    d, dtype = tables[0].shape[1], tables[0].dtype
    plans = []
    for t in tables:
        per_w = t.shape[0] // SC_WORKERS
        assert per_w * SC_WORKERS == t.shape[0]
        chunk = _sc_chunk(per_w, max_chunk)
        plans.append((per_w, chunk, per_w // chunk))
    cmax = max(c for _, c, _ in plans)
    n_t = len(tables)
    mesh = plsc.VectorSubcoreMesh(core_axis_name="c", subcore_axis_name="s")

    nb = SCATTER_BUFS
    lag = nb // 3
    scratch = []
    for _, chunk, _ in plans:
        for _ in range(nb):
            scratch += [pltpu.VMEM((chunk,), jnp.int32), pltpu.VMEM((chunk,), jnp.int32)]
    scratch += [pltpu.VMEM((cmax, d), dtype)] * nb
    scratch += [pltpu.VMEM((cmax, TAG_WORDS), jnp.int32)] * (2 * nb)
    scratch += [pltpu.SemaphoreType.DMA] * (2 * nb)

    @functools.partial(pl.kernel, mesh=mesh,
                       out_type=(jax.ShapeDtypeStruct((n_rows_out, d), dtype),
                                 jax.ShapeDtypeStruct((n_rows_out, TAG_WORDS), jnp.int32)),
                       scratch_types=scratch)
    def scatter(*refs):
        tab_hbm = refs[0:n_t]
        sa_hbm = refs[n_t:2 * n_t]
        sb_hbm = refs[2 * n_t:3 * n_t]
        out_hbm, tag_hbm = refs[3 * n_t], refs[3 * n_t + 1]
        sc = refs[3 * n_t + 2:]
        idx_refs = sc[:2 * nb * n_t]
        rows = sc[2 * nb * n_t:2 * nb * n_t + nb]
        tagbufs = sc[2 * nb * n_t + nb:2 * nb * n_t + 3 * nb]
        lsem = sc[2 * nb * n_t + 3 * nb:2 * nb * n_t + 4 * nb]
        ssem = sc[2 * nb * n_t + 4 * nb:]
        wid = _sc_worker_id()

        work = []
        for t, (per_w, chunk, n_chunks) in enumerate(plans):
            for j in range(n_chunks):
                work.append((t, wid * per_w + j * chunk, chunk))

        def parts(k):
            t, off, chunk = work[k]
            b = k % nb
            ia, ib = idx_refs[2 * nb * t + 2 * b], idx_refs[2 * nb * t + 2 * b + 1]
            full = chunk == cmax
            rv = rows[b] if full else rows[b].at[pl.ds(0, chunk)]
            ta = tagbufs[2 * b] if full else tagbufs[2 * b].at[pl.ds(0, chunk)]
            tb = tagbufs[2 * b + 1] if full else tagbufs[2 * b + 1].at[pl.ds(0, chunk)]
            return t, off, chunk, b, ia, ib, rv, ta, tb

        def start_load(k):
            t, off, chunk, b, ia, ib, rv, ta, tb = parts(k)
            return (pltpu.async_copy(tab_hbm[t].at[pl.ds(off, chunk)], rv, lsem[b]),
                    pltpu.async_copy(sa_hbm[t].at[pl.ds(off, chunk)], ia, lsem[b]),
                    pltpu.async_copy(sb_hbm[t].at[pl.ds(off, chunk)], ib, lsem[b]))

        def start_scatter(k):
            t, off, chunk, b, ia, ib, rv, ta, tb = parts(k)
            base_a, base_b = tag_bases[t]
            for r in range(chunk):
                row_id = (off + r).astype(jnp.int32)
                tagbufs[2 * b][r, pl.ds(0, SC_LANES)] = jnp.zeros((SC_LANES,), jnp.int32) + (base_a + row_id)
                tagbufs[2 * b + 1][r, pl.ds(0, SC_LANES)] = jnp.zeros((SC_LANES,), jnp.int32) + (base_b + row_id)
            return (pltpu.async_copy(rv, out_hbm.at[ia], ssem[b]), pltpu.async_copy(rv, out_hbm.at[ib], ssem[b]),
                    pltpu.async_copy(ta, tag_hbm.at[ia], ssem[b]), pltpu.async_copy(tb, tag_hbm.at[ib], ssem[b]))

        loads, scatters = {}, {}
        for k in range(len(work) + lag):
            if k < len(work):
                if k >= nb:
                    for c in scatters.pop(k - nb):
                        c.wait()
                loads[k] = start_load(k)
            w = k - lag
            if w >= 0:
                for c in loads.pop(w):
                    c.wait()
                scatters[w] = start_scatter(w)
        for w in sorted(scatters):
            for c in scatters[w]:
                c.wait()

    return scatter(*tables, *slots_a, *slots_b)


def _sc_scatter_back(ys, dest, n_rows_out):
    n_rows, d = ys.shape
    per_w = n_rows // SC_WORKERS
    assert per_w * SC_WORKERS == n_rows
    chunk = _sc_chunk(per_w, BACK_CHUNK)
    n_chunks = per_w // chunk
    nb = BACK_BUFS
    lag = nb // 2
    mesh = plsc.VectorSubcoreMesh(core_axis_name="c", subcore_axis_name="s")

    @functools.partial(
        pl.kernel, mesh=mesh,
        out_type=jax.ShapeDtypeStruct((n_rows_out, d), ys.dtype),
        scratch_types=([pltpu.VMEM((chunk,), jnp.int32)] * nb + [pltpu.VMEM((chunk, d), ys.dtype)] * nb
                       + [pltpu.SemaphoreType.DMA] * (2 * nb)),
    )
    def scatter_back(ys_hbm, dest_hbm, out_hbm, *rest):
        idx = rest[:nb]
        rows = rest[nb:2 * nb]
        lsem = rest[2 * nb:3 * nb]
        ssem = rest[3 * nb:]
        base = _sc_worker_id() * per_w

        loads, scatters = {}, {}
        for k in range(n_chunks + lag):
            if k < n_chunks:
                b = k % nb
                if k >= nb:
                    scatters.pop(k - nb).wait()
                off = base + k * chunk
                loads[k] = (pltpu.async_copy(ys_hbm.at[pl.ds(off, chunk)], rows[b], lsem[b]),
                            pltpu.async_copy(dest_hbm.at[pl.ds(off, chunk)], idx[b], lsem[b]))
            w = k - lag
            if w >= 0:
                b = w % nb
                for c in loads.pop(w):
                    c.wait()
                scatters[w] = pltpu.async_copy(rows[b], out_hbm.at[idx[b]], ssem[b])
        for w in sorted(scatters):
            scatters[w].wait()

    return scatter_back(ys, dest)


def _experts_kernel(first_ref, nblk_ref, cnt_ref, tot_ref, xs_hbm, wg_ref, wu_ref, wd_ref, ys_hbm,
                    xbuf, ybuf, wg_bf, wu_bf, wd_bf, in_sem, out_sem):
    e = pl.program_id(0)
    nb = nblk_ref[e]
    first = first_ref[e]
    cnt = cnt_ref[e]
    total = tot_ref[0]
    half = D_MODEL // 2

    def in_copy(gb):
        slot = lax.rem(gb, X_BUFS)
        return pltpu.make_async_copy(xs_hbm.at[pl.ds(gb * BM, BM)], xbuf.at[slot], in_sem.at[slot])

    def out_copy(gb):
        slot = lax.rem(gb, Y_BUFS)
        return pltpu.make_async_copy(ybuf.at[slot], ys_hbm.at[pl.ds(gb * BM, BM)], out_sem.at[slot])

    @pl.when(nb > 0)
    def _():
        @pl.when(first == 0)
        def _():
            for k in range(X_LOOKAHEAD):
                @pl.when(k < total)
                def _():
                    in_copy(k).start(priority=ROW_DMA_PRIORITY)

        wg_bf[...] = wg_ref[0].astype(BF16)
        wu_bf[...] = wu_ref[0].astype(BF16)
        wd_bf[...] = wd_ref[0].astype(BF16)

        def acquire(gb):
            @pl.when(gb + X_LOOKAHEAD < total)
            def _():
                in_copy(gb + X_LOOKAHEAD).start(priority=ROW_DMA_PRIORITY)

            in_copy(gb).wait()

            @pl.when(gb >= Y_BUFS)
            def _():
                out_copy(gb - Y_BUFS).wait()

        def ffn(gb, j):
            live = lax.broadcasted_iota(jnp.int32, (BM, half), 0) < cnt - j * BM
            lo, hi = _unpack_bf16_pairs(jnp.where(live, xbuf[lax.rem(gb, X_BUFS)], jnp.uint32(0)))
            g = _dot(lo, wg_bf[0:half, :]) + _dot(hi, wg_bf[half:, :])
            u = _dot(lo, wu_bf[0:half, :]) + _dot(hi, wu_bf[half:, :])
            hm = (g * _sigmoid(g) * u).astype(BF16)
            y = _dot(hm, wd_bf[...])
            ybuf[lax.rem(gb, Y_BUFS)] = _pack_bf16_pairs(y.astype(BF16).astype(F32))

        def block_pair(jp, carry):
            j0 = 2 * jp
            g0 = first + j0
            acquire(g0)
            acquire(g0 + 1)
            ffn(g0, j0)
            ffn(g0 + 1, j0 + 1)
            out_copy(g0).start(priority=ROW_DMA_PRIORITY)
            out_copy(g0 + 1).start(priority=ROW_DMA_PRIORITY)
            return carry

        lax.fori_loop(0, nb // 2, block_pair, 0)

        @pl.when(lax.rem(nb, 2) == 1)
        def _():
            gl = first + nb - 1
            acquire(gl)
            ffn(gl, nb - 1)
            out_copy(gl).start(priority=ROW_DMA_PRIORITY)

        @pl.when(first + nb == total)
        def _():
            for k in range(Y_BUFS):
                @pl.when(total - 1 - k >= 0)
                def _():
                    out_copy(total - 1 - k).wait()


def _experts(xs, n_rows_out, first_block, n_blocks_e, counts, w_eg, w_eu, w_ed):
    w_map = lambda e, fb, nb, ct, tot: (e, 0, 0)
    half = D_MODEL // 2
    total = jnp.sum(n_blocks_e).astype(jnp.int32).reshape(1)
    return pl.pallas_call(
        _experts_kernel,
        grid_spec=pltpu.PrefetchScalarGridSpec(
            num_scalar_prefetch=4,
            grid=(N_EXPERTS,),
            in_specs=[pl.BlockSpec(memory_space=pl.ANY),
                      pl.BlockSpec((1, D_MODEL, D_EXPERT), w_map),
                      pl.BlockSpec((1, D_MODEL, D_EXPERT), w_map),
                      pl.BlockSpec((1, D_EXPERT, D_MODEL), w_map)],
            out_specs=pl.BlockSpec(memory_space=pl.ANY),
            scratch_shapes=[pltpu.VMEM((X_BUFS, BM, half), jnp.uint32), pltpu.VMEM((Y_BUFS, BM, half), jnp.uint32),
                            pltpu.VMEM((D_MODEL, D_EXPERT), BF16), pltpu.VMEM((D_MODEL, D_EXPERT), BF16),
                            pltpu.VMEM((D_EXPERT, D_MODEL), BF16),
                            pltpu.SemaphoreType.DMA((X_BUFS,)), pltpu.SemaphoreType.DMA((Y_BUFS,))]),
        out_shape=jax.ShapeDtypeStruct((n_rows_out, half), jnp.uint32),
        compiler_params=pltpu.CompilerParams(dimension_semantics=("arbitrary",), vmem_limit_bytes=VMEM_LIMIT),
        name="experts",
    )(first_block, n_blocks_e, counts, total, xs, w_eg, w_eu, w_ed)


def _combine_kernel(x2_ref, y1_ref, y2_ref, rt_ref, g_ref, o_ref):
    rt = rt_ref[...]
    r = jnp.transpose(jnp.concatenate([rt, jnp.zeros((LANES - rt.shape[0], rt.shape[1]), F32)], axis=0))
    g1, g2 = r[:, 2:3], r[:, 3:4]
    half = D_MODEL // 2
    y1_lo, y1_hi = _unpack_bf16_pairs_f32(y1_ref[...])
    y2_lo, y2_hi = _unpack_bf16_pairs_f32(y2_ref[...])
    x_lo = x2_ref[:, 0:half] + g1 * y1_lo + g2 * y2_lo
    x_hi = x2_ref[:, half:] + g1 * y1_hi + g2 * y2_hi
    ms = (jnp.sum(x_lo * x_lo, axis=-1, keepdims=True) + jnp.sum(x_hi * x_hi, axis=-1, keepdims=True)) / D_MODEL
    inv = lax.rsqrt(ms + EPS)
    o_ref[:, 0:half] = x_lo * inv * g_ref[:, 0:half]
    o_ref[:, half:] = x_hi * inv * g_ref[:, half:]


def _combine(x2, yg, rt, g_final, tm, blk1, blk2):
    n = x2.shape[0]
    return pl.pallas_call(
        _combine_kernel,
        grid=(n // tm,),
        in_specs=[pl.BlockSpec((tm, D_MODEL), lambda i: (i, 0)),
                  pl.BlockSpec((tm, D_MODEL // 2), lambda i: (blk1 + i, 0)),
                  pl.BlockSpec((tm, D_MODEL // 2), lambda i: (blk2 + i, 0)),
                  pl.BlockSpec((SUBLANES, tm), lambda i: (0, i)),
                  pl.BlockSpec((1, D_MODEL), lambda i: (0, 0))],
        out_specs=pl.BlockSpec((tm, D_MODEL), lambda i: (i, 0)),
        out_shape=jax.ShapeDtypeStruct((n, D_MODEL), F32),
        compiler_params=pltpu.CompilerParams(dimension_semantics=("arbitrary",), vmem_limit_bytes=VMEM_LIMIT),
        name="combine",
    )(x2, yg, yg, rt, g_final)


def _scatter_back(ys, dest, n_rows_out):
    return _sc_scatter_back(ys, dest, n_rows_out)


def _scatter_rows2(tables, slots_a, slots_b, tag_bases, n_rows_out):
    return _sc_scatter_rows2(tables, slots_a, slots_b, tag_bases, n_rows_out, SCATTER_CHUNK)


def kernel(x_prompt, x_sample, mem_prompt, state_conv, cache_mem_k, cache_mem_v, g_mix, w_in, conv_w, conv_b, ln_conv_g, ln_conv_b, ln_v_g, ln_v_b, w_sg, b_sg, w_out, g_mem, w_mk, w_mv, g_xattn, w_xq, w_xo, g_ffn, w_router_group, b_router_group, w_router_expert, b_router_expert, w_expert_gate, w_expert_up, w_expert_down, g_final):
    assert x_prompt.shape[0] == 1 and g_mix.shape[0] == 1
    n_p = x_prompt.shape[1]
    n_batch, t_len = x_sample.shape[0], x_sample.shape[1]
    n_s = n_batch * t_len
    row = lambda a: a.reshape(1, -1)

    w_router = jnp.concatenate(
        [w_router_group[0], jnp.transpose(w_router_expert[0], (1, 0, 2)).reshape(D_MODEL, N_EXPERTS)], axis=1)
    w_router = jnp.pad(w_router, ((0, 0), (0, LOGIT_LANES - w_router.shape[1]))).astype(BF16)
    b_router = jnp.pad(jnp.concatenate([b_router_group[0], b_router_expert[0].reshape(-1)]),
                       (0, LOGIT_LANES - N_GROUPS - N_EXPERTS)).reshape(1, LOGIT_LANES)
    tril_t = jnp.tril(jnp.ones((t_len, t_len), bool))
    w_sg_t = jnp.where(tril_t, w_sg[0][:, :t_len, :t_len], 0.0)
    eye_b = jnp.eye(n_batch, dtype=F32)
    w_sg_bd = jnp.einsum("ab,hij->haibj", eye_b, w_sg_t).reshape(SG_HEADS, n_s, n_s).astype(BF16)
    p = {
        "g_mix": row(g_mix[0]), "w_in": w_in[0],
        "conv_w": jnp.pad(conv_w[0], ((0, 1), (0, 0))), "conv_b": row(conv_b[0]),
        "ln_conv_g": row(ln_conv_g[0]), "ln_conv_b": row(ln_conv_b[0]),
        "ln_v_g": row(ln_v_g[0]), "ln_v_b": row(ln_v_b[0]),
        "w_sg": w_sg[0],
        "b_sg_rows": jnp.repeat(b_sg[0].T, SG_HEAD_DIM, axis=1),
        "w_sg_bd": w_sg_bd,
        "b_sg_rows_s": jnp.tile(jnp.repeat(b_sg[0][:, :t_len].T, SG_HEAD_DIM, axis=1), (n_batch, 1)),
        "w_out": w_out[0], "g_xattn": row(g_xattn[0]),
        "w_xq": w_xq[0], "w_xo": w_xo[0], "g_ffn": row(g_ffn[0]),
        "w_router": w_router, "b_router": b_router,
        "lower": jnp.tril(jnp.ones((n_s, n_s), BF16), -1),
    }

    k_p, v_p = _memkv(mem_prompt[0], row(g_mem[0]), w_mk[0], w_mv[0])
    p["k"] = k_p.astype(BF16)
    p["v"] = v_p.astype(BF16)
    p["k_s"] = jnp.transpose(cache_mem_k[0].astype(BF16), (0, 2, 3, 1)).reshape(n_batch, D_MODEL, N_MEM)
    p["v_s"] = cache_mem_v[0].astype(BF16).reshape(n_batch, N_MEM, D_MODEL)

    assert n_p % n_s == 0
    x2_p, h3_p, logits_p, hist_p = _trunk_prompt(x_prompt[0], p)
    rt_p, cnt_t = _router(logits_p, TM)
    cnt_p = cnt_t[:, 0].reshape(1, LOGIT_LANES)
    x2_s, h3_s, rt_s, hist_s, sgv_s, cnt = _trunk_sample(
        x_sample.reshape(n_s, D_MODEL), state_conv[0], cnt_p, p, n_batch, t_len)

    experts = jnp.arange(N_EXPERTS, dtype=jnp.int32)
    w_e = (w_expert_gate[0], w_expert_up[0], w_expert_down[0])

    def moe_pass(cnt, h3_tables, rts, n_real):
        n_tot = sum(n_real)
        n_slots = -(-(n_tot * 2) // BM) * BM + N_EXPERTS * BM
        counts = cnt[0, :N_EXPERTS].astype(jnp.int32)
        padded = (counts + BM - 1) // BM * BM
        pad_start = jnp.cumsum(padded) - padded

        def one(e_row, rank_row):
            e = e_row.astype(jnp.int32)
            start = jnp.sum(jnp.where(e[None, :] == experts[:, None], pad_start[:, None], 0), axis=0)
            return start + rank_row.astype(jnp.int32)

        slots = [(one(rt[0], rt[4]), one(rt[1], rt[5])) for rt in rts]
        sa, sb, tag_bases, spare0, dest0 = [], [], [], n_slots, 0
        for tab, (a, b), n in zip(h3_tables, slots, n_real):
            n_spare = tab.shape[0] - n
            spare = spare0 + jnp.arange(n_spare, dtype=jnp.int32)
            sa.append(jnp.concatenate([a, spare]))
            sb.append(jnp.concatenate([b, spare + n_spare]))
            spare0 += 2 * n_spare
            tag_bases.append((dest0, dest0 + n))
            dest0 += 2 * n
        xs, tags = _scatter_rows2(tuple(h3_tables), tuple(sa), tuple(sb), tuple(tag_bases), spare0)
        ys = _experts(xs, n_slots, pad_start // BM, padded // BM, counts, *w_e)
        slot = jnp.arange(n_slots, dtype=jnp.int32)
        e_of = jnp.sum((jnp.cumsum(padded)[None, :] <= slot[:, None]).astype(jnp.int32), axis=1)
        e_of = jnp.minimum(e_of, N_EXPERTS - 1)
        is_e = e_of[:, None] == experts[None, :]
        live = slot - jnp.sum(jnp.where(is_e, pad_start, 0), axis=1) < jnp.sum(jnp.where(is_e, counts, 0), axis=1)
        dest = jnp.where(live, tags[:n_slots, 0], dest0 + slot)
        return _scatter_back(ys, dest, dest0 + n_slots)

    yg = moe_pass(cnt, [h3_p, h3_s], [rt_p, rt_s], [n_p, n_s])

    gf = row(g_final)
    y_p = _combine(x2_p, yg, rt_p, gf, TM_COMBINE, 0, n_p // TM_COMBINE)
    y_s = _combine(x2_s, yg, rt_s, gf, n_s, 2 * n_p // n_s, 2 * n_p // n_s + 1)

    return (y_p.reshape(1, n_p, D_MODEL),
            y_s.reshape(n_batch, t_len, D_MODEL),
            hist_p[HALO - HIST:].reshape(1, 1, HIST, D_CONV),
            hist_s.reshape(1, n_batch, HIST, D_CONV),
            k_p.reshape(1, 1, N_MEM, X_HEADS, X_HEAD_DIM),
            v_p.reshape(1, 1, N_MEM, X_HEADS, X_HEAD_DIM),
            sgv_s.reshape(1, n_batch, t_len, D_SG))
```

```python
import functools

import jax
import jax.numpy as jnp
from jax import lax
from jax.experimental import pallas as pl
from jax.experimental.pallas import tpu as pltpu
from jax.experimental.pallas import tpu_sc as plsc

D_MODEL = 1024
D_CONV = 512
D_SG = 512
CONV_WIDTH = 31
HIST = CONV_WIDTH - 1
SG_HEADS = 4
SG_HEAD_DIM = 128
SG_CHUNK = 128
N_MEM = 256
X_HEADS = 4
X_HEAD_DIM = 256
N_GROUPS = 4
EXPERTS_PER_GROUP = 8
N_EXPERTS = 32
D_EXPERT = 512
EPS = 1e-6

LANES = 128
SUBLANES = 8
SC_CORES = 2
SC_SUBCORES = 16
SC_WORKERS = SC_CORES * SC_SUBCORES
SC_LANES = 16
VMEM_LIMIT = 56 * 1024 * 1024

TM = 512
TM_COMBINE = 1024
ROUTE_ROWS = 2048
HALO = 32
SEG = TM // SUBLANES
SEG_HALO = 32
CONV_BLOCK = 16
CAST_ROWS = 64
BM = 256
X_LOOKAHEAD = 6
X_BUFS = X_LOOKAHEAD + 2
Y_BUFS = 6
ROW_DMA_PRIORITY = 1
SCATTER_CHUNK = 32
SCATTER_BUFS = 4
TAG_WORDS = 128
BACK_CHUNK = 56
BACK_BUFS = 4
LOGIT_LANES = 128

F32 = jnp.float32
BF16 = jnp.bfloat16


def _dot(a, b):
    return jnp.dot(a, b, preferred_element_type=F32)


def _rms(x, g):
    return x * lax.rsqrt(jnp.mean(x * x, axis=-1, keepdims=True) + EPS) * g


def _ln(x, g, b):
    mu = jnp.mean(x, axis=-1, keepdims=True)
    xc = x - mu
    var = jnp.mean(xc * xc, axis=-1, keepdims=True)
    return xc * lax.rsqrt(var + EPS) * g + b


def _sigmoid(x):
    return 1.0 / (1.0 + jnp.exp(-x))


def _pack_bf16_pairs(h):
    bits = lax.bitcast_convert_type(h, jnp.uint32)
    half = h.shape[1] // 2
    lo = lax.shift_right_logical(bits[:, :half], jnp.uint32(16))
    hi = bits[:, half:] & jnp.uint32(0xFFFF0000)
    return hi | lo


def _unpack_bf16_pairs_f32(p):
    lo = lax.bitcast_convert_type(lax.shift_left(p, jnp.uint32(16)), F32)
    hi = lax.bitcast_convert_type(p & jnp.uint32(0xFFFF0000), F32)
    return lo, hi


def _unpack_bf16_pairs(p):
    lo, hi = _unpack_bf16_pairs_f32(p)
    return lo.astype(BF16), hi.astype(BF16)


def _memkv_kernel(mem_ref, g_ref, wk_ref, wv_ref, k_ref, v_ref):
    m = _rms(mem_ref[...], g_ref[...]).astype(BF16)
    k_ref[...] = _dot(m, wk_ref[...].astype(BF16))
    v_ref[...] = _dot(m, wv_ref[...].astype(BF16))


def _memkv(mem, g_mem, w_mk, w_mv):
    return pl.pallas_call(
        _memkv_kernel,
        out_shape=(jax.ShapeDtypeStruct((N_MEM, D_MODEL), F32), jax.ShapeDtypeStruct((N_MEM, D_MODEL), F32)),
        compiler_params=pltpu.CompilerParams(vmem_limit_bytes=VMEM_LIMIT),
        name="memkv",
    )(mem, g_mem, w_mk, w_mv)


def _attn_heads(q, k, v, k_transposed):
    outs = []
    for h in range(X_HEADS):
        sl = slice(h * X_HEAD_DIM, (h + 1) * X_HEAD_DIM)
        if k_transposed:
            s = _dot(q[:, sl], k[sl, :])
        else:
            s = lax.dot_general(q[:, sl], k[:, sl], (((1,), (1,)), ((), ())), preferred_element_type=F32)
        s = s * (X_HEAD_DIM ** -0.5)
        s = s - jnp.max(s, axis=-1, keepdims=True)
        p = jnp.exp(s)
        p = p / jnp.sum(p, axis=-1, keepdims=True)
        outs.append(_dot(p.astype(BF16), v[:, sl]).astype(BF16))
    return jnp.concatenate(outs, axis=1)


def _route(logits, run, strict_lower):
    m = logits.shape[0]
    r = strict_lower.shape[0]
    lane = lax.broadcasted_iota(jnp.int32, (m, LOGIT_LANES), 1).astype(F32)
    neg = jnp.float32(-jnp.inf)
    big = jnp.float32(LOGIT_LANES)

    def first_argmax(vals):
        mx = jnp.max(vals, axis=-1, keepdims=True)
        idx = jnp.min(jnp.where(vals == mx, lane, big), axis=-1, keepdims=True)
        return mx, idx

    lg = jnp.where(lane < N_GROUPS, logits, neg)
    g_max, g_idx = first_argmax(lg)
    g_w = 1.0 / jnp.sum(jnp.exp(lg - g_max), axis=-1, keepdims=True)

    lo = N_GROUPS + g_idx * EXPERTS_PER_GROUP
    le = jnp.where((lane >= lo) & (lane < lo + EXPERTS_PER_GROUP), logits, neg)
    v1, i1 = first_argmax(le)
    v2, i2 = first_argmax(jnp.where(lane == i1, neg, le))
    t = jnp.exp(v2 - v1)
    gate1 = g_w / (1.0 + t)
    gate2 = g_w * t / (1.0 + t)
    e1 = i1 - N_GROUPS
    e2 = i2 - N_GROUPS

    oh1 = (lane == e1).astype(F32)
    oh2 = (lane == e2).astype(F32)
    oh = oh1 + oh2
    befores = []
    for r0 in range(0, m, r):
        oh_r = oh[r0:r0 + r, :]
        befores.append(_dot(strict_lower, oh_r.astype(BF16)) + run)
        run = run + jnp.sum(oh_r, axis=0, keepdims=True)
    before = befores[0] if len(befores) == 1 else jnp.concatenate(befores, axis=0)
    rank1 = jnp.sum(before * oh1, axis=-1, keepdims=True)
    rank2 = jnp.sum(before * oh2, axis=-1, keepdims=True)
    new_run = run

    rinfo = jnp.where(lane == 0, e1,
            jnp.where(lane == 1, e2,
            jnp.where(lane == 2, gate1,
            jnp.where(lane == 3, gate2,
            jnp.where(lane == 4, rank1,
            jnp.where(lane == 5, rank2, 0.0))))))
    return jnp.transpose(rinfo)[0:SUBLANES, :], new_run


def _conv_segments(a, w_ref, seg_ref, tail_ref, yseg_ref, conv_ref):
    sub = lax.broadcasted_iota(jnp.int32, (SUBLANES, LANES), 0)
    for lt in range(D_CONV // LANES):
        ls = slice(lt * LANES, (lt + 1) * LANES)
        for t0 in range(0, TM, SUBLANES):
            s, m = divmod(t0, SEG)
            seg_ref[lt, pl.ds((SEG_HALO + m) * SUBLANES + s, SUBLANES, stride=SUBLANES), :] = a[t0:t0 + SUBLANES, ls]
        for j in range(SEG_HALO):
            cur = seg_ref[lt, (SEG + j) * SUBLANES:(SEG + j + 1) * SUBLANES, :]
            prev = tail_ref[lt, j * SUBLANES:(j + 1) * SUBLANES, :]
            seg_ref[lt, j * SUBLANES:(j + 1) * SUBLANES, :] = jnp.where(
                sub == 0, pltpu.roll(prev, 1, axis=0), pltpu.roll(cur, 1, axis=0))
            tail_ref[lt, j * SUBLANES:(j + 1) * SUBLANES, :] = cur
        for m0 in range(0, SEG, CONV_BLOCK):
            acc = [jnp.zeros((SUBLANES, LANES), F32) for _ in range(CONV_BLOCK)]
            for idx in range(m0 - HIST, m0 + CONV_BLOCK):
                b = seg_ref[lt, (SEG_HALO + idx) * SUBLANES:(SEG_HALO + idx + 1) * SUBLANES, :]
                for m in range(max(m0, idx), min(m0 + CONV_BLOCK, idx + CONV_WIDTH)):
                    k = idx - m + HIST
                    acc[m - m0] = acc[m - m0] + b * w_ref[k:k + 1, ls]
            for m in range(m0, m0 + CONV_BLOCK):
                yseg_ref[lt, m * SUBLANES:(m + 1) * SUBLANES, :] = acc[m - m0]
        for t0 in range(0, TM, SUBLANES):
            s, m = divmod(t0, SEG)
            conv_ref[t0:t0 + SUBLANES, ls] = yseg_ref[lt, pl.ds(m * SUBLANES + s, SUBLANES, stride=SUBLANES), :]


def _cast_rows(src_ref, dst_ref):
    rows = src_ref.shape[0]

    def body(c, carry):
        r0 = pl.multiple_of(c * CAST_ROWS, CAST_ROWS)
        dst_ref[pl.ds(r0, CAST_ROWS), :] = src_ref[pl.ds(r0, CAST_ROWS), :].astype(BF16)
        return carry

    lax.fori_loop(0, rows // CAST_ROWS, body, 0)


def _trunk_prompt_kernel(x_ref, gmix_ref, win32_ref, convw_ref, convb_ref, lncg_ref, lncb_ref, lnvg_ref, lnvb_ref,
                         wsg_ref, bsg_ref, wout32_ref, gx_ref, wxq32_ref, kmem_ref, v_ref, wxo32_ref, gffn_ref, wr_ref,
                         br_ref,
                         x2_ref, h3_ref, logit_ref, hist_ref,
                         seg_ref, tail_ref, yseg_ref, conv_ref, win_ref, wout_ref, wxq_ref, wxo_ref):
    i = pl.program_id(0)

    @pl.when(i == 0)
    def _():
        tail_ref[...] = jnp.zeros(tail_ref.shape, F32)
        _cast_rows(win32_ref, win_ref)
        _cast_rows(wout32_ref, wout_ref)
        _cast_rows(wxq32_ref, wxq_ref)
        _cast_rows(wxo32_ref, wxo_ref)

    x = x_ref[...]
    h = _rms(x, gmix_ref[...]).astype(BF16)

    a_in = _dot(h, win_ref[:, 0:D_CONV])
    a_gate = _dot(h, win_ref[:, D_CONV:2 * D_CONV])
    a = a_in * _sigmoid(a_gate)
    hist_ref[...] = a[TM - HALO:, :]
    _conv_segments(a, convw_ref, seg_ref, tail_ref, yseg_ref, conv_ref)

    y = _ln(conv_ref[...] + convb_ref[...], lncg_ref[...], lncb_ref[...])
    a_out = (y * _sigmoid(y)).astype(BF16)

    u = _dot(h, win_ref[:, 2 * D_CONV:2 * D_CONV + D_SG])
    v = _ln(_dot(h, win_ref[:, 2 * D_CONV + D_SG:]), lnvg_ref[...], lnvb_ref[...]).astype(BF16)
    ri = lax.broadcasted_iota(jnp.int32, (SG_CHUNK, SG_CHUNK), 0)
    ci = lax.broadcasted_iota(jnp.int32, (SG_CHUNK, SG_CHUNK), 1)
    w_tril = [jnp.where(ci <= ri, wsg_ref[hh], 0.0).astype(BF16) for hh in range(SG_HEADS)]
    gate_rows = []
    for c in range(TM // SG_CHUNK):
        rs = slice(c * SG_CHUNK, (c + 1) * SG_CHUNK)
        heads = [_dot(w_tril[hh], v[rs, hh * SG_HEAD_DIM:(hh + 1) * SG_HEAD_DIM]) for hh in range(SG_HEADS)]
        gate_rows.append(jnp.concatenate(heads, axis=1) + bsg_ref[...])
    b_out = (u * jnp.concatenate(gate_rows, axis=0)).astype(BF16)

    x1 = x + _dot(a_out, wout_ref[0:D_CONV, :]) + _dot(b_out, wout_ref[D_CONV:, :])

    hx = _rms(x1, gx_ref[...]).astype(BF16)
    q = _dot(hx, wxq_ref[...]).astype(BF16)
    x2 = x1 + _dot(_attn_heads(q, kmem_ref[...], v_ref[...], False), wxo_ref[...])
    x2_ref[...] = x2

    h3 = _rms(x2, gffn_ref[...]).astype(BF16)
    h3_ref[...] = _pack_bf16_pairs(h3.astype(F32))
    logit_ref[...] = _dot(h3, wr_ref[...]) + br_ref[...]


def _router_kernel(logit_ref, upper_ref, ones_ref, rt_ref, cnt_ref, run_ref):
    @pl.when(pl.program_id(0) == 0)
    def _():
        run_ref[...] = jnp.zeros(run_ref.shape, F32)

    n = ROUTE_ROWS
    lt = jnp.transpose(logit_ref[...])
    neg = jnp.float32(-jnp.inf)
    big = jnp.float32(LOGIT_LANES)

    def first_argmax(vals, rows):
        mx = jnp.max(vals, axis=0, keepdims=True)
        idx = jnp.min(jnp.where(vals == mx, rows, big), axis=0, keepdims=True)
        return mx, idx

    row8 = lax.broadcasted_iota(jnp.int32, (SUBLANES, n), 0).astype(F32)
    lg = jnp.where(row8 < N_GROUPS, lt[0:SUBLANES, :], neg)
    g_max, g_idx = first_argmax(lg, row8)
    g_w = 1.0 / jnp.sum(jnp.exp(lg - g_max), axis=0, keepdims=True)

    n_rows = N_GROUPS + N_EXPERTS + (-(N_GROUPS + N_EXPERTS)) % SUBLANES
    rows = lax.broadcasted_iota(jnp.int32, (n_rows, n), 0).astype(F32)
    lo = N_GROUPS + g_idx * EXPERTS_PER_GROUP
    le = jnp.where((rows >= lo) & (rows < lo + EXPERTS_PER_GROUP), lt[0:n_rows, :], neg)
    v1, i1 = first_argmax(le, rows)
    v2, i2 = first_argmax(jnp.where(rows == i1, neg, le), rows)
    t = jnp.exp(v2 - v1)
    gate1 = g_w / (1.0 + t)
    gate2 = g_w * t / (1.0 + t)
    e1 = i1 - N_GROUPS
    e2 = i2 - N_GROUPS

    erow = lax.broadcasted_iota(jnp.int32, (LOGIT_LANES, n), 0).astype(F32)
    oh1 = (erow == e1).astype(F32)
    oh2 = (erow == e2).astype(F32)
    oh = (oh1 + oh2).astype(BF16)
    r = upper_ref.shape[0]
    run = run_ref[...]
    rank1, rank2 = [], []
    for c0 in range(0, n, r):
        cs = slice(c0, c0 + r)
        before = _dot(oh[:, cs], upper_ref[...]) + run
        rank1.append(jnp.sum(before * oh1[:, cs], axis=0, keepdims=True))
        rank2.append(jnp.sum(before * oh2[:, cs], axis=0, keepdims=True))
        run = run + _dot(oh[:, cs], ones_ref[...])
    run_ref[...] = run
    cnt_ref[...] = run[:, 0:LANES]

    sub = lax.broadcasted_iota(jnp.int32, (SUBLANES, n), 0)
    vals = (e1, e2, gate1, gate2, jnp.concatenate(rank1, axis=1), jnp.concatenate(rank2, axis=1))
    rt = jnp.zeros((SUBLANES, n), F32)
    for k, v in enumerate(vals):
        rt = jnp.where(sub == k, v, rt)
    rt_ref[...] = rt


def _router(logits, rank_block):
    n = logits.shape[0]
    assert n % ROUTE_ROWS == 0 and ROUTE_ROWS % rank_block == 0
    upper = jnp.triu(jnp.ones((rank_block, rank_block), BF16), 1)
    ones = jnp.ones((rank_block, rank_block), BF16)
    return pl.pallas_call(
        _router_kernel,
        grid=(n // ROUTE_ROWS,),
        in_specs=[pl.BlockSpec((ROUTE_ROWS, LOGIT_LANES), lambda i: (i, 0)),
                  pl.BlockSpec(upper.shape, lambda i: (0, 0)),
                  pl.BlockSpec(ones.shape, lambda i: (0, 0))],
        out_specs=(pl.BlockSpec((SUBLANES, ROUTE_ROWS), lambda i: (0, i)),
                   pl.BlockSpec((LOGIT_LANES, LANES), lambda i: (0, 0))),
        out_shape=(jax.ShapeDtypeStruct((SUBLANES, n), F32), jax.ShapeDtypeStruct((LOGIT_LANES, LANES), F32)),
        scratch_shapes=[pltpu.VMEM((LOGIT_LANES, rank_block), F32)],
        compiler_params=pltpu.CompilerParams(dimension_semantics=("arbitrary",), vmem_limit_bytes=VMEM_LIMIT),
        name="router",
    )(logits, upper, ones)


def _const_spec(shape):
    nd = len(shape)
    return pl.BlockSpec(shape, lambda i: (0,) * nd, pipeline_mode=pl.Buffered(1))


def _trunk_prompt(x, p):
    n = x.shape[0]
    assert n % TM == 0
    row = lambda w: pl.BlockSpec((TM, w), lambda i: (i, 0))
    consts = [p["g_mix"], p["w_in"], p["conv_w"], p["conv_b"], p["ln_conv_g"], p["ln_conv_b"], p["ln_v_g"],
              p["ln_v_b"], p["w_sg"], p["b_sg_rows"], p["w_out"], p["g_xattn"], p["w_xq"], p["k"], p["v"],
              p["w_xo"], p["g_ffn"], p["w_router"], p["b_router"]]
    return pl.pallas_call(
        _trunk_prompt_kernel,
        grid=(n // TM,),
        in_specs=[row(D_MODEL)] + [_const_spec(c.shape) for c in consts],
        out_specs=(row(D_MODEL), row(D_MODEL // 2), row(LOGIT_LANES),
                   pl.BlockSpec((HALO, D_CONV), lambda i: (0, 0))),
        out_shape=(jax.ShapeDtypeStruct((n, D_MODEL), F32),
                   jax.ShapeDtypeStruct((n, D_MODEL // 2), jnp.uint32),
                   jax.ShapeDtypeStruct((n, LOGIT_LANES), F32),
                   jax.ShapeDtypeStruct((HALO, D_CONV), F32)),
        scratch_shapes=[pltpu.VMEM((D_CONV // LANES, (SEG_HALO + SEG) * SUBLANES, LANES), F32),
                        pltpu.VMEM((D_CONV // LANES, SEG_HALO * SUBLANES, LANES), F32),
                        pltpu.VMEM((D_CONV // LANES, TM, LANES), F32),
                        pltpu.VMEM((TM, D_CONV), F32),
                        pltpu.VMEM(p["w_in"].shape, BF16), pltpu.VMEM(p["w_out"].shape, BF16),
                        pltpu.VMEM(p["w_xq"].shape, BF16), pltpu.VMEM(p["w_xo"].shape, BF16)],
        compiler_params=pltpu.CompilerParams(dimension_semantics=("arbitrary",), vmem_limit_bytes=VMEM_LIMIT),
        name="trunk_prompt",
    )(x, *consts)


def _trunk_sample_kernel(n_batch, t_len,
                         x_ref, hist_in_ref, run_in_ref, gmix_ref, win_ref, convw_ref, convb_ref, lncg_ref, lncb_ref,
                         lnvg_ref, lnvb_ref, wsgbd_ref, bsg_ref, wout_ref, gx_ref, wxq_ref, kmem_ref, v_ref, wxo_ref,
                         gffn_ref, wr_ref, br_ref, lower_ref,
                         x2_ref, h3_ref, rt_ref, hist_ref, sgv_ref, cnt_ref,
                         ext_ref, conv_ref, att_ref):
    x = x_ref[...]
    h = _rms(x, gmix_ref[...]).astype(BF16)
    z = _dot(h, win_ref[...].astype(BF16))
    a = z[:, 0:D_CONV] * _sigmoid(z[:, D_CONV:2 * D_CONV])
    ext_len = HIST + t_len
    for b in range(n_batch):
        ext_ref[b, 0:HIST, :] = hist_in_ref[b]
        ext_ref[b, HIST:ext_len, :] = a[b * t_len:(b + 1) * t_len, :]
    for b in range(n_batch):
        acc = jnp.zeros((t_len, D_CONV), F32)
        for k in range(CONV_WIDTH):
            acc = acc + ext_ref[b, k:k + t_len, :] * convw_ref[k:k + 1, :]
        conv_ref[b * t_len:(b + 1) * t_len, :] = acc
        hist_ref[b] = ext_ref[b, ext_len - HIST:ext_len, :]

    y = _ln(conv_ref[...] + convb_ref[...], lncg_ref[...], lncb_ref[...])
    a_out = (y * _sigmoid(y)).astype(BF16)

    u = z[:, 2 * D_CONV:2 * D_CONV + D_SG]
    v = _ln(z[:, 2 * D_CONV + D_SG:], lnvg_ref[...], lnvb_ref[...])
    sgv_ref[...] = v
    vb = v.astype(BF16)
    heads = [_dot(wsgbd_ref[hh], vb[:, hh * SG_HEAD_DIM:(hh + 1) * SG_HEAD_DIM]) for hh in range(SG_HEADS)]
    b_out = (u * (jnp.concatenate(heads, axis=1) + bsg_ref[...])).astype(BF16)

    x1 = (x + _dot(a_out, wout_ref[0:D_CONV, :].astype(BF16))
          + _dot(b_out, wout_ref[D_CONV:, :].astype(BF16)))

    hx = _rms(x1, gx_ref[...]).astype(BF16)
    q = _dot(hx, wxq_ref[...].astype(BF16)).astype(BF16)
    for b in range(n_batch):
        rs = slice(b * t_len, (b + 1) * t_len)
        att_ref[rs, :] = _attn_heads(q[rs, :], kmem_ref[b], v_ref[b], True)
    x2 = x1 + _dot(att_ref[...], wxo_ref[...].astype(BF16))
    x2_ref[...] = x2

    h3 = _rms(x2, gffn_ref[...]).astype(BF16)
    m = n_batch * t_len
    h3_ref[0:m, :] = _pack_bf16_pairs(h3.astype(F32))
    if h3_ref.shape[0] > m:
        h3_ref[m:, :] = jnp.zeros((h3_ref.shape[0] - m, D_MODEL // 2), jnp.uint32)
    rt, new_run = _route(_dot(h3, wr_ref[...]) + br_ref[...], run_in_ref[...], lower_ref[...])
    rt_ref[...] = rt
    cnt_ref[...] = new_run


def _trunk_sample(x, hist, run, p, n_batch, t_len):
    m = n_batch * t_len
    args = [x, hist, run, p["g_mix"], p["w_in"], p["conv_w"], p["conv_b"], p["ln_conv_g"], p["ln_conv_b"],
            p["ln_v_g"], p["ln_v_b"], p["w_sg_bd"], p["b_sg_rows_s"], p["w_out"], p["g_xattn"], p["w_xq"],
            p["k_s"], p["v_s"], p["w_xo"], p["g_ffn"], p["w_router"], p["b_router"], p["lower"]]
    return pl.pallas_call(
        functools.partial(_trunk_sample_kernel, n_batch, t_len),
        out_shape=(jax.ShapeDtypeStruct((m, D_MODEL), F32),
                   jax.ShapeDtypeStruct((-(-m // (SC_WORKERS * SUBLANES)) * SC_WORKERS * SUBLANES, D_MODEL // 2),
                                        jnp.uint32),
                   jax.ShapeDtypeStruct((SUBLANES, m), F32),
                   jax.ShapeDtypeStruct((n_batch, HIST, D_CONV), F32),
                   jax.ShapeDtypeStruct((m, D_SG), F32),
                   jax.ShapeDtypeStruct((1, LOGIT_LANES), F32)),
        scratch_shapes=[pltpu.VMEM((n_batch, HIST + t_len, D_CONV), F32),
                        pltpu.VMEM((m, D_CONV), F32),
                        pltpu.VMEM((m, D_MODEL), BF16)],
        compiler_params=pltpu.CompilerParams(vmem_limit_bytes=VMEM_LIMIT),
        name="trunk_sample",
    )(*args)


def _sc_worker_id():
    return lax.axis_index("s") * SC_CORES + lax.axis_index("c")


def _sc_chunk(per_w, max_chunk):
    assert per_w % SUBLANES == 0 and max_chunk <= LANES
    return max(c for c in range(SUBLANES, max_chunk + 1, SUBLANES) if per_w % c == 0)


def _sc_scatter_rows2(tables, slots_a, slots_b, tag_bases, n_rows_out, max_chunk):
    d, dtype = tables[0].shape[1], tables[0].dtype
    plans = []
    for t in tables:
        per_w = t.shape[0] // SC_WORKERS
        assert per_w * SC_WORKERS == t.shape[0]
        chunk = _sc_chunk(per_w, max_chunk)
        plans.append((per_w, chunk, per_w // chunk))
    cmax = max(c for _, c, _ in plans)
    n_t = len(tables)
    mesh = plsc.VectorSubcoreMesh(core_axis_name="c", subcore_axis_name="s")

    nb = SCATTER_BUFS
    lag = nb // 3
    scratch = []
    for _, chunk, _ in plans:
        for _ in range(nb):
            scratch += [pltpu.VMEM((chunk,), jnp.int32), pltpu.VMEM((chunk,), jnp.int32)]
    scratch += [pltpu.VMEM((cmax, d), dtype)] * nb
    scratch += [pltpu.VMEM((cmax, TAG_WORDS), jnp.int32)] * (2 * nb)
    scratch += [pltpu.SemaphoreType.DMA] * (2 * nb)

    @functools.partial(pl.kernel, mesh=mesh,
                       out_type=(jax.ShapeDtypeStruct((n_rows_out, d), dtype),
                                 jax.ShapeDtypeStruct((n_rows_out, TAG_WORDS), jnp.int32)),
                       scratch_types=scratch)
    def scatter(*refs):
        tab_hbm = refs[0:n_t]
        sa_hbm = refs[n_t:2 * n_t]
        sb_hbm = refs[2 * n_t:3 * n_t]
        out_hbm, tag_hbm = refs[3 * n_t], refs[3 * n_t + 1]
        sc = refs[3 * n_t + 2:]
        idx_refs = sc[:2 * nb * n_t]
        rows = sc[2 * nb * n_t:2 * nb * n_t + nb]
        tagbufs = sc[2 * nb * n_t + nb:2 * nb * n_t + 3 * nb]
        lsem = sc[2 * nb * n_t + 3 * nb:2 * nb * n_t + 4 * nb]
        ssem = sc[2 * nb * n_t + 4 * nb:]
        wid = _sc_worker_id()

        work = []
        for t, (per_w, chunk, n_chunks) in enumerate(plans):
            for j in range(n_chunks):
                work.append((t, wid * per_w + j * chunk, chunk))

        def parts(k):
            t, off, chunk = work[k]
            b = k % nb
            ia, ib = idx_refs[2 * nb * t + 2 * b], idx_refs[2 * nb * t + 2 * b + 1]
            full = chunk == cmax
            rv = rows[b] if full else rows[b].at[pl.ds(0, chunk)]
            ta = tagbufs[2 * b] if full else tagbufs[2 * b].at[pl.ds(0, chunk)]
            tb = tagbufs[2 * b + 1] if full else tagbufs[2 * b + 1].at[pl.ds(0, chunk)]
            return t, off, chunk, b, ia, ib, rv, ta, tb

        def start_load(k):
            t, off, chunk, b, ia, ib, rv, ta, tb = parts(k)
            return (pltpu.async_copy(tab_hbm[t].at[pl.ds(off, chunk)], rv, lsem[b]),
                    pltpu.async_copy(sa_hbm[t].at[pl.ds(off, chunk)], ia, lsem[b]),
                    pltpu.async_copy(sb_hbm[t].at[pl.ds(off, chunk)], ib, lsem[b]))

        def start_scatter(k):
            t, off, chunk, b, ia, ib, rv, ta, tb = parts(k)
            base_a, base_b = tag_bases[t]
            for r in range(chunk):
                row_id = (off + r).astype(jnp.int32)
                tagbufs[2 * b][r, pl.ds(0, SC_LANES)] = jnp.zeros((SC_LANES,), jnp.int32) + (base_a + row_id)
                tagbufs[2 * b + 1][r, pl.ds(0, SC_LANES)] = jnp.zeros((SC_LANES,), jnp.int32) + (base_b + row_id)
            return (pltpu.async_copy(rv, out_hbm.at[ia], ssem[b]), pltpu.async_copy(rv, out_hbm.at[ib], ssem[b]),
                    pltpu.async_copy(ta, tag_hbm.at[ia], ssem[b]), pltpu.async_copy(tb, tag_hbm.at[ib], ssem[b]))

        loads, scatters = {}, {}
        for k in range(len(work) + lag):
            if k < len(work):
                if k >= nb:
                    for c in scatters.pop(k - nb):
                        c.wait()
                loads[k] = start_load(k)
            w = k - lag
            if w >= 0:
                for c in loads.pop(w):
                    c.wait()
                scatters[w] = start_scatter(w)
        for w in sorted(scatters):
            for c in scatters[w]:
                c.wait()

    return scatter(*tables, *slots_a, *slots_b)


def _sc_scatter_back(ys, dest, n_rows_out):
    n_rows, d = ys.shape
    per_w = n_rows // SC_WORKERS
    assert per_w * SC_WORKERS == n_rows
    chunk = _sc_chunk(per_w, BACK_CHUNK)
    n_chunks = per_w // chunk
    nb = BACK_BUFS
    lag = nb // 2
    mesh = plsc.VectorSubcoreMesh(core_axis_name="c", subcore_axis_name="s")

    @functools.partial(
        pl.kernel, mesh=mesh,
        out_type=jax.ShapeDtypeStruct((n_rows_out, d), ys.dtype),
        scratch_types=([pltpu.VMEM((chunk,), jnp.int32)] * nb + [pltpu.VMEM((chunk, d), ys.dtype)] * nb
                       + [pltpu.SemaphoreType.DMA] * (2 * nb)),
    )
    def scatter_back(ys_hbm, dest_hbm, out_hbm, *rest):
        idx = rest[:nb]
        rows = rest[nb:2 * nb]
        lsem = rest[2 * nb:3 * nb]
        ssem = rest[3 * nb:]
        base = _sc_worker_id() * per_w

        loads, scatters = {}, {}
        for k in range(n_chunks + lag):
            if k < n_chunks:
                b = k % nb
                if k >= nb:
                    scatters.pop(k - nb).wait()
                off = base + k * chunk
                loads[k] = (pltpu.async_copy(ys_hbm.at[pl.ds(off, chunk)], rows[b], lsem[b]),
                            pltpu.async_copy(dest_hbm.at[pl.ds(off, chunk)], idx[b], lsem[b]))
            w = k - lag
            if w >= 0:
                b = w % nb
                for c in loads.pop(w):
                    c.wait()
                scatters[w] = pltpu.async_copy(rows[b], out_hbm.at[idx[b]], ssem[b])
        for w in sorted(scatters):
            scatters[w].wait()

    return scatter_back(ys, dest)


def _experts_kernel(dump_base, first_ref, nblk_ref, cnt_ref, tot_ref, xs_hbm, tag_hbm, wg_ref, wu_ref, wd_ref,
                    ys_hbm, dest_hbm, xbuf, tbuf, ybuf, dbuf, wg_bf, wu_bf, wd_bf, in_sem, tin_sem, out_sem, dout_sem):
    e = pl.program_id(0)
    nb = nblk_ref[e]
    first = first_ref[e]
    cnt = cnt_ref[e]
    total = tot_ref[0]
    half = D_MODEL // 2

    def in_copies(gb):
        slot = lax.rem(gb, X_BUFS)
        return (pltpu.make_async_copy(xs_hbm.at[pl.ds(gb * BM, BM)], xbuf.at[slot], in_sem.at[slot]),
                pltpu.make_async_copy(tag_hbm.at[pl.ds(gb * BM, BM)], tbuf.at[slot], tin_sem.at[slot]))

    def out_copies(gb):
        slot = lax.rem(gb, Y_BUFS)
        return (pltpu.make_async_copy(ybuf.at[slot], ys_hbm.at[pl.ds(gb * BM, BM)], out_sem.at[slot]),
                pltpu.make_async_copy(dbuf.at[slot], dest_hbm.at[pl.ds(gb * SUBLANES, SUBLANES)], dout_sem.at[slot]))

    def start_in(gb):
        for c in in_copies(gb):
            c.start(priority=ROW_DMA_PRIORITY)

    def start_out(gb):
        for c in out_copies(gb):
            c.start(priority=ROW_DMA_PRIORITY)

    def wait_out(gb):
        for c in out_copies(gb):
            c.wait()

    @pl.when(nb > 0)
    def _():
        @pl.when(first == 0)
        def _():
            for k in range(X_LOOKAHEAD):
                @pl.when(k < total)
                def _():
                    start_in(k)

        wg_bf[...] = wg_ref[0].astype(BF16)
        wu_bf[...] = wu_ref[0].astype(BF16)
        wd_bf[...] = wd_ref[0].astype(BF16)

        def acquire(gb):
            @pl.when(gb + X_LOOKAHEAD < total)
            def _():
                start_in(gb + X_LOOKAHEAD)

            for c in in_copies(gb):
                c.wait()

            @pl.when(gb >= Y_BUFS)
            def _():
                wait_out(gb - Y_BUFS)

        def ffn(gb, j):
            n_live = cnt - j * BM
            tags_t = jnp.transpose(tbuf[lax.rem(gb, X_BUFS)].astype(F32))
            lane = lax.broadcasted_iota(jnp.int32, (SUBLANES, BM), 1)
            own = (dump_base + gb * BM + lane).astype(F32)
            dest = jnp.where(lane < n_live, jnp.broadcast_to(tags_t[0:1, :], (SUBLANES, BM)), own)
            dbuf[lax.rem(gb, Y_BUFS)] = dest.astype(jnp.int32)
            live = lax.broadcasted_iota(jnp.int32, (BM, half), 0) < n_live
            lo, hi = _unpack_bf16_pairs(jnp.where(live, xbuf[lax.rem(gb, X_BUFS)], jnp.uint32(0)))
            g = _dot(lo, wg_bf[0:half, :]) + _dot(hi, wg_bf[half:, :])
            u = _dot(lo, wu_bf[0:half, :]) + _dot(hi, wu_bf[half:, :])
            hm = (g * _sigmoid(g) * u).astype(BF16)
            y = _dot(hm, wd_bf[...])
            ybuf[lax.rem(gb, Y_BUFS)] = _pack_bf16_pairs(y.astype(BF16).astype(F32))

        def block_pair(jp, carry):
            j0 = 2 * jp
            g0 = first + j0
            acquire(g0)
            acquire(g0 + 1)
            ffn(g0, j0)
            ffn(g0 + 1, j0 + 1)
            start_out(g0)
            start_out(g0 + 1)
            return carry

        lax.fori_loop(0, nb // 2, block_pair, 0)

        @pl.when(lax.rem(nb, 2) == 1)
        def _():
            gl = first + nb - 1
            acquire(gl)
            ffn(gl, nb - 1)
            start_out(gl)

        @pl.when(first + nb == total)
        def _():
            for k in range(Y_BUFS):
                @pl.when(total - 1 - k >= 0)
                def _():
                    wait_out(total - 1 - k)


def _experts(xs, tags, n_rows_out, dump_base, first_block, n_blocks_e, counts, w_eg, w_eu, w_ed):
    w_map = lambda e, fb, nb, ct, tot: (e, 0, 0)
    half = D_MODEL // 2
    total = jnp.sum(n_blocks_e).astype(jnp.int32).reshape(1)
    n_blocks = n_rows_out // BM
    return pl.pallas_call(
        functools.partial(_experts_kernel, dump_base),
        grid_spec=pltpu.PrefetchScalarGridSpec(
            num_scalar_prefetch=4,
            grid=(N_EXPERTS,),
            in_specs=[pl.BlockSpec(memory_space=pl.ANY),
                      pl.BlockSpec(memory_space=pl.ANY),
                      pl.BlockSpec((1, D_MODEL, D_EXPERT), w_map),
                      pl.BlockSpec((1, D_MODEL, D_EXPERT), w_map),
                      pl.BlockSpec((1, D_EXPERT, D_MODEL), w_map)],
            out_specs=(pl.BlockSpec(memory_space=pl.ANY), pl.BlockSpec(memory_space=pl.ANY)),
            scratch_shapes=[pltpu.VMEM((X_BUFS, BM, half), jnp.uint32), pltpu.VMEM((X_BUFS, BM, TAG_WORDS), jnp.int32),
                            pltpu.VMEM((Y_BUFS, BM, half), jnp.uint32), pltpu.VMEM((Y_BUFS, SUBLANES, BM), jnp.int32),
                            pltpu.VMEM((D_MODEL, D_EXPERT), BF16), pltpu.VMEM((D_MODEL, D_EXPERT), BF16),
                            pltpu.VMEM((D_EXPERT, D_MODEL), BF16),
                            pltpu.SemaphoreType.DMA((X_BUFS,)), pltpu.SemaphoreType.DMA((X_BUFS,)),
                            pltpu.SemaphoreType.DMA((Y_BUFS,)), pltpu.SemaphoreType.DMA((Y_BUFS,))]),
        out_shape=(jax.ShapeDtypeStruct((n_rows_out, half), jnp.uint32),
                   jax.ShapeDtypeStruct((n_blocks * SUBLANES, BM), jnp.int32)),
        compiler_params=pltpu.CompilerParams(dimension_semantics=("arbitrary",), vmem_limit_bytes=VMEM_LIMIT),
        name="experts",
    )(first_block, n_blocks_e, counts, total, xs, tags, w_eg, w_eu, w_ed)


def _combine_kernel(x2_ref, y1_ref, y2_ref, rt_ref, g_ref, o_ref):
    rt = rt_ref[...]
    r = jnp.transpose(jnp.concatenate([rt, jnp.zeros((LANES - rt.shape[0], rt.shape[1]), F32)], axis=0))
    g1, g2 = r[:, 2:3], r[:, 3:4]
    half = D_MODEL // 2
    y1_lo, y1_hi = _unpack_bf16_pairs_f32(y1_ref[...])
    y2_lo, y2_hi = _unpack_bf16_pairs_f32(y2_ref[...])
    x_lo = x2_ref[:, 0:half] + g1 * y1_lo + g2 * y2_lo
    x_hi = x2_ref[:, half:] + g1 * y1_hi + g2 * y2_hi
    ms = (jnp.sum(x_lo * x_lo, axis=-1, keepdims=True) + jnp.sum(x_hi * x_hi, axis=-1, keepdims=True)) / D_MODEL
    inv = lax.rsqrt(ms + EPS)
    o_ref[:, 0:half] = x_lo * inv * g_ref[:, 0:half]
    o_ref[:, half:] = x_hi * inv * g_ref[:, half:]


def _combine(x2, yg, rt, g_final, tm, blk1, blk2):
    n = x2.shape[0]
    return pl.pallas_call(
        _combine_kernel,
        grid=(n // tm,),
        in_specs=[pl.BlockSpec((tm, D_MODEL), lambda i: (i, 0)),
                  pl.BlockSpec((tm, D_MODEL // 2), lambda i: (blk1 + i, 0)),
                  pl.BlockSpec((tm, D_MODEL // 2), lambda i: (blk2 + i, 0)),
                  pl.BlockSpec((SUBLANES, tm), lambda i: (0, i)),
                  pl.BlockSpec((1, D_MODEL), lambda i: (0, 0))],
        out_specs=pl.BlockSpec((tm, D_MODEL), lambda i: (i, 0)),
        out_shape=jax.ShapeDtypeStruct((n, D_MODEL), F32),
        compiler_params=pltpu.CompilerParams(dimension_semantics=("arbitrary",), vmem_limit_bytes=VMEM_LIMIT),
        name="combine",
    )(x2, yg, yg, rt, g_final)


def _scatter_back(ys, dest, n_rows_out):
    return _sc_scatter_back(ys, dest, n_rows_out)


def _scatter_rows2(tables, slots_a, slots_b, tag_bases, n_rows_out):
    return _sc_scatter_rows2(tables, slots_a, slots_b, tag_bases, n_rows_out, SCATTER_CHUNK)


def kernel(x_prompt, x_sample, mem_prompt, state_conv, cache_mem_k, cache_mem_v, g_mix, w_in, conv_w, conv_b, ln_conv_g, ln_conv_b, ln_v_g, ln_v_b, w_sg, b_sg, w_out, g_mem, w_mk, w_mv, g_xattn, w_xq, w_xo, g_ffn, w_router_group, b_router_group, w_router_expert, b_router_expert, w_expert_gate, w_expert_up, w_expert_down, g_final):
    assert x_prompt.shape[0] == 1 and g_mix.shape[0] == 1
    n_p = x_prompt.shape[1]
    n_batch, t_len = x_sample.shape[0], x_sample.shape[1]
    n_s = n_batch * t_len
    row = lambda a: a.reshape(1, -1)

    w_router = jnp.concatenate(
        [w_router_group[0], jnp.transpose(w_router_expert[0], (1, 0, 2)).reshape(D_MODEL, N_EXPERTS)], axis=1)
    w_router = jnp.pad(w_router, ((0, 0), (0, LOGIT_LANES - w_router.shape[1]))).astype(BF16)
    b_router = jnp.pad(jnp.concatenate([b_router_group[0], b_router_expert[0].reshape(-1)]),
                       (0, LOGIT_LANES - N_GROUPS - N_EXPERTS)).reshape(1, LOGIT_LANES)
    tril_t = jnp.tril(jnp.ones((t_len, t_len), bool))
    w_sg_t = jnp.where(tril_t, w_sg[0][:, :t_len, :t_len], 0.0)
    eye_b = jnp.eye(n_batch, dtype=F32)
    w_sg_bd = jnp.einsum("ab,hij->haibj", eye_b, w_sg_t).reshape(SG_HEADS, n_s, n_s).astype(BF16)
    p = {
        "g_mix": row(g_mix[0]), "w_in": w_in[0],
        "conv_w": jnp.pad(conv_w[0], ((0, 1), (0, 0))), "conv_b": row(conv_b[0]),
        "ln_conv_g": row(ln_conv_g[0]), "ln_conv_b": row(ln_conv_b[0]),
        "ln_v_g": row(ln_v_g[0]), "ln_v_b": row(ln_v_b[0]),
        "w_sg": w_sg[0],
        "b_sg_rows": jnp.repeat(b_sg[0].T, SG_HEAD_DIM, axis=1),
        "w_sg_bd": w_sg_bd,
        "b_sg_rows_s": jnp.tile(jnp.repeat(b_sg[0][:, :t_len].T, SG_HEAD_DIM, axis=1), (n_batch, 1)),
        "w_out": w_out[0], "g_xattn": row(g_xattn[0]),
        "w_xq": w_xq[0], "w_xo": w_xo[0], "g_ffn": row(g_ffn[0]),
        "w_router": w_router, "b_router": b_router,
        "lower": jnp.tril(jnp.ones((n_s, n_s), BF16), -1),
    }

    k_p, v_p = _memkv(mem_prompt[0], row(g_mem[0]), w_mk[0], w_mv[0])
    p["k"] = k_p.astype(BF16)
    p["v"] = v_p.astype(BF16)
    p["k_s"] = jnp.transpose(cache_mem_k[0].astype(BF16), (0, 2, 3, 1)).reshape(n_batch, D_MODEL, N_MEM)
    p["v_s"] = cache_mem_v[0].astype(BF16).reshape(n_batch, N_MEM, D_MODEL)

    assert n_p % n_s == 0
    x2_p, h3_p, logits_p, hist_p = _trunk_prompt(x_prompt[0], p)
    rt_p, cnt_t = _router(logits_p, TM)
    cnt_p = cnt_t[:, 0].reshape(1, LOGIT_LANES)
    x2_s, h3_s, rt_s, hist_s, sgv_s, cnt = _trunk_sample(
        x_sample.reshape(n_s, D_MODEL), state_conv[0], cnt_p, p, n_batch, t_len)

    experts = jnp.arange(N_EXPERTS, dtype=jnp.int32)
    w_e = (w_expert_gate[0], w_expert_up[0], w_expert_down[0])

    def moe_pass(cnt, h3_tables, rts, n_real):
        n_tot = sum(n_real)
        n_slots = -(-(n_tot * 2) // BM) * BM + N_EXPERTS * BM
        counts = cnt[0, :N_EXPERTS].astype(jnp.int32)
        padded = (counts + BM - 1) // BM * BM
        pad_start = jnp.cumsum(padded) - padded

        def one(e_row, rank_row):
            e = e_row.astype(jnp.int32)
            start = jnp.sum(jnp.where(e[None, :] == experts[:, None], pad_start[:, None], 0), axis=0)
            return start + rank_row.astype(jnp.int32)

        slots = [(one(rt[0], rt[4]), one(rt[1], rt[5])) for rt in rts]
        sa, sb, tag_bases, spare0, dest0 = [], [], [], n_slots, 0
        for tab, (a, b), n in zip(h3_tables, slots, n_real):
            n_spare = tab.shape[0] - n
            spare = spare0 + jnp.arange(n_spare, dtype=jnp.int32)
            sa.append(jnp.concatenate([a, spare]))
            sb.append(jnp.concatenate([b, spare + n_spare]))
            spare0 += 2 * n_spare
            tag_bases.append((dest0, dest0 + n))
            dest0 += 2 * n
        xs, tags = _scatter_rows2(tuple(h3_tables), tuple(sa), tuple(sb), tuple(tag_bases), spare0)
        ys, dest_blocks = _experts(xs, tags, n_slots, dest0, pad_start // BM, padded // BM, counts, *w_e)
        slot = jnp.arange(n_slots, dtype=jnp.int32)
        dest = dest_blocks.reshape(n_slots // BM, SUBLANES, BM)[:, 0, :].reshape(-1)
        dest = jnp.where(slot < jnp.sum(padded), dest, dest0 + slot)
        return _scatter_back(ys, dest, dest0 + n_slots)

    yg = moe_pass(cnt, [h3_p, h3_s], [rt_p, rt_s], [n_p, n_s])

    gf = row(g_final)
    y_p = _combine(x2_p, yg, rt_p, gf, TM_COMBINE, 0, n_p // TM_COMBINE)
    y_s = _combine(x2_s, yg, rt_s, gf, n_s, 2 * n_p // n_s, 2 * n_p // n_s + 1)

    return (y_p.reshape(1, n_p, D_MODEL),
            y_s.reshape(n_batch, t_len, D_MODEL),
            hist_p[HALO - HIST:].reshape(1, 1, HIST, D_CONV),
            hist_s.reshape(1, n_batch, HIST, D_CONV),
            k_p.reshape(1, 1, N_MEM, X_HEADS, X_HEAD_DIM),
            v_p.reshape(1, 1, N_MEM, X_HEADS, X_HEAD_DIM),
            sgv_s.reshape(1, n_batch, t_len, D_SG))
```

```python
import functools

import jax
import jax.numpy as jnp
from jax import lax
from jax.experimental import pallas as pl
from jax.experimental.pallas import tpu as pltpu
from jax.experimental.pallas import tpu_sc as plsc

D_MODEL = 1024
D_CONV = 512
D_SG = 512
CONV_WIDTH = 31
HIST = CONV_WIDTH - 1
SG_HEADS = 4
SG_HEAD_DIM = 128
SG_CHUNK = 128
N_MEM = 256
X_HEADS = 4
X_HEAD_DIM = 256
N_GROUPS = 4
EXPERTS_PER_GROUP = 8
N_EXPERTS = 32
D_EXPERT = 512
EPS = 1e-6

LANES = 128
SUBLANES = 8
SC_CORES = 2
SC_SUBCORES = 16
SC_WORKERS = SC_CORES * SC_SUBCORES
SC_LANES = 16
VMEM_LIMIT = 56 * 1024 * 1024

TM = 512
TM_COMBINE = 1024
ROUTE_ROWS = 2048
HALO = 32
SEG = TM // SUBLANES
SEG_HALO = 32
CONV_BLOCK = 16
CAST_ROWS = 64
BM = 256
X_LOOKAHEAD = 6
X_BUFS = X_LOOKAHEAD + 2
Y_BUFS = 6
ROW_DMA_PRIORITY = 1
SCATTER_CHUNK = 32
SCATTER_BUFS = 4
TAG_WORDS = 128
BACK_CHUNK = 56
BACK_BUFS = 4
LOGIT_LANES = 128

F32 = jnp.float32
BF16 = jnp.bfloat16


def _dot(a, b):
    return jnp.dot(a, b, preferred_element_type=F32)


def _rms(x, g):
    return x * lax.rsqrt(jnp.mean(x * x, axis=-1, keepdims=True) + EPS) * g


def _ln(x, g, b):
    mu = jnp.mean(x, axis=-1, keepdims=True)
    xc = x - mu
    var = jnp.mean(xc * xc, axis=-1, keepdims=True)
    return xc * lax.rsqrt(var + EPS) * g + b


def _sigmoid(x):
    return 1.0 / (1.0 + jnp.exp(-x))


def _pack_bf16_pairs(h):
    bits = lax.bitcast_convert_type(h, jnp.uint32)
    half = h.shape[1] // 2
    lo = lax.shift_right_logical(bits[:, :half], jnp.uint32(16))
    hi = bits[:, half:] & jnp.uint32(0xFFFF0000)
    return hi | lo


def _unpack_bf16_pairs_f32(p):
    lo = lax.bitcast_convert_type(lax.shift_left(p, jnp.uint32(16)), F32)
    hi = lax.bitcast_convert_type(p & jnp.uint32(0xFFFF0000), F32)
    return lo, hi


def _unpack_bf16_pairs(p):
    lo, hi = _unpack_bf16_pairs_f32(p)
    return lo.astype(BF16), hi.astype(BF16)


def _memkv_kernel(mem_ref, g_ref, wk_ref, wv_ref, k_ref, v_ref, kbf_ref, vbf_ref):
    m = _rms(mem_ref[...], g_ref[...]).astype(BF16)
    k = _dot(m, wk_ref[...].astype(BF16))
    v = _dot(m, wv_ref[...].astype(BF16))
    k_ref[...] = k
    v_ref[...] = v
    kbf_ref[...] = k.astype(BF16)
    vbf_ref[...] = v.astype(BF16)


def _memkv(mem, g_mem, w_mk, w_mv):
    return pl.pallas_call(
        _memkv_kernel,
        out_shape=(jax.ShapeDtypeStruct((N_MEM, D_MODEL), F32), jax.ShapeDtypeStruct((N_MEM, D_MODEL), F32),
                   jax.ShapeDtypeStruct((N_MEM, D_MODEL), BF16), jax.ShapeDtypeStruct((N_MEM, D_MODEL), BF16)),
        compiler_params=pltpu.CompilerParams(vmem_limit_bytes=VMEM_LIMIT),
        name="memkv",
    )(mem, g_mem, w_mk, w_mv)


def _attn_heads(q, k, v, transpose_k):
    outs = []
    for h in range(X_HEADS):
        sl = slice(h * X_HEAD_DIM, (h + 1) * X_HEAD_DIM)
        if transpose_k:
            s = _dot(q[:, sl], jnp.transpose(k[:, sl]).astype(BF16))
        else:
            s = lax.dot_general(q[:, sl], k[:, sl].astype(BF16), (((1,), (1,)), ((), ())),
                                preferred_element_type=F32)
        s = s * (X_HEAD_DIM ** -0.5)
        s = s - jnp.max(s, axis=-1, keepdims=True)
        p = jnp.exp(s)
        p = p / jnp.sum(p, axis=-1, keepdims=True)
        outs.append(_dot(p.astype(BF16), v[:, sl].astype(BF16)).astype(BF16))
    return jnp.concatenate(outs, axis=1)


def _route(logits, run, strict_lower):
    m = logits.shape[0]
    r = strict_lower.shape[0]
    lane = lax.broadcasted_iota(jnp.int32, (m, LOGIT_LANES), 1).astype(F32)
    neg = jnp.float32(-jnp.inf)
    big = jnp.float32(LOGIT_LANES)

    def first_argmax(vals):
        mx = jnp.max(vals, axis=-1, keepdims=True)
        idx = jnp.min(jnp.where(vals == mx, lane, big), axis=-1, keepdims=True)
        return mx, idx

    lg = jnp.where(lane < N_GROUPS, logits, neg)
    g_max, g_idx = first_argmax(lg)
    g_w = 1.0 / jnp.sum(jnp.exp(lg - g_max), axis=-1, keepdims=True)

    lo = N_GROUPS + g_idx * EXPERTS_PER_GROUP
    le = jnp.where((lane >= lo) & (lane < lo + EXPERTS_PER_GROUP), logits, neg)
    v1, i1 = first_argmax(le)
    v2, i2 = first_argmax(jnp.where(lane == i1, neg, le))
    t = jnp.exp(v2 - v1)
    gate1 = g_w / (1.0 + t)
    gate2 = g_w * t / (1.0 + t)
    e1 = i1 - N_GROUPS
    e2 = i2 - N_GROUPS

    oh1 = (lane == e1).astype(F32)
    oh2 = (lane == e2).astype(F32)
    oh = oh1 + oh2
    befores = []
    for r0 in range(0, m, r):
        oh_r = oh[r0:r0 + r, :]
        befores.append(_dot(strict_lower, oh_r.astype(BF16)) + run)
        run = run + jnp.sum(oh_r, axis=0, keepdims=True)
    before = befores[0] if len(befores) == 1 else jnp.concatenate(befores, axis=0)
    rank1 = jnp.sum(before * oh1, axis=-1, keepdims=True)
    rank2 = jnp.sum(before * oh2, axis=-1, keepdims=True)
    new_run = run

    rinfo = jnp.where(lane == 0, e1,
            jnp.where(lane == 1, e2,
            jnp.where(lane == 2, gate1,
            jnp.where(lane == 3, gate2,
            jnp.where(lane == 4, rank1,
            jnp.where(lane == 5, rank2, 0.0))))))
    return jnp.transpose(rinfo)[0:SUBLANES, :], new_run


def _conv_segments(a, w_ref, seg_ref, tail_ref, yseg_ref, conv_ref):
    sub = lax.broadcasted_iota(jnp.int32, (SUBLANES, LANES), 0)
    for lt in range(D_CONV // LANES):
        ls = slice(lt * LANES, (lt + 1) * LANES)
        for t0 in range(0, TM, SUBLANES):
            s, m = divmod(t0, SEG)
            seg_ref[lt, pl.ds((SEG_HALO + m) * SUBLANES + s, SUBLANES, stride=SUBLANES), :] = a[t0:t0 + SUBLANES, ls]
        for j in range(SEG_HALO):
            cur = seg_ref[lt, (SEG + j) * SUBLANES:(SEG + j + 1) * SUBLANES, :]
            prev = tail_ref[lt, j * SUBLANES:(j + 1) * SUBLANES, :]
            seg_ref[lt, j * SUBLANES:(j + 1) * SUBLANES, :] = jnp.where(
                sub == 0, pltpu.roll(prev, 1, axis=0), pltpu.roll(cur, 1, axis=0))
            tail_ref[lt, j * SUBLANES:(j + 1) * SUBLANES, :] = cur
        for m0 in range(0, SEG, CONV_BLOCK):
            acc = [jnp.zeros((SUBLANES, LANES), F32) for _ in range(CONV_BLOCK)]
            for idx in range(m0 - HIST, m0 + CONV_BLOCK):
                b = seg_ref[lt, (SEG_HALO + idx) * SUBLANES:(SEG_HALO + idx + 1) * SUBLANES, :]
                for m in range(max(m0, idx), min(m0 + CONV_BLOCK, idx + CONV_WIDTH)):
                    k = idx - m + HIST
                    acc[m - m0] = acc[m - m0] + b * w_ref[k:k + 1, ls]
            for m in range(m0, m0 + CONV_BLOCK):
                yseg_ref[lt, m * SUBLANES:(m + 1) * SUBLANES, :] = acc[m - m0]
        for t0 in range(0, TM, SUBLANES):
            s, m = divmod(t0, SEG)
            conv_ref[t0:t0 + SUBLANES, ls] = yseg_ref[lt, pl.ds(m * SUBLANES + s, SUBLANES, stride=SUBLANES), :]


def _cast_rows(src_ref, dst_ref):
    rows = src_ref.shape[0]

    def body(c, carry):
        r0 = pl.multiple_of(c * CAST_ROWS, CAST_ROWS)
        dst_ref[pl.ds(r0, CAST_ROWS), :] = src_ref[pl.ds(r0, CAST_ROWS), :].astype(BF16)
        return carry

    lax.fori_loop(0, rows // CAST_ROWS, body, 0)


def _trunk_prompt_kernel(x_ref, gmix_ref, win32_ref, convw_ref, convb_ref, lncg_ref, lncb_ref, lnvg_ref, lnvb_ref,
                         wsg_ref, bsg_ref, wout32_ref, gx_ref, wxq32_ref, kmem_ref, v_ref, wxo32_ref, gffn_ref, wr_ref,
                         br_ref,
                         x2_ref, h3_ref, logit_ref, hist_ref,
                         seg_ref, tail_ref, yseg_ref, conv_ref, win_ref, wout_ref, wxq_ref, wxo_ref):
    i = pl.program_id(0)

    @pl.when(i == 0)
    def _():
        tail_ref[...] = jnp.zeros(tail_ref.shape, F32)
        _cast_rows(win32_ref, win_ref)
        _cast_rows(wout32_ref, wout_ref)
        _cast_rows(wxq32_ref, wxq_ref)
        _cast_rows(wxo32_ref, wxo_ref)

    x = x_ref[...]
    h = _rms(x, gmix_ref[...]).astype(BF16)

    a_in = _dot(h, win_ref[:, 0:D_CONV])
    a_gate = _dot(h, win_ref[:, D_CONV:2 * D_CONV])
    a = a_in * _sigmoid(a_gate)
    hist_ref[...] = a[TM - HALO:, :]
    _conv_segments(a, convw_ref, seg_ref, tail_ref, yseg_ref, conv_ref)

    y = _ln(conv_ref[...] + convb_ref[...], lncg_ref[...], lncb_ref[...])
    a_out = (y * _sigmoid(y)).astype(BF16)

    u = _dot(h, win_ref[:, 2 * D_CONV:2 * D_CONV + D_SG])
    v = _ln(_dot(h, win_ref[:, 2 * D_CONV + D_SG:]), lnvg_ref[...], lnvb_ref[...]).astype(BF16)
    ri = lax.broadcasted_iota(jnp.int32, (SG_CHUNK, SG_CHUNK), 0)
    ci = lax.broadcasted_iota(jnp.int32, (SG_CHUNK, SG_CHUNK), 1)
    w_tril = [jnp.where(ci <= ri, wsg_ref[hh], 0.0).astype(BF16) for hh in range(SG_HEADS)]
    gate_rows = []
    for c in range(TM // SG_CHUNK):
        rs = slice(c * SG_CHUNK, (c + 1) * SG_CHUNK)
        heads = [_dot(w_tril[hh], v[rs, hh * SG_HEAD_DIM:(hh + 1) * SG_HEAD_DIM]) for hh in range(SG_HEADS)]
        gate_rows.append(jnp.concatenate(heads, axis=1) + bsg_ref[...])
    b_out = (u * jnp.concatenate(gate_rows, axis=0)).astype(BF16)

    x1 = x + _dot(a_out, wout_ref[0:D_CONV, :]) + _dot(b_out, wout_ref[D_CONV:, :])

    hx = _rms(x1, gx_ref[...]).astype(BF16)
    q = _dot(hx, wxq_ref[...]).astype(BF16)
    x2 = x1 + _dot(_attn_heads(q, kmem_ref[...], v_ref[...], False), wxo_ref[...])
    x2_ref[...] = x2

    h3 = _rms(x2, gffn_ref[...]).astype(BF16)
    h3_ref[...] = _pack_bf16_pairs(h3.astype(F32))
    logit_ref[...] = _dot(h3, wr_ref[...]) + br_ref[...]


def _router_kernel(logit_ref, upper_ref, ones_ref, rt_ref, cnt_ref, run_ref):
    @pl.when(pl.program_id(0) == 0)
    def _():
        run_ref[...] = jnp.zeros(run_ref.shape, F32)

    n = ROUTE_ROWS
    lt = jnp.transpose(logit_ref[...])
    neg = jnp.float32(-jnp.inf)
    big = jnp.float32(LOGIT_LANES)

    def first_argmax(vals, rows):
        mx = jnp.max(vals, axis=0, keepdims=True)
        idx = jnp.min(jnp.where(vals == mx, rows, big), axis=0, keepdims=True)
        return mx, idx

    row8 = lax.broadcasted_iota(jnp.int32, (SUBLANES, n), 0).astype(F32)
    lg = jnp.where(row8 < N_GROUPS, lt[0:SUBLANES, :], neg)
    g_max, g_idx = first_argmax(lg, row8)
    g_w = 1.0 / jnp.sum(jnp.exp(lg - g_max), axis=0, keepdims=True)

    n_rows = N_GROUPS + N_EXPERTS + (-(N_GROUPS + N_EXPERTS)) % SUBLANES
    rows = lax.broadcasted_iota(jnp.int32, (n_rows, n), 0).astype(F32)
    lo = N_GROUPS + g_idx * EXPERTS_PER_GROUP
    le = jnp.where((rows >= lo) & (rows < lo + EXPERTS_PER_GROUP), lt[0:n_rows, :], neg)
    v1, i1 = first_argmax(le, rows)
    v2, i2 = first_argmax(jnp.where(rows == i1, neg, le), rows)
    t = jnp.exp(v2 - v1)
    gate1 = g_w / (1.0 + t)
    gate2 = g_w * t / (1.0 + t)
    e1 = i1 - N_GROUPS
    e2 = i2 - N_GROUPS

    erow = lax.broadcasted_iota(jnp.int32, (LOGIT_LANES, n), 0).astype(F32)
    oh1 = (erow == e1).astype(F32)
    oh2 = (erow == e2).astype(F32)
    oh = (oh1 + oh2).astype(BF16)
    r = upper_ref.shape[0]
    run = run_ref[...]
    rank1, rank2 = [], []
    for c0 in range(0, n, r):
        cs = slice(c0, c0 + r)
        before = _dot(oh[:, cs], upper_ref[...]) + run
        rank1.append(jnp.sum(before * oh1[:, cs], axis=0, keepdims=True))
        rank2.append(jnp.sum(before * oh2[:, cs], axis=0, keepdims=True))
        run = run + _dot(oh[:, cs], ones_ref[...])
    run_ref[...] = run
    cnt_ref[...] = run[:, 0:LANES]

    sub = lax.broadcasted_iota(jnp.int32, (SUBLANES, n), 0)
    vals = (e1, e2, gate1, gate2, jnp.concatenate(rank1, axis=1), jnp.concatenate(rank2, axis=1))
    rt = jnp.zeros((SUBLANES, n), F32)
    for k, v in enumerate(vals):
        rt = jnp.where(sub == k, v, rt)
    rt_ref[...] = rt


def _router(logits, rank_block):
    n = logits.shape[0]
    assert n % ROUTE_ROWS == 0 and ROUTE_ROWS % rank_block == 0
    upper = jnp.triu(jnp.ones((rank_block, rank_block), BF16), 1)
    ones = jnp.ones((rank_block, rank_block), BF16)
    return pl.pallas_call(
        _router_kernel,
        grid=(n // ROUTE_ROWS,),
        in_specs=[pl.BlockSpec((ROUTE_ROWS, LOGIT_LANES), lambda i: (i, 0)),
                  pl.BlockSpec(upper.shape, lambda i: (0, 0)),
                  pl.BlockSpec(ones.shape, lambda i: (0, 0))],
        out_specs=(pl.BlockSpec((SUBLANES, ROUTE_ROWS), lambda i: (0, i)),
                   pl.BlockSpec((LOGIT_LANES, LANES), lambda i: (0, 0))),
        out_shape=(jax.ShapeDtypeStruct((SUBLANES, n), F32), jax.ShapeDtypeStruct((LOGIT_LANES, LANES), F32)),
        scratch_shapes=[pltpu.VMEM((LOGIT_LANES, rank_block), F32)],
        compiler_params=pltpu.CompilerParams(dimension_semantics=("arbitrary",), vmem_limit_bytes=VMEM_LIMIT),
        name="router",
    )(logits, upper, ones)


def _const_spec(shape):
    nd = len(shape)
    return pl.BlockSpec(shape, lambda i: (0,) * nd, pipeline_mode=pl.Buffered(1))


def _trunk_prompt(x, p):
    n = x.shape[0]
    assert n % TM == 0
    row = lambda w: pl.BlockSpec((TM, w), lambda i: (i, 0))
    consts = [p["g_mix"], p["w_in"], p["conv_w"], p["conv_b"], p["ln_conv_g"], p["ln_conv_b"], p["ln_v_g"],
              p["ln_v_b"], p["w_sg"], p["b_sg_rows"], p["w_out"], p["g_xattn"], p["w_xq"], p["k"], p["v"],
              p["w_xo"], p["g_ffn"], p["w_router"], p["b_router"]]
    return pl.pallas_call(
        _trunk_prompt_kernel,
        grid=(n // TM,),
        in_specs=[row(D_MODEL)] + [_const_spec(c.shape) for c in consts],
        out_specs=(row(D_MODEL), row(D_MODEL // 2), row(LOGIT_LANES),
                   pl.BlockSpec((HALO, D_CONV), lambda i: (0, 0))),
        out_shape=(jax.ShapeDtypeStruct((n, D_MODEL), F32),
                   jax.ShapeDtypeStruct((n, D_MODEL // 2), jnp.uint32),
                   jax.ShapeDtypeStruct((n, LOGIT_LANES), F32),
                   jax.ShapeDtypeStruct((HALO, D_CONV), F32)),
        scratch_shapes=[pltpu.VMEM((D_CONV // LANES, (SEG_HALO + SEG) * SUBLANES, LANES), F32),
                        pltpu.VMEM((D_CONV // LANES, SEG_HALO * SUBLANES, LANES), F32),
                        pltpu.VMEM((D_CONV // LANES, TM, LANES), F32),
                        pltpu.VMEM((TM, D_CONV), F32),
                        pltpu.VMEM(p["w_in"].shape, BF16), pltpu.VMEM(p["w_out"].shape, BF16),
                        pltpu.VMEM(p["w_xq"].shape, BF16), pltpu.VMEM(p["w_xo"].shape, BF16)],
        compiler_params=pltpu.CompilerParams(dimension_semantics=("arbitrary",), vmem_limit_bytes=VMEM_LIMIT),
        name="trunk_prompt",
    )(x, *consts)


def _trunk_sample_kernel(n_batch, t_len,
                         x_ref, hist_in_ref, run_in_ref, gmix_ref, win_ref, convw_ref, convb_ref, lncg_ref, lncb_ref,
                         lnvg_ref, lnvb_ref, wsgbd_ref, bsg_ref, wout_ref, gx_ref, wxq_ref, kmem_ref, v_ref, wxo_ref,
                         gffn_ref, wr_ref, br_ref, lower_ref,
                         x2_ref, h3_ref, rt_ref, hist_ref, sgv_ref, cnt_ref,
                         ext_ref, conv_ref, att_ref):
    x = x_ref[...]
    h = _rms(x, gmix_ref[...]).astype(BF16)
    z = _dot(h, win_ref[...].astype(BF16))
    a = z[:, 0:D_CONV] * _sigmoid(z[:, D_CONV:2 * D_CONV])
    ext_len = HIST + t_len
    for b in range(n_batch):
        ext_ref[b, 0:HIST, :] = hist_in_ref[b]
        ext_ref[b, HIST:ext_len, :] = a[b * t_len:(b + 1) * t_len, :]
    for b in range(n_batch):
        acc = jnp.zeros((t_len, D_CONV), F32)
        for k in range(CONV_WIDTH):
            acc = acc + ext_ref[b, k:k + t_len, :] * convw_ref[k:k + 1, :]
        conv_ref[b * t_len:(b + 1) * t_len, :] = acc
        hist_ref[b] = ext_ref[b, ext_len - HIST:ext_len, :]

    y = _ln(conv_ref[...] + convb_ref[...], lncg_ref[...], lncb_ref[...])
    a_out = (y * _sigmoid(y)).astype(BF16)

    u = z[:, 2 * D_CONV:2 * D_CONV + D_SG]
    v = _ln(z[:, 2 * D_CONV + D_SG:], lnvg_ref[...], lnvb_ref[...])
    sgv_ref[...] = v
    vb = v.astype(BF16)
    heads = [_dot(wsgbd_ref[hh], vb[:, hh * SG_HEAD_DIM:(hh + 1) * SG_HEAD_DIM]) for hh in range(SG_HEADS)]
    b_out = (u * (jnp.concatenate(heads, axis=1) + bsg_ref[...])).astype(BF16)

    x1 = (x + _dot(a_out, wout_ref[0:D_CONV, :].astype(BF16))
          + _dot(b_out, wout_ref[D_CONV:, :].astype(BF16)))

    hx = _rms(x1, gx_ref[...]).astype(BF16)
    q = _dot(hx, wxq_ref[...].astype(BF16)).astype(BF16)
    for b in range(n_batch):
        rs = slice(b * t_len, (b + 1) * t_len)
        att_ref[rs, :] = _attn_heads(q[rs, :], kmem_ref[b], v_ref[b], True)
    x2 = x1 + _dot(att_ref[...], wxo_ref[...].astype(BF16))
    x2_ref[...] = x2

    h3 = _rms(x2, gffn_ref[...]).astype(BF16)
    m = n_batch * t_len
    h3_ref[0:m, :] = _pack_bf16_pairs(h3.astype(F32))
    if h3_ref.shape[0] > m:
        h3_ref[m:, :] = jnp.zeros((h3_ref.shape[0] - m, D_MODEL // 2), jnp.uint32)
    rt, new_run = _route(_dot(h3, wr_ref[...]) + br_ref[...], run_in_ref[...], lower_ref[...])
    rt_ref[...] = rt
    cnt_ref[...] = new_run


def _trunk_sample(x, hist, run, p, n_batch, t_len):
    m = n_batch * t_len
    args = [x, hist, run, p["g_mix"], p["w_in"], p["conv_w"], p["conv_b"], p["ln_conv_g"], p["ln_conv_b"],
            p["ln_v_g"], p["ln_v_b"], p["w_sg_bd"], p["b_sg_rows_s"], p["w_out"], p["g_xattn"], p["w_xq"],
            p["k_s"], p["v_s"], p["w_xo"], p["g_ffn"], p["w_router"], p["b_router"], p["lower"]]
    return pl.pallas_call(
        functools.partial(_trunk_sample_kernel, n_batch, t_len),
        out_shape=(jax.ShapeDtypeStruct((m, D_MODEL), F32),
                   jax.ShapeDtypeStruct((-(-m // (SC_WORKERS * SUBLANES)) * SC_WORKERS * SUBLANES, D_MODEL // 2),
                                        jnp.uint32),
                   jax.ShapeDtypeStruct((SUBLANES, m), F32),
                   jax.ShapeDtypeStruct((n_batch, HIST, D_CONV), F32),
                   jax.ShapeDtypeStruct((m, D_SG), F32),
                   jax.ShapeDtypeStruct((1, LOGIT_LANES), F32)),
        scratch_shapes=[pltpu.VMEM((n_batch, HIST + t_len, D_CONV), F32),
                        pltpu.VMEM((m, D_CONV), F32),
                        pltpu.VMEM((m, D_MODEL), BF16)],
        compiler_params=pltpu.CompilerParams(vmem_limit_bytes=VMEM_LIMIT),
        name="trunk_sample",
    )(*args)


def _sc_worker_id():
    return lax.axis_index("s") * SC_CORES + lax.axis_index("c")


def _sc_chunk(per_w, max_chunk):
    assert per_w % SUBLANES == 0 and max_chunk <= LANES
    return max(c for c in range(SUBLANES, max_chunk + 1, SUBLANES) if per_w % c == 0)


def _sc_scatter_rows2(tables, slots_a, slots_b, tag_bases, n_rows_out, max_chunk):
    d, dtype = tables[0].shape[1], tables[0].dtype
    plans = []
    for t in tables:
        per_w = t.shape[0] // SC_WORKERS
        assert per_w * SC_WORKERS == t.shape[0]
        chunk = _sc_chunk(per_w, max_chunk)
        plans.append((per_w, chunk, per_w // chunk))
    cmax = max(c for _, c, _ in plans)
    n_t = len(tables)
    mesh = plsc.VectorSubcoreMesh(core_axis_name="c", subcore_axis_name="s")

    nb = SCATTER_BUFS
    lag = nb // 3
    scratch = []
    for _, chunk, _ in plans:
        for _ in range(nb):
            scratch += [pltpu.VMEM((chunk,), jnp.int32), pltpu.VMEM((chunk,), jnp.int32)]
    scratch += [pltpu.VMEM((cmax, d), dtype)] * nb
    scratch += [pltpu.VMEM((cmax, TAG_WORDS), jnp.int32)] * (2 * nb)
    scratch += [pltpu.SemaphoreType.DMA] * (2 * nb)

    @functools.partial(pl.kernel, mesh=mesh,
                       out_type=(jax.ShapeDtypeStruct((n_rows_out, d), dtype),
                                 jax.ShapeDtypeStruct((n_rows_out, TAG_WORDS), jnp.int32)),
                       scratch_types=scratch)
    def scatter(*refs):
        tab_hbm = refs[0:n_t]
        sa_hbm = refs[n_t:2 * n_t]
        sb_hbm = refs[2 * n_t:3 * n_t]
        out_hbm, tag_hbm = refs[3 * n_t], refs[3 * n_t + 1]
        sc = refs[3 * n_t + 2:]
        idx_refs = sc[:2 * nb * n_t]
        rows = sc[2 * nb * n_t:2 * nb * n_t + nb]
        tagbufs = sc[2 * nb * n_t + nb:2 * nb * n_t + 3 * nb]
        lsem = sc[2 * nb * n_t + 3 * nb:2 * nb * n_t + 4 * nb]
        ssem = sc[2 * nb * n_t + 4 * nb:]
        wid = _sc_worker_id()

        work = []
        for t, (per_w, chunk, n_chunks) in enumerate(plans):
            for j in range(n_chunks):
                work.append((t, wid * per_w + j * chunk, chunk))

        def parts(k):
            t, off, chunk = work[k]
            b = k % nb
            ia, ib = idx_refs[2 * nb * t + 2 * b], idx_refs[2 * nb * t + 2 * b + 1]
            full = chunk == cmax
            rv = rows[b] if full else rows[b].at[pl.ds(0, chunk)]
            ta = tagbufs[2 * b] if full else tagbufs[2 * b].at[pl.ds(0, chunk)]
            tb = tagbufs[2 * b + 1] if full else tagbufs[2 * b + 1].at[pl.ds(0, chunk)]
            return t, off, chunk, b, ia, ib, rv, ta, tb

        def start_load(k):
            t, off, chunk, b, ia, ib, rv, ta, tb = parts(k)
            return (pltpu.async_copy(tab_hbm[t].at[pl.ds(off, chunk)], rv, lsem[b]),
                    pltpu.async_copy(sa_hbm[t].at[pl.ds(off, chunk)], ia, lsem[b]),
                    pltpu.async_copy(sb_hbm[t].at[pl.ds(off, chunk)], ib, lsem[b]))

        def start_scatter(k):
            t, off, chunk, b, ia, ib, rv, ta, tb = parts(k)
            base_a, base_b = tag_bases[t]
            for r in range(chunk):
                row_id = (off + r).astype(jnp.int32)
                tagbufs[2 * b][r, pl.ds(0, SC_LANES)] = jnp.zeros((SC_LANES,), jnp.int32) + (base_a + row_id)
                tagbufs[2 * b + 1][r, pl.ds(0, SC_LANES)] = jnp.zeros((SC_LANES,), jnp.int32) + (base_b + row_id)
            return (pltpu.async_copy(rv, out_hbm.at[ia], ssem[b]), pltpu.async_copy(rv, out_hbm.at[ib], ssem[b]),
                    pltpu.async_copy(ta, tag_hbm.at[ia], ssem[b]), pltpu.async_copy(tb, tag_hbm.at[ib], ssem[b]))

        loads, scatters = {}, {}
        for k in range(len(work) + lag):
            if k < len(work):
                if k >= nb:
                    for c in scatters.pop(k - nb):
                        c.wait()
                loads[k] = start_load(k)
            w = k - lag
            if w >= 0:
                for c in loads.pop(w):
                    c.wait()
                scatters[w] = start_scatter(w)
        for w in sorted(scatters):
            for c in scatters[w]:
                c.wait()

    return scatter(*tables, *slots_a, *slots_b)


def _sc_scatter_back(ys, dest, n_rows_out):
    n_rows, d = ys.shape
    per_w = n_rows // SC_WORKERS
    assert per_w * SC_WORKERS == n_rows
    chunk = _sc_chunk(per_w, BACK_CHUNK)
    n_chunks = per_w // chunk
    nb = BACK_BUFS
    lag = nb // 2
    mesh = plsc.VectorSubcoreMesh(core_axis_name="c", subcore_axis_name="s")

    @functools.partial(
        pl.kernel, mesh=mesh,
        out_type=jax.ShapeDtypeStruct((n_rows_out, d), ys.dtype),
        scratch_types=([pltpu.VMEM((chunk,), jnp.int32)] * nb + [pltpu.VMEM((chunk, d), ys.dtype)] * nb
                       + [pltpu.SemaphoreType.DMA] * (2 * nb)),
    )
    def scatter_back(ys_hbm, dest_hbm, out_hbm, *rest):
        idx = rest[:nb]
        rows = rest[nb:2 * nb]
        lsem = rest[2 * nb:3 * nb]
        ssem = rest[3 * nb:]
        base = _sc_worker_id() * per_w

        loads, scatters = {}, {}
        for k in range(n_chunks + lag):
            if k < n_chunks:
                b = k % nb
                if k >= nb:
                    scatters.pop(k - nb).wait()
                off = base + k * chunk
                loads[k] = (pltpu.async_copy(ys_hbm.at[pl.ds(off, chunk)], rows[b], lsem[b]),
                            pltpu.async_copy(dest_hbm.at[pl.ds(off, chunk)], idx[b], lsem[b]))
            w = k - lag
            if w >= 0:
                b = w % nb
                for c in loads.pop(w):
                    c.wait()
                scatters[w] = pltpu.async_copy(rows[b], out_hbm.at[idx[b]], ssem[b])
        for w in sorted(scatters):
            scatters[w].wait()

    return scatter_back(ys, dest)


def _experts_kernel(dump_base, first_ref, nblk_ref, cnt_ref, tot_ref, xs_hbm, tag_hbm, wg_ref, wu_ref, wd_ref,
                    ys_hbm, dest_hbm, xbuf, tbuf, ybuf, dbuf, wg_bf, wu_bf, wd_bf, in_sem, tin_sem, out_sem, dout_sem):
    e = pl.program_id(0)
    nb = nblk_ref[e]
    first = first_ref[e]
    cnt = cnt_ref[e]
    total = tot_ref[0]
    half = D_MODEL // 2

    def in_copies(gb):
        slot = lax.rem(gb, X_BUFS)
        return (pltpu.make_async_copy(xs_hbm.at[pl.ds(gb * BM, BM)], xbuf.at[slot], in_sem.at[slot]),
                pltpu.make_async_copy(tag_hbm.at[pl.ds(gb * BM, BM)], tbuf.at[slot], tin_sem.at[slot]))

    def out_copies(gb):
        slot = lax.rem(gb, Y_BUFS)
        return (pltpu.make_async_copy(ybuf.at[slot], ys_hbm.at[pl.ds(gb * BM, BM)], out_sem.at[slot]),
                pltpu.make_async_copy(dbuf.at[slot], dest_hbm.at[pl.ds(gb * SUBLANES, SUBLANES)], dout_sem.at[slot]))

    def start_in(gb):
        for c in in_copies(gb):
            c.start(priority=ROW_DMA_PRIORITY)

    def start_out(gb):
        for c in out_copies(gb):
            c.start(priority=ROW_DMA_PRIORITY)

    def wait_out(gb):
        for c in out_copies(gb):
            c.wait()

    @pl.when(nb > 0)
    def _():
        @pl.when(first == 0)
        def _():
            for k in range(X_LOOKAHEAD):
                @pl.when(k < total)
                def _():
                    start_in(k)

        wg_bf[...] = wg_ref[0].astype(BF16)
        wu_bf[...] = wu_ref[0].astype(BF16)
        wd_bf[...] = wd_ref[0].astype(BF16)

        def acquire(gb):
            @pl.when(gb + X_LOOKAHEAD < total)
            def _():
                start_in(gb + X_LOOKAHEAD)

            for c in in_copies(gb):
                c.wait()

            @pl.when(gb >= Y_BUFS)
            def _():
                wait_out(gb - Y_BUFS)

        def ffn(gb, j):
            n_live = cnt - j * BM
            tags_t = jnp.transpose(tbuf[lax.rem(gb, X_BUFS)].astype(F32))
            lane = lax.broadcasted_iota(jnp.int32, (SUBLANES, BM), 1)
            own = (dump_base + gb * BM + lane).astype(F32)
            dest = jnp.where(lane < n_live, jnp.broadcast_to(tags_t[0:1, :], (SUBLANES, BM)), own)
            dbuf[lax.rem(gb, Y_BUFS)] = dest.astype(jnp.int32)
            live = lax.broadcasted_iota(jnp.int32, (BM, half), 0) < n_live
            lo, hi = _unpack_bf16_pairs(jnp.where(live, xbuf[lax.rem(gb, X_BUFS)], jnp.uint32(0)))
            g = _dot(lo, wg_bf[0:half, :]) + _dot(hi, wg_bf[half:, :])
            u = _dot(lo, wu_bf[0:half, :]) + _dot(hi, wu_bf[half:, :])
            hm = (g * _sigmoid(g) * u).astype(BF16)
            y = _dot(hm, wd_bf[...])
            ybuf[lax.rem(gb, Y_BUFS)] = _pack_bf16_pairs(y.astype(BF16).astype(F32))

        def block_pair(jp, carry):
            j0 = 2 * jp
            g0 = first + j0
            acquire(g0)
            acquire(g0 + 1)
            ffn(g0, j0)
            ffn(g0 + 1, j0 + 1)
            start_out(g0)
            start_out(g0 + 1)
            return carry

        lax.fori_loop(0, nb // 2, block_pair, 0)

        @pl.when(lax.rem(nb, 2) == 1)
        def _():
            gl = first + nb - 1
            acquire(gl)
            ffn(gl, nb - 1)
            start_out(gl)

        @pl.when(first + nb == total)
        def _():
            for k in range(Y_BUFS):
                @pl.when(total - 1 - k >= 0)
                def _():
                    wait_out(total - 1 - k)


def _experts(xs, tags, n_rows_out, dump_base, first_block, n_blocks_e, counts, w_eg, w_eu, w_ed):
    w_map = lambda e, fb, nb, ct, tot: (e, 0, 0)
    half = D_MODEL // 2
    total = jnp.sum(n_blocks_e).astype(jnp.int32).reshape(1)
    n_blocks = n_rows_out // BM
    return pl.pallas_call(
        functools.partial(_experts_kernel, dump_base),
        grid_spec=pltpu.PrefetchScalarGridSpec(
            num_scalar_prefetch=4,
            grid=(N_EXPERTS,),
            in_specs=[pl.BlockSpec(memory_space=pl.ANY),
                      pl.BlockSpec(memory_space=pl.ANY),
                      pl.BlockSpec((1, D_MODEL, D_EXPERT), w_map),
                      pl.BlockSpec((1, D_MODEL, D_EXPERT), w_map),
                      pl.BlockSpec((1, D_EXPERT, D_MODEL), w_map)],
            out_specs=(pl.BlockSpec(memory_space=pl.ANY), pl.BlockSpec(memory_space=pl.ANY)),
            scratch_shapes=[pltpu.VMEM((X_BUFS, BM, half), jnp.uint32), pltpu.VMEM((X_BUFS, BM, TAG_WORDS), jnp.int32),
                            pltpu.VMEM((Y_BUFS, BM, half), jnp.uint32), pltpu.VMEM((Y_BUFS, SUBLANES, BM), jnp.int32),
                            pltpu.VMEM((D_MODEL, D_EXPERT), BF16), pltpu.VMEM((D_MODEL, D_EXPERT), BF16),
                            pltpu.VMEM((D_EXPERT, D_MODEL), BF16),
                            pltpu.SemaphoreType.DMA((X_BUFS,)), pltpu.SemaphoreType.DMA((X_BUFS,)),
                            pltpu.SemaphoreType.DMA((Y_BUFS,)), pltpu.SemaphoreType.DMA((Y_BUFS,))]),
        out_shape=(jax.ShapeDtypeStruct((n_rows_out, half), jnp.uint32),
                   jax.ShapeDtypeStruct((n_blocks * SUBLANES, BM), jnp.int32)),
        compiler_params=pltpu.CompilerParams(dimension_semantics=("arbitrary",), vmem_limit_bytes=VMEM_LIMIT),
        name="experts",
    )(first_block, n_blocks_e, counts, total, xs, tags, w_eg, w_eu, w_ed)


def _combine_kernel(x2_ref, y1_ref, y2_ref, rt_ref, g_ref, o_ref):
    rt = rt_ref[...]
    r = jnp.transpose(jnp.concatenate([rt, jnp.zeros((LANES - rt.shape[0], rt.shape[1]), F32)], axis=0))
    g1, g2 = r[:, 2:3], r[:, 3:4]
    half = D_MODEL // 2
    y1_lo, y1_hi = _unpack_bf16_pairs_f32(y1_ref[...])
    y2_lo, y2_hi = _unpack_bf16_pairs_f32(y2_ref[...])
    x_lo = x2_ref[:, 0:half] + g1 * y1_lo + g2 * y2_lo
    x_hi = x2_ref[:, half:] + g1 * y1_hi + g2 * y2_hi
    ms = (jnp.sum(x_lo * x_lo, axis=-1, keepdims=True) + jnp.sum(x_hi * x_hi, axis=-1, keepdims=True)) / D_MODEL
    inv = lax.rsqrt(ms + EPS)
    o_ref[:, 0:half] = x_lo * inv * g_ref[:, 0:half]
    o_ref[:, half:] = x_hi * inv * g_ref[:, half:]


def _combine(x2, yg, rt, g_final, tm, blk1, blk2):
    n = x2.shape[0]
    return pl.pallas_call(
        _combine_kernel,
        grid=(n // tm,),
        in_specs=[pl.BlockSpec((tm, D_MODEL), lambda i: (i, 0)),
                  pl.BlockSpec((tm, D_MODEL // 2), lambda i: (blk1 + i, 0)),
                  pl.BlockSpec((tm, D_MODEL // 2), lambda i: (blk2 + i, 0)),
                  pl.BlockSpec((SUBLANES, tm), lambda i: (0, i)),
                  pl.BlockSpec((1, D_MODEL), lambda i: (0, 0))],
        out_specs=pl.BlockSpec((tm, D_MODEL), lambda i: (i, 0)),
        out_shape=jax.ShapeDtypeStruct((n, D_MODEL), F32),
        compiler_params=pltpu.CompilerParams(dimension_semantics=("arbitrary",), vmem_limit_bytes=VMEM_LIMIT),
        name="combine",
    )(x2, yg, yg, rt, g_final)


def _scatter_back(ys, dest, n_rows_out):
    return _sc_scatter_back(ys, dest, n_rows_out)


def _scatter_rows2(tables, slots_a, slots_b, tag_bases, n_rows_out):
    return _sc_scatter_rows2(tables, slots_a, slots_b, tag_bases, n_rows_out, SCATTER_CHUNK)


def kernel(x_prompt, x_sample, mem_prompt, state_conv, cache_mem_k, cache_mem_v, g_mix, w_in, conv_w, conv_b, ln_conv_g, ln_conv_b, ln_v_g, ln_v_b, w_sg, b_sg, w_out, g_mem, w_mk, w_mv, g_xattn, w_xq, w_xo, g_ffn, w_router_group, b_router_group, w_router_expert, b_router_expert, w_expert_gate, w_expert_up, w_expert_down, g_final):
    assert x_prompt.shape[0] == 1 and g_mix.shape[0] == 1
    n_p = x_prompt.shape[1]
    n_batch, t_len = x_sample.shape[0], x_sample.shape[1]
    n_s = n_batch * t_len
    row = lambda a: a.reshape(1, -1)

    w_router = jnp.concatenate(
        [w_router_group[0], jnp.transpose(w_router_expert[0], (1, 0, 2)).reshape(D_MODEL, N_EXPERTS)], axis=1)
    w_router = jnp.pad(w_router, ((0, 0), (0, LOGIT_LANES - w_router.shape[1]))).astype(BF16)
    b_router = jnp.pad(jnp.concatenate([b_router_group[0], b_router_expert[0].reshape(-1)]),
                       (0, LOGIT_LANES - N_GROUPS - N_EXPERTS)).reshape(1, LOGIT_LANES)
    tril_t = jnp.tril(jnp.ones((t_len, t_len), bool))
    w_sg_t = jnp.where(tril_t, w_sg[0][:, :t_len, :t_len], 0.0)
    eye_b = jnp.eye(n_batch, dtype=F32)
    w_sg_bd = jnp.einsum("ab,hij->haibj", eye_b, w_sg_t).reshape(SG_HEADS, n_s, n_s).astype(BF16)
    p = {
        "g_mix": row(g_mix[0]), "w_in": w_in[0],
        "conv_w": jnp.pad(conv_w[0], ((0, 1), (0, 0))), "conv_b": row(conv_b[0]),
        "ln_conv_g": row(ln_conv_g[0]), "ln_conv_b": row(ln_conv_b[0]),
        "ln_v_g": row(ln_v_g[0]), "ln_v_b": row(ln_v_b[0]),
        "w_sg": w_sg[0],
        "b_sg_rows": jnp.repeat(b_sg[0].T, SG_HEAD_DIM, axis=1),
        "w_sg_bd": w_sg_bd,
        "b_sg_rows_s": jnp.tile(jnp.repeat(b_sg[0][:, :t_len].T, SG_HEAD_DIM, axis=1), (n_batch, 1)),
        "w_out": w_out[0], "g_xattn": row(g_xattn[0]),
        "w_xq": w_xq[0], "w_xo": w_xo[0], "g_ffn": row(g_ffn[0]),
        "w_router": w_router, "b_router": b_router,
        "lower": jnp.tril(jnp.ones((n_s, n_s), BF16), -1),
    }

    k_p, v_p, p["k"], p["v"] = _memkv(mem_prompt[0], row(g_mem[0]), w_mk[0], w_mv[0])
    p["k_s"] = cache_mem_k[0].reshape(n_batch, N_MEM, D_MODEL)
    p["v_s"] = cache_mem_v[0].reshape(n_batch, N_MEM, D_MODEL)

    assert n_p % n_s == 0
    x2_p, h3_p, logits_p, hist_p = _trunk_prompt(x_prompt[0], p)
    rt_p, cnt_t = _router(logits_p, TM)
    cnt_p = cnt_t[:, 0].reshape(1, LOGIT_LANES)
    x2_s, h3_s, rt_s, hist_s, sgv_s, cnt = _trunk_sample(
        x_sample.reshape(n_s, D_MODEL), state_conv[0], cnt_p, p, n_batch, t_len)

    experts = jnp.arange(N_EXPERTS, dtype=jnp.int32)
    w_e = (w_expert_gate[0], w_expert_up[0], w_expert_down[0])

    def moe_pass(cnt, h3_tables, rts, n_real):
        n_tot = sum(n_real)
        n_slots = -(-(n_tot * 2) // BM) * BM + N_EXPERTS * BM
        counts = cnt[0, :N_EXPERTS].astype(jnp.int32)
        padded = (counts + BM - 1) // BM * BM
        pad_start = jnp.cumsum(padded) - padded

        def one(e_row, rank_row):
            e = e_row.astype(jnp.int32)
            start = jnp.sum(jnp.where(e[None, :] == experts[:, None], pad_start[:, None], 0), axis=0)
            return start + rank_row.astype(jnp.int32)

        slots = [(one(rt[0], rt[4]), one(rt[1], rt[5])) for rt in rts]
        sa, sb, tag_bases, spare0, dest0 = [], [], [], n_slots, 0
        for tab, (a, b), n in zip(h3_tables, slots, n_real):
            n_spare = tab.shape[0] - n
            spare = spare0 + jnp.arange(n_spare, dtype=jnp.int32)
            sa.append(jnp.concatenate([a, spare]))
            sb.append(jnp.concatenate([b, spare + n_spare]))
            spare0 += 2 * n_spare
            tag_bases.append((dest0, dest0 + n))
            dest0 += 2 * n
        xs, tags = _scatter_rows2(tuple(h3_tables), tuple(sa), tuple(sb), tuple(tag_bases), spare0)
        ys, dest_blocks = _experts(xs, tags, n_slots, dest0, pad_start // BM, padded // BM, counts, *w_e)
        slot = jnp.arange(n_slots, dtype=jnp.int32)
        dest = dest_blocks.reshape(n_slots // BM, SUBLANES, BM)[:, 0, :].reshape(-1)
        dest = jnp.where(slot < jnp.sum(padded), dest, dest0 + slot)
        return _scatter_back(ys, dest, dest0 + n_slots)

    yg = moe_pass(cnt, [h3_p, h3_s], [rt_p, rt_s], [n_p, n_s])

    gf = row(g_final)
    y_p = _combine(x2_p, yg, rt_p, gf, TM_COMBINE, 0, n_p // TM_COMBINE)
    y_s = _combine(x2_s, yg, rt_s, gf, n_s, 2 * n_p // n_s, 2 * n_p // n_s + 1)

    return (y_p.reshape(1, n_p, D_MODEL),
            y_s.reshape(n_batch, t_len, D_MODEL),
            hist_p[HALO - HIST:].reshape(1, 1, HIST, D_CONV),
            hist_s.reshape(1, n_batch, HIST, D_CONV),
            k_p.reshape(1, 1, N_MEM, X_HEADS, X_HEAD_DIM),
            v_p.reshape(1, 1, N_MEM, X_HEADS, X_HEAD_DIM),
            sgv_s.reshape(1, n_batch, t_len, D_SG))
```

```python
import functools

import jax
import jax.numpy as jnp
from jax import lax
from jax.experimental import pallas as pl
from jax.experimental.pallas import tpu as pltpu
from jax.experimental.pallas import tpu_sc as plsc

D_MODEL = 1024
D_CONV = 512
D_SG = 512
CONV_WIDTH = 31
HIST = CONV_WIDTH - 1
SG_HEADS = 4
SG_HEAD_DIM = 128
SG_CHUNK = 128
N_MEM = 256
X_HEADS = 4
X_HEAD_DIM = 256
N_GROUPS = 4
EXPERTS_PER_GROUP = 8
N_EXPERTS = 32
D_EXPERT = 512
EPS = 1e-6

LANES = 128
SUBLANES = 8
SC_CORES = 2
SC_SUBCORES = 16
SC_WORKERS = SC_CORES * SC_SUBCORES
SC_LANES = 16
VMEM_LIMIT = 56 * 1024 * 1024

TM = 512
TM_COMBINE = 2048
ROUTE_ROWS = 4096
HALO = 32
SEG = TM // SUBLANES
SEG_HALO = 32
CONV_BLOCK = 16
CAST_ROWS = 64
BM = 256
X_LOOKAHEAD = 4
X_BUFS = X_LOOKAHEAD + 2
Y_BUFS = 4
ROW_DMA_PRIORITY = 1
SCATTER_CHUNK = 32
SCATTER_BUFS = 4
TAG_WORDS = 128
BACK_CHUNK = 56
BACK_BUFS = 4
LOGIT_LANES = 128

F32 = jnp.float32
BF16 = jnp.bfloat16


def _dot(a, b):
    return jnp.dot(a, b, preferred_element_type=F32)


def _rms(x, g):
    return x * lax.rsqrt(jnp.mean(x * x, axis=-1, keepdims=True) + EPS) * g


def _ln(x, g, b):
    mu = jnp.mean(x, axis=-1, keepdims=True)
    xc = x - mu
    var = jnp.mean(xc * xc, axis=-1, keepdims=True)
    return xc * lax.rsqrt(var + EPS) * g + b


def _sigmoid(x):
    return 1.0 / (1.0 + jnp.exp(-x))


def _pack_bf16_pairs(h):
    bits = lax.bitcast_convert_type(h, jnp.uint32)
    half = h.shape[1] // 2
    lo = lax.shift_right_logical(bits[:, :half], jnp.uint32(16))
    hi = bits[:, half:] & jnp.uint32(0xFFFF0000)
    return hi | lo


def _unpack_bf16_pairs_f32(p):
    lo = lax.bitcast_convert_type(lax.shift_left(p, jnp.uint32(16)), F32)
    hi = lax.bitcast_convert_type(p & jnp.uint32(0xFFFF0000), F32)
    return lo, hi


def _unpack_bf16_pairs(p):
    lo, hi = _unpack_bf16_pairs_f32(p)
    return lo.astype(BF16), hi.astype(BF16)


def _memkv_kernel(mem_ref, g_ref, wk_ref, wv_ref, k_ref, v_ref, kbf_ref, vbf_ref):
    m = _rms(mem_ref[...], g_ref[...]).astype(BF16)
    k = _dot(m, wk_ref[...].astype(BF16))
    v = _dot(m, wv_ref[...].astype(BF16))
    k_ref[...] = k
    v_ref[...] = v
    kbf_ref[...] = k.astype(BF16)
    vbf_ref[...] = v.astype(BF16)


def _memkv(mem, g_mem, w_mk, w_mv):
    return pl.pallas_call(
        _memkv_kernel,
        out_shape=(jax.ShapeDtypeStruct((N_MEM, D_MODEL), F32), jax.ShapeDtypeStruct((N_MEM, D_MODEL), F32),
                   jax.ShapeDtypeStruct((N_MEM, D_MODEL), BF16), jax.ShapeDtypeStruct((N_MEM, D_MODEL), BF16)),
        compiler_params=pltpu.CompilerParams(vmem_limit_bytes=VMEM_LIMIT),
        name="memkv",
    )(mem, g_mem, w_mk, w_mv)


def _attn_heads(q, k, v, k_transposed):
    outs = []
    for h in range(X_HEADS):
        sl = slice(h * X_HEAD_DIM, (h + 1) * X_HEAD_DIM)
        if k_transposed:
            s = _dot(q[:, sl], k[sl, :])
        else:
            s = lax.dot_general(q[:, sl], k[:, sl], (((1,), (1,)), ((), ())), preferred_element_type=F32)
        s = s * (X_HEAD_DIM ** -0.5)
        s = s - jnp.max(s, axis=-1, keepdims=True)
        p = jnp.exp(s)
        p = p / jnp.sum(p, axis=-1, keepdims=True)
        outs.append(_dot(p.astype(BF16), v[:, sl]).astype(BF16))
    return jnp.concatenate(outs, axis=1)


def _route(logits, run, strict_lower):
    m = logits.shape[0]
    r = strict_lower.shape[0]
    lane = lax.broadcasted_iota(jnp.int32, (m, LOGIT_LANES), 1).astype(F32)
    neg = jnp.float32(-jnp.inf)
    big = jnp.float32(LOGIT_LANES)

    def first_argmax(vals):
        mx = jnp.max(vals, axis=-1, keepdims=True)
        idx = jnp.min(jnp.where(vals == mx, lane, big), axis=-1, keepdims=True)
        return mx, idx

    lg = jnp.where(lane < N_GROUPS, logits, neg)
    g_max, g_idx = first_argmax(lg)
    g_w = 1.0 / jnp.sum(jnp.exp(lg - g_max), axis=-1, keepdims=True)

    lo = N_GROUPS + g_idx * EXPERTS_PER_GROUP
    le = jnp.where((lane >= lo) & (lane < lo + EXPERTS_PER_GROUP), logits, neg)
    v1, i1 = first_argmax(le)
    v2, i2 = first_argmax(jnp.where(lane == i1, neg, le))
    t = jnp.exp(v2 - v1)
    gate1 = g_w / (1.0 + t)
    gate2 = g_w * t / (1.0 + t)
    e1 = i1 - N_GROUPS
    e2 = i2 - N_GROUPS

    oh1 = (lane == e1).astype(F32)
    oh2 = (lane == e2).astype(F32)
    oh = oh1 + oh2
    befores = []
    for r0 in range(0, m, r):
        oh_r = oh[r0:r0 + r, :]
        befores.append(_dot(strict_lower, oh_r.astype(BF16)) + run)
        run = run + jnp.sum(oh_r, axis=0, keepdims=True)
    before = befores[0] if len(befores) == 1 else jnp.concatenate(befores, axis=0)
    rank1 = jnp.sum(before * oh1, axis=-1, keepdims=True)
    rank2 = jnp.sum(before * oh2, axis=-1, keepdims=True)
    new_run = run

    rinfo = jnp.where(lane == 0, e1,
            jnp.where(lane == 1, e2,
            jnp.where(lane == 2, gate1,
            jnp.where(lane == 3, gate2,
            jnp.where(lane == 4, rank1,
            jnp.where(lane == 5, rank2, 0.0))))))
    return jnp.transpose(rinfo)[0:SUBLANES, :], new_run


def _conv_segments(a, w_ref, seg_ref, tail_ref, yseg_ref, conv_ref):
    sub = lax.broadcasted_iota(jnp.int32, (SUBLANES, LANES), 0)
    for lt in range(D_CONV // LANES):
        ls = slice(lt * LANES, (lt + 1) * LANES)
        for t0 in range(0, TM, SUBLANES):
            s, m = divmod(t0, SEG)
            seg_ref[lt, pl.ds((SEG_HALO + m) * SUBLANES + s, SUBLANES, stride=SUBLANES), :] = a[t0:t0 + SUBLANES, ls]
        for j in range(SEG_HALO):
            cur = seg_ref[lt, (SEG + j) * SUBLANES:(SEG + j + 1) * SUBLANES, :]
            prev = tail_ref[lt, j * SUBLANES:(j + 1) * SUBLANES, :]
            seg_ref[lt, j * SUBLANES:(j + 1) * SUBLANES, :] = jnp.where(
                sub == 0, pltpu.roll(prev, 1, axis=0), pltpu.roll(cur, 1, axis=0))
            tail_ref[lt, j * SUBLANES:(j + 1) * SUBLANES, :] = cur
        for m0 in range(0, SEG, CONV_BLOCK):
            acc = [jnp.zeros((SUBLANES, LANES), F32) for _ in range(CONV_BLOCK)]
            for idx in range(m0 - HIST, m0 + CONV_BLOCK):
                b = seg_ref[lt, (SEG_HALO + idx) * SUBLANES:(SEG_HALO + idx + 1) * SUBLANES, :]
                for m in range(max(m0, idx), min(m0 + CONV_BLOCK, idx + CONV_WIDTH)):
                    k = idx - m + HIST
                    acc[m - m0] = acc[m - m0] + b * w_ref[k:k + 1, ls]
            for m in range(m0, m0 + CONV_BLOCK):
                yseg_ref[lt, m * SUBLANES:(m + 1) * SUBLANES, :] = acc[m - m0]
        for t0 in range(0, TM, SUBLANES):
            s, m = divmod(t0, SEG)
            conv_ref[t0:t0 + SUBLANES, ls] = yseg_ref[lt, pl.ds(m * SUBLANES + s, SUBLANES, stride=SUBLANES), :]


def _cast_rows(src_ref, dst_ref):
    rows = src_ref.shape[0]

    def body(c, carry):
        r0 = pl.multiple_of(c * CAST_ROWS, CAST_ROWS)
        dst_ref[pl.ds(r0, CAST_ROWS), :] = src_ref[pl.ds(r0, CAST_ROWS), :].astype(BF16)
        return carry

    lax.fori_loop(0, rows // CAST_ROWS, body, 0)


def _trunk_prompt_kernel(x_ref, gmix_ref, win32_ref, convw_ref, convb_ref, lncg_ref, lncb_ref, lnvg_ref, lnvb_ref,
                         wsg_ref, bsg_ref, wout32_ref, gx_ref, wxq32_ref, kmem_ref, v_ref, wxo32_ref, gffn_ref, wr_ref,
                         br_ref,
                         x2_ref, h3_ref, logit_ref, hist_ref,
                         seg_ref, tail_ref, yseg_ref, conv_ref, win_ref, wout_ref, wxq_ref, wxo_ref):
    i = pl.program_id(0)

    @pl.when(i == 0)
    def _():
        tail_ref[...] = jnp.zeros(tail_ref.shape, F32)
        _cast_rows(win32_ref, win_ref)
        _cast_rows(wout32_ref, wout_ref)
        _cast_rows(wxq32_ref, wxq_ref)
        _cast_rows(wxo32_ref, wxo_ref)

    x = x_ref[...]
    h = _rms(x, gmix_ref[...]).astype(BF16)

    a_in = _dot(h, win_ref[:, 0:D_CONV])
    a_gate = _dot(h, win_ref[:, D_CONV:2 * D_CONV])
    a = a_in * _sigmoid(a_gate)
    hist_ref[...] = a[TM - HALO:, :]
    _conv_segments(a, convw_ref, seg_ref, tail_ref, yseg_ref, conv_ref)

    y = _ln(conv_ref[...] + convb_ref[...], lncg_ref[...], lncb_ref[...])
    a_out = (y * _sigmoid(y)).astype(BF16)

    u = _dot(h, win_ref[:, 2 * D_CONV:2 * D_CONV + D_SG])
    v = _ln(_dot(h, win_ref[:, 2 * D_CONV + D_SG:]), lnvg_ref[...], lnvb_ref[...]).astype(BF16)
    ri = lax.broadcasted_iota(jnp.int32, (SG_CHUNK, SG_CHUNK), 0)
    ci = lax.broadcasted_iota(jnp.int32, (SG_CHUNK, SG_CHUNK), 1)
    w_tril = [jnp.where(ci <= ri, wsg_ref[hh], 0.0).astype(BF16) for hh in range(SG_HEADS)]
    gate_rows = []
    for c in range(TM // SG_CHUNK):
        rs = slice(c * SG_CHUNK, (c + 1) * SG_CHUNK)
        heads = [_dot(w_tril[hh], v[rs, hh * SG_HEAD_DIM:(hh + 1) * SG_HEAD_DIM]) for hh in range(SG_HEADS)]
        gate_rows.append(jnp.concatenate(heads, axis=1) + bsg_ref[...])
    b_out = (u * jnp.concatenate(gate_rows, axis=0)).astype(BF16)

    x1 = x + _dot(a_out, wout_ref[0:D_CONV, :]) + _dot(b_out, wout_ref[D_CONV:, :])

    hx = _rms(x1, gx_ref[...]).astype(BF16)
    q = _dot(hx, wxq_ref[...]).astype(BF16)
    x2 = x1 + _dot(_attn_heads(q, kmem_ref[...], v_ref[...], False), wxo_ref[...])
    x2_ref[...] = x2

    h3 = _rms(x2, gffn_ref[...]).astype(BF16)
    h3_ref[...] = _pack_bf16_pairs(h3.astype(F32))
    logit_ref[...] = _dot(h3, wr_ref[...]) + br_ref[...]


def _router_kernel(logit_ref, upper_ref, ones_ref, rt_ref, cnt_ref, run_ref):
    @pl.when(pl.program_id(0) == 0)
    def _():
        run_ref[...] = jnp.zeros(run_ref.shape, F32)

    n = ROUTE_ROWS
    lt = jnp.transpose(logit_ref[...])
    neg = jnp.float32(-jnp.inf)
    big = jnp.float32(LOGIT_LANES)

    def first_argmax(vals, rows):
        mx = jnp.max(vals, axis=0, keepdims=True)
        idx = jnp.min(jnp.where(vals == mx, rows, big), axis=0, keepdims=True)
        return mx, idx

    row8 = lax.broadcasted_iota(jnp.int32, (SUBLANES, n), 0).astype(F32)
    lg = jnp.where(row8 < N_GROUPS, lt[0:SUBLANES, :], neg)
    g_max, g_idx = first_argmax(lg, row8)
    g_w = 1.0 / jnp.sum(jnp.exp(lg - g_max), axis=0, keepdims=True)

    n_rows = N_GROUPS + N_EXPERTS + (-(N_GROUPS + N_EXPERTS)) % SUBLANES
    rows = lax.broadcasted_iota(jnp.int32, (n_rows, n), 0).astype(F32)
    lo = N_GROUPS + g_idx * EXPERTS_PER_GROUP
    le = jnp.where((rows >= lo) & (rows < lo + EXPERTS_PER_GROUP), lt[0:n_rows, :], neg)
    v1, i1 = first_argmax(le, rows)
    v2, i2 = first_argmax(jnp.where(rows == i1, neg, le), rows)
    t = jnp.exp(v2 - v1)
    gate1 = g_w / (1.0 + t)
    gate2 = g_w * t / (1.0 + t)
    e1 = i1 - N_GROUPS
    e2 = i2 - N_GROUPS

    erow = lax.broadcasted_iota(jnp.int32, (LOGIT_LANES, n), 0).astype(F32)
    oh1 = (erow == e1).astype(F32)
    oh2 = (erow == e2).astype(F32)
    oh = (oh1 + oh2).astype(BF16)
    r = upper_ref.shape[0]
    run = run_ref[...]
    rank1, rank2 = [], []
    for c0 in range(0, n, r):
        cs = slice(c0, c0 + r)
        before = _dot(oh[:, cs], upper_ref[...]) + run
        rank1.append(jnp.sum(before * oh1[:, cs], axis=0, keepdims=True))
        rank2.append(jnp.sum(before * oh2[:, cs], axis=0, keepdims=True))
        run = run + _dot(oh[:, cs], ones_ref[...])
    run_ref[...] = run
    cnt_ref[...] = run[:, 0:LANES]

    sub = lax.broadcasted_iota(jnp.int32, (SUBLANES, n), 0)
    vals = (e1, e2, gate1, gate2, jnp.concatenate(rank1, axis=1), jnp.concatenate(rank2, axis=1))
    rt = jnp.zeros((SUBLANES, n), F32)
    for k, v in enumerate(vals):
        rt = jnp.where(sub == k, v, rt)
    rt_ref[...] = rt


def _router(logits, rank_block):
    n = logits.shape[0]
    assert n % ROUTE_ROWS == 0 and ROUTE_ROWS % rank_block == 0
    upper = jnp.triu(jnp.ones((rank_block, rank_block), BF16), 1)
    ones = jnp.ones((rank_block, rank_block), BF16)
    return pl.pallas_call(
        _router_kernel,
        grid=(n // ROUTE_ROWS,),
        in_specs=[pl.BlockSpec((ROUTE_ROWS, LOGIT_LANES), lambda i: (i, 0)),
                  pl.BlockSpec(upper.shape, lambda i: (0, 0)),
                  pl.BlockSpec(ones.shape, lambda i: (0, 0))],
        out_specs=(pl.BlockSpec((SUBLANES, ROUTE_ROWS), lambda i: (0, i)),
                   pl.BlockSpec((LOGIT_LANES, LANES), lambda i: (0, 0))),
        out_shape=(jax.ShapeDtypeStruct((SUBLANES, n), F32), jax.ShapeDtypeStruct((LOGIT_LANES, LANES), F32)),
        scratch_shapes=[pltpu.VMEM((LOGIT_LANES, rank_block), F32)],
        compiler_params=pltpu.CompilerParams(dimension_semantics=("arbitrary",), vmem_limit_bytes=VMEM_LIMIT),
        name="router",
    )(logits, upper, ones)


def _const_spec(shape):
    nd = len(shape)
    return pl.BlockSpec(shape, lambda i: (0,) * nd, pipeline_mode=pl.Buffered(1))


def _trunk_prompt(x, p):
    n = x.shape[0]
    assert n % TM == 0
    row = lambda w: pl.BlockSpec((TM, w), lambda i: (i, 0))
    consts = [p["g_mix"], p["w_in"], p["conv_w"], p["conv_b"], p["ln_conv_g"], p["ln_conv_b"], p["ln_v_g"],
              p["ln_v_b"], p["w_sg"], p["b_sg_rows"], p["w_out"], p["g_xattn"], p["w_xq"], p["k"], p["v"],
              p["w_xo"], p["g_ffn"], p["w_router"], p["b_router"]]
    return pl.pallas_call(
        _trunk_prompt_kernel,
        grid=(n // TM,),
        in_specs=[row(D_MODEL)] + [_const_spec(c.shape) for c in consts],
        out_specs=(row(D_MODEL), row(D_MODEL // 2), row(LOGIT_LANES),
                   pl.BlockSpec((HALO, D_CONV), lambda i: (0, 0))),
        out_shape=(jax.ShapeDtypeStruct((n, D_MODEL), F32),
                   jax.ShapeDtypeStruct((n, D_MODEL // 2), jnp.uint32),
                   jax.ShapeDtypeStruct((n, LOGIT_LANES), F32),
                   jax.ShapeDtypeStruct((HALO, D_CONV), F32)),
        scratch_shapes=[pltpu.VMEM((D_CONV // LANES, (SEG_HALO + SEG) * SUBLANES, LANES), F32),
                        pltpu.VMEM((D_CONV // LANES, SEG_HALO * SUBLANES, LANES), F32),
                        pltpu.VMEM((D_CONV // LANES, TM, LANES), F32),
                        pltpu.VMEM((TM, D_CONV), F32),
                        pltpu.VMEM(p["w_in"].shape, BF16), pltpu.VMEM(p["w_out"].shape, BF16),
                        pltpu.VMEM(p["w_xq"].shape, BF16), pltpu.VMEM(p["w_xo"].shape, BF16)],
        compiler_params=pltpu.CompilerParams(dimension_semantics=("arbitrary",), vmem_limit_bytes=VMEM_LIMIT),
        name="trunk_prompt",
    )(x, *consts)


def _trunk_sample_kernel(n_batch, t_len,
                         x_ref, hist_in_ref, run_in_ref, gmix_ref, win_ref, convw_ref, convb_ref, lncg_ref, lncb_ref,
                         lnvg_ref, lnvb_ref, wsgbd_ref, bsg_ref, wout_ref, gx_ref, wxq_ref, kmem_ref, v_ref, wxo_ref,
                         gffn_ref, wr_ref, br_ref, lower_ref,
                         x2_ref, h3_ref, rt_ref, hist_ref, sgv_ref, cnt_ref,
                         ext_ref, conv_ref, att_ref):
    x = x_ref[...]
    h = _rms(x, gmix_ref[...]).astype(BF16)
    z = _dot(h, win_ref[...].astype(BF16))
    a = z[:, 0:D_CONV] * _sigmoid(z[:, D_CONV:2 * D_CONV])
    ext_len = HIST + t_len
    for b in range(n_batch):
        ext_ref[b, 0:HIST, :] = hist_in_ref[b]
        ext_ref[b, HIST:ext_len, :] = a[b * t_len:(b + 1) * t_len, :]
    for b in range(n_batch):
        acc = jnp.zeros((t_len, D_CONV), F32)
        for k in range(CONV_WIDTH):
            acc = acc + ext_ref[b, k:k + t_len, :] * convw_ref[k:k + 1, :]
        conv_ref[b * t_len:(b + 1) * t_len, :] = acc
        hist_ref[b] = ext_ref[b, ext_len - HIST:ext_len, :]

    y = _ln(conv_ref[...] + convb_ref[...], lncg_ref[...], lncb_ref[...])
    a_out = (y * _sigmoid(y)).astype(BF16)

    u = z[:, 2 * D_CONV:2 * D_CONV + D_SG]
    v = _ln(z[:, 2 * D_CONV + D_SG:], lnvg_ref[...], lnvb_ref[...])
    sgv_ref[...] = v
    vb = v.astype(BF16)
    heads = [_dot(wsgbd_ref[hh], vb[:, hh * SG_HEAD_DIM:(hh + 1) * SG_HEAD_DIM]) for hh in range(SG_HEADS)]
    b_out = (u * (jnp.concatenate(heads, axis=1) + bsg_ref[...])).astype(BF16)

    x1 = (x + _dot(a_out, wout_ref[0:D_CONV, :].astype(BF16))
          + _dot(b_out, wout_ref[D_CONV:, :].astype(BF16)))

    hx = _rms(x1, gx_ref[...]).astype(BF16)
    q = _dot(hx, wxq_ref[...].astype(BF16)).astype(BF16)
    for b in range(n_batch):
        rs = slice(b * t_len, (b + 1) * t_len)
        att_ref[rs, :] = _attn_heads(q[rs, :], kmem_ref[b], v_ref[b], True)
    x2 = x1 + _dot(att_ref[...], wxo_ref[...].astype(BF16))
    x2_ref[...] = x2

    h3 = _rms(x2, gffn_ref[...]).astype(BF16)
    m = n_batch * t_len
    h3_ref[0:m, :] = _pack_bf16_pairs(h3.astype(F32))
    if h3_ref.shape[0] > m:
        h3_ref[m:, :] = jnp.zeros((h3_ref.shape[0] - m, D_MODEL // 2), jnp.uint32)
    rt, new_run = _route(_dot(h3, wr_ref[...]) + br_ref[...], run_in_ref[...], lower_ref[...])
    rt_ref[...] = rt
    cnt_ref[...] = new_run


def _trunk_sample(x, hist, run, p, n_batch, t_len):
    m = n_batch * t_len
    args = [x, hist, run, p["g_mix"], p["w_in"], p["conv_w"], p["conv_b"], p["ln_conv_g"], p["ln_conv_b"],
            p["ln_v_g"], p["ln_v_b"], p["w_sg_bd"], p["b_sg_rows_s"], p["w_out"], p["g_xattn"], p["w_xq"],
            p["k_s"], p["v_s"], p["w_xo"], p["g_ffn"], p["w_router"], p["b_router"], p["lower"]]
    return pl.pallas_call(
        functools.partial(_trunk_sample_kernel, n_batch, t_len),
        out_shape=(jax.ShapeDtypeStruct((m, D_MODEL), F32),
                   jax.ShapeDtypeStruct((-(-m // (SC_WORKERS * SUBLANES)) * SC_WORKERS * SUBLANES, D_MODEL // 2),
                                        jnp.uint32),
                   jax.ShapeDtypeStruct((SUBLANES, m), F32),
                   jax.ShapeDtypeStruct((n_batch, HIST, D_CONV), F32),
                   jax.ShapeDtypeStruct((m, D_SG), F32),
                   jax.ShapeDtypeStruct((1, LOGIT_LANES), F32)),
        scratch_shapes=[pltpu.VMEM((n_batch, HIST + t_len, D_CONV), F32),
                        pltpu.VMEM((m, D_CONV), F32),
                        pltpu.VMEM((m, D_MODEL), BF16)],
        compiler_params=pltpu.CompilerParams(vmem_limit_bytes=VMEM_LIMIT),
        name="trunk_sample",
    )(*args)


def _sc_worker_id():
    return lax.axis_index("s") * SC_CORES + lax.axis_index("c")


def _sc_chunk(per_w, max_chunk):
    assert per_w % SUBLANES == 0 and max_chunk <= LANES
    return max(c for c in range(SUBLANES, max_chunk + 1, SUBLANES) if per_w % c == 0)


def _sc_scatter_rows2(tables, slots_a, slots_b, tag_bases, n_rows_out, max_chunk):
    d, dtype = tables[0].shape[1], tables[0].dtype
    plans = []
    for t in tables:
        per_w = t.shape[0] // SC_WORKERS
        assert per_w * SC_WORKERS == t.shape[0]
        chunk = _sc_chunk(per_w, max_chunk)
        plans.append((per_w, chunk, per_w // chunk))
    cmax = max(c for _, c, _ in plans)
    n_t = len(tables)
    mesh = plsc.VectorSubcoreMesh(core_axis_name="c", subcore_axis_name="s")

    nb = SCATTER_BUFS
    lag = nb // 3
    scratch = []
    for _, chunk, _ in plans:
        for _ in range(nb):
            scratch += [pltpu.VMEM((chunk,), jnp.int32), pltpu.VMEM((chunk,), jnp.int32)]
    scratch += [pltpu.VMEM((cmax, d), dtype)] * nb
    scratch += [pltpu.VMEM((cmax, TAG_WORDS), jnp.int32)] * (2 * nb)
    scratch += [pltpu.SemaphoreType.DMA] * (2 * nb)

    @functools.partial(pl.kernel, mesh=mesh,
                       out_type=(jax.ShapeDtypeStruct((n_rows_out, d), dtype),
                                 jax.ShapeDtypeStruct((n_rows_out, TAG_WORDS), jnp.int32)),
                       scratch_types=scratch)
    def scatter(*refs):
        tab_hbm = refs[0:n_t]
        sa_hbm = refs[n_t:2 * n_t]
        sb_hbm = refs[2 * n_t:3 * n_t]
        out_hbm, tag_hbm = refs[3 * n_t], refs[3 * n_t + 1]
        sc = refs[3 * n_t + 2:]
        idx_refs = sc[:2 * nb * n_t]
        rows = sc[2 * nb * n_t:2 * nb * n_t + nb]
        tagbufs = sc[2 * nb * n_t + nb:2 * nb * n_t + 3 * nb]
        lsem = sc[2 * nb * n_t + 3 * nb:2 * nb * n_t + 4 * nb]
        ssem = sc[2 * nb * n_t + 4 * nb:]
        wid = _sc_worker_id()

        work = []
        for t, (per_w, chunk, n_chunks) in enumerate(plans):
            for j in range(n_chunks):
                work.append((t, wid * per_w + j * chunk, chunk))

        def parts(k):
            t, off, chunk = work[k]
            b = k % nb
            ia, ib = idx_refs[2 * nb * t + 2 * b], idx_refs[2 * nb * t + 2 * b + 1]
            full = chunk == cmax
            rv = rows[b] if full else rows[b].at[pl.ds(0, chunk)]
            ta = tagbufs[2 * b] if full else tagbufs[2 * b].at[pl.ds(0, chunk)]
            tb = tagbufs[2 * b + 1] if full else tagbufs[2 * b + 1].at[pl.ds(0, chunk)]
            return t, off, chunk, b, ia, ib, rv, ta, tb

        def start_load(k):
            t, off, chunk, b, ia, ib, rv, ta, tb = parts(k)
            return (pltpu.async_copy(tab_hbm[t].at[pl.ds(off, chunk)], rv, lsem[b]),
                    pltpu.async_copy(sa_hbm[t].at[pl.ds(off, chunk)], ia, lsem[b]),
                    pltpu.async_copy(sb_hbm[t].at[pl.ds(off, chunk)], ib, lsem[b]))

        def start_scatter(k):
            t, off, chunk, b, ia, ib, rv, ta, tb = parts(k)
            base_a, base_b = tag_bases[t]
            for r in range(chunk):
                row_id = (off + r).astype(jnp.int32)
                tagbufs[2 * b][r, pl.ds(0, SC_LANES)] = jnp.zeros((SC_LANES,), jnp.int32) + (base_a + row_id)
                tagbufs[2 * b + 1][r, pl.ds(0, SC_LANES)] = jnp.zeros((SC_LANES,), jnp.int32) + (base_b + row_id)
            return (pltpu.async_copy(rv, out_hbm.at[ia], ssem[b]), pltpu.async_copy(rv, out_hbm.at[ib], ssem[b]),
                    pltpu.async_copy(ta, tag_hbm.at[ia], ssem[b]), pltpu.async_copy(tb, tag_hbm.at[ib], ssem[b]))

        loads, scatters = {}, {}
        for k in range(len(work) + lag):
            if k < len(work):
                if k >= nb:
                    for c in scatters.pop(k - nb):
                        c.wait()
                loads[k] = start_load(k)
            w = k - lag
            if w >= 0:
                for c in loads.pop(w):
                    c.wait()
                scatters[w] = start_scatter(w)
        for w in sorted(scatters):
            for c in scatters[w]:
                c.wait()

    return scatter(*tables, *slots_a, *slots_b)


def _sc_scatter_back(ys, dest, n_rows_out):
    n_rows, d = ys.shape
    per_w = n_rows // SC_WORKERS
    assert per_w * SC_WORKERS == n_rows
    chunk = _sc_chunk(per_w, BACK_CHUNK)
    n_chunks = per_w // chunk
    nb = BACK_BUFS
    lag = nb // 2
    mesh = plsc.VectorSubcoreMesh(core_axis_name="c", subcore_axis_name="s")

    @functools.partial(
        pl.kernel, mesh=mesh,
        out_type=jax.ShapeDtypeStruct((n_rows_out, d), ys.dtype),
        scratch_types=([pltpu.VMEM((chunk,), jnp.int32)] * nb + [pltpu.VMEM((chunk, d), ys.dtype)] * nb
                       + [pltpu.SemaphoreType.DMA] * (2 * nb)),
    )
    def scatter_back(ys_hbm, dest_hbm, out_hbm, *rest):
        idx = rest[:nb]
        rows = rest[nb:2 * nb]
        lsem = rest[2 * nb:3 * nb]
        ssem = rest[3 * nb:]
        base = _sc_worker_id() * per_w

        loads, scatters = {}, {}
        for k in range(n_chunks + lag):
            if k < n_chunks:
                b = k % nb
                if k >= nb:
                    scatters.pop(k - nb).wait()
                off = base + k * chunk
                loads[k] = (pltpu.async_copy(ys_hbm.at[pl.ds(off, chunk)], rows[b], lsem[b]),
                            pltpu.async_copy(dest_hbm.at[pl.ds(off, chunk)], idx[b], lsem[b]))
            w = k - lag
            if w >= 0:
                b = w % nb
                for c in loads.pop(w):
                    c.wait()
                scatters[w] = pltpu.async_copy(rows[b], out_hbm.at[idx[b]], ssem[b])
        for w in sorted(scatters):
            scatters[w].wait()

    return scatter_back(ys, dest)


def _experts_kernel(dump_base, first_ref, nblk_ref, cnt_ref, tot_ref, xs_hbm, tag_hbm, wg_ref, wu_ref, wd_ref,
                    ys_hbm, dest_hbm, xbuf, tbuf, ybuf, dbuf, wg_bf, wu_bf, wd_bf, in_sem, tin_sem, out_sem, dout_sem):
    e = pl.program_id(0)
    nb = nblk_ref[e]
    first = first_ref[e]
    cnt = cnt_ref[e]
    total = tot_ref[0]
    half = D_MODEL // 2

    def in_copies(gb):
        slot = lax.rem(gb, X_BUFS)
        return (pltpu.make_async_copy(xs_hbm.at[pl.ds(gb * BM, BM)], xbuf.at[slot], in_sem.at[slot]),
                pltpu.make_async_copy(tag_hbm.at[pl.ds(gb * BM, BM)], tbuf.at[slot], tin_sem.at[slot]))

    def out_copies(gb):
        slot = lax.rem(gb, Y_BUFS)
        return (pltpu.make_async_copy(ybuf.at[slot], ys_hbm.at[pl.ds(gb * BM, BM)], out_sem.at[slot]),
                pltpu.make_async_copy(dbuf.at[slot], dest_hbm.at[pl.ds(gb * SUBLANES, SUBLANES)], dout_sem.at[slot]))

    def start_in(gb):
        for c in in_copies(gb):
            c.start(priority=ROW_DMA_PRIORITY)

    def start_out(gb):
        for c in out_copies(gb):
            c.start(priority=ROW_DMA_PRIORITY)

    def wait_out(gb):
        for c in out_copies(gb):
            c.wait()

    @pl.when(nb > 0)
    def _():
        @pl.when(first == 0)
        def _():
            for k in range(X_LOOKAHEAD):
                @pl.when(k < total)
                def _():
                    start_in(k)

        wg_bf[...] = wg_ref[0].astype(BF16)
        wu_bf[...] = wu_ref[0].astype(BF16)
        wd_bf[...] = wd_ref[0].astype(BF16)

        def acquire(gb):
            @pl.when(gb + X_LOOKAHEAD < total)
            def _():
                start_in(gb + X_LOOKAHEAD)

            for c in in_copies(gb):
                c.wait()

            @pl.when(gb >= Y_BUFS)
            def _():
                wait_out(gb - Y_BUFS)

        def ffn(gb, j):
            n_live = cnt - j * BM
            tags_t = jnp.transpose(tbuf[lax.rem(gb, X_BUFS)].astype(F32))
            lane = lax.broadcasted_iota(jnp.int32, (SUBLANES, BM), 1)
            own = (dump_base + gb * BM + lane).astype(F32)
            dest = jnp.where(lane < n_live, jnp.broadcast_to(tags_t[0:1, :], (SUBLANES, BM)), own)
            dbuf[lax.rem(gb, Y_BUFS)] = dest.astype(jnp.int32)
            live = lax.broadcasted_iota(jnp.int32, (BM, half), 0) < n_live
            lo, hi = _unpack_bf16_pairs(jnp.where(live, xbuf[lax.rem(gb, X_BUFS)], jnp.uint32(0)))
            g = _dot(lo, wg_bf[0:half, :]) + _dot(hi, wg_bf[half:, :])
            u = _dot(lo, wu_bf[0:half, :]) + _dot(hi, wu_bf[half:, :])
            hm = (g * _sigmoid(g) * u).astype(BF16)
            y = _dot(hm, wd_bf[...])
            ybuf[lax.rem(gb, Y_BUFS)] = _pack_bf16_pairs(y.astype(BF16).astype(F32))

        def block_pair(jp, carry):
            j0 = 2 * jp
            g0 = first + j0
            acquire(g0)
            acquire(g0 + 1)
            ffn(g0, j0)
            ffn(g0 + 1, j0 + 1)
            start_out(g0)
            start_out(g0 + 1)
            return carry

        lax.fori_loop(0, nb // 2, block_pair, 0)

        @pl.when(lax.rem(nb, 2) == 1)
        def _():
            gl = first + nb - 1
            acquire(gl)
            ffn(gl, nb - 1)
            start_out(gl)

        @pl.when(first + nb == total)
        def _():
            for k in range(Y_BUFS):
                @pl.when(total - 1 - k >= 0)
                def _():
                    wait_out(total - 1 - k)


def _experts(xs, tags, n_rows_out, dump_base, first_block, n_blocks_e, counts, w_eg, w_eu, w_ed):
    w_map = lambda e, fb, nb, ct, tot: (e, 0, 0)
    half = D_MODEL // 2
    total = jnp.sum(n_blocks_e).astype(jnp.int32).reshape(1)
    n_blocks = n_rows_out // BM
    return pl.pallas_call(
        functools.partial(_experts_kernel, dump_base),
        grid_spec=pltpu.PrefetchScalarGridSpec(
            num_scalar_prefetch=4,
            grid=(N_EXPERTS,),
            in_specs=[pl.BlockSpec(memory_space=pl.ANY),
                      pl.BlockSpec(memory_space=pl.ANY),
                      pl.BlockSpec((1, D_MODEL, D_EXPERT), w_map),
                      pl.BlockSpec((1, D_MODEL, D_EXPERT), w_map),
                      pl.BlockSpec((1, D_EXPERT, D_MODEL), w_map)],
            out_specs=(pl.BlockSpec(memory_space=pl.ANY), pl.BlockSpec(memory_space=pl.ANY)),
            scratch_shapes=[pltpu.VMEM((X_BUFS, BM, half), jnp.uint32), pltpu.VMEM((X_BUFS, BM, TAG_WORDS), jnp.int32),
                            pltpu.VMEM((Y_BUFS, BM, half), jnp.uint32), pltpu.VMEM((Y_BUFS, SUBLANES, BM), jnp.int32),
                            pltpu.VMEM((D_MODEL, D_EXPERT), BF16), pltpu.VMEM((D_MODEL, D_EXPERT), BF16),
                            pltpu.VMEM((D_EXPERT, D_MODEL), BF16),
                            pltpu.SemaphoreType.DMA((X_BUFS,)), pltpu.SemaphoreType.DMA((X_BUFS,)),
                            pltpu.SemaphoreType.DMA((Y_BUFS,)), pltpu.SemaphoreType.DMA((Y_BUFS,))]),
        out_shape=(jax.ShapeDtypeStruct((n_rows_out, half), jnp.uint32),
                   jax.ShapeDtypeStruct((n_blocks * SUBLANES, BM), jnp.int32)),
        compiler_params=pltpu.CompilerParams(dimension_semantics=("arbitrary",), vmem_limit_bytes=VMEM_LIMIT),
        name="experts",
    )(first_block, n_blocks_e, counts, total, xs, tags, w_eg, w_eu, w_ed)


def _combine_kernel(x2_ref, y1_ref, y2_ref, rt_ref, g_ref, o_ref):
    rt = rt_ref[...]
    r = jnp.transpose(jnp.concatenate([rt, jnp.zeros((LANES - rt.shape[0], rt.shape[1]), F32)], axis=0))
    g1, g2 = r[:, 2:3], r[:, 3:4]
    half = D_MODEL // 2
    y1_lo, y1_hi = _unpack_bf16_pairs_f32(y1_ref[...])
    y2_lo, y2_hi = _unpack_bf16_pairs_f32(y2_ref[...])
    x_lo = x2_ref[:, 0:half] + g1 * y1_lo + g2 * y2_lo
    x_hi = x2_ref[:, half:] + g1 * y1_hi + g2 * y2_hi
    ms = (jnp.sum(x_lo * x_lo, axis=-1, keepdims=True) + jnp.sum(x_hi * x_hi, axis=-1, keepdims=True)) / D_MODEL
    inv = lax.rsqrt(ms + EPS)
    o_ref[:, 0:half] = x_lo * inv * g_ref[:, 0:half]
    o_ref[:, half:] = x_hi * inv * g_ref[:, half:]


def _combine(x2, yg, rt, g_final, tm, blk1, blk2):
    n = x2.shape[0]
    return pl.pallas_call(
        _combine_kernel,
        grid=(n // tm,),
        in_specs=[pl.BlockSpec((tm, D_MODEL), lambda i: (i, 0)),
                  pl.BlockSpec((tm, D_MODEL // 2), lambda i: (blk1 + i, 0)),
                  pl.BlockSpec((tm, D_MODEL // 2), lambda i: (blk2 + i, 0)),
                  pl.BlockSpec((SUBLANES, tm), lambda i: (0, i)),
                  pl.BlockSpec((1, D_MODEL), lambda i: (0, 0))],
        out_specs=pl.BlockSpec((tm, D_MODEL), lambda i: (i, 0)),
        out_shape=jax.ShapeDtypeStruct((n, D_MODEL), F32),
        compiler_params=pltpu.CompilerParams(dimension_semantics=("arbitrary",), vmem_limit_bytes=VMEM_LIMIT),
        name="combine",
    )(x2, yg, yg, rt, g_final)


def _scatter_back(ys, dest, n_rows_out):
    return _sc_scatter_back(ys, dest, n_rows_out)


def _scatter_rows2(tables, slots_a, slots_b, tag_bases, n_rows_out):
    return _sc_scatter_rows2(tables, slots_a, slots_b, tag_bases, n_rows_out, SCATTER_CHUNK)


def kernel(x_prompt, x_sample, mem_prompt, state_conv, cache_mem_k, cache_mem_v, g_mix, w_in, conv_w, conv_b, ln_conv_g, ln_conv_b, ln_v_g, ln_v_b, w_sg, b_sg, w_out, g_mem, w_mk, w_mv, g_xattn, w_xq, w_xo, g_ffn, w_router_group, b_router_group, w_router_expert, b_router_expert, w_expert_gate, w_expert_up, w_expert_down, g_final):
    assert x_prompt.shape[0] == 1 and g_mix.shape[0] == 1
    n_p = x_prompt.shape[1]
    n_batch, t_len = x_sample.shape[0], x_sample.shape[1]
    n_s = n_batch * t_len
    row = lambda a: a.reshape(1, -1)

    w_router = jnp.concatenate(
        [w_router_group[0], jnp.transpose(w_router_expert[0], (1, 0, 2)).reshape(D_MODEL, N_EXPERTS)], axis=1)
    w_router = jnp.pad(w_router, ((0, 0), (0, LOGIT_LANES - w_router.shape[1]))).astype(BF16)
    b_router = jnp.pad(jnp.concatenate([b_router_group[0], b_router_expert[0].reshape(-1)]),
                       (0, LOGIT_LANES - N_GROUPS - N_EXPERTS)).reshape(1, LOGIT_LANES)
    tril_t = jnp.tril(jnp.ones((t_len, t_len), bool))
    w_sg_t = jnp.where(tril_t, w_sg[0][:, :t_len, :t_len], 0.0)
    eye_b = jnp.eye(n_batch, dtype=F32)
    w_sg_bd = jnp.einsum("ab,hij->haibj", eye_b, w_sg_t).reshape(SG_HEADS, n_s, n_s).astype(BF16)
    p = {
        "g_mix": row(g_mix[0]), "w_in": w_in[0],
        "conv_w": jnp.pad(conv_w[0], ((0, 1), (0, 0))), "conv_b": row(conv_b[0]),
        "ln_conv_g": row(ln_conv_g[0]), "ln_conv_b": row(ln_conv_b[0]),
        "ln_v_g": row(ln_v_g[0]), "ln_v_b": row(ln_v_b[0]),
        "w_sg": w_sg[0],
        "b_sg_rows": jnp.repeat(b_sg[0].T, SG_HEAD_DIM, axis=1),
        "w_sg_bd": w_sg_bd,
        "b_sg_rows_s": jnp.tile(jnp.repeat(b_sg[0][:, :t_len].T, SG_HEAD_DIM, axis=1), (n_batch, 1)),
        "w_out": w_out[0], "g_xattn": row(g_xattn[0]),
        "w_xq": w_xq[0], "w_xo": w_xo[0], "g_ffn": row(g_ffn[0]),
        "w_router": w_router, "b_router": b_router,
        "lower": jnp.tril(jnp.ones((n_s, n_s), BF16), -1),
    }

    k_p, v_p, p["k"], p["v"] = _memkv(mem_prompt[0], row(g_mem[0]), w_mk[0], w_mv[0])
    p["k_s"] = jnp.transpose(cache_mem_k[0].astype(BF16), (0, 2, 3, 1)).reshape(n_batch, D_MODEL, N_MEM)
    p["v_s"] = cache_mem_v[0].astype(BF16).reshape(n_batch, N_MEM, D_MODEL)

    assert n_p % n_s == 0
    x2_p, h3_p, logits_p, hist_p = _trunk_prompt(x_prompt[0], p)
    rt_p, cnt_t = _router(logits_p, TM)
    cnt_p = cnt_t[:, 0].reshape(1, LOGIT_LANES)
    x2_s, h3_s, rt_s, hist_s, sgv_s, cnt = _trunk_sample(
        x_sample.reshape(n_s, D_MODEL), state_conv[0], cnt_p, p, n_batch, t_len)

    experts = jnp.arange(N_EXPERTS, dtype=jnp.int32)
    w_e = (w_expert_gate[0], w_expert_up[0], w_expert_down[0])

    def moe_pass(cnt, h3_tables, rts, n_real):
        n_tot = sum(n_real)
        n_slots = -(-(n_tot * 2) // BM) * BM + N_EXPERTS * BM
        counts = cnt[0, :N_EXPERTS].astype(jnp.int32)
        padded = (counts + BM - 1) // BM * BM
        pad_start = jnp.cumsum(padded) - padded

        def one(e_row, rank_row):
            e = e_row.astype(jnp.int32)
            start = jnp.sum(jnp.where(e[None, :] == experts[:, None], pad_start[:, None], 0), axis=0)
            return start + rank_row.astype(jnp.int32)

        slots = [(one(rt[0], rt[4]), one(rt[1], rt[5])) for rt in rts]
        sa, sb, tag_bases, spare0, dest0 = [], [], [], n_slots, 0
        for tab, (a, b), n in zip(h3_tables, slots, n_real):
            n_spare = tab.shape[0] - n
            spare = spare0 + jnp.arange(n_spare, dtype=jnp.int32)
            sa.append(jnp.concatenate([a, spare]))
            sb.append(jnp.concatenate([b, spare + n_spare]))
            spare0 += 2 * n_spare
            tag_bases.append((dest0, dest0 + n))
            dest0 += 2 * n
        xs, tags = _scatter_rows2(tuple(h3_tables), tuple(sa), tuple(sb), tuple(tag_bases), spare0)
        ys, dest_blocks = _experts(xs, tags, n_slots, dest0, pad_start // BM, padded // BM, counts, *w_e)
        slot = jnp.arange(n_slots, dtype=jnp.int32)
        dest = dest_blocks.reshape(n_slots // BM, SUBLANES, BM)[:, 0, :].reshape(-1)
        dest = jnp.where(slot < jnp.sum(padded), dest, dest0 + slot)
        return _scatter_back(ys, dest, dest0 + n_slots)

    yg = moe_pass(cnt, [h3_p, h3_s], [rt_p, rt_s], [n_p, n_s])

    gf = row(g_final)
    y_p = _combine(x2_p, yg, rt_p, gf, TM_COMBINE, 0, n_p // TM_COMBINE)
    y_s = _combine(x2_s, yg, rt_s, gf, n_s, 2 * n_p // n_s, 2 * n_p // n_s + 1)

    return (y_p.reshape(1, n_p, D_MODEL),
            y_s.reshape(n_batch, t_len, D_MODEL),
            hist_p[HALO - HIST:].reshape(1, 1, HIST, D_CONV),
            hist_s.reshape(1, n_batch, HIST, D_CONV),
            k_p.reshape(1, 1, N_MEM, X_HEADS, X_HEAD_DIM),
            v_p.reshape(1, 1, N_MEM, X_HEADS, X_HEAD_DIM),
            sgv_s.reshape(1, n_batch, t_len, D_SG))
```

```python
import functools

import jax
import jax.numpy as jnp
from jax import lax
from jax.experimental import pallas as pl
from jax.experimental.pallas import tpu as pltpu
from jax.experimental.pallas import tpu_sc as plsc

D_MODEL = 1024
D_CONV = 512
D_SG = 512
CONV_WIDTH = 31
HIST = CONV_WIDTH - 1
SG_HEADS = 4
SG_HEAD_DIM = 128
SG_CHUNK = 128
N_MEM = 256
X_HEADS = 4
X_HEAD_DIM = 256
N_GROUPS = 4
EXPERTS_PER_GROUP = 8
N_EXPERTS = 32
D_EXPERT = 512
EPS = 1e-6

LANES = 128
SUBLANES = 8
SC_CORES = 2
SC_SUBCORES = 16
SC_WORKERS = SC_CORES * SC_SUBCORES
SC_LANES = 16
VMEM_LIMIT = 56 * 1024 * 1024

TM = 512
TM_COMBINE = 2048
ROUTE_ROWS = 4096
HALO = 32
SEG = TM // SUBLANES
SEG_HALO = 32
CONV_BLOCK = 16
CAST_ROWS = 64
BM = 256
X_LOOKAHEAD = 4
X_BUFS = X_LOOKAHEAD + 2
Y_BUFS = 4
ROW_DMA_PRIORITY = 1
SCATTER_CHUNK = 32
SCATTER_BUFS = 4
TAG_WORDS = 128
BACK_CHUNK = 56
BACK_BUFS = 4
LOGIT_LANES = 128

F32 = jnp.float32
BF16 = jnp.bfloat16


def _dot(a, b):
    return jnp.dot(a, b, preferred_element_type=F32)


def _rms(x, g):
    return x * lax.rsqrt(jnp.mean(x * x, axis=-1, keepdims=True) + EPS) * g


def _ln(x, g, b):
    mu = jnp.mean(x, axis=-1, keepdims=True)
    xc = x - mu
    var = jnp.mean(xc * xc, axis=-1, keepdims=True)
    return xc * lax.rsqrt(var + EPS) * g + b


def _sigmoid(x):
    return 1.0 / (1.0 + jnp.exp(-x))


def _pack_bf16_pairs(h):
    bits = lax.bitcast_convert_type(h, jnp.uint32)
    half = h.shape[1] // 2
    lo = lax.shift_right_logical(bits[:, :half], jnp.uint32(16))
    hi = bits[:, half:] & jnp.uint32(0xFFFF0000)
    return hi | lo


def _unpack_bf16_pairs_f32(p):
    lo = lax.bitcast_convert_type(lax.shift_left(p, jnp.uint32(16)), F32)
    hi = lax.bitcast_convert_type(p & jnp.uint32(0xFFFF0000), F32)
    return lo, hi


def _unpack_bf16_pairs(p):
    lo, hi = _unpack_bf16_pairs_f32(p)
    return lo.astype(BF16), hi.astype(BF16)


def _memkv_kernel(mem_ref, g_ref, wk_ref, wv_ref, k_ref, v_ref, kbf_ref, vbf_ref):
    m = _rms(mem_ref[...], g_ref[...]).astype(BF16)
    k = _dot(m, wk_ref[...].astype(BF16))
    v = _dot(m, wv_ref[...].astype(BF16))
    k_ref[...] = k
    v_ref[...] = v
    kbf_ref[...] = k.astype(BF16)
    vbf_ref[...] = v.astype(BF16)


def _memkv(mem, g_mem, w_mk, w_mv):
    return pl.pallas_call(
        _memkv_kernel,
        out_shape=(jax.ShapeDtypeStruct((N_MEM, D_MODEL), F32), jax.ShapeDtypeStruct((N_MEM, D_MODEL), F32),
                   jax.ShapeDtypeStruct((N_MEM, D_MODEL), BF16), jax.ShapeDtypeStruct((N_MEM, D_MODEL), BF16)),
        compiler_params=pltpu.CompilerParams(vmem_limit_bytes=VMEM_LIMIT),
        name="memkv",
    )(mem, g_mem, w_mk, w_mv)


def _attn_heads(q, k, v, k_transposed):
    outs = []
    for h in range(X_HEADS):
        sl = slice(h * X_HEAD_DIM, (h + 1) * X_HEAD_DIM)
        if k_transposed:
            s = _dot(q[:, sl], k[sl, :])
        else:
            s = lax.dot_general(q[:, sl], k[:, sl], (((1,), (1,)), ((), ())), preferred_element_type=F32)
        s = s * (X_HEAD_DIM ** -0.5)
        s = s - jnp.max(s, axis=-1, keepdims=True)
        p = jnp.exp(s)
        p = p / jnp.sum(p, axis=-1, keepdims=True)
        outs.append(_dot(p.astype(BF16), v[:, sl]).astype(BF16))
    return jnp.concatenate(outs, axis=1)


def _route(logits, run, strict_lower):
    m = logits.shape[0]
    r = strict_lower.shape[0]
    lane = lax.broadcasted_iota(jnp.int32, (m, LOGIT_LANES), 1).astype(F32)
    neg = jnp.float32(-jnp.inf)
    big = jnp.float32(LOGIT_LANES)

    def first_argmax(vals):
        mx = jnp.max(vals, axis=-1, keepdims=True)
        idx = jnp.min(jnp.where(vals == mx, lane, big), axis=-1, keepdims=True)
        return mx, idx

    lg = jnp.where(lane < N_GROUPS, logits, neg)
    g_max, g_idx = first_argmax(lg)
    g_w = 1.0 / jnp.sum(jnp.exp(lg - g_max), axis=-1, keepdims=True)

    lo = N_GROUPS + g_idx * EXPERTS_PER_GROUP
    le = jnp.where((lane >= lo) & (lane < lo + EXPERTS_PER_GROUP), logits, neg)
    v1, i1 = first_argmax(le)
    v2, i2 = first_argmax(jnp.where(lane == i1, neg, le))
    t = jnp.exp(v2 - v1)
    gate1 = g_w / (1.0 + t)
    gate2 = g_w * t / (1.0 + t)
    e1 = i1 - N_GROUPS
    e2 = i2 - N_GROUPS

    oh1 = (lane == e1).astype(F32)
    oh2 = (lane == e2).astype(F32)
    oh = oh1 + oh2
    befores = []
    for r0 in range(0, m, r):
        oh_r = oh[r0:r0 + r, :]
        befores.append(_dot(strict_lower, oh_r.astype(BF16)) + run)
        run = run + jnp.sum(oh_r, axis=0, keepdims=True)
    before = befores[0] if len(befores) == 1 else jnp.concatenate(befores, axis=0)
    rank1 = jnp.sum(before * oh1, axis=-1, keepdims=True)
    rank2 = jnp.sum(before * oh2, axis=-1, keepdims=True)
    new_run = run

    rinfo = jnp.where(lane == 0, e1,
            jnp.where(lane == 1, e2,
            jnp.where(lane == 2, gate1,
            jnp.where(lane == 3, gate2,
            jnp.where(lane == 4, rank1,
            jnp.where(lane == 5, rank2, 0.0))))))
    return jnp.transpose(rinfo)[0:SUBLANES, :], new_run


def _conv_segments(a, w_ref, seg_ref, tail_ref, yseg_ref, conv_ref):
    sub = lax.broadcasted_iota(jnp.int32, (SUBLANES, LANES), 0)
    for lt in range(D_CONV // LANES):
        ls = slice(lt * LANES, (lt + 1) * LANES)
        for t0 in range(0, TM, SUBLANES):
            s, m = divmod(t0, SEG)
            seg_ref[lt, pl.ds((SEG_HALO + m) * SUBLANES + s, SUBLANES, stride=SUBLANES), :] = a[t0:t0 + SUBLANES, ls]
        for j in range(SEG_HALO):
            cur = seg_ref[lt, (SEG + j) * SUBLANES:(SEG + j + 1) * SUBLANES, :]
            prev = tail_ref[lt, j * SUBLANES:(j + 1) * SUBLANES, :]
            seg_ref[lt, j * SUBLANES:(j + 1) * SUBLANES, :] = jnp.where(
                sub == 0, pltpu.roll(prev, 1, axis=0), pltpu.roll(cur, 1, axis=0))
            tail_ref[lt, j * SUBLANES:(j + 1) * SUBLANES, :] = cur
        for m0 in range(0, SEG, CONV_BLOCK):
            acc = [jnp.zeros((SUBLANES, LANES), F32) for _ in range(CONV_BLOCK)]
            for idx in range(m0 - HIST, m0 + CONV_BLOCK):
                b = seg_ref[lt, (SEG_HALO + idx) * SUBLANES:(SEG_HALO + idx + 1) * SUBLANES, :]
                for m in range(max(m0, idx), min(m0 + CONV_BLOCK, idx + CONV_WIDTH)):
                    k = idx - m + HIST
                    acc[m - m0] = acc[m - m0] + b * w_ref[k:k + 1, ls]
            for m in range(m0, m0 + CONV_BLOCK):
                yseg_ref[lt, m * SUBLANES:(m + 1) * SUBLANES, :] = acc[m - m0]
        for t0 in range(0, TM, SUBLANES):
            s, m = divmod(t0, SEG)
            conv_ref[t0:t0 + SUBLANES, ls] = yseg_ref[lt, pl.ds(m * SUBLANES + s, SUBLANES, stride=SUBLANES), :]


def _cast_rows(src_ref, dst_ref):
    rows = src_ref.shape[0]

    def body(c, carry):
        r0 = pl.multiple_of(c * CAST_ROWS, CAST_ROWS)
        dst_ref[pl.ds(r0, CAST_ROWS), :] = src_ref[pl.ds(r0, CAST_ROWS), :].astype(BF16)
        return carry

    lax.fori_loop(0, rows // CAST_ROWS, body, 0)


def _trunk_prompt_kernel(x_ref, gmix_ref, win32_ref, convw_ref, convb_ref, lncg_ref, lncb_ref, lnvg_ref, lnvb_ref,
                         wsg_ref, bsg_ref, wout32_ref, gx_ref, wxq32_ref, kmem_ref, v_ref, wxo32_ref, gffn_ref, wr_ref,
                         br_ref,
                         x2_ref, h3_ref, logit_ref, hist_ref,
                         seg_ref, tail_ref, yseg_ref, conv_ref, win_ref, wout_ref, wxq_ref, wxo_ref):
    i = pl.program_id(0)

    @pl.when(i == 0)
    def _():
        tail_ref[...] = jnp.zeros(tail_ref.shape, F32)
        _cast_rows(win32_ref, win_ref)
        _cast_rows(wout32_ref, wout_ref)
        _cast_rows(wxq32_ref, wxq_ref)
        _cast_rows(wxo32_ref, wxo_ref)

    x = x_ref[...]
    h = _rms(x, gmix_ref[...]).astype(BF16)

    a_in = _dot(h, win_ref[:, 0:D_CONV])
    a_gate = _dot(h, win_ref[:, D_CONV:2 * D_CONV])
    a = a_in * _sigmoid(a_gate)
    hist_ref[...] = a[TM - HALO:, :]
    _conv_segments(a, convw_ref, seg_ref, tail_ref, yseg_ref, conv_ref)

    y = _ln(conv_ref[...] + convb_ref[...], lncg_ref[...], lncb_ref[...])
    a_out = (y * _sigmoid(y)).astype(BF16)

    u = _dot(h, win_ref[:, 2 * D_CONV:2 * D_CONV + D_SG])
    v = _ln(_dot(h, win_ref[:, 2 * D_CONV + D_SG:]), lnvg_ref[...], lnvb_ref[...]).astype(BF16)
    ri = lax.broadcasted_iota(jnp.int32, (SG_CHUNK, SG_CHUNK), 0)
    ci = lax.broadcasted_iota(jnp.int32, (SG_CHUNK, SG_CHUNK), 1)
    w_tril = [jnp.where(ci <= ri, wsg_ref[hh], 0.0).astype(BF16) for hh in range(SG_HEADS)]
    gate_rows = []
    for c in range(TM // SG_CHUNK):
        rs = slice(c * SG_CHUNK, (c + 1) * SG_CHUNK)
        heads = [_dot(w_tril[hh], v[rs, hh * SG_HEAD_DIM:(hh + 1) * SG_HEAD_DIM]) for hh in range(SG_HEADS)]
        gate_rows.append(jnp.concatenate(heads, axis=1) + bsg_ref[...])
    b_out = (u * jnp.concatenate(gate_rows, axis=0)).astype(BF16)

    x1 = x + _dot(a_out, wout_ref[0:D_CONV, :]) + _dot(b_out, wout_ref[D_CONV:, :])

    hx = _rms(x1, gx_ref[...]).astype(BF16)
    q = _dot(hx, wxq_ref[...]).astype(BF16)
    x2 = x1 + _dot(_attn_heads(q, kmem_ref[...], v_ref[...], False), wxo_ref[...])
    x2_ref[...] = x2

    h3 = _rms(x2, gffn_ref[...]).astype(BF16)
    h3_ref[...] = _pack_bf16_pairs(h3.astype(F32))
    logit_ref[...] = _dot(h3, wr_ref[...]) + br_ref[...]


def _router_kernel(logit_ref, upper_ref, ones_ref, rt_ref, cnt_ref, run_ref):
    @pl.when(pl.program_id(0) == 0)
    def _():
        run_ref[...] = jnp.zeros(run_ref.shape, F32)

    n = ROUTE_ROWS
    lt = jnp.transpose(logit_ref[...])
    neg = jnp.float32(-jnp.inf)
    big = jnp.float32(LOGIT_LANES)

    def first_argmax(vals, rows):
        mx = jnp.max(vals, axis=0, keepdims=True)
        idx = jnp.min(jnp.where(vals == mx, rows, big), axis=0, keepdims=True)
        return mx, idx

    row8 = lax.broadcasted_iota(jnp.int32, (SUBLANES, n), 0).astype(F32)
    lg = jnp.where(row8 < N_GROUPS, lt[0:SUBLANES, :], neg)
    g_max, g_idx = first_argmax(lg, row8)
    g_w = 1.0 / jnp.sum(jnp.exp(lg - g_max), axis=0, keepdims=True)

    n_rows = N_GROUPS + N_EXPERTS + (-(N_GROUPS + N_EXPERTS)) % SUBLANES
    rows = lax.broadcasted_iota(jnp.int32, (n_rows, n), 0).astype(F32)
    lo = N_GROUPS + g_idx * EXPERTS_PER_GROUP
    le = jnp.where((rows >= lo) & (rows < lo + EXPERTS_PER_GROUP), lt[0:n_rows, :], neg)
    v1, i1 = first_argmax(le, rows)
    v2, i2 = first_argmax(jnp.where(rows == i1, neg, le), rows)
    t = jnp.exp(v2 - v1)
    gate1 = g_w / (1.0 + t)
    gate2 = g_w * t / (1.0 + t)
    e1 = i1 - N_GROUPS
    e2 = i2 - N_GROUPS

    erow = lax.broadcasted_iota(jnp.int32, (LOGIT_LANES, n), 0).astype(F32)
    oh1 = (erow == e1).astype(F32)
    oh2 = (erow == e2).astype(F32)
    oh = (oh1 + oh2).astype(BF16)
    r = upper_ref.shape[0]
    run = run_ref[...]
    rank1, rank2 = [], []
    for c0 in range(0, n, r):
        cs = slice(c0, c0 + r)
        before = _dot(oh[:, cs], upper_ref[...]) + run
        rank1.append(jnp.sum(before * oh1[:, cs], axis=0, keepdims=True))
        rank2.append(jnp.sum(before * oh2[:, cs], axis=0, keepdims=True))
        run = run + _dot(oh[:, cs], ones_ref[...])
    run_ref[...] = run
    cnt_ref[...] = run[:, 0:LANES]

    sub = lax.broadcasted_iota(jnp.int32, (SUBLANES, n), 0)
    vals = (e1, e2, gate1, gate2, jnp.concatenate(rank1, axis=1), jnp.concatenate(rank2, axis=1))
    rt = jnp.zeros((SUBLANES, n), F32)
    for k, v in enumerate(vals):
        rt = jnp.where(sub == k, v, rt)
    rt_ref[...] = rt


def _router(logits, rank_block):
    n = logits.shape[0]
    assert n % ROUTE_ROWS == 0 and ROUTE_ROWS % rank_block == 0
    upper = jnp.triu(jnp.ones((rank_block, rank_block), BF16), 1)
    ones = jnp.ones((rank_block, rank_block), BF16)
    return pl.pallas_call(
        _router_kernel,
        grid=(n // ROUTE_ROWS,),
        in_specs=[pl.BlockSpec((ROUTE_ROWS, LOGIT_LANES), lambda i: (i, 0)),
                  pl.BlockSpec(upper.shape, lambda i: (0, 0)),
                  pl.BlockSpec(ones.shape, lambda i: (0, 0))],
        out_specs=(pl.BlockSpec((SUBLANES, ROUTE_ROWS), lambda i: (0, i)),
                   pl.BlockSpec((LOGIT_LANES, LANES), lambda i: (0, 0))),
        out_shape=(jax.ShapeDtypeStruct((SUBLANES, n), F32), jax.ShapeDtypeStruct((LOGIT_LANES, LANES), F32)),
        scratch_shapes=[pltpu.VMEM((LOGIT_LANES, rank_block), F32)],
        compiler_params=pltpu.CompilerParams(dimension_semantics=("arbitrary",), vmem_limit_bytes=VMEM_LIMIT),
        name="router",
    )(logits, upper, ones)


def _const_spec(shape):
    nd = len(shape)
    return pl.BlockSpec(shape, lambda i: (0,) * nd, pipeline_mode=pl.Buffered(1))


def _trunk_prompt(x, p):
    n = x.shape[0]
    assert n % TM == 0
    row = lambda w: pl.BlockSpec((TM, w), lambda i: (i, 0))
    consts = [p["g_mix"], p["w_in"], p["conv_w"], p["conv_b"], p["ln_conv_g"], p["ln_conv_b"], p["ln_v_g"],
              p["ln_v_b"], p["w_sg"], p["b_sg_rows"], p["w_out"], p["g_xattn"], p["w_xq"], p["k"], p["v"],
              p["w_xo"], p["g_ffn"], p["w_router"], p["b_router"]]
    return pl.pallas_call(
        _trunk_prompt_kernel,
        grid=(n // TM,),
        in_specs=[row(D_MODEL)] + [_const_spec(c.shape) for c in consts],
        out_specs=(row(D_MODEL), row(D_MODEL // 2), row(LOGIT_LANES),
                   pl.BlockSpec((HALO, D_CONV), lambda i: (0, 0))),
        out_shape=(jax.ShapeDtypeStruct((n, D_MODEL), F32),
                   jax.ShapeDtypeStruct((n, D_MODEL // 2), jnp.uint32),
                   jax.ShapeDtypeStruct((n, LOGIT_LANES), F32),
                   jax.ShapeDtypeStruct((HALO, D_CONV), F32)),
        scratch_shapes=[pltpu.VMEM((D_CONV // LANES, (SEG_HALO + SEG) * SUBLANES, LANES), F32),
                        pltpu.VMEM((D_CONV // LANES, SEG_HALO * SUBLANES, LANES), F32),
                        pltpu.VMEM((D_CONV // LANES, TM, LANES), F32),
                        pltpu.VMEM((TM, D_CONV), F32),
                        pltpu.VMEM(p["w_in"].shape, BF16), pltpu.VMEM(p["w_out"].shape, BF16),
                        pltpu.VMEM(p["w_xq"].shape, BF16), pltpu.VMEM(p["w_xo"].shape, BF16)],
        compiler_params=pltpu.CompilerParams(dimension_semantics=("arbitrary",), vmem_limit_bytes=VMEM_LIMIT),
        name="trunk_prompt",
    )(x, *consts)


def _trunk_sample_kernel(n_batch, t_len,
                         x_ref, hist_in_ref, run_in_ref, gmix_ref, win_ref, convw_ref, convb_ref, lncg_ref, lncb_ref,
                         lnvg_ref, lnvb_ref, wsgbd_ref, bsg_ref, wout_ref, gx_ref, wxq_ref, kmem_ref, v_ref, wxo_ref,
                         gffn_ref, wr_ref, br_ref, lower_ref,
                         x2_ref, h3_ref, rt_ref, hist_ref, sgv_ref, cnt_ref,
                         ext_ref, conv_ref, att_ref):
    x = x_ref[...]
    h = _rms(x, gmix_ref[...]).astype(BF16)
    z = _dot(h, win_ref[...].astype(BF16))
    a = z[:, 0:D_CONV] * _sigmoid(z[:, D_CONV:2 * D_CONV])
    ext_len = HIST + t_len
    for b in range(n_batch):
        ext_ref[b, 0:HIST, :] = hist_in_ref[b]
        ext_ref[b, HIST:ext_len, :] = a[b * t_len:(b + 1) * t_len, :]
    for b in range(n_batch):
        acc = jnp.zeros((t_len, D_CONV), F32)
        for k in range(CONV_WIDTH):
            acc = acc + ext_ref[b, k:k + t_len, :] * convw_ref[k:k + 1, :]
        conv_ref[b * t_len:(b + 1) * t_len, :] = acc
        hist_ref[b] = ext_ref[b, ext_len - HIST:ext_len, :]

    y = _ln(conv_ref[...] + convb_ref[...], lncg_ref[...], lncb_ref[...])
    a_out = (y * _sigmoid(y)).astype(BF16)

    u = z[:, 2 * D_CONV:2 * D_CONV + D_SG]
    v = _ln(z[:, 2 * D_CONV + D_SG:], lnvg_ref[...], lnvb_ref[...])
    sgv_ref[...] = v
    vb = v.astype(BF16)
    heads = [_dot(wsgbd_ref[hh], vb[:, hh * SG_HEAD_DIM:(hh + 1) * SG_HEAD_DIM]) for hh in range(SG_HEADS)]
    b_out = (u * (jnp.concatenate(heads, axis=1) + bsg_ref[...])).astype(BF16)

    x1 = (x + _dot(a_out, wout_ref[0:D_CONV, :].astype(BF16))
          + _dot(b_out, wout_ref[D_CONV:, :].astype(BF16)))

    hx = _rms(x1, gx_ref[...]).astype(BF16)
    q = _dot(hx, wxq_ref[...].astype(BF16)).astype(BF16)
    for b in range(n_batch):
        rs = slice(b * t_len, (b + 1) * t_len)
        att_ref[rs, :] = _attn_heads(q[rs, :], kmem_ref[b], v_ref[b], True)
    x2 = x1 + _dot(att_ref[...], wxo_ref[...].astype(BF16))
    x2_ref[...] = x2

    h3 = _rms(x2, gffn_ref[...]).astype(BF16)
    m = n_batch * t_len
    h3_ref[0:m, :] = _pack_bf16_pairs(h3.astype(F32))
    if h3_ref.shape[0] > m:
        h3_ref[m:, :] = jnp.zeros((h3_ref.shape[0] - m, D_MODEL // 2), jnp.uint32)
    rt, new_run = _route(_dot(h3, wr_ref[...]) + br_ref[...], run_in_ref[...], lower_ref[...])
    rt_ref[...] = rt
    cnt_ref[...] = new_run


def _trunk_sample(x, hist, run, p, n_batch, t_len):
    m = n_batch * t_len
    args = [x, hist, run, p["g_mix"], p["w_in"], p["conv_w"], p["conv_b"], p["ln_conv_g"], p["ln_conv_b"],
            p["ln_v_g"], p["ln_v_b"], p["w_sg_bd"], p["b_sg_rows_s"], p["w_out"], p["g_xattn"], p["w_xq"],
            p["k_s"], p["v_s"], p["w_xo"], p["g_ffn"], p["w_router"], p["b_router"], p["lower"]]
    return pl.pallas_call(
        functools.partial(_trunk_sample_kernel, n_batch, t_len),
        out_shape=(jax.ShapeDtypeStruct((m, D_MODEL), F32),
                   jax.ShapeDtypeStruct((-(-m // (SC_WORKERS * SUBLANES)) * SC_WORKERS * SUBLANES, D_MODEL // 2),
                                        jnp.uint32),
                   jax.ShapeDtypeStruct((SUBLANES, m), F32),
                   jax.ShapeDtypeStruct((n_batch, HIST, D_CONV), F32),
                   jax.ShapeDtypeStruct((m, D_SG), F32),
                   jax.ShapeDtypeStruct((1, LOGIT_LANES), F32)),
        scratch_shapes=[pltpu.VMEM((n_batch, HIST + t_len, D_CONV), F32),
                        pltpu.VMEM((m, D_CONV), F32),
                        pltpu.VMEM((m, D_MODEL), BF16)],
        compiler_params=pltpu.CompilerParams(vmem_limit_bytes=VMEM_LIMIT),
        name="trunk_sample",
    )(*args)


def _sc_worker_id():
    return lax.axis_index("s") * SC_CORES + lax.axis_index("c")


def _sc_chunk(per_w, max_chunk):
    assert per_w % SUBLANES == 0 and max_chunk <= LANES
    return max(c for c in range(SUBLANES, max_chunk + 1, SUBLANES) if per_w % c == 0)


def _sc_scatter_rows2(tables, slots_a, slots_b, tag_bases, n_rows_out, max_chunk):
    d, dtype = tables[0].shape[1], tables[0].dtype
    plans = []
    for t in tables:
        per_w = t.shape[0] // SC_WORKERS
        assert per_w * SC_WORKERS == t.shape[0]
        chunk = _sc_chunk(per_w, max_chunk)
        plans.append((per_w, chunk, per_w // chunk))
    cmax = max(c for _, c, _ in plans)
    n_t = len(tables)
    mesh = plsc.VectorSubcoreMesh(core_axis_name="c", subcore_axis_name="s")

    nb = SCATTER_BUFS
    lag = nb // 3
    scratch = []
    for _, chunk, _ in plans:
        for _ in range(nb):
            scratch += [pltpu.VMEM((chunk,), jnp.int32), pltpu.VMEM((chunk,), jnp.int32)]
    scratch += [pltpu.VMEM((cmax, d), dtype)] * nb
    scratch += [pltpu.VMEM((cmax, TAG_WORDS), jnp.int32)] * (2 * nb)
    scratch += [pltpu.SemaphoreType.DMA] * (2 * nb)

    @functools.partial(pl.kernel, mesh=mesh,
                       out_type=(jax.ShapeDtypeStruct((n_rows_out, d), dtype),
                                 jax.ShapeDtypeStruct((n_rows_out, TAG_WORDS), jnp.int32)),
                       scratch_types=scratch)
    def scatter(*refs):
        tab_hbm = refs[0:n_t]
        sa_hbm = refs[n_t:2 * n_t]
        sb_hbm = refs[2 * n_t:3 * n_t]
        out_hbm, tag_hbm = refs[3 * n_t], refs[3 * n_t + 1]
        sc = refs[3 * n_t + 2:]
        idx_refs = sc[:2 * nb * n_t]
        rows = sc[2 * nb * n_t:2 * nb * n_t + nb]
        tagbufs = sc[2 * nb * n_t + nb:2 * nb * n_t + 3 * nb]
        lsem = sc[2 * nb * n_t + 3 * nb:2 * nb * n_t + 4 * nb]
        ssem = sc[2 * nb * n_t + 4 * nb:]
        wid = _sc_worker_id()

        work = []
        for t, (per_w, chunk, n_chunks) in enumerate(plans):
            for j in range(n_chunks):
                work.append((t, wid * per_w + j * chunk, chunk))

        def parts(k):
            t, off, chunk = work[k]
            b = k % nb
            ia, ib = idx_refs[2 * nb * t + 2 * b], idx_refs[2 * nb * t + 2 * b + 1]
            full = chunk == cmax
            rv = rows[b] if full else rows[b].at[pl.ds(0, chunk)]
            ta = tagbufs[2 * b] if full else tagbufs[2 * b].at[pl.ds(0, chunk)]
            tb = tagbufs[2 * b + 1] if full else tagbufs[2 * b + 1].at[pl.ds(0, chunk)]
            return t, off, chunk, b, ia, ib, rv, ta, tb

        def start_load(k):
            t, off, chunk, b, ia, ib, rv, ta, tb = parts(k)
            return (pltpu.async_copy(tab_hbm[t].at[pl.ds(off, chunk)], rv, lsem[b]),
                    pltpu.async_copy(sa_hbm[t].at[pl.ds(off, chunk)], ia, lsem[b]),
                    pltpu.async_copy(sb_hbm[t].at[pl.ds(off, chunk)], ib, lsem[b]))

        def start_scatter(k):
            t, off, chunk, b, ia, ib, rv, ta, tb = parts(k)
            base_a, base_b = tag_bases[t]
            for r in range(chunk):
                row_id = (off + r).astype(jnp.int32)
                tagbufs[2 * b][r, pl.ds(0, SC_LANES)] = jnp.zeros((SC_LANES,), jnp.int32) + (base_a + row_id)
                tagbufs[2 * b + 1][r, pl.ds(0, SC_LANES)] = jnp.zeros((SC_LANES,), jnp.int32) + (base_b + row_id)
            return (pltpu.async_copy(rv, out_hbm.at[ia], ssem[b]), pltpu.async_copy(rv, out_hbm.at[ib], ssem[b]),
                    pltpu.async_copy(ta, tag_hbm.at[ia], ssem[b]), pltpu.async_copy(tb, tag_hbm.at[ib], ssem[b]))

        loads, scatters = {}, {}
        for k in range(len(work) + lag):
            if k < len(work):
                if k >= nb:
                    for c in scatters.pop(k - nb):
                        c.wait()
                loads[k] = start_load(k)
            w = k - lag
            if w >= 0:
                for c in loads.pop(w):
                    c.wait()
                scatters[w] = start_scatter(w)
        for w in sorted(scatters):
            for c in scatters[w]:
                c.wait()

    return scatter(*tables, *slots_a, *slots_b)


def _sc_scatter_back(ys, dest, n_used, n_rows_out):
    n_rows, d = ys.shape
    per_w = n_rows // SC_WORKERS
    assert per_w * SC_WORKERS == n_rows
    chunk = _sc_chunk(per_w, BACK_CHUNK)
    n_chunks = per_w // chunk
    nb = BACK_BUFS
    lag = nb // 2
    mesh = plsc.VectorSubcoreMesh(core_axis_name="c", subcore_axis_name="s")

    @functools.partial(
        pl.kernel, mesh=mesh,
        out_type=jax.ShapeDtypeStruct((n_rows_out, d), ys.dtype),
        scratch_types=([pltpu.VMEM((chunk,), jnp.int32)] * nb + [pltpu.VMEM((chunk, d), ys.dtype)] * nb
                       + [pltpu.VMEM((SC_LANES,), jnp.int32)] + [pltpu.SemaphoreType.DMA] * (2 * nb)),
        compiler_params=pltpu.CompilerParams(needs_layout_passes=False),
    )
    def scatter_back(ys_hbm, dest_hbm, used_hbm, out_hbm, *rest):
        idx = rest[:nb]
        rows = rest[nb:2 * nb]
        used_s = rest[2 * nb]
        lsem = rest[2 * nb + 1:3 * nb + 1]
        ssem = rest[3 * nb + 1:]
        base = _sc_worker_id() * per_w
        pltpu.sync_copy(used_hbm, used_s)
        n_used_rows = jnp.max(used_s[...])

        def load(k):
            b, off = k % nb, base + k * chunk
            return (pltpu.make_async_copy(ys_hbm.at[pl.ds(off, chunk)], rows[b], lsem[b]),
                    pltpu.make_async_copy(dest_hbm.at[pl.ds(off, chunk)], idx[b], lsem[b]))

        def scatter(k):
            b = k % nb
            return pltpu.make_async_copy(rows[b], out_hbm.at[idx[b]], ssem[b])

        def when_used(k, fn):
            @pl.when(base + k * chunk < n_used_rows)
            def _():
                fn()

        for k in range(n_chunks + lag):
            if k < n_chunks:
                if k >= nb:
                    when_used(k - nb, lambda k=k: scatter(k - nb).wait())
                when_used(k, lambda k=k: [c.start() for c in load(k)])
            w = k - lag
            if w >= 0:
                def consume(w=w):
                    for c in load(w):
                        c.wait()
                    scatter(w).start()
                when_used(w, consume)
        for w in range(max(n_chunks - nb, 0), n_chunks):
            when_used(w, lambda w=w: scatter(w).wait())

    return scatter_back(ys, dest, n_used)


def _experts_kernel(dump_base, first_ref, nblk_ref, cnt_ref, tot_ref, xs_hbm, tag_hbm, wg_ref, wu_ref, wd_ref,
                    ys_hbm, dest_hbm, xbuf, tbuf, ybuf, dbuf, wg_bf, wu_bf, wd_bf, in_sem, tin_sem, out_sem, dout_sem):
    e = pl.program_id(0)
    nb = nblk_ref[e]
    first = first_ref[e]
    cnt = cnt_ref[e]
    total = tot_ref[0]
    half = D_MODEL // 2

    def in_copies(gb):
        slot = lax.rem(gb, X_BUFS)
        return (pltpu.make_async_copy(xs_hbm.at[pl.ds(gb * BM, BM)], xbuf.at[slot], in_sem.at[slot]),
                pltpu.make_async_copy(tag_hbm.at[pl.ds(gb * BM, BM)], tbuf.at[slot], tin_sem.at[slot]))

    def out_copies(gb):
        slot = lax.rem(gb, Y_BUFS)
        return (pltpu.make_async_copy(ybuf.at[slot], ys_hbm.at[pl.ds(gb * BM, BM)], out_sem.at[slot]),
                pltpu.make_async_copy(dbuf.at[slot], dest_hbm.at[pl.ds(gb * SUBLANES, SUBLANES)], dout_sem.at[slot]))

    def start_in(gb):
        for c in in_copies(gb):
            c.start(priority=ROW_DMA_PRIORITY)

    def start_out(gb):
        for c in out_copies(gb):
            c.start(priority=ROW_DMA_PRIORITY)

    def wait_out(gb):
        for c in out_copies(gb):
            c.wait()

    @pl.when(nb > 0)
    def _():
        @pl.when(first == 0)
        def _():
            for k in range(X_LOOKAHEAD):
                @pl.when(k < total)
                def _():
                    start_in(k)

        wg_bf[...] = wg_ref[0].astype(BF16)
        wu_bf[...] = wu_ref[0].astype(BF16)
        wd_bf[...] = wd_ref[0].astype(BF16)

        def acquire(gb):
            @pl.when(gb + X_LOOKAHEAD < total)
            def _():
                start_in(gb + X_LOOKAHEAD)

            for c in in_copies(gb):
                c.wait()

            @pl.when(gb >= Y_BUFS)
            def _():
                wait_out(gb - Y_BUFS)

        def ffn(gb, j):
            n_live = cnt - j * BM
            tags_t = jnp.transpose(tbuf[lax.rem(gb, X_BUFS)].astype(F32))
            lane = lax.broadcasted_iota(jnp.int32, (SUBLANES, BM), 1)
            own = (dump_base + gb * BM + lane).astype(F32)
            dest = jnp.where(lane < n_live, jnp.broadcast_to(tags_t[0:1, :], (SUBLANES, BM)), own)
            dbuf[lax.rem(gb, Y_BUFS)] = dest.astype(jnp.int32)
            live = lax.broadcasted_iota(jnp.int32, (BM, half), 0) < n_live
            lo, hi = _unpack_bf16_pairs(jnp.where(live, xbuf[lax.rem(gb, X_BUFS)], jnp.uint32(0)))
            g = _dot(lo, wg_bf[0:half, :]) + _dot(hi, wg_bf[half:, :])
            u = _dot(lo, wu_bf[0:half, :]) + _dot(hi, wu_bf[half:, :])
            hm = (g * _sigmoid(g) * u).astype(BF16)
            y = _dot(hm, wd_bf[...])
            ybuf[lax.rem(gb, Y_BUFS)] = _pack_bf16_pairs(y.astype(BF16).astype(F32))

        def block_pair(jp, carry):
            j0 = 2 * jp
            g0 = first + j0
            acquire(g0)
            acquire(g0 + 1)
            ffn(g0, j0)
            ffn(g0 + 1, j0 + 1)
            start_out(g0)
            start_out(g0 + 1)
            return carry

        lax.fori_loop(0, nb // 2, block_pair, 0)

        @pl.when(lax.rem(nb, 2) == 1)
        def _():
            gl = first + nb - 1
            acquire(gl)
            ffn(gl, nb - 1)
            start_out(gl)

        @pl.when(first + nb == total)
        def _():
            for k in range(Y_BUFS):
                @pl.when(total - 1 - k >= 0)
                def _():
                    wait_out(total - 1 - k)


def _experts(xs, tags, n_rows_out, dump_base, first_block, n_blocks_e, counts, w_eg, w_eu, w_ed):
    w_map = lambda e, fb, nb, ct, tot: (e, 0, 0)
    half = D_MODEL // 2
    total = jnp.sum(n_blocks_e).astype(jnp.int32).reshape(1)
    n_blocks = n_rows_out // BM
    return pl.pallas_call(
        functools.partial(_experts_kernel, dump_base),
        grid_spec=pltpu.PrefetchScalarGridSpec(
            num_scalar_prefetch=4,
            grid=(N_EXPERTS,),
            in_specs=[pl.BlockSpec(memory_space=pl.ANY),
                      pl.BlockSpec(memory_space=pl.ANY),
                      pl.BlockSpec((1, D_MODEL, D_EXPERT), w_map),
                      pl.BlockSpec((1, D_MODEL, D_EXPERT), w_map),
                      pl.BlockSpec((1, D_EXPERT, D_MODEL), w_map)],
            out_specs=(pl.BlockSpec(memory_space=pl.ANY), pl.BlockSpec(memory_space=pl.ANY)),
            scratch_shapes=[pltpu.VMEM((X_BUFS, BM, half), jnp.uint32), pltpu.VMEM((X_BUFS, BM, TAG_WORDS), jnp.int32),
                            pltpu.VMEM((Y_BUFS, BM, half), jnp.uint32), pltpu.VMEM((Y_BUFS, SUBLANES, BM), jnp.int32),
                            pltpu.VMEM((D_MODEL, D_EXPERT), BF16), pltpu.VMEM((D_MODEL, D_EXPERT), BF16),
                            pltpu.VMEM((D_EXPERT, D_MODEL), BF16),
                            pltpu.SemaphoreType.DMA((X_BUFS,)), pltpu.SemaphoreType.DMA((X_BUFS,)),
                            pltpu.SemaphoreType.DMA((Y_BUFS,)), pltpu.SemaphoreType.DMA((Y_BUFS,))]),
        out_shape=(jax.ShapeDtypeStruct((n_rows_out, half), jnp.uint32),
                   jax.ShapeDtypeStruct((n_blocks * SUBLANES, BM), jnp.int32)),
        compiler_params=pltpu.CompilerParams(dimension_semantics=("arbitrary",), vmem_limit_bytes=VMEM_LIMIT),
        name="experts",
    )(first_block, n_blocks_e, counts, total, xs, tags, w_eg, w_eu, w_ed)


def _combine_kernel(x2_ref, y1_ref, y2_ref, rt_ref, g_ref, o_ref):
    rt = rt_ref[...]
    r = jnp.transpose(jnp.concatenate([rt, jnp.zeros((LANES - rt.shape[0], rt.shape[1]), F32)], axis=0))
    g1, g2 = r[:, 2:3], r[:, 3:4]
    half = D_MODEL // 2
    y1_lo, y1_hi = _unpack_bf16_pairs_f32(y1_ref[...])
    y2_lo, y2_hi = _unpack_bf16_pairs_f32(y2_ref[...])
    x_lo = x2_ref[:, 0:half] + g1 * y1_lo + g2 * y2_lo
    x_hi = x2_ref[:, half:] + g1 * y1_hi + g2 * y2_hi
    ms = (jnp.sum(x_lo * x_lo, axis=-1, keepdims=True) + jnp.sum(x_hi * x_hi, axis=-1, keepdims=True)) / D_MODEL
    inv = lax.rsqrt(ms + EPS)
    o_ref[:, 0:half] = x_lo * inv * g_ref[:, 0:half]
    o_ref[:, half:] = x_hi * inv * g_ref[:, half:]


def _combine(x2, yg, rt, g_final, tm, blk1, blk2):
    n = x2.shape[0]
    return pl.pallas_call(
        _combine_kernel,
        grid=(n // tm,),
        in_specs=[pl.BlockSpec((tm, D_MODEL), lambda i: (i, 0)),
                  pl.BlockSpec((tm, D_MODEL // 2), lambda i: (blk1 + i, 0)),
                  pl.BlockSpec((tm, D_MODEL // 2), lambda i: (blk2 + i, 0)),
                  pl.BlockSpec((SUBLANES, tm), lambda i: (0, i)),
                  pl.BlockSpec((1, D_MODEL), lambda i: (0, 0))],
        out_specs=pl.BlockSpec((tm, D_MODEL), lambda i: (i, 0)),
        out_shape=jax.ShapeDtypeStruct((n, D_MODEL), F32),
        compiler_params=pltpu.CompilerParams(dimension_semantics=("arbitrary",), vmem_limit_bytes=VMEM_LIMIT),
        name="combine",
    )(x2, yg, yg, rt, g_final)


def _scatter_back(ys, dest, n_used, n_rows_out):
    return _sc_scatter_back(ys, dest, n_used, n_rows_out)


def _scatter_rows2(tables, slots_a, slots_b, tag_bases, n_rows_out):
    return _sc_scatter_rows2(tables, slots_a, slots_b, tag_bases, n_rows_out, SCATTER_CHUNK)


def kernel(x_prompt, x_sample, mem_prompt, state_conv, cache_mem_k, cache_mem_v, g_mix, w_in, conv_w, conv_b, ln_conv_g, ln_conv_b, ln_v_g, ln_v_b, w_sg, b_sg, w_out, g_mem, w_mk, w_mv, g_xattn, w_xq, w_xo, g_ffn, w_router_group, b_router_group, w_router_expert, b_router_expert, w_expert_gate, w_expert_up, w_expert_down, g_final):
    assert x_prompt.shape[0] == 1 and g_mix.shape[0] == 1
    n_p = x_prompt.shape[1]
    n_batch, t_len = x_sample.shape[0], x_sample.shape[1]
    n_s = n_batch * t_len
    row = lambda a: a.reshape(1, -1)

    w_router = jnp.concatenate(
        [w_router_group[0], jnp.transpose(w_router_expert[0], (1, 0, 2)).reshape(D_MODEL, N_EXPERTS)], axis=1)
    w_router = jnp.pad(w_router, ((0, 0), (0, LOGIT_LANES - w_router.shape[1]))).astype(BF16)
    b_router = jnp.pad(jnp.concatenate([b_router_group[0], b_router_expert[0].reshape(-1)]),
                       (0, LOGIT_LANES - N_GROUPS - N_EXPERTS)).reshape(1, LOGIT_LANES)
    tril_t = jnp.tril(jnp.ones((t_len, t_len), bool))
    w_sg_t = jnp.where(tril_t, w_sg[0][:, :t_len, :t_len], 0.0)
    eye_b = jnp.eye(n_batch, dtype=F32)
    w_sg_bd = jnp.einsum("ab,hij->haibj", eye_b, w_sg_t).reshape(SG_HEADS, n_s, n_s).astype(BF16)
    p = {
        "g_mix": row(g_mix[0]), "w_in": w_in[0],
        "conv_w": jnp.pad(conv_w[0], ((0, 1), (0, 0))), "conv_b": row(conv_b[0]),
        "ln_conv_g": row(ln_conv_g[0]), "ln_conv_b": row(ln_conv_b[0]),
        "ln_v_g": row(ln_v_g[0]), "ln_v_b": row(ln_v_b[0]),
        "w_sg": w_sg[0],
        "b_sg_rows": jnp.repeat(b_sg[0].T, SG_HEAD_DIM, axis=1),
        "w_sg_bd": w_sg_bd,
        "b_sg_rows_s": jnp.tile(jnp.repeat(b_sg[0][:, :t_len].T, SG_HEAD_DIM, axis=1), (n_batch, 1)),
        "w_out": w_out[0], "g_xattn": row(g_xattn[0]),
        "w_xq": w_xq[0], "w_xo": w_xo[0], "g_ffn": row(g_ffn[0]),
        "w_router": w_router, "b_router": b_router,
        "lower": jnp.tril(jnp.ones((n_s, n_s), BF16), -1),
    }

    k_p, v_p, p["k"], p["v"] = _memkv(mem_prompt[0], row(g_mem[0]), w_mk[0], w_mv[0])
    p["k_s"] = jnp.transpose(cache_mem_k[0].astype(BF16), (0, 2, 3, 1)).reshape(n_batch, D_MODEL, N_MEM)
    p["v_s"] = cache_mem_v[0].astype(BF16).reshape(n_batch, N_MEM, D_MODEL)

    assert n_p % n_s == 0
    x2_p, h3_p, logits_p, hist_p = _trunk_prompt(x_prompt[0], p)
    rt_p, cnt_t = _router(logits_p, TM)
    cnt_p = cnt_t[:, 0].reshape(1, LOGIT_LANES)
    x2_s, h3_s, rt_s, hist_s, sgv_s, cnt = _trunk_sample(
        x_sample.reshape(n_s, D_MODEL), state_conv[0], cnt_p, p, n_batch, t_len)

    experts = jnp.arange(N_EXPERTS, dtype=jnp.int32)
    w_e = (w_expert_gate[0], w_expert_up[0], w_expert_down[0])

    def moe_pass(cnt, h3_tables, rts, n_real):
        n_tot = sum(n_real)
        n_slots = -(-(n_tot * 2) // BM) * BM + N_EXPERTS * BM
        counts = cnt[0, :N_EXPERTS].astype(jnp.int32)
        padded = (counts + BM - 1) // BM * BM
        pad_start = jnp.cumsum(padded) - padded

        def one(e_row, rank_row):
            e = e_row.astype(jnp.int32)
            start = jnp.sum(jnp.where(e[None, :] == experts[:, None], pad_start[:, None], 0), axis=0)
            return start + rank_row.astype(jnp.int32)

        slots = [(one(rt[0], rt[4]), one(rt[1], rt[5])) for rt in rts]
        sa, sb, tag_bases, spare0, dest0 = [], [], [], n_slots, 0
        for tab, (a, b), n in zip(h3_tables, slots, n_real):
            n_spare = tab.shape[0] - n
            spare = spare0 + jnp.arange(n_spare, dtype=jnp.int32)
            sa.append(jnp.concatenate([a, spare]))
            sb.append(jnp.concatenate([b, spare + n_spare]))
            spare0 += 2 * n_spare
            tag_bases.append((dest0, dest0 + n))
            dest0 += 2 * n
        xs, tags = _scatter_rows2(tuple(h3_tables), tuple(sa), tuple(sb), tuple(tag_bases), spare0)
        ys, dest_blocks = _experts(xs, tags, n_slots, dest0, pad_start // BM, padded // BM, counts, *w_e)
        slot = jnp.arange(n_slots, dtype=jnp.int32)
        dest = dest_blocks.reshape(n_slots // BM, SUBLANES, BM)[:, 0, :].reshape(-1)
        n_used = jnp.sum(padded)
        dest = jnp.where(slot < n_used, dest, dest0 + slot)
        return _scatter_back(ys, dest, jnp.full((SC_LANES,), n_used, jnp.int32), dest0 + n_slots)

    yg = moe_pass(cnt, [h3_p, h3_s], [rt_p, rt_s], [n_p, n_s])

    gf = row(g_final)
    y_p = _combine(x2_p, yg, rt_p, gf, TM_COMBINE, 0, n_p // TM_COMBINE)
    y_s = _combine(x2_s, yg, rt_s, gf, n_s, 2 * n_p // n_s, 2 * n_p // n_s + 1)

    return (y_p.reshape(1, n_p, D_MODEL),
            y_s.reshape(n_batch, t_len, D_MODEL),
            hist_p[HALO - HIST:].reshape(1, 1, HIST, D_CONV),
            hist_s.reshape(1, n_batch, HIST, D_CONV),
            k_p.reshape(1, 1, N_MEM, X_HEADS, X_HEAD_DIM),
            v_p.reshape(1, 1, N_MEM, X_HEADS, X_HEAD_DIM),
            sgv_s.reshape(1, n_batch, t_len, D_SG))
```

```python
import functools

import jax
import jax.numpy as jnp
from jax import lax
from jax.experimental import pallas as pl
from jax.experimental.pallas import tpu as pltpu
from jax.experimental.pallas import tpu_sc as plsc

D_MODEL = 1024
D_CONV = 512
D_SG = 512
CONV_WIDTH = 31
HIST = CONV_WIDTH - 1
SG_HEADS = 4
SG_HEAD_DIM = 128
SG_CHUNK = 128
N_MEM = 256
X_HEADS = 4
X_HEAD_DIM = 256
N_GROUPS = 4
EXPERTS_PER_GROUP = 8
N_EXPERTS = 32
D_EXPERT = 512
EPS = 1e-6

LANES = 128
SUBLANES = 8
SC_CORES = 2
SC_SUBCORES = 16
SC_WORKERS = SC_CORES * SC_SUBCORES
SC_LANES = 16
VMEM_LIMIT = 56 * 1024 * 1024

TM = 512
TM_COMBINE = 2048
ROUTE_ROWS = 4096
HALO = 32
SEG = TM // SUBLANES
SEG_HALO = 32
CONV_BLOCK = 16
CAST_ROWS = 64
BM = 256
X_LOOKAHEAD = 4
X_BUFS = X_LOOKAHEAD + 2
Y_BUFS = 4
ROW_DMA_PRIORITY = 1
SCATTER_CHUNK = 32
SCATTER_BUFS = 4
TAG_WORDS = 128
BACK_CHUNK = 56
BACK_BUFS = 4
LOGIT_LANES = 128

F32 = jnp.float32
BF16 = jnp.bfloat16


def _dot(a, b):
    return jnp.dot(a, b, preferred_element_type=F32)


def _rms(x, g):
    return x * lax.rsqrt(jnp.mean(x * x, axis=-1, keepdims=True) + EPS) * g


def _ln(x, g, b):
    mu = jnp.mean(x, axis=-1, keepdims=True)
    xc = x - mu
    var = jnp.mean(xc * xc, axis=-1, keepdims=True)
    return xc * lax.rsqrt(var + EPS) * g + b


def _sigmoid(x):
    return 1.0 / (1.0 + jnp.exp(-x))


def _pack_bf16_pairs(h):
    bits = lax.bitcast_convert_type(h, jnp.uint32)
    half = h.shape[1] // 2
    lo = lax.shift_right_logical(bits[:, :half], jnp.uint32(16))
    hi = bits[:, half:] & jnp.uint32(0xFFFF0000)
    return hi | lo


def _unpack_bf16_pairs_f32(p):
    lo = lax.bitcast_convert_type(lax.shift_left(p, jnp.uint32(16)), F32)
    hi = lax.bitcast_convert_type(p & jnp.uint32(0xFFFF0000), F32)
    return lo, hi


def _unpack_bf16_pairs(p):
    lo, hi = _unpack_bf16_pairs_f32(p)
    return lo.astype(BF16), hi.astype(BF16)


def _memkv_kernel(mem_ref, g_ref, wk_ref, wv_ref, k_ref, v_ref, kbf_ref, vbf_ref):
    m = _rms(mem_ref[...], g_ref[...]).astype(BF16)
    k = _dot(m, wk_ref[...].astype(BF16))
    v = _dot(m, wv_ref[...].astype(BF16))
    k_ref[...] = k
    v_ref[...] = v
    kbf_ref[...] = k.astype(BF16)
    vbf_ref[...] = v.astype(BF16)


def _memkv(mem, g_mem, w_mk, w_mv):
    return pl.pallas_call(
        _memkv_kernel,
        out_shape=(jax.ShapeDtypeStruct((N_MEM, D_MODEL), F32), jax.ShapeDtypeStruct((N_MEM, D_MODEL), F32),
                   jax.ShapeDtypeStruct((N_MEM, D_MODEL), BF16), jax.ShapeDtypeStruct((N_MEM, D_MODEL), BF16)),
        compiler_params=pltpu.CompilerParams(vmem_limit_bytes=VMEM_LIMIT),
        name="memkv",
    )(mem, g_mem, w_mk, w_mv)


def _attn_heads(q, k, v, k_transposed):
    outs = []
    for h in range(X_HEADS):
        sl = slice(h * X_HEAD_DIM, (h + 1) * X_HEAD_DIM)
        if k_transposed:
            s = _dot(q[:, sl], k[sl, :])
        else:
            s = lax.dot_general(q[:, sl], k[:, sl], (((1,), (1,)), ((), ())), preferred_element_type=F32)
        s = s * (X_HEAD_DIM ** -0.5)
        s = s - jnp.max(s, axis=-1, keepdims=True)
        p = jnp.exp(s)
        p = p / jnp.sum(p, axis=-1, keepdims=True)
        outs.append(_dot(p.astype(BF16), v[:, sl]).astype(BF16))
    return jnp.concatenate(outs, axis=1)


def _route(logits, run, strict_lower):
    m = logits.shape[0]
    r = strict_lower.shape[0]
    lane = lax.broadcasted_iota(jnp.int32, (m, LOGIT_LANES), 1).astype(F32)
    neg = jnp.float32(-jnp.inf)
    big = jnp.float32(LOGIT_LANES)

    def first_argmax(vals):
        mx = jnp.max(vals, axis=-1, keepdims=True)
        idx = jnp.min(jnp.where(vals == mx, lane, big), axis=-1, keepdims=True)
        return mx, idx

    lg = jnp.where(lane < N_GROUPS, logits, neg)
    g_max, g_idx = first_argmax(lg)
    g_w = 1.0 / jnp.sum(jnp.exp(lg - g_max), axis=-1, keepdims=True)

    lo = N_GROUPS + g_idx * EXPERTS_PER_GROUP
    le = jnp.where((lane >= lo) & (lane < lo + EXPERTS_PER_GROUP), logits, neg)
    v1, i1 = first_argmax(le)
    v2, i2 = first_argmax(jnp.where(lane == i1, neg, le))
    t = jnp.exp(v2 - v1)
    gate1 = g_w / (1.0 + t)
    gate2 = g_w * t / (1.0 + t)
    e1 = i1 - N_GROUPS
    e2 = i2 - N_GROUPS

    oh1 = (lane == e1).astype(F32)
    oh2 = (lane == e2).astype(F32)
    oh = oh1 + oh2
    befores = []
    for r0 in range(0, m, r):
        oh_r = oh[r0:r0 + r, :]
        befores.append(_dot(strict_lower, oh_r.astype(BF16)) + run)
        run = run + jnp.sum(oh_r, axis=0, keepdims=True)
    before = befores[0] if len(befores) == 1 else jnp.concatenate(befores, axis=0)
    rank1 = jnp.sum(before * oh1, axis=-1, keepdims=True)
    rank2 = jnp.sum(before * oh2, axis=-1, keepdims=True)
    new_run = run

    rinfo = jnp.where(lane == 0, e1,
            jnp.where(lane == 1, e2,
            jnp.where(lane == 2, gate1,
            jnp.where(lane == 3, gate2,
            jnp.where(lane == 4, rank1,
            jnp.where(lane == 5, rank2, 0.0))))))
    return jnp.transpose(rinfo)[0:SUBLANES, :], new_run


def _conv_segments(a, w_ref, seg_ref, tail_ref, yseg_ref, conv_ref):
    sub = lax.broadcasted_iota(jnp.int32, (SUBLANES, LANES), 0)
    for lt in range(D_CONV // LANES):
        ls = slice(lt * LANES, (lt + 1) * LANES)
        for t0 in range(0, TM, SUBLANES):
            s, m = divmod(t0, SEG)
            seg_ref[lt, pl.ds((SEG_HALO + m) * SUBLANES + s, SUBLANES, stride=SUBLANES), :] = a[t0:t0 + SUBLANES, ls]
        for j in range(SEG_HALO):
            cur = seg_ref[lt, (SEG + j) * SUBLANES:(SEG + j + 1) * SUBLANES, :]
            prev = tail_ref[lt, j * SUBLANES:(j + 1) * SUBLANES, :]
            seg_ref[lt, j * SUBLANES:(j + 1) * SUBLANES, :] = jnp.where(
                sub == 0, pltpu.roll(prev, 1, axis=0), pltpu.roll(cur, 1, axis=0))
            tail_ref[lt, j * SUBLANES:(j + 1) * SUBLANES, :] = cur
        for m0 in range(0, SEG, CONV_BLOCK):
            acc = [jnp.zeros((SUBLANES, LANES), F32) for _ in range(CONV_BLOCK)]
            for idx in range(m0 - HIST, m0 + CONV_BLOCK):
                b = seg_ref[lt, (SEG_HALO + idx) * SUBLANES:(SEG_HALO + idx + 1) * SUBLANES, :]
                for m in range(max(m0, idx), min(m0 + CONV_BLOCK, idx + CONV_WIDTH)):
                    k = idx - m + HIST
                    acc[m - m0] = acc[m - m0] + b * w_ref[k:k + 1, ls]
            for m in range(m0, m0 + CONV_BLOCK):
                yseg_ref[lt, m * SUBLANES:(m + 1) * SUBLANES, :] = acc[m - m0]
        for t0 in range(0, TM, SUBLANES):
            s, m = divmod(t0, SEG)
            conv_ref[t0:t0 + SUBLANES, ls] = yseg_ref[lt, pl.ds(m * SUBLANES + s, SUBLANES, stride=SUBLANES), :]


def _cast_rows(src_ref, dst_ref):
    rows = src_ref.shape[0]

    def body(c, carry):
        r0 = pl.multiple_of(c * CAST_ROWS, CAST_ROWS)
        dst_ref[pl.ds(r0, CAST_ROWS), :] = src_ref[pl.ds(r0, CAST_ROWS), :].astype(BF16)
        return carry

    lax.fori_loop(0, rows // CAST_ROWS, body, 0)


def _trunk_prompt_kernel(x_ref, gmix_ref, win32_ref, convw_ref, convb_ref, lncg_ref, lncb_ref, lnvg_ref, lnvb_ref,
                         wsg_ref, bsg_ref, wout32_ref, gx_ref, wxq32_ref, kmem_ref, v_ref, wxo32_ref, gffn_ref, wr_ref,
                         br_ref,
                         x2_ref, h3_ref, logit_ref, hist_ref,
                         seg_ref, tail_ref, yseg_ref, conv_ref, win_ref, wout_ref, wxq_ref, wxo_ref):
    i = pl.program_id(0)

    @pl.when(i == 0)
    def _():
        tail_ref[...] = jnp.zeros(tail_ref.shape, F32)
        _cast_rows(win32_ref, win_ref)
        _cast_rows(wout32_ref, wout_ref)
        _cast_rows(wxq32_ref, wxq_ref)
        _cast_rows(wxo32_ref, wxo_ref)

    x = x_ref[...]
    h = _rms(x, gmix_ref[...]).astype(BF16)

    a_in = _dot(h, win_ref[:, 0:D_CONV])
    a_gate = _dot(h, win_ref[:, D_CONV:2 * D_CONV])
    a = a_in * _sigmoid(a_gate)
    hist_ref[...] = a[TM - HALO:, :]
    _conv_segments(a, convw_ref, seg_ref, tail_ref, yseg_ref, conv_ref)

    y = _ln(conv_ref[...] + convb_ref[...], lncg_ref[...], lncb_ref[...])
    a_out = (y * _sigmoid(y)).astype(BF16)

    u = _dot(h, win_ref[:, 2 * D_CONV:2 * D_CONV + D_SG])
    v = _ln(_dot(h, win_ref[:, 2 * D_CONV + D_SG:]), lnvg_ref[...], lnvb_ref[...]).astype(BF16)
    ri = lax.broadcasted_iota(jnp.int32, (SG_CHUNK, SG_CHUNK), 0)
    ci = lax.broadcasted_iota(jnp.int32, (SG_CHUNK, SG_CHUNK), 1)
    w_tril = [jnp.where(ci <= ri, wsg_ref[hh], 0.0).astype(BF16) for hh in range(SG_HEADS)]
    gate_rows = []
    for c in range(TM // SG_CHUNK):
        rs = slice(c * SG_CHUNK, (c + 1) * SG_CHUNK)
        heads = [_dot(w_tril[hh], v[rs, hh * SG_HEAD_DIM:(hh + 1) * SG_HEAD_DIM]) for hh in range(SG_HEADS)]
        gate_rows.append(jnp.concatenate(heads, axis=1) + bsg_ref[...])
    b_out = (u * jnp.concatenate(gate_rows, axis=0)).astype(BF16)

    x1 = x + _dot(a_out, wout_ref[0:D_CONV, :]) + _dot(b_out, wout_ref[D_CONV:, :])

    hx = _rms(x1, gx_ref[...]).astype(BF16)
    q = _dot(hx, wxq_ref[...]).astype(BF16)
    x2 = x1 + _dot(_attn_heads(q, kmem_ref[...], v_ref[...], False), wxo_ref[...])
    x2_ref[...] = x2

    h3 = _rms(x2, gffn_ref[...]).astype(BF16)
    h3_ref[...] = _pack_bf16_pairs(h3.astype(F32))
    logit_ref[...] = _dot(h3, wr_ref[...]) + br_ref[...]


def _router_kernel(logit_ref, upper_ref, ones_ref, rt_ref, cnt_ref, run_ref):
    @pl.when(pl.program_id(0) == 0)
    def _():
        run_ref[...] = jnp.zeros(run_ref.shape, F32)

    n = ROUTE_ROWS
    lt = jnp.transpose(logit_ref[...])
    neg = jnp.float32(-jnp.inf)
    big = jnp.float32(LOGIT_LANES)

    def first_argmax(vals, rows):
        mx = jnp.max(vals, axis=0, keepdims=True)
        idx = jnp.min(jnp.where(vals == mx, rows, big), axis=0, keepdims=True)
        return mx, idx

    row8 = lax.broadcasted_iota(jnp.int32, (SUBLANES, n), 0).astype(F32)
    lg = jnp.where(row8 < N_GROUPS, lt[0:SUBLANES, :], neg)
    g_max, g_idx = first_argmax(lg, row8)
    g_w = 1.0 / jnp.sum(jnp.exp(lg - g_max), axis=0, keepdims=True)

    n_rows = N_GROUPS + N_EXPERTS + (-(N_GROUPS + N_EXPERTS)) % SUBLANES
    rows = lax.broadcasted_iota(jnp.int32, (n_rows, n), 0).astype(F32)
    lo = N_GROUPS + g_idx * EXPERTS_PER_GROUP
    le = jnp.where((rows >= lo) & (rows < lo + EXPERTS_PER_GROUP), lt[0:n_rows, :], neg)
    v1, i1 = first_argmax(le, rows)
    v2, i2 = first_argmax(jnp.where(rows == i1, neg, le), rows)
    t = jnp.exp(v2 - v1)
    gate1 = g_w / (1.0 + t)
    gate2 = g_w * t / (1.0 + t)
    e1 = i1 - N_GROUPS
    e2 = i2 - N_GROUPS

    erow = lax.broadcasted_iota(jnp.int32, (LOGIT_LANES, n), 0).astype(F32)
    oh1 = (erow == e1).astype(F32)
    oh2 = (erow == e2).astype(F32)
    oh = (oh1 + oh2).astype(BF16)
    r = upper_ref.shape[0]
    run = run_ref[...]
    rank1, rank2 = [], []
    for c0 in range(0, n, r):
        cs = slice(c0, c0 + r)
        before = _dot(oh[:, cs], upper_ref[...]) + run
        rank1.append(jnp.sum(before * oh1[:, cs], axis=0, keepdims=True))
        rank2.append(jnp.sum(before * oh2[:, cs], axis=0, keepdims=True))
        run = run + _dot(oh[:, cs], ones_ref[...])
    run_ref[...] = run
    cnt_ref[...] = run[:, 0:LANES]

    sub = lax.broadcasted_iota(jnp.int32, (SUBLANES, n), 0)
    vals = (e1, e2, gate1, gate2, jnp.concatenate(rank1, axis=1), jnp.concatenate(rank2, axis=1))
    rt = jnp.zeros((SUBLANES, n), F32)
    for k, v in enumerate(vals):
        rt = jnp.where(sub == k, v, rt)
    rt_ref[...] = rt


def _router(logits, rank_block):
    n = logits.shape[0]
    assert n % ROUTE_ROWS == 0 and ROUTE_ROWS % rank_block == 0
    upper = jnp.triu(jnp.ones((rank_block, rank_block), BF16), 1)
    ones = jnp.ones((rank_block, rank_block), BF16)
    return pl.pallas_call(
        _router_kernel,
        grid=(n // ROUTE_ROWS,),
        in_specs=[pl.BlockSpec((ROUTE_ROWS, LOGIT_LANES), lambda i: (i, 0)),
                  pl.BlockSpec(upper.shape, lambda i: (0, 0)),
                  pl.BlockSpec(ones.shape, lambda i: (0, 0))],
        out_specs=(pl.BlockSpec((SUBLANES, ROUTE_ROWS), lambda i: (0, i)),
                   pl.BlockSpec((LOGIT_LANES, LANES), lambda i: (0, 0))),
        out_shape=(jax.ShapeDtypeStruct((SUBLANES, n), F32), jax.ShapeDtypeStruct((LOGIT_LANES, LANES), F32)),
        scratch_shapes=[pltpu.VMEM((LOGIT_LANES, rank_block), F32)],
        compiler_params=pltpu.CompilerParams(dimension_semantics=("arbitrary",), vmem_limit_bytes=VMEM_LIMIT),
        name="router",
    )(logits, upper, ones)


def _const_spec(shape):
    nd = len(shape)
    return pl.BlockSpec(shape, lambda i: (0,) * nd, pipeline_mode=pl.Buffered(1))


def _trunk_prompt(x, p):
    n = x.shape[0]
    assert n % TM == 0
    row = lambda w: pl.BlockSpec((TM, w), lambda i: (i, 0))
    consts = [p["g_mix"], p["w_in"], p["conv_w"], p["conv_b"], p["ln_conv_g"], p["ln_conv_b"], p["ln_v_g"],
              p["ln_v_b"], p["w_sg"], p["b_sg_rows"], p["w_out"], p["g_xattn"], p["w_xq"], p["k"], p["v"],
              p["w_xo"], p["g_ffn"], p["w_router"], p["b_router"]]
    return pl.pallas_call(
        _trunk_prompt_kernel,
        grid=(n // TM,),
        in_specs=[row(D_MODEL)] + [_const_spec(c.shape) for c in consts],
        out_specs=(row(D_MODEL), row(D_MODEL // 2), row(LOGIT_LANES),
                   pl.BlockSpec((HALO, D_CONV), lambda i: (0, 0))),
        out_shape=(jax.ShapeDtypeStruct((n, D_MODEL), F32),
                   jax.ShapeDtypeStruct((n, D_MODEL // 2), jnp.uint32),
                   jax.ShapeDtypeStruct((n, LOGIT_LANES), F32),
                   jax.ShapeDtypeStruct((HALO, D_CONV), F32)),
        scratch_shapes=[pltpu.VMEM((D_CONV // LANES, (SEG_HALO + SEG) * SUBLANES, LANES), F32),
                        pltpu.VMEM((D_CONV // LANES, SEG_HALO * SUBLANES, LANES), F32),
                        pltpu.VMEM((D_CONV // LANES, TM, LANES), F32),
                        pltpu.VMEM((TM, D_CONV), F32),
                        pltpu.VMEM(p["w_in"].shape, BF16), pltpu.VMEM(p["w_out"].shape, BF16),
                        pltpu.VMEM(p["w_xq"].shape, BF16), pltpu.VMEM(p["w_xo"].shape, BF16)],
        compiler_params=pltpu.CompilerParams(dimension_semantics=("arbitrary",), vmem_limit_bytes=VMEM_LIMIT),
        name="trunk_prompt",
    )(x, *consts)


def _trunk_sample_kernel(n_batch, t_len,
                         x_ref, hist_in_ref, run_in_ref, gmix_ref, win_ref, convw_ref, convb_ref, lncg_ref, lncb_ref,
                         lnvg_ref, lnvb_ref, wsgbd_ref, bsg_ref, wout_ref, gx_ref, wxq_ref, kmem_ref, v_ref, wxo_ref,
                         gffn_ref, wr_ref, br_ref, lower_ref,
                         x2_ref, h3_ref, rt_ref, hist_ref, sgv_ref, cnt_ref,
                         ext_ref, conv_ref, att_ref):
    x = x_ref[...]
    h = _rms(x, gmix_ref[...]).astype(BF16)
    z = _dot(h, win_ref[...].astype(BF16))
    a = z[:, 0:D_CONV] * _sigmoid(z[:, D_CONV:2 * D_CONV])
    ext_len = HIST + t_len
    for b in range(n_batch):
        ext_ref[b, 0:HIST, :] = hist_in_ref[b]
        ext_ref[b, HIST:ext_len, :] = a[b * t_len:(b + 1) * t_len, :]
    for b in range(n_batch):
        acc = jnp.zeros((t_len, D_CONV), F32)
        for k in range(CONV_WIDTH):
            acc = acc + ext_ref[b, k:k + t_len, :] * convw_ref[k:k + 1, :]
        conv_ref[b * t_len:(b + 1) * t_len, :] = acc
        hist_ref[b] = ext_ref[b, ext_len - HIST:ext_len, :]

    y = _ln(conv_ref[...] + convb_ref[...], lncg_ref[...], lncb_ref[...])
    a_out = (y * _sigmoid(y)).astype(BF16)

    u = z[:, 2 * D_CONV:2 * D_CONV + D_SG]
    v = _ln(z[:, 2 * D_CONV + D_SG:], lnvg_ref[...], lnvb_ref[...])
    sgv_ref[...] = v
    vb = v.astype(BF16)
    heads = [_dot(wsgbd_ref[hh], vb[:, hh * SG_HEAD_DIM:(hh + 1) * SG_HEAD_DIM]) for hh in range(SG_HEADS)]
    b_out = (u * (jnp.concatenate(heads, axis=1) + bsg_ref[...])).astype(BF16)

    x1 = (x + _dot(a_out, wout_ref[0:D_CONV, :].astype(BF16))
          + _dot(b_out, wout_ref[D_CONV:, :].astype(BF16)))

    hx = _rms(x1, gx_ref[...]).astype(BF16)
    q = _dot(hx, wxq_ref[...].astype(BF16)).astype(BF16)
    for b in range(n_batch):
        rs = slice(b * t_len, (b + 1) * t_len)
        att_ref[rs, :] = _attn_heads(q[rs, :], kmem_ref[b], v_ref[b], True)
    x2 = x1 + _dot(att_ref[...], wxo_ref[...].astype(BF16))
    x2_ref[...] = x2

    h3 = _rms(x2, gffn_ref[...]).astype(BF16)
    m = n_batch * t_len
    h3_ref[0:m, :] = _pack_bf16_pairs(h3.astype(F32))
    if h3_ref.shape[0] > m:
        h3_ref[m:, :] = jnp.zeros((h3_ref.shape[0] - m, D_MODEL // 2), jnp.uint32)
    rt, new_run = _route(_dot(h3, wr_ref[...]) + br_ref[...], run_in_ref[...], lower_ref[...])
    rt_ref[...] = rt
    cnt_ref[...] = new_run


def _trunk_sample(x, hist, run, p, n_batch, t_len):
    m = n_batch * t_len
    args = [x, hist, run, p["g_mix"], p["w_in"], p["conv_w"], p["conv_b"], p["ln_conv_g"], p["ln_conv_b"],
            p["ln_v_g"], p["ln_v_b"], p["w_sg_bd"], p["b_sg_rows_s"], p["w_out"], p["g_xattn"], p["w_xq"],
            p["k_s"], p["v_s"], p["w_xo"], p["g_ffn"], p["w_router"], p["b_router"], p["lower"]]
    return pl.pallas_call(
        functools.partial(_trunk_sample_kernel, n_batch, t_len),
        out_shape=(jax.ShapeDtypeStruct((m, D_MODEL), F32),
                   jax.ShapeDtypeStruct((-(-m // (SC_WORKERS * SUBLANES)) * SC_WORKERS * SUBLANES, D_MODEL // 2),
                                        jnp.uint32),
                   jax.ShapeDtypeStruct((SUBLANES, m), F32),
                   jax.ShapeDtypeStruct((n_batch, HIST, D_CONV), F32),
                   jax.ShapeDtypeStruct((m, D_SG), F32),
                   jax.ShapeDtypeStruct((1, LOGIT_LANES), F32)),
        scratch_shapes=[pltpu.VMEM((n_batch, HIST + t_len, D_CONV), F32),
                        pltpu.VMEM((m, D_CONV), F32),
                        pltpu.VMEM((m, D_MODEL), BF16)],
        compiler_params=pltpu.CompilerParams(vmem_limit_bytes=VMEM_LIMIT),
        name="trunk_sample",
    )(*args)


def _sc_worker_id():
    return lax.axis_index("s") * SC_CORES + lax.axis_index("c")


def _sc_chunk(per_w, max_chunk):
    assert per_w % SUBLANES == 0 and max_chunk <= LANES
    return max(c for c in range(SUBLANES, max_chunk + 1, SUBLANES) if per_w % c == 0)


def _sc_scatter_rows2(tables, slots_a, slots_b, tag_bases, n_rows_out, max_chunk):
    d, dtype = tables[0].shape[1], tables[0].dtype
    plans = []
    for t in tables:
        per_w = t.shape[0] // SC_WORKERS
        assert per_w * SC_WORKERS == t.shape[0]
        chunk = _sc_chunk(per_w, max_chunk)
        plans.append((per_w, chunk, per_w // chunk))
    cmax = max(c for _, c, _ in plans)
    n_t = len(tables)
    mesh = plsc.VectorSubcoreMesh(core_axis_name="c", subcore_axis_name="s")

    nb = SCATTER_BUFS
    lag = nb // 3
    scratch = []
    for _, chunk, _ in plans:
        for _ in range(nb):
            scratch += [pltpu.VMEM((chunk,), jnp.int32), pltpu.VMEM((chunk,), jnp.int32)]
    scratch += [pltpu.VMEM((cmax, d), dtype)] * nb
    scratch += [pltpu.VMEM((cmax, TAG_WORDS), jnp.int32)] * (2 * nb)
    scratch += [pltpu.SemaphoreType.DMA] * (2 * nb)

    @functools.partial(pl.kernel, mesh=mesh,
                       out_type=(jax.ShapeDtypeStruct((n_rows_out, d), dtype),
                                 jax.ShapeDtypeStruct((n_rows_out, TAG_WORDS), jnp.int32)),
                       scratch_types=scratch)
    def scatter(*refs):
        tab_hbm = refs[0:n_t]
        sa_hbm = refs[n_t:2 * n_t]
        sb_hbm = refs[2 * n_t:3 * n_t]
        out_hbm, tag_hbm = refs[3 * n_t], refs[3 * n_t + 1]
        sc = refs[3 * n_t + 2:]
        idx_refs = sc[:2 * nb * n_t]
        rows = sc[2 * nb * n_t:2 * nb * n_t + nb]
        tagbufs = sc[2 * nb * n_t + nb:2 * nb * n_t + 3 * nb]
        lsem = sc[2 * nb * n_t + 3 * nb:2 * nb * n_t + 4 * nb]
        ssem = sc[2 * nb * n_t + 4 * nb:]
        wid = _sc_worker_id()

        work = []
        for t, (per_w, chunk, n_chunks) in enumerate(plans):
            for j in range(n_chunks):
                work.append((t, wid * per_w + j * chunk, chunk))

        def parts(k):
            t, off, chunk = work[k]
            b = k % nb
            ia, ib = idx_refs[2 * nb * t + 2 * b], idx_refs[2 * nb * t + 2 * b + 1]
            full = chunk == cmax
            rv = rows[b] if full else rows[b].at[pl.ds(0, chunk)]
            ta = tagbufs[2 * b] if full else tagbufs[2 * b].at[pl.ds(0, chunk)]
            tb = tagbufs[2 * b + 1] if full else tagbufs[2 * b + 1].at[pl.ds(0, chunk)]
            return t, off, chunk, b, ia, ib, rv, ta, tb

        def start_load(k):
            t, off, chunk, b, ia, ib, rv, ta, tb = parts(k)
            return (pltpu.async_copy(tab_hbm[t].at[pl.ds(off, chunk)], rv, lsem[b]),
                    pltpu.async_copy(sa_hbm[t].at[pl.ds(off, chunk)], ia, lsem[b]),
                    pltpu.async_copy(sb_hbm[t].at[pl.ds(off, chunk)], ib, lsem[b]))

        def start_scatter(k):
            t, off, chunk, b, ia, ib, rv, ta, tb = parts(k)
            base_a, base_b = tag_bases[t]
            for r in range(chunk):
                row_id = (off + r).astype(jnp.int32)
                tagbufs[2 * b][r, pl.ds(0, SC_LANES)] = jnp.zeros((SC_LANES,), jnp.int32) + (base_a + row_id)
                tagbufs[2 * b + 1][r, pl.ds(0, SC_LANES)] = jnp.zeros((SC_LANES,), jnp.int32) + (base_b + row_id)
            return (pltpu.async_copy(rv, out_hbm.at[ia], ssem[b]), pltpu.async_copy(rv, out_hbm.at[ib], ssem[b]),
                    pltpu.async_copy(ta, tag_hbm.at[ia], ssem[b]), pltpu.async_copy(tb, tag_hbm.at[ib], ssem[b]))

        loads, scatters = {}, {}
        for k in range(len(work) + lag):
            if k < len(work):
                if k >= nb:
                    for c in scatters.pop(k - nb):
                        c.wait()
                loads[k] = start_load(k)
            w = k - lag
            if w >= 0:
                for c in loads.pop(w):
                    c.wait()
                scatters[w] = start_scatter(w)
        for w in sorted(scatters):
            for c in scatters[w]:
                c.wait()

    return scatter(*tables, *slots_a, *slots_b)


def _sc_scatter_back(ys, dest, n_used, n_rows_out):
    n_rows, d = ys.shape
    per_w = n_rows // SC_WORKERS
    assert per_w * SC_WORKERS == n_rows
    chunk = _sc_chunk(per_w, BACK_CHUNK)
    n_chunks = per_w // chunk
    nb = BACK_BUFS
    lag = nb // 2
    mesh = plsc.VectorSubcoreMesh(core_axis_name="c", subcore_axis_name="s")

    @functools.partial(
        pl.kernel, mesh=mesh,
        out_type=jax.ShapeDtypeStruct((n_rows_out, d), ys.dtype),
        scratch_types=([pltpu.VMEM((chunk,), jnp.int32)] * nb + [pltpu.VMEM((chunk, d), ys.dtype)] * nb
                       + [pltpu.VMEM((SC_LANES,), jnp.int32)] + [pltpu.SemaphoreType.DMA] * (2 * nb)),
        compiler_params=pltpu.CompilerParams(needs_layout_passes=False),
    )
    def scatter_back(ys_hbm, dest_hbm, used_hbm, out_hbm, *rest):
        idx = rest[:nb]
        rows = rest[nb:2 * nb]
        used_s = rest[2 * nb]
        lsem = rest[2 * nb + 1:3 * nb + 1]
        ssem = rest[3 * nb + 1:]
        wid = _sc_worker_id()
        pltpu.sync_copy(used_hbm, used_s)
        n_used_rows = jnp.max(used_s[...])

        def row0(k):
            return (k * SC_WORKERS + wid) * chunk

        def load(k):
            b, off = k % nb, row0(k)
            return (pltpu.make_async_copy(ys_hbm.at[pl.ds(off, chunk)], rows[b], lsem[b]),
                    pltpu.make_async_copy(dest_hbm.at[pl.ds(off, chunk)], idx[b], lsem[b]))

        def scatter(k):
            b = k % nb
            return pltpu.make_async_copy(rows[b], out_hbm.at[idx[b]], ssem[b])

        def when_used(k, fn):
            @pl.when(row0(k) < n_used_rows)
            def _():
                fn()

        for k in range(n_chunks + lag):
            if k < n_chunks:
                if k >= nb:
                    when_used(k - nb, lambda k=k: scatter(k - nb).wait())
                when_used(k, lambda k=k: [c.start() for c in load(k)])
            w = k - lag
            if w >= 0:
                def consume(w=w):
                    for c in load(w):
                        c.wait()
                    scatter(w).start()
                when_used(w, consume)
        for w in range(max(n_chunks - nb, 0), n_chunks):
            when_used(w, lambda w=w: scatter(w).wait())

    return scatter_back(ys, dest, n_used)


def _experts_kernel(dump_base, first_ref, nblk_ref, cnt_ref, tot_ref, xs_hbm, tag_hbm, wg_ref, wu_ref, wd_ref,
                    ys_hbm, dest_hbm, xbuf, tbuf, ybuf, dbuf, wg_bf, wu_bf, wd_bf, in_sem, tin_sem, out_sem, dout_sem):
    e = pl.program_id(0)
    nb = nblk_ref[e]
    first = first_ref[e]
    cnt = cnt_ref[e]
    total = tot_ref[0]
    half = D_MODEL // 2

    def in_copies(gb):
        slot = lax.rem(gb, X_BUFS)
        return (pltpu.make_async_copy(xs_hbm.at[pl.ds(gb * BM, BM)], xbuf.at[slot], in_sem.at[slot]),
                pltpu.make_async_copy(tag_hbm.at[pl.ds(gb * BM, BM)], tbuf.at[slot], tin_sem.at[slot]))

    def out_copies(gb):
        slot = lax.rem(gb, Y_BUFS)
        return (pltpu.make_async_copy(ybuf.at[slot], ys_hbm.at[pl.ds(gb * BM, BM)], out_sem.at[slot]),
                pltpu.make_async_copy(dbuf.at[slot], dest_hbm.at[pl.ds(gb * SUBLANES, SUBLANES)], dout_sem.at[slot]))

    def start_in(gb):
        for c in in_copies(gb):
            c.start(priority=ROW_DMA_PRIORITY)

    def start_out(gb):
        for c in out_copies(gb):
            c.start(priority=ROW_DMA_PRIORITY)

    def wait_out(gb):
        for c in out_copies(gb):
            c.wait()

    @pl.when(nb > 0)
    def _():
        @pl.when(first == 0)
        def _():
            for k in range(X_LOOKAHEAD):
                @pl.when(k < total)
                def _():
                    start_in(k)

        wg_bf[...] = wg_ref[0].astype(BF16)
        wu_bf[...] = wu_ref[0].astype(BF16)
        wd_bf[...] = wd_ref[0].astype(BF16)

        def acquire(gb):
            @pl.when(gb + X_LOOKAHEAD < total)
            def _():
                start_in(gb + X_LOOKAHEAD)

            for c in in_copies(gb):
                c.wait()

            @pl.when(gb >= Y_BUFS)
            def _():
                wait_out(gb - Y_BUFS)

        def ffn(gb, j):
            n_live = cnt - j * BM
            tags_t = jnp.transpose(tbuf[lax.rem(gb, X_BUFS)].astype(F32))
            lane = lax.broadcasted_iota(jnp.int32, (SUBLANES, BM), 1)
            own = (dump_base + gb * BM + lane).astype(F32)
            dest = jnp.where(lane < n_live, jnp.broadcast_to(tags_t[0:1, :], (SUBLANES, BM)), own)
            dbuf[lax.rem(gb, Y_BUFS)] = dest.astype(jnp.int32)
            live = lax.broadcasted_iota(jnp.int32, (BM, half), 0) < n_live
            lo, hi = _unpack_bf16_pairs(jnp.where(live, xbuf[lax.rem(gb, X_BUFS)], jnp.uint32(0)))
            g = _dot(lo, wg_bf[0:half, :]) + _dot(hi, wg_bf[half:, :])
            u = _dot(lo, wu_bf[0:half, :]) + _dot(hi, wu_bf[half:, :])
            hm = (g * _sigmoid(g) * u).astype(BF16)
            y = _dot(hm, wd_bf[...])
            ybuf[lax.rem(gb, Y_BUFS)] = _pack_bf16_pairs(y.astype(BF16).astype(F32))

        def block_pair(jp, carry):
            j0 = 2 * jp
            g0 = first + j0
            acquire(g0)
            acquire(g0 + 1)
            ffn(g0, j0)
            ffn(g0 + 1, j0 + 1)
            start_out(g0)
            start_out(g0 + 1)
            return carry

        lax.fori_loop(0, nb // 2, block_pair, 0)

        @pl.when(lax.rem(nb, 2) == 1)
        def _():
            gl = first + nb - 1
            acquire(gl)
            ffn(gl, nb - 1)
            start_out(gl)

        @pl.when(first + nb == total)
        def _():
            for k in range(Y_BUFS):
                @pl.when(total - 1 - k >= 0)
                def _():
                    wait_out(total - 1 - k)


def _experts(xs, tags, n_rows_out, dump_base, first_block, n_blocks_e, counts, w_eg, w_eu, w_ed):
    w_map = lambda e, fb, nb, ct, tot: (e, 0, 0)
    half = D_MODEL // 2
    total = jnp.sum(n_blocks_e).astype(jnp.int32).reshape(1)
    n_blocks = n_rows_out // BM
    return pl.pallas_call(
        functools.partial(_experts_kernel, dump_base),
        grid_spec=pltpu.PrefetchScalarGridSpec(
            num_scalar_prefetch=4,
            grid=(N_EXPERTS,),
            in_specs=[pl.BlockSpec(memory_space=pl.ANY),
                      pl.BlockSpec(memory_space=pl.ANY),
                      pl.BlockSpec((1, D_MODEL, D_EXPERT), w_map),
                      pl.BlockSpec((1, D_MODEL, D_EXPERT), w_map),
                      pl.BlockSpec((1, D_EXPERT, D_MODEL), w_map)],
            out_specs=(pl.BlockSpec(memory_space=pl.ANY), pl.BlockSpec(memory_space=pl.ANY)),
            scratch_shapes=[pltpu.VMEM((X_BUFS, BM, half), jnp.uint32), pltpu.VMEM((X_BUFS, BM, TAG_WORDS), jnp.int32),
                            pltpu.VMEM((Y_BUFS, BM, half), jnp.uint32), pltpu.VMEM((Y_BUFS, SUBLANES, BM), jnp.int32),
                            pltpu.VMEM((D_MODEL, D_EXPERT), BF16), pltpu.VMEM((D_MODEL, D_EXPERT), BF16),
                            pltpu.VMEM((D_EXPERT, D_MODEL), BF16),
                            pltpu.SemaphoreType.DMA((X_BUFS,)), pltpu.SemaphoreType.DMA((X_BUFS,)),
                            pltpu.SemaphoreType.DMA((Y_BUFS,)), pltpu.SemaphoreType.DMA((Y_BUFS,))]),
        out_shape=(jax.ShapeDtypeStruct((n_rows_out, half), jnp.uint32),
                   jax.ShapeDtypeStruct((n_blocks * SUBLANES, BM), jnp.int32)),
        compiler_params=pltpu.CompilerParams(dimension_semantics=("arbitrary",), vmem_limit_bytes=VMEM_LIMIT),
        name="experts",
    )(first_block, n_blocks_e, counts, total, xs, tags, w_eg, w_eu, w_ed)


def _combine_kernel(x2_ref, y1_ref, y2_ref, rt_ref, g_ref, o_ref):
    rt = rt_ref[...]
    r = jnp.transpose(jnp.concatenate([rt, jnp.zeros((LANES - rt.shape[0], rt.shape[1]), F32)], axis=0))
    g1, g2 = r[:, 2:3], r[:, 3:4]
    half = D_MODEL // 2
    y1_lo, y1_hi = _unpack_bf16_pairs_f32(y1_ref[...])
    y2_lo, y2_hi = _unpack_bf16_pairs_f32(y2_ref[...])
    x_lo = x2_ref[:, 0:half] + g1 * y1_lo + g2 * y2_lo
    x_hi = x2_ref[:, half:] + g1 * y1_hi + g2 * y2_hi
    ms = (jnp.sum(x_lo * x_lo, axis=-1, keepdims=True) + jnp.sum(x_hi * x_hi, axis=-1, keepdims=True)) / D_MODEL
    inv = lax.rsqrt(ms + EPS)
    o_ref[:, 0:half] = x_lo * inv * g_ref[:, 0:half]
    o_ref[:, half:] = x_hi * inv * g_ref[:, half:]


def _combine(x2, yg, rt, g_final, tm, blk1, blk2):
    n = x2.shape[0]
    return pl.pallas_call(
        _combine_kernel,
        grid=(n // tm,),
        in_specs=[pl.BlockSpec((tm, D_MODEL), lambda i: (i, 0)),
                  pl.BlockSpec((tm, D_MODEL // 2), lambda i: (blk1 + i, 0)),
                  pl.BlockSpec((tm, D_MODEL // 2), lambda i: (blk2 + i, 0)),
                  pl.BlockSpec((SUBLANES, tm), lambda i: (0, i)),
                  pl.BlockSpec((1, D_MODEL), lambda i: (0, 0))],
        out_specs=pl.BlockSpec((tm, D_MODEL), lambda i: (i, 0)),
        out_shape=jax.ShapeDtypeStruct((n, D_MODEL), F32),
        compiler_params=pltpu.CompilerParams(dimension_semantics=("arbitrary",), vmem_limit_bytes=VMEM_LIMIT),
        name="combine",
    )(x2, yg, yg, rt, g_final)


def _scatter_back(ys, dest, n_used, n_rows_out):
    return _sc_scatter_back(ys, dest, n_used, n_rows_out)


def _scatter_rows2(tables, slots_a, slots_b, tag_bases, n_rows_out):
    return _sc_scatter_rows2(tables, slots_a, slots_b, tag_bases, n_rows_out, SCATTER_CHUNK)


def kernel(x_prompt, x_sample, mem_prompt, state_conv, cache_mem_k, cache_mem_v, g_mix, w_in, conv_w, conv_b, ln_conv_g, ln_conv_b, ln_v_g, ln_v_b, w_sg, b_sg, w_out, g_mem, w_mk, w_mv, g_xattn, w_xq, w_xo, g_ffn, w_router_group, b_router_group, w_router_expert, b_router_expert, w_expert_gate, w_expert_up, w_expert_down, g_final):
    assert x_prompt.shape[0] == 1 and g_mix.shape[0] == 1
    n_p = x_prompt.shape[1]
    n_batch, t_len = x_sample.shape[0], x_sample.shape[1]
    n_s = n_batch * t_len
    row = lambda a: a.reshape(1, -1)

    w_router = jnp.concatenate(
        [w_router_group[0], jnp.transpose(w_router_expert[0], (1, 0, 2)).reshape(D_MODEL, N_EXPERTS)], axis=1)
    w_router = jnp.pad(w_router, ((0, 0), (0, LOGIT_LANES - w_router.shape[1]))).astype(BF16)
    b_router = jnp.pad(jnp.concatenate([b_router_group[0], b_router_expert[0].reshape(-1)]),
                       (0, LOGIT_LANES - N_GROUPS - N_EXPERTS)).reshape(1, LOGIT_LANES)
    tril_t = jnp.tril(jnp.ones((t_len, t_len), bool))
    w_sg_t = jnp.where(tril_t, w_sg[0][:, :t_len, :t_len], 0.0)
    eye_b = jnp.eye(n_batch, dtype=F32)
    w_sg_bd = jnp.einsum("ab,hij->haibj", eye_b, w_sg_t).reshape(SG_HEADS, n_s, n_s).astype(BF16)
    p = {
        "g_mix": row(g_mix[0]), "w_in": w_in[0],
        "conv_w": jnp.pad(conv_w[0], ((0, 1), (0, 0))), "conv_b": row(conv_b[0]),
        "ln_conv_g": row(ln_conv_g[0]), "ln_conv_b": row(ln_conv_b[0]),
        "ln_v_g": row(ln_v_g[0]), "ln_v_b": row(ln_v_b[0]),
        "w_sg": w_sg[0],
        "b_sg_rows": jnp.repeat(b_sg[0].T, SG_HEAD_DIM, axis=1),
        "w_sg_bd": w_sg_bd,
        "b_sg_rows_s": jnp.tile(jnp.repeat(b_sg[0][:, :t_len].T, SG_HEAD_DIM, axis=1), (n_batch, 1)),
        "w_out": w_out[0], "g_xattn": row(g_xattn[0]),
        "w_xq": w_xq[0], "w_xo": w_xo[0], "g_ffn": row(g_ffn[0]),
        "w_router": w_router, "b_router": b_router,
        "lower": jnp.tril(jnp.ones((n_s, n_s), BF16), -1),
    }

    k_p, v_p, p["k"], p["v"] = _memkv(mem_prompt[0], row(g_mem[0]), w_mk[0], w_mv[0])
    p["k_s"] = jnp.transpose(cache_mem_k[0].astype(BF16), (0, 2, 3, 1)).reshape(n_batch, D_MODEL, N_MEM)
    p["v_s"] = cache_mem_v[0].astype(BF16).reshape(n_batch, N_MEM, D_MODEL)

    assert n_p % n_s == 0
    x2_p, h3_p, logits_p, hist_p = _trunk_prompt(x_prompt[0], p)
    rt_p, cnt_t = _router(logits_p, TM)
    cnt_p = cnt_t[:, 0].reshape(1, LOGIT_LANES)
    x2_s, h3_s, rt_s, hist_s, sgv_s, cnt = _trunk_sample(
        x_sample.reshape(n_s, D_MODEL), state_conv[0], cnt_p, p, n_batch, t_len)

    experts = jnp.arange(N_EXPERTS, dtype=jnp.int32)
    w_e = (w_expert_gate[0], w_expert_up[0], w_expert_down[0])

    def moe_pass(cnt, h3_tables, rts, n_real):
        n_tot = sum(n_real)
        n_slots = -(-(n_tot * 2) // BM) * BM + N_EXPERTS * BM
        counts = cnt[0, :N_EXPERTS].astype(jnp.int32)
        padded = (counts + BM - 1) // BM * BM
        pad_start = jnp.cumsum(padded) - padded

        def one(e_row, rank_row):
            e = e_row.astype(jnp.int32)
            start = jnp.sum(jnp.where(e[None, :] == experts[:, None], pad_start[:, None], 0), axis=0)
            return start + rank_row.astype(jnp.int32)

        slots = [(one(rt[0], rt[4]), one(rt[1], rt[5])) for rt in rts]
        sa, sb, tag_bases, spare0, dest0 = [], [], [], n_slots, 0
        for tab, (a, b), n in zip(h3_tables, slots, n_real):
            n_spare = tab.shape[0] - n
            spare = spare0 + jnp.arange(n_spare, dtype=jnp.int32)
            sa.append(jnp.concatenate([a, spare]))
            sb.append(jnp.concatenate([b, spare + n_spare]))
            spare0 += 2 * n_spare
            tag_bases.append((dest0, dest0 + n))
            dest0 += 2 * n
        xs, tags = _scatter_rows2(tuple(h3_tables), tuple(sa), tuple(sb), tuple(tag_bases), spare0)
        ys, dest_blocks = _experts(xs, tags, n_slots, dest0, pad_start // BM, padded // BM, counts, *w_e)
        slot = jnp.arange(n_slots, dtype=jnp.int32)
        dest = dest_blocks.reshape(n_slots // BM, SUBLANES, BM)[:, 0, :].reshape(-1)
        n_used = jnp.sum(padded)
        dest = jnp.where(slot < n_used, dest, dest0 + slot)
        return _scatter_back(ys, dest, jnp.full((SC_LANES,), n_used, jnp.int32), dest0 + n_slots)

    yg = moe_pass(cnt, [h3_p, h3_s], [rt_p, rt_s], [n_p, n_s])

    gf = row(g_final)
    y_p = _combine(x2_p, yg, rt_p, gf, TM_COMBINE, 0, n_p // TM_COMBINE)
    y_s = _combine(x2_s, yg, rt_s, gf, n_s, 2 * n_p // n_s, 2 * n_p // n_s + 1)

    return (y_p.reshape(1, n_p, D_MODEL),
            y_s.reshape(n_batch, t_len, D_MODEL),
            hist_p[HALO - HIST:].reshape(1, 1, HIST, D_CONV),
            hist_s.reshape(1, n_batch, HIST, D_CONV),
            k_p.reshape(1, 1, N_MEM, X_HEADS, X_HEAD_DIM),
            v_p.reshape(1, 1, N_MEM, X_HEADS, X_HEAD_DIM),
            sgv_s.reshape(1, n_batch, t_len, D_SG))
```

```python
import functools

import jax
import jax.numpy as jnp
from jax import lax
from jax.experimental import pallas as pl
from jax.experimental.pallas import tpu as pltpu
from jax.experimental.pallas import tpu_sc as plsc

D_MODEL = 1024
D_CONV = 512
D_SG = 512
CONV_WIDTH = 31
HIST = CONV_WIDTH - 1
SG_HEADS = 4
SG_HEAD_DIM = 128
SG_CHUNK = 128
N_MEM = 256
X_HEADS = 4
X_HEAD_DIM = 256
N_GROUPS = 4
EXPERTS_PER_GROUP = 8
N_EXPERTS = 32
D_EXPERT = 512
EPS = 1e-6

LANES = 128
SUBLANES = 8
SC_CORES = 2
SC_SUBCORES = 16
SC_WORKERS = SC_CORES * SC_SUBCORES
SC_LANES = 16
VMEM_LIMIT = 56 * 1024 * 1024

TM = 1024
TM_COMBINE = 2048
ROUTE_ROWS = 4096
HALO = 32
SEG = TM // SUBLANES
SEG_HALO = 32
CONV_BLOCK = 16
CAST_ROWS = 64
STAGE_ROWS = 256
BM = 256
X_LOOKAHEAD = 4
X_BUFS = X_LOOKAHEAD + 2
Y_BUFS = 4
ROW_DMA_PRIORITY = 1
SCATTER_CHUNK = 32
SCATTER_BUFS = 4
TAG_WORDS = 128
BACK_CHUNK = 56
BACK_BUFS = 4
LOGIT_LANES = 128

F32 = jnp.float32
BF16 = jnp.bfloat16


def _dot(a, b):
    return jnp.dot(a, b, preferred_element_type=F32)


def _rms(x, g):
    return x * lax.rsqrt(jnp.mean(x * x, axis=-1, keepdims=True) + EPS) * g


def _ln(x, g, b):
    mu = jnp.mean(x, axis=-1, keepdims=True)
    xc = x - mu
    var = jnp.mean(xc * xc, axis=-1, keepdims=True)
    return xc * lax.rsqrt(var + EPS) * g + b


def _sigmoid(x):
    return 1.0 / (1.0 + jnp.exp(-x))


def _pack_bf16_pairs(h):
    bits = lax.bitcast_convert_type(h, jnp.uint32)
    half = h.shape[1] // 2
    lo = lax.shift_right_logical(bits[:, :half], jnp.uint32(16))
    hi = bits[:, half:] & jnp.uint32(0xFFFF0000)
    return hi | lo


def _unpack_bf16_pairs_f32(p):
    lo = lax.bitcast_convert_type(lax.shift_left(p, jnp.uint32(16)), F32)
    hi = lax.bitcast_convert_type(p & jnp.uint32(0xFFFF0000), F32)
    return lo, hi


def _unpack_bf16_pairs(p):
    lo, hi = _unpack_bf16_pairs_f32(p)
    return lo.astype(BF16), hi.astype(BF16)


def _memkv_kernel(mem_ref, g_ref, wk_ref, wv_ref, k_ref, v_ref, kbf_ref, vbf_ref):
    m = _rms(mem_ref[...], g_ref[...]).astype(BF16)
    k = _dot(m, wk_ref[...].astype(BF16))
    v = _dot(m, wv_ref[...].astype(BF16))
    k_ref[...] = k
    v_ref[...] = v
    kbf_ref[...] = k.astype(BF16)
    vbf_ref[...] = v.astype(BF16)


def _memkv(mem, g_mem, w_mk, w_mv):
    return pl.pallas_call(
        _memkv_kernel,
        out_shape=(jax.ShapeDtypeStruct((N_MEM, D_MODEL), F32), jax.ShapeDtypeStruct((N_MEM, D_MODEL), F32),
                   jax.ShapeDtypeStruct((N_MEM, D_MODEL), BF16), jax.ShapeDtypeStruct((N_MEM, D_MODEL), BF16)),
        compiler_params=pltpu.CompilerParams(vmem_limit_bytes=VMEM_LIMIT),
        name="memkv",
    )(mem, g_mem, w_mk, w_mv)


def _attn_heads(q, k, v, k_transposed):
    outs = []
    for h in range(X_HEADS):
        sl = slice(h * X_HEAD_DIM, (h + 1) * X_HEAD_DIM)
        if k_transposed:
            s = _dot(q[:, sl], k[sl, :])
        else:
            s = lax.dot_general(q[:, sl], k[:, sl], (((1,), (1,)), ((), ())), preferred_element_type=F32)
        s = s * (X_HEAD_DIM ** -0.5)
        s = s - jnp.max(s, axis=-1, keepdims=True)
        p = jnp.exp(s)
        p = p / jnp.sum(p, axis=-1, keepdims=True)
        outs.append(_dot(p.astype(BF16), v[:, sl]).astype(BF16))
    return jnp.concatenate(outs, axis=1)


def _route(logits, run, strict_lower):
    m = logits.shape[0]
    r = strict_lower.shape[0]
    lane = lax.broadcasted_iota(jnp.int32, (m, LOGIT_LANES), 1).astype(F32)
    neg = jnp.float32(-jnp.inf)
    big = jnp.float32(LOGIT_LANES)

    def first_argmax(vals):
        mx = jnp.max(vals, axis=-1, keepdims=True)
        idx = jnp.min(jnp.where(vals == mx, lane, big), axis=-1, keepdims=True)
        return mx, idx

    lg = jnp.where(lane < N_GROUPS, logits, neg)
    g_max, g_idx = first_argmax(lg)
    g_w = 1.0 / jnp.sum(jnp.exp(lg - g_max), axis=-1, keepdims=True)

    lo = N_GROUPS + g_idx * EXPERTS_PER_GROUP
    le = jnp.where((lane >= lo) & (lane < lo + EXPERTS_PER_GROUP), logits, neg)
    v1, i1 = first_argmax(le)
    v2, i2 = first_argmax(jnp.where(lane == i1, neg, le))
    t = jnp.exp(v2 - v1)
    gate1 = g_w / (1.0 + t)
    gate2 = g_w * t / (1.0 + t)
    e1 = i1 - N_GROUPS
    e2 = i2 - N_GROUPS

    oh1 = (lane == e1).astype(F32)
    oh2 = (lane == e2).astype(F32)
    oh = oh1 + oh2
    befores = []
    for r0 in range(0, m, r):
        oh_r = oh[r0:r0 + r, :]
        befores.append(_dot(strict_lower, oh_r.astype(BF16)) + run)
        run = run + jnp.sum(oh_r, axis=0, keepdims=True)
    before = befores[0] if len(befores) == 1 else jnp.concatenate(befores, axis=0)
    rank1 = jnp.sum(before * oh1, axis=-1, keepdims=True)
    rank2 = jnp.sum(before * oh2, axis=-1, keepdims=True)
    new_run = run

    rinfo = jnp.where(lane == 0, e1,
            jnp.where(lane == 1, e2,
            jnp.where(lane == 2, gate1,
            jnp.where(lane == 3, gate2,
            jnp.where(lane == 4, rank1,
            jnp.where(lane == 5, rank2, 0.0))))))
    return jnp.transpose(rinfo)[0:SUBLANES, :], new_run


def _conv_segments(a, w_ref, seg_ref, tail_ref, yseg_ref, conv_ref):
    sub = lax.broadcasted_iota(jnp.int32, (SUBLANES, LANES), 0)
    for lt in range(D_CONV // LANES):
        ls = slice(lt * LANES, (lt + 1) * LANES)
        for t0 in range(0, TM, SUBLANES):
            s, m = divmod(t0, SEG)
            seg_ref[lt, pl.ds((SEG_HALO + m) * SUBLANES + s, SUBLANES, stride=SUBLANES), :] = a[t0:t0 + SUBLANES, ls]
        for j in range(SEG_HALO):
            cur = seg_ref[lt, (SEG + j) * SUBLANES:(SEG + j + 1) * SUBLANES, :]
            prev = tail_ref[lt, j * SUBLANES:(j + 1) * SUBLANES, :]
            seg_ref[lt, j * SUBLANES:(j + 1) * SUBLANES, :] = jnp.where(
                sub == 0, pltpu.roll(prev, 1, axis=0), pltpu.roll(cur, 1, axis=0))
            tail_ref[lt, j * SUBLANES:(j + 1) * SUBLANES, :] = cur
        for m0 in range(0, SEG, CONV_BLOCK):
            acc = [jnp.zeros((SUBLANES, LANES), F32) for _ in range(CONV_BLOCK)]
            for idx in range(m0 - HIST, m0 + CONV_BLOCK):
                b = seg_ref[lt, (SEG_HALO + idx) * SUBLANES:(SEG_HALO + idx + 1) * SUBLANES, :]
                for m in range(max(m0, idx), min(m0 + CONV_BLOCK, idx + CONV_WIDTH)):
                    k = idx - m + HIST
                    acc[m - m0] = acc[m - m0] + b * w_ref[k:k + 1, ls]
            for m in range(m0, m0 + CONV_BLOCK):
                yseg_ref[lt, m * SUBLANES:(m + 1) * SUBLANES, :] = acc[m - m0]
        for t0 in range(0, TM, SUBLANES):
            s, m = divmod(t0, SEG)
            conv_ref[t0:t0 + SUBLANES, ls] = yseg_ref[lt, pl.ds(m * SUBLANES + s, SUBLANES, stride=SUBLANES), :]


def _stage_cast(pairs, stage_ref, sem):
    work = [(src, dst, r0) for src, dst in pairs for r0 in range(0, src.shape[0], STAGE_ROWS)]

    def copy(k):
        src, _, r0 = work[k]
        return pltpu.make_async_copy(src.at[pl.ds(r0, STAGE_ROWS)],
                                     stage_ref.at[k % 2, :, pl.ds(0, src.shape[1])], sem.at[k % 2])

    copy(0).start()
    for k, (src, dst, r0) in enumerate(work):
        if k + 1 < len(work):
            copy(k + 1).start()
        copy(k).wait()

        def body(c, carry, k=k, src=src, dst=dst, r0=r0):
            c0 = pl.multiple_of(c * CAST_ROWS, CAST_ROWS)
            dst[pl.ds(r0 + c0, CAST_ROWS), :] = stage_ref[k % 2, pl.ds(c0, CAST_ROWS), 0:src.shape[1]].astype(BF16)
            return carry

        lax.fori_loop(0, STAGE_ROWS // CAST_ROWS, body, 0)


def _trunk_prompt_kernel(x_ref, gmix_ref, win32_ref, convw_ref, convb_ref, lncg_ref, lncb_ref, lnvg_ref, lnvb_ref,
                         wsg_ref, bsg_ref, wout32_ref, gx_ref, wxq32_ref, kmem_ref, v_ref, wxo32_ref, gffn_ref, wr_ref,
                         br_ref,
                         x2_ref, h3_ref, logit_ref, hist_ref,
                         seg_ref, tail_ref, yseg_ref, conv_ref, win_ref, wout_ref, wxq_ref, wxo_ref, stage_ref, stage_sem):
    i = pl.program_id(0)

    @pl.when(i == 0)
    def _():
        tail_ref[...] = jnp.zeros(tail_ref.shape, F32)
        _stage_cast([(win32_ref, win_ref), (wout32_ref, wout_ref), (wxq32_ref, wxq_ref), (wxo32_ref, wxo_ref)],
                    stage_ref, stage_sem)

    x = x_ref[...]
    h = _rms(x, gmix_ref[...]).astype(BF16)

    a_in = _dot(h, win_ref[:, 0:D_CONV])
    a_gate = _dot(h, win_ref[:, D_CONV:2 * D_CONV])
    a = a_in * _sigmoid(a_gate)
    hist_ref[...] = a[TM - HALO:, :]
    _conv_segments(a, convw_ref, seg_ref, tail_ref, yseg_ref, conv_ref)

    y = _ln(conv_ref[...] + convb_ref[...], lncg_ref[...], lncb_ref[...])
    a_out = (y * _sigmoid(y)).astype(BF16)

    u = _dot(h, win_ref[:, 2 * D_CONV:2 * D_CONV + D_SG])
    v = _ln(_dot(h, win_ref[:, 2 * D_CONV + D_SG:]), lnvg_ref[...], lnvb_ref[...]).astype(BF16)
    ri = lax.broadcasted_iota(jnp.int32, (SG_CHUNK, SG_CHUNK), 0)
    ci = lax.broadcasted_iota(jnp.int32, (SG_CHUNK, SG_CHUNK), 1)
    w_tril = [jnp.where(ci <= ri, wsg_ref[hh], 0.0).astype(BF16) for hh in range(SG_HEADS)]
    gate_rows = []
    for c in range(TM // SG_CHUNK):
        rs = slice(c * SG_CHUNK, (c + 1) * SG_CHUNK)
        heads = [_dot(w_tril[hh], v[rs, hh * SG_HEAD_DIM:(hh + 1) * SG_HEAD_DIM]) for hh in range(SG_HEADS)]
        gate_rows.append(jnp.concatenate(heads, axis=1) + bsg_ref[...])
    b_out = (u * jnp.concatenate(gate_rows, axis=0)).astype(BF16)

    x1 = x + _dot(a_out, wout_ref[0:D_CONV, :]) + _dot(b_out, wout_ref[D_CONV:, :])

    hx = _rms(x1, gx_ref[...]).astype(BF16)
    q = _dot(hx, wxq_ref[...]).astype(BF16)
    x2 = x1 + _dot(_attn_heads(q, kmem_ref[...], v_ref[...], False), wxo_ref[...])
    x2_ref[...] = x2

    h3 = _rms(x2, gffn_ref[...]).astype(BF16)
    h3_ref[...] = _pack_bf16_pairs(h3.astype(F32))
    logit_ref[...] = _dot(h3, wr_ref[...]) + br_ref[...]


def _router_kernel(logit_ref, upper_ref, ones_ref, rt_ref, cnt_ref, run_ref):
    @pl.when(pl.program_id(0) == 0)
    def _():
        run_ref[...] = jnp.zeros(run_ref.shape, F32)

    n = ROUTE_ROWS
    lt = jnp.transpose(logit_ref[...])
    neg = jnp.float32(-jnp.inf)
    big = jnp.float32(LOGIT_LANES)

    def first_argmax(vals, rows):
        mx = jnp.max(vals, axis=0, keepdims=True)
        idx = jnp.min(jnp.where(vals == mx, rows, big), axis=0, keepdims=True)
        return mx, idx

    row8 = lax.broadcasted_iota(jnp.int32, (SUBLANES, n), 0).astype(F32)
    lg = jnp.where(row8 < N_GROUPS, lt[0:SUBLANES, :], neg)
    g_max, g_idx = first_argmax(lg, row8)
    g_w = 1.0 / jnp.sum(jnp.exp(lg - g_max), axis=0, keepdims=True)

    n_rows = N_GROUPS + N_EXPERTS + (-(N_GROUPS + N_EXPERTS)) % SUBLANES
    rows = lax.broadcasted_iota(jnp.int32, (n_rows, n), 0).astype(F32)
    lo = N_GROUPS + g_idx * EXPERTS_PER_GROUP
    le = jnp.where((rows >= lo) & (rows < lo + EXPERTS_PER_GROUP), lt[0:n_rows, :], neg)
    v1, i1 = first_argmax(le, rows)
    v2, i2 = first_argmax(jnp.where(rows == i1, neg, le), rows)
    t = jnp.exp(v2 - v1)
    gate1 = g_w / (1.0 + t)
    gate2 = g_w * t / (1.0 + t)
    e1 = i1 - N_GROUPS
    e2 = i2 - N_GROUPS

    erow = lax.broadcasted_iota(jnp.int32, (LOGIT_LANES, n), 0).astype(F32)
    oh1 = (erow == e1).astype(F32)
    oh2 = (erow == e2).astype(F32)
    oh = (oh1 + oh2).astype(BF16)
    r = upper_ref.shape[0]
    run = run_ref[...]
    rank1, rank2 = [], []
    for c0 in range(0, n, r):
        cs = slice(c0, c0 + r)
        before = _dot(oh[:, cs], upper_ref[...]) + run
        rank1.append(jnp.sum(before * oh1[:, cs], axis=0, keepdims=True))
        rank2.append(jnp.sum(before * oh2[:, cs], axis=0, keepdims=True))
        run = run + _dot(oh[:, cs], ones_ref[...])
    run_ref[...] = run
    cnt_ref[...] = run[:, 0:LANES]

    sub = lax.broadcasted_iota(jnp.int32, (SUBLANES, n), 0)
    vals = (e1, e2, gate1, gate2, jnp.concatenate(rank1, axis=1), jnp.concatenate(rank2, axis=1))
    rt = jnp.zeros((SUBLANES, n), F32)
    for k, v in enumerate(vals):
        rt = jnp.where(sub == k, v, rt)
    rt_ref[...] = rt


def _router(logits, rank_block):
    n = logits.shape[0]
    assert n % ROUTE_ROWS == 0 and ROUTE_ROWS % rank_block == 0
    upper = jnp.triu(jnp.ones((rank_block, rank_block), BF16), 1)
    ones = jnp.ones((rank_block, rank_block), BF16)
    return pl.pallas_call(
        _router_kernel,
        grid=(n // ROUTE_ROWS,),
        in_specs=[pl.BlockSpec((ROUTE_ROWS, LOGIT_LANES), lambda i: (i, 0)),
                  pl.BlockSpec(upper.shape, lambda i: (0, 0)),
                  pl.BlockSpec(ones.shape, lambda i: (0, 0))],
        out_specs=(pl.BlockSpec((SUBLANES, ROUTE_ROWS), lambda i: (0, i)),
                   pl.BlockSpec((LOGIT_LANES, LANES), lambda i: (0, 0))),
        out_shape=(jax.ShapeDtypeStruct((SUBLANES, n), F32), jax.ShapeDtypeStruct((LOGIT_LANES, LANES), F32)),
        scratch_shapes=[pltpu.VMEM((LOGIT_LANES, rank_block), F32)],
        compiler_params=pltpu.CompilerParams(dimension_semantics=("arbitrary",), vmem_limit_bytes=VMEM_LIMIT),
        name="router",
    )(logits, upper, ones)


def _const_spec(shape):
    nd = len(shape)
    return pl.BlockSpec(shape, lambda i: (0,) * nd, pipeline_mode=pl.Buffered(1))


def _trunk_prompt(x, p):
    n = x.shape[0]
    assert n % TM == 0
    row = lambda w: pl.BlockSpec((TM, w), lambda i: (i, 0))
    consts = [p["g_mix"], p["w_in"], p["conv_w"], p["conv_b"], p["ln_conv_g"], p["ln_conv_b"], p["ln_v_g"],
              p["ln_v_b"], p["w_sg"], p["b_sg_rows"], p["w_out"], p["g_xattn"], p["w_xq"], p["k"], p["v"],
              p["w_xo"], p["g_ffn"], p["w_router"], p["b_router"]]
    staged = (p["w_in"], p["w_out"], p["w_xq"], p["w_xo"])
    spec = lambda c: pl.BlockSpec(memory_space=pl.ANY) if any(c is s for s in staged) else _const_spec(c.shape)
    return pl.pallas_call(
        _trunk_prompt_kernel,
        grid=(n // TM,),
        in_specs=[row(D_MODEL)] + [spec(c) for c in consts],
        out_specs=(row(D_MODEL), row(D_MODEL // 2), row(LOGIT_LANES),
                   pl.BlockSpec((HALO, D_CONV), lambda i: (0, 0))),
        out_shape=(jax.ShapeDtypeStruct((n, D_MODEL), F32),
                   jax.ShapeDtypeStruct((n, D_MODEL // 2), jnp.uint32),
                   jax.ShapeDtypeStruct((n, LOGIT_LANES), F32),
                   jax.ShapeDtypeStruct((HALO, D_CONV), F32)),
        scratch_shapes=[pltpu.VMEM((D_CONV // LANES, (SEG_HALO + SEG) * SUBLANES, LANES), F32),
                        pltpu.VMEM((D_CONV // LANES, SEG_HALO * SUBLANES, LANES), F32),
                        pltpu.VMEM((D_CONV // LANES, TM, LANES), F32),
                        pltpu.VMEM((TM, D_CONV), F32),
                        pltpu.VMEM(p["w_in"].shape, BF16), pltpu.VMEM(p["w_out"].shape, BF16),
                        pltpu.VMEM(p["w_xq"].shape, BF16), pltpu.VMEM(p["w_xo"].shape, BF16),
                        pltpu.VMEM((2, STAGE_ROWS, max(s.shape[1] for s in staged)), F32),
                        pltpu.SemaphoreType.DMA((2,))],
        compiler_params=pltpu.CompilerParams(dimension_semantics=("arbitrary",), vmem_limit_bytes=VMEM_LIMIT),
        name="trunk_prompt",
    )(x, *consts)


def _trunk_sample_kernel(n_batch, t_len,
                         x_ref, hist_in_ref, run_in_ref, gmix_ref, win_ref, convw_ref, convb_ref, lncg_ref, lncb_ref,
                         lnvg_ref, lnvb_ref, wsgbd_ref, bsg_ref, wout_ref, gx_ref, wxq_ref, kmem_ref, v_ref, wxo_ref,
                         gffn_ref, wr_ref, br_ref, lower_ref,
                         x2_ref, h3_ref, rt_ref, hist_ref, sgv_ref, cnt_ref,
                         ext_ref, conv_ref, att_ref):
    x = x_ref[...]
    h = _rms(x, gmix_ref[...]).astype(BF16)
    z = _dot(h, win_ref[...].astype(BF16))
    a = z[:, 0:D_CONV] * _sigmoid(z[:, D_CONV:2 * D_CONV])
    ext_len = HIST + t_len
    for b in range(n_batch):
        ext_ref[b, 0:HIST, :] = hist_in_ref[b]
        ext_ref[b, HIST:ext_len, :] = a[b * t_len:(b + 1) * t_len, :]
    for b in range(n_batch):
        acc = jnp.zeros((t_len, D_CONV), F32)
        for k in range(CONV_WIDTH):
            acc = acc + ext_ref[b, k:k + t_len, :] * convw_ref[k:k + 1, :]
        conv_ref[b * t_len:(b + 1) * t_len, :] = acc
        hist_ref[b] = ext_ref[b, ext_len - HIST:ext_len, :]

    y = _ln(conv_ref[...] + convb_ref[...], lncg_ref[...], lncb_ref[...])
    a_out = (y * _sigmoid(y)).astype(BF16)

    u = z[:, 2 * D_CONV:2 * D_CONV + D_SG]
    v = _ln(z[:, 2 * D_CONV + D_SG:], lnvg_ref[...], lnvb_ref[...])
    sgv_ref[...] = v
    vb = v.astype(BF16)
    heads = [_dot(wsgbd_ref[hh], vb[:, hh * SG_HEAD_DIM:(hh + 1) * SG_HEAD_DIM]) for hh in range(SG_HEADS)]
    b_out = (u * (jnp.concatenate(heads, axis=1) + bsg_ref[...])).astype(BF16)

    x1 = (x + _dot(a_out, wout_ref[0:D_CONV, :].astype(BF16))
          + _dot(b_out, wout_ref[D_CONV:, :].astype(BF16)))

    hx = _rms(x1, gx_ref[...]).astype(BF16)
    q = _dot(hx, wxq_ref[...].astype(BF16)).astype(BF16)
    for b in range(n_batch):
        rs = slice(b * t_len, (b + 1) * t_len)
        att_ref[rs, :] = _attn_heads(q[rs, :], kmem_ref[b], v_ref[b], True)
    x2 = x1 + _dot(att_ref[...], wxo_ref[...].astype(BF16))
    x2_ref[...] = x2

    h3 = _rms(x2, gffn_ref[...]).astype(BF16)
    m = n_batch * t_len
    h3_ref[0:m, :] = _pack_bf16_pairs(h3.astype(F32))
    if h3_ref.shape[0] > m:
        h3_ref[m:, :] = jnp.zeros((h3_ref.shape[0] - m, D_MODEL // 2), jnp.uint32)
    rt, new_run = _route(_dot(h3, wr_ref[...]) + br_ref[...], run_in_ref[...], lower_ref[...])
    rt_ref[...] = rt
    cnt_ref[...] = new_run


def _trunk_sample(x, hist, run, p, n_batch, t_len):
    m = n_batch * t_len
    args = [x, hist, run, p["g_mix"], p["w_in"], p["conv_w"], p["conv_b"], p["ln_conv_g"], p["ln_conv_b"],
            p["ln_v_g"], p["ln_v_b"], p["w_sg_bd"], p["b_sg_rows_s"], p["w_out"], p["g_xattn"], p["w_xq"],
            p["k_s"], p["v_s"], p["w_xo"], p["g_ffn"], p["w_router"], p["b_router"], p["lower"]]
    return pl.pallas_call(
        functools.partial(_trunk_sample_kernel, n_batch, t_len),
        out_shape=(jax.ShapeDtypeStruct((m, D_MODEL), F32),
                   jax.ShapeDtypeStruct((-(-m // (SC_WORKERS * SUBLANES)) * SC_WORKERS * SUBLANES, D_MODEL // 2),
                                        jnp.uint32),
                   jax.ShapeDtypeStruct((SUBLANES, m), F32),
                   jax.ShapeDtypeStruct((n_batch, HIST, D_CONV), F32),
                   jax.ShapeDtypeStruct((m, D_SG), F32),
                   jax.ShapeDtypeStruct((1, LOGIT_LANES), F32)),
        scratch_shapes=[pltpu.VMEM((n_batch, HIST + t_len, D_CONV), F32),
                        pltpu.VMEM((m, D_CONV), F32),
                        pltpu.VMEM((m, D_MODEL), BF16)],
        compiler_params=pltpu.CompilerParams(vmem_limit_bytes=VMEM_LIMIT),
        name="trunk_sample",
    )(*args)


def _sc_worker_id():
    return lax.axis_index("s") * SC_CORES + lax.axis_index("c")


def _sc_chunk(per_w, max_chunk):
    assert per_w % SUBLANES == 0 and max_chunk <= LANES
    return max(c for c in range(SUBLANES, max_chunk + 1, SUBLANES) if per_w % c == 0)


def _sc_scatter_rows2(tables, slots_a, slots_b, tag_bases, n_rows_out, max_chunk):
    d, dtype = tables[0].shape[1], tables[0].dtype
    plans = []
    for t in tables:
        per_w = t.shape[0] // SC_WORKERS
        assert per_w * SC_WORKERS == t.shape[0]
        chunk = _sc_chunk(per_w, max_chunk)
        plans.append((per_w, chunk, per_w // chunk))
    cmax = max(c for _, c, _ in plans)
    n_t = len(tables)
    mesh = plsc.VectorSubcoreMesh(core_axis_name="c", subcore_axis_name="s")

    nb = SCATTER_BUFS
    lag = nb // 3
    scratch = []
    for _, chunk, _ in plans:
        for _ in range(nb):
            scratch += [pltpu.VMEM((chunk,), jnp.int32), pltpu.VMEM((chunk,), jnp.int32)]
    scratch += [pltpu.VMEM((cmax, d), dtype)] * nb
    scratch += [pltpu.VMEM((cmax, TAG_WORDS), jnp.int32)] * (2 * nb)
    scratch += [pltpu.SemaphoreType.DMA] * (2 * nb)

    @functools.partial(pl.kernel, mesh=mesh,
                       out_type=(jax.ShapeDtypeStruct((n_rows_out, d), dtype),
                                 jax.ShapeDtypeStruct((n_rows_out, TAG_WORDS), jnp.int32)),
                       scratch_types=scratch)
    def scatter(*refs):
        tab_hbm = refs[0:n_t]
        sa_hbm = refs[n_t:2 * n_t]
        sb_hbm = refs[2 * n_t:3 * n_t]
        out_hbm, tag_hbm = refs[3 * n_t], refs[3 * n_t + 1]
        sc = refs[3 * n_t + 2:]
        idx_refs = sc[:2 * nb * n_t]
        rows = sc[2 * nb * n_t:2 * nb * n_t + nb]
        tagbufs = sc[2 * nb * n_t + nb:2 * nb * n_t + 3 * nb]
        lsem = sc[2 * nb * n_t + 3 * nb:2 * nb * n_t + 4 * nb]
        ssem = sc[2 * nb * n_t + 4 * nb:]
        wid = _sc_worker_id()

        work = []
        for t, (per_w, chunk, n_chunks) in enumerate(plans):
            for j in range(n_chunks):
                work.append((t, wid * per_w + j * chunk, chunk))

        def parts(k):
            t, off, chunk = work[k]
            b = k % nb
            ia, ib = idx_refs[2 * nb * t + 2 * b], idx_refs[2 * nb * t + 2 * b + 1]
            full = chunk == cmax
            rv = rows[b] if full else rows[b].at[pl.ds(0, chunk)]
            ta = tagbufs[2 * b] if full else tagbufs[2 * b].at[pl.ds(0, chunk)]
            tb = tagbufs[2 * b + 1] if full else tagbufs[2 * b + 1].at[pl.ds(0, chunk)]
            return t, off, chunk, b, ia, ib, rv, ta, tb

        def start_load(k):
            t, off, chunk, b, ia, ib, rv, ta, tb = parts(k)
            return (pltpu.async_copy(tab_hbm[t].at[pl.ds(off, chunk)], rv, lsem[b]),
                    pltpu.async_copy(sa_hbm[t].at[pl.ds(off, chunk)], ia, lsem[b]),
                    pltpu.async_copy(sb_hbm[t].at[pl.ds(off, chunk)], ib, lsem[b]))

        def start_scatter(k):
            t, off, chunk, b, ia, ib, rv, ta, tb = parts(k)
            base_a, base_b = tag_bases[t]
            for r in range(chunk):
                row_id = (off + r).astype(jnp.int32)
                tagbufs[2 * b][r, pl.ds(0, SC_LANES)] = jnp.zeros((SC_LANES,), jnp.int32) + (base_a + row_id)
                tagbufs[2 * b + 1][r, pl.ds(0, SC_LANES)] = jnp.zeros((SC_LANES,), jnp.int32) + (base_b + row_id)
            return (pltpu.async_copy(rv, out_hbm.at[ia], ssem[b]), pltpu.async_copy(rv, out_hbm.at[ib], ssem[b]),
                    pltpu.async_copy(ta, tag_hbm.at[ia], ssem[b]), pltpu.async_copy(tb, tag_hbm.at[ib], ssem[b]))

        loads, scatters = {}, {}
        for k in range(len(work) + lag):
            if k < len(work):
                if k >= nb:
                    for c in scatters.pop(k - nb):
                        c.wait()
                loads[k] = start_load(k)
            w = k - lag
            if w >= 0:
                for c in loads.pop(w):
                    c.wait()
                scatters[w] = start_scatter(w)
        for w in sorted(scatters):
            for c in scatters[w]:
                c.wait()

    return scatter(*tables, *slots_a, *slots_b)


def _sc_scatter_back(ys, dest, n_rows_out):
    n_rows, d = ys.shape
    per_w = n_rows // SC_WORKERS
    assert per_w * SC_WORKERS == n_rows
    chunk = _sc_chunk(per_w, BACK_CHUNK)
    n_chunks = per_w // chunk
    nb = BACK_BUFS
    lag = nb // 2
    mesh = plsc.VectorSubcoreMesh(core_axis_name="c", subcore_axis_name="s")

    @functools.partial(
        pl.kernel, mesh=mesh,
        out_type=jax.ShapeDtypeStruct((n_rows_out, d), ys.dtype),
        scratch_types=([pltpu.VMEM((chunk,), jnp.int32)] * nb + [pltpu.VMEM((chunk, d), ys.dtype)] * nb
                       + [pltpu.SemaphoreType.DMA] * (2 * nb)),
    )
    def scatter_back(ys_hbm, dest_hbm, out_hbm, *rest):
        idx = rest[:nb]
        rows = rest[nb:2 * nb]
        lsem = rest[2 * nb:3 * nb]
        ssem = rest[3 * nb:]
        base = _sc_worker_id() * per_w

        loads, scatters = {}, {}
        for k in range(n_chunks + lag):
            if k < n_chunks:
                b = k % nb
                if k >= nb:
                    scatters.pop(k - nb).wait()
                off = base + k * chunk
                loads[k] = (pltpu.async_copy(ys_hbm.at[pl.ds(off, chunk)], rows[b], lsem[b]),
                            pltpu.async_copy(dest_hbm.at[pl.ds(off, chunk)], idx[b], lsem[b]))
            w = k - lag
            if w >= 0:
                b = w % nb
                for c in loads.pop(w):
                    c.wait()
                scatters[w] = pltpu.async_copy(rows[b], out_hbm.at[idx[b]], ssem[b])
        for w in sorted(scatters):
            scatters[w].wait()

    return scatter_back(ys, dest)


def _experts_kernel(dump_base, first_ref, nblk_ref, cnt_ref, tot_ref, xs_hbm, tag_hbm, wg_ref, wu_ref, wd_ref,
                    ys_hbm, dest_hbm, xbuf, tbuf, ybuf, dbuf, wg_bf, wu_bf, wd_bf, in_sem, tin_sem, out_sem, dout_sem):
    e = pl.program_id(0)
    nb = nblk_ref[e]
    first = first_ref[e]
    cnt = cnt_ref[e]
    total = tot_ref[0]
    half = D_MODEL // 2

    def in_copies(gb):
        slot = lax.rem(gb, X_BUFS)
        return (pltpu.make_async_copy(xs_hbm.at[pl.ds(gb * BM, BM)], xbuf.at[slot], in_sem.at[slot]),
                pltpu.make_async_copy(tag_hbm.at[pl.ds(gb * BM, BM)], tbuf.at[slot], tin_sem.at[slot]))

    def out_copies(gb):
        slot = lax.rem(gb, Y_BUFS)
        return (pltpu.make_async_copy(ybuf.at[slot], ys_hbm.at[pl.ds(gb * BM, BM)], out_sem.at[slot]),
                pltpu.make_async_copy(dbuf.at[slot], dest_hbm.at[pl.ds(gb * SUBLANES, SUBLANES)], dout_sem.at[slot]))

    def start_in(gb):
        for c in in_copies(gb):
            c.start(priority=ROW_DMA_PRIORITY)

    def start_out(gb):
        for c in out_copies(gb):
            c.start(priority=ROW_DMA_PRIORITY)

    def wait_out(gb):
        for c in out_copies(gb):
            c.wait()

    @pl.when(nb > 0)
    def _():
        @pl.when(first == 0)
        def _():
            for k in range(X_LOOKAHEAD):
                @pl.when(k < total)
                def _():
                    start_in(k)

        wg_bf[...] = wg_ref[0].astype(BF16)
        wu_bf[...] = wu_ref[0].astype(BF16)
        wd_bf[...] = wd_ref[0].astype(BF16)

        def acquire(gb):
            @pl.when(gb + X_LOOKAHEAD < total)
            def _():
                start_in(gb + X_LOOKAHEAD)

            for c in in_copies(gb):
                c.wait()

            @pl.when(gb >= Y_BUFS)
            def _():
                wait_out(gb - Y_BUFS)

        def ffn(gb, j):
            n_live = cnt - j * BM
            tags_t = jnp.transpose(tbuf[lax.rem(gb, X_BUFS)].astype(F32))
            lane = lax.broadcasted_iota(jnp.int32, (SUBLANES, BM), 1)
            own = (dump_base + gb * BM + lane).astype(F32)
            dest = jnp.where(lane < n_live, jnp.broadcast_to(tags_t[0:1, :], (SUBLANES, BM)), own)
            dbuf[lax.rem(gb, Y_BUFS)] = dest.astype(jnp.int32)
            live = lax.broadcasted_iota(jnp.int32, (BM, half), 0) < n_live
            lo, hi = _unpack_bf16_pairs(jnp.where(live, xbuf[lax.rem(gb, X_BUFS)], jnp.uint32(0)))
            g = _dot(lo, wg_bf[0:half, :]) + _dot(hi, wg_bf[half:, :])
            u = _dot(lo, wu_bf[0:half, :]) + _dot(hi, wu_bf[half:, :])
            hm = (g * _sigmoid(g) * u).astype(BF16)
            y = _dot(hm, wd_bf[...])
            ybuf[lax.rem(gb, Y_BUFS)] = _pack_bf16_pairs(y.astype(BF16).astype(F32))

        def block_pair(jp, carry):
            j0 = 2 * jp
            g0 = first + j0
            acquire(g0)
            acquire(g0 + 1)
            ffn(g0, j0)
            ffn(g0 + 1, j0 + 1)
            start_out(g0)
            start_out(g0 + 1)
            return carry

        lax.fori_loop(0, nb // 2, block_pair, 0)

        @pl.when(lax.rem(nb, 2) == 1)
        def _():
            gl = first + nb - 1
            acquire(gl)
            ffn(gl, nb - 1)
            start_out(gl)

        @pl.when(first + nb == total)
        def _():
            for k in range(Y_BUFS):
                @pl.when(total - 1 - k >= 0)
                def _():
                    wait_out(total - 1 - k)


def _experts(xs, tags, n_rows_out, dump_base, first_block, n_blocks_e, counts, w_eg, w_eu, w_ed):
    w_map = lambda e, fb, nb, ct, tot: (e, 0, 0)
    half = D_MODEL // 2
    total = jnp.sum(n_blocks_e).astype(jnp.int32).reshape(1)
    n_blocks = n_rows_out // BM
    return pl.pallas_call(
        functools.partial(_experts_kernel, dump_base),
        grid_spec=pltpu.PrefetchScalarGridSpec(
            num_scalar_prefetch=4,
            grid=(N_EXPERTS,),
            in_specs=[pl.BlockSpec(memory_space=pl.ANY),
                      pl.BlockSpec(memory_space=pl.ANY),
                      pl.BlockSpec((1, D_MODEL, D_EXPERT), w_map),
                      pl.BlockSpec((1, D_MODEL, D_EXPERT), w_map),
                      pl.BlockSpec((1, D_EXPERT, D_MODEL), w_map)],
            out_specs=(pl.BlockSpec(memory_space=pl.ANY), pl.BlockSpec(memory_space=pl.ANY)),
            scratch_shapes=[pltpu.VMEM((X_BUFS, BM, half), jnp.uint32), pltpu.VMEM((X_BUFS, BM, TAG_WORDS), jnp.int32),
                            pltpu.VMEM((Y_BUFS, BM, half), jnp.uint32), pltpu.VMEM((Y_BUFS, SUBLANES, BM), jnp.int32),
                            pltpu.VMEM((D_MODEL, D_EXPERT), BF16), pltpu.VMEM((D_MODEL, D_EXPERT), BF16),
                            pltpu.VMEM((D_EXPERT, D_MODEL), BF16),
                            pltpu.SemaphoreType.DMA((X_BUFS,)), pltpu.SemaphoreType.DMA((X_BUFS,)),
                            pltpu.SemaphoreType.DMA((Y_BUFS,)), pltpu.SemaphoreType.DMA((Y_BUFS,))]),
        out_shape=(jax.ShapeDtypeStruct((n_rows_out, half), jnp.uint32),
                   jax.ShapeDtypeStruct((n_blocks * SUBLANES, BM), jnp.int32)),
        compiler_params=pltpu.CompilerParams(dimension_semantics=("arbitrary",), vmem_limit_bytes=VMEM_LIMIT),
        name="experts",
    )(first_block, n_blocks_e, counts, total, xs, tags, w_eg, w_eu, w_ed)


def _combine_kernel(x2_ref, y1_ref, y2_ref, rt_ref, g_ref, o_ref):
    rt = rt_ref[...]
    r = jnp.transpose(jnp.concatenate([rt, jnp.zeros((LANES - rt.shape[0], rt.shape[1]), F32)], axis=0))
    g1, g2 = r[:, 2:3], r[:, 3:4]
    half = D_MODEL // 2
    y1_lo, y1_hi = _unpack_bf16_pairs_f32(y1_ref[...])
    y2_lo, y2_hi = _unpack_bf16_pairs_f32(y2_ref[...])
    x_lo = x2_ref[:, 0:half] + g1 * y1_lo + g2 * y2_lo
    x_hi = x2_ref[:, half:] + g1 * y1_hi + g2 * y2_hi
    ms = (jnp.sum(x_lo * x_lo, axis=-1, keepdims=True) + jnp.sum(x_hi * x_hi, axis=-1, keepdims=True)) / D_MODEL
    inv = lax.rsqrt(ms + EPS)
    o_ref[:, 0:half] = x_lo * inv * g_ref[:, 0:half]
    o_ref[:, half:] = x_hi * inv * g_ref[:, half:]


def _combine(x2, yg, rt, g_final, tm, blk1, blk2):
    n = x2.shape[0]
    return pl.pallas_call(
        _combine_kernel,
        grid=(n // tm,),
        in_specs=[pl.BlockSpec((tm, D_MODEL), lambda i: (i, 0)),
                  pl.BlockSpec((tm, D_MODEL // 2), lambda i: (blk1 + i, 0)),
                  pl.BlockSpec((tm, D_MODEL // 2), lambda i: (blk2 + i, 0)),
                  pl.BlockSpec((SUBLANES, tm), lambda i: (0, i)),
                  pl.BlockSpec((1, D_MODEL), lambda i: (0, 0))],
        out_specs=pl.BlockSpec((tm, D_MODEL), lambda i: (i, 0)),
        out_shape=jax.ShapeDtypeStruct((n, D_MODEL), F32),
        compiler_params=pltpu.CompilerParams(dimension_semantics=("arbitrary",), vmem_limit_bytes=VMEM_LIMIT),
        name="combine",
    )(x2, yg, yg, rt, g_final)


def _scatter_back(ys, dest, n_rows_out):
    return _sc_scatter_back(ys, dest, n_rows_out)


def _scatter_rows2(tables, slots_a, slots_b, tag_bases, n_rows_out):
    return _sc_scatter_rows2(tables, slots_a, slots_b, tag_bases, n_rows_out, SCATTER_CHUNK)


def kernel(x_prompt, x_sample, mem_prompt, state_conv, cache_mem_k, cache_mem_v, g_mix, w_in, conv_w, conv_b, ln_conv_g, ln_conv_b, ln_v_g, ln_v_b, w_sg, b_sg, w_out, g_mem, w_mk, w_mv, g_xattn, w_xq, w_xo, g_ffn, w_router_group, b_router_group, w_router_expert, b_router_expert, w_expert_gate, w_expert_up, w_expert_down, g_final):
    assert x_prompt.shape[0] == 1 and g_mix.shape[0] == 1
    n_p = x_prompt.shape[1]
    n_batch, t_len = x_sample.shape[0], x_sample.shape[1]
    n_s = n_batch * t_len
    row = lambda a: a.reshape(1, -1)

    w_router = jnp.concatenate(
        [w_router_group[0], jnp.transpose(w_router_expert[0], (1, 0, 2)).reshape(D_MODEL, N_EXPERTS)], axis=1)
    w_router = jnp.pad(w_router, ((0, 0), (0, LOGIT_LANES - w_router.shape[1]))).astype(BF16)
    b_router = jnp.pad(jnp.concatenate([b_router_group[0], b_router_expert[0].reshape(-1)]),
                       (0, LOGIT_LANES - N_GROUPS - N_EXPERTS)).reshape(1, LOGIT_LANES)
    tril_t = jnp.tril(jnp.ones((t_len, t_len), bool))
    w_sg_t = jnp.where(tril_t, w_sg[0][:, :t_len, :t_len], 0.0)
    eye_b = jnp.eye(n_batch, dtype=F32)
    w_sg_bd = jnp.einsum("ab,hij->haibj", eye_b, w_sg_t).reshape(SG_HEADS, n_s, n_s).astype(BF16)
    p = {
        "g_mix": row(g_mix[0]), "w_in": w_in[0],
        "conv_w": jnp.pad(conv_w[0], ((0, 1), (0, 0))), "conv_b": row(conv_b[0]),
        "ln_conv_g": row(ln_conv_g[0]), "ln_conv_b": row(ln_conv_b[0]),
        "ln_v_g": row(ln_v_g[0]), "ln_v_b": row(ln_v_b[0]),
        "w_sg": w_sg[0],
        "b_sg_rows": jnp.repeat(b_sg[0].T, SG_HEAD_DIM, axis=1),
        "w_sg_bd": w_sg_bd,
        "b_sg_rows_s": jnp.tile(jnp.repeat(b_sg[0][:, :t_len].T, SG_HEAD_DIM, axis=1), (n_batch, 1)),
        "w_out": w_out[0], "g_xattn": row(g_xattn[0]),
        "w_xq": w_xq[0], "w_xo": w_xo[0], "g_ffn": row(g_ffn[0]),
        "w_router": w_router, "b_router": b_router,
        "lower": jnp.tril(jnp.ones((n_s, n_s), BF16), -1),
    }

    k_p, v_p, p["k"], p["v"] = _memkv(mem_prompt[0], row(g_mem[0]), w_mk[0], w_mv[0])
    p["k_s"] = jnp.transpose(cache_mem_k[0].astype(BF16), (0, 2, 3, 1)).reshape(n_batch, D_MODEL, N_MEM)
    p["v_s"] = cache_mem_v[0].astype(BF16).reshape(n_batch, N_MEM, D_MODEL)

    assert n_p % n_s == 0
    x2_p, h3_p, logits_p, hist_p = _trunk_prompt(x_prompt[0], p)
    rt_p, cnt_t = _router(logits_p, TM)
    cnt_p = cnt_t[:, 0].reshape(1, LOGIT_LANES)
    x2_s, h3_s, rt_s, hist_s, sgv_s, cnt = _trunk_sample(
        x_sample.reshape(n_s, D_MODEL), state_conv[0], cnt_p, p, n_batch, t_len)

    experts = jnp.arange(N_EXPERTS, dtype=jnp.int32)
    w_e = (w_expert_gate[0], w_expert_up[0], w_expert_down[0])

    def moe_pass(cnt, h3_tables, rts, n_real):
        n_tot = sum(n_real)
        n_slots = -(-(n_tot * 2) // BM) * BM + N_EXPERTS * BM
        counts = cnt[0, :N_EXPERTS].astype(jnp.int32)
        padded = (counts + BM - 1) // BM * BM
        pad_start = jnp.cumsum(padded) - padded

        def one(e_row, rank_row):
            e = e_row.astype(jnp.int32)
            start = jnp.sum(jnp.where(e[None, :] == experts[:, None], pad_start[:, None], 0), axis=0)
            return start + rank_row.astype(jnp.int32)

        slots = [(one(rt[0], rt[4]), one(rt[1], rt[5])) for rt in rts]
        sa, sb, tag_bases, spare0, dest0 = [], [], [], n_slots, 0
        for tab, (a, b), n in zip(h3_tables, slots, n_real):
            n_spare = tab.shape[0] - n
            spare = spare0 + jnp.arange(n_spare, dtype=jnp.int32)
            sa.append(jnp.concatenate([a, spare]))
            sb.append(jnp.concatenate([b, spare + n_spare]))
            spare0 += 2 * n_spare
            tag_bases.append((dest0, dest0 + n))
            dest0 += 2 * n
        xs, tags = _scatter_rows2(tuple(h3_tables), tuple(sa), tuple(sb), tuple(tag_bases), spare0)
        ys, dest_blocks = _experts(xs, tags, n_slots, dest0, pad_start // BM, padded // BM, counts, *w_e)
        slot = jnp.arange(n_slots, dtype=jnp.int32)
        dest = dest_blocks.reshape(n_slots // BM, SUBLANES, BM)[:, 0, :].reshape(-1)
        dest = jnp.where(slot < jnp.sum(padded), dest, dest0 + slot)
        return _scatter_back(ys, dest, dest0 + n_slots)

    yg = moe_pass(cnt, [h3_p, h3_s], [rt_p, rt_s], [n_p, n_s])

    gf = row(g_final)
    y_p = _combine(x2_p, yg, rt_p, gf, TM_COMBINE, 0, n_p // TM_COMBINE)
    y_s = _combine(x2_s, yg, rt_s, gf, n_s, 2 * n_p // n_s, 2 * n_p // n_s + 1)

    return (y_p.reshape(1, n_p, D_MODEL),
            y_s.reshape(n_batch, t_len, D_MODEL),
            hist_p[HALO - HIST:].reshape(1, 1, HIST, D_CONV),
            hist_s.reshape(1, n_batch, HIST, D_CONV),
            k_p.reshape(1, 1, N_MEM, X_HEADS, X_HEAD_DIM),
            v_p.reshape(1, 1, N_MEM, X_HEADS, X_HEAD_DIM),
            sgv_s.reshape(1, n_batch, t_len, D_SG))
```

```python
import functools

import jax
import jax.numpy as jnp
from jax import lax
from jax.experimental import pallas as pl
from jax.experimental.pallas import tpu as pltpu
from jax.experimental.pallas import tpu_sc as plsc

D_MODEL = 1024
D_CONV = 512
D_SG = 512
CONV_WIDTH = 31
HIST = CONV_WIDTH - 1
SG_HEADS = 4
SG_HEAD_DIM = 128
SG_CHUNK = 128
N_MEM = 256
X_HEADS = 4
X_HEAD_DIM = 256
N_GROUPS = 4
EXPERTS_PER_GROUP = 8
N_EXPERTS = 32
D_EXPERT = 512
EPS = 1e-6

LANES = 128
SUBLANES = 8
SC_CORES = 2
SC_SUBCORES = 16
SC_WORKERS = SC_CORES * SC_SUBCORES
SC_LANES = 16
VMEM_LIMIT = 56 * 1024 * 1024

TM = 1024
TM_COMBINE = 2048
ROUTE_ROWS = 4096
ROUTE_RANK_ROWS = 512
HALO = 32
SEG = TM // SUBLANES
SEG_HALO = 32
CONV_BLOCK = 16
CAST_ROWS = 64
STAGE_ROWS = 256
BM = 256
X_LOOKAHEAD = 4
X_BUFS = X_LOOKAHEAD + 2
Y_BUFS = 4
ROW_DMA_PRIORITY = 1
SCATTER_CHUNK = 32
SCATTER_BUFS = 4
TAG_WORDS = 128
BACK_CHUNK = 56
BACK_BUFS = 4
LOGIT_LANES = 128

F32 = jnp.float32
BF16 = jnp.bfloat16


def _dot(a, b):
    return jnp.dot(a, b, preferred_element_type=F32)


def _rms(x, g):
    return x * lax.rsqrt(jnp.mean(x * x, axis=-1, keepdims=True) + EPS) * g


def _ln(x, g, b):
    mu = jnp.mean(x, axis=-1, keepdims=True)
    xc = x - mu
    var = jnp.mean(xc * xc, axis=-1, keepdims=True)
    return xc * lax.rsqrt(var + EPS) * g + b


def _sigmoid(x):
    return 1.0 / (1.0 + jnp.exp(-x))


def _pack_bf16_pairs(h):
    bits = lax.bitcast_convert_type(h, jnp.uint32)
    half = h.shape[1] // 2
    lo = lax.shift_right_logical(bits[:, :half], jnp.uint32(16))
    hi = bits[:, half:] & jnp.uint32(0xFFFF0000)
    return hi | lo


def _unpack_bf16_pairs_f32(p):
    lo = lax.bitcast_convert_type(lax.shift_left(p, jnp.uint32(16)), F32)
    hi = lax.bitcast_convert_type(p & jnp.uint32(0xFFFF0000), F32)
    return lo, hi


def _unpack_bf16_pairs(p):
    lo, hi = _unpack_bf16_pairs_f32(p)
    return lo.astype(BF16), hi.astype(BF16)


def _memkv_kernel(mem_ref, g_ref, wk_ref, wv_ref, k_ref, v_ref, kbf_ref, vbf_ref):
    m = _rms(mem_ref[...], g_ref[...]).astype(BF16)
    k = _dot(m, wk_ref[...].astype(BF16))
    v = _dot(m, wv_ref[...].astype(BF16))
    k_ref[...] = k
    v_ref[...] = v
    kbf_ref[...] = k.astype(BF16)
    vbf_ref[...] = v.astype(BF16)


def _memkv(mem, g_mem, w_mk, w_mv):
    return pl.pallas_call(
        _memkv_kernel,
        out_shape=(jax.ShapeDtypeStruct((N_MEM, D_MODEL), F32), jax.ShapeDtypeStruct((N_MEM, D_MODEL), F32),
                   jax.ShapeDtypeStruct((N_MEM, D_MODEL), BF16), jax.ShapeDtypeStruct((N_MEM, D_MODEL), BF16)),
        compiler_params=pltpu.CompilerParams(vmem_limit_bytes=VMEM_LIMIT),
        name="memkv",
    )(mem, g_mem, w_mk, w_mv)


def _attn_heads(q, k, v, k_transposed):
    outs = []
    for h in range(X_HEADS):
        sl = slice(h * X_HEAD_DIM, (h + 1) * X_HEAD_DIM)
        if k_transposed:
            s = _dot(q[:, sl], k[sl, :])
        else:
            s = lax.dot_general(q[:, sl], k[:, sl], (((1,), (1,)), ((), ())), preferred_element_type=F32)
        s = s * (X_HEAD_DIM ** -0.5)
        s = s - jnp.max(s, axis=-1, keepdims=True)
        p = jnp.exp(s)
        p = p / jnp.sum(p, axis=-1, keepdims=True)
        outs.append(_dot(p.astype(BF16), v[:, sl]).astype(BF16))
    return jnp.concatenate(outs, axis=1)


def _route(logits, run, strict_lower):
    m = logits.shape[0]
    r = strict_lower.shape[0]
    lane = lax.broadcasted_iota(jnp.int32, (m, LOGIT_LANES), 1).astype(F32)
    neg = jnp.float32(-jnp.inf)
    big = jnp.float32(LOGIT_LANES)

    def first_argmax(vals):
        mx = jnp.max(vals, axis=-1, keepdims=True)
        idx = jnp.min(jnp.where(vals == mx, lane, big), axis=-1, keepdims=True)
        return mx, idx

    lg = jnp.where(lane < N_GROUPS, logits, neg)
    g_max, g_idx = first_argmax(lg)
    g_w = 1.0 / jnp.sum(jnp.exp(lg - g_max), axis=-1, keepdims=True)

    lo = N_GROUPS + g_idx * EXPERTS_PER_GROUP
    le = jnp.where((lane >= lo) & (lane < lo + EXPERTS_PER_GROUP), logits, neg)
    v1, i1 = first_argmax(le)
    v2, i2 = first_argmax(jnp.where(lane == i1, neg, le))
    t = jnp.exp(v2 - v1)
    gate1 = g_w / (1.0 + t)
    gate2 = g_w * t / (1.0 + t)
    e1 = i1 - N_GROUPS
    e2 = i2 - N_GROUPS

    oh1 = (lane == e1).astype(F32)
    oh2 = (lane == e2).astype(F32)
    oh = oh1 + oh2
    befores = []
    for r0 in range(0, m, r):
        oh_r = oh[r0:r0 + r, :]
        befores.append(_dot(strict_lower, oh_r.astype(BF16)) + run)
        run = run + jnp.sum(oh_r, axis=0, keepdims=True)
    before = befores[0] if len(befores) == 1 else jnp.concatenate(befores, axis=0)
    rank1 = jnp.sum(before * oh1, axis=-1, keepdims=True)
    rank2 = jnp.sum(before * oh2, axis=-1, keepdims=True)
    new_run = run

    rinfo = jnp.where(lane == 0, e1,
            jnp.where(lane == 1, e2,
            jnp.where(lane == 2, gate1,
            jnp.where(lane == 3, gate2,
            jnp.where(lane == 4, rank1,
            jnp.where(lane == 5, rank2, 0.0))))))
    return jnp.transpose(rinfo)[0:SUBLANES, :], new_run


def _conv_segments(a, w_ref, seg_ref, tail_ref, yseg_ref, conv_ref):
    sub = lax.broadcasted_iota(jnp.int32, (SUBLANES, LANES), 0)
    for lt in range(D_CONV // LANES):
        ls = slice(lt * LANES, (lt + 1) * LANES)
        for t0 in range(0, TM, SUBLANES):
            s, m = divmod(t0, SEG)
            seg_ref[lt, pl.ds((SEG_HALO + m) * SUBLANES + s, SUBLANES, stride=SUBLANES), :] = a[t0:t0 + SUBLANES, ls]
        for j in range(SEG_HALO):
            cur = seg_ref[lt, (SEG + j) * SUBLANES:(SEG + j + 1) * SUBLANES, :]
            prev = tail_ref[lt, j * SUBLANES:(j + 1) * SUBLANES, :]
            seg_ref[lt, j * SUBLANES:(j + 1) * SUBLANES, :] = jnp.where(
                sub == 0, pltpu.roll(prev, 1, axis=0), pltpu.roll(cur, 1, axis=0))
            tail_ref[lt, j * SUBLANES:(j + 1) * SUBLANES, :] = cur
        for m0 in range(0, SEG, CONV_BLOCK):
            acc = [jnp.zeros((SUBLANES, LANES), F32) for _ in range(CONV_BLOCK)]
            for idx in range(m0 - HIST, m0 + CONV_BLOCK):
                b = seg_ref[lt, (SEG_HALO + idx) * SUBLANES:(SEG_HALO + idx + 1) * SUBLANES, :]
                for m in range(max(m0, idx), min(m0 + CONV_BLOCK, idx + CONV_WIDTH)):
                    k = idx - m + HIST
                    acc[m - m0] = acc[m - m0] + b * w_ref[k:k + 1, ls]
            for m in range(m0, m0 + CONV_BLOCK):
                yseg_ref[lt, m * SUBLANES:(m + 1) * SUBLANES, :] = acc[m - m0]
        for t0 in range(0, TM, SUBLANES):
            s, m = divmod(t0, SEG)
            conv_ref[t0:t0 + SUBLANES, ls] = yseg_ref[lt, pl.ds(m * SUBLANES + s, SUBLANES, stride=SUBLANES), :]


def _stage_cast(pairs, stage_ref, sem):
    work = [(src, dst, r0) for src, dst in pairs for r0 in range(0, src.shape[0], STAGE_ROWS)]

    def copy(k):
        src, _, r0 = work[k]
        return pltpu.make_async_copy(src.at[pl.ds(r0, STAGE_ROWS)],
                                     stage_ref.at[k % 2, :, pl.ds(0, src.shape[1])], sem.at[k % 2])

    copy(0).start()
    for k, (src, dst, r0) in enumerate(work):
        if k + 1 < len(work):
            copy(k + 1).start()
        copy(k).wait()

        def body(c, carry, k=k, src=src, dst=dst, r0=r0):
            c0 = pl.multiple_of(c * CAST_ROWS, CAST_ROWS)
            dst[pl.ds(r0 + c0, CAST_ROWS), :] = stage_ref[k % 2, pl.ds(c0, CAST_ROWS), 0:src.shape[1]].astype(BF16)
            return carry

        lax.fori_loop(0, STAGE_ROWS // CAST_ROWS, body, 0)


def _trunk_prompt_kernel(x_ref, gmix_ref, win32_ref, convw_ref, convb_ref, lncg_ref, lncb_ref, lnvg_ref, lnvb_ref,
                         wsg_ref, bsg_ref, wout32_ref, gx_ref, wxq32_ref, kmem_ref, v_ref, wxo32_ref, gffn_ref, wr_ref,
                         br_ref,
                         x2_ref, h3_ref, logit_ref, hist_ref,
                         seg_ref, tail_ref, yseg_ref, conv_ref, win_ref, wout_ref, wxq_ref, wxo_ref, stage_ref, stage_sem):
    i = pl.program_id(0)

    @pl.when(i == 0)
    def _():
        tail_ref[...] = jnp.zeros(tail_ref.shape, F32)
        _stage_cast([(win32_ref, win_ref), (wout32_ref, wout_ref), (wxq32_ref, wxq_ref), (wxo32_ref, wxo_ref)],
                    stage_ref, stage_sem)

    x = x_ref[...]
    h = _rms(x, gmix_ref[...]).astype(BF16)

    a_in = _dot(h, win_ref[:, 0:D_CONV])
    a_gate = _dot(h, win_ref[:, D_CONV:2 * D_CONV])
    a = a_in * _sigmoid(a_gate)
    hist_ref[...] = a[TM - HALO:, :]
    _conv_segments(a, convw_ref, seg_ref, tail_ref, yseg_ref, conv_ref)

    y = _ln(conv_ref[...] + convb_ref[...], lncg_ref[...], lncb_ref[...])
    a_out = (y * _sigmoid(y)).astype(BF16)

    u = _dot(h, win_ref[:, 2 * D_CONV:2 * D_CONV + D_SG])
    v = _ln(_dot(h, win_ref[:, 2 * D_CONV + D_SG:]), lnvg_ref[...], lnvb_ref[...]).astype(BF16)
    ri = lax.broadcasted_iota(jnp.int32, (SG_CHUNK, SG_CHUNK), 0)
    ci = lax.broadcasted_iota(jnp.int32, (SG_CHUNK, SG_CHUNK), 1)
    w_tril = [jnp.where(ci <= ri, wsg_ref[hh], 0.0).astype(BF16) for hh in range(SG_HEADS)]
    gate_rows = []
    for c in range(TM // SG_CHUNK):
        rs = slice(c * SG_CHUNK, (c + 1) * SG_CHUNK)
        heads = [_dot(w_tril[hh], v[rs, hh * SG_HEAD_DIM:(hh + 1) * SG_HEAD_DIM]) for hh in range(SG_HEADS)]
        gate_rows.append(jnp.concatenate(heads, axis=1) + bsg_ref[...])
    b_out = (u * jnp.concatenate(gate_rows, axis=0)).astype(BF16)

    x1 = x + _dot(a_out, wout_ref[0:D_CONV, :]) + _dot(b_out, wout_ref[D_CONV:, :])

    hx = _rms(x1, gx_ref[...]).astype(BF16)
    q = _dot(hx, wxq_ref[...]).astype(BF16)
    x2 = x1 + _dot(_attn_heads(q, kmem_ref[...], v_ref[...], False), wxo_ref[...])
    x2_ref[...] = x2

    h3 = _rms(x2, gffn_ref[...]).astype(BF16)
    h3_ref[...] = _pack_bf16_pairs(h3.astype(F32))
    logit_ref[...] = _dot(h3, wr_ref[...]) + br_ref[...]


def _router_kernel(logit_ref, upper_ref, ones_ref, rt_ref, cnt_ref, run_ref):
    @pl.when(pl.program_id(0) == 0)
    def _():
        run_ref[...] = jnp.zeros(run_ref.shape, F32)

    n = ROUTE_ROWS
    lt = jnp.transpose(logit_ref[...])
    neg = jnp.float32(-jnp.inf)
    big = jnp.float32(LOGIT_LANES)

    def first_argmax(vals, rows):
        mx = jnp.max(vals, axis=0, keepdims=True)
        idx = jnp.min(jnp.where(vals == mx, rows, big), axis=0, keepdims=True)
        return mx, idx

    row8 = lax.broadcasted_iota(jnp.int32, (SUBLANES, n), 0).astype(F32)
    lg = jnp.where(row8 < N_GROUPS, lt[0:SUBLANES, :], neg)
    g_max, g_idx = first_argmax(lg, row8)
    g_w = 1.0 / jnp.sum(jnp.exp(lg - g_max), axis=0, keepdims=True)

    n_rows = N_GROUPS + N_EXPERTS + (-(N_GROUPS + N_EXPERTS)) % SUBLANES
    rows = lax.broadcasted_iota(jnp.int32, (n_rows, n), 0).astype(F32)
    lo = N_GROUPS + g_idx * EXPERTS_PER_GROUP
    le = jnp.where((rows >= lo) & (rows < lo + EXPERTS_PER_GROUP), lt[0:n_rows, :], neg)
    v1, i1 = first_argmax(le, rows)
    v2, i2 = first_argmax(jnp.where(rows == i1, neg, le), rows)
    t = jnp.exp(v2 - v1)
    gate1 = g_w / (1.0 + t)
    gate2 = g_w * t / (1.0 + t)
    e1 = i1 - N_GROUPS
    e2 = i2 - N_GROUPS

    erow = lax.broadcasted_iota(jnp.int32, (LOGIT_LANES, n), 0).astype(F32)
    oh1 = (erow == e1).astype(F32)
    oh2 = (erow == e2).astype(F32)
    oh = (oh1 + oh2).astype(BF16)
    r = upper_ref.shape[0]
    run = run_ref[...]
    rank1, rank2 = [], []
    for c0 in range(0, n, r):
        cs = slice(c0, c0 + r)
        before = _dot(oh[:, cs], upper_ref[...]) + run
        rank1.append(jnp.sum(before * oh1[:, cs], axis=0, keepdims=True))
        rank2.append(jnp.sum(before * oh2[:, cs], axis=0, keepdims=True))
        run = run + _dot(oh[:, cs], ones_ref[...])
    run_ref[...] = run
    cnt_ref[...] = run[:, 0:LANES]

    sub = lax.broadcasted_iota(jnp.int32, (SUBLANES, n), 0)
    vals = (e1, e2, gate1, gate2, jnp.concatenate(rank1, axis=1), jnp.concatenate(rank2, axis=1))
    rt = jnp.zeros((SUBLANES, n), F32)
    for k, v in enumerate(vals):
        rt = jnp.where(sub == k, v, rt)
    rt_ref[...] = rt


def _router(logits, rank_block):
    n = logits.shape[0]
    assert n % ROUTE_ROWS == 0 and ROUTE_ROWS % rank_block == 0
    upper = jnp.triu(jnp.ones((rank_block, rank_block), BF16), 1)
    ones = jnp.ones((rank_block, rank_block), BF16)
    return pl.pallas_call(
        _router_kernel,
        grid=(n // ROUTE_ROWS,),
        in_specs=[pl.BlockSpec((ROUTE_ROWS, LOGIT_LANES), lambda i: (i, 0)),
                  pl.BlockSpec(upper.shape, lambda i: (0, 0)),
                  pl.BlockSpec(ones.shape, lambda i: (0, 0))],
        out_specs=(pl.BlockSpec((SUBLANES, ROUTE_ROWS), lambda i: (0, i)),
                   pl.BlockSpec((LOGIT_LANES, LANES), lambda i: (0, 0))),
        out_shape=(jax.ShapeDtypeStruct((SUBLANES, n), F32), jax.ShapeDtypeStruct((LOGIT_LANES, LANES), F32)),
        scratch_shapes=[pltpu.VMEM((LOGIT_LANES, rank_block), F32)],
        compiler_params=pltpu.CompilerParams(dimension_semantics=("arbitrary",), vmem_limit_bytes=VMEM_LIMIT),
        name="router",
    )(logits, upper, ones)


def _const_spec(shape):
    nd = len(shape)
    return pl.BlockSpec(shape, lambda i: (0,) * nd, pipeline_mode=pl.Buffered(1))


def _trunk_prompt(x, p):
    n = x.shape[0]
    assert n % TM == 0
    row = lambda w: pl.BlockSpec((TM, w), lambda i: (i, 0))
    consts = [p["g_mix"], p["w_in"], p["conv_w"], p["conv_b"], p["ln_conv_g"], p["ln_conv_b"], p["ln_v_g"],
              p["ln_v_b"], p["w_sg"], p["b_sg_rows"], p["w_out"], p["g_xattn"], p["w_xq"], p["k"], p["v"],
              p["w_xo"], p["g_ffn"], p["w_router"], p["b_router"]]
    staged = (p["w_in"], p["w_out"], p["w_xq"], p["w_xo"])
    spec = lambda c: pl.BlockSpec(memory_space=pl.ANY) if any(c is s for s in staged) else _const_spec(c.shape)
    return pl.pallas_call(
        _trunk_prompt_kernel,
        grid=(n // TM,),
        in_specs=[row(D_MODEL)] + [spec(c) for c in consts],
        out_specs=(row(D_MODEL), row(D_MODEL // 2), row(LOGIT_LANES),
                   pl.BlockSpec((HALO, D_CONV), lambda i: (0, 0))),
        out_shape=(jax.ShapeDtypeStruct((n, D_MODEL), F32),
                   jax.ShapeDtypeStruct((n, D_MODEL // 2), jnp.uint32),
                   jax.ShapeDtypeStruct((n, LOGIT_LANES), F32),
                   jax.ShapeDtypeStruct((HALO, D_CONV), F32)),
        scratch_shapes=[pltpu.VMEM((D_CONV // LANES, (SEG_HALO + SEG) * SUBLANES, LANES), F32),
                        pltpu.VMEM((D_CONV // LANES, SEG_HALO * SUBLANES, LANES), F32),
                        pltpu.VMEM((D_CONV // LANES, TM, LANES), F32),
                        pltpu.VMEM((TM, D_CONV), F32),
                        pltpu.VMEM(p["w_in"].shape, BF16), pltpu.VMEM(p["w_out"].shape, BF16),
                        pltpu.VMEM(p["w_xq"].shape, BF16), pltpu.VMEM(p["w_xo"].shape, BF16),
                        pltpu.VMEM((2, STAGE_ROWS, max(s.shape[1] for s in staged)), F32),
                        pltpu.SemaphoreType.DMA((2,))],
        compiler_params=pltpu.CompilerParams(dimension_semantics=("arbitrary",), vmem_limit_bytes=VMEM_LIMIT),
        name="trunk_prompt",
    )(x, *consts)


def _trunk_sample_kernel(n_batch, t_len,
                         x_ref, hist_in_ref, run_in_ref, gmix_ref, win_ref, convw_ref, convb_ref, lncg_ref, lncb_ref,
                         lnvg_ref, lnvb_ref, wsgbd_ref, bsg_ref, wout_ref, gx_ref, wxq_ref, kmem_ref, v_ref, wxo_ref,
                         gffn_ref, wr_ref, br_ref, lower_ref,
                         x2_ref, h3_ref, rt_ref, hist_ref, sgv_ref, cnt_ref,
                         ext_ref, conv_ref, att_ref):
    x = x_ref[...]
    h = _rms(x, gmix_ref[...]).astype(BF16)
    z = _dot(h, win_ref[...].astype(BF16))
    a = z[:, 0:D_CONV] * _sigmoid(z[:, D_CONV:2 * D_CONV])
    ext_len = HIST + t_len
    for b in range(n_batch):
        ext_ref[b, 0:HIST, :] = hist_in_ref[b]
        ext_ref[b, HIST:ext_len, :] = a[b * t_len:(b + 1) * t_len, :]
    for b in range(n_batch):
        acc = jnp.zeros((t_len, D_CONV), F32)
        for k in range(CONV_WIDTH):
            acc = acc + ext_ref[b, k:k + t_len, :] * convw_ref[k:k + 1, :]
        conv_ref[b * t_len:(b + 1) * t_len, :] = acc
        hist_ref[b] = ext_ref[b, ext_len - HIST:ext_len, :]

    y = _ln(conv_ref[...] + convb_ref[...], lncg_ref[...], lncb_ref[...])
    a_out = (y * _sigmoid(y)).astype(BF16)

    u = z[:, 2 * D_CONV:2 * D_CONV + D_SG]
    v = _ln(z[:, 2 * D_CONV + D_SG:], lnvg_ref[...], lnvb_ref[...])
    sgv_ref[...] = v
    vb = v.astype(BF16)
    heads = [_dot(wsgbd_ref[hh], vb[:, hh * SG_HEAD_DIM:(hh + 1) * SG_HEAD_DIM]) for hh in range(SG_HEADS)]
    b_out = (u * (jnp.concatenate(heads, axis=1) + bsg_ref[...])).astype(BF16)

    x1 = (x + _dot(a_out, wout_ref[0:D_CONV, :].astype(BF16))
          + _dot(b_out, wout_ref[D_CONV:, :].astype(BF16)))

    hx = _rms(x1, gx_ref[...]).astype(BF16)
    q = _dot(hx, wxq_ref[...].astype(BF16)).astype(BF16)
    for b in range(n_batch):
        rs = slice(b * t_len, (b + 1) * t_len)
        att_ref[rs, :] = _attn_heads(q[rs, :], kmem_ref[b], v_ref[b], True)
    x2 = x1 + _dot(att_ref[...], wxo_ref[...].astype(BF16))
    x2_ref[...] = x2

    h3 = _rms(x2, gffn_ref[...]).astype(BF16)
    m = n_batch * t_len
    h3_ref[0:m, :] = _pack_bf16_pairs(h3.astype(F32))
    if h3_ref.shape[0] > m:
        h3_ref[m:, :] = jnp.zeros((h3_ref.shape[0] - m, D_MODEL // 2), jnp.uint32)
    rt, new_run = _route(_dot(h3, wr_ref[...]) + br_ref[...], run_in_ref[...], lower_ref[...])
    rt_ref[...] = rt
    cnt_ref[...] = new_run


def _trunk_sample(x, hist, run, p, n_batch, t_len):
    m = n_batch * t_len
    args = [x, hist, run, p["g_mix"], p["w_in"], p["conv_w"], p["conv_b"], p["ln_conv_g"], p["ln_conv_b"],
            p["ln_v_g"], p["ln_v_b"], p["w_sg_bd"], p["b_sg_rows_s"], p["w_out"], p["g_xattn"], p["w_xq"],
            p["k_s"], p["v_s"], p["w_xo"], p["g_ffn"], p["w_router"], p["b_router"], p["lower"]]
    return pl.pallas_call(
        functools.partial(_trunk_sample_kernel, n_batch, t_len),
        out_shape=(jax.ShapeDtypeStruct((m, D_MODEL), F32),
                   jax.ShapeDtypeStruct((-(-m // (SC_WORKERS * SUBLANES)) * SC_WORKERS * SUBLANES, D_MODEL // 2),
                                        jnp.uint32),
                   jax.ShapeDtypeStruct((SUBLANES, m), F32),
                   jax.ShapeDtypeStruct((n_batch, HIST, D_CONV), F32),
                   jax.ShapeDtypeStruct((m, D_SG), F32),
                   jax.ShapeDtypeStruct((1, LOGIT_LANES), F32)),
        scratch_shapes=[pltpu.VMEM((n_batch, HIST + t_len, D_CONV), F32),
                        pltpu.VMEM((m, D_CONV), F32),
                        pltpu.VMEM((m, D_MODEL), BF16)],
        compiler_params=pltpu.CompilerParams(vmem_limit_bytes=VMEM_LIMIT),
        name="trunk_sample",
    )(*args)


def _sc_worker_id():
    return lax.axis_index("s") * SC_CORES + lax.axis_index("c")


def _sc_chunk(per_w, max_chunk):
    assert per_w % SUBLANES == 0 and max_chunk <= LANES
    return max(c for c in range(SUBLANES, max_chunk + 1, SUBLANES) if per_w % c == 0)


def _sc_scatter_rows2(tables, slots_a, slots_b, tag_bases, n_rows_out, max_chunk):
    d, dtype = tables[0].shape[1], tables[0].dtype
    plans = []
    for t in tables:
        per_w = t.shape[0] // SC_WORKERS
        assert per_w * SC_WORKERS == t.shape[0]
        chunk = _sc_chunk(per_w, max_chunk)
        plans.append((per_w, chunk, per_w // chunk))
    cmax = max(c for _, c, _ in plans)
    n_t = len(tables)
    mesh = plsc.VectorSubcoreMesh(core_axis_name="c", subcore_axis_name="s")

    nb = SCATTER_BUFS
    lag = nb // 3
    scratch = []
    for _, chunk, _ in plans:
        for _ in range(nb):
            scratch += [pltpu.VMEM((chunk,), jnp.int32), pltpu.VMEM((chunk,), jnp.int32)]
    scratch += [pltpu.VMEM((cmax, d), dtype)] * nb
    scratch += [pltpu.VMEM((cmax, TAG_WORDS), jnp.int32)] * (2 * nb)
    scratch += [pltpu.SemaphoreType.DMA] * (2 * nb)

    @functools.partial(pl.kernel, mesh=mesh,
                       out_type=(jax.ShapeDtypeStruct((n_rows_out, d), dtype),
                                 jax.ShapeDtypeStruct((n_rows_out, TAG_WORDS), jnp.int32)),
                       scratch_types=scratch)
    def scatter(*refs):
        tab_hbm = refs[0:n_t]
        sa_hbm = refs[n_t:2 * n_t]
        sb_hbm = refs[2 * n_t:3 * n_t]
        out_hbm, tag_hbm = refs[3 * n_t], refs[3 * n_t + 1]
        sc = refs[3 * n_t + 2:]
        idx_refs = sc[:2 * nb * n_t]
        rows = sc[2 * nb * n_t:2 * nb * n_t + nb]
        tagbufs = sc[2 * nb * n_t + nb:2 * nb * n_t + 3 * nb]
        lsem = sc[2 * nb * n_t + 3 * nb:2 * nb * n_t + 4 * nb]
        ssem = sc[2 * nb * n_t + 4 * nb:]
        wid = _sc_worker_id()

        work = []
        for t, (per_w, chunk, n_chunks) in enumerate(plans):
            for j in range(n_chunks):
                work.append((t, wid * per_w + j * chunk, chunk))

        def parts(k):
            t, off, chunk = work[k]
            b = k % nb
            ia, ib = idx_refs[2 * nb * t + 2 * b], idx_refs[2 * nb * t + 2 * b + 1]
            full = chunk == cmax
            rv = rows[b] if full else rows[b].at[pl.ds(0, chunk)]
            ta = tagbufs[2 * b] if full else tagbufs[2 * b].at[pl.ds(0, chunk)]
            tb = tagbufs[2 * b + 1] if full else tagbufs[2 * b + 1].at[pl.ds(0, chunk)]
            return t, off, chunk, b, ia, ib, rv, ta, tb

        def start_load(k):
            t, off, chunk, b, ia, ib, rv, ta, tb = parts(k)
            return (pltpu.async_copy(tab_hbm[t].at[pl.ds(off, chunk)], rv, lsem[b]),
                    pltpu.async_copy(sa_hbm[t].at[pl.ds(off, chunk)], ia, lsem[b]),
                    pltpu.async_copy(sb_hbm[t].at[pl.ds(off, chunk)], ib, lsem[b]))

        def start_scatter(k):
            t, off, chunk, b, ia, ib, rv, ta, tb = parts(k)
            base_a, base_b = tag_bases[t]
            for r in range(chunk):
                row_id = (off + r).astype(jnp.int32)
                tagbufs[2 * b][r, pl.ds(0, SC_LANES)] = jnp.zeros((SC_LANES,), jnp.int32) + (base_a + row_id)
                tagbufs[2 * b + 1][r, pl.ds(0, SC_LANES)] = jnp.zeros((SC_LANES,), jnp.int32) + (base_b + row_id)
            return (pltpu.async_copy(rv, out_hbm.at[ia], ssem[b]), pltpu.async_copy(rv, out_hbm.at[ib], ssem[b]),
                    pltpu.async_copy(ta, tag_hbm.at[ia], ssem[b]), pltpu.async_copy(tb, tag_hbm.at[ib], ssem[b]))

        loads, scatters = {}, {}
        for k in range(len(work) + lag):
            if k < len(work):
                if k >= nb:
                    for c in scatters.pop(k - nb):
                        c.wait()
                loads[k] = start_load(k)
            w = k - lag
            if w >= 0:
                for c in loads.pop(w):
                    c.wait()
                scatters[w] = start_scatter(w)
        for w in sorted(scatters):
            for c in scatters[w]:
                c.wait()

    return scatter(*tables, *slots_a, *slots_b)


def _sc_scatter_back(ys, dest, n_rows_out):
    n_rows, d = ys.shape
    per_w = n_rows // SC_WORKERS
    assert per_w * SC_WORKERS == n_rows
    chunk = _sc_chunk(per_w, BACK_CHUNK)
    n_chunks = per_w // chunk
    nb = BACK_BUFS
    lag = nb // 2
    mesh = plsc.VectorSubcoreMesh(core_axis_name="c", subcore_axis_name="s")

    @functools.partial(
        pl.kernel, mesh=mesh,
        out_type=jax.ShapeDtypeStruct((n_rows_out, d), ys.dtype),
        scratch_types=([pltpu.VMEM((chunk,), jnp.int32)] * nb + [pltpu.VMEM((chunk, d), ys.dtype)] * nb
                       + [pltpu.SemaphoreType.DMA] * (2 * nb)),
    )
    def scatter_back(ys_hbm, dest_hbm, out_hbm, *rest):
        idx = rest[:nb]
        rows = rest[nb:2 * nb]
        lsem = rest[2 * nb:3 * nb]
        ssem = rest[3 * nb:]
        base = _sc_worker_id() * per_w

        loads, scatters = {}, {}
        for k in range(n_chunks + lag):
            if k < n_chunks:
                b = k % nb
                if k >= nb:
                    scatters.pop(k - nb).wait()
                off = base + k * chunk
                loads[k] = (pltpu.async_copy(ys_hbm.at[pl.ds(off, chunk)], rows[b], lsem[b]),
                            pltpu.async_copy(dest_hbm.at[pl.ds(off, chunk)], idx[b], lsem[b]))
            w = k - lag
            if w >= 0:
                b = w % nb
                for c in loads.pop(w):
                    c.wait()
                scatters[w] = pltpu.async_copy(rows[b], out_hbm.at[idx[b]], ssem[b])
        for w in sorted(scatters):
            scatters[w].wait()

    return scatter_back(ys, dest)


def _experts_kernel(dump_base, first_ref, nblk_ref, cnt_ref, tot_ref, xs_hbm, tag_hbm, wg_ref, wu_ref, wd_ref,
                    ys_hbm, dest_hbm, xbuf, tbuf, ybuf, dbuf, wg_bf, wu_bf, wd_bf, in_sem, tin_sem, out_sem, dout_sem):
    e = pl.program_id(0)
    nb = nblk_ref[e]
    first = first_ref[e]
    cnt = cnt_ref[e]
    total = tot_ref[0]
    half = D_MODEL // 2

    def in_copies(gb):
        slot = lax.rem(gb, X_BUFS)
        return (pltpu.make_async_copy(xs_hbm.at[pl.ds(gb * BM, BM)], xbuf.at[slot], in_sem.at[slot]),
                pltpu.make_async_copy(tag_hbm.at[pl.ds(gb * BM, BM)], tbuf.at[slot], tin_sem.at[slot]))

    def out_copies(gb):
        slot = lax.rem(gb, Y_BUFS)
        return (pltpu.make_async_copy(ybuf.at[slot], ys_hbm.at[pl.ds(gb * BM, BM)], out_sem.at[slot]),
                pltpu.make_async_copy(dbuf.at[slot], dest_hbm.at[pl.ds(gb * SUBLANES, SUBLANES)], dout_sem.at[slot]))

    def start_in(gb):
        for c in in_copies(gb):
            c.start(priority=ROW_DMA_PRIORITY)

    def start_out(gb):
        for c in out_copies(gb):
            c.start(priority=ROW_DMA_PRIORITY)

    def wait_out(gb):
        for c in out_copies(gb):
            c.wait()

    @pl.when(nb > 0)
    def _():
        @pl.when(first == 0)
        def _():
            for k in range(X_LOOKAHEAD):
                @pl.when(k < total)
                def _():
                    start_in(k)

        wg_bf[...] = wg_ref[0].astype(BF16)
        wu_bf[...] = wu_ref[0].astype(BF16)
        wd_bf[...] = wd_ref[0].astype(BF16)

        def acquire(gb):
            @pl.when(gb + X_LOOKAHEAD < total)
            def _():
                start_in(gb + X_LOOKAHEAD)

            for c in in_copies(gb):
                c.wait()

            @pl.when(gb >= Y_BUFS)
            def _():
                wait_out(gb - Y_BUFS)

        def ffn(gb, j):
            n_live = cnt - j * BM
            tags_t = jnp.transpose(tbuf[lax.rem(gb, X_BUFS)].astype(F32))
            lane = lax.broadcasted_iota(jnp.int32, (SUBLANES, BM), 1)
            own = (dump_base + gb * BM + lane).astype(F32)
            dest = jnp.where(lane < n_live, jnp.broadcast_to(tags_t[0:1, :], (SUBLANES, BM)), own)
            dbuf[lax.rem(gb, Y_BUFS)] = dest.astype(jnp.int32)
            live = lax.broadcasted_iota(jnp.int32, (BM, half), 0) < n_live
            lo, hi = _unpack_bf16_pairs(jnp.where(live, xbuf[lax.rem(gb, X_BUFS)], jnp.uint32(0)))
            g = _dot(lo, wg_bf[0:half, :]) + _dot(hi, wg_bf[half:, :])
            u = _dot(lo, wu_bf[0:half, :]) + _dot(hi, wu_bf[half:, :])
            hm = (g * _sigmoid(g) * u).astype(BF16)
            y = _dot(hm, wd_bf[...])
            ybuf[lax.rem(gb, Y_BUFS)] = _pack_bf16_pairs(y.astype(BF16).astype(F32))

        def block_pair(jp, carry):
            j0 = 2 * jp
            g0 = first + j0
            acquire(g0)
            acquire(g0 + 1)
            ffn(g0, j0)
            ffn(g0 + 1, j0 + 1)
            start_out(g0)
            start_out(g0 + 1)
            return carry

        lax.fori_loop(0, nb // 2, block_pair, 0)

        @pl.when(lax.rem(nb, 2) == 1)
        def _():
            gl = first + nb - 1
            acquire(gl)
            ffn(gl, nb - 1)
            start_out(gl)

        @pl.when(first + nb == total)
        def _():
            for k in range(Y_BUFS):
                @pl.when(total - 1 - k >= 0)
                def _():
                    wait_out(total - 1 - k)


def _experts(xs, tags, n_rows_out, dump_base, first_block, n_blocks_e, counts, w_eg, w_eu, w_ed):
    w_map = lambda e, fb, nb, ct, tot: (e, 0, 0)
    half = D_MODEL // 2
    total = jnp.sum(n_blocks_e).astype(jnp.int32).reshape(1)
    n_blocks = n_rows_out // BM
    return pl.pallas_call(
        functools.partial(_experts_kernel, dump_base),
        grid_spec=pltpu.PrefetchScalarGridSpec(
            num_scalar_prefetch=4,
            grid=(N_EXPERTS,),
            in_specs=[pl.BlockSpec(memory_space=pl.ANY),
                      pl.BlockSpec(memory_space=pl.ANY),
                      pl.BlockSpec((1, D_MODEL, D_EXPERT), w_map),
                      pl.BlockSpec((1, D_MODEL, D_EXPERT), w_map),
                      pl.BlockSpec((1, D_EXPERT, D_MODEL), w_map)],
            out_specs=(pl.BlockSpec(memory_space=pl.ANY), pl.BlockSpec(memory_space=pl.ANY)),
            scratch_shapes=[pltpu.VMEM((X_BUFS, BM, half), jnp.uint32), pltpu.VMEM((X_BUFS, BM, TAG_WORDS), jnp.int32),
                            pltpu.VMEM((Y_BUFS, BM, half), jnp.uint32), pltpu.VMEM((Y_BUFS, SUBLANES, BM), jnp.int32),
                            pltpu.VMEM((D_MODEL, D_EXPERT), BF16), pltpu.VMEM((D_MODEL, D_EXPERT), BF16),
                            pltpu.VMEM((D_EXPERT, D_MODEL), BF16),
                            pltpu.SemaphoreType.DMA((X_BUFS,)), pltpu.SemaphoreType.DMA((X_BUFS,)),
                            pltpu.SemaphoreType.DMA((Y_BUFS,)), pltpu.SemaphoreType.DMA((Y_BUFS,))]),
        out_shape=(jax.ShapeDtypeStruct((n_rows_out, half), jnp.uint32),
                   jax.ShapeDtypeStruct((n_blocks * SUBLANES, BM), jnp.int32)),
        compiler_params=pltpu.CompilerParams(dimension_semantics=("arbitrary",), vmem_limit_bytes=VMEM_LIMIT),
        name="experts",
    )(first_block, n_blocks_e, counts, total, xs, tags, w_eg, w_eu, w_ed)


def _combine_kernel(x2_ref, y1_ref, y2_ref, rt_ref, g_ref, o_ref):
    rt = rt_ref[...]
    r = jnp.transpose(jnp.concatenate([rt, jnp.zeros((LANES - rt.shape[0], rt.shape[1]), F32)], axis=0))
    g1, g2 = r[:, 2:3], r[:, 3:4]
    half = D_MODEL // 2
    y1_lo, y1_hi = _unpack_bf16_pairs_f32(y1_ref[...])
    y2_lo, y2_hi = _unpack_bf16_pairs_f32(y2_ref[...])
    x_lo = x2_ref[:, 0:half] + g1 * y1_lo + g2 * y2_lo
    x_hi = x2_ref[:, half:] + g1 * y1_hi + g2 * y2_hi
    ms = (jnp.sum(x_lo * x_lo, axis=-1, keepdims=True) + jnp.sum(x_hi * x_hi, axis=-1, keepdims=True)) / D_MODEL
    inv = lax.rsqrt(ms + EPS)
    o_ref[:, 0:half] = x_lo * inv * g_ref[:, 0:half]
    o_ref[:, half:] = x_hi * inv * g_ref[:, half:]


def _combine(x2, yg, rt, g_final, tm, blk1, blk2):
    n = x2.shape[0]
    return pl.pallas_call(
        _combine_kernel,
        grid=(n // tm,),
        in_specs=[pl.BlockSpec((tm, D_MODEL), lambda i: (i, 0)),
                  pl.BlockSpec((tm, D_MODEL // 2), lambda i: (blk1 + i, 0)),
                  pl.BlockSpec((tm, D_MODEL // 2), lambda i: (blk2 + i, 0)),
                  pl.BlockSpec((SUBLANES, tm), lambda i: (0, i)),
                  pl.BlockSpec((1, D_MODEL), lambda i: (0, 0))],
        out_specs=pl.BlockSpec((tm, D_MODEL), lambda i: (i, 0)),
        out_shape=jax.ShapeDtypeStruct((n, D_MODEL), F32),
        compiler_params=pltpu.CompilerParams(dimension_semantics=("arbitrary",), vmem_limit_bytes=VMEM_LIMIT),
        name="combine",
    )(x2, yg, yg, rt, g_final)


def _scatter_back(ys, dest, n_rows_out):
    return _sc_scatter_back(ys, dest, n_rows_out)


def _scatter_rows2(tables, slots_a, slots_b, tag_bases, n_rows_out):
    return _sc_scatter_rows2(tables, slots_a, slots_b, tag_bases, n_rows_out, SCATTER_CHUNK)


def kernel(x_prompt, x_sample, mem_prompt, state_conv, cache_mem_k, cache_mem_v, g_mix, w_in, conv_w, conv_b, ln_conv_g, ln_conv_b, ln_v_g, ln_v_b, w_sg, b_sg, w_out, g_mem, w_mk, w_mv, g_xattn, w_xq, w_xo, g_ffn, w_router_group, b_router_group, w_router_expert, b_router_expert, w_expert_gate, w_expert_up, w_expert_down, g_final):
    assert x_prompt.shape[0] == 1 and g_mix.shape[0] == 1
    n_p = x_prompt.shape[1]
    n_batch, t_len = x_sample.shape[0], x_sample.shape[1]
    n_s = n_batch * t_len
    row = lambda a: a.reshape(1, -1)

    w_router = jnp.concatenate(
        [w_router_group[0], jnp.transpose(w_router_expert[0], (1, 0, 2)).reshape(D_MODEL, N_EXPERTS)], axis=1)
    w_router = jnp.pad(w_router, ((0, 0), (0, LOGIT_LANES - w_router.shape[1]))).astype(BF16)
    b_router = jnp.pad(jnp.concatenate([b_router_group[0], b_router_expert[0].reshape(-1)]),
                       (0, LOGIT_LANES - N_GROUPS - N_EXPERTS)).reshape(1, LOGIT_LANES)
    tril_t = jnp.tril(jnp.ones((t_len, t_len), bool))
    w_sg_t = jnp.where(tril_t, w_sg[0][:, :t_len, :t_len], 0.0)
    eye_b = jnp.eye(n_batch, dtype=F32)
    w_sg_bd = jnp.einsum("ab,hij->haibj", eye_b, w_sg_t).reshape(SG_HEADS, n_s, n_s).astype(BF16)
    p = {
        "g_mix": row(g_mix[0]), "w_in": w_in[0],
        "conv_w": jnp.pad(conv_w[0], ((0, 1), (0, 0))), "conv_b": row(conv_b[0]),
        "ln_conv_g": row(ln_conv_g[0]), "ln_conv_b": row(ln_conv_b[0]),
        "ln_v_g": row(ln_v_g[0]), "ln_v_b": row(ln_v_b[0]),
        "w_sg": w_sg[0],
        "b_sg_rows": jnp.repeat(b_sg[0].T, SG_HEAD_DIM, axis=1),
        "w_sg_bd": w_sg_bd,
        "b_sg_rows_s": jnp.tile(jnp.repeat(b_sg[0][:, :t_len].T, SG_HEAD_DIM, axis=1), (n_batch, 1)),
        "w_out": w_out[0], "g_xattn": row(g_xattn[0]),
        "w_xq": w_xq[0], "w_xo": w_xo[0], "g_ffn": row(g_ffn[0]),
        "w_router": w_router, "b_router": b_router,
        "lower": jnp.tril(jnp.ones((n_s, n_s), BF16), -1),
    }

    k_p, v_p, p["k"], p["v"] = _memkv(mem_prompt[0], row(g_mem[0]), w_mk[0], w_mv[0])
    p["k_s"] = jnp.transpose(cache_mem_k[0].astype(BF16), (0, 2, 3, 1)).reshape(n_batch, D_MODEL, N_MEM)
    p["v_s"] = cache_mem_v[0].astype(BF16).reshape(n_batch, N_MEM, D_MODEL)

    assert n_p % n_s == 0
    x2_p, h3_p, logits_p, hist_p = _trunk_prompt(x_prompt[0], p)
    rt_p, cnt_t = _router(logits_p, ROUTE_RANK_ROWS)
    cnt_p = cnt_t[:, 0].reshape(1, LOGIT_LANES)
    x2_s, h3_s, rt_s, hist_s, sgv_s, cnt = _trunk_sample(
        x_sample.reshape(n_s, D_MODEL), state_conv[0], cnt_p, p, n_batch, t_len)

    experts = jnp.arange(N_EXPERTS, dtype=jnp.int32)
    w_e = (w_expert_gate[0], w_expert_up[0], w_expert_down[0])

    def moe_pass(cnt, h3_tables, rts, n_real):
        n_tot = sum(n_real)
        n_slots = -(-(n_tot * 2) // BM) * BM + N_EXPERTS * BM
        counts = cnt[0, :N_EXPERTS].astype(jnp.int32)
        padded = (counts + BM - 1) // BM * BM
        pad_start = jnp.cumsum(padded) - padded

        def one(e_row, rank_row):
            e = e_row.astype(jnp.int32)
            start = jnp.sum(jnp.where(e[None, :] == experts[:, None], pad_start[:, None], 0), axis=0)
            return start + rank_row.astype(jnp.int32)

        slots = [(one(rt[0], rt[4]), one(rt[1], rt[5])) for rt in rts]
        sa, sb, tag_bases, spare0, dest0 = [], [], [], n_slots, 0
        for tab, (a, b), n in zip(h3_tables, slots, n_real):
            n_spare = tab.shape[0] - n
            spare = spare0 + jnp.arange(n_spare, dtype=jnp.int32)
            sa.append(jnp.concatenate([a, spare]))
            sb.append(jnp.concatenate([b, spare + n_spare]))
            spare0 += 2 * n_spare
            tag_bases.append((dest0, dest0 + n))
            dest0 += 2 * n
        xs, tags = _scatter_rows2(tuple(h3_tables), tuple(sa), tuple(sb), tuple(tag_bases), spare0)
        ys, dest_blocks = _experts(xs, tags, n_slots, dest0, pad_start // BM, padded // BM, counts, *w_e)
        slot = jnp.arange(n_slots, dtype=jnp.int32)
        dest = dest_blocks.reshape(n_slots // BM, SUBLANES, BM)[:, 0, :].reshape(-1)
        dest = jnp.where(slot < jnp.sum(padded), dest, dest0 + slot)
        return _scatter_back(ys, dest, dest0 + n_slots)

    yg = moe_pass(cnt, [h3_p, h3_s], [rt_p, rt_s], [n_p, n_s])

    gf = row(g_final)
    y_p = _combine(x2_p, yg, rt_p, gf, TM_COMBINE, 0, n_p // TM_COMBINE)
    y_s = _combine(x2_s, yg, rt_s, gf, n_s, 2 * n_p // n_s, 2 * n_p // n_s + 1)

    return (y_p.reshape(1, n_p, D_MODEL),
            y_s.reshape(n_batch, t_len, D_MODEL),
            hist_p[HALO - HIST:].reshape(1, 1, HIST, D_CONV),
            hist_s.reshape(1, n_batch, HIST, D_CONV),
            k_p.reshape(1, 1, N_MEM, X_HEADS, X_HEAD_DIM),
            v_p.reshape(1, 1, N_MEM, X_HEADS, X_HEAD_DIM),
            sgv_s.reshape(1, n_batch, t_len, D_SG))
```

```python
import functools

import jax
import jax.numpy as jnp
from jax import lax
from jax.experimental import pallas as pl
from jax.experimental.pallas import tpu as pltpu
from jax.experimental.pallas import tpu_sc as plsc

D_MODEL = 1024
D_CONV = 512
D_SG = 512
CONV_WIDTH = 31
HIST = CONV_WIDTH - 1
SG_HEADS = 4
SG_HEAD_DIM = 128
SG_CHUNK = 128
N_MEM = 256
X_HEADS = 4
X_HEAD_DIM = 256
N_GROUPS = 4
EXPERTS_PER_GROUP = 8
N_EXPERTS = 32
D_EXPERT = 512
EPS = 1e-6

LANES = 128
SUBLANES = 8
SC_CORES = 2
SC_SUBCORES = 16
SC_WORKERS = SC_CORES * SC_SUBCORES
SC_LANES = 16
VMEM_LIMIT = 56 * 1024 * 1024

TM = 1024
TM_COMBINE = 2048
ROUTE_ROWS = 4096
ROUTE_RANK_ROWS = 512
HALO = 32
SEG = TM // SUBLANES
SEG_HALO = 32
CONV_BLOCK = 16
CAST_ROWS = 64
STAGE_ROWS = 256
BM = 256
X_LOOKAHEAD = 4
X_BUFS = X_LOOKAHEAD + 2
Y_BUFS = 4
ROW_DMA_PRIORITY = 1
SCATTER_CHUNK = 32
SCATTER_BUFS = 5
TAG_WORDS = 128
BACK_CHUNK = 56
BACK_BUFS = 4
LOGIT_LANES = 128

F32 = jnp.float32
BF16 = jnp.bfloat16


def _dot(a, b):
    return jnp.dot(a, b, preferred_element_type=F32)


def _rms(x, g):
    return x * lax.rsqrt(jnp.mean(x * x, axis=-1, keepdims=True) + EPS) * g


def _ln(x, g, b):
    mu = jnp.mean(x, axis=-1, keepdims=True)
    xc = x - mu
    var = jnp.mean(xc * xc, axis=-1, keepdims=True)
    return xc * lax.rsqrt(var + EPS) * g + b


def _sigmoid(x):
    return 1.0 / (1.0 + jnp.exp(-x))


def _pack_bf16_pairs(h):
    bits = lax.bitcast_convert_type(h, jnp.uint32)
    half = h.shape[1] // 2
    lo = lax.shift_right_logical(bits[:, :half], jnp.uint32(16))
    hi = bits[:, half:] & jnp.uint32(0xFFFF0000)
    return hi | lo


def _unpack_bf16_pairs_f32(p):
    lo = lax.bitcast_convert_type(lax.shift_left(p, jnp.uint32(16)), F32)
    hi = lax.bitcast_convert_type(p & jnp.uint32(0xFFFF0000), F32)
    return lo, hi


def _unpack_bf16_pairs(p):
    lo, hi = _unpack_bf16_pairs_f32(p)
    return lo.astype(BF16), hi.astype(BF16)


def _memkv_kernel(mem_ref, g_ref, wk_ref, wv_ref, k_ref, v_ref, kbf_ref, vbf_ref):
    m = _rms(mem_ref[...], g_ref[...]).astype(BF16)
    k = _dot(m, wk_ref[...].astype(BF16))
    v = _dot(m, wv_ref[...].astype(BF16))
    k_ref[...] = k
    v_ref[...] = v
    kbf_ref[...] = k.astype(BF16)
    vbf_ref[...] = v.astype(BF16)


def _memkv(mem, g_mem, w_mk, w_mv):
    return pl.pallas_call(
        _memkv_kernel,
        out_shape=(jax.ShapeDtypeStruct((N_MEM, D_MODEL), F32), jax.ShapeDtypeStruct((N_MEM, D_MODEL), F32),
                   jax.ShapeDtypeStruct((N_MEM, D_MODEL), BF16), jax.ShapeDtypeStruct((N_MEM, D_MODEL), BF16)),
        compiler_params=pltpu.CompilerParams(vmem_limit_bytes=VMEM_LIMIT),
        name="memkv",
    )(mem, g_mem, w_mk, w_mv)


def _attn_heads(q, k, v, k_transposed):
    outs = []
    for h in range(X_HEADS):
        sl = slice(h * X_HEAD_DIM, (h + 1) * X_HEAD_DIM)
        if k_transposed:
            s = _dot(q[:, sl], k[sl, :])
        else:
            s = lax.dot_general(q[:, sl], k[:, sl], (((1,), (1,)), ((), ())), preferred_element_type=F32)
        s = s * (X_HEAD_DIM ** -0.5)
        s = s - jnp.max(s, axis=-1, keepdims=True)
        p = jnp.exp(s)
        p = p / jnp.sum(p, axis=-1, keepdims=True)
        outs.append(_dot(p.astype(BF16), v[:, sl]).astype(BF16))
    return jnp.concatenate(outs, axis=1)


def _route(logits, run, strict_lower):
    m = logits.shape[0]
    r = strict_lower.shape[0]
    lane = lax.broadcasted_iota(jnp.int32, (m, LOGIT_LANES), 1).astype(F32)
    neg = jnp.float32(-jnp.inf)
    big = jnp.float32(LOGIT_LANES)

    def first_argmax(vals):
        mx = jnp.max(vals, axis=-1, keepdims=True)
        idx = jnp.min(jnp.where(vals == mx, lane, big), axis=-1, keepdims=True)
        return mx, idx

    lg = jnp.where(lane < N_GROUPS, logits, neg)
    g_max, g_idx = first_argmax(lg)
    g_w = 1.0 / jnp.sum(jnp.exp(lg - g_max), axis=-1, keepdims=True)

    lo = N_GROUPS + g_idx * EXPERTS_PER_GROUP
    le = jnp.where((lane >= lo) & (lane < lo + EXPERTS_PER_GROUP), logits, neg)
    v1, i1 = first_argmax(le)
    v2, i2 = first_argmax(jnp.where(lane == i1, neg, le))
    t = jnp.exp(v2 - v1)
    gate1 = g_w / (1.0 + t)
    gate2 = g_w * t / (1.0 + t)
    e1 = i1 - N_GROUPS
    e2 = i2 - N_GROUPS

    oh1 = (lane == e1).astype(F32)
    oh2 = (lane == e2).astype(F32)
    oh = oh1 + oh2
    befores = []
    for r0 in range(0, m, r):
        oh_r = oh[r0:r0 + r, :]
        befores.append(_dot(strict_lower, oh_r.astype(BF16)) + run)
        run = run + jnp.sum(oh_r, axis=0, keepdims=True)
    before = befores[0] if len(befores) == 1 else jnp.concatenate(befores, axis=0)
    rank1 = jnp.sum(before * oh1, axis=-1, keepdims=True)
    rank2 = jnp.sum(before * oh2, axis=-1, keepdims=True)
    new_run = run

    rinfo = jnp.where(lane == 0, e1,
            jnp.where(lane == 1, e2,
            jnp.where(lane == 2, gate1,
            jnp.where(lane == 3, gate2,
            jnp.where(lane == 4, rank1,
            jnp.where(lane == 5, rank2, 0.0))))))
    return jnp.transpose(rinfo)[0:SUBLANES, :], new_run


def _conv_segments(a, w_ref, seg_ref, tail_ref, yseg_ref, conv_ref):
    sub = lax.broadcasted_iota(jnp.int32, (SUBLANES, LANES), 0)
    for lt in range(D_CONV // LANES):
        ls = slice(lt * LANES, (lt + 1) * LANES)
        for t0 in range(0, TM, SUBLANES):
            s, m = divmod(t0, SEG)
            seg_ref[lt, pl.ds((SEG_HALO + m) * SUBLANES + s, SUBLANES, stride=SUBLANES), :] = a[t0:t0 + SUBLANES, ls]
        for j in range(SEG_HALO):
            cur = seg_ref[lt, (SEG + j) * SUBLANES:(SEG + j + 1) * SUBLANES, :]
            prev = tail_ref[lt, j * SUBLANES:(j + 1) * SUBLANES, :]
            seg_ref[lt, j * SUBLANES:(j + 1) * SUBLANES, :] = jnp.where(
                sub == 0, pltpu.roll(prev, 1, axis=0), pltpu.roll(cur, 1, axis=0))
            tail_ref[lt, j * SUBLANES:(j + 1) * SUBLANES, :] = cur
        for m0 in range(0, SEG, CONV_BLOCK):
            acc = [jnp.zeros((SUBLANES, LANES), F32) for _ in range(CONV_BLOCK)]
            for idx in range(m0 - HIST, m0 + CONV_BLOCK):
                b = seg_ref[lt, (SEG_HALO + idx) * SUBLANES:(SEG_HALO + idx + 1) * SUBLANES, :]
                for m in range(max(m0, idx), min(m0 + CONV_BLOCK, idx + CONV_WIDTH)):
                    k = idx - m + HIST
                    acc[m - m0] = acc[m - m0] + b * w_ref[k:k + 1, ls]
            for m in range(m0, m0 + CONV_BLOCK):
                yseg_ref[lt, m * SUBLANES:(m + 1) * SUBLANES, :] = acc[m - m0]
        for t0 in range(0, TM, SUBLANES):
            s, m = divmod(t0, SEG)
            conv_ref[t0:t0 + SUBLANES, ls] = yseg_ref[lt, pl.ds(m * SUBLANES + s, SUBLANES, stride=SUBLANES), :]


def _stage_cast(pairs, stage_ref, sem):
    work = [(src, dst, r0) for src, dst in pairs for r0 in range(0, src.shape[0], STAGE_ROWS)]

    def copy(k):
        src, _, r0 = work[k]
        return pltpu.make_async_copy(src.at[pl.ds(r0, STAGE_ROWS)],
                                     stage_ref.at[k % 2, :, pl.ds(0, src.shape[1])], sem.at[k % 2])

    copy(0).start()
    for k, (src, dst, r0) in enumerate(work):
        if k + 1 < len(work):
            copy(k + 1).start()
        copy(k).wait()

        def body(c, carry, k=k, src=src, dst=dst, r0=r0):
            c0 = pl.multiple_of(c * CAST_ROWS, CAST_ROWS)
            dst[pl.ds(r0 + c0, CAST_ROWS), :] = stage_ref[k % 2, pl.ds(c0, CAST_ROWS), 0:src.shape[1]].astype(BF16)
            return carry

        lax.fori_loop(0, STAGE_ROWS // CAST_ROWS, body, 0)


def _trunk_prompt_kernel(x_ref, gmix_ref, win32_ref, convw_ref, convb_ref, lncg_ref, lncb_ref, lnvg_ref, lnvb_ref,
                         wsg_ref, bsg_ref, wout32_ref, gx_ref, wxq32_ref, kmem_ref, v_ref, wxo32_ref, gffn_ref, wr_ref,
                         br_ref,
                         x2_ref, h3_ref, logit_ref, hist_ref,
                         seg_ref, tail_ref, yseg_ref, conv_ref, win_ref, wout_ref, wxq_ref, wxo_ref, stage_ref, stage_sem):
    i = pl.program_id(0)

    @pl.when(i == 0)
    def _():
        tail_ref[...] = jnp.zeros(tail_ref.shape, F32)
        _stage_cast([(win32_ref, win_ref), (wout32_ref, wout_ref), (wxq32_ref, wxq_ref), (wxo32_ref, wxo_ref)],
                    stage_ref, stage_sem)

    x = x_ref[...]
    h = _rms(x, gmix_ref[...]).astype(BF16)

    a_in = _dot(h, win_ref[:, 0:D_CONV])
    a_gate = _dot(h, win_ref[:, D_CONV:2 * D_CONV])
    a = a_in * _sigmoid(a_gate)
    hist_ref[...] = a[TM - HALO:, :]
    _conv_segments(a, convw_ref, seg_ref, tail_ref, yseg_ref, conv_ref)

    y = _ln(conv_ref[...] + convb_ref[...], lncg_ref[...], lncb_ref[...])
    a_out = (y * _sigmoid(y)).astype(BF16)

    u = _dot(h, win_ref[:, 2 * D_CONV:2 * D_CONV + D_SG])
    v = _ln(_dot(h, win_ref[:, 2 * D_CONV + D_SG:]), lnvg_ref[...], lnvb_ref[...]).astype(BF16)
    ri = lax.broadcasted_iota(jnp.int32, (SG_CHUNK, SG_CHUNK), 0)
    ci = lax.broadcasted_iota(jnp.int32, (SG_CHUNK, SG_CHUNK), 1)
    w_tril = [jnp.where(ci <= ri, wsg_ref[hh], 0.0).astype(BF16) for hh in range(SG_HEADS)]
    gate_rows = []
    for c in range(TM // SG_CHUNK):
        rs = slice(c * SG_CHUNK, (c + 1) * SG_CHUNK)
        heads = [_dot(w_tril[hh], v[rs, hh * SG_HEAD_DIM:(hh + 1) * SG_HEAD_DIM]) for hh in range(SG_HEADS)]
        gate_rows.append(jnp.concatenate(heads, axis=1) + bsg_ref[...])
    b_out = (u * jnp.concatenate(gate_rows, axis=0)).astype(BF16)

    x1 = x + _dot(a_out, wout_ref[0:D_CONV, :]) + _dot(b_out, wout_ref[D_CONV:, :])

    hx = _rms(x1, gx_ref[...]).astype(BF16)
    q = _dot(hx, wxq_ref[...]).astype(BF16)
    x2 = x1 + _dot(_attn_heads(q, kmem_ref[...], v_ref[...], False), wxo_ref[...])
    x2_ref[...] = x2

    h3 = _rms(x2, gffn_ref[...]).astype(BF16)
    h3_ref[...] = _pack_bf16_pairs(h3.astype(F32))
    logit_ref[...] = _dot(h3, wr_ref[...]) + br_ref[...]


def _router_kernel(logit_ref, upper_ref, ones_ref, rt_ref, cnt_ref, run_ref):
    @pl.when(pl.program_id(0) == 0)
    def _():
        run_ref[...] = jnp.zeros(run_ref.shape, F32)

    n = ROUTE_ROWS
    lt = jnp.transpose(logit_ref[...])
    neg = jnp.float32(-jnp.inf)
    big = jnp.float32(LOGIT_LANES)

    def first_argmax(vals, rows):
        mx = jnp.max(vals, axis=0, keepdims=True)
        idx = jnp.min(jnp.where(vals == mx, rows, big), axis=0, keepdims=True)
        return mx, idx

    row8 = lax.broadcasted_iota(jnp.int32, (SUBLANES, n), 0).astype(F32)
    lg = jnp.where(row8 < N_GROUPS, lt[0:SUBLANES, :], neg)
    g_max, g_idx = first_argmax(lg, row8)
    g_w = 1.0 / jnp.sum(jnp.exp(lg - g_max), axis=0, keepdims=True)

    n_rows = N_GROUPS + N_EXPERTS + (-(N_GROUPS + N_EXPERTS)) % SUBLANES
    rows = lax.broadcasted_iota(jnp.int32, (n_rows, n), 0).astype(F32)
    lo = N_GROUPS + g_idx * EXPERTS_PER_GROUP
    le = jnp.where((rows >= lo) & (rows < lo + EXPERTS_PER_GROUP), lt[0:n_rows, :], neg)
    v1, i1 = first_argmax(le, rows)
    v2, i2 = first_argmax(jnp.where(rows == i1, neg, le), rows)
    t = jnp.exp(v2 - v1)
    gate1 = g_w / (1.0 + t)
    gate2 = g_w * t / (1.0 + t)
    e1 = i1 - N_GROUPS
    e2 = i2 - N_GROUPS

    erow = lax.broadcasted_iota(jnp.int32, (LOGIT_LANES, n), 0).astype(F32)
    oh1 = (erow == e1).astype(F32)
    oh2 = (erow == e2).astype(F32)
    oh = (oh1 + oh2).astype(BF16)
    r = upper_ref.shape[0]
    run = run_ref[...]
    rank1, rank2 = [], []
    for c0 in range(0, n, r):
        cs = slice(c0, c0 + r)
        before = _dot(oh[:, cs], upper_ref[...]) + run
        rank1.append(jnp.sum(before * oh1[:, cs], axis=0, keepdims=True))
        rank2.append(jnp.sum(before * oh2[:, cs], axis=0, keepdims=True))
        run = run + _dot(oh[:, cs], ones_ref[...])
    run_ref[...] = run
    cnt_ref[...] = run[:, 0:LANES]

    sub = lax.broadcasted_iota(jnp.int32, (SUBLANES, n), 0)
    vals = (e1, e2, gate1, gate2, jnp.concatenate(rank1, axis=1), jnp.concatenate(rank2, axis=1))
    rt = jnp.zeros((SUBLANES, n), F32)
    for k, v in enumerate(vals):
        rt = jnp.where(sub == k, v, rt)
    rt_ref[...] = rt


def _router(logits, rank_block):
    n = logits.shape[0]
    assert n % ROUTE_ROWS == 0 and ROUTE_ROWS % rank_block == 0
    upper = jnp.triu(jnp.ones((rank_block, rank_block), BF16), 1)
    ones = jnp.ones((rank_block, rank_block), BF16)
    return pl.pallas_call(
        _router_kernel,
        grid=(n // ROUTE_ROWS,),
        in_specs=[pl.BlockSpec((ROUTE_ROWS, LOGIT_LANES), lambda i: (i, 0)),
                  pl.BlockSpec(upper.shape, lambda i: (0, 0)),
                  pl.BlockSpec(ones.shape, lambda i: (0, 0))],
        out_specs=(pl.BlockSpec((SUBLANES, ROUTE_ROWS), lambda i: (0, i)),
                   pl.BlockSpec((LOGIT_LANES, LANES), lambda i: (0, 0))),
        out_shape=(jax.ShapeDtypeStruct((SUBLANES, n), F32), jax.ShapeDtypeStruct((LOGIT_LANES, LANES), F32)),
        scratch_shapes=[pltpu.VMEM((LOGIT_LANES, rank_block), F32)],
        compiler_params=pltpu.CompilerParams(dimension_semantics=("arbitrary",), vmem_limit_bytes=VMEM_LIMIT),
        name="router",
    )(logits, upper, ones)


def _const_spec(shape):
    nd = len(shape)
    return pl.BlockSpec(shape, lambda i: (0,) * nd, pipeline_mode=pl.Buffered(1))


def _trunk_prompt(x, p):
    n = x.shape[0]
    assert n % TM == 0
    row = lambda w: pl.BlockSpec((TM, w), lambda i: (i, 0))
    consts = [p["g_mix"], p["w_in"], p["conv_w"], p["conv_b"], p["ln_conv_g"], p["ln_conv_b"], p["ln_v_g"],
              p["ln_v_b"], p["w_sg"], p["b_sg_rows"], p["w_out"], p["g_xattn"], p["w_xq"], p["k"], p["v"],
              p["w_xo"], p["g_ffn"], p["w_router"], p["b_router"]]
    staged = (p["w_in"], p["w_out"], p["w_xq"], p["w_xo"])
    spec = lambda c: pl.BlockSpec(memory_space=pl.ANY) if any(c is s for s in staged) else _const_spec(c.shape)
    return pl.pallas_call(
        _trunk_prompt_kernel,
        grid=(n // TM,),
        in_specs=[row(D_MODEL)] + [spec(c) for c in consts],
        out_specs=(row(D_MODEL), row(D_MODEL // 2), row(LOGIT_LANES),
                   pl.BlockSpec((HALO, D_CONV), lambda i: (0, 0))),
        out_shape=(jax.ShapeDtypeStruct((n, D_MODEL), F32),
                   jax.ShapeDtypeStruct((n, D_MODEL // 2), jnp.uint32),
                   jax.ShapeDtypeStruct((n, LOGIT_LANES), F32),
                   jax.ShapeDtypeStruct((HALO, D_CONV), F32)),
        scratch_shapes=[pltpu.VMEM((D_CONV // LANES, (SEG_HALO + SEG) * SUBLANES, LANES), F32),
                        pltpu.VMEM((D_CONV // LANES, SEG_HALO * SUBLANES, LANES), F32),
                        pltpu.VMEM((D_CONV // LANES, TM, LANES), F32),
                        pltpu.VMEM((TM, D_CONV), F32),
                        pltpu.VMEM(p["w_in"].shape, BF16), pltpu.VMEM(p["w_out"].shape, BF16),
                        pltpu.VMEM(p["w_xq"].shape, BF16), pltpu.VMEM(p["w_xo"].shape, BF16),
                        pltpu.VMEM((2, STAGE_ROWS, max(s.shape[1] for s in staged)), F32),
                        pltpu.SemaphoreType.DMA((2,))],
        compiler_params=pltpu.CompilerParams(dimension_semantics=("arbitrary",), vmem_limit_bytes=VMEM_LIMIT),
        name="trunk_prompt",
    )(x, *consts)


def _trunk_sample_kernel(n_batch, t_len,
                         x_ref, hist_in_ref, run_in_ref, gmix_ref, win_ref, convw_ref, convb_ref, lncg_ref, lncb_ref,
                         lnvg_ref, lnvb_ref, wsgbd_ref, bsg_ref, wout_ref, gx_ref, wxq_ref, kmem_ref, v_ref, wxo_ref,
                         gffn_ref, wr_ref, br_ref, lower_ref,
                         x2_ref, h3_ref, rt_ref, hist_ref, sgv_ref, cnt_ref,
                         ext_ref, conv_ref, att_ref):
    x = x_ref[...]
    h = _rms(x, gmix_ref[...]).astype(BF16)
    z = _dot(h, win_ref[...].astype(BF16))
    a = z[:, 0:D_CONV] * _sigmoid(z[:, D_CONV:2 * D_CONV])
    ext_len = HIST + t_len
    for b in range(n_batch):
        ext_ref[b, 0:HIST, :] = hist_in_ref[b]
        ext_ref[b, HIST:ext_len, :] = a[b * t_len:(b + 1) * t_len, :]
    for b in range(n_batch):
        acc = jnp.zeros((t_len, D_CONV), F32)
        for k in range(CONV_WIDTH):
            acc = acc + ext_ref[b, k:k + t_len, :] * convw_ref[k:k + 1, :]
        conv_ref[b * t_len:(b + 1) * t_len, :] = acc
        hist_ref[b] = ext_ref[b, ext_len - HIST:ext_len, :]

    y = _ln(conv_ref[...] + convb_ref[...], lncg_ref[...], lncb_ref[...])
    a_out = (y * _sigmoid(y)).astype(BF16)

    u = z[:, 2 * D_CONV:2 * D_CONV + D_SG]
    v = _ln(z[:, 2 * D_CONV + D_SG:], lnvg_ref[...], lnvb_ref[...])
    sgv_ref[...] = v
    vb = v.astype(BF16)
    heads = [_dot(wsgbd_ref[hh], vb[:, hh * SG_HEAD_DIM:(hh + 1) * SG_HEAD_DIM]) for hh in range(SG_HEADS)]
    b_out = (u * (jnp.concatenate(heads, axis=1) + bsg_ref[...])).astype(BF16)

    x1 = (x + _dot(a_out, wout_ref[0:D_CONV, :].astype(BF16))
          + _dot(b_out, wout_ref[D_CONV:, :].astype(BF16)))

    hx = _rms(x1, gx_ref[...]).astype(BF16)
    q = _dot(hx, wxq_ref[...].astype(BF16)).astype(BF16)
    for b in range(n_batch):
        rs = slice(b * t_len, (b + 1) * t_len)
        att_ref[rs, :] = _attn_heads(q[rs, :], kmem_ref[b], v_ref[b], True)
    x2 = x1 + _dot(att_ref[...], wxo_ref[...].astype(BF16))
    x2_ref[...] = x2

    h3 = _rms(x2, gffn_ref[...]).astype(BF16)
    m = n_batch * t_len
    h3_ref[0:m, :] = _pack_bf16_pairs(h3.astype(F32))
    if h3_ref.shape[0] > m:
        h3_ref[m:, :] = jnp.zeros((h3_ref.shape[0] - m, D_MODEL // 2), jnp.uint32)
    rt, new_run = _route(_dot(h3, wr_ref[...]) + br_ref[...], run_in_ref[...], lower_ref[...])
    rt_ref[...] = rt
    cnt_ref[...] = new_run


def _trunk_sample(x, hist, run, p, n_batch, t_len):
    m = n_batch * t_len
    args = [x, hist, run, p["g_mix"], p["w_in"], p["conv_w"], p["conv_b"], p["ln_conv_g"], p["ln_conv_b"],
            p["ln_v_g"], p["ln_v_b"], p["w_sg_bd"], p["b_sg_rows_s"], p["w_out"], p["g_xattn"], p["w_xq"],
            p["k_s"], p["v_s"], p["w_xo"], p["g_ffn"], p["w_router"], p["b_router"], p["lower"]]
    return pl.pallas_call(
        functools.partial(_trunk_sample_kernel, n_batch, t_len),
        out_shape=(jax.ShapeDtypeStruct((m, D_MODEL), F32),
                   jax.ShapeDtypeStruct((-(-m // (SC_WORKERS * SUBLANES)) * SC_WORKERS * SUBLANES, D_MODEL // 2),
                                        jnp.uint32),
                   jax.ShapeDtypeStruct((SUBLANES, m), F32),
                   jax.ShapeDtypeStruct((n_batch, HIST, D_CONV), F32),
                   jax.ShapeDtypeStruct((m, D_SG), F32),
                   jax.ShapeDtypeStruct((1, LOGIT_LANES), F32)),
        scratch_shapes=[pltpu.VMEM((n_batch, HIST + t_len, D_CONV), F32),
                        pltpu.VMEM((m, D_CONV), F32),
                        pltpu.VMEM((m, D_MODEL), BF16)],
        compiler_params=pltpu.CompilerParams(vmem_limit_bytes=VMEM_LIMIT),
        name="trunk_sample",
    )(*args)


def _sc_worker_id():
    return lax.axis_index("s") * SC_CORES + lax.axis_index("c")


def _sc_chunk(per_w, max_chunk):
    assert per_w % SUBLANES == 0 and max_chunk <= LANES
    return max(c for c in range(SUBLANES, max_chunk + 1, SUBLANES) if per_w % c == 0)


def _sc_scatter_rows2(tables, slots_a, slots_b, tag_bases, n_rows_out, max_chunk):
    d, dtype = tables[0].shape[1], tables[0].dtype
    plans = []
    for t in tables:
        per_w = t.shape[0] // SC_WORKERS
        assert per_w * SC_WORKERS == t.shape[0]
        chunk = _sc_chunk(per_w, max_chunk)
        plans.append((per_w, chunk, per_w // chunk))
    cmax = max(c for _, c, _ in plans)
    n_t = len(tables)
    mesh = plsc.VectorSubcoreMesh(core_axis_name="c", subcore_axis_name="s")

    nb = SCATTER_BUFS
    lag = 2
    scratch = []
    for _, chunk, _ in plans:
        for _ in range(nb):
            scratch += [pltpu.VMEM((chunk,), jnp.int32), pltpu.VMEM((chunk,), jnp.int32)]
    scratch += [pltpu.VMEM((cmax, d), dtype)] * nb
    scratch += [pltpu.VMEM((cmax, TAG_WORDS), jnp.int32)] * (2 * nb)
    scratch += [pltpu.SemaphoreType.DMA] * (2 * nb)

    @functools.partial(pl.kernel, mesh=mesh,
                       out_type=(jax.ShapeDtypeStruct((n_rows_out, d), dtype),
                                 jax.ShapeDtypeStruct((n_rows_out, TAG_WORDS), jnp.int32)),
                       scratch_types=scratch)
    def scatter(*refs):
        tab_hbm = refs[0:n_t]
        sa_hbm = refs[n_t:2 * n_t]
        sb_hbm = refs[2 * n_t:3 * n_t]
        out_hbm, tag_hbm = refs[3 * n_t], refs[3 * n_t + 1]
        sc = refs[3 * n_t + 2:]
        idx_refs = sc[:2 * nb * n_t]
        rows = sc[2 * nb * n_t:2 * nb * n_t + nb]
        tagbufs = sc[2 * nb * n_t + nb:2 * nb * n_t + 3 * nb]
        lsem = sc[2 * nb * n_t + 3 * nb:2 * nb * n_t + 4 * nb]
        ssem = sc[2 * nb * n_t + 4 * nb:]
        wid = _sc_worker_id()

        work = []
        for t, (per_w, chunk, n_chunks) in enumerate(plans):
            for j in range(n_chunks):
                work.append((t, wid * per_w + j * chunk, chunk))

        def parts(k):
            t, off, chunk = work[k]
            b = k % nb
            ia, ib = idx_refs[2 * nb * t + 2 * b], idx_refs[2 * nb * t + 2 * b + 1]
            full = chunk == cmax
            rv = rows[b] if full else rows[b].at[pl.ds(0, chunk)]
            ta = tagbufs[2 * b] if full else tagbufs[2 * b].at[pl.ds(0, chunk)]
            tb = tagbufs[2 * b + 1] if full else tagbufs[2 * b + 1].at[pl.ds(0, chunk)]
            return t, off, chunk, b, ia, ib, rv, ta, tb

        def start_load(k):
            t, off, chunk, b, ia, ib, rv, ta, tb = parts(k)
            return (pltpu.async_copy(tab_hbm[t].at[pl.ds(off, chunk)], rv, lsem[b]),
                    pltpu.async_copy(sa_hbm[t].at[pl.ds(off, chunk)], ia, lsem[b]),
                    pltpu.async_copy(sb_hbm[t].at[pl.ds(off, chunk)], ib, lsem[b]))

        def start_scatter(k):
            t, off, chunk, b, ia, ib, rv, ta, tb = parts(k)
            base_a, base_b = tag_bases[t]
            for r in range(chunk):
                row_id = (off + r).astype(jnp.int32)
                tagbufs[2 * b][r, pl.ds(0, SC_LANES)] = jnp.zeros((SC_LANES,), jnp.int32) + (base_a + row_id)
                tagbufs[2 * b + 1][r, pl.ds(0, SC_LANES)] = jnp.zeros((SC_LANES,), jnp.int32) + (base_b + row_id)
            return (pltpu.async_copy(rv, out_hbm.at[ia], ssem[b]), pltpu.async_copy(rv, out_hbm.at[ib], ssem[b]),
                    pltpu.async_copy(ta, tag_hbm.at[ia], ssem[b]), pltpu.async_copy(tb, tag_hbm.at[ib], ssem[b]))

        loads, scatters = {}, {}
        for k in range(len(work) + lag):
            if k < len(work):
                if k >= nb:
                    for c in scatters.pop(k - nb):
                        c.wait()
                loads[k] = start_load(k)
            w = k - lag
            if w >= 0:
                for c in loads.pop(w):
                    c.wait()
                scatters[w] = start_scatter(w)
        for w in sorted(scatters):
            for c in scatters[w]:
                c.wait()

    return scatter(*tables, *slots_a, *slots_b)


def _sc_scatter_back(ys, dest, n_rows_out):
    n_rows, d = ys.shape
    per_w = n_rows // SC_WORKERS
    assert per_w * SC_WORKERS == n_rows
    chunk = _sc_chunk(per_w, BACK_CHUNK)
    n_chunks = per_w // chunk
    nb = BACK_BUFS
    lag = nb // 2
    mesh = plsc.VectorSubcoreMesh(core_axis_name="c", subcore_axis_name="s")

    @functools.partial(
        pl.kernel, mesh=mesh,
        out_type=jax.ShapeDtypeStruct((n_rows_out, d), ys.dtype),
        scratch_types=([pltpu.VMEM((chunk,), jnp.int32)] * nb + [pltpu.VMEM((chunk, d), ys.dtype)] * nb
                       + [pltpu.SemaphoreType.DMA] * (2 * nb)),
    )
    def scatter_back(ys_hbm, dest_hbm, out_hbm, *rest):
        idx = rest[:nb]
        rows = rest[nb:2 * nb]
        lsem = rest[2 * nb:3 * nb]
        ssem = rest[3 * nb:]
        base = _sc_worker_id() * per_w

        loads, scatters = {}, {}
        for k in range(n_chunks + lag):
            if k < n_chunks:
                b = k % nb
                if k >= nb:
                    scatters.pop(k - nb).wait()
                off = base + k * chunk
                loads[k] = (pltpu.async_copy(ys_hbm.at[pl.ds(off, chunk)], rows[b], lsem[b]),
                            pltpu.async_copy(dest_hbm.at[pl.ds(off, chunk)], idx[b], lsem[b]))
            w = k - lag
            if w >= 0:
                b = w % nb
                for c in loads.pop(w):
                    c.wait()
                scatters[w] = pltpu.async_copy(rows[b], out_hbm.at[idx[b]], ssem[b])
        for w in sorted(scatters):
            scatters[w].wait()

    return scatter_back(ys, dest)


def _experts_kernel(dump_base, first_ref, nblk_ref, cnt_ref, tot_ref, xs_hbm, tag_hbm, wg_ref, wu_ref, wd_ref,
                    ys_hbm, dest_hbm, xbuf, tbuf, ybuf, dbuf, wg_bf, wu_bf, wd_bf, in_sem, tin_sem, out_sem, dout_sem):
    e = pl.program_id(0)
    nb = nblk_ref[e]
    first = first_ref[e]
    cnt = cnt_ref[e]
    total = tot_ref[0]
    half = D_MODEL // 2

    def in_copies(gb):
        slot = lax.rem(gb, X_BUFS)
        return (pltpu.make_async_copy(xs_hbm.at[pl.ds(gb * BM, BM)], xbuf.at[slot], in_sem.at[slot]),
                pltpu.make_async_copy(tag_hbm.at[pl.ds(gb * BM, BM)], tbuf.at[slot], tin_sem.at[slot]))

    def out_copies(gb):
        slot = lax.rem(gb, Y_BUFS)
        return (pltpu.make_async_copy(ybuf.at[slot], ys_hbm.at[pl.ds(gb * BM, BM)], out_sem.at[slot]),
                pltpu.make_async_copy(dbuf.at[slot], dest_hbm.at[pl.ds(gb * SUBLANES, SUBLANES)], dout_sem.at[slot]))

    def start_in(gb):
        for c in in_copies(gb):
            c.start(priority=ROW_DMA_PRIORITY)

    def start_out(gb):
        for c in out_copies(gb):
            c.start(priority=ROW_DMA_PRIORITY)

    def wait_out(gb):
        for c in out_copies(gb):
            c.wait()

    @pl.when(nb > 0)
    def _():
        @pl.when(first == 0)
        def _():
            for k in range(X_LOOKAHEAD):
                @pl.when(k < total)
                def _():
                    start_in(k)

        wg_bf[...] = wg_ref[0].astype(BF16)
        wu_bf[...] = wu_ref[0].astype(BF16)
        wd_bf[...] = wd_ref[0].astype(BF16)

        def acquire(gb):
            @pl.when(gb + X_LOOKAHEAD < total)
            def _():
                start_in(gb + X_LOOKAHEAD)

            for c in in_copies(gb):
                c.wait()

            @pl.when(gb >= Y_BUFS)
            def _():
                wait_out(gb - Y_BUFS)

        def ffn(gb, j):
            n_live = cnt - j * BM
            tags_t = jnp.transpose(tbuf[lax.rem(gb, X_BUFS)].astype(F32))
            lane = lax.broadcasted_iota(jnp.int32, (SUBLANES, BM), 1)
            own = (dump_base + gb * BM + lane).astype(F32)
            dest = jnp.where(lane < n_live, jnp.broadcast_to(tags_t[0:1, :], (SUBLANES, BM)), own)
            dbuf[lax.rem(gb, Y_BUFS)] = dest.astype(jnp.int32)
            live = lax.broadcasted_iota(jnp.int32, (BM, half), 0) < n_live
            lo, hi = _unpack_bf16_pairs(jnp.where(live, xbuf[lax.rem(gb, X_BUFS)], jnp.uint32(0)))
            g = _dot(lo, wg_bf[0:half, :]) + _dot(hi, wg_bf[half:, :])
            u = _dot(lo, wu_bf[0:half, :]) + _dot(hi, wu_bf[half:, :])
            hm = (g * _sigmoid(g) * u).astype(BF16)
            y = _dot(hm, wd_bf[...])
            ybuf[lax.rem(gb, Y_BUFS)] = _pack_bf16_pairs(y.astype(BF16).astype(F32))

        def block_pair(jp, carry):
            j0 = 2 * jp
            g0 = first + j0
            acquire(g0)
            acquire(g0 + 1)
            ffn(g0, j0)
            ffn(g0 + 1, j0 + 1)
            start_out(g0)
            start_out(g0 + 1)
            return carry

        lax.fori_loop(0, nb // 2, block_pair, 0)

        @pl.when(lax.rem(nb, 2) == 1)
        def _():
            gl = first + nb - 1
            acquire(gl)
            ffn(gl, nb - 1)
            start_out(gl)

        @pl.when(first + nb == total)
        def _():
            for k in range(Y_BUFS):
                @pl.when(total - 1 - k >= 0)
                def _():
                    wait_out(total - 1 - k)


def _experts(xs, tags, n_rows_out, dump_base, first_block, n_blocks_e, counts, w_eg, w_eu, w_ed):
    w_map = lambda e, fb, nb, ct, tot: (e, 0, 0)
    half = D_MODEL // 2
    total = jnp.sum(n_blocks_e).astype(jnp.int32).reshape(1)
    n_blocks = n_rows_out // BM
    return pl.pallas_call(
        functools.partial(_experts_kernel, dump_base),
        grid_spec=pltpu.PrefetchScalarGridSpec(
            num_scalar_prefetch=4,
            grid=(N_EXPERTS,),
            in_specs=[pl.BlockSpec(memory_space=pl.ANY),
                      pl.BlockSpec(memory_space=pl.ANY),
                      pl.BlockSpec((1, D_MODEL, D_EXPERT), w_map),
                      pl.BlockSpec((1, D_MODEL, D_EXPERT), w_map),
                      pl.BlockSpec((1, D_EXPERT, D_MODEL), w_map)],
            out_specs=(pl.BlockSpec(memory_space=pl.ANY), pl.BlockSpec(memory_space=pl.ANY)),
            scratch_shapes=[pltpu.VMEM((X_BUFS, BM, half), jnp.uint32), pltpu.VMEM((X_BUFS, BM, TAG_WORDS), jnp.int32),
                            pltpu.VMEM((Y_BUFS, BM, half), jnp.uint32), pltpu.VMEM((Y_BUFS, SUBLANES, BM), jnp.int32),
                            pltpu.VMEM((D_MODEL, D_EXPERT), BF16), pltpu.VMEM((D_MODEL, D_EXPERT), BF16),
                            pltpu.VMEM((D_EXPERT, D_MODEL), BF16),
                            pltpu.SemaphoreType.DMA((X_BUFS,)), pltpu.SemaphoreType.DMA((X_BUFS,)),
                            pltpu.SemaphoreType.DMA((Y_BUFS,)), pltpu.SemaphoreType.DMA((Y_BUFS,))]),
        out_shape=(jax.ShapeDtypeStruct((n_rows_out, half), jnp.uint32),
                   jax.ShapeDtypeStruct((n_blocks * SUBLANES, BM), jnp.int32)),
        compiler_params=pltpu.CompilerParams(dimension_semantics=("arbitrary",), vmem_limit_bytes=VMEM_LIMIT),
        name="experts",
    )(first_block, n_blocks_e, counts, total, xs, tags, w_eg, w_eu, w_ed)


def _combine_kernel(x2_ref, y1_ref, y2_ref, rt_ref, g_ref, o_ref):
    rt = rt_ref[...]
    r = jnp.transpose(jnp.concatenate([rt, jnp.zeros((LANES - rt.shape[0], rt.shape[1]), F32)], axis=0))
    g1, g2 = r[:, 2:3], r[:, 3:4]
    half = D_MODEL // 2
    y1_lo, y1_hi = _unpack_bf16_pairs_f32(y1_ref[...])
    y2_lo, y2_hi = _unpack_bf16_pairs_f32(y2_ref[...])
    x_lo = x2_ref[:, 0:half] + g1 * y1_lo + g2 * y2_lo
    x_hi = x2_ref[:, half:] + g1 * y1_hi + g2 * y2_hi
    ms = (jnp.sum(x_lo * x_lo, axis=-1, keepdims=True) + jnp.sum(x_hi * x_hi, axis=-1, keepdims=True)) / D_MODEL
    inv = lax.rsqrt(ms + EPS)
    o_ref[:, 0:half] = x_lo * inv * g_ref[:, 0:half]
    o_ref[:, half:] = x_hi * inv * g_ref[:, half:]


def _combine(x2, yg, rt, g_final, tm, blk1, blk2):
    n = x2.shape[0]
    return pl.pallas_call(
        _combine_kernel,
        grid=(n // tm,),
        in_specs=[pl.BlockSpec((tm, D_MODEL), lambda i: (i, 0)),
                  pl.BlockSpec((tm, D_MODEL // 2), lambda i: (blk1 + i, 0)),
                  pl.BlockSpec((tm, D_MODEL // 2), lambda i: (blk2 + i, 0)),
                  pl.BlockSpec((SUBLANES, tm), lambda i: (0, i)),
                  pl.BlockSpec((1, D_MODEL), lambda i: (0, 0))],
        out_specs=pl.BlockSpec((tm, D_MODEL), lambda i: (i, 0)),
        out_shape=jax.ShapeDtypeStruct((n, D_MODEL), F32),
        compiler_params=pltpu.CompilerParams(dimension_semantics=("arbitrary",), vmem_limit_bytes=VMEM_LIMIT),
        name="combine",
    )(x2, yg, yg, rt, g_final)


def _scatter_back(ys, dest, n_rows_out):
    return _sc_scatter_back(ys, dest, n_rows_out)


def _scatter_rows2(tables, slots_a, slots_b, tag_bases, n_rows_out):
    return _sc_scatter_rows2(tables, slots_a, slots_b, tag_bases, n_rows_out, SCATTER_CHUNK)


def kernel(x_prompt, x_sample, mem_prompt, state_conv, cache_mem_k, cache_mem_v, g_mix, w_in, conv_w, conv_b, ln_conv_g, ln_conv_b, ln_v_g, ln_v_b, w_sg, b_sg, w_out, g_mem, w_mk, w_mv, g_xattn, w_xq, w_xo, g_ffn, w_router_group, b_router_group, w_router_expert, b_router_expert, w_expert_gate, w_expert_up, w_expert_down, g_final):
    assert x_prompt.shape[0] == 1 and g_mix.shape[0] == 1
    n_p = x_prompt.shape[1]
    n_batch, t_len = x_sample.shape[0], x_sample.shape[1]
    n_s = n_batch * t_len
    row = lambda a: a.reshape(1, -1)

    w_router = jnp.concatenate(
        [w_router_group[0], jnp.transpose(w_router_expert[0], (1, 0, 2)).reshape(D_MODEL, N_EXPERTS)], axis=1)
    w_router = jnp.pad(w_router, ((0, 0), (0, LOGIT_LANES - w_router.shape[1]))).astype(BF16)
    b_router = jnp.pad(jnp.concatenate([b_router_group[0], b_router_expert[0].reshape(-1)]),
                       (0, LOGIT_LANES - N_GROUPS - N_EXPERTS)).reshape(1, LOGIT_LANES)
    tril_t = jnp.tril(jnp.ones((t_len, t_len), bool))
    w_sg_t = jnp.where(tril_t, w_sg[0][:, :t_len, :t_len], 0.0)
    eye_b = jnp.eye(n_batch, dtype=F32)
    w_sg_bd = jnp.einsum("ab,hij->haibj", eye_b, w_sg_t).reshape(SG_HEADS, n_s, n_s).astype(BF16)
    p = {
        "g_mix": row(g_mix[0]), "w_in": w_in[0],
        "conv_w": jnp.pad(conv_w[0], ((0, 1), (0, 0))), "conv_b": row(conv_b[0]),
        "ln_conv_g": row(ln_conv_g[0]), "ln_conv_b": row(ln_conv_b[0]),
        "ln_v_g": row(ln_v_g[0]), "ln_v_b": row(ln_v_b[0]),
        "w_sg": w_sg[0],
        "b_sg_rows": jnp.repeat(b_sg[0].T, SG_HEAD_DIM, axis=1),
        "w_sg_bd": w_sg_bd,
        "b_sg_rows_s": jnp.tile(jnp.repeat(b_sg[0][:, :t_len].T, SG_HEAD_DIM, axis=1), (n_batch, 1)),
        "w_out": w_out[0], "g_xattn": row(g_xattn[0]),
        "w_xq": w_xq[0], "w_xo": w_xo[0], "g_ffn": row(g_ffn[0]),
        "w_router": w_router, "b_router": b_router,
        "lower": jnp.tril(jnp.ones((n_s, n_s), BF16), -1),
    }

    k_p, v_p, p["k"], p["v"] = _memkv(mem_prompt[0], row(g_mem[0]), w_mk[0], w_mv[0])
    p["k_s"] = jnp.transpose(cache_mem_k[0].astype(BF16), (0, 2, 3, 1)).reshape(n_batch, D_MODEL, N_MEM)
    p["v_s"] = cache_mem_v[0].astype(BF16).reshape(n_batch, N_MEM, D_MODEL)

    assert n_p % n_s == 0
    x2_p, h3_p, logits_p, hist_p = _trunk_prompt(x_prompt[0], p)
    rt_p, cnt_t = _router(logits_p, ROUTE_RANK_ROWS)
    cnt_p = cnt_t[:, 0].reshape(1, LOGIT_LANES)
    x2_s, h3_s, rt_s, hist_s, sgv_s, cnt = _trunk_sample(
        x_sample.reshape(n_s, D_MODEL), state_conv[0], cnt_p, p, n_batch, t_len)

    experts = jnp.arange(N_EXPERTS, dtype=jnp.int32)
    w_e = (w_expert_gate[0], w_expert_up[0], w_expert_down[0])

    def moe_pass(cnt, h3_tables, rts, n_real):
        n_tot = sum(n_real)
        n_slots = -(-(n_tot * 2) // BM) * BM + N_EXPERTS * BM
        counts = cnt[0, :N_EXPERTS].astype(jnp.int32)
        padded = (counts + BM - 1) // BM * BM
        pad_start = jnp.cumsum(padded) - padded

        def one(e_row, rank_row):
            e = e_row.astype(jnp.int32)
            start = jnp.sum(jnp.where(e[None, :] == experts[:, None], pad_start[:, None], 0), axis=0)
            return start + rank_row.astype(jnp.int32)

        slots = [(one(rt[0], rt[4]), one(rt[1], rt[5])) for rt in rts]
        sa, sb, tag_bases, spare0, dest0 = [], [], [], n_slots, 0
        for tab, (a, b), n in zip(h3_tables, slots, n_real):
            n_spare = tab.shape[0] - n
            spare = spare0 + jnp.arange(n_spare, dtype=jnp.int32)
            sa.append(jnp.concatenate([a, spare]))
            sb.append(jnp.concatenate([b, spare + n_spare]))
            spare0 += 2 * n_spare
            tag_bases.append((dest0, dest0 + n))
            dest0 += 2 * n
        xs, tags = _scatter_rows2(tuple(h3_tables), tuple(sa), tuple(sb), tuple(tag_bases), spare0)
        ys, dest_blocks = _experts(xs, tags, n_slots, dest0, pad_start // BM, padded // BM, counts, *w_e)
        slot = jnp.arange(n_slots, dtype=jnp.int32)
        dest = dest_blocks.reshape(n_slots // BM, SUBLANES, BM)[:, 0, :].reshape(-1)
        dest = jnp.where(slot < jnp.sum(padded), dest, dest0 + slot)
        return _scatter_back(ys, dest, dest0 + n_slots)

    yg = moe_pass(cnt, [h3_p, h3_s], [rt_p, rt_s], [n_p, n_s])

    gf = row(g_final)
    y_p = _combine(x2_p, yg, rt_p, gf, TM_COMBINE, 0, n_p // TM_COMBINE)
    y_s = _combine(x2_s, yg, rt_s, gf, n_s, 2 * n_p // n_s, 2 * n_p // n_s + 1)

    return (y_p.reshape(1, n_p, D_MODEL),
            y_s.reshape(n_batch, t_len, D_MODEL),
            hist_p[HALO - HIST:].reshape(1, 1, HIST, D_CONV),
            hist_s.reshape(1, n_batch, HIST, D_CONV),
            k_p.reshape(1, 1, N_MEM, X_HEADS, X_HEAD_DIM),
            v_p.reshape(1, 1, N_MEM, X_HEADS, X_HEAD_DIM),
            sgv_s.reshape(1, n_batch, t_len, D_SG))
```

```python
import functools

import jax
import jax.numpy as jnp
from jax import lax
from jax.experimental import pallas as pl
from jax.experimental.pallas import tpu as pltpu
from jax.experimental.pallas import tpu_sc as plsc

D_MODEL = 1024
D_CONV = 512
D_SG = 512
CONV_WIDTH = 31
HIST = CONV_WIDTH - 1
SG_HEADS = 4
SG_HEAD_DIM = 128
SG_CHUNK = 128
N_MEM = 256
X_HEADS = 4
X_HEAD_DIM = 256
N_GROUPS = 4
EXPERTS_PER_GROUP = 8
N_EXPERTS = 32
D_EXPERT = 512
EPS = 1e-6

LANES = 128
SUBLANES = 8
SC_CORES = 2
SC_SUBCORES = 16
SC_WORKERS = SC_CORES * SC_SUBCORES
SC_LANES = 16
VMEM_LIMIT = 56 * 1024 * 1024

TM = 1024
TM_COMBINE = 2048
ROUTE_ROWS = 4096
ROUTE_RANK_ROWS = 512
HALO = 32
SEG = TM // SUBLANES
SEG_HALO = 32
CONV_BLOCK = 32
CAST_ROWS = 64
STAGE_ROWS = 256
BM = 256
X_LOOKAHEAD = 4
X_BUFS = X_LOOKAHEAD + 2
Y_BUFS = 4
ROW_DMA_PRIORITY = 1
SCATTER_CHUNK = 32
SCATTER_BUFS = 5
TAG_WORDS = 128
BACK_CHUNK = 56
BACK_BUFS = 4
LOGIT_LANES = 128

F32 = jnp.float32
BF16 = jnp.bfloat16


def _dot(a, b):
    return jnp.dot(a, b, preferred_element_type=F32)


def _rms(x, g):
    return x * lax.rsqrt(jnp.mean(x * x, axis=-1, keepdims=True) + EPS) * g


def _ln(x, g, b):
    mu = jnp.mean(x, axis=-1, keepdims=True)
    xc = x - mu
    var = jnp.mean(xc * xc, axis=-1, keepdims=True)
    return xc * lax.rsqrt(var + EPS) * g + b


def _sigmoid(x):
    return 1.0 / (1.0 + jnp.exp(-x))


def _pack_bf16_pairs(h):
    bits = lax.bitcast_convert_type(h, jnp.uint32)
    half = h.shape[1] // 2
    lo = lax.shift_right_logical(bits[:, :half], jnp.uint32(16))
    hi = bits[:, half:] & jnp.uint32(0xFFFF0000)
    return hi | lo


def _unpack_bf16_pairs_f32(p):
    lo = lax.bitcast_convert_type(lax.shift_left(p, jnp.uint32(16)), F32)
    hi = lax.bitcast_convert_type(p & jnp.uint32(0xFFFF0000), F32)
    return lo, hi


def _unpack_bf16_pairs(p):
    lo, hi = _unpack_bf16_pairs_f32(p)
    return lo.astype(BF16), hi.astype(BF16)


def _memkv_kernel(mem_ref, g_ref, wk_ref, wv_ref, k_ref, v_ref, kbf_ref, vbf_ref):
    m = _rms(mem_ref[...], g_ref[...]).astype(BF16)
    k = _dot(m, wk_ref[...].astype(BF16))
    v = _dot(m, wv_ref[...].astype(BF16))
    k_ref[...] = k
    v_ref[...] = v
    kbf_ref[...] = k.astype(BF16)
    vbf_ref[...] = v.astype(BF16)


def _memkv(mem, g_mem, w_mk, w_mv):
    return pl.pallas_call(
        _memkv_kernel,
        out_shape=(jax.ShapeDtypeStruct((N_MEM, D_MODEL), F32), jax.ShapeDtypeStruct((N_MEM, D_MODEL), F32),
                   jax.ShapeDtypeStruct((N_MEM, D_MODEL), BF16), jax.ShapeDtypeStruct((N_MEM, D_MODEL), BF16)),
        compiler_params=pltpu.CompilerParams(vmem_limit_bytes=VMEM_LIMIT),
        name="memkv",
    )(mem, g_mem, w_mk, w_mv)


def _attn_heads(q, k, v, k_transposed):
    outs = []
    for h in range(X_HEADS):
        sl = slice(h * X_HEAD_DIM, (h + 1) * X_HEAD_DIM)
        if k_transposed:
            s = _dot(q[:, sl], k[sl, :])
        else:
            s = lax.dot_general(q[:, sl], k[:, sl], (((1,), (1,)), ((), ())), preferred_element_type=F32)
        s = s * (X_HEAD_DIM ** -0.5)
        s = s - jnp.max(s, axis=-1, keepdims=True)
        p = jnp.exp(s)
        p = p / jnp.sum(p, axis=-1, keepdims=True)
        outs.append(_dot(p.astype(BF16), v[:, sl]).astype(BF16))
    return jnp.concatenate(outs, axis=1)


def _route(logits, run, strict_lower):
    m = logits.shape[0]
    r = strict_lower.shape[0]
    lane = lax.broadcasted_iota(jnp.int32, (m, LOGIT_LANES), 1).astype(F32)
    neg = jnp.float32(-jnp.inf)
    big = jnp.float32(LOGIT_LANES)

    def first_argmax(vals):
        mx = jnp.max(vals, axis=-1, keepdims=True)
        idx = jnp.min(jnp.where(vals == mx, lane, big), axis=-1, keepdims=True)
        return mx, idx

    lg = jnp.where(lane < N_GROUPS, logits, neg)
    g_max, g_idx = first_argmax(lg)
    g_w = 1.0 / jnp.sum(jnp.exp(lg - g_max), axis=-1, keepdims=True)

    lo = N_GROUPS + g_idx * EXPERTS_PER_GROUP
    le = jnp.where((lane >= lo) & (lane < lo + EXPERTS_PER_GROUP), logits, neg)
    v1, i1 = first_argmax(le)
    v2, i2 = first_argmax(jnp.where(lane == i1, neg, le))
    t = jnp.exp(v2 - v1)
    gate1 = g_w / (1.0 + t)
    gate2 = g_w * t / (1.0 + t)
    e1 = i1 - N_GROUPS
    e2 = i2 - N_GROUPS

    oh1 = (lane == e1).astype(F32)
    oh2 = (lane == e2).astype(F32)
    oh = oh1 + oh2
    befores = []
    for r0 in range(0, m, r):
        oh_r = oh[r0:r0 + r, :]
        befores.append(_dot(strict_lower, oh_r.astype(BF16)) + run)
        run = run + jnp.sum(oh_r, axis=0, keepdims=True)
    before = befores[0] if len(befores) == 1 else jnp.concatenate(befores, axis=0)
    rank1 = jnp.sum(before * oh1, axis=-1, keepdims=True)
    rank2 = jnp.sum(before * oh2, axis=-1, keepdims=True)
    new_run = run

    rinfo = jnp.where(lane == 0, e1,
            jnp.where(lane == 1, e2,
            jnp.where(lane == 2, gate1,
            jnp.where(lane == 3, gate2,
            jnp.where(lane == 4, rank1,
            jnp.where(lane == 5, rank2, 0.0))))))
    return jnp.transpose(rinfo)[0:SUBLANES, :], new_run


def _conv_segments(a, w_ref, seg_ref, tail_ref, yseg_ref, conv_ref):
    sub = lax.broadcasted_iota(jnp.int32, (SUBLANES, LANES), 0)
    for lt in range(D_CONV // LANES):
        ls = slice(lt * LANES, (lt + 1) * LANES)
        for t0 in range(0, TM, SUBLANES):
            s, m = divmod(t0, SEG)
            seg_ref[lt, pl.ds((SEG_HALO + m) * SUBLANES + s, SUBLANES, stride=SUBLANES), :] = a[t0:t0 + SUBLANES, ls]
        for j in range(SEG_HALO):
            cur = seg_ref[lt, (SEG + j) * SUBLANES:(SEG + j + 1) * SUBLANES, :]
            prev = tail_ref[lt, j * SUBLANES:(j + 1) * SUBLANES, :]
            seg_ref[lt, j * SUBLANES:(j + 1) * SUBLANES, :] = jnp.where(
                sub == 0, pltpu.roll(prev, 1, axis=0), pltpu.roll(cur, 1, axis=0))
            tail_ref[lt, j * SUBLANES:(j + 1) * SUBLANES, :] = cur
        for m0 in range(0, SEG, CONV_BLOCK):
            acc = [jnp.zeros((SUBLANES, LANES), F32) for _ in range(CONV_BLOCK)]
            for idx in range(m0 - HIST, m0 + CONV_BLOCK):
                b = seg_ref[lt, (SEG_HALO + idx) * SUBLANES:(SEG_HALO + idx + 1) * SUBLANES, :]
                for m in range(max(m0, idx), min(m0 + CONV_BLOCK, idx + CONV_WIDTH)):
                    k = idx - m + HIST
                    acc[m - m0] = acc[m - m0] + b * w_ref[k:k + 1, ls]
            for m in range(m0, m0 + CONV_BLOCK):
                yseg_ref[lt, m * SUBLANES:(m + 1) * SUBLANES, :] = acc[m - m0]
        for t0 in range(0, TM, SUBLANES):
            s, m = divmod(t0, SEG)
            conv_ref[t0:t0 + SUBLANES, ls] = yseg_ref[lt, pl.ds(m * SUBLANES + s, SUBLANES, stride=SUBLANES), :]


def _stage_cast(pairs, stage_ref, sem):
    work = [(src, dst, r0) for src, dst in pairs for r0 in range(0, src.shape[0], STAGE_ROWS)]

    def copy(k):
        src, _, r0 = work[k]
        return pltpu.make_async_copy(src.at[pl.ds(r0, STAGE_ROWS)],
                                     stage_ref.at[k % 2, :, pl.ds(0, src.shape[1])], sem.at[k % 2])

    copy(0).start()
    for k, (src, dst, r0) in enumerate(work):
        if k + 1 < len(work):
            copy(k + 1).start()
        copy(k).wait()

        def body(c, carry, k=k, src=src, dst=dst, r0=r0):
            c0 = pl.multiple_of(c * CAST_ROWS, CAST_ROWS)
            dst[pl.ds(r0 + c0, CAST_ROWS), :] = stage_ref[k % 2, pl.ds(c0, CAST_ROWS), 0:src.shape[1]].astype(BF16)
            return carry

        lax.fori_loop(0, STAGE_ROWS // CAST_ROWS, body, 0)


def _trunk_prompt_kernel(x_ref, gmix_ref, win32_ref, convw_ref, convb_ref, lncg_ref, lncb_ref, lnvg_ref, lnvb_ref,
                         wsg_ref, bsg_ref, wout32_ref, gx_ref, wxq32_ref, kmem_ref, v_ref, wxo32_ref, gffn_ref, wr_ref,
                         br_ref,
                         x2_ref, h3_ref, logit_ref, hist_ref,
                         seg_ref, tail_ref, yseg_ref, conv_ref, win_ref, wout_ref, wxq_ref, wxo_ref, stage_ref, stage_sem):
    i = pl.program_id(0)

    @pl.when(i == 0)
    def _():
        tail_ref[...] = jnp.zeros(tail_ref.shape, F32)
        _stage_cast([(win32_ref, win_ref), (wout32_ref, wout_ref), (wxq32_ref, wxq_ref), (wxo32_ref, wxo_ref)],
                    stage_ref, stage_sem)

    x = x_ref[...]
    h = _rms(x, gmix_ref[...]).astype(BF16)

    a_in = _dot(h, win_ref[:, 0:D_CONV])
    a_gate = _dot(h, win_ref[:, D_CONV:2 * D_CONV])
    a = a_in * _sigmoid(a_gate)
    hist_ref[...] = a[TM - HALO:, :]
    _conv_segments(a, convw_ref, seg_ref, tail_ref, yseg_ref, conv_ref)

    y = _ln(conv_ref[...] + convb_ref[...], lncg_ref[...], lncb_ref[...])
    a_out = (y * _sigmoid(y)).astype(BF16)

    u = _dot(h, win_ref[:, 2 * D_CONV:2 * D_CONV + D_SG])
    v = _ln(_dot(h, win_ref[:, 2 * D_CONV + D_SG:]), lnvg_ref[...], lnvb_ref[...]).astype(BF16)
    ri = lax.broadcasted_iota(jnp.int32, (SG_CHUNK, SG_CHUNK), 0)
    ci = lax.broadcasted_iota(jnp.int32, (SG_CHUNK, SG_CHUNK), 1)
    w_tril = [jnp.where(ci <= ri, wsg_ref[hh], 0.0).astype(BF16) for hh in range(SG_HEADS)]
    gate_rows = []
    for c in range(TM // SG_CHUNK):
        rs = slice(c * SG_CHUNK, (c + 1) * SG_CHUNK)
        heads = [_dot(w_tril[hh], v[rs, hh * SG_HEAD_DIM:(hh + 1) * SG_HEAD_DIM]) for hh in range(SG_HEADS)]
        gate_rows.append(jnp.concatenate(heads, axis=1) + bsg_ref[...])
    b_out = (u * jnp.concatenate(gate_rows, axis=0)).astype(BF16)

    x1 = x + _dot(a_out, wout_ref[0:D_CONV, :]) + _dot(b_out, wout_ref[D_CONV:, :])

    hx = _rms(x1, gx_ref[...]).astype(BF16)
    q = _dot(hx, wxq_ref[...]).astype(BF16)
    x2 = x1 + _dot(_attn_heads(q, kmem_ref[...], v_ref[...], False), wxo_ref[...])
    x2_ref[...] = x2

    h3 = _rms(x2, gffn_ref[...]).astype(BF16)
    h3_ref[...] = _pack_bf16_pairs(h3.astype(F32))
    logit_ref[...] = _dot(h3, wr_ref[...]) + br_ref[...]


def _router_kernel(logit_ref, upper_ref, ones_ref, rt_ref, cnt_ref, run_ref):
    @pl.when(pl.program_id(0) == 0)
    def _():
        run_ref[...] = jnp.zeros(run_ref.shape, F32)

    n = ROUTE_ROWS
    lt = jnp.transpose(logit_ref[...])
    neg = jnp.float32(-jnp.inf)
    big = jnp.float32(LOGIT_LANES)

    def first_argmax(vals, rows):
        mx = jnp.max(vals, axis=0, keepdims=True)
        idx = jnp.min(jnp.where(vals == mx, rows, big), axis=0, keepdims=True)
        return mx, idx

    row8 = lax.broadcasted_iota(jnp.int32, (SUBLANES, n), 0).astype(F32)
    lg = jnp.where(row8 < N_GROUPS, lt[0:SUBLANES, :], neg)
    g_max, g_idx = first_argmax(lg, row8)
    g_w = 1.0 / jnp.sum(jnp.exp(lg - g_max), axis=0, keepdims=True)

    n_rows = N_GROUPS + N_EXPERTS + (-(N_GROUPS + N_EXPERTS)) % SUBLANES
    rows = lax.broadcasted_iota(jnp.int32, (n_rows, n), 0).astype(F32)
    lo = N_GROUPS + g_idx * EXPERTS_PER_GROUP
    le = jnp.where((rows >= lo) & (rows < lo + EXPERTS_PER_GROUP), lt[0:n_rows, :], neg)
    v1, i1 = first_argmax(le, rows)
    v2, i2 = first_argmax(jnp.where(rows == i1, neg, le), rows)
    t = jnp.exp(v2 - v1)
    gate1 = g_w / (1.0 + t)
    gate2 = g_w * t / (1.0 + t)
    e1 = i1 - N_GROUPS
    e2 = i2 - N_GROUPS

    erow = lax.broadcasted_iota(jnp.int32, (LOGIT_LANES, n), 0).astype(F32)
    oh1 = (erow == e1).astype(F32)
    oh2 = (erow == e2).astype(F32)
    oh = (oh1 + oh2).astype(BF16)
    r = upper_ref.shape[0]
    run = run_ref[...]
    rank1, rank2 = [], []
    for c0 in range(0, n, r):
        cs = slice(c0, c0 + r)
        before = _dot(oh[:, cs], upper_ref[...]) + run
        rank1.append(jnp.sum(before * oh1[:, cs], axis=0, keepdims=True))
        rank2.append(jnp.sum(before * oh2[:, cs], axis=0, keepdims=True))
        run = run + _dot(oh[:, cs], ones_ref[...])
    run_ref[...] = run
    cnt_ref[...] = run[:, 0:LANES]

    sub = lax.broadcasted_iota(jnp.int32, (SUBLANES, n), 0)
    vals = (e1, e2, gate1, gate2, jnp.concatenate(rank1, axis=1), jnp.concatenate(rank2, axis=1))
    rt = jnp.zeros((SUBLANES, n), F32)
    for k, v in enumerate(vals):
        rt = jnp.where(sub == k, v, rt)
    rt_ref[...] = rt


def _router(logits, rank_block):
    n = logits.shape[0]
    assert n % ROUTE_ROWS == 0 and ROUTE_ROWS % rank_block == 0
    upper = jnp.triu(jnp.ones((rank_block, rank_block), BF16), 1)
    ones = jnp.ones((rank_block, rank_block), BF16)
    return pl.pallas_call(
        _router_kernel,
        grid=(n // ROUTE_ROWS,),
        in_specs=[pl.BlockSpec((ROUTE_ROWS, LOGIT_LANES), lambda i: (i, 0)),
                  pl.BlockSpec(upper.shape, lambda i: (0, 0)),
                  pl.BlockSpec(ones.shape, lambda i: (0, 0))],
        out_specs=(pl.BlockSpec((SUBLANES, ROUTE_ROWS), lambda i: (0, i)),
                   pl.BlockSpec((LOGIT_LANES, LANES), lambda i: (0, 0))),
        out_shape=(jax.ShapeDtypeStruct((SUBLANES, n), F32), jax.ShapeDtypeStruct((LOGIT_LANES, LANES), F32)),
        scratch_shapes=[pltpu.VMEM((LOGIT_LANES, rank_block), F32)],
        compiler_params=pltpu.CompilerParams(dimension_semantics=("arbitrary",), vmem_limit_bytes=VMEM_LIMIT),
        name="router",
    )(logits, upper, ones)


def _const_spec(shape):
    nd = len(shape)
    return pl.BlockSpec(shape, lambda i: (0,) * nd, pipeline_mode=pl.Buffered(1))


def _trunk_prompt(x, p):
    n = x.shape[0]
    assert n % TM == 0
    row = lambda w: pl.BlockSpec((TM, w), lambda i: (i, 0))
    consts = [p["g_mix"], p["w_in"], p["conv_w"], p["conv_b"], p["ln_conv_g"], p["ln_conv_b"], p["ln_v_g"],
              p["ln_v_b"], p["w_sg"], p["b_sg_rows"], p["w_out"], p["g_xattn"], p["w_xq"], p["k"], p["v"],
              p["w_xo"], p["g_ffn"], p["w_router"], p["b_router"]]
    staged = (p["w_in"], p["w_out"], p["w_xq"], p["w_xo"])
    spec = lambda c: pl.BlockSpec(memory_space=pl.ANY) if any(c is s for s in staged) else _const_spec(c.shape)
    return pl.pallas_call(
        _trunk_prompt_kernel,
        grid=(n // TM,),
        in_specs=[row(D_MODEL)] + [spec(c) for c in consts],
        out_specs=(row(D_MODEL), row(D_MODEL // 2), row(LOGIT_LANES),
                   pl.BlockSpec((HALO, D_CONV), lambda i: (0, 0))),
        out_shape=(jax.ShapeDtypeStruct((n, D_MODEL), F32),
                   jax.ShapeDtypeStruct((n, D_MODEL // 2), jnp.uint32),
                   jax.ShapeDtypeStruct((n, LOGIT_LANES), F32),
                   jax.ShapeDtypeStruct((HALO, D_CONV), F32)),
        scratch_shapes=[pltpu.VMEM((D_CONV // LANES, (SEG_HALO + SEG) * SUBLANES, LANES), F32),
                        pltpu.VMEM((D_CONV // LANES, SEG_HALO * SUBLANES, LANES), F32),
                        pltpu.VMEM((D_CONV // LANES, TM, LANES), F32),
                        pltpu.VMEM((TM, D_CONV), F32),
                        pltpu.VMEM(p["w_in"].shape, BF16), pltpu.VMEM(p["w_out"].shape, BF16),
                        pltpu.VMEM(p["w_xq"].shape, BF16), pltpu.VMEM(p["w_xo"].shape, BF16),
                        pltpu.VMEM((2, STAGE_ROWS, max(s.shape[1] for s in staged)), F32),
                        pltpu.SemaphoreType.DMA((2,))],
        compiler_params=pltpu.CompilerParams(dimension_semantics=("arbitrary",), vmem_limit_bytes=VMEM_LIMIT),
        name="trunk_prompt",
    )(x, *consts)


def _trunk_sample_kernel(n_batch, t_len,
                         x_ref, hist_in_ref, run_in_ref, gmix_ref, win_ref, convw_ref, convb_ref, lncg_ref, lncb_ref,
                         lnvg_ref, lnvb_ref, wsgbd_ref, bsg_ref, wout_ref, gx_ref, wxq_ref, kmem_ref, v_ref, wxo_ref,
                         gffn_ref, wr_ref, br_ref, lower_ref,
                         x2_ref, h3_ref, rt_ref, hist_ref, sgv_ref, cnt_ref,
                         ext_ref, conv_ref, att_ref):
    x = x_ref[...]
    h = _rms(x, gmix_ref[...]).astype(BF16)
    z = _dot(h, win_ref[...].astype(BF16))
    a = z[:, 0:D_CONV] * _sigmoid(z[:, D_CONV:2 * D_CONV])
    ext_len = HIST + t_len
    for b in range(n_batch):
        ext_ref[b, 0:HIST, :] = hist_in_ref[b]
        ext_ref[b, HIST:ext_len, :] = a[b * t_len:(b + 1) * t_len, :]
    for b in range(n_batch):
        acc = jnp.zeros((t_len, D_CONV), F32)
        for k in range(CONV_WIDTH):
            acc = acc + ext_ref[b, k:k + t_len, :] * convw_ref[k:k + 1, :]
        conv_ref[b * t_len:(b + 1) * t_len, :] = acc
        hist_ref[b] = ext_ref[b, ext_len - HIST:ext_len, :]

    y = _ln(conv_ref[...] + convb_ref[...], lncg_ref[...], lncb_ref[...])
    a_out = (y * _sigmoid(y)).astype(BF16)

    u = z[:, 2 * D_CONV:2 * D_CONV + D_SG]
    v = _ln(z[:, 2 * D_CONV + D_SG:], lnvg_ref[...], lnvb_ref[...])
    sgv_ref[...] = v
    vb = v.astype(BF16)
    heads = [_dot(wsgbd_ref[hh], vb[:, hh * SG_HEAD_DIM:(hh + 1) * SG_HEAD_DIM]) for hh in range(SG_HEADS)]
    b_out = (u * (jnp.concatenate(heads, axis=1) + bsg_ref[...])).astype(BF16)

    x1 = (x + _dot(a_out, wout_ref[0:D_CONV, :].astype(BF16))
          + _dot(b_out, wout_ref[D_CONV:, :].astype(BF16)))

    hx = _rms(x1, gx_ref[...]).astype(BF16)
    q = _dot(hx, wxq_ref[...].astype(BF16)).astype(BF16)
    for b in range(n_batch):
        rs = slice(b * t_len, (b + 1) * t_len)
        att_ref[rs, :] = _attn_heads(q[rs, :], kmem_ref[b], v_ref[b], True)
    x2 = x1 + _dot(att_ref[...], wxo_ref[...].astype(BF16))
    x2_ref[...] = x2

    h3 = _rms(x2, gffn_ref[...]).astype(BF16)
    m = n_batch * t_len
    h3_ref[0:m, :] = _pack_bf16_pairs(h3.astype(F32))
    if h3_ref.shape[0] > m:
        h3_ref[m:, :] = jnp.zeros((h3_ref.shape[0] - m, D_MODEL // 2), jnp.uint32)
    rt, new_run = _route(_dot(h3, wr_ref[...]) + br_ref[...], run_in_ref[...], lower_ref[...])
    rt_ref[...] = rt
    cnt_ref[...] = new_run


def _trunk_sample(x, hist, run, p, n_batch, t_len):
    m = n_batch * t_len
    args = [x, hist, run, p["g_mix"], p["w_in"], p["conv_w"], p["conv_b"], p["ln_conv_g"], p["ln_conv_b"],
            p["ln_v_g"], p["ln_v_b"], p["w_sg_bd"], p["b_sg_rows_s"], p["w_out"], p["g_xattn"], p["w_xq"],
            p["k_s"], p["v_s"], p["w_xo"], p["g_ffn"], p["w_router"], p["b_router"], p["lower"]]
    return pl.pallas_call(
        functools.partial(_trunk_sample_kernel, n_batch, t_len),
        out_shape=(jax.ShapeDtypeStruct((m, D_MODEL), F32),
                   jax.ShapeDtypeStruct((-(-m // (SC_WORKERS * SUBLANES)) * SC_WORKERS * SUBLANES, D_MODEL // 2),
                                        jnp.uint32),
                   jax.ShapeDtypeStruct((SUBLANES, m), F32),
                   jax.ShapeDtypeStruct((n_batch, HIST, D_CONV), F32),
                   jax.ShapeDtypeStruct((m, D_SG), F32),
                   jax.ShapeDtypeStruct((1, LOGIT_LANES), F32)),
        scratch_shapes=[pltpu.VMEM((n_batch, HIST + t_len, D_CONV), F32),
                        pltpu.VMEM((m, D_CONV), F32),
                        pltpu.VMEM((m, D_MODEL), BF16)],
        compiler_params=pltpu.CompilerParams(vmem_limit_bytes=VMEM_LIMIT),
        name="trunk_sample",
    )(*args)


def _sc_worker_id():
    return lax.axis_index("s") * SC_CORES + lax.axis_index("c")


def _sc_chunk(per_w, max_chunk):
    assert per_w % SUBLANES == 0 and max_chunk <= LANES
    return max(c for c in range(SUBLANES, max_chunk + 1, SUBLANES) if per_w % c == 0)


def _sc_scatter_rows2(tables, slots_a, slots_b, tag_bases, n_rows_out, max_chunk):
    d, dtype = tables[0].shape[1], tables[0].dtype
    plans = []
    for t in tables:
        per_w = t.shape[0] // SC_WORKERS
        assert per_w * SC_WORKERS == t.shape[0]
        chunk = _sc_chunk(per_w, max_chunk)
        plans.append((per_w, chunk, per_w // chunk))
    cmax = max(c for _, c, _ in plans)
    n_t = len(tables)
    mesh = plsc.VectorSubcoreMesh(core_axis_name="c", subcore_axis_name="s")

    nb = SCATTER_BUFS
    lag = 2
    scratch = []
    for _, chunk, _ in plans:
        for _ in range(nb):
            scratch += [pltpu.VMEM((chunk,), jnp.int32), pltpu.VMEM((chunk,), jnp.int32)]
    scratch += [pltpu.VMEM((cmax, d), dtype)] * nb
    scratch += [pltpu.VMEM((cmax, TAG_WORDS), jnp.int32)] * (2 * nb)
    scratch += [pltpu.SemaphoreType.DMA] * (2 * nb)

    @functools.partial(pl.kernel, mesh=mesh,
                       out_type=(jax.ShapeDtypeStruct((n_rows_out, d), dtype),
                                 jax.ShapeDtypeStruct((n_rows_out, TAG_WORDS), jnp.int32)),
                       scratch_types=scratch)
    def scatter(*refs):
        tab_hbm = refs[0:n_t]
        sa_hbm = refs[n_t:2 * n_t]
        sb_hbm = refs[2 * n_t:3 * n_t]
        out_hbm, tag_hbm = refs[3 * n_t], refs[3 * n_t + 1]
        sc = refs[3 * n_t + 2:]
        idx_refs = sc[:2 * nb * n_t]
        rows = sc[2 * nb * n_t:2 * nb * n_t + nb]
        tagbufs = sc[2 * nb * n_t + nb:2 * nb * n_t + 3 * nb]
        lsem = sc[2 * nb * n_t + 3 * nb:2 * nb * n_t + 4 * nb]
        ssem = sc[2 * nb * n_t + 4 * nb:]
        wid = _sc_worker_id()

        work = []
        for t, (per_w, chunk, n_chunks) in enumerate(plans):
            for j in range(n_chunks):
                work.append((t, wid * per_w + j * chunk, chunk))

        def parts(k):
            t, off, chunk = work[k]
            b = k % nb
            ia, ib = idx_refs[2 * nb * t + 2 * b], idx_refs[2 * nb * t + 2 * b + 1]
            full = chunk == cmax
            rv = rows[b] if full else rows[b].at[pl.ds(0, chunk)]
            ta = tagbufs[2 * b] if full else tagbufs[2 * b].at[pl.ds(0, chunk)]
            tb = tagbufs[2 * b + 1] if full else tagbufs[2 * b + 1].at[pl.ds(0, chunk)]
            return t, off, chunk, b, ia, ib, rv, ta, tb

        def start_load(k):
            t, off, chunk, b, ia, ib, rv, ta, tb = parts(k)
            return (pltpu.async_copy(tab_hbm[t].at[pl.ds(off, chunk)], rv, lsem[b]),
                    pltpu.async_copy(sa_hbm[t].at[pl.ds(off, chunk)], ia, lsem[b]),
                    pltpu.async_copy(sb_hbm[t].at[pl.ds(off, chunk)], ib, lsem[b]))

        def start_scatter(k):
            t, off, chunk, b, ia, ib, rv, ta, tb = parts(k)
            base_a, base_b = tag_bases[t]
            for r in range(chunk):
                row_id = (off + r).astype(jnp.int32)
                tagbufs[2 * b][r, pl.ds(0, SC_LANES)] = jnp.zeros((SC_LANES,), jnp.int32) + (base_a + row_id)
                tagbufs[2 * b + 1][r, pl.ds(0, SC_LANES)] = jnp.zeros((SC_LANES,), jnp.int32) + (base_b + row_id)
            return (pltpu.async_copy(rv, out_hbm.at[ia], ssem[b]), pltpu.async_copy(rv, out_hbm.at[ib], ssem[b]),
                    pltpu.async_copy(ta, tag_hbm.at[ia], ssem[b]), pltpu.async_copy(tb, tag_hbm.at[ib], ssem[b]))

        loads, scatters = {}, {}
        for k in range(len(work) + lag):
            if k < len(work):
                if k >= nb:
                    for c in scatters.pop(k - nb):
                        c.wait()
                loads[k] = start_load(k)
            w = k - lag
            if w >= 0:
                for c in loads.pop(w):
                    c.wait()
                scatters[w] = start_scatter(w)
        for w in sorted(scatters):
            for c in scatters[w]:
                c.wait()

    return scatter(*tables, *slots_a, *slots_b)


def _sc_scatter_back(ys, dest, n_rows_out):
    n_rows, d = ys.shape
    per_w = n_rows // SC_WORKERS
    assert per_w * SC_WORKERS == n_rows
    chunk = _sc_chunk(per_w, BACK_CHUNK)
    n_chunks = per_w // chunk
    nb = BACK_BUFS
    lag = nb // 2
    mesh = plsc.VectorSubcoreMesh(core_axis_name="c", subcore_axis_name="s")

    @functools.partial(
        pl.kernel, mesh=mesh,
        out_type=jax.ShapeDtypeStruct((n_rows_out, d), ys.dtype),
        scratch_types=([pltpu.VMEM((chunk,), jnp.int32)] * nb + [pltpu.VMEM((chunk, d), ys.dtype)] * nb
                       + [pltpu.SemaphoreType.DMA] * (2 * nb)),
    )
    def scatter_back(ys_hbm, dest_hbm, out_hbm, *rest):
        idx = rest[:nb]
        rows = rest[nb:2 * nb]
        lsem = rest[2 * nb:3 * nb]
        ssem = rest[3 * nb:]
        base = _sc_worker_id() * per_w

        loads, scatters = {}, {}
        for k in range(n_chunks + lag):
            if k < n_chunks:
                b = k % nb
                if k >= nb:
                    scatters.pop(k - nb).wait()
                off = base + k * chunk
                loads[k] = (pltpu.async_copy(ys_hbm.at[pl.ds(off, chunk)], rows[b], lsem[b]),
                            pltpu.async_copy(dest_hbm.at[pl.ds(off, chunk)], idx[b], lsem[b]))
            w = k - lag
            if w >= 0:
                b = w % nb
                for c in loads.pop(w):
                    c.wait()
                scatters[w] = pltpu.async_copy(rows[b], out_hbm.at[idx[b]], ssem[b])
        for w in sorted(scatters):
            scatters[w].wait()

    return scatter_back(ys, dest)


def _experts_kernel(dump_base, first_ref, nblk_ref, cnt_ref, tot_ref, xs_hbm, tag_hbm, wg_ref, wu_ref, wd_ref,
                    ys_hbm, dest_hbm, xbuf, tbuf, ybuf, dbuf, wg_bf, wu_bf, wd_bf, in_sem, tin_sem, out_sem, dout_sem):
    e = pl.program_id(0)
    nb = nblk_ref[e]
    first = first_ref[e]
    cnt = cnt_ref[e]
    total = tot_ref[0]
    half = D_MODEL // 2

    def in_copies(gb):
        slot = lax.rem(gb, X_BUFS)
        return (pltpu.make_async_copy(xs_hbm.at[pl.ds(gb * BM, BM)], xbuf.at[slot], in_sem.at[slot]),
                pltpu.make_async_copy(tag_hbm.at[pl.ds(gb * BM, BM)], tbuf.at[slot], tin_sem.at[slot]))

    def out_copies(gb):
        slot = lax.rem(gb, Y_BUFS)
        return (pltpu.make_async_copy(ybuf.at[slot], ys_hbm.at[pl.ds(gb * BM, BM)], out_sem.at[slot]),
                pltpu.make_async_copy(dbuf.at[slot], dest_hbm.at[pl.ds(gb * SUBLANES, SUBLANES)], dout_sem.at[slot]))

    def start_in(gb):
        for c in in_copies(gb):
            c.start(priority=ROW_DMA_PRIORITY)

    def start_out(gb):
        for c in out_copies(gb):
            c.start(priority=ROW_DMA_PRIORITY)

    def wait_out(gb):
        for c in out_copies(gb):
            c.wait()

    @pl.when(nb > 0)
    def _():
        @pl.when(first == 0)
        def _():
            for k in range(X_LOOKAHEAD):
                @pl.when(k < total)
                def _():
                    start_in(k)

        wg_bf[...] = wg_ref[0].astype(BF16)
        wu_bf[...] = wu_ref[0].astype(BF16)
        wd_bf[...] = wd_ref[0].astype(BF16)

        def acquire(gb):
            @pl.when(gb + X_LOOKAHEAD < total)
            def _():
                start_in(gb + X_LOOKAHEAD)

            for c in in_copies(gb):
                c.wait()

            @pl.when(gb >= Y_BUFS)
            def _():
                wait_out(gb - Y_BUFS)

        def ffn(gb, j):
            n_live = cnt - j * BM
            tags_t = jnp.transpose(tbuf[lax.rem(gb, X_BUFS)].astype(F32))
            lane = lax.broadcasted_iota(jnp.int32, (SUBLANES, BM), 1)
            own = (dump_base + gb * BM + lane).astype(F32)
            dest = jnp.where(lane < n_live, jnp.broadcast_to(tags_t[0:1, :], (SUBLANES, BM)), own)
            dbuf[lax.rem(gb, Y_BUFS)] = dest.astype(jnp.int32)
            live = lax.broadcasted_iota(jnp.int32, (BM, half), 0) < n_live
            lo, hi = _unpack_bf16_pairs(jnp.where(live, xbuf[lax.rem(gb, X_BUFS)], jnp.uint32(0)))
            g = _dot(lo, wg_bf[0:half, :]) + _dot(hi, wg_bf[half:, :])
            u = _dot(lo, wu_bf[0:half, :]) + _dot(hi, wu_bf[half:, :])
            hm = (g * _sigmoid(g) * u).astype(BF16)
            y = _dot(hm, wd_bf[...])
            ybuf[lax.rem(gb, Y_BUFS)] = _pack_bf16_pairs(y.astype(BF16).astype(F32))

        def block_pair(jp, carry):
            j0 = 2 * jp
            g0 = first + j0
            acquire(g0)
            acquire(g0 + 1)
            ffn(g0, j0)
            ffn(g0 + 1, j0 + 1)
            start_out(g0)
            start_out(g0 + 1)
            return carry

        lax.fori_loop(0, nb // 2, block_pair, 0)

        @pl.when(lax.rem(nb, 2) == 1)
        def _():
            gl = first + nb - 1
            acquire(gl)
            ffn(gl, nb - 1)
            start_out(gl)

        @pl.when(first + nb == total)
        def _():
            for k in range(Y_BUFS):
                @pl.when(total - 1 - k >= 0)
                def _():
                    wait_out(total - 1 - k)


def _experts(xs, tags, n_rows_out, dump_base, first_block, n_blocks_e, counts, w_eg, w_eu, w_ed):
    w_map = lambda e, fb, nb, ct, tot: (e, 0, 0)
    half = D_MODEL // 2
    total = jnp.sum(n_blocks_e).astype(jnp.int32).reshape(1)
    n_blocks = n_rows_out // BM
    return pl.pallas_call(
        functools.partial(_experts_kernel, dump_base),
        grid_spec=pltpu.PrefetchScalarGridSpec(
            num_scalar_prefetch=4,
            grid=(N_EXPERTS,),
            in_specs=[pl.BlockSpec(memory_space=pl.ANY),
                      pl.BlockSpec(memory_space=pl.ANY),
                      pl.BlockSpec((1, D_MODEL, D_EXPERT), w_map),
                      pl.BlockSpec((1, D_MODEL, D_EXPERT), w_map),
                      pl.BlockSpec((1, D_EXPERT, D_MODEL), w_map)],
            out_specs=(pl.BlockSpec(memory_space=pl.ANY), pl.BlockSpec(memory_space=pl.ANY)),
            scratch_shapes=[pltpu.VMEM((X_BUFS, BM, half), jnp.uint32), pltpu.VMEM((X_BUFS, BM, TAG_WORDS), jnp.int32),
                            pltpu.VMEM((Y_BUFS, BM, half), jnp.uint32), pltpu.VMEM((Y_BUFS, SUBLANES, BM), jnp.int32),
                            pltpu.VMEM((D_MODEL, D_EXPERT), BF16), pltpu.VMEM((D_MODEL, D_EXPERT), BF16),
                            pltpu.VMEM((D_EXPERT, D_MODEL), BF16),
                            pltpu.SemaphoreType.DMA((X_BUFS,)), pltpu.SemaphoreType.DMA((X_BUFS,)),
                            pltpu.SemaphoreType.DMA((Y_BUFS,)), pltpu.SemaphoreType.DMA((Y_BUFS,))]),
        out_shape=(jax.ShapeDtypeStruct((n_rows_out, half), jnp.uint32),
                   jax.ShapeDtypeStruct((n_blocks * SUBLANES, BM), jnp.int32)),
        compiler_params=pltpu.CompilerParams(dimension_semantics=("arbitrary",), vmem_limit_bytes=VMEM_LIMIT),
        name="experts",
    )(first_block, n_blocks_e, counts, total, xs, tags, w_eg, w_eu, w_ed)


def _combine_kernel(x2_ref, y1_ref, y2_ref, rt_ref, g_ref, o_ref):
    rt = rt_ref[...]
    r = jnp.transpose(jnp.concatenate([rt, jnp.zeros((LANES - rt.shape[0], rt.shape[1]), F32)], axis=0))
    g1, g2 = r[:, 2:3], r[:, 3:4]
    half = D_MODEL // 2
    y1_lo, y1_hi = _unpack_bf16_pairs_f32(y1_ref[...])
    y2_lo, y2_hi = _unpack_bf16_pairs_f32(y2_ref[...])
    x_lo = x2_ref[:, 0:half] + g1 * y1_lo + g2 * y2_lo
    x_hi = x2_ref[:, half:] + g1 * y1_hi + g2 * y2_hi
    ms = (jnp.sum(x_lo * x_lo, axis=-1, keepdims=True) + jnp.sum(x_hi * x_hi, axis=-1, keepdims=True)) / D_MODEL
    inv = lax.rsqrt(ms + EPS)
    o_ref[:, 0:half] = x_lo * inv * g_ref[:, 0:half]
    o_ref[:, half:] = x_hi * inv * g_ref[:, half:]


def _combine(x2, yg, rt, g_final, tm, blk1, blk2):
    n = x2.shape[0]
    return pl.pallas_call(
        _combine_kernel,
        grid=(n // tm,),
        in_specs=[pl.BlockSpec((tm, D_MODEL), lambda i: (i, 0)),
                  pl.BlockSpec((tm, D_MODEL // 2), lambda i: (blk1 + i, 0)),
                  pl.BlockSpec((tm, D_MODEL // 2), lambda i: (blk2 + i, 0)),
                  pl.BlockSpec((SUBLANES, tm), lambda i: (0, i)),
                  pl.BlockSpec((1, D_MODEL), lambda i: (0, 0))],
        out_specs=pl.BlockSpec((tm, D_MODEL), lambda i: (i, 0)),
        out_shape=jax.ShapeDtypeStruct((n, D_MODEL), F32),
        compiler_params=pltpu.CompilerParams(dimension_semantics=("arbitrary",), vmem_limit_bytes=VMEM_LIMIT),
        name="combine",
    )(x2, yg, yg, rt, g_final)


def _scatter_back(ys, dest, n_rows_out):
    return _sc_scatter_back(ys, dest, n_rows_out)


def _scatter_rows2(tables, slots_a, slots_b, tag_bases, n_rows_out):
    return _sc_scatter_rows2(tables, slots_a, slots_b, tag_bases, n_rows_out, SCATTER_CHUNK)


def kernel(x_prompt, x_sample, mem_prompt, state_conv, cache_mem_k, cache_mem_v, g_mix, w_in, conv_w, conv_b, ln_conv_g, ln_conv_b, ln_v_g, ln_v_b, w_sg, b_sg, w_out, g_mem, w_mk, w_mv, g_xattn, w_xq, w_xo, g_ffn, w_router_group, b_router_group, w_router_expert, b_router_expert, w_expert_gate, w_expert_up, w_expert_down, g_final):
    assert x_prompt.shape[0] == 1 and g_mix.shape[0] == 1
    n_p = x_prompt.shape[1]
    n_batch, t_len = x_sample.shape[0], x_sample.shape[1]
    n_s = n_batch * t_len
    row = lambda a: a.reshape(1, -1)

    w_router = jnp.concatenate(
        [w_router_group[0], jnp.transpose(w_router_expert[0], (1, 0, 2)).reshape(D_MODEL, N_EXPERTS)], axis=1)
    w_router = jnp.pad(w_router, ((0, 0), (0, LOGIT_LANES - w_router.shape[1]))).astype(BF16)
    b_router = jnp.pad(jnp.concatenate([b_router_group[0], b_router_expert[0].reshape(-1)]),
                       (0, LOGIT_LANES - N_GROUPS - N_EXPERTS)).reshape(1, LOGIT_LANES)
    tril_t = jnp.tril(jnp.ones((t_len, t_len), bool))
    w_sg_t = jnp.where(tril_t, w_sg[0][:, :t_len, :t_len], 0.0)
    eye_b = jnp.eye(n_batch, dtype=F32)
    w_sg_bd = jnp.einsum("ab,hij->haibj", eye_b, w_sg_t).reshape(SG_HEADS, n_s, n_s).astype(BF16)
    p = {
        "g_mix": row(g_mix[0]), "w_in": w_in[0],
        "conv_w": jnp.pad(conv_w[0], ((0, 1), (0, 0))), "conv_b": row(conv_b[0]),
        "ln_conv_g": row(ln_conv_g[0]), "ln_conv_b": row(ln_conv_b[0]),
        "ln_v_g": row(ln_v_g[0]), "ln_v_b": row(ln_v_b[0]),
        "w_sg": w_sg[0],
        "b_sg_rows": jnp.repeat(b_sg[0].T, SG_HEAD_DIM, axis=1),
        "w_sg_bd": w_sg_bd,
        "b_sg_rows_s": jnp.tile(jnp.repeat(b_sg[0][:, :t_len].T, SG_HEAD_DIM, axis=1), (n_batch, 1)),
        "w_out": w_out[0], "g_xattn": row(g_xattn[0]),
        "w_xq": w_xq[0], "w_xo": w_xo[0], "g_ffn": row(g_ffn[0]),
        "w_router": w_router, "b_router": b_router,
        "lower": jnp.tril(jnp.ones((n_s, n_s), BF16), -1),
    }

    k_p, v_p, p["k"], p["v"] = _memkv(mem_prompt[0], row(g_mem[0]), w_mk[0], w_mv[0])
    p["k_s"] = jnp.transpose(cache_mem_k[0].astype(BF16), (0, 2, 3, 1)).reshape(n_batch, D_MODEL, N_MEM)
    p["v_s"] = cache_mem_v[0].astype(BF16).reshape(n_batch, N_MEM, D_MODEL)

    assert n_p % n_s == 0
    x2_p, h3_p, logits_p, hist_p = _trunk_prompt(x_prompt[0], p)
    rt_p, cnt_t = _router(logits_p, ROUTE_RANK_ROWS)
    cnt_p = cnt_t[:, 0].reshape(1, LOGIT_LANES)
    x2_s, h3_s, rt_s, hist_s, sgv_s, cnt = _trunk_sample(
        x_sample.reshape(n_s, D_MODEL), state_conv[0], cnt_p, p, n_batch, t_len)

    experts = jnp.arange(N_EXPERTS, dtype=jnp.int32)
    w_e = (w_expert_gate[0], w_expert_up[0], w_expert_down[0])

    def moe_pass(cnt, h3_tables, rts, n_real):
        n_tot = sum(n_real)
        n_slots = -(-(n_tot * 2) // BM) * BM + N_EXPERTS * BM
        counts = cnt[0, :N_EXPERTS].astype(jnp.int32)
        padded = (counts + BM - 1) // BM * BM
        pad_start = jnp.cumsum(padded) - padded

        def one(e_row, rank_row):
            e = e_row.astype(jnp.int32)
            start = jnp.sum(jnp.where(e[None, :] == experts[:, None], pad_start[:, None], 0), axis=0)
            return start + rank_row.astype(jnp.int32)

        slots = [(one(rt[0], rt[4]), one(rt[1], rt[5])) for rt in rts]
        sa, sb, tag_bases, spare0, dest0 = [], [], [], n_slots, 0
        for tab, (a, b), n in zip(h3_tables, slots, n_real):
            n_spare = tab.shape[0] - n
            spare = spare0 + jnp.arange(n_spare, dtype=jnp.int32)
            sa.append(jnp.concatenate([a, spare]))
            sb.append(jnp.concatenate([b, spare + n_spare]))
            spare0 += 2 * n_spare
            tag_bases.append((dest0, dest0 + n))
            dest0 += 2 * n
        xs, tags = _scatter_rows2(tuple(h3_tables), tuple(sa), tuple(sb), tuple(tag_bases), spare0)
        ys, dest_blocks = _experts(xs, tags, n_slots, dest0, pad_start // BM, padded // BM, counts, *w_e)
        slot = jnp.arange(n_slots, dtype=jnp.int32)
        dest = dest_blocks.reshape(n_slots // BM, SUBLANES, BM)[:, 0, :].reshape(-1)
        dest = jnp.where(slot < jnp.sum(padded), dest, dest0 + slot)
        return _scatter_back(ys, dest, dest0 + n_slots)

    yg = moe_pass(cnt, [h3_p, h3_s], [rt_p, rt_s], [n_p, n_s])

    gf = row(g_final)
    y_p = _combine(x2_p, yg, rt_p, gf, TM_COMBINE, 0, n_p // TM_COMBINE)
    y_s = _combine(x2_s, yg, rt_s, gf, n_s, 2 * n_p // n_s, 2 * n_p // n_s + 1)

    return (y_p.reshape(1, n_p, D_MODEL),
            y_s.reshape(n_batch, t_len, D_MODEL),
            hist_p[HALO - HIST:].reshape(1, 1, HIST, D_CONV),
            hist_s.reshape(1, n_batch, HIST, D_CONV),
            k_p.reshape(1, 1, N_MEM, X_HEADS, X_HEAD_DIM),
            v_p.reshape(1, 1, N_MEM, X_HEADS, X_HEAD_DIM),
            sgv_s.reshape(1, n_batch, t_len, D_SG))
```

```python
import functools

import jax
import jax.numpy as jnp
from jax import lax
from jax.experimental import pallas as pl
from jax.experimental.pallas import tpu as pltpu
from jax.experimental.pallas import tpu_sc as plsc

D_MODEL = 1024
D_CONV = 512
D_SG = 512
CONV_WIDTH = 31
HIST = CONV_WIDTH - 1
SG_HEADS = 4
SG_HEAD_DIM = 128
SG_CHUNK = 128
N_MEM = 256
X_HEADS = 4
X_HEAD_DIM = 256
N_GROUPS = 4
EXPERTS_PER_GROUP = 8
N_EXPERTS = 32
D_EXPERT = 512
EPS = 1e-6

LANES = 128
SUBLANES = 8
SC_CORES = 2
SC_SUBCORES = 16
SC_WORKERS = SC_CORES * SC_SUBCORES
SC_LANES = 16
VMEM_LIMIT = 56 * 1024 * 1024

TM = 1024
TM_COMBINE = 2048
ROUTE_ROWS = 4096
ROUTE_RANK_ROWS = 512
HALO = 32
SEG = TM // SUBLANES
SEG_HALO = 32
CONV_BLOCK = 32
CAST_ROWS = 64
STAGE_ROWS = 256
BM = 256
X_LOOKAHEAD = 4
X_BUFS = X_LOOKAHEAD + 2
Y_BUFS = 4
ROW_DMA_PRIORITY = 1
SCATTER_CHUNK = 32
SCATTER_BUFS = 5
TAG_WORDS = 128
BACK_CHUNK = 56
BACK_BUFS = 4
LOGIT_LANES = 128

F32 = jnp.float32
BF16 = jnp.bfloat16


def _dot(a, b):
    return jnp.dot(a, b, preferred_element_type=F32)


def _rms(x, g):
    return x * lax.rsqrt(jnp.mean(x * x, axis=-1, keepdims=True) + EPS) * g


def _ln(x, g, b):
    mu = jnp.mean(x, axis=-1, keepdims=True)
    xc = x - mu
    var = jnp.mean(xc * xc, axis=-1, keepdims=True)
    return xc * lax.rsqrt(var + EPS) * g + b


def _sigmoid(x):
    return 1.0 / (1.0 + jnp.exp(-x))


def _pack_bf16_pairs(h):
    bits = lax.bitcast_convert_type(h, jnp.uint32)
    half = h.shape[1] // 2
    lo = lax.shift_right_logical(bits[:, :half], jnp.uint32(16))
    hi = bits[:, half:] & jnp.uint32(0xFFFF0000)
    return hi | lo


def _unpack_bf16_pairs_f32(p):
    lo = lax.bitcast_convert_type(lax.shift_left(p, jnp.uint32(16)), F32)
    hi = lax.bitcast_convert_type(p & jnp.uint32(0xFFFF0000), F32)
    return lo, hi


def _unpack_bf16_pairs(p):
    lo, hi = _unpack_bf16_pairs_f32(p)
    return lo.astype(BF16), hi.astype(BF16)


def _memkv_kernel(mem_ref, g_ref, wk_ref, wv_ref, k_ref, v_ref, kbf_ref, vbf_ref):
    m = _rms(mem_ref[...], g_ref[...]).astype(BF16)
    k = _dot(m, wk_ref[...].astype(BF16))
    v = _dot(m, wv_ref[...].astype(BF16))
    k_ref[...] = k
    v_ref[...] = v
    kbf_ref[...] = k.astype(BF16)
    vbf_ref[...] = v.astype(BF16)


def _memkv(mem, g_mem, w_mk, w_mv):
    return pl.pallas_call(
        _memkv_kernel,
        out_shape=(jax.ShapeDtypeStruct((N_MEM, D_MODEL), F32), jax.ShapeDtypeStruct((N_MEM, D_MODEL), F32),
                   jax.ShapeDtypeStruct((N_MEM, D_MODEL), BF16), jax.ShapeDtypeStruct((N_MEM, D_MODEL), BF16)),
        compiler_params=pltpu.CompilerParams(vmem_limit_bytes=VMEM_LIMIT),
        name="memkv",
    )(mem, g_mem, w_mk, w_mv)


def _attn_heads(q, k, v, k_transposed):
    outs = []
    for h in range(X_HEADS):
        sl = slice(h * X_HEAD_DIM, (h + 1) * X_HEAD_DIM)
        if k_transposed:
            s = _dot(q[:, sl], k[sl, :])
        else:
            s = lax.dot_general(q[:, sl], k[:, sl], (((1,), (1,)), ((), ())), preferred_element_type=F32)
        s = s * (X_HEAD_DIM ** -0.5)
        s = s - jnp.max(s, axis=-1, keepdims=True)
        p = jnp.exp(s)
        p = p / jnp.sum(p, axis=-1, keepdims=True)
        outs.append(_dot(p.astype(BF16), v[:, sl]).astype(BF16))
    return jnp.concatenate(outs, axis=1)


def _route(logits, run, strict_lower):
    m = logits.shape[0]
    r = strict_lower.shape[0]
    lane = lax.broadcasted_iota(jnp.int32, (m, LOGIT_LANES), 1).astype(F32)
    neg = jnp.float32(-jnp.inf)
    big = jnp.float32(LOGIT_LANES)

    def first_argmax(vals):
        mx = jnp.max(vals, axis=-1, keepdims=True)
        idx = jnp.min(jnp.where(vals == mx, lane, big), axis=-1, keepdims=True)
        return mx, idx

    lg = jnp.where(lane < N_GROUPS, logits, neg)
    g_max, g_idx = first_argmax(lg)
    g_w = 1.0 / jnp.sum(jnp.exp(lg - g_max), axis=-1, keepdims=True)

    lo = N_GROUPS + g_idx * EXPERTS_PER_GROUP
    le = jnp.where((lane >= lo) & (lane < lo + EXPERTS_PER_GROUP), logits, neg)
    v1, i1 = first_argmax(le)
    v2, i2 = first_argmax(jnp.where(lane == i1, neg, le))
    t = jnp.exp(v2 - v1)
    gate1 = g_w / (1.0 + t)
    gate2 = g_w * t / (1.0 + t)
    e1 = i1 - N_GROUPS
    e2 = i2 - N_GROUPS

    oh1 = (lane == e1).astype(F32)
    oh2 = (lane == e2).astype(F32)
    oh = oh1 + oh2
    befores = []
    for r0 in range(0, m, r):
        oh_r = oh[r0:r0 + r, :]
        befores.append(_dot(strict_lower, oh_r.astype(BF16)) + run)
        run = run + jnp.sum(oh_r, axis=0, keepdims=True)
    before = befores[0] if len(befores) == 1 else jnp.concatenate(befores, axis=0)
    rank1 = jnp.sum(before * oh1, axis=-1, keepdims=True)
    rank2 = jnp.sum(before * oh2, axis=-1, keepdims=True)
    new_run = run

    rinfo = jnp.where(lane == 0, e1,
            jnp.where(lane == 1, e2,
            jnp.where(lane == 2, gate1,
            jnp.where(lane == 3, gate2,
            jnp.where(lane == 4, rank1,
            jnp.where(lane == 5, rank2, 0.0))))))
    return jnp.transpose(rinfo)[0:SUBLANES, :], new_run


def _conv_segments(a, w_ref, seg_ref, tail_ref, yseg_ref, conv_ref):
    sub = lax.broadcasted_iota(jnp.int32, (SUBLANES, LANES), 0)
    for lt in range(D_CONV // LANES):
        ls = slice(lt * LANES, (lt + 1) * LANES)
        for t0 in range(0, TM, SUBLANES):
            s, m = divmod(t0, SEG)
            seg_ref[lt, pl.ds((SEG_HALO + m) * SUBLANES + s, SUBLANES, stride=SUBLANES), :] = a[t0:t0 + SUBLANES, ls]
        for j in range(SEG_HALO):
            cur = seg_ref[lt, (SEG + j) * SUBLANES:(SEG + j + 1) * SUBLANES, :]
            prev = tail_ref[lt, j * SUBLANES:(j + 1) * SUBLANES, :]
            seg_ref[lt, j * SUBLANES:(j + 1) * SUBLANES, :] = jnp.where(
                sub == 0, pltpu.roll(prev, 1, axis=0), pltpu.roll(cur, 1, axis=0))
            tail_ref[lt, j * SUBLANES:(j + 1) * SUBLANES, :] = cur
        for m0 in range(0, SEG, CONV_BLOCK):
            acc = [jnp.zeros((SUBLANES, LANES), F32) for _ in range(CONV_BLOCK)]
            for idx in range(m0 - HIST, m0 + CONV_BLOCK):
                b = seg_ref[lt, (SEG_HALO + idx) * SUBLANES:(SEG_HALO + idx + 1) * SUBLANES, :]
                for m in range(max(m0, idx), min(m0 + CONV_BLOCK, idx + CONV_WIDTH)):
                    k = idx - m + HIST
                    acc[m - m0] = acc[m - m0] + b * w_ref[k:k + 1, ls]
            for m in range(m0, m0 + CONV_BLOCK):
                yseg_ref[lt, m * SUBLANES:(m + 1) * SUBLANES, :] = acc[m - m0]
        for t0 in range(0, TM, SUBLANES):
            s, m = divmod(t0, SEG)
            conv_ref[t0:t0 + SUBLANES, ls] = yseg_ref[lt, pl.ds(m * SUBLANES + s, SUBLANES, stride=SUBLANES), :]


def _stage_cast(pairs, stage_ref, sem):
    work = [(src, dst, r0) for src, dst in pairs for r0 in range(0, src.shape[0], STAGE_ROWS)]

    def copy(k):
        src, _, r0 = work[k]
        return pltpu.make_async_copy(src.at[pl.ds(r0, STAGE_ROWS)],
                                     stage_ref.at[k % 2, :, pl.ds(0, src.shape[1])], sem.at[k % 2])

    copy(0).start()
    for k, (src, dst, r0) in enumerate(work):
        if k + 1 < len(work):
            copy(k + 1).start()
        copy(k).wait()

        def body(c, carry, k=k, src=src, dst=dst, r0=r0):
            c0 = pl.multiple_of(c * CAST_ROWS, CAST_ROWS)
            dst[pl.ds(r0 + c0, CAST_ROWS), :] = stage_ref[k % 2, pl.ds(c0, CAST_ROWS), 0:src.shape[1]].astype(BF16)
            return carry

        lax.fori_loop(0, STAGE_ROWS // CAST_ROWS, body, 0)


def _trunk_prompt_kernel(x_ref, gmix_ref, win32_ref, convw_ref, convb_ref, lncg_ref, lncb_ref, lnvg_ref, lnvb_ref,
                         wsg_ref, bsg_ref, wout32_ref, gx_ref, wxq32_ref, kmem_ref, v_ref, wxo32_ref, gffn_ref, wr_ref,
                         br_ref,
                         x2_ref, h3_ref, logit_ref, hist_ref,
                         seg_ref, tail_ref, yseg_ref, conv_ref, win_ref, wout_ref, wxq_ref, wxo_ref, stage_ref, stage_sem):
    i = pl.program_id(0)

    @pl.when(i == 0)
    def _():
        tail_ref[...] = jnp.zeros(tail_ref.shape, F32)
        _stage_cast([(win32_ref, win_ref), (wout32_ref, wout_ref), (wxq32_ref, wxq_ref), (wxo32_ref, wxo_ref)],
                    stage_ref, stage_sem)

    x = x_ref[...]
    h = _rms(x, gmix_ref[...]).astype(BF16)

    a_in = _dot(h, win_ref[:, 0:D_CONV])
    a_gate = _dot(h, win_ref[:, D_CONV:2 * D_CONV])
    a = a_in * _sigmoid(a_gate)
    hist_ref[...] = a[TM - HALO:, :]
    _conv_segments(a, convw_ref, seg_ref, tail_ref, yseg_ref, conv_ref)

    y = _ln(conv_ref[...] + convb_ref[...], lncg_ref[...], lncb_ref[...])
    a_out = (y * _sigmoid(y)).astype(BF16)

    u = _dot(h, win_ref[:, 2 * D_CONV:2 * D_CONV + D_SG])
    v = _ln(_dot(h, win_ref[:, 2 * D_CONV + D_SG:]), lnvg_ref[...], lnvb_ref[...]).astype(BF16)
    ri = lax.broadcasted_iota(jnp.int32, (SG_CHUNK, SG_CHUNK), 0)
    ci = lax.broadcasted_iota(jnp.int32, (SG_CHUNK, SG_CHUNK), 1)
    w_tril = [jnp.where(ci <= ri, wsg_ref[hh], 0.0).astype(BF16) for hh in range(SG_HEADS)]
    gate_rows = []
    for c in range(TM // SG_CHUNK):
        rs = slice(c * SG_CHUNK, (c + 1) * SG_CHUNK)
        heads = [_dot(w_tril[hh], v[rs, hh * SG_HEAD_DIM:(hh + 1) * SG_HEAD_DIM]) for hh in range(SG_HEADS)]
        gate_rows.append(jnp.concatenate(heads, axis=1) + bsg_ref[...])
    b_out = (u * jnp.concatenate(gate_rows, axis=0)).astype(BF16)

    x1 = x + _dot(a_out, wout_ref[0:D_CONV, :]) + _dot(b_out, wout_ref[D_CONV:, :])

    hx = _rms(x1, gx_ref[...]).astype(BF16)
    q = _dot(hx, wxq_ref[...]).astype(BF16)
    x2 = x1 + _dot(_attn_heads(q, kmem_ref[...], v_ref[...], False), wxo_ref[...])
    x2_ref[...] = x2

    h3 = _rms(x2, gffn_ref[...]).astype(BF16)
    h3_ref[...] = _pack_bf16_pairs(h3.astype(F32))
    logit_ref[...] = _dot(h3, wr_ref[...]) + br_ref[...]


def _router_kernel(logit_ref, upper_ref, ones_ref, rt_ref, cnt_ref, run_ref):
    @pl.when(pl.program_id(0) == 0)
    def _():
        run_ref[...] = jnp.zeros(run_ref.shape, F32)

    n = ROUTE_ROWS
    lt = jnp.transpose(logit_ref[...])
    neg = jnp.float32(-jnp.inf)
    big = jnp.float32(LOGIT_LANES)

    def first_argmax(vals, rows):
        mx = jnp.max(vals, axis=0, keepdims=True)
        idx = jnp.min(jnp.where(vals == mx, rows, big), axis=0, keepdims=True)
        return mx, idx

    row8 = lax.broadcasted_iota(jnp.int32, (SUBLANES, n), 0).astype(F32)
    lg = jnp.where(row8 < N_GROUPS, lt[0:SUBLANES, :], neg)
    g_max, g_idx = first_argmax(lg, row8)
    g_w = 1.0 / jnp.sum(jnp.exp(lg - g_max), axis=0, keepdims=True)

    n_rows = N_GROUPS + N_EXPERTS + (-(N_GROUPS + N_EXPERTS)) % SUBLANES
    rows = lax.broadcasted_iota(jnp.int32, (n_rows, n), 0).astype(F32)
    lo = N_GROUPS + g_idx * EXPERTS_PER_GROUP
    le = jnp.where((rows >= lo) & (rows < lo + EXPERTS_PER_GROUP), lt[0:n_rows, :], neg)
    v1, i1 = first_argmax(le, rows)
    v2, i2 = first_argmax(jnp.where(rows == i1, neg, le), rows)
    t = jnp.exp(v2 - v1)
    gate1 = g_w / (1.0 + t)
    gate2 = g_w * t / (1.0 + t)
    e1 = i1 - N_GROUPS
    e2 = i2 - N_GROUPS

    erow = lax.broadcasted_iota(jnp.int32, (LOGIT_LANES, n), 0).astype(F32)
    oh1 = (erow == e1).astype(F32)
    oh2 = (erow == e2).astype(F32)
    oh = (oh1 + oh2).astype(BF16)
    r = upper_ref.shape[0]
    run = run_ref[...]
    rank1, rank2 = [], []
    for c0 in range(0, n, r):
        cs = slice(c0, c0 + r)
        before = _dot(oh[:, cs], upper_ref[...]) + run
        rank1.append(jnp.sum(before * oh1[:, cs], axis=0, keepdims=True))
        rank2.append(jnp.sum(before * oh2[:, cs], axis=0, keepdims=True))
        run = run + _dot(oh[:, cs], ones_ref[...])
    run_ref[...] = run
    cnt_ref[...] = run[:, 0:LANES]

    sub = lax.broadcasted_iota(jnp.int32, (SUBLANES, n), 0)
    vals = (e1, e2, gate1, gate2, jnp.concatenate(rank1, axis=1), jnp.concatenate(rank2, axis=1))
    rt = jnp.zeros((SUBLANES, n), F32)
    for k, v in enumerate(vals):
        rt = jnp.where(sub == k, v, rt)
    rt_ref[...] = rt


def _router(logits, rank_block):
    n = logits.shape[0]
    assert n % ROUTE_ROWS == 0 and ROUTE_ROWS % rank_block == 0
    upper = jnp.triu(jnp.ones((rank_block, rank_block), BF16), 1)
    ones = jnp.ones((rank_block, rank_block), BF16)
    return pl.pallas_call(
        _router_kernel,
        grid=(n // ROUTE_ROWS,),
        in_specs=[pl.BlockSpec((ROUTE_ROWS, LOGIT_LANES), lambda i: (i, 0)),
                  pl.BlockSpec(upper.shape, lambda i: (0, 0)),
                  pl.BlockSpec(ones.shape, lambda i: (0, 0))],
        out_specs=(pl.BlockSpec((SUBLANES, ROUTE_ROWS), lambda i: (0, i)),
                   pl.BlockSpec((LOGIT_LANES, LANES), lambda i: (0, 0))),
        out_shape=(jax.ShapeDtypeStruct((SUBLANES, n), F32), jax.ShapeDtypeStruct((LOGIT_LANES, LANES), F32)),
        scratch_shapes=[pltpu.VMEM((LOGIT_LANES, rank_block), F32)],
        compiler_params=pltpu.CompilerParams(dimension_semantics=("arbitrary",), vmem_limit_bytes=VMEM_LIMIT),
        name="router",
    )(logits, upper, ones)


def _const_spec(shape):
    nd = len(shape)
    return pl.BlockSpec(shape, lambda i: (0,) * nd, pipeline_mode=pl.Buffered(1))


def _trunk_prompt(x, p):
    n = x.shape[0]
    assert n % TM == 0
    row = lambda w: pl.BlockSpec((TM, w), lambda i: (i, 0))
    consts = [p["g_mix"], p["w_in"], p["conv_w"], p["conv_b"], p["ln_conv_g"], p["ln_conv_b"], p["ln_v_g"],
              p["ln_v_b"], p["w_sg"], p["b_sg_rows"], p["w_out"], p["g_xattn"], p["w_xq"], p["k"], p["v"],
              p["w_xo"], p["g_ffn"], p["w_router"], p["b_router"]]
    staged = (p["w_in"], p["w_out"], p["w_xq"], p["w_xo"])
    spec = lambda c: pl.BlockSpec(memory_space=pl.ANY) if any(c is s for s in staged) else _const_spec(c.shape)
    return pl.pallas_call(
        _trunk_prompt_kernel,
        grid=(n // TM,),
        in_specs=[row(D_MODEL)] + [spec(c) for c in consts],
        out_specs=(row(D_MODEL), row(D_MODEL // 2), row(LOGIT_LANES),
                   pl.BlockSpec((HALO, D_CONV), lambda i: (0, 0))),
        out_shape=(jax.ShapeDtypeStruct((n, D_MODEL), F32),
                   jax.ShapeDtypeStruct((n, D_MODEL // 2), jnp.uint32),
                   jax.ShapeDtypeStruct((n, LOGIT_LANES), F32),
                   jax.ShapeDtypeStruct((HALO, D_CONV), F32)),
        scratch_shapes=[pltpu.VMEM((D_CONV // LANES, (SEG_HALO + SEG) * SUBLANES, LANES), F32),
                        pltpu.VMEM((D_CONV // LANES, SEG_HALO * SUBLANES, LANES), F32),
                        pltpu.VMEM((D_CONV // LANES, TM, LANES), F32),
                        pltpu.VMEM((TM, D_CONV), F32),
                        pltpu.VMEM(p["w_in"].shape, BF16), pltpu.VMEM(p["w_out"].shape, BF16),
                        pltpu.VMEM(p["w_xq"].shape, BF16), pltpu.VMEM(p["w_xo"].shape, BF16),
                        pltpu.VMEM((2, STAGE_ROWS, max(s.shape[1] for s in staged)), F32),
                        pltpu.SemaphoreType.DMA((2,))],
        compiler_params=pltpu.CompilerParams(dimension_semantics=("arbitrary",), vmem_limit_bytes=VMEM_LIMIT),
        name="trunk_prompt",
    )(x, *consts)


def _trunk_sample_kernel(n_batch, t_len,
                         x_ref, hist_in_ref, run_in_ref, gmix_ref, win_ref, convw_ref, convb_ref, lncg_ref, lncb_ref,
                         lnvg_ref, lnvb_ref, wsgbd_ref, bsg_ref, wout_ref, gx_ref, wxq_ref, kmem_ref, v_ref, wxo_ref,
                         gffn_ref, wr_ref, br_ref, lower_ref,
                         x2_ref, h3_ref, rt_ref, hist_ref, sgv_ref, cnt_ref,
                         ext_ref, conv_ref, att_ref):
    x = x_ref[...]
    h = _rms(x, gmix_ref[...]).astype(BF16)
    z = _dot(h, win_ref[...].astype(BF16))
    a = z[:, 0:D_CONV] * _sigmoid(z[:, D_CONV:2 * D_CONV])
    ext_len = HIST + t_len
    for b in range(n_batch):
        ext_ref[b, 0:HIST, :] = hist_in_ref[b]
        ext_ref[b, HIST:ext_len, :] = a[b * t_len:(b + 1) * t_len, :]
    for b in range(n_batch):
        acc = jnp.zeros((t_len, D_CONV), F32)
        for k in range(CONV_WIDTH):
            acc = acc + ext_ref[b, k:k + t_len, :] * convw_ref[k:k + 1, :]
        conv_ref[b * t_len:(b + 1) * t_len, :] = acc
        hist_ref[b] = ext_ref[b, ext_len - HIST:ext_len, :]

    y = _ln(conv_ref[...] + convb_ref[...], lncg_ref[...], lncb_ref[...])
    a_out = (y * _sigmoid(y)).astype(BF16)

    u = z[:, 2 * D_CONV:2 * D_CONV + D_SG]
    v = _ln(z[:, 2 * D_CONV + D_SG:], lnvg_ref[...], lnvb_ref[...])
    sgv_ref[...] = v
    vb = v.astype(BF16)
    heads = [_dot(wsgbd_ref[hh], vb[:, hh * SG_HEAD_DIM:(hh + 1) * SG_HEAD_DIM]) for hh in range(SG_HEADS)]
    b_out = (u * (jnp.concatenate(heads, axis=1) + bsg_ref[...])).astype(BF16)

    x1 = (x + _dot(a_out, wout_ref[0:D_CONV, :].astype(BF16))
          + _dot(b_out, wout_ref[D_CONV:, :].astype(BF16)))

    hx = _rms(x1, gx_ref[...]).astype(BF16)
    q = _dot(hx, wxq_ref[...].astype(BF16)).astype(BF16)
    for b in range(n_batch):
        rs = slice(b * t_len, (b + 1) * t_len)
        att_ref[rs, :] = _attn_heads(q[rs, :], kmem_ref[b], v_ref[b], True)
    x2 = x1 + _dot(att_ref[...], wxo_ref[...].astype(BF16))
    x2_ref[...] = x2

    h3 = _rms(x2, gffn_ref[...]).astype(BF16)
    m = n_batch * t_len
    h3_ref[0:m, :] = _pack_bf16_pairs(h3.astype(F32))
    if h3_ref.shape[0] > m:
        h3_ref[m:, :] = jnp.zeros((h3_ref.shape[0] - m, D_MODEL // 2), jnp.uint32)
    rt, new_run = _route(_dot(h3, wr_ref[...]) + br_ref[...], run_in_ref[...], lower_ref[...])
    rt_ref[...] = rt
    cnt_ref[...] = new_run


def _trunk_sample(x, hist, run, p, n_batch, t_len):
    m = n_batch * t_len
    args = [x, hist, run, p["g_mix"], p["w_in"], p["conv_w"], p["conv_b"], p["ln_conv_g"], p["ln_conv_b"],
            p["ln_v_g"], p["ln_v_b"], p["w_sg_bd"], p["b_sg_rows_s"], p["w_out"], p["g_xattn"], p["w_xq"],
            p["k_s"], p["v_s"], p["w_xo"], p["g_ffn"], p["w_router"], p["b_router"], p["lower"]]
    return pl.pallas_call(
        functools.partial(_trunk_sample_kernel, n_batch, t_len),
        out_shape=(jax.ShapeDtypeStruct((m, D_MODEL), F32),
                   jax.ShapeDtypeStruct((-(-m // (SC_WORKERS * SUBLANES)) * SC_WORKERS * SUBLANES, D_MODEL // 2),
                                        jnp.uint32),
                   jax.ShapeDtypeStruct((SUBLANES, m), F32),
                   jax.ShapeDtypeStruct((n_batch, HIST, D_CONV), F32),
                   jax.ShapeDtypeStruct((m, D_SG), F32),
                   jax.ShapeDtypeStruct((1, LOGIT_LANES), F32)),
        scratch_shapes=[pltpu.VMEM((n_batch, HIST + t_len, D_CONV), F32),
                        pltpu.VMEM((m, D_CONV), F32),
                        pltpu.VMEM((m, D_MODEL), BF16)],
        compiler_params=pltpu.CompilerParams(vmem_limit_bytes=VMEM_LIMIT),
        name="trunk_sample",
    )(*args)


def _sc_worker_id():
    return lax.axis_index("s") * SC_CORES + lax.axis_index("c")


def _sc_chunk(per_w, max_chunk):
    assert per_w % SUBLANES == 0 and max_chunk <= LANES
    return max(c for c in range(SUBLANES, max_chunk + 1, SUBLANES) if per_w % c == 0)


def _sc_scatter_rows2(tables, slots_a, slots_b, tag_bases, n_rows_out, max_chunk):
    d, dtype = tables[0].shape[1], tables[0].dtype
    plans = []
    for t in tables:
        per_w = t.shape[0] // SC_WORKERS
        assert per_w * SC_WORKERS == t.shape[0]
        chunk = _sc_chunk(per_w, max_chunk)
        plans.append((per_w, chunk, per_w // chunk))
    cmax = max(c for _, c, _ in plans)
    n_t = len(tables)
    mesh = plsc.VectorSubcoreMesh(core_axis_name="c", subcore_axis_name="s")

    nb = SCATTER_BUFS
    lag = 2
    scratch = []
    for _, chunk, _ in plans:
        for _ in range(nb):
            scratch += [pltpu.VMEM((chunk,), jnp.int32), pltpu.VMEM((chunk,), jnp.int32)]
    scratch += [pltpu.VMEM((cmax, d), dtype)] * nb
    scratch += [pltpu.VMEM((cmax, TAG_WORDS), jnp.int32)] * (2 * nb)
    scratch += [pltpu.SemaphoreType.DMA] * (2 * nb)

    @functools.partial(pl.kernel, mesh=mesh,
                       out_type=(jax.ShapeDtypeStruct((n_rows_out, d), dtype),
                                 jax.ShapeDtypeStruct((n_rows_out, TAG_WORDS), jnp.int32)),
                       scratch_types=scratch)
    def scatter(*refs):
        tab_hbm = refs[0:n_t]
        sa_hbm = refs[n_t:2 * n_t]
        sb_hbm = refs[2 * n_t:3 * n_t]
        out_hbm, tag_hbm = refs[3 * n_t], refs[3 * n_t + 1]
        sc = refs[3 * n_t + 2:]
        idx_refs = sc[:2 * nb * n_t]
        rows = sc[2 * nb * n_t:2 * nb * n_t + nb]
        tagbufs = sc[2 * nb * n_t + nb:2 * nb * n_t + 3 * nb]
        lsem = sc[2 * nb * n_t + 3 * nb:2 * nb * n_t + 4 * nb]
        ssem = sc[2 * nb * n_t + 4 * nb:]
        wid = _sc_worker_id()

        work = []
        for t, (per_w, chunk, n_chunks) in enumerate(plans):
            for j in range(n_chunks):
                work.append((t, wid * per_w + j * chunk, chunk))

        def parts(k):
            t, off, chunk = work[k]
            b = k % nb
            ia, ib = idx_refs[2 * nb * t + 2 * b], idx_refs[2 * nb * t + 2 * b + 1]
            full = chunk == cmax
            rv = rows[b] if full else rows[b].at[pl.ds(0, chunk)]
            ta = tagbufs[2 * b] if full else tagbufs[2 * b].at[pl.ds(0, chunk)]
            tb = tagbufs[2 * b + 1] if full else tagbufs[2 * b + 1].at[pl.ds(0, chunk)]
            return t, off, chunk, b, ia, ib, rv, ta, tb

        def start_load(k):
            t, off, chunk, b, ia, ib, rv, ta, tb = parts(k)
            return (pltpu.async_copy(tab_hbm[t].at[pl.ds(off, chunk)], rv, lsem[b]),
                    pltpu.async_copy(sa_hbm[t].at[pl.ds(off, chunk)], ia, lsem[b]),
                    pltpu.async_copy(sb_hbm[t].at[pl.ds(off, chunk)], ib, lsem[b]))

        def start_scatter(k):
            t, off, chunk, b, ia, ib, rv, ta, tb = parts(k)
            base_a, base_b = tag_bases[t]
            for r in range(chunk):
                row_id = (off + r).astype(jnp.int32)
                tagbufs[2 * b][r, pl.ds(0, SC_LANES)] = jnp.zeros((SC_LANES,), jnp.int32) + (base_a + row_id)
                tagbufs[2 * b + 1][r, pl.ds(0, SC_LANES)] = jnp.zeros((SC_LANES,), jnp.int32) + (base_b + row_id)
            return (pltpu.async_copy(rv, out_hbm.at[ia], ssem[b]), pltpu.async_copy(rv, out_hbm.at[ib], ssem[b]),
                    pltpu.async_copy(ta, tag_hbm.at[ia], ssem[b]), pltpu.async_copy(tb, tag_hbm.at[ib], ssem[b]))

        loads, scatters = {}, {}
        for k in range(len(work) + lag):
            if k < len(work):
                if k >= nb:
                    for c in scatters.pop(k - nb):
                        c.wait()
                loads[k] = start_load(k)
            w = k - lag
            if w >= 0:
                for c in loads.pop(w):
                    c.wait()
                scatters[w] = start_scatter(w)
        for w in sorted(scatters):
            for c in scatters[w]:
                c.wait()

    return scatter(*tables, *slots_a, *slots_b)


def _sc_scatter_back(ys, dest, n_rows_out):
    n_rows, d = ys.shape
    per_w = n_rows // SC_WORKERS
    assert per_w * SC_WORKERS == n_rows
    chunk = _sc_chunk(per_w, BACK_CHUNK)
    n_chunks = per_w // chunk
    nb = BACK_BUFS
    lag = 1
    mesh = plsc.VectorSubcoreMesh(core_axis_name="c", subcore_axis_name="s")

    @functools.partial(
        pl.kernel, mesh=mesh,
        out_type=jax.ShapeDtypeStruct((n_rows_out, d), ys.dtype),
        scratch_types=([pltpu.VMEM((chunk,), jnp.int32)] * nb + [pltpu.VMEM((chunk, d), ys.dtype)] * nb
                       + [pltpu.SemaphoreType.DMA] * (2 * nb)),
    )
    def scatter_back(ys_hbm, dest_hbm, out_hbm, *rest):
        idx = rest[:nb]
        rows = rest[nb:2 * nb]
        lsem = rest[2 * nb:3 * nb]
        ssem = rest[3 * nb:]
        base = _sc_worker_id() * per_w

        loads, scatters = {}, {}
        for k in range(n_chunks + lag):
            if k < n_chunks:
                b = k % nb
                if k >= nb:
                    scatters.pop(k - nb).wait()
                off = base + k * chunk
                loads[k] = (pltpu.async_copy(ys_hbm.at[pl.ds(off, chunk)], rows[b], lsem[b]),
                            pltpu.async_copy(dest_hbm.at[pl.ds(off, chunk)], idx[b], lsem[b]))
            w = k - lag
            if w >= 0:
                b = w % nb
                for c in loads.pop(w):
                    c.wait()
                scatters[w] = pltpu.async_copy(rows[b], out_hbm.at[idx[b]], ssem[b])
        for w in sorted(scatters):
            scatters[w].wait()

    return scatter_back(ys, dest)


def _experts_kernel(dump_base, first_ref, nblk_ref, cnt_ref, tot_ref, xs_hbm, tag_hbm, wg_ref, wu_ref, wd_ref,
                    ys_hbm, dest_hbm, xbuf, tbuf, ybuf, dbuf, wg_bf, wu_bf, wd_bf, in_sem, tin_sem, out_sem, dout_sem):
    e = pl.program_id(0)
    nb = nblk_ref[e]
    first = first_ref[e]
    cnt = cnt_ref[e]
    total = tot_ref[0]
    half = D_MODEL // 2

    def in_copies(gb):
        slot = lax.rem(gb, X_BUFS)
        return (pltpu.make_async_copy(xs_hbm.at[pl.ds(gb * BM, BM)], xbuf.at[slot], in_sem.at[slot]),
                pltpu.make_async_copy(tag_hbm.at[pl.ds(gb * BM, BM)], tbuf.at[slot], tin_sem.at[slot]))

    def out_copies(gb):
        slot = lax.rem(gb, Y_BUFS)
        return (pltpu.make_async_copy(ybuf.at[slot], ys_hbm.at[pl.ds(gb * BM, BM)], out_sem.at[slot]),
                pltpu.make_async_copy(dbuf.at[slot], dest_hbm.at[pl.ds(gb * SUBLANES, SUBLANES)], dout_sem.at[slot]))

    def start_in(gb):
        for c in in_copies(gb):
            c.start(priority=ROW_DMA_PRIORITY)

    def start_out(gb):
        for c in out_copies(gb):
            c.start(priority=ROW_DMA_PRIORITY)

    def wait_out(gb):
        for c in out_copies(gb):
            c.wait()

    @pl.when(nb > 0)
    def _():
        @pl.when(first == 0)
        def _():
            for k in range(X_LOOKAHEAD):
                @pl.when(k < total)
                def _():
                    start_in(k)

        wg_bf[...] = wg_ref[0].astype(BF16)
        wu_bf[...] = wu_ref[0].astype(BF16)
        wd_bf[...] = wd_ref[0].astype(BF16)

        def acquire(gb):
            @pl.when(gb + X_LOOKAHEAD < total)
            def _():
                start_in(gb + X_LOOKAHEAD)

            for c in in_copies(gb):
                c.wait()

            @pl.when(gb >= Y_BUFS)
            def _():
                wait_out(gb - Y_BUFS)

        def ffn(gb, j):
            n_live = cnt - j * BM
            tags_t = jnp.transpose(tbuf[lax.rem(gb, X_BUFS)].astype(F32))
            lane = lax.broadcasted_iota(jnp.int32, (SUBLANES, BM), 1)
            own = (dump_base + gb * BM + lane).astype(F32)
            dest = jnp.where(lane < n_live, jnp.broadcast_to(tags_t[0:1, :], (SUBLANES, BM)), own)
            dbuf[lax.rem(gb, Y_BUFS)] = dest.astype(jnp.int32)
            live = lax.broadcasted_iota(jnp.int32, (BM, half), 0) < n_live
            lo, hi = _unpack_bf16_pairs(jnp.where(live, xbuf[lax.rem(gb, X_BUFS)], jnp.uint32(0)))
            g = _dot(lo, wg_bf[0:half, :]) + _dot(hi, wg_bf[half:, :])
            u = _dot(lo, wu_bf[0:half, :]) + _dot(hi, wu_bf[half:, :])
            hm = (g * _sigmoid(g) * u).astype(BF16)
            y = _dot(hm, wd_bf[...])
            ybuf[lax.rem(gb, Y_BUFS)] = _pack_bf16_pairs(y.astype(BF16).astype(F32))

        def block_pair(jp, carry):
            j0 = 2 * jp
            g0 = first + j0
            acquire(g0)
            acquire(g0 + 1)
            ffn(g0, j0)
            ffn(g0 + 1, j0 + 1)
            start_out(g0)
            start_out(g0 + 1)
            return carry

        lax.fori_loop(0, nb // 2, block_pair, 0)

        @pl.when(lax.rem(nb, 2) == 1)
        def _():
            gl = first + nb - 1
            acquire(gl)
            ffn(gl, nb - 1)
            start_out(gl)

        @pl.when(first + nb == total)
        def _():
            for k in range(Y_BUFS):
                @pl.when(total - 1 - k >= 0)
                def _():
                    wait_out(total - 1 - k)


def _experts(xs, tags, n_rows_out, dump_base, first_block, n_blocks_e, counts, w_eg, w_eu, w_ed):
    w_map = lambda e, fb, nb, ct, tot: (e, 0, 0)
    half = D_MODEL // 2
    total = jnp.sum(n_blocks_e).astype(jnp.int32).reshape(1)
    n_blocks = n_rows_out // BM
    return pl.pallas_call(
        functools.partial(_experts_kernel, dump_base),
        grid_spec=pltpu.PrefetchScalarGridSpec(
            num_scalar_prefetch=4,
            grid=(N_EXPERTS,),
            in_specs=[pl.BlockSpec(memory_space=pl.ANY),
                      pl.BlockSpec(memory_space=pl.ANY),
                      pl.BlockSpec((1, D_MODEL, D_EXPERT), w_map),
                      pl.BlockSpec((1, D_MODEL, D_EXPERT), w_map),
                      pl.BlockSpec((1, D_EXPERT, D_MODEL), w_map)],
            out_specs=(pl.BlockSpec(memory_space=pl.ANY), pl.BlockSpec(memory_space=pl.ANY)),
            scratch_shapes=[pltpu.VMEM((X_BUFS, BM, half), jnp.uint32), pltpu.VMEM((X_BUFS, BM, TAG_WORDS), jnp.int32),
                            pltpu.VMEM((Y_BUFS, BM, half), jnp.uint32), pltpu.VMEM((Y_BUFS, SUBLANES, BM), jnp.int32),
                            pltpu.VMEM((D_MODEL, D_EXPERT), BF16), pltpu.VMEM((D_MODEL, D_EXPERT), BF16),
                            pltpu.VMEM((D_EXPERT, D_MODEL), BF16),
                            pltpu.SemaphoreType.DMA((X_BUFS,)), pltpu.SemaphoreType.DMA((X_BUFS,)),
                            pltpu.SemaphoreType.DMA((Y_BUFS,)), pltpu.SemaphoreType.DMA((Y_BUFS,))]),
        out_shape=(jax.ShapeDtypeStruct((n_rows_out, half), jnp.uint32),
                   jax.ShapeDtypeStruct((n_blocks * SUBLANES, BM), jnp.int32)),
        compiler_params=pltpu.CompilerParams(dimension_semantics=("arbitrary",), vmem_limit_bytes=VMEM_LIMIT),
        name="experts",
    )(first_block, n_blocks_e, counts, total, xs, tags, w_eg, w_eu, w_ed)


def _combine_kernel(x2_ref, y1_ref, y2_ref, rt_ref, g_ref, o_ref):
    rt = rt_ref[...]
    r = jnp.transpose(jnp.concatenate([rt, jnp.zeros((LANES - rt.shape[0], rt.shape[1]), F32)], axis=0))
    g1, g2 = r[:, 2:3], r[:, 3:4]
    half = D_MODEL // 2
    y1_lo, y1_hi = _unpack_bf16_pairs_f32(y1_ref[...])
    y2_lo, y2_hi = _unpack_bf16_pairs_f32(y2_ref[...])
    x_lo = x2_ref[:, 0:half] + g1 * y1_lo + g2 * y2_lo
    x_hi = x2_ref[:, half:] + g1 * y1_hi + g2 * y2_hi
    ms = (jnp.sum(x_lo * x_lo, axis=-1, keepdims=True) + jnp.sum(x_hi * x_hi, axis=-1, keepdims=True)) / D_MODEL
    inv = lax.rsqrt(ms + EPS)
    o_ref[:, 0:half] = x_lo * inv * g_ref[:, 0:half]
    o_ref[:, half:] = x_hi * inv * g_ref[:, half:]


def _combine(x2, yg, rt, g_final, tm, blk1, blk2):
    n = x2.shape[0]
    return pl.pallas_call(
        _combine_kernel,
        grid=(n // tm,),
        in_specs=[pl.BlockSpec((tm, D_MODEL), lambda i: (i, 0)),
                  pl.BlockSpec((tm, D_MODEL // 2), lambda i: (blk1 + i, 0)),
                  pl.BlockSpec((tm, D_MODEL // 2), lambda i: (blk2 + i, 0)),
                  pl.BlockSpec((SUBLANES, tm), lambda i: (0, i)),
                  pl.BlockSpec((1, D_MODEL), lambda i: (0, 0))],
        out_specs=pl.BlockSpec((tm, D_MODEL), lambda i: (i, 0)),
        out_shape=jax.ShapeDtypeStruct((n, D_MODEL), F32),
        compiler_params=pltpu.CompilerParams(dimension_semantics=("arbitrary",), vmem_limit_bytes=VMEM_LIMIT),
        name="combine",
    )(x2, yg, yg, rt, g_final)


def _scatter_back(ys, dest, n_rows_out):
    return _sc_scatter_back(ys, dest, n_rows_out)


def _scatter_rows2(tables, slots_a, slots_b, tag_bases, n_rows_out):
    return _sc_scatter_rows2(tables, slots_a, slots_b, tag_bases, n_rows_out, SCATTER_CHUNK)


def kernel(x_prompt, x_sample, mem_prompt, state_conv, cache_mem_k, cache_mem_v, g_mix, w_in, conv_w, conv_b, ln_conv_g, ln_conv_b, ln_v_g, ln_v_b, w_sg, b_sg, w_out, g_mem, w_mk, w_mv, g_xattn, w_xq, w_xo, g_ffn, w_router_group, b_router_group, w_router_expert, b_router_expert, w_expert_gate, w_expert_up, w_expert_down, g_final):
    assert x_prompt.shape[0] == 1 and g_mix.shape[0] == 1
    n_p = x_prompt.shape[1]
    n_batch, t_len = x_sample.shape[0], x_sample.shape[1]
    n_s = n_batch * t_len
    row = lambda a: a.reshape(1, -1)

    w_router = jnp.concatenate(
        [w_router_group[0], jnp.transpose(w_router_expert[0], (1, 0, 2)).reshape(D_MODEL, N_EXPERTS)], axis=1)
    w_router = jnp.pad(w_router, ((0, 0), (0, LOGIT_LANES - w_router.shape[1]))).astype(BF16)
    b_router = jnp.pad(jnp.concatenate([b_router_group[0], b_router_expert[0].reshape(-1)]),
                       (0, LOGIT_LANES - N_GROUPS - N_EXPERTS)).reshape(1, LOGIT_LANES)
    tril_t = jnp.tril(jnp.ones((t_len, t_len), bool))
    w_sg_t = jnp.where(tril_t, w_sg[0][:, :t_len, :t_len], 0.0)
    eye_b = jnp.eye(n_batch, dtype=F32)
    w_sg_bd = jnp.einsum("ab,hij->haibj", eye_b, w_sg_t).reshape(SG_HEADS, n_s, n_s).astype(BF16)
    p = {
        "g_mix": row(g_mix[0]), "w_in": w_in[0],
        "conv_w": jnp.pad(conv_w[0], ((0, 1), (0, 0))), "conv_b": row(conv_b[0]),
        "ln_conv_g": row(ln_conv_g[0]), "ln_conv_b": row(ln_conv_b[0]),
        "ln_v_g": row(ln_v_g[0]), "ln_v_b": row(ln_v_b[0]),
        "w_sg": w_sg[0],
        "b_sg_rows": jnp.repeat(b_sg[0].T, SG_HEAD_DIM, axis=1),
        "w_sg_bd": w_sg_bd,
        "b_sg_rows_s": jnp.tile(jnp.repeat(b_sg[0][:, :t_len].T, SG_HEAD_DIM, axis=1), (n_batch, 1)),
        "w_out": w_out[0], "g_xattn": row(g_xattn[0]),
        "w_xq": w_xq[0], "w_xo": w_xo[0], "g_ffn": row(g_ffn[0]),
        "w_router": w_router, "b_router": b_router,
        "lower": jnp.tril(jnp.ones((n_s, n_s), BF16), -1),
    }

    k_p, v_p, p["k"], p["v"] = _memkv(mem_prompt[0], row(g_mem[0]), w_mk[0], w_mv[0])
    p["k_s"] = jnp.transpose(cache_mem_k[0].astype(BF16), (0, 2, 3, 1)).reshape(n_batch, D_MODEL, N_MEM)
    p["v_s"] = cache_mem_v[0].astype(BF16).reshape(n_batch, N_MEM, D_MODEL)

    assert n_p % n_s == 0
    x2_p, h3_p, logits_p, hist_p = _trunk_prompt(x_prompt[0], p)
    rt_p, cnt_t = _router(logits_p, ROUTE_RANK_ROWS)
    cnt_p = cnt_t[:, 0].reshape(1, LOGIT_LANES)
    x2_s, h3_s, rt_s, hist_s, sgv_s, cnt = _trunk_sample(
        x_sample.reshape(n_s, D_MODEL), state_conv[0], cnt_p, p, n_batch, t_len)

    experts = jnp.arange(N_EXPERTS, dtype=jnp.int32)
    w_e = (w_expert_gate[0], w_expert_up[0], w_expert_down[0])

    def moe_pass(cnt, h3_tables, rts, n_real):
        n_tot = sum(n_real)
        n_slots = -(-(n_tot * 2) // BM) * BM + N_EXPERTS * BM
        counts = cnt[0, :N_EXPERTS].astype(jnp.int32)
        padded = (counts + BM - 1) // BM * BM
        pad_start = jnp.cumsum(padded) - padded

        def one(e_row, rank_row):
            e = e_row.astype(jnp.int32)
            start = jnp.sum(jnp.where(e[None, :] == experts[:, None], pad_start[:, None], 0), axis=0)
            return start + rank_row.astype(jnp.int32)

        slots = [(one(rt[0], rt[4]), one(rt[1], rt[5])) for rt in rts]
        sa, sb, tag_bases, spare0, dest0 = [], [], [], n_slots, 0
        for tab, (a, b), n in zip(h3_tables, slots, n_real):
            n_spare = tab.shape[0] - n
            spare = spare0 + jnp.arange(n_spare, dtype=jnp.int32)
            sa.append(jnp.concatenate([a, spare]))
            sb.append(jnp.concatenate([b, spare + n_spare]))
            spare0 += 2 * n_spare
            tag_bases.append((dest0, dest0 + n))
            dest0 += 2 * n
        xs, tags = _scatter_rows2(tuple(h3_tables), tuple(sa), tuple(sb), tuple(tag_bases), spare0)
        ys, dest_blocks = _experts(xs, tags, n_slots, dest0, pad_start // BM, padded // BM, counts, *w_e)
        slot = jnp.arange(n_slots, dtype=jnp.int32)
        dest = dest_blocks.reshape(n_slots // BM, SUBLANES, BM)[:, 0, :].reshape(-1)
        dest = jnp.where(slot < jnp.sum(padded), dest, dest0 + slot)
        return _scatter_back(ys, dest, dest0 + n_slots)

    yg = moe_pass(cnt, [h3_p, h3_s], [rt_p, rt_s], [n_p, n_s])

    gf = row(g_final)
    y_p = _combine(x2_p, yg, rt_p, gf, TM_COMBINE, 0, n_p // TM_COMBINE)
    y_s = _combine(x2_s, yg, rt_s, gf, n_s, 2 * n_p // n_s, 2 * n_p // n_s + 1)

    return (y_p.reshape(1, n_p, D_MODEL),
            y_s.reshape(n_batch, t_len, D_MODEL),
            hist_p[HALO - HIST:].reshape(1, 1, HIST, D_CONV),
            hist_s.reshape(1, n_batch, HIST, D_CONV),
            k_p.reshape(1, 1, N_MEM, X_HEADS, X_HEAD_DIM),
            v_p.reshape(1, 1, N_MEM, X_HEADS, X_HEAD_DIM),
            sgv_s.reshape(1, n_batch, t_len, D_SG))
```

```python
import functools

import jax
import jax.numpy as jnp
from jax import lax
from jax.experimental import pallas as pl
from jax.experimental.pallas import tpu as pltpu
from jax.experimental.pallas import tpu_sc as plsc

D_MODEL = 1024
D_CONV = 512
D_SG = 512
CONV_WIDTH = 31
HIST = CONV_WIDTH - 1
SG_HEADS = 4
SG_HEAD_DIM = 128
SG_CHUNK = 128
N_MEM = 256
X_HEADS = 4
X_HEAD_DIM = 256
N_GROUPS = 4
EXPERTS_PER_GROUP = 8
N_EXPERTS = 32
D_EXPERT = 512
EPS = 1e-6

LANES = 128
SUBLANES = 8
SC_CORES = 2
SC_SUBCORES = 16
SC_WORKERS = SC_CORES * SC_SUBCORES
SC_LANES = 16
VMEM_LIMIT = 56 * 1024 * 1024

TM = 1024
TM_COMBINE = 2048
ROUTE_ROWS = 4096
ROUTE_RANK_ROWS = 512
HALO = 32
SEG = TM // SUBLANES
SEG_HALO = 32
CONV_BLOCK = 32
CAST_ROWS = 64
STAGE_ROWS = 256
BM = 256
X_LOOKAHEAD = 4
X_BUFS = X_LOOKAHEAD + 2
Y_BUFS = 4
ROW_DMA_PRIORITY = 1
SCATTER_CHUNK = 32
SCATTER_BUFS = 5
TAG_WORDS = 128
BACK_CHUNK = 56
BACK_BUFS = 4
LOGIT_LANES = 128

F32 = jnp.float32
BF16 = jnp.bfloat16


def _dot(a, b):
    return jnp.dot(a, b, preferred_element_type=F32)


def _rms(x, g):
    return x * lax.rsqrt(jnp.mean(x * x, axis=-1, keepdims=True) + EPS) * g


def _ln(x, g, b):
    mu = jnp.mean(x, axis=-1, keepdims=True)
    xc = x - mu
    var = jnp.mean(xc * xc, axis=-1, keepdims=True)
    return xc * lax.rsqrt(var + EPS) * g + b


def _sigmoid(x):
    return 1.0 / (1.0 + jnp.exp(-x))


def _pack_bf16_pairs(h):
    bits = lax.bitcast_convert_type(h, jnp.uint32)
    half = h.shape[1] // 2
    lo = lax.shift_right_logical(bits[:, :half], jnp.uint32(16))
    hi = bits[:, half:] & jnp.uint32(0xFFFF0000)
    return hi | lo


def _unpack_bf16_pairs_f32(p):
    lo = lax.bitcast_convert_type(lax.shift_left(p, jnp.uint32(16)), F32)
    hi = lax.bitcast_convert_type(p & jnp.uint32(0xFFFF0000), F32)
    return lo, hi


def _unpack_bf16_pairs(p):
    lo, hi = _unpack_bf16_pairs_f32(p)
    return lo.astype(BF16), hi.astype(BF16)


def _memkv_kernel(mem_ref, g_ref, wk_ref, wv_ref, k_ref, v_ref, kbf_ref, vbf_ref):
    m = _rms(mem_ref[...], g_ref[...]).astype(BF16)
    k = _dot(m, wk_ref[...].astype(BF16))
    v = _dot(m, wv_ref[...].astype(BF16))
    k_ref[...] = k
    v_ref[...] = v
    kbf_ref[...] = k.astype(BF16)
    vbf_ref[...] = v.astype(BF16)


def _memkv(mem, g_mem, w_mk, w_mv):
    return pl.pallas_call(
        _memkv_kernel,
        out_shape=(jax.ShapeDtypeStruct((N_MEM, D_MODEL), F32), jax.ShapeDtypeStruct((N_MEM, D_MODEL), F32),
                   jax.ShapeDtypeStruct((N_MEM, D_MODEL), BF16), jax.ShapeDtypeStruct((N_MEM, D_MODEL), BF16)),
        compiler_params=pltpu.CompilerParams(vmem_limit_bytes=VMEM_LIMIT),
        name="memkv",
    )(mem, g_mem, w_mk, w_mv)


def _attn_heads(q, k, v, k_transposed):
    outs = []
    for h in range(X_HEADS):
        sl = slice(h * X_HEAD_DIM, (h + 1) * X_HEAD_DIM)
        if k_transposed:
            s = _dot(q[:, sl], k[sl, :])
        else:
            s = lax.dot_general(q[:, sl], k[:, sl], (((1,), (1,)), ((), ())), preferred_element_type=F32)
        s = s * (X_HEAD_DIM ** -0.5)
        s = s - jnp.max(s, axis=-1, keepdims=True)
        p = jnp.exp(s)
        p = p / jnp.sum(p, axis=-1, keepdims=True)
        outs.append(_dot(p.astype(BF16), v[:, sl]).astype(BF16))
    return jnp.concatenate(outs, axis=1)


def _route(logits, run, strict_lower):
    m = logits.shape[0]
    r = strict_lower.shape[0]
    lane = lax.broadcasted_iota(jnp.int32, (m, LOGIT_LANES), 1).astype(F32)
    neg = jnp.float32(-jnp.inf)
    big = jnp.float32(LOGIT_LANES)

    def first_argmax(vals):
        mx = jnp.max(vals, axis=-1, keepdims=True)
        idx = jnp.min(jnp.where(vals == mx, lane, big), axis=-1, keepdims=True)
        return mx, idx

    lg = jnp.where(lane < N_GROUPS, logits, neg)
    g_max, g_idx = first_argmax(lg)
    g_w = 1.0 / jnp.sum(jnp.exp(lg - g_max), axis=-1, keepdims=True)

    lo = N_GROUPS + g_idx * EXPERTS_PER_GROUP
    le = jnp.where((lane >= lo) & (lane < lo + EXPERTS_PER_GROUP), logits, neg)
    v1, i1 = first_argmax(le)
    v2, i2 = first_argmax(jnp.where(lane == i1, neg, le))
    t = jnp.exp(v2 - v1)
    gate1 = g_w / (1.0 + t)
    gate2 = g_w * t / (1.0 + t)
    e1 = i1 - N_GROUPS
    e2 = i2 - N_GROUPS

    oh1 = (lane == e1).astype(F32)
    oh2 = (lane == e2).astype(F32)
    oh = oh1 + oh2
    befores = []
    for r0 in range(0, m, r):
        oh_r = oh[r0:r0 + r, :]
        befores.append(_dot(strict_lower, oh_r.astype(BF16)) + run)
        run = run + jnp.sum(oh_r, axis=0, keepdims=True)
    before = befores[0] if len(befores) == 1 else jnp.concatenate(befores, axis=0)
    rank1 = jnp.sum(before * oh1, axis=-1, keepdims=True)
    rank2 = jnp.sum(before * oh2, axis=-1, keepdims=True)
    new_run = run

    rinfo = jnp.where(lane == 0, e1,
            jnp.where(lane == 1, e2,
            jnp.where(lane == 2, gate1,
            jnp.where(lane == 3, gate2,
            jnp.where(lane == 4, rank1,
            jnp.where(lane == 5, rank2, 0.0))))))
    return jnp.transpose(rinfo)[0:SUBLANES, :], new_run


def _conv_segments(a, w_ref, seg_ref, tail_ref, yseg_ref, conv_ref):
    sub = lax.broadcasted_iota(jnp.int32, (SUBLANES, LANES), 0)
    for lt in range(D_CONV // LANES):
        ls = slice(lt * LANES, (lt + 1) * LANES)
        for t0 in range(0, TM, SUBLANES):
            s, m = divmod(t0, SEG)
            seg_ref[lt, pl.ds((SEG_HALO + m) * SUBLANES + s, SUBLANES, stride=SUBLANES), :] = a[t0:t0 + SUBLANES, ls]
        for j in range(SEG_HALO):
            cur = seg_ref[lt, (SEG + j) * SUBLANES:(SEG + j + 1) * SUBLANES, :]
            prev = tail_ref[lt, j * SUBLANES:(j + 1) * SUBLANES, :]
            seg_ref[lt, j * SUBLANES:(j + 1) * SUBLANES, :] = jnp.where(
                sub == 0, pltpu.roll(prev, 1, axis=0), pltpu.roll(cur, 1, axis=0))
            tail_ref[lt, j * SUBLANES:(j + 1) * SUBLANES, :] = cur
        for m0 in range(0, SEG, CONV_BLOCK):
            acc = [jnp.zeros((SUBLANES, LANES), F32) for _ in range(CONV_BLOCK)]
            for idx in range(m0 - HIST, m0 + CONV_BLOCK):
                b = seg_ref[lt, (SEG_HALO + idx) * SUBLANES:(SEG_HALO + idx + 1) * SUBLANES, :]
                for m in range(max(m0, idx), min(m0 + CONV_BLOCK, idx + CONV_WIDTH)):
                    k = idx - m + HIST
                    acc[m - m0] = acc[m - m0] + b * w_ref[k:k + 1, ls]
            for m in range(m0, m0 + CONV_BLOCK):
                yseg_ref[lt, m * SUBLANES:(m + 1) * SUBLANES, :] = acc[m - m0]
        for t0 in range(0, TM, SUBLANES):
            s, m = divmod(t0, SEG)
            conv_ref[t0:t0 + SUBLANES, ls] = yseg_ref[lt, pl.ds(m * SUBLANES + s, SUBLANES, stride=SUBLANES), :]


def _stage_cast(pairs, stage_ref, sem):
    work = [(src, dst, r0) for src, dst in pairs for r0 in range(0, src.shape[0], STAGE_ROWS)]

    def copy(k):
        src, _, r0 = work[k]
        return pltpu.make_async_copy(src.at[pl.ds(r0, STAGE_ROWS)],
                                     stage_ref.at[k % 2, :, pl.ds(0, src.shape[1])], sem.at[k % 2])

    copy(0).start()
    for k, (src, dst, r0) in enumerate(work):
        if k + 1 < len(work):
            copy(k + 1).start()
        copy(k).wait()

        def body(c, carry, k=k, src=src, dst=dst, r0=r0):
            c0 = pl.multiple_of(c * CAST_ROWS, CAST_ROWS)
            dst[pl.ds(r0 + c0, CAST_ROWS), :] = stage_ref[k % 2, pl.ds(c0, CAST_ROWS), 0:src.shape[1]].astype(BF16)
            return carry

        lax.fori_loop(0, STAGE_ROWS // CAST_ROWS, body, 0)


def _trunk_prompt_kernel(x_ref, gmix_ref, win32_ref, convw_ref, convb_ref, lncg_ref, lncb_ref, lnvg_ref, lnvb_ref,
                         wsg_ref, bsg_ref, wout32_ref, gx_ref, wxq32_ref, kmem_ref, v_ref, wxo32_ref, gffn_ref, wr_ref,
                         br_ref,
                         x2_ref, h3_ref, logit_ref, hist_ref,
                         seg_ref, tail_ref, yseg_ref, conv_ref, win_ref, wout_ref, wxq_ref, wxo_ref, stage_ref, stage_sem):
    i = pl.program_id(0)

    @pl.when(i == 0)
    def _():
        tail_ref[...] = jnp.zeros(tail_ref.shape, F32)
        _stage_cast([(win32_ref, win_ref), (wout32_ref, wout_ref), (wxq32_ref, wxq_ref), (wxo32_ref, wxo_ref)],
                    stage_ref, stage_sem)

    x = x_ref[...]
    h = _rms(x, gmix_ref[...]).astype(BF16)

    a_in = _dot(h, win_ref[:, 0:D_CONV])
    a_gate = _dot(h, win_ref[:, D_CONV:2 * D_CONV])
    a = a_in * _sigmoid(a_gate)
    hist_ref[...] = a[TM - HALO:, :]
    _conv_segments(a, convw_ref, seg_ref, tail_ref, yseg_ref, conv_ref)

    y = _ln(conv_ref[...] + convb_ref[...], lncg_ref[...], lncb_ref[...])
    a_out = (y * _sigmoid(y)).astype(BF16)

    u = _dot(h, win_ref[:, 2 * D_CONV:2 * D_CONV + D_SG])
    v = _ln(_dot(h, win_ref[:, 2 * D_CONV + D_SG:]), lnvg_ref[...], lnvb_ref[...]).astype(BF16)
    ri = lax.broadcasted_iota(jnp.int32, (SG_CHUNK, SG_CHUNK), 0)
    ci = lax.broadcasted_iota(jnp.int32, (SG_CHUNK, SG_CHUNK), 1)
    w_tril = [jnp.where(ci <= ri, wsg_ref[hh], 0.0).astype(BF16) for hh in range(SG_HEADS)]
    gate_rows = []
    for c in range(TM // SG_CHUNK):
        rs = slice(c * SG_CHUNK, (c + 1) * SG_CHUNK)
        heads = [_dot(w_tril[hh], v[rs, hh * SG_HEAD_DIM:(hh + 1) * SG_HEAD_DIM]) for hh in range(SG_HEADS)]
        gate_rows.append(jnp.concatenate(heads, axis=1) + bsg_ref[...])
    b_out = (u * jnp.concatenate(gate_rows, axis=0)).astype(BF16)

    x1 = x + _dot(a_out, wout_ref[0:D_CONV, :]) + _dot(b_out, wout_ref[D_CONV:, :])

    hx = _rms(x1, gx_ref[...]).astype(BF16)
    q = _dot(hx, wxq_ref[...]).astype(BF16)
    x2 = x1 + _dot(_attn_heads(q, kmem_ref[...], v_ref[...], False), wxo_ref[...])
    x2_ref[...] = x2

    h3 = _rms(x2, gffn_ref[...]).astype(BF16)
    h3_ref[...] = _pack_bf16_pairs(h3.astype(F32))
    logit_ref[...] = _dot(h3, wr_ref[...]) + br_ref[...]


def _router_kernel(logit_ref, upper_ref, ones_ref, rt_ref, cnt_ref, run_ref):
    @pl.when(pl.program_id(0) == 0)
    def _():
        run_ref[...] = jnp.zeros(run_ref.shape, F32)

    n = ROUTE_ROWS
    lt = jnp.transpose(logit_ref[...])
    neg = jnp.float32(-jnp.inf)
    big = jnp.float32(LOGIT_LANES)

    def first_argmax(vals, rows):
        mx = jnp.max(vals, axis=0, keepdims=True)
        idx = jnp.min(jnp.where(vals == mx, rows, big), axis=0, keepdims=True)
        return mx, idx

    row8 = lax.broadcasted_iota(jnp.int32, (SUBLANES, n), 0).astype(F32)
    lg = jnp.where(row8 < N_GROUPS, lt[0:SUBLANES, :], neg)
    g_max, g_idx = first_argmax(lg, row8)
    g_w = 1.0 / jnp.sum(jnp.exp(lg - g_max), axis=0, keepdims=True)

    n_rows = N_GROUPS + N_EXPERTS + (-(N_GROUPS + N_EXPERTS)) % SUBLANES
    rows = lax.broadcasted_iota(jnp.int32, (n_rows, n), 0).astype(F32)
    lo = N_GROUPS + g_idx * EXPERTS_PER_GROUP
    le = jnp.where((rows >= lo) & (rows < lo + EXPERTS_PER_GROUP), lt[0:n_rows, :], neg)
    v1, i1 = first_argmax(le, rows)
    v2, i2 = first_argmax(jnp.where(rows == i1, neg, le), rows)
    t = jnp.exp(v2 - v1)
    gate1 = g_w / (1.0 + t)
    gate2 = g_w * t / (1.0 + t)
    e1 = i1 - N_GROUPS
    e2 = i2 - N_GROUPS

    erow = lax.broadcasted_iota(jnp.int32, (LOGIT_LANES, n), 0).astype(F32)
    oh1 = (erow == e1).astype(F32)
    oh2 = (erow == e2).astype(F32)
    oh = (oh1 + oh2).astype(BF16)
    r = upper_ref.shape[0]
    run = run_ref[...]
    rank1, rank2 = [], []
    for c0 in range(0, n, r):
        cs = slice(c0, c0 + r)
        before = _dot(oh[:, cs], upper_ref[...]) + run
        rank1.append(jnp.sum(before * oh1[:, cs], axis=0, keepdims=True))
        rank2.append(jnp.sum(before * oh2[:, cs], axis=0, keepdims=True))
        run = run + _dot(oh[:, cs], ones_ref[...])
    run_ref[...] = run
    cnt_ref[...] = run[:, 0:LANES]

    sub = lax.broadcasted_iota(jnp.int32, (SUBLANES, n), 0)
    vals = (e1, e2, gate1, gate2, jnp.concatenate(rank1, axis=1), jnp.concatenate(rank2, axis=1))
    rt = jnp.zeros((SUBLANES, n), F32)
    for k, v in enumerate(vals):
        rt = jnp.where(sub == k, v, rt)
    rt_ref[...] = rt


def _router(logits, rank_block):
    n = logits.shape[0]
    assert n % ROUTE_ROWS == 0 and ROUTE_ROWS % rank_block == 0
    upper = jnp.triu(jnp.ones((rank_block, rank_block), BF16), 1)
    ones = jnp.ones((rank_block, rank_block), BF16)
    return pl.pallas_call(
        _router_kernel,
        grid=(n // ROUTE_ROWS,),
        in_specs=[pl.BlockSpec((ROUTE_ROWS, LOGIT_LANES), lambda i: (i, 0)),
                  pl.BlockSpec(upper.shape, lambda i: (0, 0)),
                  pl.BlockSpec(ones.shape, lambda i: (0, 0))],
        out_specs=(pl.BlockSpec((SUBLANES, ROUTE_ROWS), lambda i: (0, i)),
                   pl.BlockSpec((LOGIT_LANES, LANES), lambda i: (0, 0))),
        out_shape=(jax.ShapeDtypeStruct((SUBLANES, n), F32), jax.ShapeDtypeStruct((LOGIT_LANES, LANES), F32)),
        scratch_shapes=[pltpu.VMEM((LOGIT_LANES, rank_block), F32)],
        compiler_params=pltpu.CompilerParams(dimension_semantics=("arbitrary",), vmem_limit_bytes=VMEM_LIMIT),
        name="router",
    )(logits, upper, ones)


def _const_spec(shape):
    nd = len(shape)
    return pl.BlockSpec(shape, lambda i: (0,) * nd, pipeline_mode=pl.Buffered(1))


def _trunk_prompt(x, p):
    n = x.shape[0]
    assert n % TM == 0
    row = lambda w: pl.BlockSpec((TM, w), lambda i: (i, 0))
    consts = [p["g_mix"], p["w_in"], p["conv_w"], p["conv_b"], p["ln_conv_g"], p["ln_conv_b"], p["ln_v_g"],
              p["ln_v_b"], p["w_sg"], p["b_sg_rows"], p["w_out"], p["g_xattn"], p["w_xq"], p["k"], p["v"],
              p["w_xo"], p["g_ffn"], p["w_router"], p["b_router"]]
    staged = (p["w_in"], p["w_out"], p["w_xq"], p["w_xo"])
    spec = lambda c: pl.BlockSpec(memory_space=pl.ANY) if any(c is s for s in staged) else _const_spec(c.shape)
    return pl.pallas_call(
        _trunk_prompt_kernel,
        grid=(n // TM,),
        in_specs=[row(D_MODEL)] + [spec(c) for c in consts],
        out_specs=(row(D_MODEL), row(D_MODEL // 2), row(LOGIT_LANES),
                   pl.BlockSpec((HALO, D_CONV), lambda i: (0, 0))),
        out_shape=(jax.ShapeDtypeStruct((n, D_MODEL), F32),
                   jax.ShapeDtypeStruct((n, D_MODEL // 2), jnp.uint32),
                   jax.ShapeDtypeStruct((n, LOGIT_LANES), F32),
                   jax.ShapeDtypeStruct((HALO, D_CONV), F32)),
        scratch_shapes=[pltpu.VMEM((D_CONV // LANES, (SEG_HALO + SEG) * SUBLANES, LANES), F32),
                        pltpu.VMEM((D_CONV // LANES, SEG_HALO * SUBLANES, LANES), F32),
                        pltpu.VMEM((D_CONV // LANES, TM, LANES), F32),
                        pltpu.VMEM((TM, D_CONV), F32),
                        pltpu.VMEM(p["w_in"].shape, BF16), pltpu.VMEM(p["w_out"].shape, BF16),
                        pltpu.VMEM(p["w_xq"].shape, BF16), pltpu.VMEM(p["w_xo"].shape, BF16),
                        pltpu.VMEM((2, STAGE_ROWS, max(s.shape[1] for s in staged)), F32),
                        pltpu.SemaphoreType.DMA((2,))],
        compiler_params=pltpu.CompilerParams(dimension_semantics=("arbitrary",), vmem_limit_bytes=VMEM_LIMIT),
        name="trunk_prompt",
    )(x, *consts)


def _trunk_sample_kernel(n_batch, t_len,
                         x_ref, hist_in_ref, run_in_ref, gmix_ref, win_ref, convw_ref, convb_ref, lncg_ref, lncb_ref,
                         lnvg_ref, lnvb_ref, wsgbd_ref, bsg_ref, wout_ref, gx_ref, wxq_ref, kmem_ref, v_ref, wxo_ref,
                         gffn_ref, wr_ref, br_ref, lower_ref,
                         x2_ref, h3_ref, rt_ref, hist_ref, sgv_ref, cnt_ref,
                         ext_ref, conv_ref, att_ref):
    x = x_ref[...]
    h = _rms(x, gmix_ref[...]).astype(BF16)
    z = _dot(h, win_ref[...].astype(BF16))
    a = z[:, 0:D_CONV] * _sigmoid(z[:, D_CONV:2 * D_CONV])
    ext_len = HIST + t_len
    for b in range(n_batch):
        ext_ref[b, 0:HIST, :] = hist_in_ref[b]
        ext_ref[b, HIST:ext_len, :] = a[b * t_len:(b + 1) * t_len, :]
    for b in range(n_batch):
        acc = jnp.zeros((t_len, D_CONV), F32)
        for k in range(CONV_WIDTH):
            acc = acc + ext_ref[b, k:k + t_len, :] * convw_ref[k:k + 1, :]
        conv_ref[b * t_len:(b + 1) * t_len, :] = acc
        hist_ref[b] = ext_ref[b, ext_len - HIST:ext_len, :]

    y = _ln(conv_ref[...] + convb_ref[...], lncg_ref[...], lncb_ref[...])
    a_out = (y * _sigmoid(y)).astype(BF16)

    u = z[:, 2 * D_CONV:2 * D_CONV + D_SG]
    v = _ln(z[:, 2 * D_CONV + D_SG:], lnvg_ref[...], lnvb_ref[...])
    sgv_ref[...] = v
    vb = v.astype(BF16)
    heads = [_dot(wsgbd_ref[hh], vb[:, hh * SG_HEAD_DIM:(hh + 1) * SG_HEAD_DIM]) for hh in range(SG_HEADS)]
    b_out = (u * (jnp.concatenate(heads, axis=1) + bsg_ref[...])).astype(BF16)

    x1 = (x + _dot(a_out, wout_ref[0:D_CONV, :].astype(BF16))
          + _dot(b_out, wout_ref[D_CONV:, :].astype(BF16)))

    hx = _rms(x1, gx_ref[...]).astype(BF16)
    q = _dot(hx, wxq_ref[...].astype(BF16)).astype(BF16)
    for b in range(n_batch):
        rs = slice(b * t_len, (b + 1) * t_len)
        att_ref[rs, :] = _attn_heads(q[rs, :], kmem_ref[b], v_ref[b], True)
    x2 = x1 + _dot(att_ref[...], wxo_ref[...].astype(BF16))
    x2_ref[...] = x2

    h3 = _rms(x2, gffn_ref[...]).astype(BF16)
    m = n_batch * t_len
    h3_ref[0:m, :] = _pack_bf16_pairs(h3.astype(F32))
    if h3_ref.shape[0] > m:
        h3_ref[m:, :] = jnp.zeros((h3_ref.shape[0] - m, D_MODEL // 2), jnp.uint32)
    rt, new_run = _route(_dot(h3, wr_ref[...]) + br_ref[...], run_in_ref[...], lower_ref[...])
    rt_ref[...] = rt
    cnt_ref[...] = new_run


def _trunk_sample(x, hist, run, p, n_batch, t_len):
    m = n_batch * t_len
    args = [x, hist, run, p["g_mix"], p["w_in"], p["conv_w"], p["conv_b"], p["ln_conv_g"], p["ln_conv_b"],
            p["ln_v_g"], p["ln_v_b"], p["w_sg_bd"], p["b_sg_rows_s"], p["w_out"], p["g_xattn"], p["w_xq"],
            p["k_s"], p["v_s"], p["w_xo"], p["g_ffn"], p["w_router"], p["b_router"], p["lower"]]
    return pl.pallas_call(
        functools.partial(_trunk_sample_kernel, n_batch, t_len),
        out_shape=(jax.ShapeDtypeStruct((m, D_MODEL), F32),
                   jax.ShapeDtypeStruct((-(-m // (SC_WORKERS * SUBLANES)) * SC_WORKERS * SUBLANES, D_MODEL // 2),
                                        jnp.uint32),
                   jax.ShapeDtypeStruct((SUBLANES, m), F32),
                   jax.ShapeDtypeStruct((n_batch, HIST, D_CONV), F32),
                   jax.ShapeDtypeStruct((m, D_SG), F32),
                   jax.ShapeDtypeStruct((1, LOGIT_LANES), F32)),
        scratch_shapes=[pltpu.VMEM((n_batch, HIST + t_len, D_CONV), F32),
                        pltpu.VMEM((m, D_CONV), F32),
                        pltpu.VMEM((m, D_MODEL), BF16)],
        compiler_params=pltpu.CompilerParams(vmem_limit_bytes=VMEM_LIMIT),
        name="trunk_sample",
    )(*args)


def _sc_worker_id():
    return lax.axis_index("s") * SC_CORES + lax.axis_index("c")


def _sc_chunk(per_w, max_chunk):
    assert per_w % SUBLANES == 0 and max_chunk <= LANES
    return max(c for c in range(SUBLANES, max_chunk + 1, SUBLANES) if per_w % c == 0)


def _sc_scatter_rows2(tables, slots_a, slots_b, tag_bases, n_rows_out, max_chunk):
    d, dtype = tables[0].shape[1], tables[0].dtype
    plans = []
    for t in tables:
        per_w = t.shape[0] // SC_WORKERS
        assert per_w * SC_WORKERS == t.shape[0]
        chunk = _sc_chunk(per_w, max_chunk)
        plans.append((per_w, chunk, per_w // chunk))
    cmax = max(c for _, c, _ in plans)
    n_t = len(tables)
    mesh = plsc.VectorSubcoreMesh(core_axis_name="c", subcore_axis_name="s")

    nb = SCATTER_BUFS
    lag = 2
    scratch = []
    for _, chunk, _ in plans:
        for _ in range(nb):
            scratch += [pltpu.VMEM((chunk,), jnp.int32), pltpu.VMEM((chunk,), jnp.int32)]
    scratch += [pltpu.VMEM((cmax, d), dtype)] * nb
    scratch += [pltpu.VMEM((cmax, TAG_WORDS), jnp.int32)] * (2 * nb)
    scratch += [pltpu.SemaphoreType.DMA] * (2 * nb)

    @functools.partial(pl.kernel, mesh=mesh,
                       out_type=(jax.ShapeDtypeStruct((n_rows_out, d), dtype),
                                 jax.ShapeDtypeStruct((n_rows_out, TAG_WORDS), jnp.int32)),
                       scratch_types=scratch)
    def scatter(*refs):
        tab_hbm = refs[0:n_t]
        sa_hbm = refs[n_t:2 * n_t]
        sb_hbm = refs[2 * n_t:3 * n_t]
        out_hbm, tag_hbm = refs[3 * n_t], refs[3 * n_t + 1]
        sc = refs[3 * n_t + 2:]
        idx_refs = sc[:2 * nb * n_t]
        rows = sc[2 * nb * n_t:2 * nb * n_t + nb]
        tagbufs = sc[2 * nb * n_t + nb:2 * nb * n_t + 3 * nb]
        lsem = sc[2 * nb * n_t + 3 * nb:2 * nb * n_t + 4 * nb]
        ssem = sc[2 * nb * n_t + 4 * nb:]
        wid = _sc_worker_id()

        work = []
        for t, (per_w, chunk, n_chunks) in enumerate(plans):
            for j in range(n_chunks):
                work.append((t, wid * per_w + j * chunk, chunk))

        def parts(k):
            t, off, chunk = work[k]
            b = k % nb
            ia, ib = idx_refs[2 * nb * t + 2 * b], idx_refs[2 * nb * t + 2 * b + 1]
            full = chunk == cmax
            rv = rows[b] if full else rows[b].at[pl.ds(0, chunk)]
            ta = tagbufs[2 * b] if full else tagbufs[2 * b].at[pl.ds(0, chunk)]
            tb = tagbufs[2 * b + 1] if full else tagbufs[2 * b + 1].at[pl.ds(0, chunk)]
            return t, off, chunk, b, ia, ib, rv, ta, tb

        def start_load(k):
            t, off, chunk, b, ia, ib, rv, ta, tb = parts(k)
            return (pltpu.async_copy(tab_hbm[t].at[pl.ds(off, chunk)], rv, lsem[b]),
                    pltpu.async_copy(sa_hbm[t].at[pl.ds(off, chunk)], ia, lsem[b]),
                    pltpu.async_copy(sb_hbm[t].at[pl.ds(off, chunk)], ib, lsem[b]))

        def start_scatter(k):
            t, off, chunk, b, ia, ib, rv, ta, tb = parts(k)
            base_a, base_b = tag_bases[t]
            for r in range(chunk):
                row_id = (off + r).astype(jnp.int32)
                tagbufs[2 * b][r, pl.ds(0, SC_LANES)] = jnp.zeros((SC_LANES,), jnp.int32) + (base_a + row_id)
                tagbufs[2 * b + 1][r, pl.ds(0, SC_LANES)] = jnp.zeros((SC_LANES,), jnp.int32) + (base_b + row_id)
            return (pltpu.async_copy(rv, out_hbm.at[ia], ssem[b]), pltpu.async_copy(rv, out_hbm.at[ib], ssem[b]),
                    pltpu.async_copy(ta, tag_hbm.at[ia], ssem[b]), pltpu.async_copy(tb, tag_hbm.at[ib], ssem[b]))

        loads, scatters = {}, {}
        for k in range(len(work) + lag):
            if k < len(work):
                if k >= nb:
                    for c in scatters.pop(k - nb):
                        c.wait()
                loads[k] = start_load(k)
            w = k - lag
            if w >= 0:
                for c in loads.pop(w):
                    c.wait()
                scatters[w] = start_scatter(w)
        for w in sorted(scatters):
            for c in scatters[w]:
                c.wait()

    return scatter(*tables, *slots_a, *slots_b)


def _sc_scatter_back(ys, dest, n_rows_out):
    n_rows, d = ys.shape
    per_w = n_rows // SC_WORKERS
    assert per_w * SC_WORKERS == n_rows
    chunk = _sc_chunk(per_w, BACK_CHUNK)
    n_chunks = per_w // chunk
    nb = BACK_BUFS
    lag = 1
    mesh = plsc.VectorSubcoreMesh(core_axis_name="c", subcore_axis_name="s")

    @functools.partial(
        pl.kernel, mesh=mesh,
        out_type=jax.ShapeDtypeStruct((n_rows_out, d), ys.dtype),
        scratch_types=([pltpu.VMEM((chunk,), jnp.int32)] * nb + [pltpu.VMEM((chunk, d), ys.dtype)] * nb
                       + [pltpu.SemaphoreType.DMA] * (2 * nb)),
    )
    def scatter_back(ys_hbm, dest_hbm, out_hbm, *rest):
        idx = rest[:nb]
        rows = rest[nb:2 * nb]
        lsem = rest[2 * nb:3 * nb]
        ssem = rest[3 * nb:]
        base = _sc_worker_id() * per_w

        loads, scatters = {}, {}
        for k in range(n_chunks + lag):
            if k < n_chunks:
                b = k % nb
                if k >= nb:
                    scatters.pop(k - nb).wait()
                off = base + k * chunk
                loads[k] = (pltpu.async_copy(ys_hbm.at[pl.ds(off, chunk)], rows[b], lsem[b]),
                            pltpu.async_copy(dest_hbm.at[pl.ds(off, chunk)], idx[b], lsem[b]))
            w = k - lag
            if w >= 0:
                b = w % nb
                for c in loads.pop(w):
                    c.wait()
                scatters[w] = pltpu.async_copy(rows[b], out_hbm.at[idx[b]], ssem[b])
        for w in sorted(scatters):
            scatters[w].wait()

    return scatter_back(ys, dest)


def _experts_kernel(dump_base, first_ref, nblk_ref, cnt_ref, tot_ref, xs_hbm, tag_hbm, wg_ref, wu_ref, wd_ref,
                    ys_hbm, dest_hbm, xbuf, tbuf, ybuf, dbuf, wg_bf, wu_bf, wd_bf, in_sem, tin_sem, out_sem, dout_sem):
    e = pl.program_id(0)
    nb = nblk_ref[e]
    first = first_ref[e]
    cnt = cnt_ref[e]
    total = tot_ref[0]
    half = D_MODEL // 2

    def in_copies(gb):
        slot = lax.rem(gb, X_BUFS)
        return (pltpu.make_async_copy(xs_hbm.at[pl.ds(gb * BM, BM)], xbuf.at[slot], in_sem.at[slot]),
                pltpu.make_async_copy(tag_hbm.at[pl.ds(gb * BM, BM)], tbuf.at[slot], tin_sem.at[slot]))

    def out_copies(gb):
        slot = lax.rem(gb, Y_BUFS)
        return (pltpu.make_async_copy(ybuf.at[slot], ys_hbm.at[pl.ds(gb * BM, BM)], out_sem.at[slot]),
                pltpu.make_async_copy(dbuf.at[slot], dest_hbm.at[pl.ds(gb * SUBLANES, SUBLANES)], dout_sem.at[slot]))

    def start_in(gb):
        for c in in_copies(gb):
            c.start(priority=ROW_DMA_PRIORITY)

    def start_out(gb):
        for c in out_copies(gb):
            c.start(priority=ROW_DMA_PRIORITY)

    def wait_out(gb):
        for c in out_copies(gb):
            c.wait()

    @pl.when(nb > 0)
    def _():
        @pl.when(first == 0)
        def _():
            for k in range(X_LOOKAHEAD):
                @pl.when(k < total)
                def _():
                    start_in(k)

        wg_bf[...] = wg_ref[0].astype(BF16)
        wu_bf[...] = wu_ref[0].astype(BF16)
        wd_bf[...] = wd_ref[0].astype(BF16)

        def acquire(gb):
            @pl.when(gb + X_LOOKAHEAD < total)
            def _():
                start_in(gb + X_LOOKAHEAD)

            for c in in_copies(gb):
                c.wait()

            @pl.when(gb >= Y_BUFS)
            def _():
                wait_out(gb - Y_BUFS)

        def ffn(gb, j):
            n_live = cnt - j * BM
            tags_t = jnp.transpose(tbuf[lax.rem(gb, X_BUFS)].astype(F32))
            lane = lax.broadcasted_iota(jnp.int32, (SUBLANES, BM), 1)
            own = (dump_base + gb * BM + lane).astype(F32)
            dest = jnp.where(lane < n_live, jnp.broadcast_to(tags_t[0:1, :], (SUBLANES, BM)), own)
            dbuf[lax.rem(gb, Y_BUFS)] = dest.astype(jnp.int32)
            live = lax.broadcasted_iota(jnp.int32, (BM, half), 0) < n_live
            lo, hi = _unpack_bf16_pairs(jnp.where(live, xbuf[lax.rem(gb, X_BUFS)], jnp.uint32(0)))
            y = None
            for c0 in range(0, D_EXPERT, D_EXPERT // 2):
                cs = slice(c0, c0 + D_EXPERT // 2)
                g = _dot(lo, wg_bf[0:half, cs]) + _dot(hi, wg_bf[half:, cs])
                u = _dot(lo, wu_bf[0:half, cs]) + _dot(hi, wu_bf[half:, cs])
                hm = (g * _sigmoid(g) * u).astype(BF16)
                part = _dot(hm, wd_bf[cs, :])
                y = part if y is None else y + part
            ybuf[lax.rem(gb, Y_BUFS)] = _pack_bf16_pairs(y.astype(BF16).astype(F32))

        def block_pair(jp, carry):
            j0 = 2 * jp
            g0 = first + j0
            acquire(g0)
            acquire(g0 + 1)
            ffn(g0, j0)
            ffn(g0 + 1, j0 + 1)
            start_out(g0)
            start_out(g0 + 1)
            return carry

        lax.fori_loop(0, nb // 2, block_pair, 0)

        @pl.when(lax.rem(nb, 2) == 1)
        def _():
            gl = first + nb - 1
            acquire(gl)
            ffn(gl, nb - 1)
            start_out(gl)

        @pl.when(first + nb == total)
        def _():
            for k in range(Y_BUFS):
                @pl.when(total - 1 - k >= 0)
                def _():
                    wait_out(total - 1 - k)


def _experts(xs, tags, n_rows_out, dump_base, first_block, n_blocks_e, counts, w_eg, w_eu, w_ed):
    w_map = lambda e, fb, nb, ct, tot: (e, 0, 0)
    half = D_MODEL // 2
    total = jnp.sum(n_blocks_e).astype(jnp.int32).reshape(1)
    n_blocks = n_rows_out // BM
    return pl.pallas_call(
        functools.partial(_experts_kernel, dump_base),
        grid_spec=pltpu.PrefetchScalarGridSpec(
            num_scalar_prefetch=4,
            grid=(N_EXPERTS,),
            in_specs=[pl.BlockSpec(memory_space=pl.ANY),
                      pl.BlockSpec(memory_space=pl.ANY),
                      pl.BlockSpec((1, D_MODEL, D_EXPERT), w_map),
                      pl.BlockSpec((1, D_MODEL, D_EXPERT), w_map),
                      pl.BlockSpec((1, D_EXPERT, D_MODEL), w_map)],
            out_specs=(pl.BlockSpec(memory_space=pl.ANY), pl.BlockSpec(memory_space=pl.ANY)),
            scratch_shapes=[pltpu.VMEM((X_BUFS, BM, half), jnp.uint32), pltpu.VMEM((X_BUFS, BM, TAG_WORDS), jnp.int32),
                            pltpu.VMEM((Y_BUFS, BM, half), jnp.uint32), pltpu.VMEM((Y_BUFS, SUBLANES, BM), jnp.int32),
                            pltpu.VMEM((D_MODEL, D_EXPERT), BF16), pltpu.VMEM((D_MODEL, D_EXPERT), BF16),
                            pltpu.VMEM((D_EXPERT, D_MODEL), BF16),
                            pltpu.SemaphoreType.DMA((X_BUFS,)), pltpu.SemaphoreType.DMA((X_BUFS,)),
                            pltpu.SemaphoreType.DMA((Y_BUFS,)), pltpu.SemaphoreType.DMA((Y_BUFS,))]),
        out_shape=(jax.ShapeDtypeStruct((n_rows_out, half), jnp.uint32),
                   jax.ShapeDtypeStruct((n_blocks * SUBLANES, BM), jnp.int32)),
        compiler_params=pltpu.CompilerParams(dimension_semantics=("arbitrary",), vmem_limit_bytes=VMEM_LIMIT),
        name="experts",
    )(first_block, n_blocks_e, counts, total, xs, tags, w_eg, w_eu, w_ed)


def _combine_kernel(x2_ref, y1_ref, y2_ref, rt_ref, g_ref, o_ref):
    rt = rt_ref[...]
    r = jnp.transpose(jnp.concatenate([rt, jnp.zeros((LANES - rt.shape[0], rt.shape[1]), F32)], axis=0))
    g1, g2 = r[:, 2:3], r[:, 3:4]
    half = D_MODEL // 2
    y1_lo, y1_hi = _unpack_bf16_pairs_f32(y1_ref[...])
    y2_lo, y2_hi = _unpack_bf16_pairs_f32(y2_ref[...])
    x_lo = x2_ref[:, 0:half] + g1 * y1_lo + g2 * y2_lo
    x_hi = x2_ref[:, half:] + g1 * y1_hi + g2 * y2_hi
    ms = (jnp.sum(x_lo * x_lo, axis=-1, keepdims=True) + jnp.sum(x_hi * x_hi, axis=-1, keepdims=True)) / D_MODEL
    inv = lax.rsqrt(ms + EPS)
    o_ref[:, 0:half] = x_lo * inv * g_ref[:, 0:half]
    o_ref[:, half:] = x_hi * inv * g_ref[:, half:]


def _combine(x2, yg, rt, g_final, tm, blk1, blk2):
    n = x2.shape[0]
    return pl.pallas_call(
        _combine_kernel,
        grid=(n // tm,),
        in_specs=[pl.BlockSpec((tm, D_MODEL), lambda i: (i, 0)),
                  pl.BlockSpec((tm, D_MODEL // 2), lambda i: (blk1 + i, 0)),
                  pl.BlockSpec((tm, D_MODEL // 2), lambda i: (blk2 + i, 0)),
                  pl.BlockSpec((SUBLANES, tm), lambda i: (0, i)),
                  pl.BlockSpec((1, D_MODEL), lambda i: (0, 0))],
        out_specs=pl.BlockSpec((tm, D_MODEL), lambda i: (i, 0)),
        out_shape=jax.ShapeDtypeStruct((n, D_MODEL), F32),
        compiler_params=pltpu.CompilerParams(dimension_semantics=("arbitrary",), vmem_limit_bytes=VMEM_LIMIT),
        name="combine",
    )(x2, yg, yg, rt, g_final)


def _scatter_back(ys, dest, n_rows_out):
    return _sc_scatter_back(ys, dest, n_rows_out)


def _scatter_rows2(tables, slots_a, slots_b, tag_bases, n_rows_out):
    return _sc_scatter_rows2(tables, slots_a, slots_b, tag_bases, n_rows_out, SCATTER_CHUNK)


def kernel(x_prompt, x_sample, mem_prompt, state_conv, cache_mem_k, cache_mem_v, g_mix, w_in, conv_w, conv_b, ln_conv_g, ln_conv_b, ln_v_g, ln_v_b, w_sg, b_sg, w_out, g_mem, w_mk, w_mv, g_xattn, w_xq, w_xo, g_ffn, w_router_group, b_router_group, w_router_expert, b_router_expert, w_expert_gate, w_expert_up, w_expert_down, g_final):
    assert x_prompt.shape[0] == 1 and g_mix.shape[0] == 1
    n_p = x_prompt.shape[1]
    n_batch, t_len = x_sample.shape[0], x_sample.shape[1]
    n_s = n_batch * t_len
    row = lambda a: a.reshape(1, -1)

    w_router = jnp.concatenate(
        [w_router_group[0], jnp.transpose(w_router_expert[0], (1, 0, 2)).reshape(D_MODEL, N_EXPERTS)], axis=1)
    w_router = jnp.pad(w_router, ((0, 0), (0, LOGIT_LANES - w_router.shape[1]))).astype(BF16)
    b_router = jnp.pad(jnp.concatenate([b_router_group[0], b_router_expert[0].reshape(-1)]),
                       (0, LOGIT_LANES - N_GROUPS - N_EXPERTS)).reshape(1, LOGIT_LANES)
    tril_t = jnp.tril(jnp.ones((t_len, t_len), bool))
    w_sg_t = jnp.where(tril_t, w_sg[0][:, :t_len, :t_len], 0.0)
    eye_b = jnp.eye(n_batch, dtype=F32)
    w_sg_bd = jnp.einsum("ab,hij->haibj", eye_b, w_sg_t).reshape(SG_HEADS, n_s, n_s).astype(BF16)
    p = {
        "g_mix": row(g_mix[0]), "w_in": w_in[0],
        "conv_w": jnp.pad(conv_w[0], ((0, 1), (0, 0))), "conv_b": row(conv_b[0]),
        "ln_conv_g": row(ln_conv_g[0]), "ln_conv_b": row(ln_conv_b[0]),
        "ln_v_g": row(ln_v_g[0]), "ln_v_b": row(ln_v_b[0]),
        "w_sg": w_sg[0],
        "b_sg_rows": jnp.repeat(b_sg[0].T, SG_HEAD_DIM, axis=1),
        "w_sg_bd": w_sg_bd,
        "b_sg_rows_s": jnp.tile(jnp.repeat(b_sg[0][:, :t_len].T, SG_HEAD_DIM, axis=1), (n_batch, 1)),
        "w_out": w_out[0], "g_xattn": row(g_xattn[0]),
        "w_xq": w_xq[0], "w_xo": w_xo[0], "g_ffn": row(g_ffn[0]),
        "w_router": w_router, "b_router": b_router,
        "lower": jnp.tril(jnp.ones((n_s, n_s), BF16), -1),
    }

    k_p, v_p, p["k"], p["v"] = _memkv(mem_prompt[0], row(g_mem[0]), w_mk[0], w_mv[0])
    p["k_s"] = jnp.transpose(cache_mem_k[0].astype(BF16), (0, 2, 3, 1)).reshape(n_batch, D_MODEL, N_MEM)
    p["v_s"] = cache_mem_v[0].astype(BF16).reshape(n_batch, N_MEM, D_MODEL)

    assert n_p % n_s == 0
    x2_p, h3_p, logits_p, hist_p = _trunk_prompt(x_prompt[0], p)
    rt_p, cnt_t = _router(logits_p, ROUTE_RANK_ROWS)
    cnt_p = cnt_t[:, 0].reshape(1, LOGIT_LANES)
    x2_s, h3_s, rt_s, hist_s, sgv_s, cnt = _trunk_sample(
        x_sample.reshape(n_s, D_MODEL), state_conv[0], cnt_p, p, n_batch, t_len)

    experts = jnp.arange(N_EXPERTS, dtype=jnp.int32)
    w_e = (w_expert_gate[0], w_expert_up[0], w_expert_down[0])

    def moe_pass(cnt, h3_tables, rts, n_real):
        n_tot = sum(n_real)
        n_slots = -(-(n_tot * 2) // BM) * BM + N_EXPERTS * BM
        counts = cnt[0, :N_EXPERTS].astype(jnp.int32)
        padded = (counts + BM - 1) // BM * BM
        pad_start = jnp.cumsum(padded) - padded

        def one(e_row, rank_row):
            e = e_row.astype(jnp.int32)
            start = jnp.sum(jnp.where(e[None, :] == experts[:, None], pad_start[:, None], 0), axis=0)
            return start + rank_row.astype(jnp.int32)

        slots = [(one(rt[0], rt[4]), one(rt[1], rt[5])) for rt in rts]
        sa, sb, tag_bases, spare0, dest0 = [], [], [], n_slots, 0
        for tab, (a, b), n in zip(h3_tables, slots, n_real):
            n_spare = tab.shape[0] - n
            spare = spare0 + jnp.arange(n_spare, dtype=jnp.int32)
            sa.append(jnp.concatenate([a, spare]))
            sb.append(jnp.concatenate([b, spare + n_spare]))
            spare0 += 2 * n_spare
            tag_bases.append((dest0, dest0 + n))
            dest0 += 2 * n
        xs, tags = _scatter_rows2(tuple(h3_tables), tuple(sa), tuple(sb), tuple(tag_bases), spare0)
        ys, dest_blocks = _experts(xs, tags, n_slots, dest0, pad_start // BM, padded // BM, counts, *w_e)
        slot = jnp.arange(n_slots, dtype=jnp.int32)
        dest = dest_blocks.reshape(n_slots // BM, SUBLANES, BM)[:, 0, :].reshape(-1)
        dest = jnp.where(slot < jnp.sum(padded), dest, dest0 + slot)
        return _scatter_back(ys, dest, dest0 + n_slots)

    yg = moe_pass(cnt, [h3_p, h3_s], [rt_p, rt_s], [n_p, n_s])

    gf = row(g_final)
    y_p = _combine(x2_p, yg, rt_p, gf, TM_COMBINE, 0, n_p // TM_COMBINE)
    y_s = _combine(x2_s, yg, rt_s, gf, n_s, 2 * n_p // n_s, 2 * n_p // n_s + 1)

    return (y_p.reshape(1, n_p, D_MODEL),
            y_s.reshape(n_batch, t_len, D_MODEL),
            hist_p[HALO - HIST:].reshape(1, 1, HIST, D_CONV),
            hist_s.reshape(1, n_batch, HIST, D_CONV),
            k_p.reshape(1, 1, N_MEM, X_HEADS, X_HEAD_DIM),
            v_p.reshape(1, 1, N_MEM, X_HEADS, X_HEAD_DIM),
            sgv_s.reshape(1, n_batch, t_len, D_SG))
```

```python
import functools

import jax
import jax.numpy as jnp
from jax import lax
from jax.experimental import pallas as pl
from jax.experimental.pallas import tpu as pltpu
from jax.experimental.pallas import tpu_sc as plsc

D_MODEL = 1024
D_CONV = 512
D_SG = 512
CONV_WIDTH = 31
HIST = CONV_WIDTH - 1
SG_HEADS = 4
SG_HEAD_DIM = 128
SG_CHUNK = 128
N_MEM = 256
X_HEADS = 4
X_HEAD_DIM = 256
N_GROUPS = 4
EXPERTS_PER_GROUP = 8
N_EXPERTS = 32
D_EXPERT = 512
EPS = 1e-6

LANES = 128
SUBLANES = 8
SC_CORES = 2
SC_SUBCORES = 16
SC_WORKERS = SC_CORES * SC_SUBCORES
SC_LANES = 16
VMEM_LIMIT = 56 * 1024 * 1024

TM = 1024
TM_COMBINE = 1024
ROUTE_ROWS = 4096
ROUTE_RANK_ROWS = 512
HALO = 32
SEG = TM // SUBLANES
SEG_HALO = 32
CONV_BLOCK = 32
CAST_ROWS = 64
STAGE_ROWS = 256
BM = 256
X_LOOKAHEAD = 4
X_BUFS = X_LOOKAHEAD + 2
Y_BUFS = 4
ROW_DMA_PRIORITY = 1
SCATTER_CHUNK = 32
SCATTER_BUFS = 5
TAG_WORDS = 128
BACK_CHUNK = 56
BACK_BUFS = 4
LOGIT_LANES = 128

F32 = jnp.float32
BF16 = jnp.bfloat16


def _dot(a, b):
    return jnp.dot(a, b, preferred_element_type=F32)


def _rms(x, g):
    return x * lax.rsqrt(jnp.mean(x * x, axis=-1, keepdims=True) + EPS) * g


def _ln(x, g, b):
    mu = jnp.mean(x, axis=-1, keepdims=True)
    xc = x - mu
    var = jnp.mean(xc * xc, axis=-1, keepdims=True)
    return xc * lax.rsqrt(var + EPS) * g + b


def _sigmoid(x):
    return 1.0 / (1.0 + jnp.exp(-x))


def _pack_bf16_pairs(h):
    bits = lax.bitcast_convert_type(h, jnp.uint32)
    half = h.shape[1] // 2
    lo = lax.shift_right_logical(bits[:, :half], jnp.uint32(16))
    hi = bits[:, half:] & jnp.uint32(0xFFFF0000)
    return hi | lo


def _unpack_bf16_pairs_f32(p):
    lo = lax.bitcast_convert_type(lax.shift_left(p, jnp.uint32(16)), F32)
    hi = lax.bitcast_convert_type(p & jnp.uint32(0xFFFF0000), F32)
    return lo, hi


def _unpack_bf16_pairs(p):
    lo, hi = _unpack_bf16_pairs_f32(p)
    return lo.astype(BF16), hi.astype(BF16)


def _memkv_kernel(mem_ref, g_ref, wk_ref, wv_ref, k_ref, v_ref, kbf_ref, vbf_ref):
    m = _rms(mem_ref[...], g_ref[...]).astype(BF16)
    k = _dot(m, wk_ref[...].astype(BF16))
    v = _dot(m, wv_ref[...].astype(BF16))
    k_ref[...] = k
    v_ref[...] = v
    kbf_ref[...] = k.astype(BF16)
    vbf_ref[...] = v.astype(BF16)


def _memkv(mem, g_mem, w_mk, w_mv):
    return pl.pallas_call(
        _memkv_kernel,
        out_shape=(jax.ShapeDtypeStruct((N_MEM, D_MODEL), F32), jax.ShapeDtypeStruct((N_MEM, D_MODEL), F32),
                   jax.ShapeDtypeStruct((N_MEM, D_MODEL), BF16), jax.ShapeDtypeStruct((N_MEM, D_MODEL), BF16)),
        compiler_params=pltpu.CompilerParams(vmem_limit_bytes=VMEM_LIMIT),
        name="memkv",
    )(mem, g_mem, w_mk, w_mv)


def _attn_heads(q, k, v, k_transposed):
    outs = []
    for h in range(X_HEADS):
        sl = slice(h * X_HEAD_DIM, (h + 1) * X_HEAD_DIM)
        if k_transposed:
            s = _dot(q[:, sl], k[sl, :])
        else:
            s = lax.dot_general(q[:, sl], k[:, sl], (((1,), (1,)), ((), ())), preferred_element_type=F32)
        s = s * (X_HEAD_DIM ** -0.5)
        s = s - jnp.max(s, axis=-1, keepdims=True)
        p = jnp.exp(s)
        p = p / jnp.sum(p, axis=-1, keepdims=True)
        outs.append(_dot(p.astype(BF16), v[:, sl]).astype(BF16))
    return jnp.concatenate(outs, axis=1)


def _route(logits, run, strict_lower):
    m = logits.shape[0]
    r = strict_lower.shape[0]
    lane = lax.broadcasted_iota(jnp.int32, (m, LOGIT_LANES), 1).astype(F32)
    neg = jnp.float32(-jnp.inf)
    big = jnp.float32(LOGIT_LANES)

    def first_argmax(vals):
        mx = jnp.max(vals, axis=-1, keepdims=True)
        idx = jnp.min(jnp.where(vals == mx, lane, big), axis=-1, keepdims=True)
        return mx, idx

    lg = jnp.where(lane < N_GROUPS, logits, neg)
    g_max, g_idx = first_argmax(lg)
    g_w = 1.0 / jnp.sum(jnp.exp(lg - g_max), axis=-1, keepdims=True)

    lo = N_GROUPS + g_idx * EXPERTS_PER_GROUP
    le = jnp.where((lane >= lo) & (lane < lo + EXPERTS_PER_GROUP), logits, neg)
    v1, i1 = first_argmax(le)
    v2, i2 = first_argmax(jnp.where(lane == i1, neg, le))
    t = jnp.exp(v2 - v1)
    gate1 = g_w / (1.0 + t)
    gate2 = g_w * t / (1.0 + t)
    e1 = i1 - N_GROUPS
    e2 = i2 - N_GROUPS

    oh1 = (lane == e1).astype(F32)
    oh2 = (lane == e2).astype(F32)
    oh = oh1 + oh2
    befores = []
    for r0 in range(0, m, r):
        oh_r = oh[r0:r0 + r, :]
        befores.append(_dot(strict_lower, oh_r.astype(BF16)) + run)
        run = run + jnp.sum(oh_r, axis=0, keepdims=True)
    before = befores[0] if len(befores) == 1 else jnp.concatenate(befores, axis=0)
    rank1 = jnp.sum(before * oh1, axis=-1, keepdims=True)
    rank2 = jnp.sum(before * oh2, axis=-1, keepdims=True)
    new_run = run

    rinfo = jnp.where(lane == 0, e1,
            jnp.where(lane == 1, e2,
            jnp.where(lane == 2, gate1,
            jnp.where(lane == 3, gate2,
            jnp.where(lane == 4, rank1,
            jnp.where(lane == 5, rank2, 0.0))))))
    return jnp.transpose(rinfo)[0:SUBLANES, :], new_run


def _conv_segments(a, w_ref, seg_ref, tail_ref, yseg_ref, conv_ref):
    sub = lax.broadcasted_iota(jnp.int32, (SUBLANES, LANES), 0)
    for lt in range(D_CONV // LANES):
        ls = slice(lt * LANES, (lt + 1) * LANES)
        for t0 in range(0, TM, SUBLANES):
            s, m = divmod(t0, SEG)
            seg_ref[lt, pl.ds((SEG_HALO + m) * SUBLANES + s, SUBLANES, stride=SUBLANES), :] = a[t0:t0 + SUBLANES, ls]
        for j in range(SEG_HALO):
            cur = seg_ref[lt, (SEG + j) * SUBLANES:(SEG + j + 1) * SUBLANES, :]
            prev = tail_ref[lt, j * SUBLANES:(j + 1) * SUBLANES, :]
            seg_ref[lt, j * SUBLANES:(j + 1) * SUBLANES, :] = jnp.where(
                sub == 0, pltpu.roll(prev, 1, axis=0), pltpu.roll(cur, 1, axis=0))
            tail_ref[lt, j * SUBLANES:(j + 1) * SUBLANES, :] = cur
        for m0 in range(0, SEG, CONV_BLOCK):
            acc = [jnp.zeros((SUBLANES, LANES), F32) for _ in range(CONV_BLOCK)]
            for idx in range(m0 - HIST, m0 + CONV_BLOCK):
                b = seg_ref[lt, (SEG_HALO + idx) * SUBLANES:(SEG_HALO + idx + 1) * SUBLANES, :]
                for m in range(max(m0, idx), min(m0 + CONV_BLOCK, idx + CONV_WIDTH)):
                    k = idx - m + HIST
                    acc[m - m0] = acc[m - m0] + b * w_ref[k:k + 1, ls]
            for m in range(m0, m0 + CONV_BLOCK):
                yseg_ref[lt, m * SUBLANES:(m + 1) * SUBLANES, :] = acc[m - m0]
        for t0 in range(0, TM, SUBLANES):
            s, m = divmod(t0, SEG)
            conv_ref[t0:t0 + SUBLANES, ls] = yseg_ref[lt, pl.ds(m * SUBLANES + s, SUBLANES, stride=SUBLANES), :]


def _stage_cast(pairs, stage_ref, sem):
    work = [(src, dst, r0) for src, dst in pairs for r0 in range(0, src.shape[0], STAGE_ROWS)]

    def copy(k):
        src, _, r0 = work[k]
        return pltpu.make_async_copy(src.at[pl.ds(r0, STAGE_ROWS)],
                                     stage_ref.at[k % 2, :, pl.ds(0, src.shape[1])], sem.at[k % 2])

    copy(0).start()
    for k, (src, dst, r0) in enumerate(work):
        if k + 1 < len(work):
            copy(k + 1).start()
        copy(k).wait()

        def body(c, carry, k=k, src=src, dst=dst, r0=r0):
            c0 = pl.multiple_of(c * CAST_ROWS, CAST_ROWS)
            dst[pl.ds(r0 + c0, CAST_ROWS), :] = stage_ref[k % 2, pl.ds(c0, CAST_ROWS), 0:src.shape[1]].astype(BF16)
            return carry

        lax.fori_loop(0, STAGE_ROWS // CAST_ROWS, body, 0)


def _trunk_prompt_kernel(x_ref, gmix_ref, win32_ref, convw_ref, convb_ref, lncg_ref, lncb_ref, lnvg_ref, lnvb_ref,
                         wsg_ref, bsg_ref, wout32_ref, gx_ref, wxq32_ref, kmem_ref, v_ref, wxo32_ref, gffn_ref, wr_ref,
                         br_ref,
                         x2_ref, h3_ref, logit_ref, hist_ref,
                         seg_ref, tail_ref, yseg_ref, conv_ref, win_ref, wout_ref, wxq_ref, wxo_ref, stage_ref, stage_sem):
    i = pl.program_id(0)

    @pl.when(i == 0)
    def _():
        tail_ref[...] = jnp.zeros(tail_ref.shape, F32)
        _stage_cast([(win32_ref, win_ref), (wout32_ref, wout_ref), (wxq32_ref, wxq_ref), (wxo32_ref, wxo_ref)],
                    stage_ref, stage_sem)

    x = x_ref[...]
    h = _rms(x, gmix_ref[...]).astype(BF16)

    a_in = _dot(h, win_ref[:, 0:D_CONV])
    a_gate = _dot(h, win_ref[:, D_CONV:2 * D_CONV])
    a = a_in * _sigmoid(a_gate)
    hist_ref[...] = a[TM - HALO:, :]
    _conv_segments(a, convw_ref, seg_ref, tail_ref, yseg_ref, conv_ref)

    y = _ln(conv_ref[...] + convb_ref[...], lncg_ref[...], lncb_ref[...])
    a_out = (y * _sigmoid(y)).astype(BF16)

    u = _dot(h, win_ref[:, 2 * D_CONV:2 * D_CONV + D_SG])
    v = _ln(_dot(h, win_ref[:, 2 * D_CONV + D_SG:]), lnvg_ref[...], lnvb_ref[...]).astype(BF16)
    ri = lax.broadcasted_iota(jnp.int32, (SG_CHUNK, SG_CHUNK), 0)
    ci = lax.broadcasted_iota(jnp.int32, (SG_CHUNK, SG_CHUNK), 1)
    w_tril = [jnp.where(ci <= ri, wsg_ref[hh], 0.0).astype(BF16) for hh in range(SG_HEADS)]
    gate_rows = []
    for c in range(TM // SG_CHUNK):
        rs = slice(c * SG_CHUNK, (c + 1) * SG_CHUNK)
        heads = [_dot(w_tril[hh], v[rs, hh * SG_HEAD_DIM:(hh + 1) * SG_HEAD_DIM]) for hh in range(SG_HEADS)]
        gate_rows.append(jnp.concatenate(heads, axis=1) + bsg_ref[...])
    b_out = (u * jnp.concatenate(gate_rows, axis=0)).astype(BF16)

    x1 = x + _dot(a_out, wout_ref[0:D_CONV, :]) + _dot(b_out, wout_ref[D_CONV:, :])

    hx = _rms(x1, gx_ref[...]).astype(BF16)
    q = _dot(hx, wxq_ref[...]).astype(BF16)
    x2 = x1 + _dot(_attn_heads(q, kmem_ref[...], v_ref[...], False), wxo_ref[...])
    x2_ref[...] = x2

    h3 = _rms(x2, gffn_ref[...]).astype(BF16)
    h3_ref[...] = _pack_bf16_pairs(h3.astype(F32))
    logit_ref[...] = _dot(h3, wr_ref[...]) + br_ref[...]


def _router_kernel(logit_ref, upper_ref, ones_ref, rt_ref, cnt_ref, run_ref):
    @pl.when(pl.program_id(0) == 0)
    def _():
        run_ref[...] = jnp.zeros(run_ref.shape, F32)

    n = ROUTE_ROWS
    lt = jnp.transpose(logit_ref[...])
    neg = jnp.float32(-jnp.inf)
    big = jnp.float32(LOGIT_LANES)

    def first_argmax(vals, rows):
        mx = jnp.max(vals, axis=0, keepdims=True)
        idx = jnp.min(jnp.where(vals == mx, rows, big), axis=0, keepdims=True)
        return mx, idx

    row8 = lax.broadcasted_iota(jnp.int32, (SUBLANES, n), 0).astype(F32)
    lg = jnp.where(row8 < N_GROUPS, lt[0:SUBLANES, :], neg)
    g_max, g_idx = first_argmax(lg, row8)
    g_w = 1.0 / jnp.sum(jnp.exp(lg - g_max), axis=0, keepdims=True)

    n_rows = N_GROUPS + N_EXPERTS + (-(N_GROUPS + N_EXPERTS)) % SUBLANES
    rows = lax.broadcasted_iota(jnp.int32, (n_rows, n), 0).astype(F32)
    lo = N_GROUPS + g_idx * EXPERTS_PER_GROUP
    le = jnp.where((rows >= lo) & (rows < lo + EXPERTS_PER_GROUP), lt[0:n_rows, :], neg)
    v1, i1 = first_argmax(le, rows)
    v2, i2 = first_argmax(jnp.where(rows == i1, neg, le), rows)
    t = jnp.exp(v2 - v1)
    gate1 = g_w / (1.0 + t)
    gate2 = g_w * t / (1.0 + t)
    e1 = i1 - N_GROUPS
    e2 = i2 - N_GROUPS

    erow = lax.broadcasted_iota(jnp.int32, (LOGIT_LANES, n), 0).astype(F32)
    oh1 = (erow == e1).astype(F32)
    oh2 = (erow == e2).astype(F32)
    oh = (oh1 + oh2).astype(BF16)
    r = upper_ref.shape[0]
    run = run_ref[...]
    rank1, rank2 = [], []
    for c0 in range(0, n, r):
        cs = slice(c0, c0 + r)
        before = _dot(oh[:, cs], upper_ref[...]) + run
        rank1.append(jnp.sum(before * oh1[:, cs], axis=0, keepdims=True))
        rank2.append(jnp.sum(before * oh2[:, cs], axis=0, keepdims=True))
        run = run + _dot(oh[:, cs], ones_ref[...])
    run_ref[...] = run
    cnt_ref[...] = run[:, 0:LANES]

    sub = lax.broadcasted_iota(jnp.int32, (SUBLANES, n), 0)
    vals = (e1, e2, gate1, gate2, jnp.concatenate(rank1, axis=1), jnp.concatenate(rank2, axis=1))
    rt = jnp.zeros((SUBLANES, n), F32)
    for k, v in enumerate(vals):
        rt = jnp.where(sub == k, v, rt)
    rt_ref[...] = rt


def _router(logits, rank_block):
    n = logits.shape[0]
    assert n % ROUTE_ROWS == 0 and ROUTE_ROWS % rank_block == 0
    upper = jnp.triu(jnp.ones((rank_block, rank_block), BF16), 1)
    ones = jnp.ones((rank_block, rank_block), BF16)
    return pl.pallas_call(
        _router_kernel,
        grid=(n // ROUTE_ROWS,),
        in_specs=[pl.BlockSpec((ROUTE_ROWS, LOGIT_LANES), lambda i: (i, 0)),
                  pl.BlockSpec(upper.shape, lambda i: (0, 0)),
                  pl.BlockSpec(ones.shape, lambda i: (0, 0))],
        out_specs=(pl.BlockSpec((SUBLANES, ROUTE_ROWS), lambda i: (0, i)),
                   pl.BlockSpec((LOGIT_LANES, LANES), lambda i: (0, 0))),
        out_shape=(jax.ShapeDtypeStruct((SUBLANES, n), F32), jax.ShapeDtypeStruct((LOGIT_LANES, LANES), F32)),
        scratch_shapes=[pltpu.VMEM((LOGIT_LANES, rank_block), F32)],
        compiler_params=pltpu.CompilerParams(dimension_semantics=("arbitrary",), vmem_limit_bytes=VMEM_LIMIT),
        name="router",
    )(logits, upper, ones)


def _const_spec(shape):
    nd = len(shape)
    return pl.BlockSpec(shape, lambda i: (0,) * nd, pipeline_mode=pl.Buffered(1))


def _trunk_prompt(x, p):
    n = x.shape[0]
    assert n % TM == 0
    row = lambda w: pl.BlockSpec((TM, w), lambda i: (i, 0))
    consts = [p["g_mix"], p["w_in"], p["conv_w"], p["conv_b"], p["ln_conv_g"], p["ln_conv_b"], p["ln_v_g"],
              p["ln_v_b"], p["w_sg"], p["b_sg_rows"], p["w_out"], p["g_xattn"], p["w_xq"], p["k"], p["v"],
              p["w_xo"], p["g_ffn"], p["w_router"], p["b_router"]]
    staged = (p["w_in"], p["w_out"], p["w_xq"], p["w_xo"])
    spec = lambda c: pl.BlockSpec(memory_space=pl.ANY) if any(c is s for s in staged) else _const_spec(c.shape)
    return pl.pallas_call(
        _trunk_prompt_kernel,
        grid=(n // TM,),
        in_specs=[row(D_MODEL)] + [spec(c) for c in consts],
        out_specs=(row(D_MODEL), row(D_MODEL // 2), row(LOGIT_LANES),
                   pl.BlockSpec((HALO, D_CONV), lambda i: (0, 0))),
        out_shape=(jax.ShapeDtypeStruct((n, D_MODEL), F32),
                   jax.ShapeDtypeStruct((n, D_MODEL // 2), jnp.uint32),
                   jax.ShapeDtypeStruct((n, LOGIT_LANES), F32),
                   jax.ShapeDtypeStruct((HALO, D_CONV), F32)),
        scratch_shapes=[pltpu.VMEM((D_CONV // LANES, (SEG_HALO + SEG) * SUBLANES, LANES), F32),
                        pltpu.VMEM((D_CONV // LANES, SEG_HALO * SUBLANES, LANES), F32),
                        pltpu.VMEM((D_CONV // LANES, TM, LANES), F32),
                        pltpu.VMEM((TM, D_CONV), F32),
                        pltpu.VMEM(p["w_in"].shape, BF16), pltpu.VMEM(p["w_out"].shape, BF16),
                        pltpu.VMEM(p["w_xq"].shape, BF16), pltpu.VMEM(p["w_xo"].shape, BF16),
                        pltpu.VMEM((2, STAGE_ROWS, max(s.shape[1] for s in staged)), F32),
                        pltpu.SemaphoreType.DMA((2,))],
        compiler_params=pltpu.CompilerParams(dimension_semantics=("arbitrary",), vmem_limit_bytes=VMEM_LIMIT),
        name="trunk_prompt",
    )(x, *consts)


def _trunk_sample_kernel(n_batch, t_len,
                         x_ref, hist_in_ref, run_in_ref, gmix_ref, win_ref, convw_ref, convb_ref, lncg_ref, lncb_ref,
                         lnvg_ref, lnvb_ref, wsgbd_ref, bsg_ref, wout_ref, gx_ref, wxq_ref, kmem_ref, v_ref, wxo_ref,
                         gffn_ref, wr_ref, br_ref, lower_ref,
                         x2_ref, h3_ref, rt_ref, hist_ref, sgv_ref, cnt_ref,
                         ext_ref, conv_ref, att_ref):
    x = x_ref[...]
    h = _rms(x, gmix_ref[...]).astype(BF16)
    z = _dot(h, win_ref[...].astype(BF16))
    a = z[:, 0:D_CONV] * _sigmoid(z[:, D_CONV:2 * D_CONV])
    ext_len = HIST + t_len
    for b in range(n_batch):
        ext_ref[b, 0:HIST, :] = hist_in_ref[b]
        ext_ref[b, HIST:ext_len, :] = a[b * t_len:(b + 1) * t_len, :]
    for b in range(n_batch):
        acc = jnp.zeros((t_len, D_CONV), F32)
        for k in range(CONV_WIDTH):
            acc = acc + ext_ref[b, k:k + t_len, :] * convw_ref[k:k + 1, :]
        conv_ref[b * t_len:(b + 1) * t_len, :] = acc
        hist_ref[b] = ext_ref[b, ext_len - HIST:ext_len, :]

    y = _ln(conv_ref[...] + convb_ref[...], lncg_ref[...], lncb_ref[...])
    a_out = (y * _sigmoid(y)).astype(BF16)

    u = z[:, 2 * D_CONV:2 * D_CONV + D_SG]
    v = _ln(z[:, 2 * D_CONV + D_SG:], lnvg_ref[...], lnvb_ref[...])
    sgv_ref[...] = v
    vb = v.astype(BF16)
    heads = [_dot(wsgbd_ref[hh], vb[:, hh * SG_HEAD_DIM:(hh + 1) * SG_HEAD_DIM]) for hh in range(SG_HEADS)]
    b_out = (u * (jnp.concatenate(heads, axis=1) + bsg_ref[...])).astype(BF16)

    x1 = (x + _dot(a_out, wout_ref[0:D_CONV, :].astype(BF16))
          + _dot(b_out, wout_ref[D_CONV:, :].astype(BF16)))

    hx = _rms(x1, gx_ref[...]).astype(BF16)
    q = _dot(hx, wxq_ref[...].astype(BF16)).astype(BF16)
    for b in range(n_batch):
        rs = slice(b * t_len, (b + 1) * t_len)
        att_ref[rs, :] = _attn_heads(q[rs, :], kmem_ref[b], v_ref[b], True)
    x2 = x1 + _dot(att_ref[...], wxo_ref[...].astype(BF16))
    x2_ref[...] = x2

    h3 = _rms(x2, gffn_ref[...]).astype(BF16)
    m = n_batch * t_len
    h3_ref[0:m, :] = _pack_bf16_pairs(h3.astype(F32))
    if h3_ref.shape[0] > m:
        h3_ref[m:, :] = jnp.zeros((h3_ref.shape[0] - m, D_MODEL // 2), jnp.uint32)
    rt, new_run = _route(_dot(h3, wr_ref[...]) + br_ref[...], run_in_ref[...], lower_ref[...])
    rt_ref[...] = rt
    cnt_ref[...] = new_run


def _trunk_sample(x, hist, run, p, n_batch, t_len):
    m = n_batch * t_len
    args = [x, hist, run, p["g_mix"], p["w_in"], p["conv_w"], p["conv_b"], p["ln_conv_g"], p["ln_conv_b"],
            p["ln_v_g"], p["ln_v_b"], p["w_sg_bd"], p["b_sg_rows_s"], p["w_out"], p["g_xattn"], p["w_xq"],
            p["k_s"], p["v_s"], p["w_xo"], p["g_ffn"], p["w_router"], p["b_router"], p["lower"]]
    return pl.pallas_call(
        functools.partial(_trunk_sample_kernel, n_batch, t_len),
        out_shape=(jax.ShapeDtypeStruct((m, D_MODEL), F32),
                   jax.ShapeDtypeStruct((-(-m // (SC_WORKERS * SUBLANES)) * SC_WORKERS * SUBLANES, D_MODEL // 2),
                                        jnp.uint32),
                   jax.ShapeDtypeStruct((SUBLANES, m), F32),
                   jax.ShapeDtypeStruct((n_batch, HIST, D_CONV), F32),
                   jax.ShapeDtypeStruct((m, D_SG), F32),
                   jax.ShapeDtypeStruct((1, LOGIT_LANES), F32)),
        scratch_shapes=[pltpu.VMEM((n_batch, HIST + t_len, D_CONV), F32),
                        pltpu.VMEM((m, D_CONV), F32),
                        pltpu.VMEM((m, D_MODEL), BF16)],
        compiler_params=pltpu.CompilerParams(vmem_limit_bytes=VMEM_LIMIT),
        name="trunk_sample",
    )(*args)


def _sc_worker_id():
    return lax.axis_index("s") * SC_CORES + lax.axis_index("c")


def _sc_chunk(per_w, max_chunk):
    assert per_w % SUBLANES == 0 and max_chunk <= LANES
    return max(c for c in range(SUBLANES, max_chunk + 1, SUBLANES) if per_w % c == 0)


def _sc_scatter_rows2(tables, slots_a, slots_b, tag_bases, n_rows_out, max_chunk):
    d, dtype = tables[0].shape[1], tables[0].dtype
    plans = []
    for t in tables:
        per_w = t.shape[0] // SC_WORKERS
        assert per_w * SC_WORKERS == t.shape[0]
        chunk = _sc_chunk(per_w, max_chunk)
        plans.append((per_w, chunk, per_w // chunk))
    cmax = max(c for _, c, _ in plans)
    n_t = len(tables)
    mesh = plsc.VectorSubcoreMesh(core_axis_name="c", subcore_axis_name="s")

    nb = SCATTER_BUFS
    lag = 2
    scratch = []
    for _, chunk, _ in plans:
        for _ in range(nb):
            scratch += [pltpu.VMEM((chunk,), jnp.int32), pltpu.VMEM((chunk,), jnp.int32)]
    scratch += [pltpu.VMEM((cmax, d), dtype)] * nb
    scratch += [pltpu.VMEM((cmax, TAG_WORDS), jnp.int32)] * (2 * nb)
    scratch += [pltpu.SemaphoreType.DMA] * (2 * nb)

    @functools.partial(pl.kernel, mesh=mesh,
                       out_type=(jax.ShapeDtypeStruct((n_rows_out, d), dtype),
                                 jax.ShapeDtypeStruct((n_rows_out, TAG_WORDS), jnp.int32)),
                       scratch_types=scratch)
    def scatter(*refs):
        tab_hbm = refs[0:n_t]
        sa_hbm = refs[n_t:2 * n_t]
        sb_hbm = refs[2 * n_t:3 * n_t]
        out_hbm, tag_hbm = refs[3 * n_t], refs[3 * n_t + 1]
        sc = refs[3 * n_t + 2:]
        idx_refs = sc[:2 * nb * n_t]
        rows = sc[2 * nb * n_t:2 * nb * n_t + nb]
        tagbufs = sc[2 * nb * n_t + nb:2 * nb * n_t + 3 * nb]
        lsem = sc[2 * nb * n_t + 3 * nb:2 * nb * n_t + 4 * nb]
        ssem = sc[2 * nb * n_t + 4 * nb:]
        wid = _sc_worker_id()

        work = []
        for t, (per_w, chunk, n_chunks) in enumerate(plans):
            for j in range(n_chunks):
                work.append((t, wid * per_w + j * chunk, chunk))

        def parts(k):
            t, off, chunk = work[k]
            b = k % nb
            ia, ib = idx_refs[2 * nb * t + 2 * b], idx_refs[2 * nb * t + 2 * b + 1]
            full = chunk == cmax
            rv = rows[b] if full else rows[b].at[pl.ds(0, chunk)]
            ta = tagbufs[2 * b] if full else tagbufs[2 * b].at[pl.ds(0, chunk)]
            tb = tagbufs[2 * b + 1] if full else tagbufs[2 * b + 1].at[pl.ds(0, chunk)]
            return t, off, chunk, b, ia, ib, rv, ta, tb

        def start_load(k):
            t, off, chunk, b, ia, ib, rv, ta, tb = parts(k)
            return (pltpu.async_copy(tab_hbm[t].at[pl.ds(off, chunk)], rv, lsem[b]),
                    pltpu.async_copy(sa_hbm[t].at[pl.ds(off, chunk)], ia, lsem[b]),
                    pltpu.async_copy(sb_hbm[t].at[pl.ds(off, chunk)], ib, lsem[b]))

        def start_scatter(k):
            t, off, chunk, b, ia, ib, rv, ta, tb = parts(k)
            base_a, base_b = tag_bases[t]
            for r in range(chunk):
                row_id = (off + r).astype(jnp.int32)
                tagbufs[2 * b][r, pl.ds(0, SC_LANES)] = jnp.zeros((SC_LANES,), jnp.int32) + (base_a + row_id)
                tagbufs[2 * b + 1][r, pl.ds(0, SC_LANES)] = jnp.zeros((SC_LANES,), jnp.int32) + (base_b + row_id)
            return (pltpu.async_copy(rv, out_hbm.at[ia], ssem[b]), pltpu.async_copy(rv, out_hbm.at[ib], ssem[b]),
                    pltpu.async_copy(ta, tag_hbm.at[ia], ssem[b]), pltpu.async_copy(tb, tag_hbm.at[ib], ssem[b]))

        loads, scatters = {}, {}
        for k in range(len(work) + lag):
            if k < len(work):
                if k >= nb:
                    for c in scatters.pop(k - nb):
                        c.wait()
                loads[k] = start_load(k)
            w = k - lag
            if w >= 0:
                for c in loads.pop(w):
                    c.wait()
                scatters[w] = start_scatter(w)
        for w in sorted(scatters):
            for c in scatters[w]:
                c.wait()

    return scatter(*tables, *slots_a, *slots_b)


def _sc_scatter_back(ys, dest, n_rows_out):
    n_rows, d = ys.shape
    per_w = n_rows // SC_WORKERS
    assert per_w * SC_WORKERS == n_rows
    chunk = _sc_chunk(per_w, BACK_CHUNK)
    n_chunks = per_w // chunk
    nb = BACK_BUFS
    lag = 1
    mesh = plsc.VectorSubcoreMesh(core_axis_name="c", subcore_axis_name="s")

    @functools.partial(
        pl.kernel, mesh=mesh,
        out_type=jax.ShapeDtypeStruct((n_rows_out, d), ys.dtype),
        scratch_types=([pltpu.VMEM((chunk,), jnp.int32)] * nb + [pltpu.VMEM((chunk, d), ys.dtype)] * nb
                       + [pltpu.SemaphoreType.DMA] * (2 * nb)),
    )
    def scatter_back(ys_hbm, dest_hbm, out_hbm, *rest):
        idx = rest[:nb]
        rows = rest[nb:2 * nb]
        lsem = rest[2 * nb:3 * nb]
        ssem = rest[3 * nb:]
        base = _sc_worker_id() * per_w

        loads, scatters = {}, {}
        for k in range(n_chunks + lag):
            if k < n_chunks:
                b = k % nb
                if k >= nb:
                    scatters.pop(k - nb).wait()
                off = base + k * chunk
                loads[k] = (pltpu.async_copy(ys_hbm.at[pl.ds(off, chunk)], rows[b], lsem[b]),
                            pltpu.async_copy(dest_hbm.at[pl.ds(off, chunk)], idx[b], lsem[b]))
            w = k - lag
            if w >= 0:
                b = w % nb
                for c in loads.pop(w):
                    c.wait()
                scatters[w] = pltpu.async_copy(rows[b], out_hbm.at[idx[b]], ssem[b])
        for w in sorted(scatters):
            scatters[w].wait()

    return scatter_back(ys, dest)


def _experts_kernel(dump_base, first_ref, nblk_ref, cnt_ref, tot_ref, xs_hbm, tag_hbm, wg_ref, wu_ref, wd_ref,
                    ys_hbm, dest_hbm, xbuf, tbuf, ybuf, dbuf, wg_bf, wu_bf, wd_bf, in_sem, tin_sem, out_sem, dout_sem):
    e = pl.program_id(0)
    nb = nblk_ref[e]
    first = first_ref[e]
    cnt = cnt_ref[e]
    total = tot_ref[0]
    half = D_MODEL // 2

    def in_copies(gb):
        slot = lax.rem(gb, X_BUFS)
        return (pltpu.make_async_copy(xs_hbm.at[pl.ds(gb * BM, BM)], xbuf.at[slot], in_sem.at[slot]),
                pltpu.make_async_copy(tag_hbm.at[pl.ds(gb * BM, BM)], tbuf.at[slot], tin_sem.at[slot]))

    def out_copies(gb):
        slot = lax.rem(gb, Y_BUFS)
        return (pltpu.make_async_copy(ybuf.at[slot], ys_hbm.at[pl.ds(gb * BM, BM)], out_sem.at[slot]),
                pltpu.make_async_copy(dbuf.at[slot], dest_hbm.at[pl.ds(gb * SUBLANES, SUBLANES)], dout_sem.at[slot]))

    def start_in(gb):
        for c in in_copies(gb):
            c.start(priority=ROW_DMA_PRIORITY)

    def start_out(gb):
        for c in out_copies(gb):
            c.start(priority=ROW_DMA_PRIORITY)

    def wait_out(gb):
        for c in out_copies(gb):
            c.wait()

    @pl.when(nb > 0)
    def _():
        @pl.when(first == 0)
        def _():
            for k in range(X_LOOKAHEAD):
                @pl.when(k < total)
                def _():
                    start_in(k)

        wg_bf[...] = wg_ref[0].astype(BF16)
        wu_bf[...] = wu_ref[0].astype(BF16)
        wd_bf[...] = wd_ref[0].astype(BF16)

        def acquire(gb):
            @pl.when(gb + X_LOOKAHEAD < total)
            def _():
                start_in(gb + X_LOOKAHEAD)

            for c in in_copies(gb):
                c.wait()

            @pl.when(gb >= Y_BUFS)
            def _():
                wait_out(gb - Y_BUFS)

        def ffn(gb, j):
            n_live = cnt - j * BM
            tags_t = jnp.transpose(tbuf[lax.rem(gb, X_BUFS)].astype(F32))
            lane = lax.broadcasted_iota(jnp.int32, (SUBLANES, BM), 1)
            own = (dump_base + gb * BM + lane).astype(F32)
            dest = jnp.where(lane < n_live, jnp.broadcast_to(tags_t[0:1, :], (SUBLANES, BM)), own)
            dbuf[lax.rem(gb, Y_BUFS)] = dest.astype(jnp.int32)
            live = lax.broadcasted_iota(jnp.int32, (BM, half), 0) < n_live
            lo, hi = _unpack_bf16_pairs(jnp.where(live, xbuf[lax.rem(gb, X_BUFS)], jnp.uint32(0)))
            g = _dot(lo, wg_bf[0:half, :]) + _dot(hi, wg_bf[half:, :])
            u = _dot(lo, wu_bf[0:half, :]) + _dot(hi, wu_bf[half:, :])
            hm = (g * _sigmoid(g) * u).astype(BF16)
            y = _dot(hm, wd_bf[...])
            ybuf[lax.rem(gb, Y_BUFS)] = _pack_bf16_pairs(y.astype(BF16).astype(F32))

        def block_pair(jp, carry):
            j0 = 2 * jp
            g0 = first + j0
            acquire(g0)
            acquire(g0 + 1)
            ffn(g0, j0)
            ffn(g0 + 1, j0 + 1)
            start_out(g0)
            start_out(g0 + 1)
            return carry

        lax.fori_loop(0, nb // 2, block_pair, 0)

        @pl.when(lax.rem(nb, 2) == 1)
        def _():
            gl = first + nb - 1
            acquire(gl)
            ffn(gl, nb - 1)
            start_out(gl)

        @pl.when(first + nb == total)
        def _():
            for k in range(Y_BUFS):
                @pl.when(total - 1 - k >= 0)
                def _():
                    wait_out(total - 1 - k)


def _experts(xs, tags, n_rows_out, dump_base, first_block, n_blocks_e, counts, w_eg, w_eu, w_ed):
    w_map = lambda e, fb, nb, ct, tot: (e, 0, 0)
    half = D_MODEL // 2
    total = jnp.sum(n_blocks_e).astype(jnp.int32).reshape(1)
    n_blocks = n_rows_out // BM
    return pl.pallas_call(
        functools.partial(_experts_kernel, dump_base),
        grid_spec=pltpu.PrefetchScalarGridSpec(
            num_scalar_prefetch=4,
            grid=(N_EXPERTS,),
            in_specs=[pl.BlockSpec(memory_space=pl.ANY),
                      pl.BlockSpec(memory_space=pl.ANY),
                      pl.BlockSpec((1, D_MODEL, D_EXPERT), w_map),
                      pl.BlockSpec((1, D_MODEL, D_EXPERT), w_map),
                      pl.BlockSpec((1, D_EXPERT, D_MODEL), w_map)],
            out_specs=(pl.BlockSpec(memory_space=pl.ANY), pl.BlockSpec(memory_space=pl.ANY)),
            scratch_shapes=[pltpu.VMEM((X_BUFS, BM, half), jnp.uint32), pltpu.VMEM((X_BUFS, BM, TAG_WORDS), jnp.int32),
                            pltpu.VMEM((Y_BUFS, BM, half), jnp.uint32), pltpu.VMEM((Y_BUFS, SUBLANES, BM), jnp.int32),
                            pltpu.VMEM((D_MODEL, D_EXPERT), BF16), pltpu.VMEM((D_MODEL, D_EXPERT), BF16),
                            pltpu.VMEM((D_EXPERT, D_MODEL), BF16),
                            pltpu.SemaphoreType.DMA((X_BUFS,)), pltpu.SemaphoreType.DMA((X_BUFS,)),
                            pltpu.SemaphoreType.DMA((Y_BUFS,)), pltpu.SemaphoreType.DMA((Y_BUFS,))]),
        out_shape=(jax.ShapeDtypeStruct((n_rows_out, half), jnp.uint32),
                   jax.ShapeDtypeStruct((n_blocks * SUBLANES, BM), jnp.int32)),
        compiler_params=pltpu.CompilerParams(dimension_semantics=("arbitrary",), vmem_limit_bytes=VMEM_LIMIT),
        name="experts",
    )(first_block, n_blocks_e, counts, total, xs, tags, w_eg, w_eu, w_ed)


def _combine_kernel(x2_ref, y1_ref, y2_ref, rt_ref, g_ref, o_ref):
    rt = rt_ref[...]
    r = jnp.transpose(jnp.concatenate([rt, jnp.zeros((LANES - rt.shape[0], rt.shape[1]), F32)], axis=0))
    g1, g2 = r[:, 2:3], r[:, 3:4]
    half = D_MODEL // 2
    y1_lo, y1_hi = _unpack_bf16_pairs_f32(y1_ref[...])
    y2_lo, y2_hi = _unpack_bf16_pairs_f32(y2_ref[...])
    x_lo = x2_ref[:, 0:half] + g1 * y1_lo + g2 * y2_lo
    x_hi = x2_ref[:, half:] + g1 * y1_hi + g2 * y2_hi
    ms = (jnp.sum(x_lo * x_lo, axis=-1, keepdims=True) + jnp.sum(x_hi * x_hi, axis=-1, keepdims=True)) / D_MODEL
    inv = lax.rsqrt(ms + EPS)
    o_ref[:, 0:half] = x_lo * inv * g_ref[:, 0:half]
    o_ref[:, half:] = x_hi * inv * g_ref[:, half:]


def _combine(x2, yg, rt, g_final, tm, blk1, blk2):
    n = x2.shape[0]
    in_specs = [pl.BlockSpec((tm, D_MODEL), lambda i: (i, 0)),
                pl.BlockSpec((tm, D_MODEL // 2), lambda i: (blk1 + i, 0)),
                pl.BlockSpec((tm, D_MODEL // 2), lambda i: (blk2 + i, 0)),
                pl.BlockSpec((SUBLANES, tm), lambda i: (0, i)),
                pl.BlockSpec((1, D_MODEL), lambda i: (0, 0))]
    out_spec = pl.BlockSpec((tm, D_MODEL), lambda i: (i, 0))
    out_shape = jax.ShapeDtypeStruct((n, D_MODEL), F32)
    params = pltpu.CompilerParams(vmem_limit_bytes=VMEM_LIMIT)
    if n // tm <= 2:
        return pl.pallas_call(_combine_kernel, grid=(n // tm,), in_specs=in_specs, out_specs=out_spec,
                              out_shape=out_shape, compiler_params=params, name="combine")(x2, yg, yg, rt, g_final)

    deep = [pl.BlockSpec(s.block_shape, s.index_map, pipeline_mode=pl.Buffered(3)) for s in in_specs[:3]]

    def outer(x2_hbm, y1_hbm, y2_hbm, rt_hbm, g_hbm, o_hbm):
        pltpu.emit_pipeline(_combine_kernel, grid=(n // tm,), in_specs=deep + in_specs[3:],
                            out_specs=[out_spec])(x2_hbm, y1_hbm, y2_hbm, rt_hbm, g_hbm, o_hbm)

    return pl.pallas_call(
        outer,
        in_specs=[pl.BlockSpec(memory_space=pl.ANY)] * 5,
        out_specs=pl.BlockSpec(memory_space=pl.ANY),
        out_shape=out_shape, compiler_params=params, name="combine",
    )(x2, yg, yg, rt, g_final)


def _scatter_back(ys, dest, n_rows_out):
    return _sc_scatter_back(ys, dest, n_rows_out)


def _scatter_rows2(tables, slots_a, slots_b, tag_bases, n_rows_out):
    return _sc_scatter_rows2(tables, slots_a, slots_b, tag_bases, n_rows_out, SCATTER_CHUNK)


def kernel(x_prompt, x_sample, mem_prompt, state_conv, cache_mem_k, cache_mem_v, g_mix, w_in, conv_w, conv_b, ln_conv_g, ln_conv_b, ln_v_g, ln_v_b, w_sg, b_sg, w_out, g_mem, w_mk, w_mv, g_xattn, w_xq, w_xo, g_ffn, w_router_group, b_router_group, w_router_expert, b_router_expert, w_expert_gate, w_expert_up, w_expert_down, g_final):
    assert x_prompt.shape[0] == 1 and g_mix.shape[0] == 1
    n_p = x_prompt.shape[1]
    n_batch, t_len = x_sample.shape[0], x_sample.shape[1]
    n_s = n_batch * t_len
    row = lambda a: a.reshape(1, -1)

    w_router = jnp.concatenate(
        [w_router_group[0], jnp.transpose(w_router_expert[0], (1, 0, 2)).reshape(D_MODEL, N_EXPERTS)], axis=1)
    w_router = jnp.pad(w_router, ((0, 0), (0, LOGIT_LANES - w_router.shape[1]))).astype(BF16)
    b_router = jnp.pad(jnp.concatenate([b_router_group[0], b_router_expert[0].reshape(-1)]),
                       (0, LOGIT_LANES - N_GROUPS - N_EXPERTS)).reshape(1, LOGIT_LANES)
    tril_t = jnp.tril(jnp.ones((t_len, t_len), bool))
    w_sg_t = jnp.where(tril_t, w_sg[0][:, :t_len, :t_len], 0.0)
    eye_b = jnp.eye(n_batch, dtype=F32)
    w_sg_bd = jnp.einsum("ab,hij->haibj", eye_b, w_sg_t).reshape(SG_HEADS, n_s, n_s).astype(BF16)
    p = {
        "g_mix": row(g_mix[0]), "w_in": w_in[0],
        "conv_w": jnp.pad(conv_w[0], ((0, 1), (0, 0))), "conv_b": row(conv_b[0]),
        "ln_conv_g": row(ln_conv_g[0]), "ln_conv_b": row(ln_conv_b[0]),
        "ln_v_g": row(ln_v_g[0]), "ln_v_b": row(ln_v_b[0]),
        "w_sg": w_sg[0],
        "b_sg_rows": jnp.repeat(b_sg[0].T, SG_HEAD_DIM, axis=1),
        "w_sg_bd": w_sg_bd,
        "b_sg_rows_s": jnp.tile(jnp.repeat(b_sg[0][:, :t_len].T, SG_HEAD_DIM, axis=1), (n_batch, 1)),
        "w_out": w_out[0], "g_xattn": row(g_xattn[0]),
        "w_xq": w_xq[0], "w_xo": w_xo[0], "g_ffn": row(g_ffn[0]),
        "w_router": w_router, "b_router": b_router,
        "lower": jnp.tril(jnp.ones((n_s, n_s), BF16), -1),
    }

    k_p, v_p, p["k"], p["v"] = _memkv(mem_prompt[0], row(g_mem[0]), w_mk[0], w_mv[0])
    p["k_s"] = jnp.transpose(cache_mem_k[0].astype(BF16), (0, 2, 3, 1)).reshape(n_batch, D_MODEL, N_MEM)
    p["v_s"] = cache_mem_v[0].astype(BF16).reshape(n_batch, N_MEM, D_MODEL)

    assert n_p % n_s == 0
    x2_p, h3_p, logits_p, hist_p = _trunk_prompt(x_prompt[0], p)
    rt_p, cnt_t = _router(logits_p, ROUTE_RANK_ROWS)
    cnt_p = cnt_t[:, 0].reshape(1, LOGIT_LANES)
    x2_s, h3_s, rt_s, hist_s, sgv_s, cnt = _trunk_sample(
        x_sample.reshape(n_s, D_MODEL), state_conv[0], cnt_p, p, n_batch, t_len)

    experts = jnp.arange(N_EXPERTS, dtype=jnp.int32)
    w_e = (w_expert_gate[0], w_expert_up[0], w_expert_down[0])

    def moe_pass(cnt, h3_tables, rts, n_real):
        n_tot = sum(n_real)
        n_slots = -(-(n_tot * 2) // BM) * BM + N_EXPERTS * BM
        counts = cnt[0, :N_EXPERTS].astype(jnp.int32)
        padded = (counts + BM - 1) // BM * BM
        pad_start = jnp.cumsum(padded) - padded

        def one(e_row, rank_row):
            e = e_row.astype(jnp.int32)
            start = jnp.sum(jnp.where(e[None, :] == experts[:, None], pad_start[:, None], 0), axis=0)
            return start + rank_row.astype(jnp.int32)

        slots = [(one(rt[0], rt[4]), one(rt[1], rt[5])) for rt in rts]
        sa, sb, tag_bases, spare0, dest0 = [], [], [], n_slots, 0
        for tab, (a, b), n in zip(h3_tables, slots, n_real):
            n_spare = tab.shape[0] - n
            spare = spare0 + jnp.arange(n_spare, dtype=jnp.int32)
            sa.append(jnp.concatenate([a, spare]))
            sb.append(jnp.concatenate([b, spare + n_spare]))
            spare0 += 2 * n_spare
            tag_bases.append((dest0, dest0 + n))
            dest0 += 2 * n
        xs, tags = _scatter_rows2(tuple(h3_tables), tuple(sa), tuple(sb), tuple(tag_bases), spare0)
        ys, dest_blocks = _experts(xs, tags, n_slots, dest0, pad_start // BM, padded // BM, counts, *w_e)
        slot = jnp.arange(n_slots, dtype=jnp.int32)
        dest = dest_blocks.reshape(n_slots // BM, SUBLANES, BM)[:, 0, :].reshape(-1)
        dest = jnp.where(slot < jnp.sum(padded), dest, dest0 + slot)
        return _scatter_back(ys, dest, dest0 + n_slots)

    yg = moe_pass(cnt, [h3_p, h3_s], [rt_p, rt_s], [n_p, n_s])

    gf = row(g_final)
    y_p = _combine(x2_p, yg, rt_p, gf, TM_COMBINE, 0, n_p // TM_COMBINE)
    y_s = _combine(x2_s, yg, rt_s, gf, n_s, 2 * n_p // n_s, 2 * n_p // n_s + 1)

    return (y_p.reshape(1, n_p, D_MODEL),
            y_s.reshape(n_batch, t_len, D_MODEL),
            hist_p[HALO - HIST:].reshape(1, 1, HIST, D_CONV),
            hist_s.reshape(1, n_batch, HIST, D_CONV),
            k_p.reshape(1, 1, N_MEM, X_HEADS, X_HEAD_DIM),
            v_p.reshape(1, 1, N_MEM, X_HEADS, X_HEAD_DIM),
            sgv_s.reshape(1, n_batch, t_len, D_SG))
```

```python
import functools

import jax
import jax.numpy as jnp
from jax import lax
from jax.experimental import pallas as pl
from jax.experimental.pallas import tpu as pltpu
from jax.experimental.pallas import tpu_sc as plsc

D_MODEL = 1024
D_CONV = 512
D_SG = 512
CONV_WIDTH = 31
HIST = CONV_WIDTH - 1
SG_HEADS = 4
SG_HEAD_DIM = 128
SG_CHUNK = 128
N_MEM = 256
X_HEADS = 4
X_HEAD_DIM = 256
N_GROUPS = 4
EXPERTS_PER_GROUP = 8
N_EXPERTS = 32
D_EXPERT = 512
EPS = 1e-6

LANES = 128
SUBLANES = 8
SC_CORES = 2
SC_SUBCORES = 16
SC_WORKERS = SC_CORES * SC_SUBCORES
SC_LANES = 16
VMEM_LIMIT = 56 * 1024 * 1024

TM = 1024
TM_COMBINE = 2048
ROUTE_ROWS = 4096
ROUTE_RANK_ROWS = 512
HALO = 32
SEG = TM // SUBLANES
SEG_HALO = 32
CONV_BLOCK = 32
CAST_ROWS = 64
STAGE_ROWS = 256
BM = 256
X_LOOKAHEAD = 4
X_BUFS = X_LOOKAHEAD + 2
Y_BUFS = 4
ROW_DMA_PRIORITY = 1
SCATTER_CHUNK = 32
SCATTER_BUFS = 5
TAG_WORDS = 128
BACK_CHUNK = 56
BACK_BUFS = 4
LOGIT_LANES = 128

F32 = jnp.float32
BF16 = jnp.bfloat16


def _dot(a, b):
    return jnp.dot(a, b, preferred_element_type=F32)


def _rms(x, g):
    return x * lax.rsqrt(jnp.mean(x * x, axis=-1, keepdims=True) + EPS) * g


def _ln(x, g, b):
    mu = jnp.mean(x, axis=-1, keepdims=True)
    xc = x - mu
    var = jnp.mean(xc * xc, axis=-1, keepdims=True)
    return xc * lax.rsqrt(var + EPS) * g + b


def _sigmoid(x):
    return 1.0 / (1.0 + jnp.exp(-x))


def _pack_bf16_pairs(h):
    bits = lax.bitcast_convert_type(h, jnp.uint32)
    half = h.shape[1] // 2
    lo = lax.shift_right_logical(bits[:, :half], jnp.uint32(16))
    hi = bits[:, half:] & jnp.uint32(0xFFFF0000)
    return hi | lo


def _unpack_bf16_pairs_f32(p):
    lo = lax.bitcast_convert_type(lax.shift_left(p, jnp.uint32(16)), F32)
    hi = lax.bitcast_convert_type(p & jnp.uint32(0xFFFF0000), F32)
    return lo, hi


def _unpack_bf16_pairs(p):
    lo, hi = _unpack_bf16_pairs_f32(p)
    return lo.astype(BF16), hi.astype(BF16)


def _memkv_kernel(mem_ref, g_ref, wk_ref, wv_ref, k_ref, v_ref, kbf_ref, vbf_ref):
    m = _rms(mem_ref[...], g_ref[...]).astype(BF16)
    k = _dot(m, wk_ref[...].astype(BF16))
    v = _dot(m, wv_ref[...].astype(BF16))
    k_ref[...] = k
    v_ref[...] = v
    kbf_ref[...] = k.astype(BF16)
    vbf_ref[...] = v.astype(BF16)


def _memkv(mem, g_mem, w_mk, w_mv):
    return pl.pallas_call(
        _memkv_kernel,
        out_shape=(jax.ShapeDtypeStruct((N_MEM, D_MODEL), F32), jax.ShapeDtypeStruct((N_MEM, D_MODEL), F32),
                   jax.ShapeDtypeStruct((N_MEM, D_MODEL), BF16), jax.ShapeDtypeStruct((N_MEM, D_MODEL), BF16)),
        compiler_params=pltpu.CompilerParams(vmem_limit_bytes=VMEM_LIMIT),
        name="memkv",
    )(mem, g_mem, w_mk, w_mv)


def _attn_heads(q, k, v, k_transposed):
    outs = []
    for h in range(X_HEADS):
        sl = slice(h * X_HEAD_DIM, (h + 1) * X_HEAD_DIM)
        if k_transposed:
            s = _dot(q[:, sl], k[sl, :])
        else:
            s = lax.dot_general(q[:, sl], k[:, sl], (((1,), (1,)), ((), ())), preferred_element_type=F32)
        s = s * (X_HEAD_DIM ** -0.5)
        s = s - jnp.max(s, axis=-1, keepdims=True)
        p = jnp.exp(s)
        p = p / jnp.sum(p, axis=-1, keepdims=True)
        outs.append(_dot(p.astype(BF16), v[:, sl]).astype(BF16))
    return jnp.concatenate(outs, axis=1)


def _route(logits, run, strict_lower):
    m = logits.shape[0]
    r = strict_lower.shape[0]
    lane = lax.broadcasted_iota(jnp.int32, (m, LOGIT_LANES), 1).astype(F32)
    neg = jnp.float32(-jnp.inf)
    big = jnp.float32(LOGIT_LANES)

    def first_argmax(vals):
        mx = jnp.max(vals, axis=-1, keepdims=True)
        idx = jnp.min(jnp.where(vals == mx, lane, big), axis=-1, keepdims=True)
        return mx, idx

    lg = jnp.where(lane < N_GROUPS, logits, neg)
    g_max, g_idx = first_argmax(lg)
    g_w = 1.0 / jnp.sum(jnp.exp(lg - g_max), axis=-1, keepdims=True)

    lo = N_GROUPS + g_idx * EXPERTS_PER_GROUP
    le = jnp.where((lane >= lo) & (lane < lo + EXPERTS_PER_GROUP), logits, neg)
    v1, i1 = first_argmax(le)
    v2, i2 = first_argmax(jnp.where(lane == i1, neg, le))
    t = jnp.exp(v2 - v1)
    gate1 = g_w / (1.0 + t)
    gate2 = g_w * t / (1.0 + t)
    e1 = i1 - N_GROUPS
    e2 = i2 - N_GROUPS

    oh1 = (lane == e1).astype(F32)
    oh2 = (lane == e2).astype(F32)
    oh = oh1 + oh2
    befores = []
    for r0 in range(0, m, r):
        oh_r = oh[r0:r0 + r, :]
        befores.append(_dot(strict_lower, oh_r.astype(BF16)) + run)
        run = run + jnp.sum(oh_r, axis=0, keepdims=True)
    before = befores[0] if len(befores) == 1 else jnp.concatenate(befores, axis=0)
    rank1 = jnp.sum(before * oh1, axis=-1, keepdims=True)
    rank2 = jnp.sum(before * oh2, axis=-1, keepdims=True)
    new_run = run

    rinfo = jnp.where(lane == 0, e1,
            jnp.where(lane == 1, e2,
            jnp.where(lane == 2, gate1,
            jnp.where(lane == 3, gate2,
            jnp.where(lane == 4, rank1,
            jnp.where(lane == 5, rank2, 0.0))))))
    return jnp.transpose(rinfo)[0:SUBLANES, :], new_run


def _conv_segments(a, w_ref, seg_ref, tail_ref, yseg_ref, conv_ref):
    sub = lax.broadcasted_iota(jnp.int32, (SUBLANES, LANES), 0)
    for lt in range(D_CONV // LANES):
        ls = slice(lt * LANES, (lt + 1) * LANES)
        for t0 in range(0, TM, SUBLANES):
            s, m = divmod(t0, SEG)
            seg_ref[lt, pl.ds((SEG_HALO + m) * SUBLANES + s, SUBLANES, stride=SUBLANES), :] = a[t0:t0 + SUBLANES, ls]
        for j in range(SEG_HALO):
            cur = seg_ref[lt, (SEG + j) * SUBLANES:(SEG + j + 1) * SUBLANES, :]
            prev = tail_ref[lt, j * SUBLANES:(j + 1) * SUBLANES, :]
            seg_ref[lt, j * SUBLANES:(j + 1) * SUBLANES, :] = jnp.where(
                sub == 0, pltpu.roll(prev, 1, axis=0), pltpu.roll(cur, 1, axis=0))
            tail_ref[lt, j * SUBLANES:(j + 1) * SUBLANES, :] = cur
        for m0 in range(0, SEG, CONV_BLOCK):
            acc = [jnp.zeros((SUBLANES, LANES), F32) for _ in range(CONV_BLOCK)]
            for idx in range(m0 - HIST, m0 + CONV_BLOCK):
                b = seg_ref[lt, (SEG_HALO + idx) * SUBLANES:(SEG_HALO + idx + 1) * SUBLANES, :]
                for m in range(max(m0, idx), min(m0 + CONV_BLOCK, idx + CONV_WIDTH)):
                    k = idx - m + HIST
                    acc[m - m0] = acc[m - m0] + b * w_ref[k:k + 1, ls]
            for m in range(m0, m0 + CONV_BLOCK):
                yseg_ref[lt, m * SUBLANES:(m + 1) * SUBLANES, :] = acc[m - m0]
        for t0 in range(0, TM, SUBLANES):
            s, m = divmod(t0, SEG)
            conv_ref[t0:t0 + SUBLANES, ls] = yseg_ref[lt, pl.ds(m * SUBLANES + s, SUBLANES, stride=SUBLANES), :]


def _stage_cast(pairs, stage_ref, sem):
    work = [(src, dst, r0) for src, dst in pairs for r0 in range(0, src.shape[0], STAGE_ROWS)]

    def copy(k):
        src, _, r0 = work[k]
        return pltpu.make_async_copy(src.at[pl.ds(r0, STAGE_ROWS)],
                                     stage_ref.at[k % 2, :, pl.ds(0, src.shape[1])], sem.at[k % 2])

    copy(0).start()
    for k, (src, dst, r0) in enumerate(work):
        if k + 1 < len(work):
            copy(k + 1).start()
        copy(k).wait()

        def body(c, carry, k=k, src=src, dst=dst, r0=r0):
            c0 = pl.multiple_of(c * CAST_ROWS, CAST_ROWS)
            dst[pl.ds(r0 + c0, CAST_ROWS), :] = stage_ref[k % 2, pl.ds(c0, CAST_ROWS), 0:src.shape[1]].astype(BF16)
            return carry

        lax.fori_loop(0, STAGE_ROWS // CAST_ROWS, body, 0)


def _trunk_prompt_kernel(x_ref, gmix_ref, win32_ref, convw_ref, convb_ref, lncg_ref, lncb_ref, lnvg_ref, lnvb_ref,
                         wsg_ref, bsg_ref, wout32_ref, gx_ref, wxq32_ref, kmem_ref, v_ref, wxo32_ref, gffn_ref, wr_ref,
                         br_ref,
                         x2_ref, h3_ref, logit_ref, hist_ref,
                         seg_ref, tail_ref, yseg_ref, conv_ref, win_ref, wout_ref, wxq_ref, wxo_ref, stage_ref, stage_sem):
    i = pl.program_id(0)

    @pl.when(i == 0)
    def _():
        tail_ref[...] = jnp.zeros(tail_ref.shape, F32)
        _stage_cast([(win32_ref, win_ref), (wout32_ref, wout_ref), (wxq32_ref, wxq_ref), (wxo32_ref, wxo_ref)],
                    stage_ref, stage_sem)

    x = x_ref[...]
    h = _rms(x, gmix_ref[...]).astype(BF16)

    a_in = _dot(h, win_ref[:, 0:D_CONV])
    a_gate = _dot(h, win_ref[:, D_CONV:2 * D_CONV])
    a = a_in * _sigmoid(a_gate)
    hist_ref[...] = a[TM - HALO:, :]
    _conv_segments(a, convw_ref, seg_ref, tail_ref, yseg_ref, conv_ref)

    y = _ln(conv_ref[...] + convb_ref[...], lncg_ref[...], lncb_ref[...])
    a_out = (y * _sigmoid(y)).astype(BF16)

    u = _dot(h, win_ref[:, 2 * D_CONV:2 * D_CONV + D_SG])
    v = _ln(_dot(h, win_ref[:, 2 * D_CONV + D_SG:]), lnvg_ref[...], lnvb_ref[...]).astype(BF16)
    ri = lax.broadcasted_iota(jnp.int32, (SG_CHUNK, SG_CHUNK), 0)
    ci = lax.broadcasted_iota(jnp.int32, (SG_CHUNK, SG_CHUNK), 1)
    w_tril = [jnp.where(ci <= ri, wsg_ref[hh], 0.0).astype(BF16) for hh in range(SG_HEADS)]
    gate_rows = []
    for c in range(TM // SG_CHUNK):
        rs = slice(c * SG_CHUNK, (c + 1) * SG_CHUNK)
        heads = [_dot(w_tril[hh], v[rs, hh * SG_HEAD_DIM:(hh + 1) * SG_HEAD_DIM]) for hh in range(SG_HEADS)]
        gate_rows.append(jnp.concatenate(heads, axis=1) + bsg_ref[...])
    b_out = (u * jnp.concatenate(gate_rows, axis=0)).astype(BF16)

    x1 = x + _dot(a_out, wout_ref[0:D_CONV, :]) + _dot(b_out, wout_ref[D_CONV:, :])

    hx = _rms(x1, gx_ref[...]).astype(BF16)
    q = _dot(hx, wxq_ref[...]).astype(BF16)
    x2 = x1 + _dot(_attn_heads(q, kmem_ref[...], v_ref[...], False), wxo_ref[...])
    x2_ref[...] = x2

    h3 = _rms(x2, gffn_ref[...]).astype(BF16)
    h3_ref[...] = _pack_bf16_pairs(h3.astype(F32))
    logit_ref[...] = _dot(h3, wr_ref[...]) + br_ref[...]


def _router_kernel(logit_ref, upper_ref, ones_ref, rt_ref, cnt_ref, run_ref):
    @pl.when(pl.program_id(0) == 0)
    def _():
        run_ref[...] = jnp.zeros(run_ref.shape, F32)

    n = ROUTE_ROWS
    lt = jnp.transpose(logit_ref[...])
    neg = jnp.float32(-jnp.inf)
    big = jnp.float32(LOGIT_LANES)

    def first_argmax(vals, rows):
        mx = jnp.max(vals, axis=0, keepdims=True)
        idx = jnp.min(jnp.where(vals == mx, rows, big), axis=0, keepdims=True)
        return mx, idx

    row8 = lax.broadcasted_iota(jnp.int32, (SUBLANES, n), 0).astype(F32)
    lg = jnp.where(row8 < N_GROUPS, lt[0:SUBLANES, :], neg)
    g_max, g_idx = first_argmax(lg, row8)
    g_w = 1.0 / jnp.sum(jnp.exp(lg - g_max), axis=0, keepdims=True)

    n_rows = N_GROUPS + N_EXPERTS + (-(N_GROUPS + N_EXPERTS)) % SUBLANES
    rows = lax.broadcasted_iota(jnp.int32, (n_rows, n), 0).astype(F32)
    lo = N_GROUPS + g_idx * EXPERTS_PER_GROUP
    le = jnp.where((rows >= lo) & (rows < lo + EXPERTS_PER_GROUP), lt[0:n_rows, :], neg)
    v1, i1 = first_argmax(le, rows)
    v2, i2 = first_argmax(jnp.where(rows == i1, neg, le), rows)
    t = jnp.exp(v2 - v1)
    gate1 = g_w / (1.0 + t)
    gate2 = g_w * t / (1.0 + t)
    e1 = i1 - N_GROUPS
    e2 = i2 - N_GROUPS

    erow = lax.broadcasted_iota(jnp.int32, (LOGIT_LANES, n), 0).astype(F32)
    oh1 = (erow == e1).astype(F32)
    oh2 = (erow == e2).astype(F32)
    oh = (oh1 + oh2).astype(BF16)
    r = upper_ref.shape[0]
    run = run_ref[...]
    rank1, rank2 = [], []
    for c0 in range(0, n, r):
        cs = slice(c0, c0 + r)
        before = _dot(oh[:, cs], upper_ref[...]) + run
        rank1.append(jnp.sum(before * oh1[:, cs], axis=0, keepdims=True))
        rank2.append(jnp.sum(before * oh2[:, cs], axis=0, keepdims=True))
        run = run + _dot(oh[:, cs], ones_ref[...])
    run_ref[...] = run
    cnt_ref[...] = run[:, 0:LANES]

    sub = lax.broadcasted_iota(jnp.int32, (SUBLANES, n), 0)
    vals = (e1, e2, gate1, gate2, jnp.concatenate(rank1, axis=1), jnp.concatenate(rank2, axis=1))
    rt = jnp.zeros((SUBLANES, n), F32)
    for k, v in enumerate(vals):
        rt = jnp.where(sub == k, v, rt)
    rt_ref[...] = rt


def _router(logits, rank_block):
    n = logits.shape[0]
    assert n % ROUTE_ROWS == 0 and ROUTE_ROWS % rank_block == 0
    upper = jnp.triu(jnp.ones((rank_block, rank_block), BF16), 1)
    ones = jnp.ones((rank_block, rank_block), BF16)
    return pl.pallas_call(
        _router_kernel,
        grid=(n // ROUTE_ROWS,),
        in_specs=[pl.BlockSpec((ROUTE_ROWS, LOGIT_LANES), lambda i: (i, 0)),
                  pl.BlockSpec(upper.shape, lambda i: (0, 0)),
                  pl.BlockSpec(ones.shape, lambda i: (0, 0))],
        out_specs=(pl.BlockSpec((SUBLANES, ROUTE_ROWS), lambda i: (0, i)),
                   pl.BlockSpec((LOGIT_LANES, LANES), lambda i: (0, 0))),
        out_shape=(jax.ShapeDtypeStruct((SUBLANES, n), F32), jax.ShapeDtypeStruct((LOGIT_LANES, LANES), F32)),
        scratch_shapes=[pltpu.VMEM((LOGIT_LANES, rank_block), F32)],
        compiler_params=pltpu.CompilerParams(dimension_semantics=("arbitrary",), vmem_limit_bytes=VMEM_LIMIT),
        name="router",
    )(logits, upper, ones)


def _const_spec(shape):
    nd = len(shape)
    return pl.BlockSpec(shape, lambda i: (0,) * nd, pipeline_mode=pl.Buffered(1))


def _trunk_prompt(x, p):
    n = x.shape[0]
    assert n % TM == 0
    row = lambda w: pl.BlockSpec((TM, w), lambda i: (i, 0))
    consts = [p["g_mix"], p["w_in"], p["conv_w"], p["conv_b"], p["ln_conv_g"], p["ln_conv_b"], p["ln_v_g"],
              p["ln_v_b"], p["w_sg"], p["b_sg_rows"], p["w_out"], p["g_xattn"], p["w_xq"], p["k"], p["v"],
              p["w_xo"], p["g_ffn"], p["w_router"], p["b_router"]]
    staged = (p["w_in"], p["w_out"], p["w_xq"], p["w_xo"])
    spec = lambda c: pl.BlockSpec(memory_space=pl.ANY) if any(c is s for s in staged) else _const_spec(c.shape)
    return pl.pallas_call(
        _trunk_prompt_kernel,
        grid=(n // TM,),
        in_specs=[row(D_MODEL)] + [spec(c) for c in consts],
        out_specs=(row(D_MODEL), row(D_MODEL // 2), row(LOGIT_LANES),
                   pl.BlockSpec((HALO, D_CONV), lambda i: (0, 0))),
        out_shape=(jax.ShapeDtypeStruct((n, D_MODEL), F32),
                   jax.ShapeDtypeStruct((n, D_MODEL // 2), jnp.uint32),
                   jax.ShapeDtypeStruct((n, LOGIT_LANES), F32),
                   jax.ShapeDtypeStruct((HALO, D_CONV), F32)),
        scratch_shapes=[pltpu.VMEM((D_CONV // LANES, (SEG_HALO + SEG) * SUBLANES, LANES), F32),
                        pltpu.VMEM((D_CONV // LANES, SEG_HALO * SUBLANES, LANES), F32),
                        pltpu.VMEM((D_CONV // LANES, TM, LANES), F32),
                        pltpu.VMEM((TM, D_CONV), F32),
                        pltpu.VMEM(p["w_in"].shape, BF16), pltpu.VMEM(p["w_out"].shape, BF16),
                        pltpu.VMEM(p["w_xq"].shape, BF16), pltpu.VMEM(p["w_xo"].shape, BF16),
                        pltpu.VMEM((2, STAGE_ROWS, max(s.shape[1] for s in staged)), F32),
                        pltpu.SemaphoreType.DMA((2,))],
        compiler_params=pltpu.CompilerParams(dimension_semantics=("arbitrary",), vmem_limit_bytes=VMEM_LIMIT),
        name="trunk_prompt",
    )(x, *consts)


def _trunk_sample_kernel(n_batch, t_len,
                         x_ref, hist_in_ref, run_in_ref, gmix_ref, win_ref, convw_ref, convb_ref, lncg_ref, lncb_ref,
                         lnvg_ref, lnvb_ref, wsgbd_ref, bsg_ref, wout_ref, gx_ref, wxq_ref, kmem_ref, v_ref, wxo_ref,
                         gffn_ref, wr_ref, br_ref, lower_ref,
                         x2_ref, h3_ref, rt_ref, hist_ref, sgv_ref, cnt_ref,
                         ext_ref, conv_ref, att_ref):
    x = x_ref[...]
    h = _rms(x, gmix_ref[...]).astype(BF16)
    z = _dot(h, win_ref[...].astype(BF16))
    a = z[:, 0:D_CONV] * _sigmoid(z[:, D_CONV:2 * D_CONV])
    ext_len = HIST + t_len
    for b in range(n_batch):
        ext_ref[b, 0:HIST, :] = hist_in_ref[b]
        ext_ref[b, HIST:ext_len, :] = a[b * t_len:(b + 1) * t_len, :]
    for b in range(n_batch):
        acc = jnp.zeros((t_len, D_CONV), F32)
        for k in range(CONV_WIDTH):
            acc = acc + ext_ref[b, k:k + t_len, :] * convw_ref[k:k + 1, :]
        conv_ref[b * t_len:(b + 1) * t_len, :] = acc
        hist_ref[b] = ext_ref[b, ext_len - HIST:ext_len, :]

    y = _ln(conv_ref[...] + convb_ref[...], lncg_ref[...], lncb_ref[...])
    a_out = (y * _sigmoid(y)).astype(BF16)

    u = z[:, 2 * D_CONV:2 * D_CONV + D_SG]
    v = _ln(z[:, 2 * D_CONV + D_SG:], lnvg_ref[...], lnvb_ref[...])
    sgv_ref[...] = v
    vb = v.astype(BF16)
    heads = [_dot(wsgbd_ref[hh], vb[:, hh * SG_HEAD_DIM:(hh + 1) * SG_HEAD_DIM]) for hh in range(SG_HEADS)]
    b_out = (u * (jnp.concatenate(heads, axis=1) + bsg_ref[...])).astype(BF16)

    x1 = (x + _dot(a_out, wout_ref[0:D_CONV, :].astype(BF16))
          + _dot(b_out, wout_ref[D_CONV:, :].astype(BF16)))

    hx = _rms(x1, gx_ref[...]).astype(BF16)
    q = _dot(hx, wxq_ref[...].astype(BF16)).astype(BF16)
    for b in range(n_batch):
        rs = slice(b * t_len, (b + 1) * t_len)
        att_ref[rs, :] = _attn_heads(q[rs, :], kmem_ref[b], v_ref[b], True)
    x2 = x1 + _dot(att_ref[...], wxo_ref[...].astype(BF16))
    x2_ref[...] = x2

    h3 = _rms(x2, gffn_ref[...]).astype(BF16)
    m = n_batch * t_len
    h3_ref[0:m, :] = _pack_bf16_pairs(h3.astype(F32))
    if h3_ref.shape[0] > m:
        h3_ref[m:, :] = jnp.zeros((h3_ref.shape[0] - m, D_MODEL // 2), jnp.uint32)
    rt, new_run = _route(_dot(h3, wr_ref[...]) + br_ref[...], run_in_ref[...], lower_ref[...])
    rt_ref[...] = rt
    cnt_ref[...] = new_run


def _trunk_sample(x, hist, run, p, n_batch, t_len):
    m = n_batch * t_len
    args = [x, hist, run, p["g_mix"], p["w_in"], p["conv_w"], p["conv_b"], p["ln_conv_g"], p["ln_conv_b"],
            p["ln_v_g"], p["ln_v_b"], p["w_sg_bd"], p["b_sg_rows_s"], p["w_out"], p["g_xattn"], p["w_xq"],
            p["k_s"], p["v_s"], p["w_xo"], p["g_ffn"], p["w_router"], p["b_router"], p["lower"]]
    return pl.pallas_call(
        functools.partial(_trunk_sample_kernel, n_batch, t_len),
        out_shape=(jax.ShapeDtypeStruct((m, D_MODEL), F32),
                   jax.ShapeDtypeStruct((-(-m // (SC_WORKERS * SUBLANES)) * SC_WORKERS * SUBLANES, D_MODEL // 2),
                                        jnp.uint32),
                   jax.ShapeDtypeStruct((SUBLANES, m), F32),
                   jax.ShapeDtypeStruct((n_batch, HIST, D_CONV), F32),
                   jax.ShapeDtypeStruct((m, D_SG), F32),
                   jax.ShapeDtypeStruct((1, LOGIT_LANES), F32)),
        scratch_shapes=[pltpu.VMEM((n_batch, HIST + t_len, D_CONV), F32),
                        pltpu.VMEM((m, D_CONV), F32),
                        pltpu.VMEM((m, D_MODEL), BF16)],
        compiler_params=pltpu.CompilerParams(vmem_limit_bytes=VMEM_LIMIT),
        name="trunk_sample",
    )(*args)


def _sc_worker_id():
    return lax.axis_index("s") * SC_CORES + lax.axis_index("c")


def _sc_chunk(per_w, max_chunk):
    assert per_w % SUBLANES == 0 and max_chunk <= LANES
    return max(c for c in range(SUBLANES, max_chunk + 1, SUBLANES) if per_w % c == 0)


def _sc_scatter_rows2(tables, slots_a, slots_b, tag_bases, n_rows_out, max_chunk):
    d, dtype = tables[0].shape[1], tables[0].dtype
    plans = []
    for t in tables:
        per_w = t.shape[0] // SC_WORKERS
        assert per_w * SC_WORKERS == t.shape[0]
        chunk = _sc_chunk(per_w, max_chunk)
        plans.append((per_w, chunk, per_w // chunk))
    cmax = max(c for _, c, _ in plans)
    n_t = len(tables)
    mesh = plsc.VectorSubcoreMesh(core_axis_name="c", subcore_axis_name="s")

    nb = SCATTER_BUFS
    lag = 2
    scratch = []
    for _, chunk, _ in plans:
        for _ in range(nb):
            scratch += [pltpu.VMEM((chunk,), jnp.int32), pltpu.VMEM((chunk,), jnp.int32)]
    scratch += [pltpu.VMEM((cmax, d), dtype)] * nb
    scratch += [pltpu.VMEM((cmax, TAG_WORDS), jnp.int32)] * (2 * nb)
    scratch += [pltpu.SemaphoreType.DMA] * (2 * nb)

    @functools.partial(pl.kernel, mesh=mesh,
                       out_type=(jax.ShapeDtypeStruct((n_rows_out, d), dtype),
                                 jax.ShapeDtypeStruct((n_rows_out, TAG_WORDS), jnp.int32)),
                       scratch_types=scratch)
    def scatter(*refs):
        tab_hbm = refs[0:n_t]
        sa_hbm = refs[n_t:2 * n_t]
        sb_hbm = refs[2 * n_t:3 * n_t]
        out_hbm, tag_hbm = refs[3 * n_t], refs[3 * n_t + 1]
        sc = refs[3 * n_t + 2:]
        idx_refs = sc[:2 * nb * n_t]
        rows = sc[2 * nb * n_t:2 * nb * n_t + nb]
        tagbufs = sc[2 * nb * n_t + nb:2 * nb * n_t + 3 * nb]
        lsem = sc[2 * nb * n_t + 3 * nb:2 * nb * n_t + 4 * nb]
        ssem = sc[2 * nb * n_t + 4 * nb:]
        wid = _sc_worker_id()

        work = []
        for t, (per_w, chunk, n_chunks) in enumerate(plans):
            for j in range(n_chunks):
                work.append((t, wid * per_w + j * chunk, chunk))

        def parts(k):
            t, off, chunk = work[k]
            b = k % nb
            ia, ib = idx_refs[2 * nb * t + 2 * b], idx_refs[2 * nb * t + 2 * b + 1]
            full = chunk == cmax
            rv = rows[b] if full else rows[b].at[pl.ds(0, chunk)]
            ta = tagbufs[2 * b] if full else tagbufs[2 * b].at[pl.ds(0, chunk)]
            tb = tagbufs[2 * b + 1] if full else tagbufs[2 * b + 1].at[pl.ds(0, chunk)]
            return t, off, chunk, b, ia, ib, rv, ta, tb

        def start_load(k):
            t, off, chunk, b, ia, ib, rv, ta, tb = parts(k)
            return (pltpu.async_copy(tab_hbm[t].at[pl.ds(off, chunk)], rv, lsem[b]),
                    pltpu.async_copy(sa_hbm[t].at[pl.ds(off, chunk)], ia, lsem[b]),
                    pltpu.async_copy(sb_hbm[t].at[pl.ds(off, chunk)], ib, lsem[b]))

        def start_scatter(k):
            t, off, chunk, b, ia, ib, rv, ta, tb = parts(k)
            base_a, base_b = tag_bases[t]
            for r in range(chunk):
                row_id = (off + r).astype(jnp.int32)
                tagbufs[2 * b][r, pl.ds(0, SC_LANES)] = jnp.zeros((SC_LANES,), jnp.int32) + (base_a + row_id)
                tagbufs[2 * b + 1][r, pl.ds(0, SC_LANES)] = jnp.zeros((SC_LANES,), jnp.int32) + (base_b + row_id)
            return (pltpu.async_copy(rv, out_hbm.at[ia], ssem[b]), pltpu.async_copy(rv, out_hbm.at[ib], ssem[b]),
                    pltpu.async_copy(ta, tag_hbm.at[ia], ssem[b]), pltpu.async_copy(tb, tag_hbm.at[ib], ssem[b]))

        loads, scatters = {}, {}
        for k in range(len(work) + lag):
            if k < len(work):
                if k >= nb:
                    for c in scatters.pop(k - nb):
                        c.wait()
                loads[k] = start_load(k)
            w = k - lag
            if w >= 0:
                for c in loads.pop(w):
                    c.wait()
                scatters[w] = start_scatter(w)
        for w in sorted(scatters):
            for c in scatters[w]:
                c.wait()

    return scatter(*tables, *slots_a, *slots_b)


def _sc_scatter_back(ys, dest, n_rows_out):
    n_rows, d = ys.shape
    per_w = n_rows // SC_WORKERS
    assert per_w * SC_WORKERS == n_rows
    chunk = _sc_chunk(per_w, BACK_CHUNK)
    n_chunks = per_w // chunk
    nb = BACK_BUFS
    lag = 1
    mesh = plsc.VectorSubcoreMesh(core_axis_name="c", subcore_axis_name="s")

    @functools.partial(
        pl.kernel, mesh=mesh,
        out_type=jax.ShapeDtypeStruct((n_rows_out, d), ys.dtype),
        scratch_types=([pltpu.VMEM((chunk,), jnp.int32)] * nb + [pltpu.VMEM((chunk, d), ys.dtype)] * nb
                       + [pltpu.SemaphoreType.DMA] * (2 * nb)),
    )
    def scatter_back(ys_hbm, dest_hbm, out_hbm, *rest):
        idx = rest[:nb]
        rows = rest[nb:2 * nb]
        lsem = rest[2 * nb:3 * nb]
        ssem = rest[3 * nb:]
        base = _sc_worker_id() * per_w

        loads, scatters = {}, {}
        for k in range(n_chunks + lag):
            if k < n_chunks:
                b = k % nb
                if k >= nb:
                    scatters.pop(k - nb).wait()
                off = base + k * chunk
                loads[k] = (pltpu.async_copy(ys_hbm.at[pl.ds(off, chunk)], rows[b], lsem[b]),
                            pltpu.async_copy(dest_hbm.at[pl.ds(off, chunk)], idx[b], lsem[b]))
            w = k - lag
            if w >= 0:
                b = w % nb
                for c in loads.pop(w):
                    c.wait()
                scatters[w] = pltpu.async_copy(rows[b], out_hbm.at[idx[b]], ssem[b])
        for w in sorted(scatters):
            scatters[w].wait()

    return scatter_back(ys, dest)


def _experts_kernel(dump_base, first_ref, nblk_ref, cnt_ref, tot_ref, xs_hbm, tag_hbm, wg_ref, wu_ref, wd_ref,
                    ys_hbm, dest_hbm, xbuf, tbuf, ybuf, dbuf, wg_bf, wu_bf, wd_bf, in_sem, tin_sem, out_sem, dout_sem):
    e = pl.program_id(0)
    nb = nblk_ref[e]
    first = first_ref[e]
    cnt = cnt_ref[e]
    total = tot_ref[0]
    half = D_MODEL // 2

    def in_copies(gb):
        slot = lax.rem(gb, X_BUFS)
        return (pltpu.make_async_copy(xs_hbm.at[pl.ds(gb * BM, BM)], xbuf.at[slot], in_sem.at[slot]),
                pltpu.make_async_copy(tag_hbm.at[pl.ds(gb * BM, BM)], tbuf.at[slot], tin_sem.at[slot]))

    def out_copies(gb):
        slot = lax.rem(gb, Y_BUFS)
        return (pltpu.make_async_copy(ybuf.at[slot], ys_hbm.at[pl.ds(gb * BM, BM)], out_sem.at[slot]),
                pltpu.make_async_copy(dbuf.at[slot], dest_hbm.at[pl.ds(gb * SUBLANES, SUBLANES)], dout_sem.at[slot]))

    def start_in(gb):
        for c in in_copies(gb):
            c.start(priority=ROW_DMA_PRIORITY)

    def start_out(gb):
        for c in out_copies(gb):
            c.start()

    def wait_out(gb):
        for c in out_copies(gb):
            c.wait()

    @pl.when(nb > 0)
    def _():
        @pl.when(first == 0)
        def _():
            for k in range(X_LOOKAHEAD):
                @pl.when(k < total)
                def _():
                    start_in(k)

        wg_bf[...] = wg_ref[0].astype(BF16)
        wu_bf[...] = wu_ref[0].astype(BF16)
        wd_bf[...] = wd_ref[0].astype(BF16)

        def acquire(gb):
            @pl.when(gb + X_LOOKAHEAD < total)
            def _():
                start_in(gb + X_LOOKAHEAD)

            for c in in_copies(gb):
                c.wait()

            @pl.when(gb >= Y_BUFS)
            def _():
                wait_out(gb - Y_BUFS)

        def ffn(gb, j):
            n_live = cnt - j * BM
            tags_t = jnp.transpose(tbuf[lax.rem(gb, X_BUFS)].astype(F32))
            lane = lax.broadcasted_iota(jnp.int32, (SUBLANES, BM), 1)
            own = (dump_base + gb * BM + lane).astype(F32)
            dest = jnp.where(lane < n_live, jnp.broadcast_to(tags_t[0:1, :], (SUBLANES, BM)), own)
            dbuf[lax.rem(gb, Y_BUFS)] = dest.astype(jnp.int32)
            live = lax.broadcasted_iota(jnp.int32, (BM, half), 0) < n_live
            lo, hi = _unpack_bf16_pairs(jnp.where(live, xbuf[lax.rem(gb, X_BUFS)], jnp.uint32(0)))
            g = _dot(lo, wg_bf[0:half, :]) + _dot(hi, wg_bf[half:, :])
            u = _dot(lo, wu_bf[0:half, :]) + _dot(hi, wu_bf[half:, :])
            hm = (g * _sigmoid(g) * u).astype(BF16)
            y = _dot(hm, wd_bf[...])
            ybuf[lax.rem(gb, Y_BUFS)] = _pack_bf16_pairs(y.astype(BF16).astype(F32))

        def block_pair(jp, carry):
            j0 = 2 * jp
            g0 = first + j0
            acquire(g0)
            acquire(g0 + 1)
            ffn(g0, j0)
            ffn(g0 + 1, j0 + 1)
            start_out(g0)
            start_out(g0 + 1)
            return carry

        lax.fori_loop(0, nb // 2, block_pair, 0)

        @pl.when(lax.rem(nb, 2) == 1)
        def _():
            gl = first + nb - 1
            acquire(gl)
            ffn(gl, nb - 1)
            start_out(gl)

        @pl.when(first + nb == total)
        def _():
            for k in range(Y_BUFS):
                @pl.when(total - 1 - k >= 0)
                def _():
                    wait_out(total - 1 - k)


def _experts(xs, tags, n_rows_out, dump_base, first_block, n_blocks_e, counts, w_eg, w_eu, w_ed):
    w_map = lambda e, fb, nb, ct, tot: (e, 0, 0)
    half = D_MODEL // 2
    total = jnp.sum(n_blocks_e).astype(jnp.int32).reshape(1)
    n_blocks = n_rows_out // BM
    return pl.pallas_call(
        functools.partial(_experts_kernel, dump_base),
        grid_spec=pltpu.PrefetchScalarGridSpec(
            num_scalar_prefetch=4,
            grid=(N_EXPERTS,),
            in_specs=[pl.BlockSpec(memory_space=pl.ANY),
                      pl.BlockSpec(memory_space=pl.ANY),
                      pl.BlockSpec((1, D_MODEL, D_EXPERT), w_map),
                      pl.BlockSpec((1, D_MODEL, D_EXPERT), w_map),
                      pl.BlockSpec((1, D_EXPERT, D_MODEL), w_map)],
            out_specs=(pl.BlockSpec(memory_space=pl.ANY), pl.BlockSpec(memory_space=pl.ANY)),
            scratch_shapes=[pltpu.VMEM((X_BUFS, BM, half), jnp.uint32), pltpu.VMEM((X_BUFS, BM, TAG_WORDS), jnp.int32),
                            pltpu.VMEM((Y_BUFS, BM, half), jnp.uint32), pltpu.VMEM((Y_BUFS, SUBLANES, BM), jnp.int32),
                            pltpu.VMEM((D_MODEL, D_EXPERT), BF16), pltpu.VMEM((D_MODEL, D_EXPERT), BF16),
                            pltpu.VMEM((D_EXPERT, D_MODEL), BF16),
                            pltpu.SemaphoreType.DMA((X_BUFS,)), pltpu.SemaphoreType.DMA((X_BUFS,)),
                            pltpu.SemaphoreType.DMA((Y_BUFS,)), pltpu.SemaphoreType.DMA((Y_BUFS,))]),
        out_shape=(jax.ShapeDtypeStruct((n_rows_out, half), jnp.uint32),
                   jax.ShapeDtypeStruct((n_blocks * SUBLANES, BM), jnp.int32)),
        compiler_params=pltpu.CompilerParams(dimension_semantics=("arbitrary",), vmem_limit_bytes=VMEM_LIMIT),
        name="experts",
    )(first_block, n_blocks_e, counts, total, xs, tags, w_eg, w_eu, w_ed)


def _combine_kernel(x2_ref, y1_ref, y2_ref, rt_ref, g_ref, o_ref):
    rt = rt_ref[...]
    r = jnp.transpose(jnp.concatenate([rt, jnp.zeros((LANES - rt.shape[0], rt.shape[1]), F32)], axis=0))
    g1, g2 = r[:, 2:3], r[:, 3:4]
    half = D_MODEL // 2
    y1_lo, y1_hi = _unpack_bf16_pairs_f32(y1_ref[...])
    y2_lo, y2_hi = _unpack_bf16_pairs_f32(y2_ref[...])
    x_lo = x2_ref[:, 0:half] + g1 * y1_lo + g2 * y2_lo
    x_hi = x2_ref[:, half:] + g1 * y1_hi + g2 * y2_hi
    ms = (jnp.sum(x_lo * x_lo, axis=-1, keepdims=True) + jnp.sum(x_hi * x_hi, axis=-1, keepdims=True)) / D_MODEL
    inv = lax.rsqrt(ms + EPS)
    o_ref[:, 0:half] = x_lo * inv * g_ref[:, 0:half]
    o_ref[:, half:] = x_hi * inv * g_ref[:, half:]


def _combine(x2, yg, rt, g_final, tm, blk1, blk2):
    n = x2.shape[0]
    return pl.pallas_call(
        _combine_kernel,
        grid=(n // tm,),
        in_specs=[pl.BlockSpec((tm, D_MODEL), lambda i: (i, 0)),
                  pl.BlockSpec((tm, D_MODEL // 2), lambda i: (blk1 + i, 0)),
                  pl.BlockSpec((tm, D_MODEL // 2), lambda i: (blk2 + i, 0)),
                  pl.BlockSpec((SUBLANES, tm), lambda i: (0, i)),
                  pl.BlockSpec((1, D_MODEL), lambda i: (0, 0))],
        out_specs=pl.BlockSpec((tm, D_MODEL), lambda i: (i, 0)),
        out_shape=jax.ShapeDtypeStruct((n, D_MODEL), F32),
        compiler_params=pltpu.CompilerParams(dimension_semantics=("arbitrary",), vmem_limit_bytes=VMEM_LIMIT),
        name="combine",
    )(x2, yg, yg, rt, g_final)


def _scatter_back(ys, dest, n_rows_out):
    return _sc_scatter_back(ys, dest, n_rows_out)


def _scatter_rows2(tables, slots_a, slots_b, tag_bases, n_rows_out):
    return _sc_scatter_rows2(tables, slots_a, slots_b, tag_bases, n_rows_out, SCATTER_CHUNK)


def kernel(x_prompt, x_sample, mem_prompt, state_conv, cache_mem_k, cache_mem_v, g_mix, w_in, conv_w, conv_b, ln_conv_g, ln_conv_b, ln_v_g, ln_v_b, w_sg, b_sg, w_out, g_mem, w_mk, w_mv, g_xattn, w_xq, w_xo, g_ffn, w_router_group, b_router_group, w_router_expert, b_router_expert, w_expert_gate, w_expert_up, w_expert_down, g_final):
    assert x_prompt.shape[0] == 1 and g_mix.shape[0] == 1
    n_p = x_prompt.shape[1]
    n_batch, t_len = x_sample.shape[0], x_sample.shape[1]
    n_s = n_batch * t_len
    row = lambda a: a.reshape(1, -1)

    w_router = jnp.concatenate(
        [w_router_group[0], jnp.transpose(w_router_expert[0], (1, 0, 2)).reshape(D_MODEL, N_EXPERTS)], axis=1)
    w_router = jnp.pad(w_router, ((0, 0), (0, LOGIT_LANES - w_router.shape[1]))).astype(BF16)
    b_router = jnp.pad(jnp.concatenate([b_router_group[0], b_router_expert[0].reshape(-1)]),
                       (0, LOGIT_LANES - N_GROUPS - N_EXPERTS)).reshape(1, LOGIT_LANES)
    tril_t = jnp.tril(jnp.ones((t_len, t_len), bool))
    w_sg_t = jnp.where(tril_t, w_sg[0][:, :t_len, :t_len], 0.0)
    eye_b = jnp.eye(n_batch, dtype=F32)
    w_sg_bd = jnp.einsum("ab,hij->haibj", eye_b, w_sg_t).reshape(SG_HEADS, n_s, n_s).astype(BF16)
    p = {
        "g_mix": row(g_mix[0]), "w_in": w_in[0],
        "conv_w": jnp.pad(conv_w[0], ((0, 1), (0, 0))), "conv_b": row(conv_b[0]),
        "ln_conv_g": row(ln_conv_g[0]), "ln_conv_b": row(ln_conv_b[0]),
        "ln_v_g": row(ln_v_g[0]), "ln_v_b": row(ln_v_b[0]),
        "w_sg": w_sg[0],
        "b_sg_rows": jnp.repeat(b_sg[0].T, SG_HEAD_DIM, axis=1),
        "w_sg_bd": w_sg_bd,
        "b_sg_rows_s": jnp.tile(jnp.repeat(b_sg[0][:, :t_len].T, SG_HEAD_DIM, axis=1), (n_batch, 1)),
        "w_out": w_out[0], "g_xattn": row(g_xattn[0]),
        "w_xq": w_xq[0], "w_xo": w_xo[0], "g_ffn": row(g_ffn[0]),
        "w_router": w_router, "b_router": b_router,
        "lower": jnp.tril(jnp.ones((n_s, n_s), BF16), -1),
    }

    k_p, v_p, p["k"], p["v"] = _memkv(mem_prompt[0], row(g_mem[0]), w_mk[0], w_mv[0])
    p["k_s"] = jnp.transpose(cache_mem_k[0].astype(BF16), (0, 2, 3, 1)).reshape(n_batch, D_MODEL, N_MEM)
    p["v_s"] = cache_mem_v[0].astype(BF16).reshape(n_batch, N_MEM, D_MODEL)

    assert n_p % n_s == 0
    x2_p, h3_p, logits_p, hist_p = _trunk_prompt(x_prompt[0], p)
    rt_p, cnt_t = _router(logits_p, ROUTE_RANK_ROWS)
    cnt_p = cnt_t[:, 0].reshape(1, LOGIT_LANES)
    x2_s, h3_s, rt_s, hist_s, sgv_s, cnt = _trunk_sample(
        x_sample.reshape(n_s, D_MODEL), state_conv[0], cnt_p, p, n_batch, t_len)

    experts = jnp.arange(N_EXPERTS, dtype=jnp.int32)
    w_e = (w_expert_gate[0], w_expert_up[0], w_expert_down[0])

    def moe_pass(cnt, h3_tables, rts, n_real):
        n_tot = sum(n_real)
        n_slots = -(-(n_tot * 2) // BM) * BM + N_EXPERTS * BM
        counts = cnt[0, :N_EXPERTS].astype(jnp.int32)
        padded = (counts + BM - 1) // BM * BM
        pad_start = jnp.cumsum(padded) - padded

        def one(e_row, rank_row):
            e = e_row.astype(jnp.int32)
            start = jnp.sum(jnp.where(e[None, :] == experts[:, None], pad_start[:, None], 0), axis=0)
            return start + rank_row.astype(jnp.int32)

        slots = [(one(rt[0], rt[4]), one(rt[1], rt[5])) for rt in rts]
        sa, sb, tag_bases, spare0, dest0 = [], [], [], n_slots, 0
        for tab, (a, b), n in zip(h3_tables, slots, n_real):
            n_spare = tab.shape[0] - n
            spare = spare0 + jnp.arange(n_spare, dtype=jnp.int32)
            sa.append(jnp.concatenate([a, spare]))
            sb.append(jnp.concatenate([b, spare + n_spare]))
            spare0 += 2 * n_spare
            tag_bases.append((dest0, dest0 + n))
            dest0 += 2 * n
        xs, tags = _scatter_rows2(tuple(h3_tables), tuple(sa), tuple(sb), tuple(tag_bases), spare0)
        ys, dest_blocks = _experts(xs, tags, n_slots, dest0, pad_start // BM, padded // BM, counts, *w_e)
        slot = jnp.arange(n_slots, dtype=jnp.int32)
        dest = dest_blocks.reshape(n_slots // BM, SUBLANES, BM)[:, 0, :].reshape(-1)
        dest = jnp.where(slot < jnp.sum(padded), dest, dest0 + slot)
        return _scatter_back(ys, dest, dest0 + n_slots)

    yg = moe_pass(cnt, [h3_p, h3_s], [rt_p, rt_s], [n_p, n_s])

    gf = row(g_final)
    y_p = _combine(x2_p, yg, rt_p, gf, TM_COMBINE, 0, n_p // TM_COMBINE)
    y_s = _combine(x2_s, yg, rt_s, gf, n_s, 2 * n_p // n_s, 2 * n_p // n_s + 1)

    return (y_p.reshape(1, n_p, D_MODEL),
            y_s.reshape(n_batch, t_len, D_MODEL),
            hist_p[HALO - HIST:].reshape(1, 1, HIST, D_CONV),
            hist_s.reshape(1, n_batch, HIST, D_CONV),
            k_p.reshape(1, 1, N_MEM, X_HEADS, X_HEAD_DIM),
            v_p.reshape(1, 1, N_MEM, X_HEADS, X_HEAD_DIM),
            sgv_s.reshape(1, n_batch, t_len, D_SG))
```
